```python
import math
import jax
import jax.numpy as jnp
from jax import lax
import numpy as np

D_MODEL = 1024
BATCH = 8
SEQ = 4096
DEPTH = 2

QB = 128
NEG_INF = -1e30
BIG = 1e9
NORM_EPS = 1e-6

DIFF_HEADS = 4
DIFF_DH = 64
MLA_HEADS = 4
MLA_NOPE = 128
MLA_ROPE = 64
MLA_VDIM = 128
MLA_Q_LORA = 256
MLA_KV_LORA = 256
ROPE_THETA = 10000.0
FOX_HEADS = 4
FOX_DH = 128
NSA_HEADS = 8
NSA_GROUPS = 2
NSA_HPG = NSA_HEADS // NSA_GROUPS
NSA_DH = 64
CMP_STRIDE = 16
CMP_LEN = 2 * CMP_STRIDE
CMP_HIDDEN = 128
SLC_LEN = 64
SLC_TOPK = 8
WINDOW = 256
N_BRANCH = 4
BRANCH_WIDTH = 512
D_FF = 2816
CONV_WIDTH = 3

IN_WIDTHS = (
    DIFF_HEADS * 2 * DIFF_DH,
    DIFF_HEADS * 2 * DIFF_DH,
    DIFF_HEADS * 2 * DIFF_DH,
    MLA_Q_LORA,
    MLA_KV_LORA,
    MLA_ROPE,
    FOX_HEADS * FOX_DH,
    FOX_HEADS * FOX_DH,
    FOX_HEADS * FOX_DH,
    FOX_HEADS,
    NSA_HEADS * NSA_DH,
    3 * 2 * NSA_GROUPS * NSA_DH,
    3 * NSA_HEADS,
    N_BRANCH * D_MODEL,
)
N_IN = sum(IN_WIDTHS)

kernel_name = 'hybrid_gated_four_mixer_trunk'


def rms_norm(t, g):
    tf = t.astype(jnp.float32)
    y = tf * lax.rsqrt(jnp.mean(tf * tf, axis=-1, keepdims=True) + NORM_EPS)
    return (y * g.astype(jnp.float32)).astype(t.dtype)


def masked_softmax(s, mask):
    p = jax.nn.softmax(jnp.where(mask, s, NEG_INF), axis=-1)
    return jnp.where(mask, p, 0.0)


def alibi_slopes(n):
    return jnp.asarray(np.exp2(-8.0 * np.arange(1, n + 1) / n), dtype=jnp.float32)


def rope(t, pos):
    d = t.shape[-1]
    inv_freq = ROPE_THETA ** (-jnp.arange(0, d, 2, dtype=jnp.float32) / d)
    ang = pos.astype(jnp.float32)[:, None] * inv_freq[None, :]
    cos, sin = jnp.cos(ang), jnp.sin(ang)
    tf = t.astype(jnp.float32)
    t1, t2 = tf[..., : d // 2], tf[..., d // 2:]
    return jnp.concatenate([t1 * cos - t2 * sin, t2 * cos + t1 * sin], axis=-1).astype(t.dtype)


def to_blocks(a, seq_axis):
    seq_axis = seq_axis % a.ndim
    s = a.shape[seq_axis]
    a = a.reshape(a.shape[:seq_axis] + (s // QB, QB) + a.shape[seq_axis + 1:])
    return jnp.moveaxis(a, seq_axis, 0)


def from_blocks(o):
    o = jnp.moveaxis(o, 0, -3)
    return o.reshape(o.shape[:-3] + (o.shape[-3] * o.shape[-2], o.shape[-1]))


def merge_heads(t):
    t = jnp.moveaxis(t, -2, 1)
    return t.reshape(t.shape[0], t.shape[1], -1)


def diff_attention(q, k, v, lam, lam_init, subln_g):
    s, d = q.shape[-2], q.shape[-1]
    slopes = alibi_slopes(q.shape[1])[None, :, None, None, None]
    k_pos = jnp.arange(s)

    def block(args):
        i, qb = args
        q_pos = i * QB + jnp.arange(QB)
        dist = q_pos[:, None] - k_pos[None, :]
        sc = jnp.einsum('bhcqd,bhcsd->bhcqs', qb, k, preferred_element_type=jnp.float32) * d ** -0.5
        sc = sc - slopes * dist.astype(jnp.float32)
        p = masked_softmax(sc, dist >= 0)
        a = p[:, :, 0] - lam * p[:, :, 1]
        return jnp.einsum('bhqs,bhsd->bhqd', a.astype(v.dtype), v)

    o = from_blocks(lax.map(block, (jnp.arange(s // QB), to_blocks(q, -2))))
    return rms_norm(o, subln_g) * (1.0 - lam_init)


def mla_attention(c_q, c_kv, k_rope_in, norm_q, w_uq, norm_kv, w_ukv):
    b, s, _ = c_q.shape
    pos = jnp.arange(s)
    q = (rms_norm(c_q, norm_q) @ w_uq).reshape(b, s, MLA_HEADS, MLA_NOPE + MLA_ROPE).transpose(0, 2, 1, 3)
    q_nope, q_pe = q[..., :MLA_NOPE], rope(q[..., MLA_NOPE:], pos)
    kv = (rms_norm(c_kv, norm_kv) @ w_ukv).reshape(b, s, MLA_HEADS, MLA_NOPE + MLA_VDIM).transpose(0, 2, 1, 3)
    k_nope, v = kv[..., :MLA_NOPE], kv[..., MLA_NOPE:]
    k_pe = rope(k_rope_in, pos)
    scale = (MLA_NOPE + MLA_ROPE) ** -0.5

    def block(args):
        i, qn, qr = args
        q_pos = i * QB + jnp.arange(QB)
        mask = pos[None, :] <= q_pos[:, None]
        sc = (jnp.einsum('bhqd,bhsd->bhqs', qn, k_nope, preferred_element_type=jnp.float32)
              + jnp.einsum('bhqd,bsd->bhqs', qr, k_pe, preferred_element_type=jnp.float32)) * scale
        p = masked_softmax(sc, mask)
        return jnp.einsum('bhqs,bhsd->bhqd', p.astype(v.dtype), v)

    return from_blocks(lax.map(block, (jnp.arange(s // QB), to_blocks(q_nope, -2), to_blocks(q_pe, -2))))


def forgetting_attention(q, k, v, log_f):
    s, d = q.shape[-2], q.shape[-1]
    c = lax.cumsum(log_f, axis=2)
    k_pos = jnp.arange(s)

    def block(args):
        i, qb, cq = args
        q_pos = i * QB + jnp.arange(QB)
        mask = k_pos[None, :] <= q_pos[:, None]
        sc = jnp.einsum('bhqd,bhsd->bhqs', qb, k, preferred_element_type=jnp.float32) * d ** -0.5
        sc = sc + (cq[..., :, None] - c[..., None, :])
        p = masked_softmax(sc, mask)
        return jnp.einsum('bhqs,bhsd->bhqd', p.astype(v.dtype), v)

    return from_blocks(lax.map(block, (jnp.arange(s // QB), to_blocks(q, -2), to_blocks(c, -1))))


def compress(kx, pe, w1, w2):
    b, g, s, d = kx.shape
    ch = kx.reshape(b, g, s // CMP_STRIDE, CMP_STRIDE, d)
    blocks = jnp.concatenate([ch[:, :, :-1], ch[:, :, 1:]], axis=3) + pe
    return jax.nn.gelu(blocks.reshape(b, g, -1, CMP_LEN * d) @ w1) @ w2


def nsa_attention(q, gates, kv, cmp_pe, cmp_w1, cmp_w2):
    b, g, hg, s, d = q.shape
    scale = d ** -0.5
    slopes = alibi_slopes(NSA_HEADS).reshape(g, hg)[None, :, :, None, None]
    k_c = compress(kv[0, 0], cmp_pe[0], cmp_w1[0], cmp_w2[0])
    v_c = compress(kv[0, 1], cmp_pe[1], cmp_w1[1], cmp_w2[1])
    n_c = k_c.shape[2]
    cmp_start = jnp.arange(n_c) * CMP_STRIDE
    cmp_end = cmp_start + CMP_LEN - 1
    n_sel = s // SLC_LEN
    n_topk = min(SLC_TOPK, n_sel)
    sel_start = jnp.arange(n_sel) * SLC_LEN
    overlap = jnp.maximum(
        jnp.minimum(cmp_start[:, None] + CMP_LEN, sel_start[None, :] + SLC_LEN)
        - jnp.maximum(cmp_start[:, None], sel_start[None, :]), 0).astype(jnp.float32) / CMP_LEN
    k_s = kv[1, 0].reshape(b, g, n_sel, SLC_LEN, d)
    v_s = kv[1, 1].reshape(b, g, n_sel, SLC_LEN, d)
    pad = ((0, 0), (0, 0), (WINDOW, 0), (0, 0))
    k_w = jnp.pad(kv[2, 0], pad)
    v_w = jnp.pad(kv[2, 1], pad)
    gather = jax.vmap(jax.vmap(lambda blk, ix: blk[ix]))

    def block(args):
        i, qb, gb = args
        q_pos = i * QB + jnp.arange(QB)
        dist_c = q_pos[:, None] - cmp_end[None, :]
        sc = jnp.einsum('bghqd,bgcd->bghqc', qb, k_c, preferred_element_type=jnp.float32) * scale
        p_c = masked_softmax(sc - slopes * dist_c.astype(jnp.float32), dist_c >= 0)
        o_c = jnp.einsum('bghqc,bgcd->bghqd', p_c.astype(v_c.dtype), v_c)
        imp = jnp.einsum('bghqc,cj->bgqj', p_c, overlap)
        blk = (q_pos // SLC_LEN)[:, None]
        j = jnp.arange(n_sel)[None, :]
        forced = (j == 0) | (j == blk) | (j == blk - 1)
        imp = jnp.where(j > blk, NEG_INF, jnp.where(forced, BIG, imp))
        _, idx = lax.top_k(imp, n_topk)
        k_g = gather(k_s, idx)
        v_g = gather(v_s, idx)
        key_pos = idx[..., None] * SLC_LEN + jnp.arange(SLC_LEN)
        dist_s = (q_pos[:, None, None] - key_pos)[:, :, None]
        sc = jnp.einsum('bghqd,bgqkld->bghqkl', qb, k_g, preferred_element_type=jnp.float32) * scale
        sc = sc - slopes[..., None] * dist_s.astype(jnp.float32)
        p_s = masked_softmax(sc.reshape(b, g, hg, QB, n_topk * SLC_LEN),
                             (dist_s >= 0).reshape(b, g, 1, QB, n_topk * SLC_LEN))
        o_s = jnp.einsum('bghqn,bgqnd->bghqd', p_s.astype(v_g.dtype),
                         v_g.reshape(b, g, QB, n_topk * SLC_LEN, d))
        k_wb = lax.dynamic_slice_in_dim(k_w, i * QB, WINDOW + QB, axis=2)
        v_wb = lax.dynamic_slice_in_dim(v_w, i * QB, WINDOW + QB, axis=2)
        key_pos_w = i * QB - WINDOW + jnp.arange(WINDOW + QB)
        dist_w = q_pos[:, None] - key_pos_w[None, :]
        mask_w = (dist_w >= 0) & (dist_w < WINDOW) & (key_pos_w[None, :] >= 0)
        sc = jnp.einsum('bghqd,bgkd->bghqk', qb, k_wb, preferred_element_type=jnp.float32) * scale
        p_w = masked_softmax(sc - slopes * dist_w.astype(jnp.float32), mask_w)
        o_w = jnp.einsum('bghqk,bgkd->bghqd', p_w.astype(v_wb.dtype), v_wb)
        return gb[..., 0:1] * o_c + gb[..., 1:2] * o_s + gb[..., 2:3] * o_w

    return from_blocks(lax.map(block, (jnp.arange(s // QB), to_blocks(q, -2), to_blocks(gates, -2))))


def token_mixers(h, layer_idx, w_in, diff_lambda, diff_subln, mla_norm_q, mla_w_uq, mla_norm_kv,
                 mla_w_ukv, fox_b_f, nsa_cmp_pe, nsa_cmp_w1, nsa_cmp_w2, w_branch, w_out):
    b, s, _ = h.shape
    proj = h @ w_in
    (a_q, a_k, a_v, b_cq, b_ckv, b_kr, c_q, c_k, c_v, c_f,
     d_q, d_kv, d_g, gate) = jnp.split(proj, np.cumsum(IN_WIDTHS)[:-1].tolist(), axis=-1)

    lam_init = 0.8 - 0.6 * math.exp(-0.3 * layer_idx)
    lf = diff_lambda.astype(jnp.float32)
    lam = jnp.exp(jnp.sum(lf[0] * lf[1])) - jnp.exp(jnp.sum(lf[2] * lf[3])) + lam_init
    qa = a_q.reshape(b, s, DIFF_HEADS, 2, DIFF_DH).transpose(0, 2, 3, 1, 4)
    ka = a_k.reshape(b, s, DIFF_HEADS, 2, DIFF_DH).transpose(0, 2, 3, 1, 4)
    va = a_v.reshape(b, s, DIFF_HEADS, 2 * DIFF_DH).transpose(0, 2, 1, 3)
    y_a = merge_heads(diff_attention(qa, ka, va, lam, lam_init, diff_subln))

    y_b = merge_heads(mla_attention(b_cq, b_ckv, b_kr, mla_norm_q, mla_w_uq, mla_norm_kv, mla_w_ukv))

    def heads(t):
        return t.reshape(b, s, FOX_HEADS, FOX_DH).transpose(0, 2, 1, 3)
    log_f = jax.nn.log_sigmoid((c_f + fox_b_f).astype(jnp.float32)).transpose(0, 2, 1)
    y_c = merge_heads(forgetting_attention(heads(c_q), heads(c_k), heads(c_v), log_f))

    qd = d_q.reshape(b, s, NSA_GROUPS, NSA_HPG, NSA_DH).transpose(0, 2, 3, 1, 4)
    gd = jax.nn.sigmoid(d_g.reshape(b, s, NSA_GROUPS, NSA_HPG, 3)).transpose(0, 2, 3, 1, 4)
    kvd = d_kv.reshape(b, s, 3, 2, NSA_GROUPS, NSA_DH).transpose(2, 3, 0, 4, 1, 5)
    y_d = merge_heads(nsa_attention(qd, gd, kvd, nsa_cmp_pe, nsa_cmp_w1, nsa_cmp_w2))

    ys = jnp.stack([y_a, y_b, y_c, y_d], axis=2)
    branch_out = jnp.einsum('bsnc,ncd->bsnd', ys, w_branch)
    gates = jax.nn.sigmoid(gate.reshape(b, s, N_BRANCH, D_MODEL))
    merged = jnp.einsum('bsnd,bsnd->bsd', gates, branch_out)
    return merged @ w_out


def conv_glu_mlp(h, w_up, conv_w, conv_b, w_down):
    s = h.shape[1]
    u = h @ w_up
    up = jnp.pad(u, ((0, 0), (CONV_WIDTH - 1, 0), (0, 0)))
    u = conv_b + sum(conv_w[k] * up[:, k:k + s] for k in range(CONV_WIDTH))
    a, g = jnp.split(u, 2, axis=-1)
    return (jax.nn.silu(a) * g) @ w_down


def setup_inputs(seed: int = 0) -> dict:
    key = jax.random.key(seed)
    ks = iter(jax.random.split(key, 32))
    L = DEPTH

    def nrm(shape, fan_in):
        return jax.random.normal(next(ks), shape, jnp.float32) * fan_in ** -0.5

    def gain(shape):
        return 1.0 + 0.02 * jax.random.normal(next(ks), shape, jnp.float32)

    def small(shape, scale):
        return scale * jax.random.normal(next(ks), shape, jnp.float32)

    return {
        'x': jax.random.normal(next(ks), (BATCH, SEQ, D_MODEL), jnp.float32),
        'norm_mix': gain((L, D_MODEL)),
        'w_in': nrm((L, D_MODEL, N_IN), D_MODEL),
        'diff_lambda': small((L, 4, DIFF_DH), 0.1),
        'diff_subln': gain((L, 2 * DIFF_DH)),
        'mla_norm_q': gain((L, MLA_Q_LORA)),
        'mla_w_uq': nrm((L, MLA_Q_LORA, MLA_HEADS * (MLA_NOPE + MLA_ROPE)), MLA_Q_LORA),
        'mla_norm_kv': gain((L, MLA_KV_LORA)),
        'mla_w_ukv': nrm((L, MLA_KV_LORA, MLA_HEADS * (MLA_NOPE + MLA_VDIM)), MLA_KV_LORA),
        'fox_b_f': 3.0 + small((L, FOX_HEADS), 0.5),
        'nsa_cmp_pe': small((L, 2, CMP_LEN, NSA_DH), 0.02),
        'nsa_cmp_w1': nrm((L, 2, CMP_LEN * NSA_DH, CMP_HIDDEN), CMP_LEN * NSA_DH),
        'nsa_cmp_w2': nrm((L, 2, CMP_HIDDEN, NSA_DH), CMP_HIDDEN),
        'w_branch': nrm((L, N_BRANCH, BRANCH_WIDTH, D_MODEL), BRANCH_WIDTH),
        'w_out': nrm((L, D_MODEL, D_MODEL), D_MODEL),
        'norm_ffn': gain((L, D_MODEL)),
        'w_up': nrm((L, D_MODEL, 2 * D_FF), D_MODEL),
        'conv_w': nrm((L, CONV_WIDTH, 2 * D_FF), CONV_WIDTH),
        'conv_b': small((L, 2 * D_FF), 0.02),
        'w_down': nrm((L, D_FF, D_MODEL), D_FF),
        'norm_final': gain((D_MODEL,)),
    }


def reference(x, norm_mix, w_in, diff_lambda, diff_subln, mla_norm_q, mla_w_uq, mla_norm_kv, mla_w_ukv,
              fox_b_f, nsa_cmp_pe, nsa_cmp_w1, nsa_cmp_w2, w_branch, w_out, norm_ffn, w_up, conv_w,
              conv_b, w_down, norm_final):
    for l in range(DEPTH):
        h = rms_norm(x, norm_mix[l])
        x = x + token_mixers(h, l, w_in[l], diff_lambda[l], diff_subln[l], mla_norm_q[l], mla_w_uq[l],
                             mla_norm_kv[l], mla_w_ukv[l], fox_b_f[l], nsa_cmp_pe[l], nsa_cmp_w1[l],
                             nsa_cmp_w2[l], w_branch[l], w_out[l])
        h = rms_norm(x, norm_ffn[l])
        x = x + conv_glu_mlp(h, w_up[l], conv_w[l], conv_b[l], w_down[l])
    return rms_norm(x, norm_final)
```

```python
import functools
import math

import numpy as np
import jax
import jax.numpy as jnp
from jax import lax
from jax.experimental import pallas as pl
from jax.experimental.pallas import tpu as pltpu

F32 = jnp.float32
CDT = jnp.bfloat16

NEG = -1e30
NEG_INF = -1e30
BIG = 1e9
NORM_EPS = 1e-6
LANES = 128

D_MODEL = 1024
DIFF_HEADS, DIFF_DH = 4, 64
MLA_HEADS, MLA_NOPE, MLA_ROPE, MLA_VDIM = 4, 128, 64, 128
MLA_Q_LORA, MLA_KV_LORA = 256, 256
ROPE_THETA = 10000.0
FOX_HEADS, FOX_DH = 4, 128
NSA_HEADS, NSA_GROUPS, NSA_DH = 8, 2, 64
NSA_HPG = NSA_HEADS // NSA_GROUPS
CMP_STRIDE = 16
CMP_LEN = 2 * CMP_STRIDE
CMP_HIDDEN = 128
SLC_LEN = 64
SLC_SHIFT = 6
HALF_SHIFT = 6
SLC_TOPK = 8
WINDOW = 256
N_BRANCH = 4
BRANCH_WIDTH = 512
D_FF = 2816
CONV_WIDTH = 3

PB_AQ, PB_AK, PB_AV = 0, 4, 8
PB_CQ, PB_CK, PB_CV = 12, 16, 20
PB_DQ = 24
PB_CMP_K, PB_CMP_V, PB_SEL_K, PB_SEL_V, PB_WIN_K, PB_WIN_V = 28, 29, 30, 31, 32, 33
PB_BCQ, PB_BCKV, PB_BKR, PB_BKRS = 34, 36, 38, 39
PB_GATE = 40
N_PROJ = 72 * LANES
SMALL_F, SMALL_G = 0, 4

VMEM_LIMIT = 56 * 1024 * 1024


def _cparams(sem):
    return pltpu.CompilerParams(dimension_semantics=sem, vmem_limit_bytes=VMEM_LIMIT)


def _rms(xf, g):
    return xf * lax.rsqrt(jnp.mean(xf * xf, axis=-1, keepdims=True) + NORM_EPS) * g


def _dot(a, b):
    return jnp.dot(a, b, preferred_element_type=F32)


def _dot_nt(a, b):
    return lax.dot_general(a, b, (((1,), (1,)), ((), ())), preferred_element_type=F32)


def _split_dot(a, b):
    hi = a.astype(CDT)
    lo = (a - hi.astype(F32)).astype(CDT)
    return _dot(hi, b) + _dot(lo, b)


def _alibi_slopes(n):
    return np.exp2(-8.0 * np.arange(1, n + 1) / n).astype(np.float32)


def _inproj_kernel(x_ref, g_ref, w_ref, ws_ref, o_ref, os_ref, h_ref):
    @pl.when(pl.program_id(1) == 0)
    def _():
        h = _rms(x_ref[...], g_ref[...]).astype(CDT)
        h_ref[...] = h
        os_ref[...] = _dot(h, ws_ref[...])

    o_ref[...] = _dot(h_ref[...], w_ref[...]).astype(o_ref.dtype)


def _in_proj(x2, g, w, ws):
    t, d = x2.shape
    n = w.shape[1]
    tm = min(1024, t)
    tn = 1024
    return pl.pallas_call(
        _inproj_kernel,
        grid=(t // tm, n // tn),
        in_specs=[
            pl.BlockSpec((tm, d), lambda i, j: (i, 0)),
            pl.BlockSpec((1, d), lambda i, j: (0, 0)),
            pl.BlockSpec((d, tn), lambda i, j: (0, j)),
            pl.BlockSpec((d, LANES), lambda i, j: (0, 0)),
        ],
        out_specs=[
            pl.BlockSpec((tm, tn), lambda i, j: (i, j)),
            pl.BlockSpec((tm, LANES), lambda i, j: (i, 0)),
        ],
        out_shape=[jax.ShapeDtypeStruct((t, n), CDT), jax.ShapeDtypeStruct((t, LANES), F32)],
        scratch_shapes=[pltpu.VMEM((tm, d), CDT)],
        compiler_params=_cparams(("parallel", "arbitrary")),
        name="in_proj",
    )(x2, g.reshape(1, d), w, ws)


def _fox_cumsum_kernel(cf_ref, bf_ref, o_ref):
    rows, s = cf_ref.shape
    lane = lax.broadcasted_iota(jnp.int32, (rows, LANES), 1)
    carry = jnp.zeros((rows, 1), F32)
    for c in range(s // LANES):
        z = cf_ref[:, c * LANES:(c + 1) * LANES] + bf_ref[...]
        xs = jnp.minimum(z, 0.0) - jnp.log1p(jnp.exp(-jnp.abs(z)))
        d = 1
        while d < LANES:
            xs = xs + jnp.where(lane >= d, pltpu.roll(xs, d, axis=1), 0.0)
            d *= 2
        xs = xs + carry
        o_ref[:, c * LANES:(c + 1) * LANES] = xs
        carry = xs[:, LANES - 1:LANES]


def _fox_cumsum(cf_rows, bias_rows):
    return pl.pallas_call(
        _fox_cumsum_kernel,
        out_shape=jax.ShapeDtypeStruct(cf_rows.shape, F32),
        name="fox_cumsum",
    )(cf_rows, bias_rows)


def _flash_init(m_ref, l_ref, acc_ref):
    m_ref[...] = jnp.full(m_ref.shape, NEG, F32)
    l_ref[...] = jnp.zeros(l_ref.shape, F32)
    acc_ref[...] = jnp.zeros(acc_ref.shape, F32)


def _flash_update(s, v, m_ref, l_ref, acc_ref, rows=None):
    sl = slice(None) if rows is None else rows
    m_old = m_ref[sl]
    m_new = jnp.maximum(m_old, jnp.max(s, axis=-1, keepdims=True))
    alpha = jnp.exp(m_old - m_new)
    p = jnp.exp(s - m_new)
    l_ref[sl] = alpha * l_ref[sl] + jnp.sum(p, axis=-1, keepdims=True)
    acc_ref[sl] = alpha * acc_ref[sl] + _dot(p.astype(CDT), v)
    m_ref[sl] = m_new


def _causal_loop(qi, tile):
    def body(ki, c):
        tile(ki, False)
        return c
    lax.fori_loop(0, qi, body, 0)
    tile(qi, True)


def _diff_attn_kernel(slopes_ref, lam_ref, g_ref, q_ref, k_ref, v_ref, o_ref,
                      qq_ref, m_ref, l_ref, acc_ref, *, tq, lam_init):
    h = pl.program_id(1)
    qi = pl.program_id(2)
    slope = slopes_ref[h]
    q = q_ref[0]
    lane = lax.broadcasted_iota(jnp.int32, q.shape, 1)
    zero = jnp.zeros_like(q)
    qq_ref[0:tq] = jnp.where(lane < DIFF_DH, q, zero)
    qq_ref[tq:2 * tq] = jnp.where(lane >= DIFF_DH, q, zero)
    _flash_init(m_ref, l_ref, acc_ref)
    col = lax.broadcasted_iota(jnp.int32, (1, tq), 1).astype(F32)

    def tile(ki, masked):
        k0 = pl.multiple_of(ki * tq, tq)
        k = k_ref[0, pl.ds(k0, tq), :]
        v = v_ref[0, pl.ds(k0, tq), :]
        s = _dot_nt(qq_ref[...], k)
        rel = ((ki - qi) * tq).astype(F32)
        s = s + slope * (col + rel)
        if masked:
            r = lax.broadcasted_iota(jnp.int32, s.shape, 0)
            c = lax.broadcasted_iota(jnp.int32, s.shape, 1)
            r = jnp.where(r >= tq, r - tq, r)
            s = jnp.where(c <= r, s, NEG)
        _flash_update(s, v, m_ref, l_ref, acc_ref)

    _causal_loop(qi, tile)

    lf = lam_ref[...]
    lam = (jnp.exp(jnp.sum(lf[0:1] * lf[1:2], axis=-1, keepdims=True))
           - jnp.exp(jnp.sum(lf[2:3] * lf[3:4], axis=-1, keepdims=True)) + lam_init)
    o = acc_ref[...] / l_ref[...]
    d = o[0:tq] - lam * o[tq:2 * tq]
    o_ref[0] = (_rms(d, g_ref[...]) * (1.0 - lam_init)).astype(o_ref.dtype)


def _diff_attention(proj3, diff_lambda, subln, lam_init):
    b, s, _ = proj3.shape
    tq = min(512, s)
    dv = 2 * DIFF_DH
    kern = functools.partial(_diff_attn_kernel, tq=tq, lam_init=lam_init)
    return pl.pallas_call(
        kern,
        grid=(b, DIFF_HEADS, s // tq),
        in_specs=[
            pl.BlockSpec(memory_space=pltpu.SMEM),
            pl.BlockSpec((4, DIFF_DH), lambda bi, h, qi: (0, 0)),
            pl.BlockSpec((1, dv), lambda bi, h, qi: (0, 0)),
            pl.BlockSpec((1, tq, LANES), lambda bi, h, qi: (bi, qi, PB_AQ + h)),
            pl.BlockSpec((1, s, LANES), lambda bi, h, qi: (bi, 0, PB_AK + h)),
            pl.BlockSpec((1, s, LANES), lambda bi, h, qi: (bi, 0, PB_AV + h)),
        ],
        out_specs=pl.BlockSpec((1, tq, dv), lambda bi, h, qi: (bi, qi, h)),
        out_shape=jax.ShapeDtypeStruct((b, s, DIFF_HEADS * dv), CDT),
        scratch_shapes=[
            pltpu.VMEM((2 * tq, LANES), CDT),
            pltpu.VMEM((2 * tq, 1), F32),
            pltpu.VMEM((2 * tq, 1), F32),
            pltpu.VMEM((2 * tq, dv), F32),
        ],
        compiler_params=_cparams(("parallel", "parallel", "arbitrary")),
        name="diff_attention",
    )(jnp.asarray(_alibi_slopes(DIFF_HEADS)), diff_lambda, subln.reshape(1, dv), proj3, proj3, proj3)


def _mla_prep_kernel(cq_ref, ckv_ref, kr_ref, krs_ref, gq_ref, gkv_ref, wqm_ref, wqs_ref, wk_ref, wv_ref,
                     cosq_ref, sinq_ref, cosk_ref, sink_ref, q_ref, k_ref, v_ref):
    hq = _rms(cq_ref[0].astype(F32), gq_ref[...]).astype(CDT)
    qm = _dot(hq, wqm_ref[...])
    qs = _dot(hq, wqs_ref[...])
    cosq, sinq = cosq_ref[...], sinq_ref[...]
    hw = 2 * LANES
    for h in range(MLA_HEADS):
        sl = slice(h * hw, (h + 1) * hw)
        q_ref[0, :, sl] = (qm[:, sl] * cosq + qs[:, sl] * sinq).astype(q_ref.dtype)
    hkv = _rms(ckv_ref[0].astype(F32), gkv_ref[...]).astype(CDT)
    kn = _dot(hkv, wk_ref[...])
    v_ref[0] = _dot(hkv, wv_ref[...]).astype(v_ref.dtype)
    kpe = (kr_ref[0].astype(F32) * cosk_ref[...] + krs_ref[0].astype(F32) * sink_ref[...]).astype(k_ref.dtype)
    for h in range(MLA_HEADS):
        k_ref[0, :, h * hw:h * hw + LANES] = kn[:, h * LANES:(h + 1) * LANES].astype(k_ref.dtype)
        k_ref[0, :, h * hw + LANES:(h + 1) * hw] = kpe


def _mla_prep(proj3, gq, gkv, wqm, wqs, wk, wv, tabs):
    b, s, _ = proj3.shape
    tm = min(512, s)
    hw = 2 * LANES
    cosq, sinq, cosk, sink = tabs
    const = lambda shape: pl.BlockSpec(shape, lambda bi, i: (0,) * len(shape))
    return pl.pallas_call(
        _mla_prep_kernel,
        grid=(b, s // tm),
        in_specs=[
            pl.BlockSpec((1, tm, MLA_Q_LORA), lambda bi, i: (bi, i, PB_BCQ // 2)),
            pl.BlockSpec((1, tm, MLA_KV_LORA), lambda bi, i: (bi, i, PB_BCKV // 2)),
            pl.BlockSpec((1, tm, LANES), lambda bi, i: (bi, i, PB_BKR)),
            pl.BlockSpec((1, tm, LANES), lambda bi, i: (bi, i, PB_BKRS)),
            const((1, MLA_Q_LORA)), const((1, MLA_KV_LORA)),
            const((MLA_Q_LORA, MLA_HEADS * hw)), const((MLA_Q_LORA, MLA_HEADS * hw)),
            const((MLA_KV_LORA, MLA_HEADS * MLA_NOPE)), const((MLA_KV_LORA, MLA_HEADS * MLA_VDIM)),
            pl.BlockSpec((tm, hw), lambda bi, i: (i, 0)), pl.BlockSpec((tm, hw), lambda bi, i: (i, 0)),
            pl.BlockSpec((tm, LANES), lambda bi, i: (i, 0)), pl.BlockSpec((tm, LANES), lambda bi, i: (i, 0)),
        ],
        out_specs=[
            pl.BlockSpec((1, tm, MLA_HEADS * hw), lambda bi, i: (bi, i, 0)),
            pl.BlockSpec((1, tm, MLA_HEADS * hw), lambda bi, i: (bi, i, 0)),
            pl.BlockSpec((1, tm, MLA_HEADS * MLA_VDIM), lambda bi, i: (bi, i, 0)),
        ],
        out_shape=[
            jax.ShapeDtypeStruct((b, s, MLA_HEADS * hw), CDT),
            jax.ShapeDtypeStruct((b, s, MLA_HEADS * hw), CDT),
            jax.ShapeDtypeStruct((b, s, MLA_HEADS * MLA_VDIM), CDT),
        ],
        compiler_params=_cparams(("parallel", "parallel")),
        name="mla_prep",
    )(proj3, proj3, proj3, proj3, gq.reshape(1, -1), gkv.reshape(1, -1), wqm, wqs, wk, wv,
      cosq, sinq, cosk, sink)


def _plain_attn_kernel(q_ref, k_ref, v_ref, o_ref, m_ref, l_ref, acc_ref, *, tq):
    qi = pl.program_id(2)
    _flash_init(m_ref, l_ref, acc_ref)

    def tile(ki, masked):
        k0 = pl.multiple_of(ki * tq, tq)
        s = _dot_nt(q_ref[0], k_ref[0, pl.ds(k0, tq), :])
        if masked:
            r = lax.broadcasted_iota(jnp.int32, s.shape, 0)
            c = lax.broadcasted_iota(jnp.int32, s.shape, 1)
            s = jnp.where(c <= r, s, NEG)
        _flash_update(s, v_ref[0, pl.ds(k0, tq), :], m_ref, l_ref, acc_ref)

    _causal_loop(qi, tile)
    o_ref[0] = (acc_ref[...] / l_ref[...]).astype(o_ref.dtype)


def _mla_attention(qc, kc, v):
    b, s, _ = qc.shape
    tq = min(512, s)
    hw = 2 * LANES
    return pl.pallas_call(
        functools.partial(_plain_attn_kernel, tq=tq),
        grid=(b, MLA_HEADS, s // tq),
        in_specs=[
            pl.BlockSpec((1, tq, hw), lambda bi, h, qi: (bi, qi, h)),
            pl.BlockSpec((1, s, hw), lambda bi, h, qi: (bi, 0, h)),
            pl.BlockSpec((1, s, MLA_VDIM), lambda bi, h, qi: (bi, 0, h)),
        ],
        out_specs=pl.BlockSpec((1, tq, MLA_VDIM), lambda bi, h, qi: (bi, qi, h)),
        out_shape=jax.ShapeDtypeStruct((b, s, MLA_HEADS * MLA_VDIM), CDT),
        scratch_shapes=[pltpu.VMEM((tq, 1), F32), pltpu.VMEM((tq, 1), F32), pltpu.VMEM((tq, MLA_VDIM), F32)],
        compiler_params=_cparams(("parallel", "parallel", "arbitrary")),
        name="mla_attention",
    )(qc, kc, v)


def _fox_attn_kernel(c_ref, q_ref, k_ref, v_ref, o_ref, m_ref, l_ref, acc_ref, *, tq):
    qi = pl.program_id(2)
    _flash_init(m_ref, l_ref, acc_ref)
    cbase = c_ref[0, 0, pl.ds(qi, 1), :][:, 0:1]

    def tile(ki, masked):
        k0 = pl.multiple_of(ki * tq, tq)
        s = _dot_nt(q_ref[0], k_ref[0, pl.ds(k0, tq), :])
        s = s + (cbase - c_ref[0, 0, pl.ds(ki, 1), :])
        if masked:
            r = lax.broadcasted_iota(jnp.int32, s.shape, 0)
            c = lax.broadcasted_iota(jnp.int32, s.shape, 1)
            s = jnp.where(c <= r, s, NEG)
        _flash_update(s, v_ref[0, pl.ds(k0, tq), :], m_ref, l_ref, acc_ref)

    _causal_loop(qi, tile)
    o_ref[0] = (acc_ref[...] / l_ref[...]).astype(o_ref.dtype)


def _fox_attention(proj3, c4):
    b, s, _ = proj3.shape
    tq = min(512, s)
    nk = s // tq
    return pl.pallas_call(
        functools.partial(_fox_attn_kernel, tq=tq),
        grid=(b, FOX_HEADS, s // tq),
        in_specs=[
            pl.BlockSpec((1, 1, nk, tq), lambda bi, h, qi: (bi, h, 0, 0)),
            pl.BlockSpec((1, tq, FOX_DH), lambda bi, h, qi: (bi, qi, PB_CQ + h)),
            pl.BlockSpec((1, s, FOX_DH), lambda bi, h, qi: (bi, 0, PB_CK + h)),
            pl.BlockSpec((1, s, FOX_DH), lambda bi, h, qi: (bi, 0, PB_CV + h)),
        ],
        out_specs=pl.BlockSpec((1, tq, FOX_DH), lambda bi, h, qi: (bi, qi, h)),
        out_shape=jax.ShapeDtypeStruct((b, s, FOX_HEADS * FOX_DH), CDT),
        scratch_shapes=[pltpu.VMEM((tq, 1), F32), pltpu.VMEM((tq, 1), F32), pltpu.VMEM((tq, FOX_DH), F32)],
        compiler_params=_cparams(("parallel", "parallel", "arbitrary")),
        name="fox_attention",
    )(c4.reshape(b, FOX_HEADS, nk, tq), proj3, proj3, proj3)


def _nsa_compress_kernel(x_ref, w1a_ref, w1b_ref, pea_ref, peb_ref, w2_ref, o_ref):
    x = x_ref[0]
    n = x.shape[0]
    pa = _dot(x, w1a_ref[...])
    pb = _dot(x, w1b_ref[...])
    pe = _dot(pea_ref[...], w1a_ref[...]) + _dot(peb_ref[...], w1b_ref[...])
    hid = pa + pltpu.roll(pb, n - 1, axis=0) + pe[0:1]
    act = 0.5 * hid * (1.0 + jnp.tanh(math.sqrt(2.0 / math.pi) * (hid + 0.044715 * hid * hid * hid)))
    o_ref[0] = _dot(act.astype(CDT), w2_ref[...]).astype(o_ref.dtype)


def _nsa_compress(xc, w1a, w1b, pea, peb, w2):
    b, n, kdim = xc.shape
    hdim = w1a.shape[1]
    const = lambda shape: pl.BlockSpec(shape, lambda bi: (0,) * len(shape))
    return pl.pallas_call(
        _nsa_compress_kernel,
        grid=(b,),
        in_specs=[pl.BlockSpec((1, n, kdim), lambda bi: (bi, 0, 0)),
                  const((kdim, hdim)), const((kdim, hdim)), const((8, kdim)), const((8, kdim)),
                  const((hdim, w2.shape[1]))],
        out_specs=pl.BlockSpec((1, n, w2.shape[1]), lambda bi: (bi, 0, 0)),
        out_shape=jax.ShapeDtypeStruct((b, n, w2.shape[1]), CDT),
        compiler_params=_cparams(("parallel",)),
        name="nsa_compress",
    )(xc, w1a, w1b, pea, peb, w2)


def _nsa_cmp_kernel(slopes_ref, q_ref, kv_ref, oc_ref, sb_ref, *, tq, n_topk):
    qi = pl.program_id(1)
    nblk = kv_ref.shape[1]
    q0 = qi * tq
    rowpos = q0 + lax.broadcasted_iota(jnp.int32, (tq, 1), 0)
    nidx = lax.broadcasted_iota(jnp.int32, (1, nblk), 1)
    disti = rowpos - (nidx * CMP_STRIDE + CMP_LEN - 1)
    valid = disti >= 0
    dist = disti.astype(F32)
    lane = lax.broadcasted_iota(jnp.int32, (tq, LANES), 1)
    low = lane < NSA_DH
    nn = lax.broadcasted_iota(jnp.int32, (nblk, LANES), 0) * CMP_STRIDE
    jj = (lax.broadcasted_iota(jnp.int32, (nblk, LANES), 1) & (NSA_DH - 1)) * SLC_LEN
    ov = (jnp.maximum(jnp.minimum(nn + CMP_LEN, jj + SLC_LEN) - jnp.maximum(nn, jj), 0).astype(F32)
          * (1.0 / CMP_LEN)).astype(CDT)
    jl = (lane & (NSA_DH - 1)).astype(F32)
    blk = (rowpos >> SLC_SHIFT).astype(F32)
    forced = (jl == 0.0) | (jl == blk) | (jl == blk - 1.0)
    outs = []
    bias = []
    for g in range(NSA_GROUPS):
        kc = kv_ref[0, :, g * LANES:(g + 1) * LANES]
        vc = kv_ref[0, :, (NSA_GROUPS + g) * LANES:(NSA_GROUPS + g + 1) * LANES]
        psum = jnp.zeros((tq, nblk), F32)
        for j in range(NSA_HPG):
            qb = q_ref[0, :, j * LANES:(j + 1) * LANES]
            qm = jnp.where(low if g == 0 else jnp.logical_not(low), qb, jnp.zeros_like(qb))
            s = _dot_nt(qm, kc) - slopes_ref[g * NSA_HPG + j] * dist
            s = jnp.where(valid, s, NEG)
            e = jnp.where(valid, jnp.exp(s - jnp.max(s, axis=-1, keepdims=True)), 0.0)
            den = jnp.sum(e, axis=-1, keepdims=True)
            p = e / jnp.where(den > 0.0, den, 1.0)
            psum = psum + p
            outs.append(_dot(p.astype(CDT), vc))
        imp = _split_dot(psum, ov)
        imp = jnp.where(jl > blk, NEG_INF, jnp.where(forced, BIG, imp))
        sb = jnp.full((tq, LANES), NEG, F32)
        for _ in range(n_topk):
            mx = jnp.max(imp, axis=-1, keepdims=True)
            idx = jnp.min(jnp.where(imp == mx, jl, float(LANES)), axis=-1, keepdims=True)
            hit = jl == idx
            sb = jnp.where(hit, 0.0, sb)
            imp = jnp.where(hit, -jnp.inf, imp)
        bias.append(sb)
    sb_ref[0] = jnp.where(low, bias[1], bias[0]).astype(sb_ref.dtype)
    for blk_i in range(NSA_HEADS // 2):
        oc_ref[0, :, blk_i * LANES:(blk_i + 1) * LANES] = jnp.where(
            low, outs[2 * blk_i], outs[2 * blk_i + 1]).astype(oc_ref.dtype)


def _nsa_cmp_select(proj3, kvc, n_topk):
    b, s, _ = proj3.shape
    tq = min(256, s)
    nblk = kvc.shape[1]
    return pl.pallas_call(
        functools.partial(_nsa_cmp_kernel, tq=tq, n_topk=n_topk),
        grid=(b, s // tq),
        in_specs=[
            pl.BlockSpec(memory_space=pltpu.SMEM),
            pl.BlockSpec((1, tq, 4 * LANES), lambda bi, qi: (bi, qi, PB_DQ // 4)),
            pl.BlockSpec((1, nblk, kvc.shape[2]), lambda bi, qi: (bi, 0, 0)),
        ],
        out_specs=[
            pl.BlockSpec((1, tq, NSA_HEADS * NSA_DH), lambda bi, qi: (bi, qi, 0)),
            pl.BlockSpec((1, tq, LANES), lambda bi, qi: (bi, qi, 0)),
        ],
        out_shape=[jax.ShapeDtypeStruct((b, s, NSA_HEADS * NSA_DH), CDT),
                   jax.ShapeDtypeStruct((b, s, LANES), CDT)],
        compiler_params=_cparams(("parallel", "parallel")),
        name="nsa_cmp_select",
    )(jnp.asarray(_alibi_slopes(NSA_HEADS)), proj3, kvc)


def _compact_heads(heads, mine, low):
    both = [jnp.where(mine, a, pltpu.roll(a, NSA_DH, axis=1)) for a in heads]
    out = [jnp.where(low, both[2 * jj], both[2 * jj + 1]) for jj in range(NSA_HPG // 2)]
    return jnp.concatenate(out, axis=1)


def _nsa_win_kernel(slopes_ref, q_ref, kp_ref, kc_ref, vp_ref, vc_ref, o_ref, *, tq):
    g = pl.program_id(1)
    qi = pl.program_id(2)
    lane = lax.broadcasted_iota(jnp.int32, (tq, LANES), 1)
    low = lane < NSA_DH
    mine = (lane >> HALF_SHIFT) == g
    k = jnp.concatenate([kp_ref[0], kc_ref[0]], axis=0)
    v = jnp.concatenate([vp_ref[0], vc_ref[0]], axis=0)
    r = lax.broadcasted_iota(jnp.int32, (tq, 2 * tq), 0)
    c = lax.broadcasted_iota(jnp.int32, (tq, 2 * tq), 1)
    disti = r + tq - c
    valid = (disti >= 0) & (disti < WINDOW) & ((c >= tq) | (qi > 0))
    dist = disti.astype(F32)
    heads = []
    for j in range(NSA_HPG):
        qb = q_ref[0, :, j * LANES:(j + 1) * LANES]
        qm = jnp.where(mine, qb, jnp.zeros_like(qb))
        s = _dot_nt(qm, k) - slopes_ref[g * NSA_HPG + j] * dist
        s = jnp.where(valid, s, NEG)
        e = jnp.exp(s - jnp.max(s, axis=-1, keepdims=True))
        p = e / jnp.sum(e, axis=-1, keepdims=True)
        heads.append(_dot(p.astype(CDT), v))
    o_ref[0] = _compact_heads(heads, mine, low).astype(o_ref.dtype)


def _nsa_window(proj3):
    b, s, _ = proj3.shape
    tq = WINDOW
    return pl.pallas_call(
        functools.partial(_nsa_win_kernel, tq=tq),
        grid=(b, NSA_GROUPS, s // tq),
        in_specs=[
            pl.BlockSpec(memory_space=pltpu.SMEM),
            pl.BlockSpec((1, tq, 4 * LANES), lambda bi, g, qi: (bi, qi, PB_DQ // 4)),
            pl.BlockSpec((1, tq, LANES), lambda bi, g, qi: (bi, jnp.maximum(qi - 1, 0), PB_WIN_K)),
            pl.BlockSpec((1, tq, LANES), lambda bi, g, qi: (bi, qi, PB_WIN_K)),
            pl.BlockSpec((1, tq, LANES), lambda bi, g, qi: (bi, jnp.maximum(qi - 1, 0), PB_WIN_V)),
            pl.BlockSpec((1, tq, LANES), lambda bi, g, qi: (bi, qi, PB_WIN_V)),
        ],
        out_specs=pl.BlockSpec((1, tq, NSA_HPG * NSA_DH), lambda bi, g, qi: (bi, qi, g)),
        out_shape=jax.ShapeDtypeStruct((b, s, NSA_HEADS * NSA_DH), CDT),
        compiler_params=_cparams(("parallel", "parallel", "parallel")),
        name="nsa_window",
    )(jnp.asarray(_alibi_slopes(NSA_HEADS)), proj3, proj3, proj3, proj3, proj3)


def _nsa_sel_kernel(slopes_ref, q_ref, sb_ref, k_ref, v_ref, oc_ref, ow_ref, gl_ref, e_ref, o_ref,
                    qa_ref, m_ref, l_ref, acc_ref, *, tq):
    g = pl.program_id(1)
    qi = pl.program_id(2)
    lane = lax.broadcasted_iota(jnp.int32, (tq, LANES), 1)
    low = lane < NSA_DH
    mine = (lane >> HALF_SHIFT) == g
    sb = sb_ref[0]
    for j in range(NSA_HPG):
        qa_ref[j * tq:(j + 1) * tq] = jnp.where(mine, q_ref[0, :, j * LANES:(j + 1) * LANES], sb)
    _flash_init(m_ref, l_ref, acc_ref)
    col = lax.broadcasted_iota(jnp.int32, (1, tq), 1).astype(F32)
    jl = lane & (NSA_DH - 1)
    krow = lax.broadcasted_iota(jnp.int32, (tq, LANES), 0)

    def tile(ki, masked):
        k0 = pl.multiple_of(ki * tq, tq)
        k = k_ref[0, pl.ds(k0, tq), :]
        v = v_ref[0, pl.ds(k0, tq), :]
        onehot = jnp.where(((k0 + krow) >> SLC_SHIFT) == jl, 1.0, 0.0).astype(k.dtype)
        s_all = _dot_nt(qa_ref[...], jnp.where(mine, k, onehot))
        rel = ((ki - qi) * tq).astype(F32)
        for j in range(NSA_HPG):
            rows = slice(j * tq, (j + 1) * tq)
            s = s_all[rows] + slopes_ref[g * NSA_HPG + j] * (col + rel)
            if masked:
                r = lax.broadcasted_iota(jnp.int32, s.shape, 0)
                c = lax.broadcasted_iota(jnp.int32, s.shape, 1)
                s = jnp.where(c <= r, s, NEG)
            _flash_update(s, v, m_ref, l_ref, acc_ref, rows=rows)

    _causal_loop(qi, tile)

    o = acc_ref[...] / l_ref[...]
    o_s = _compact_heads([o[j * tq:(j + 1) * tq] for j in range(NSA_HPG)], mine, low)
    gates = _split_dot(jax.nn.sigmoid(gl_ref[0]), e_ref[0])
    w = NSA_HPG * NSA_DH
    y = (gates[:, 0:w] * oc_ref[0].astype(F32) + gates[:, w:2 * w] * o_s
         + gates[:, 2 * w:3 * w] * ow_ref[0].astype(F32))
    o_ref[0] = y.astype(o_ref.dtype)


def _nsa_selected(proj3, sbias, o_c, o_w, small3, expand):
    b, s, _ = proj3.shape
    tq = min(256, s)
    w = NSA_HPG * NSA_DH
    return pl.pallas_call(
        functools.partial(_nsa_sel_kernel, tq=tq),
        grid=(b, NSA_GROUPS, s // tq),
        in_specs=[
            pl.BlockSpec(memory_space=pltpu.SMEM),
            pl.BlockSpec((1, tq, 4 * LANES), lambda bi, g, qi: (bi, qi, PB_DQ // 4)),
            pl.BlockSpec((1, tq, LANES), lambda bi, g, qi: (bi, qi, 0)),
            pl.BlockSpec((1, s, LANES), lambda bi, g, qi: (bi, 0, PB_SEL_K)),
            pl.BlockSpec((1, s, LANES), lambda bi, g, qi: (bi, 0, PB_SEL_V)),
            pl.BlockSpec((1, tq, w), lambda bi, g, qi: (bi, qi, g)),
            pl.BlockSpec((1, tq, w), lambda bi, g, qi: (bi, qi, g)),
            pl.BlockSpec((1, tq, LANES), lambda bi, g, qi: (bi, qi, 0)),
            pl.BlockSpec((1, LANES, 3 * w), lambda bi, g, qi: (g, 0, 0)),
        ],
        out_specs=pl.BlockSpec((1, tq, w), lambda bi, g, qi: (bi, qi, g)),
        out_shape=jax.ShapeDtypeStruct((b, s, NSA_HEADS * NSA_DH), CDT),
        scratch_shapes=[
            pltpu.VMEM((NSA_HPG * tq, LANES), CDT),
            pltpu.VMEM((NSA_HPG * tq, 1), F32),
            pltpu.VMEM((NSA_HPG * tq, 1), F32),
            pltpu.VMEM((NSA_HPG * tq, LANES), F32),
        ],
        compiler_params=_cparams(("parallel", "parallel", "arbitrary")),
        name="nsa_selected",
    )(jnp.asarray(_alibi_slopes(NSA_HEADS)), proj3, sbias, proj3, proj3, o_c, o_w, small3, expand)


def _merge_kernel(ya_ref, yb_ref, yc_ref, yd_ref, ga_ref, gb_ref, gc_ref, gd_ref, wb_ref, wo_ref, x_ref, o_ref):
    merged = None
    for n, (y_ref, g_ref) in enumerate(((ya_ref, ga_ref), (yb_ref, gb_ref), (yc_ref, gc_ref), (yd_ref, gd_ref))):
        t = jax.nn.sigmoid(g_ref[...].astype(F32)) * _dot(y_ref[...], wb_ref[n])
        merged = t if merged is None else merged + t
    o_ref[...] = x_ref[...] + _dot(merged.astype(CDT), wo_ref[...])


def _merge(ys, proj2, wb, wo, x2):
    t, d = x2.shape
    tm = min(512, t)
    gate_blk = PB_GATE * LANES // d
    yspec = pl.BlockSpec((tm, BRANCH_WIDTH), lambda i: (i, 0))
    gspecs = [pl.BlockSpec((tm, d), functools.partial(lambda i, n: (i, gate_blk + n), n=n)) for n in range(N_BRANCH)]
    return pl.pallas_call(
        _merge_kernel,
        grid=(t // tm,),
        in_specs=[yspec] * N_BRANCH + gspecs + [
            pl.BlockSpec((N_BRANCH, BRANCH_WIDTH, d), lambda i: (0, 0, 0)),
            pl.BlockSpec((d, d), lambda i: (0, 0)),
            pl.BlockSpec((tm, d), lambda i: (i, 0)),
        ],
        out_specs=pl.BlockSpec((tm, d), lambda i: (i, 0)),
        out_shape=jax.ShapeDtypeStruct((t, d), F32),
        compiler_params=_cparams(("parallel",)),
        name="merge",
    )(*ys, proj2, proj2, proj2, proj2, wb, wo, x2)


HALO = 16


def _ffn_kernel(x_ref, xh_ref, g_ref, wu_ref, cw_ref, cb_ref, wd_ref, gf_ref, o_ref, he_ref, ua_ref, ug_ref,
                *, tm, fc, final):
    i = pl.program_id(1)
    x = x_ref[0]
    g = g_ref[...]
    xh = xh_ref[0] * (i > 0).astype(F32)
    he_ref[0:HALO] = _rms(xh, g).astype(CDT)
    he_ref[HALO:HALO + tm] = _rms(x, g).astype(CDT)
    he = he_ref[...]
    acc = jnp.zeros((tm, x.shape[1]), F32)
    for c in range(D_FF // fc):
        outs = []
        for half, u_ref in ((0, ua_ref), (1, ug_ref)):
            lo = half * D_FF + c * fc
            u_ref[...] = _dot(he, wu_ref[:, lo:lo + fc])
            conv = cb_ref[:, lo:lo + fc]
            for kk in range(CONV_WIDTH):
                off = HALO - (CONV_WIDTH - 1) + kk
                conv = conv + cw_ref[kk:kk + 1, lo:lo + fc] * u_ref[off:off + tm, :]
            outs.append(conv)
        a, gg = outs
        act = (a * jax.nn.sigmoid(a) * gg).astype(CDT)
        acc = acc + _dot(act, wd_ref[c * fc:(c + 1) * fc, :])
    y = x + acc
    if final:
        y = _rms(y, gf_ref[...])
    o_ref[0] = y


def _ffn(x3, g, wu, cw, cb, wd, gf, final):
    b, s, d = x3.shape
    tm = min(512, s)
    fc = 256
    const = lambda shape: pl.BlockSpec(shape, lambda bi, i: (0,) * len(shape))
    return pl.pallas_call(
        functools.partial(_ffn_kernel, tm=tm, fc=fc, final=final),
        grid=(b, s // tm),
        in_specs=[
            pl.BlockSpec((1, tm, d), lambda bi, i: (bi, i, 0)),
            pl.BlockSpec((1, HALO, d), lambda bi, i: (bi, jnp.maximum(i * (tm // HALO) - 1, 0), 0)),
            const((1, d)), const((d, 2 * D_FF)), const((CONV_WIDTH, 2 * D_FF)), const((1, 2 * D_FF)),
            const((D_FF, d)), const((1, d)),
        ],
        out_specs=pl.BlockSpec((1, tm, d), lambda bi, i: (bi, i, 0)),
        out_shape=jax.ShapeDtypeStruct((b, s, d), F32),
        scratch_shapes=[pltpu.VMEM((tm + HALO, d), CDT), pltpu.VMEM((tm + HALO, fc), F32),
                        pltpu.VMEM((tm + HALO, fc), F32)],
        compiler_params=_cparams(("parallel", "arbitrary")),
        name="conv_glu_mlp",
    )(x3, x3, g.reshape(1, d), wu, cw, cb.reshape(1, -1), wd, gf.reshape(1, d))


def _prep_w_in(w):
    widths = (512, 512, 512, MLA_Q_LORA, MLA_KV_LORA, MLA_ROPE, 512, 512, 512, FOX_HEADS,
              512, 768, 3 * NSA_HEADS, N_BRANCH * D_MODEL)
    offs = np.cumsum((0,) + widths)
    (a_q, a_k, a_v, b_cq, b_ckv, b_kr, c_q, c_k, c_v, c_f, d_q, d_kv, d_g, gate) = [
        w[:, offs[i]:offs[i + 1]] for i in range(len(widths))]
    d = w.shape[0]
    d_q = d_q.reshape(d, NSA_GROUPS, NSA_HPG, NSA_DH).transpose(0, 2, 1, 3).reshape(d, 512)
    half = MLA_ROPE // 2
    kr_swap = jnp.concatenate([-b_kr[:, half:], b_kr[:, :half]], axis=1)
    z64 = jnp.zeros((d, LANES - MLA_ROPE), w.dtype)
    big = jnp.concatenate([
        a_q * DIFF_DH ** -0.5, a_k, a_v,
        c_q * FOX_DH ** -0.5, c_k, c_v,
        d_q * NSA_DH ** -0.5, d_kv,
        b_cq, b_ckv, b_kr, z64, kr_swap, z64,
        gate], axis=1)
    small = jnp.concatenate([c_f, d_g, jnp.zeros((d, LANES - FOX_HEADS - 3 * NSA_HEADS), w.dtype)], axis=1)
    return big.astype(CDT), small.astype(CDT)


def _prep_mla(w_uq, w_ukv):
    r = w_uq.shape[0]
    hw = 2 * LANES
    half = MLA_ROPE // 2
    scale = (MLA_NOPE + MLA_ROPE) ** -0.5
    wq = (w_uq * scale).reshape(r, MLA_HEADS, MLA_NOPE + MLA_ROPE)
    nope, t1, t2 = wq[..., :MLA_NOPE], wq[..., MLA_NOPE:MLA_NOPE + half], wq[..., MLA_NOPE + half:]
    zpad = jnp.zeros((r, MLA_HEADS, hw - MLA_NOPE - MLA_ROPE), w_uq.dtype)
    wqm = jnp.concatenate([nope, t1, t2, zpad], axis=-1).reshape(r, MLA_HEADS * hw)
    wqs = jnp.concatenate([jnp.zeros_like(nope), -t2, t1, zpad], axis=-1).reshape(r, MLA_HEADS * hw)
    wkv = w_ukv.reshape(w_ukv.shape[0], MLA_HEADS, MLA_NOPE + MLA_VDIM)
    wk = wkv[..., :MLA_NOPE].reshape(-1, MLA_HEADS * MLA_NOPE)
    wv = wkv[..., MLA_NOPE:].reshape(-1, MLA_HEADS * MLA_VDIM)
    return wqm.astype(CDT), wqs.astype(CDT), wk.astype(CDT), wv.astype(CDT)


def _rope_tables(s):
    half = MLA_ROPE // 2
    inv_freq = ROPE_THETA ** (-jnp.arange(0, MLA_ROPE, 2, dtype=F32) / MLA_ROPE)
    ang = jnp.arange(s, dtype=F32)[:, None] * inv_freq[None, :]
    cos, sin = jnp.cos(ang), jnp.sin(ang)
    z = jnp.zeros((s, LANES - MLA_ROPE), F32)
    cosk = jnp.concatenate([cos, cos, z], axis=1)
    sink = jnp.concatenate([sin, sin, z], axis=1)
    cosq = jnp.concatenate([jnp.ones((s, MLA_NOPE), F32), cosk], axis=1)
    sinq = jnp.concatenate([jnp.zeros((s, MLA_NOPE), F32), sink], axis=1)
    return cosq, sinq, cosk, sink


def _prep_compress(pe, w1, w2):
    eye2 = jnp.eye(2, dtype=F32)
    w1r = w1.reshape(2, CMP_LEN, NSA_DH, CMP_HIDDEN)

    def expand(wpart):
        t = jnp.einsum('kpdh,kK,gG->pkgdKGh', wpart, eye2, eye2)
        return t.reshape(CMP_STRIDE * 4 * NSA_DH, 4 * CMP_HIDDEN)

    w1a, w1b = expand(w1r[:, :CMP_STRIDE]), expand(w1r[:, CMP_STRIDE:])

    def pe_row(p):
        t = jnp.broadcast_to(p.transpose(1, 0, 2)[:, :, None, :], (CMP_STRIDE, 2, NSA_GROUPS, NSA_DH))
        return jnp.pad(t.reshape(1, -1), ((0, 7), (0, 0)))

    pea, peb = pe_row(pe[:, :CMP_STRIDE]), pe_row(pe[:, CMP_STRIDE:])
    w2b = jnp.einsum('khd,kK,gG,u->kghKGud', w2, eye2, eye2, jnp.ones((2,), F32))
    w2b = w2b.reshape(4 * CMP_HIDDEN, 4 * 2 * NSA_DH)
    return w1a.astype(CDT), w1b.astype(CDT), pea.astype(CDT), peb.astype(CDT), w2b.astype(CDT)


def _gate_expand():
    e = np.zeros((NSA_GROUPS, LANES, 3, NSA_HPG, NSA_DH), np.float32)
    for g in range(NSA_GROUPS):
        for j in range(NSA_HPG):
            for br in range(3):
                e[g, SMALL_G + (g * NSA_HPG + j) * 3 + br, br, j, :] = 1.0
    return jnp.asarray(e.reshape(NSA_GROUPS, LANES, 3 * NSA_HPG * NSA_DH)).astype(CDT)


def _token_mixers(x3, l, norm_mix, w_in, diff_lambda, diff_subln, mla_norm_q, mla_w_uq, mla_norm_kv, mla_w_ukv,
                  fox_b_f, nsa_cmp_pe, nsa_cmp_w1, nsa_cmp_w2, w_branch, w_out, rope_tabs):
    b, s, d = x3.shape
    t = b * s
    x2 = x3.reshape(t, d)
    w_big, w_small = _prep_w_in(w_in)
    proj, small = _in_proj(x2, norm_mix, w_big, w_small)
    proj3 = proj.reshape(b, s, N_PROJ)
    small3 = small.reshape(b, s, LANES)

    lam_init = 0.8 - 0.6 * math.exp(-0.3 * l)
    y_a = _diff_attention(proj3, diff_lambda, diff_subln, lam_init)

    wqm, wqs, wk, wv = _prep_mla(mla_w_uq, mla_w_ukv)
    qc, kc, vv = _mla_prep(proj3, mla_norm_q, mla_norm_kv, wqm, wqs, wk, wv, rope_tabs)
    y_b = _mla_attention(qc, kc, vv)

    cf_rows = small3[:, :, SMALL_F:SMALL_F + FOX_HEADS].transpose(0, 2, 1).reshape(b * FOX_HEADS, s)
    bias_rows = jnp.tile(fox_b_f.astype(F32), b).reshape(b * FOX_HEADS, 1)
    c4 = _fox_cumsum(cf_rows, bias_rows)
    y_c = _fox_attention(proj3, c4)

    w1a, w1b, pea, peb, w2b = _prep_compress(nsa_cmp_pe, nsa_cmp_w1, nsa_cmp_w2)
    xc = proj3[:, :, PB_CMP_K * LANES:(PB_CMP_V + 1) * LANES].reshape(b, s // CMP_STRIDE, CMP_STRIDE * 2 * LANES)
    kvc = _nsa_compress(xc, w1a, w1b, pea, peb, w2b)
    n_topk = min(SLC_TOPK, s // SLC_LEN)
    o_c, sbias = _nsa_cmp_select(proj3, kvc, n_topk)
    o_w = _nsa_window(proj3)
    y_d = _nsa_selected(proj3, sbias, o_c, o_w, small3, _gate_expand())

    ys = [y.reshape(t, BRANCH_WIDTH) for y in (y_a, y_b, y_c, y_d)]
    return _merge(ys, proj, w_branch.astype(CDT), w_out.astype(CDT), x2).reshape(b, s, d)


def kernel(x, norm_mix, w_in, diff_lambda, diff_subln, mla_norm_q, mla_w_uq, mla_norm_kv, mla_w_ukv, fox_b_f,
           nsa_cmp_pe, nsa_cmp_w1, nsa_cmp_w2, w_branch, w_out, norm_ffn, w_up, conv_w, conv_b, w_down, norm_final):
    depth = w_in.shape[0]
    s = x.shape[1]
    rope_tabs = _rope_tables(s)
    for l in range(depth):
        x = _token_mixers(x, l, norm_mix[l], w_in[l], diff_lambda[l], diff_subln[l], mla_norm_q[l], mla_w_uq[l],
                          mla_norm_kv[l], mla_w_ukv[l], fox_b_f[l], nsa_cmp_pe[l], nsa_cmp_w1[l], nsa_cmp_w2[l],
                          w_branch[l], w_out[l], rope_tabs)
        x = _ffn(x, norm_ffn[l], w_up[l].astype(CDT), conv_w[l], conv_b[l], w_down[l].astype(CDT), norm_final,
                 final=(l == depth - 1))
    return x
```

```python
import functools
import math

import numpy as np
import jax
import jax.numpy as jnp
from jax import lax
from jax.experimental import pallas as pl
from jax.experimental.pallas import tpu as pltpu

F32 = jnp.float32
CDT = jnp.bfloat16

NEG = -1e30
NEG_INF = -1e30
BIG = 1e9
NORM_EPS = 1e-6
LOG2E = 1.4426950408889634
LANES = 128

D_MODEL = 1024
DIFF_HEADS, DIFF_DH = 4, 64
MLA_HEADS, MLA_NOPE, MLA_ROPE, MLA_VDIM = 4, 128, 64, 128
MLA_Q_LORA, MLA_KV_LORA = 256, 256
ROPE_THETA = 10000.0
FOX_HEADS, FOX_DH = 4, 128
NSA_HEADS, NSA_GROUPS, NSA_DH = 8, 2, 64
NSA_HPG = NSA_HEADS // NSA_GROUPS
CMP_STRIDE = 16
CMP_LEN = 2 * CMP_STRIDE
CMP_HIDDEN = 128
SLC_LEN = 64
SLC_SHIFT = 6
HALF_SHIFT = 6
SLC_TOPK = 8
WINDOW = 256
N_BRANCH = 4
BRANCH_WIDTH = 512
D_FF = 2816
CONV_WIDTH = 3

PB_AQ, PB_AK, PB_AV = 0, 4, 8
PB_CQ, PB_CK, PB_CV = 12, 16, 20
PB_DQ = 24
PB_CMP_K, PB_CMP_V, PB_SEL_K, PB_SEL_V, PB_WIN_K, PB_WIN_V = 28, 29, 30, 31, 32, 33
PB_BCQ, PB_BCKV, PB_BKR, PB_BKRS = 34, 36, 38, 39
PB_GATE = 40
N_PROJ = 72 * LANES
SMALL_F, SMALL_G = 0, 4

VMEM_LIMIT = 56 * 1024 * 1024


def _cparams(sem):
    return pltpu.CompilerParams(dimension_semantics=sem, vmem_limit_bytes=VMEM_LIMIT)


def _rms(xf, g):
    return xf * lax.rsqrt(jnp.mean(xf * xf, axis=-1, keepdims=True) + NORM_EPS) * g


def _dot(a, b):
    return jnp.dot(a, b, preferred_element_type=F32)


def _dot_nt(a, b):
    return lax.dot_general(a, b, (((1,), (1,)), ((), ())), preferred_element_type=F32)


def _split_dot(a, b):
    hi = a.astype(CDT)
    lo = (a - hi.astype(F32)).astype(CDT)
    return _dot(hi, b) + _dot(lo, b)


def _alibi_slopes(n):
    return (LOG2E * np.exp2(-8.0 * np.arange(1, n + 1) / n)).astype(np.float32)


def _inproj_kernel(x_ref, g_ref, w_ref, ws_ref, o_ref, os_ref, h_ref):
    @pl.when(pl.program_id(1) == 0)
    def _():
        h = _rms(x_ref[...], g_ref[...]).astype(CDT)
        h_ref[...] = h
        os_ref[...] = _dot(h, ws_ref[...])

    o_ref[...] = _dot(h_ref[...], w_ref[...]).astype(o_ref.dtype)


def _in_proj(x2, g, w, ws):
    t, d = x2.shape
    n = w.shape[1]
    tm = min(1024, t)
    tn = 1024
    return pl.pallas_call(
        _inproj_kernel,
        grid=(t // tm, n // tn),
        in_specs=[
            pl.BlockSpec((tm, d), lambda i, j: (i, 0)),
            pl.BlockSpec((1, d), lambda i, j: (0, 0)),
            pl.BlockSpec((d, tn), lambda i, j: (0, j)),
            pl.BlockSpec((d, LANES), lambda i, j: (0, 0)),
        ],
        out_specs=[
            pl.BlockSpec((tm, tn), lambda i, j: (i, j)),
            pl.BlockSpec((tm, LANES), lambda i, j: (i, 0)),
        ],
        out_shape=[jax.ShapeDtypeStruct((t, n), CDT), jax.ShapeDtypeStruct((t, LANES), F32)],
        scratch_shapes=[pltpu.VMEM((tm, d), CDT)],
        compiler_params=_cparams(("parallel", "arbitrary")),
        name="in_proj",
    )(x2, g.reshape(1, d), w, ws)


def _fox_cumsum_kernel(cf_ref, bf_ref, o_ref):
    rows, s = cf_ref.shape
    lane = lax.broadcasted_iota(jnp.int32, (rows, LANES), 1)
    carry = jnp.zeros((rows, 1), F32)
    for c in range(s // LANES):
        z = cf_ref[:, c * LANES:(c + 1) * LANES] + bf_ref[...]
        xs = jnp.minimum(z, 0.0) - jnp.log1p(jnp.exp(-jnp.abs(z)))
        d = 1
        while d < LANES:
            xs = xs + jnp.where(lane >= d, pltpu.roll(xs, d, axis=1), 0.0)
            d *= 2
        xs = xs + carry
        o_ref[:, c * LANES:(c + 1) * LANES] = xs
        carry = xs[:, LANES - 1:LANES]


def _fox_cumsum(cf_rows, bias_rows):
    return pl.pallas_call(
        _fox_cumsum_kernel,
        out_shape=jax.ShapeDtypeStruct(cf_rows.shape, F32),
        name="fox_cumsum",
    )(cf_rows, bias_rows)


def _flash_init(m_ref, l_ref, acc_ref):
    m_ref[...] = jnp.full(m_ref.shape, NEG, F32)
    l_ref[...] = jnp.zeros(l_ref.shape, F32)
    acc_ref[...] = jnp.zeros(acc_ref.shape, F32)


def _flash_update(s, v, m_ref, l_ref, acc_ref, rows=None):
    sl = slice(None) if rows is None else rows
    m_old = m_ref[sl]
    m_new = jnp.maximum(m_old, jnp.max(s, axis=-1, keepdims=True))
    alpha = jnp.exp2(m_old - m_new)
    p = jnp.exp2(s - jnp.tile(m_new, (1, s.shape[1] // LANES)))
    l_ref[sl] = alpha * l_ref[sl] + jnp.sum(p, axis=-1, keepdims=True)
    acc_ref[sl] = alpha * acc_ref[sl] + _dot(p.astype(CDT), v)
    m_ref[sl] = m_new


def _causal_loop(qi, tile):
    def body(ki, c):
        tile(ki, False)
        return c
    lax.fori_loop(0, qi, body, 0)
    tile(qi, True)


def _diff_attn_kernel(slopes_ref, lam_ref, g_ref, q_ref, k_ref, v_ref, o_ref,
                      qq_ref, m_ref, l_ref, acc_ref, *, tq, lam_init):
    h = pl.program_id(1)
    qi = pl.program_id(2)
    slope = slopes_ref[h]
    q = q_ref[0]
    lane = lax.broadcasted_iota(jnp.int32, q.shape, 1)
    zero = jnp.zeros_like(q)
    qq_ref[0:tq] = jnp.where(lane < DIFF_DH, q, zero)
    qq_ref[tq:2 * tq] = jnp.where(lane >= DIFF_DH, q, zero)
    _flash_init(m_ref, l_ref, acc_ref)
    col = lax.broadcasted_iota(jnp.int32, (1, tq), 1).astype(F32)

    def tile(ki, masked):
        k0 = pl.multiple_of(ki * tq, tq)
        k = k_ref[0, pl.ds(k0, tq), :]
        v = v_ref[0, pl.ds(k0, tq), :]
        s = _dot_nt(qq_ref[...], k)
        rel = ((ki - qi) * tq).astype(F32)
        s = s + slope * (col + rel)
        if masked:
            r = lax.broadcasted_iota(jnp.int32, s.shape, 0)
            c = lax.broadcasted_iota(jnp.int32, s.shape, 1)
            r = jnp.where(r >= tq, r - tq, r)
            s = jnp.where(c <= r, s, NEG)
        _flash_update(s, v, m_ref, l_ref, acc_ref)

    _causal_loop(qi, tile)

    lf = lam_ref[...]
    lam = (jnp.exp(jnp.sum(lf[0:1] * lf[1:2], axis=-1, keepdims=True))
           - jnp.exp(jnp.sum(lf[2:3] * lf[3:4], axis=-1, keepdims=True)) + lam_init)
    o = acc_ref[...] / l_ref[...]
    d = o[0:tq] - lam * o[tq:2 * tq]
    o_ref[0] = (_rms(d, g_ref[...]) * (1.0 - lam_init)).astype(o_ref.dtype)


def _diff_attention(proj3, diff_lambda, subln, lam_init):
    b, s, _ = proj3.shape
    tq = min(512, s)
    dv = 2 * DIFF_DH
    kern = functools.partial(_diff_attn_kernel, tq=tq, lam_init=lam_init)
    return pl.pallas_call(
        kern,
        grid=(b, DIFF_HEADS, s // tq),
        in_specs=[
            pl.BlockSpec(memory_space=pltpu.SMEM),
            pl.BlockSpec((4, DIFF_DH), lambda bi, h, qi: (0, 0)),
            pl.BlockSpec((1, dv), lambda bi, h, qi: (0, 0)),
            pl.BlockSpec((1, tq, LANES), lambda bi, h, qi: (bi, qi, PB_AQ + h)),
            pl.BlockSpec((1, s, LANES), lambda bi, h, qi: (bi, 0, PB_AK + h)),
            pl.BlockSpec((1, s, LANES), lambda bi, h, qi: (bi, 0, PB_AV + h)),
        ],
        out_specs=pl.BlockSpec((1, tq, dv), lambda bi, h, qi: (bi, qi, h)),
        out_shape=jax.ShapeDtypeStruct((b, s, DIFF_HEADS * dv), CDT),
        scratch_shapes=[
            pltpu.VMEM((2 * tq, LANES), CDT),
            pltpu.VMEM((2 * tq, LANES), F32),
            pltpu.VMEM((2 * tq, LANES), F32),
            pltpu.VMEM((2 * tq, dv), F32),
        ],
        compiler_params=_cparams(("parallel", "parallel", "arbitrary")),
        name="diff_attention",
    )(jnp.asarray(_alibi_slopes(DIFF_HEADS)), diff_lambda, subln.reshape(1, dv), proj3, proj3, proj3)


def _mla_prep_kernel(cq_ref, ckv_ref, kr_ref, krs_ref, gq_ref, gkv_ref, wqm_ref, wqs_ref, wk_ref, wv_ref,
                     cosq_ref, sinq_ref, cosk_ref, sink_ref, q_ref, k_ref, v_ref):
    hq = _rms(cq_ref[0].astype(F32), gq_ref[...]).astype(CDT)
    qm = _dot(hq, wqm_ref[...])
    qs = _dot(hq, wqs_ref[...])
    cosq, sinq = cosq_ref[...], sinq_ref[...]
    hw = 2 * LANES
    for h in range(MLA_HEADS):
        sl = slice(h * hw, (h + 1) * hw)
        q_ref[0, :, sl] = (qm[:, sl] * cosq + qs[:, sl] * sinq).astype(q_ref.dtype)
    hkv = _rms(ckv_ref[0].astype(F32), gkv_ref[...]).astype(CDT)
    kn = _dot(hkv, wk_ref[...])
    v_ref[0] = _dot(hkv, wv_ref[...]).astype(v_ref.dtype)
    kpe = (kr_ref[0].astype(F32) * cosk_ref[...] + krs_ref[0].astype(F32) * sink_ref[...]).astype(k_ref.dtype)
    for h in range(MLA_HEADS):
        k_ref[0, :, h * hw:h * hw + LANES] = kn[:, h * LANES:(h + 1) * LANES].astype(k_ref.dtype)
        k_ref[0, :, h * hw + LANES:(h + 1) * hw] = kpe


def _mla_prep(proj3, gq, gkv, wqm, wqs, wk, wv, tabs):
    b, s, _ = proj3.shape
    tm = min(512, s)
    hw = 2 * LANES
    cosq, sinq, cosk, sink = tabs
    const = lambda shape: pl.BlockSpec(shape, lambda bi, i: (0,) * len(shape))
    return pl.pallas_call(
        _mla_prep_kernel,
        grid=(b, s // tm),
        in_specs=[
            pl.BlockSpec((1, tm, MLA_Q_LORA), lambda bi, i: (bi, i, PB_BCQ // 2)),
            pl.BlockSpec((1, tm, MLA_KV_LORA), lambda bi, i: (bi, i, PB_BCKV // 2)),
            pl.BlockSpec((1, tm, LANES), lambda bi, i: (bi, i, PB_BKR)),
            pl.BlockSpec((1, tm, LANES), lambda bi, i: (bi, i, PB_BKRS)),
            const((1, MLA_Q_LORA)), const((1, MLA_KV_LORA)),
            const((MLA_Q_LORA, MLA_HEADS * hw)), const((MLA_Q_LORA, MLA_HEADS * hw)),
            const((MLA_KV_LORA, MLA_HEADS * MLA_NOPE)), const((MLA_KV_LORA, MLA_HEADS * MLA_VDIM)),
            pl.BlockSpec((tm, hw), lambda bi, i: (i, 0)), pl.BlockSpec((tm, hw), lambda bi, i: (i, 0)),
            pl.BlockSpec((tm, LANES), lambda bi, i: (i, 0)), pl.BlockSpec((tm, LANES), lambda bi, i: (i, 0)),
        ],
        out_specs=[
            pl.BlockSpec((1, tm, MLA_HEADS * hw), lambda bi, i: (bi, i, 0)),
            pl.BlockSpec((1, tm, MLA_HEADS * hw), lambda bi, i: (bi, i, 0)),
            pl.BlockSpec((1, tm, MLA_HEADS * MLA_VDIM), lambda bi, i: (bi, i, 0)),
        ],
        out_shape=[
            jax.ShapeDtypeStruct((b, s, MLA_HEADS * hw), CDT),
            jax.ShapeDtypeStruct((b, s, MLA_HEADS * hw), CDT),
            jax.ShapeDtypeStruct((b, s, MLA_HEADS * MLA_VDIM), CDT),
        ],
        compiler_params=_cparams(("parallel", "parallel")),
        name="mla_prep",
    )(proj3, proj3, proj3, proj3, gq.reshape(1, -1), gkv.reshape(1, -1), wqm, wqs, wk, wv,
      cosq, sinq, cosk, sink)


def _plain_attn_kernel(q_ref, k_ref, v_ref, o_ref, m_ref, l_ref, acc_ref, *, tq):
    qi = pl.program_id(2)
    _flash_init(m_ref, l_ref, acc_ref)

    def tile(ki, masked):
        k0 = pl.multiple_of(ki * tq, tq)
        s = _dot_nt(q_ref[0], k_ref[0, pl.ds(k0, tq), :])
        if masked:
            r = lax.broadcasted_iota(jnp.int32, s.shape, 0)
            c = lax.broadcasted_iota(jnp.int32, s.shape, 1)
            s = jnp.where(c <= r, s, NEG)
        _flash_update(s, v_ref[0, pl.ds(k0, tq), :], m_ref, l_ref, acc_ref)

    _causal_loop(qi, tile)
    o_ref[0] = (acc_ref[...] / l_ref[...]).astype(o_ref.dtype)


def _mla_attention(qc, kc, v):
    b, s, _ = qc.shape
    tq = min(512, s)
    hw = 2 * LANES
    return pl.pallas_call(
        functools.partial(_plain_attn_kernel, tq=tq),
        grid=(b, MLA_HEADS, s // tq),
        in_specs=[
            pl.BlockSpec((1, tq, hw), lambda bi, h, qi: (bi, qi, h)),
            pl.BlockSpec((1, s, hw), lambda bi, h, qi: (bi, 0, h)),
            pl.BlockSpec((1, s, MLA_VDIM), lambda bi, h, qi: (bi, 0, h)),
        ],
        out_specs=pl.BlockSpec((1, tq, MLA_VDIM), lambda bi, h, qi: (bi, qi, h)),
        out_shape=jax.ShapeDtypeStruct((b, s, MLA_HEADS * MLA_VDIM), CDT),
        scratch_shapes=[pltpu.VMEM((tq, LANES), F32), pltpu.VMEM((tq, LANES), F32),pltpu.VMEM((tq, MLA_VDIM), F32)],
        compiler_params=_cparams(("parallel", "parallel", "arbitrary")),
        name="mla_attention",
    )(qc, kc, v)


def _fox_attn_kernel(c_ref, q_ref, k_ref, v_ref, o_ref, m_ref, l_ref, acc_ref, *, tq):
    qi = pl.program_id(2)
    _flash_init(m_ref, l_ref, acc_ref)
    cbase = c_ref[0, 0, pl.ds(qi, 1), :][:, 0:1]

    def tile(ki, masked):
        k0 = pl.multiple_of(ki * tq, tq)
        s = _dot_nt(q_ref[0], k_ref[0, pl.ds(k0, tq), :])
        s = s + LOG2E * (cbase - c_ref[0, 0, pl.ds(ki, 1), :])
        if masked:
            r = lax.broadcasted_iota(jnp.int32, s.shape, 0)
            c = lax.broadcasted_iota(jnp.int32, s.shape, 1)
            s = jnp.where(c <= r, s, NEG)
        _flash_update(s, v_ref[0, pl.ds(k0, tq), :], m_ref, l_ref, acc_ref)

    _causal_loop(qi, tile)
    o_ref[0] = (acc_ref[...] / l_ref[...]).astype(o_ref.dtype)


def _fox_attention(proj3, c4):
    b, s, _ = proj3.shape
    tq = min(512, s)
    nk = s // tq
    return pl.pallas_call(
        functools.partial(_fox_attn_kernel, tq=tq),
        grid=(b, FOX_HEADS, s // tq),
        in_specs=[
            pl.BlockSpec((1, 1, nk, tq), lambda bi, h, qi: (bi, h, 0, 0)),
            pl.BlockSpec((1, tq, FOX_DH), lambda bi, h, qi: (bi, qi, PB_CQ + h)),
            pl.BlockSpec((1, s, FOX_DH), lambda bi, h, qi: (bi, 0, PB_CK + h)),
            pl.BlockSpec((1, s, FOX_DH), lambda bi, h, qi: (bi, 0, PB_CV + h)),
        ],
        out_specs=pl.BlockSpec((1, tq, FOX_DH), lambda bi, h, qi: (bi, qi, h)),
        out_shape=jax.ShapeDtypeStruct((b, s, FOX_HEADS * FOX_DH), CDT),
        scratch_shapes=[pltpu.VMEM((tq, LANES), F32), pltpu.VMEM((tq, LANES), F32),pltpu.VMEM((tq, FOX_DH), F32)],
        compiler_params=_cparams(("parallel", "parallel", "arbitrary")),
        name="fox_attention",
    )(c4.reshape(b, FOX_HEADS, nk, tq), proj3, proj3, proj3)


def _nsa_compress_kernel(x_ref, w1a_ref, w1b_ref, pea_ref, peb_ref, w2_ref, o_ref):
    x = x_ref[0]
    n = x.shape[0]
    pa = _dot(x, w1a_ref[...])
    pb = _dot(x, w1b_ref[...])
    pe = _dot(pea_ref[...], w1a_ref[...]) + _dot(peb_ref[...], w1b_ref[...])
    hid = pa + pltpu.roll(pb, n - 1, axis=0) + pe[0:1]
    act = 0.5 * hid * (1.0 + jnp.tanh(math.sqrt(2.0 / math.pi) * (hid + 0.044715 * hid * hid * hid)))
    o_ref[0] = _dot(act.astype(CDT), w2_ref[...]).astype(o_ref.dtype)


def _nsa_compress(xc, w1a, w1b, pea, peb, w2):
    b, n, kdim = xc.shape
    hdim = w1a.shape[1]
    const = lambda shape: pl.BlockSpec(shape, lambda bi: (0,) * len(shape))
    return pl.pallas_call(
        _nsa_compress_kernel,
        grid=(b,),
        in_specs=[pl.BlockSpec((1, n, kdim), lambda bi: (bi, 0, 0)),
                  const((kdim, hdim)), const((kdim, hdim)), const((8, kdim)), const((8, kdim)),
                  const((hdim, w2.shape[1]))],
        out_specs=pl.BlockSpec((1, n, w2.shape[1]), lambda bi: (bi, 0, 0)),
        out_shape=jax.ShapeDtypeStruct((b, n, w2.shape[1]), CDT),
        compiler_params=_cparams(("parallel",)),
        name="nsa_compress",
    )(xc, w1a, w1b, pea, peb, w2)


def _nsa_cmp_kernel(slopes_ref, q_ref, kv_ref, oc_ref, sb_ref, *, tq, n_topk):
    qi = pl.program_id(1)
    nblk = kv_ref.shape[1]
    q0 = qi * tq
    rowpos = q0 + lax.broadcasted_iota(jnp.int32, (tq, 1), 0)
    nidx = lax.broadcasted_iota(jnp.int32, (1, nblk), 1)
    disti = rowpos - (nidx * CMP_STRIDE + CMP_LEN - 1)
    valid = disti >= 0
    dist = disti.astype(F32)
    lane = lax.broadcasted_iota(jnp.int32, (tq, LANES), 1)
    low = lane < NSA_DH
    nn = lax.broadcasted_iota(jnp.int32, (nblk, LANES), 0) * CMP_STRIDE
    jj = (lax.broadcasted_iota(jnp.int32, (nblk, LANES), 1) & (NSA_DH - 1)) * SLC_LEN
    ov = (jnp.maximum(jnp.minimum(nn + CMP_LEN, jj + SLC_LEN) - jnp.maximum(nn, jj), 0).astype(F32)
          * (1.0 / CMP_LEN)).astype(CDT)
    jl = (lane & (NSA_DH - 1)).astype(F32)
    blk = (rowpos >> SLC_SHIFT).astype(F32)
    forced = (jl == 0.0) | (jl == blk) | (jl == blk - 1.0)
    outs = []
    bias = []
    for g in range(NSA_GROUPS):
        kc = kv_ref[0, :, g * LANES:(g + 1) * LANES]
        vc = kv_ref[0, :, (NSA_GROUPS + g) * LANES:(NSA_GROUPS + g + 1) * LANES]
        psum = jnp.zeros((tq, nblk), F32)
        for j in range(NSA_HPG):
            qb = q_ref[0, :, j * LANES:(j + 1) * LANES]
            qm = jnp.where(low if g == 0 else jnp.logical_not(low), qb, jnp.zeros_like(qb))
            s = _dot_nt(qm, kc) - slopes_ref[g * NSA_HPG + j] * dist
            s = jnp.where(valid, s, NEG)
            e = jnp.where(valid, jnp.exp2(s - jnp.max(s, axis=-1, keepdims=True)), 0.0)
            den = jnp.sum(e, axis=-1, keepdims=True)
            p = e / jnp.where(den > 0.0, den, 1.0)
            psum = psum + p
            outs.append(_dot(p.astype(CDT), vc))
        imp = _split_dot(psum, ov)
        imp = jnp.where(jl > blk, NEG_INF, jnp.where(forced, BIG, imp))
        sb = jnp.full((tq, LANES), NEG, F32)
        for _ in range(n_topk):
            mx = jnp.max(imp, axis=-1, keepdims=True)
            idx = jnp.min(jnp.where(imp == mx, jl, float(LANES)), axis=-1, keepdims=True)
            hit = jl == idx
            sb = jnp.where(hit, 0.0, sb)
            imp = jnp.where(hit, -jnp.inf, imp)
        bias.append(sb)
    sb_ref[0] = jnp.where(low, bias[1], bias[0]).astype(sb_ref.dtype)
    for blk_i in range(NSA_HEADS // 2):
        oc_ref[0, :, blk_i * LANES:(blk_i + 1) * LANES] = jnp.where(
            low, outs[2 * blk_i], outs[2 * blk_i + 1]).astype(oc_ref.dtype)


def _nsa_cmp_select(proj3, kvc, n_topk):
    b, s, _ = proj3.shape
    tq = min(256, s)
    nblk = kvc.shape[1]
    return pl.pallas_call(
        functools.partial(_nsa_cmp_kernel, tq=tq, n_topk=n_topk),
        grid=(b, s // tq),
        in_specs=[
            pl.BlockSpec(memory_space=pltpu.SMEM),
            pl.BlockSpec((1, tq, 4 * LANES), lambda bi, qi: (bi, qi, PB_DQ // 4)),
            pl.BlockSpec((1, nblk, kvc.shape[2]), lambda bi, qi: (bi, 0, 0)),
        ],
        out_specs=[
            pl.BlockSpec((1, tq, NSA_HEADS * NSA_DH), lambda bi, qi: (bi, qi, 0)),
            pl.BlockSpec((1, tq, LANES), lambda bi, qi: (bi, qi, 0)),
        ],
        out_shape=[jax.ShapeDtypeStruct((b, s, NSA_HEADS * NSA_DH), CDT),
                   jax.ShapeDtypeStruct((b, s, LANES), CDT)],
        compiler_params=_cparams(("parallel", "parallel")),
        name="nsa_cmp_select",
    )(jnp.asarray(_alibi_slopes(NSA_HEADS)), proj3, kvc)


def _compact_heads(heads, mine, low):
    both = [jnp.where(mine, a, pltpu.roll(a, NSA_DH, axis=1)) for a in heads]
    out = [jnp.where(low, both[2 * jj], both[2 * jj + 1]) for jj in range(NSA_HPG // 2)]
    return jnp.concatenate(out, axis=1)


def _nsa_win_kernel(slopes_ref, q_ref, kp_ref, kc_ref, vp_ref, vc_ref, o_ref, *, tq):
    g = pl.program_id(1)
    qi = pl.program_id(2)
    lane = lax.broadcasted_iota(jnp.int32, (tq, LANES), 1)
    low = lane < NSA_DH
    mine = (lane >> HALF_SHIFT) == g
    k = jnp.concatenate([kp_ref[0], kc_ref[0]], axis=0)
    v = jnp.concatenate([vp_ref[0], vc_ref[0]], axis=0)
    r = lax.broadcasted_iota(jnp.int32, (tq, 2 * tq), 0)
    c = lax.broadcasted_iota(jnp.int32, (tq, 2 * tq), 1)
    disti = r + tq - c
    valid = (disti >= 0) & (disti < WINDOW) & ((c >= tq) | (qi > 0))
    dist = disti.astype(F32)
    heads = []
    for j in range(NSA_HPG):
        qb = q_ref[0, :, j * LANES:(j + 1) * LANES]
        qm = jnp.where(mine, qb, jnp.zeros_like(qb))
        s = _dot_nt(qm, k) - slopes_ref[g * NSA_HPG + j] * dist
        s = jnp.where(valid, s, NEG)
        e = jnp.exp2(s - jnp.max(s, axis=-1, keepdims=True))
        p = e / jnp.sum(e, axis=-1, keepdims=True)
        heads.append(_dot(p.astype(CDT), v))
    o_ref[0] = _compact_heads(heads, mine, low).astype(o_ref.dtype)


def _nsa_window(proj3):
    b, s, _ = proj3.shape
    tq = WINDOW
    return pl.pallas_call(
        functools.partial(_nsa_win_kernel, tq=tq),
        grid=(b, NSA_GROUPS, s // tq),
        in_specs=[
            pl.BlockSpec(memory_space=pltpu.SMEM),
            pl.BlockSpec((1, tq, 4 * LANES), lambda bi, g, qi: (bi, qi, PB_DQ // 4)),
            pl.BlockSpec((1, tq, LANES), lambda bi, g, qi: (bi, jnp.maximum(qi - 1, 0), PB_WIN_K)),
            pl.BlockSpec((1, tq, LANES), lambda bi, g, qi: (bi, qi, PB_WIN_K)),
            pl.BlockSpec((1, tq, LANES), lambda bi, g, qi: (bi, jnp.maximum(qi - 1, 0), PB_WIN_V)),
            pl.BlockSpec((1, tq, LANES), lambda bi, g, qi: (bi, qi, PB_WIN_V)),
        ],
        out_specs=pl.BlockSpec((1, tq, NSA_HPG * NSA_DH), lambda bi, g, qi: (bi, qi, g)),
        out_shape=jax.ShapeDtypeStruct((b, s, NSA_HEADS * NSA_DH), CDT),
        compiler_params=_cparams(("parallel", "parallel", "parallel")),
        name="nsa_window",
    )(jnp.asarray(_alibi_slopes(NSA_HEADS)), proj3, proj3, proj3, proj3, proj3)


def _nsa_sel_kernel(slopes_ref, q_ref, sb_ref, k_ref, v_ref, oc_ref, ow_ref, gl_ref, e_ref, o_ref,
                    qa_ref, m_ref, l_ref, acc_ref, *, tq):
    g = pl.program_id(1)
    qi = pl.program_id(2)
    lane = lax.broadcasted_iota(jnp.int32, (tq, LANES), 1)
    low = lane < NSA_DH
    mine = (lane >> HALF_SHIFT) == g
    sb = sb_ref[0]
    for j in range(NSA_HPG):
        qa_ref[j * tq:(j + 1) * tq] = jnp.where(mine, q_ref[0, :, j * LANES:(j + 1) * LANES], sb)
    _flash_init(m_ref, l_ref, acc_ref)
    col = lax.broadcasted_iota(jnp.int32, (1, tq), 1).astype(F32)
    jl = lane & (NSA_DH - 1)
    krow = lax.broadcasted_iota(jnp.int32, (tq, LANES), 0)

    def tile(ki, masked):
        k0 = pl.multiple_of(ki * tq, tq)
        k = k_ref[0, pl.ds(k0, tq), :]
        v = v_ref[0, pl.ds(k0, tq), :]
        onehot = jnp.where(((k0 + krow) >> SLC_SHIFT) == jl, 1.0, 0.0).astype(k.dtype)
        s_all = _dot_nt(qa_ref[...], jnp.where(mine, k, onehot))
        rel = ((ki - qi) * tq).astype(F32)
        for j in range(NSA_HPG):
            rows = slice(j * tq, (j + 1) * tq)
            s = s_all[rows] + slopes_ref[g * NSA_HPG + j] * (col + rel)
            if masked:
                r = lax.broadcasted_iota(jnp.int32, s.shape, 0)
                c = lax.broadcasted_iota(jnp.int32, s.shape, 1)
                s = jnp.where(c <= r, s, NEG)
            _flash_update(s, v, m_ref, l_ref, acc_ref, rows=rows)

    _causal_loop(qi, tile)

    o = acc_ref[...] / l_ref[...]
    o_s = _compact_heads([o[j * tq:(j + 1) * tq] for j in range(NSA_HPG)], mine, low)
    gates = _split_dot(jax.nn.sigmoid(gl_ref[0]), e_ref[0])
    w = NSA_HPG * NSA_DH
    y = (gates[:, 0:w] * oc_ref[0].astype(F32) + gates[:, w:2 * w] * o_s
         + gates[:, 2 * w:3 * w] * ow_ref[0].astype(F32))
    o_ref[0] = y.astype(o_ref.dtype)


def _nsa_selected(proj3, sbias, o_c, o_w, small3, expand):
    b, s, _ = proj3.shape
    tq = min(256, s)
    w = NSA_HPG * NSA_DH
    return pl.pallas_call(
        functools.partial(_nsa_sel_kernel, tq=tq),
        grid=(b, NSA_GROUPS, s // tq),
        in_specs=[
            pl.BlockSpec(memory_space=pltpu.SMEM),
            pl.BlockSpec((1, tq, 4 * LANES), lambda bi, g, qi: (bi, qi, PB_DQ // 4)),
            pl.BlockSpec((1, tq, LANES), lambda bi, g, qi: (bi, qi, 0)),
            pl.BlockSpec((1, s, LANES), lambda bi, g, qi: (bi, 0, PB_SEL_K)),
            pl.BlockSpec((1, s, LANES), lambda bi, g, qi: (bi, 0, PB_SEL_V)),
            pl.BlockSpec((1, tq, w), lambda bi, g, qi: (bi, qi, g)),
            pl.BlockSpec((1, tq, w), lambda bi, g, qi: (bi, qi, g)),
            pl.BlockSpec((1, tq, LANES), lambda bi, g, qi: (bi, qi, 0)),
            pl.BlockSpec((1, LANES, 3 * w), lambda bi, g, qi: (g, 0, 0)),
        ],
        out_specs=pl.BlockSpec((1, tq, w), lambda bi, g, qi: (bi, qi, g)),
        out_shape=jax.ShapeDtypeStruct((b, s, NSA_HEADS * NSA_DH), CDT),
        scratch_shapes=[
            pltpu.VMEM((NSA_HPG * tq, LANES), CDT),
            pltpu.VMEM((NSA_HPG * tq, LANES), F32),
            pltpu.VMEM((NSA_HPG * tq, LANES), F32),
            pltpu.VMEM((NSA_HPG * tq, LANES), F32),
        ],
        compiler_params=_cparams(("parallel", "parallel", "arbitrary")),
        name="nsa_selected",
    )(jnp.asarray(_alibi_slopes(NSA_HEADS)), proj3, sbias, proj3, proj3, o_c, o_w, small3, expand)


def _merge_kernel(ya_ref, yb_ref, yc_ref, yd_ref, ga_ref, gb_ref, gc_ref, gd_ref, wb_ref, wo_ref, x_ref, o_ref):
    merged = None
    for n, (y_ref, g_ref) in enumerate(((ya_ref, ga_ref), (yb_ref, gb_ref), (yc_ref, gc_ref), (yd_ref, gd_ref))):
        t = jax.nn.sigmoid(g_ref[...].astype(F32)) * _dot(y_ref[...], wb_ref[n])
        merged = t if merged is None else merged + t
    o_ref[...] = x_ref[...] + _dot(merged.astype(CDT), wo_ref[...])


def _merge(ys, proj2, wb, wo, x2):
    t, d = x2.shape
    tm = min(512, t)
    gate_blk = PB_GATE * LANES // d
    yspec = pl.BlockSpec((tm, BRANCH_WIDTH), lambda i: (i, 0))
    gspecs = [pl.BlockSpec((tm, d), functools.partial(lambda i, n: (i, gate_blk + n), n=n)) for n in range(N_BRANCH)]
    return pl.pallas_call(
        _merge_kernel,
        grid=(t // tm,),
        in_specs=[yspec] * N_BRANCH + gspecs + [
            pl.BlockSpec((N_BRANCH, BRANCH_WIDTH, d), lambda i: (0, 0, 0)),
            pl.BlockSpec((d, d), lambda i: (0, 0)),
            pl.BlockSpec((tm, d), lambda i: (i, 0)),
        ],
        out_specs=pl.BlockSpec((tm, d), lambda i: (i, 0)),
        out_shape=jax.ShapeDtypeStruct((t, d), F32),
        compiler_params=_cparams(("parallel",)),
        name="merge",
    )(*ys, proj2, proj2, proj2, proj2, wb, wo, x2)


HALO = 16


def _ffn_kernel(x_ref, xh_ref, g_ref, wu_ref, cw_ref, cb_ref, wd_ref, gf_ref, o_ref, he_ref, ua_ref, ug_ref,
                *, tm, fc, final):
    i = pl.program_id(1)
    x = x_ref[0]
    g = g_ref[...]
    xh = xh_ref[0] * (i > 0).astype(F32)
    he_ref[0:HALO] = _rms(xh, g).astype(CDT)
    he_ref[HALO:HALO + tm] = _rms(x, g).astype(CDT)
    he = he_ref[...]
    acc = jnp.zeros((tm, x.shape[1]), F32)
    for c in range(D_FF // fc):
        outs = []
        for half, u_ref in ((0, ua_ref), (1, ug_ref)):
            lo = half * D_FF + c * fc
            u_ref[...] = _dot(he, wu_ref[:, lo:lo + fc])
            conv = cb_ref[:, lo:lo + fc]
            for kk in range(CONV_WIDTH):
                off = HALO - (CONV_WIDTH - 1) + kk
                conv = conv + cw_ref[kk:kk + 1, lo:lo + fc] * u_ref[off:off + tm, :]
            outs.append(conv)
        a, gg = outs
        act = (a * jax.nn.sigmoid(a) * gg).astype(CDT)
        acc = acc + _dot(act, wd_ref[c * fc:(c + 1) * fc, :])
    y = x + acc
    if final:
        y = _rms(y, gf_ref[...])
    o_ref[0] = y


def _ffn(x3, g, wu, cw, cb, wd, gf, final):
    b, s, d = x3.shape
    tm = min(512, s)
    fc = 256
    const = lambda shape: pl.BlockSpec(shape, lambda bi, i: (0,) * len(shape))
    return pl.pallas_call(
        functools.partial(_ffn_kernel, tm=tm, fc=fc, final=final),
        grid=(b, s // tm),
        in_specs=[
            pl.BlockSpec((1, tm, d), lambda bi, i: (bi, i, 0)),
            pl.BlockSpec((1, HALO, d), lambda bi, i: (bi, jnp.maximum(i * (tm // HALO) - 1, 0), 0)),
            const((1, d)), const((d, 2 * D_FF)), const((CONV_WIDTH, 2 * D_FF)), const((1, 2 * D_FF)),
            const((D_FF, d)), const((1, d)),
        ],
        out_specs=pl.BlockSpec((1, tm, d), lambda bi, i: (bi, i, 0)),
        out_shape=jax.ShapeDtypeStruct((b, s, d), F32),
        scratch_shapes=[pltpu.VMEM((tm + HALO, d), CDT), pltpu.VMEM((tm + HALO, fc), F32),
                        pltpu.VMEM((tm + HALO, fc), F32)],
        compiler_params=_cparams(("parallel", "arbitrary")),
        name="conv_glu_mlp",
    )(x3, x3, g.reshape(1, d), wu, cw, cb.reshape(1, -1), wd, gf.reshape(1, d))


def _prep_w_in(w):
    widths = (512, 512, 512, MLA_Q_LORA, MLA_KV_LORA, MLA_ROPE, 512, 512, 512, FOX_HEADS,
              512, 768, 3 * NSA_HEADS, N_BRANCH * D_MODEL)
    offs = np.cumsum((0,) + widths)
    (a_q, a_k, a_v, b_cq, b_ckv, b_kr, c_q, c_k, c_v, c_f, d_q, d_kv, d_g, gate) = [
        w[:, offs[i]:offs[i + 1]] for i in range(len(widths))]
    d = w.shape[0]
    d_q = d_q.reshape(d, NSA_GROUPS, NSA_HPG, NSA_DH).transpose(0, 2, 1, 3).reshape(d, 512)
    half = MLA_ROPE // 2
    kr_swap = jnp.concatenate([-b_kr[:, half:], b_kr[:, :half]], axis=1)
    z64 = jnp.zeros((d, LANES - MLA_ROPE), w.dtype)
    big = jnp.concatenate([
        a_q * (LOG2E * DIFF_DH ** -0.5), a_k, a_v,
        c_q * (LOG2E * FOX_DH ** -0.5), c_k, c_v,
        d_q * (LOG2E * NSA_DH ** -0.5), d_kv,
        b_cq, b_ckv, b_kr, z64, kr_swap, z64,
        gate], axis=1)
    small = jnp.concatenate([c_f, d_g, jnp.zeros((d, LANES - FOX_HEADS - 3 * NSA_HEADS), w.dtype)], axis=1)
    return big.astype(CDT), small.astype(CDT)


def _prep_mla(w_uq, w_ukv):
    r = w_uq.shape[0]
    hw = 2 * LANES
    half = MLA_ROPE // 2
    scale = LOG2E * (MLA_NOPE + MLA_ROPE) ** -0.5
    wq = (w_uq * scale).reshape(r, MLA_HEADS, MLA_NOPE + MLA_ROPE)
    nope, t1, t2 = wq[..., :MLA_NOPE], wq[..., MLA_NOPE:MLA_NOPE + half], wq[..., MLA_NOPE + half:]
    zpad = jnp.zeros((r, MLA_HEADS, hw - MLA_NOPE - MLA_ROPE), w_uq.dtype)
    wqm = jnp.concatenate([nope, t1, t2, zpad], axis=-1).reshape(r, MLA_HEADS * hw)
    wqs = jnp.concatenate([jnp.zeros_like(nope), -t2, t1, zpad], axis=-1).reshape(r, MLA_HEADS * hw)
    wkv = w_ukv.reshape(w_ukv.shape[0], MLA_HEADS, MLA_NOPE + MLA_VDIM)
    wk = wkv[..., :MLA_NOPE].reshape(-1, MLA_HEADS * MLA_NOPE)
    wv = wkv[..., MLA_NOPE:].reshape(-1, MLA_HEADS * MLA_VDIM)
    return wqm.astype(CDT), wqs.astype(CDT), wk.astype(CDT), wv.astype(CDT)


def _rope_tables(s):
    half = MLA_ROPE // 2
    inv_freq = ROPE_THETA ** (-jnp.arange(0, MLA_ROPE, 2, dtype=F32) / MLA_ROPE)
    ang = jnp.arange(s, dtype=F32)[:, None] * inv_freq[None, :]
    cos, sin = jnp.cos(ang), jnp.sin(ang)
    z = jnp.zeros((s, LANES - MLA_ROPE), F32)
    cosk = jnp.concatenate([cos, cos, z], axis=1)
    sink = jnp.concatenate([sin, sin, z], axis=1)
    cosq = jnp.concatenate([jnp.ones((s, MLA_NOPE), F32), cosk], axis=1)
    sinq = jnp.concatenate([jnp.zeros((s, MLA_NOPE), F32), sink], axis=1)
    return cosq, sinq, cosk, sink


def _prep_compress(pe, w1, w2):
    eye2 = jnp.eye(2, dtype=F32)
    w1r = w1.reshape(2, CMP_LEN, NSA_DH, CMP_HIDDEN)

    def expand(wpart):
        t = jnp.einsum('kpdh,kK,gG->pkgdKGh', wpart, eye2, eye2)
        return t.reshape(CMP_STRIDE * 4 * NSA_DH, 4 * CMP_HIDDEN)

    w1a, w1b = expand(w1r[:, :CMP_STRIDE]), expand(w1r[:, CMP_STRIDE:])

    def pe_row(p):
        t = jnp.broadcast_to(p.transpose(1, 0, 2)[:, :, None, :], (CMP_STRIDE, 2, NSA_GROUPS, NSA_DH))
        return jnp.pad(t.reshape(1, -1), ((0, 7), (0, 0)))

    pea, peb = pe_row(pe[:, :CMP_STRIDE]), pe_row(pe[:, CMP_STRIDE:])
    w2b = jnp.einsum('khd,kK,gG,u->kghKGud', w2, eye2, eye2, jnp.ones((2,), F32))
    w2b = w2b.reshape(4 * CMP_HIDDEN, 4 * 2 * NSA_DH)
    return w1a.astype(CDT), w1b.astype(CDT), pea.astype(CDT), peb.astype(CDT), w2b.astype(CDT)


def _gate_expand():
    e = np.zeros((NSA_GROUPS, LANES, 3, NSA_HPG, NSA_DH), np.float32)
    for g in range(NSA_GROUPS):
        for j in range(NSA_HPG):
            for br in range(3):
                e[g, SMALL_G + (g * NSA_HPG + j) * 3 + br, br, j, :] = 1.0
    return jnp.asarray(e.reshape(NSA_GROUPS, LANES, 3 * NSA_HPG * NSA_DH)).astype(CDT)


def _token_mixers(x3, l, norm_mix, w_in, diff_lambda, diff_subln, mla_norm_q, mla_w_uq, mla_norm_kv, mla_w_ukv,
                  fox_b_f, nsa_cmp_pe, nsa_cmp_w1, nsa_cmp_w2, w_branch, w_out, rope_tabs):
    b, s, d = x3.shape
    t = b * s
    x2 = x3.reshape(t, d)
    w_big, w_small = _prep_w_in(w_in)
    proj, small = _in_proj(x2, norm_mix, w_big, w_small)
    proj3 = proj.reshape(b, s, N_PROJ)
    small3 = small.reshape(b, s, LANES)

    lam_init = 0.8 - 0.6 * math.exp(-0.3 * l)
    y_a = _diff_attention(proj3, diff_lambda, diff_subln, lam_init)

    wqm, wqs, wk, wv = _prep_mla(mla_w_uq, mla_w_ukv)
    qc, kc, vv = _mla_prep(proj3, mla_norm_q, mla_norm_kv, wqm, wqs, wk, wv, rope_tabs)
    y_b = _mla_attention(qc, kc, vv)

    cf_rows = small3[:, :, SMALL_F:SMALL_F + FOX_HEADS].transpose(0, 2, 1).reshape(b * FOX_HEADS, s)
    bias_rows = jnp.tile(fox_b_f.astype(F32), b).reshape(b * FOX_HEADS, 1)
    c4 = _fox_cumsum(cf_rows, bias_rows)
    y_c = _fox_attention(proj3, c4)

    w1a, w1b, pea, peb, w2b = _prep_compress(nsa_cmp_pe, nsa_cmp_w1, nsa_cmp_w2)
    xc = proj3[:, :, PB_CMP_K * LANES:(PB_CMP_V + 1) * LANES].reshape(b, s // CMP_STRIDE, CMP_STRIDE * 2 * LANES)
    kvc = _nsa_compress(xc, w1a, w1b, pea, peb, w2b)
    n_topk = min(SLC_TOPK, s // SLC_LEN)
    o_c, sbias = _nsa_cmp_select(proj3, kvc, n_topk)
    o_w = _nsa_window(proj3)
    y_d = _nsa_selected(proj3, sbias, o_c, o_w, small3, _gate_expand())

    ys = [y.reshape(t, BRANCH_WIDTH) for y in (y_a, y_b, y_c, y_d)]
    return _merge(ys, proj, w_branch.astype(CDT), w_out.astype(CDT), x2).reshape(b, s, d)


def kernel(x, norm_mix, w_in, diff_lambda, diff_subln, mla_norm_q, mla_w_uq, mla_norm_kv, mla_w_ukv, fox_b_f,
           nsa_cmp_pe, nsa_cmp_w1, nsa_cmp_w2, w_branch, w_out, norm_ffn, w_up, conv_w, conv_b, w_down, norm_final):
    depth = w_in.shape[0]
    s = x.shape[1]
    rope_tabs = _rope_tables(s)
    for l in range(depth):
        x = _token_mixers(x, l, norm_mix[l], w_in[l], diff_lambda[l], diff_subln[l], mla_norm_q[l], mla_w_uq[l],
                          mla_norm_kv[l], mla_w_ukv[l], fox_b_f[l], nsa_cmp_pe[l], nsa_cmp_w1[l], nsa_cmp_w2[l],
                          w_branch[l], w_out[l], rope_tabs)
        x = _ffn(x, norm_ffn[l], w_up[l].astype(CDT), conv_w[l], conv_b[l], w_down[l].astype(CDT), norm_final,
                 final=(l == depth - 1))
    return x
```

```python
import functools
import math

import numpy as np
import jax
import jax.numpy as jnp
from jax import lax
from jax.experimental import pallas as pl
from jax.experimental.pallas import tpu as pltpu

F32 = jnp.float32
CDT = jnp.bfloat16

NEG = -1e30
NEG_INF = -1e30
BIG = 1e9
NORM_EPS = 1e-6
LOG2E = 1.4426950408889634
LANES = 128

D_MODEL = 1024
DIFF_HEADS, DIFF_DH = 4, 64
MLA_HEADS, MLA_NOPE, MLA_ROPE, MLA_VDIM = 4, 128, 64, 128
MLA_Q_LORA, MLA_KV_LORA = 256, 256
ROPE_THETA = 10000.0
FOX_HEADS, FOX_DH = 4, 128
NSA_HEADS, NSA_GROUPS, NSA_DH = 8, 2, 64
NSA_HPG = NSA_HEADS // NSA_GROUPS
CMP_STRIDE = 16
CMP_LEN = 2 * CMP_STRIDE
CMP_HIDDEN = 128
SLC_LEN = 64
SLC_SHIFT = 6
HALF_SHIFT = 6
SLC_TOPK = 8
WINDOW = 256
N_BRANCH = 4
BRANCH_WIDTH = 512
D_FF = 2816
CONV_WIDTH = 3

PB_AQ, PB_AK, PB_AV = 0, 4, 8
PB_CQ, PB_CK, PB_CV = 12, 16, 20
PB_DQ = 24
PB_CMP_K, PB_CMP_V, PB_SEL_K, PB_SEL_V, PB_WIN_K, PB_WIN_V = 28, 29, 30, 31, 32, 33
PB_BCQ, PB_BCKV, PB_BKR, PB_BKRS = 34, 36, 38, 39
PB_GATE = 40
N_PROJ = 72 * LANES
SMALL_F, SMALL_G = 0, 4

VMEM_LIMIT = 56 * 1024 * 1024


def _cparams(sem):
    return pltpu.CompilerParams(dimension_semantics=sem, vmem_limit_bytes=VMEM_LIMIT)


def _rms(xf, g):
    return xf * lax.rsqrt(jnp.mean(xf * xf, axis=-1, keepdims=True) + NORM_EPS) * g


def _dot(a, b):
    return jnp.dot(a, b, preferred_element_type=F32)


def _dot_nt(a, b):
    return lax.dot_general(a, b, (((1,), (1,)), ((), ())), preferred_element_type=F32)


def _split_dot(a, b):
    hi = a.astype(CDT)
    lo = (a - hi.astype(F32)).astype(CDT)
    return _dot(hi, b) + _dot(lo, b)


def _alibi_slopes(n):
    return (LOG2E * np.exp2(-8.0 * np.arange(1, n + 1) / n)).astype(np.float32)


def _inproj_kernel(x_ref, g_ref, w_ref, ws_ref, o_ref, os_ref, h_ref):
    @pl.when(pl.program_id(1) == 0)
    def _():
        h = _rms(x_ref[...], g_ref[...]).astype(CDT)
        h_ref[...] = h
        os_ref[...] = _dot(h, ws_ref[...])

    o_ref[...] = _dot(h_ref[...], w_ref[...]).astype(o_ref.dtype)


def _in_proj(x2, g, w, ws):
    t, d = x2.shape
    n = w.shape[1]
    tm = min(1024, t)
    tn = 1024
    return pl.pallas_call(
        _inproj_kernel,
        grid=(t // tm, n // tn),
        in_specs=[
            pl.BlockSpec((tm, d), lambda i, j: (i, 0)),
            pl.BlockSpec((1, d), lambda i, j: (0, 0)),
            pl.BlockSpec((d, tn), lambda i, j: (0, j)),
            pl.BlockSpec((d, LANES), lambda i, j: (0, 0)),
        ],
        out_specs=[
            pl.BlockSpec((tm, tn), lambda i, j: (i, j)),
            pl.BlockSpec((tm, LANES), lambda i, j: (i, 0)),
        ],
        out_shape=[jax.ShapeDtypeStruct((t, n), CDT), jax.ShapeDtypeStruct((t, LANES), F32)],
        scratch_shapes=[pltpu.VMEM((tm, d), CDT)],
        compiler_params=_cparams(("parallel", "arbitrary")),
        name="in_proj",
    )(x2, g.reshape(1, d), w, ws)


def _fox_cumsum_kernel(cf_ref, bf_ref, o_ref):
    rows, s = cf_ref.shape
    lane = lax.broadcasted_iota(jnp.int32, (rows, LANES), 1)
    carry = jnp.zeros((rows, 1), F32)
    for c in range(s // LANES):
        z = cf_ref[:, c * LANES:(c + 1) * LANES] + bf_ref[...]
        xs = jnp.minimum(z, 0.0) - jnp.log1p(jnp.exp(-jnp.abs(z)))
        d = 1
        while d < LANES:
            xs = xs + jnp.where(lane >= d, pltpu.roll(xs, d, axis=1), 0.0)
            d *= 2
        xs = xs + carry
        o_ref[:, c * LANES:(c + 1) * LANES] = xs
        carry = xs[:, LANES - 1:LANES]


def _fox_cumsum(cf_rows, bias_rows):
    return pl.pallas_call(
        _fox_cumsum_kernel,
        out_shape=jax.ShapeDtypeStruct(cf_rows.shape, F32),
        name="fox_cumsum",
    )(cf_rows, bias_rows)


def _flash_scratch(rows, tk):
    return [pltpu.VMEM((rows, LANES), F32), pltpu.VMEM((rows, 2 * LANES), F32),
            pltpu.VMEM((rows, tk), F32), pltpu.VMEM((rows, tk), F32),
            pltpu.VMEM((rows, LANES), F32), pltpu.VMEM((rows, LANES), F32)]


def _flash_reset(m_ref, acc_ref):
    m_ref[...] = jnp.full(m_ref.shape, NEG, F32)
    acc_ref[...] = jnp.zeros(acc_ref.shape, F32)


def _row_max(s):
    return jnp.broadcast_to(jnp.max(s, axis=-1, keepdims=True), (s.shape[0], LANES))


def _put_logits(buf, s, rows=slice(None)):
    buf[0][rows] = s
    buf[1][rows] = _row_max(s)


def _with_ones(v):
    return jnp.concatenate([v, jnp.ones((v.shape[0], LANES), v.dtype)], axis=1)


def _flash_consume(buf, v, m_ref, acc_ref, masked, tq):
    s = buf[0][...]
    m_cur = buf[1][...]
    if masked:
        r = lax.broadcasted_iota(jnp.int32, s.shape, 0) & (tq - 1)
        c = lax.broadcasted_iota(jnp.int32, s.shape, 1)
        s = jnp.where(c <= r, s, NEG)
        m_cur = _row_max(s)
    m_old = m_ref[...]
    m_new = jnp.maximum(m_old, m_cur)
    alpha = jnp.exp2(m_old - m_new)
    p = jnp.exp2(s - jnp.tile(m_new, (1, s.shape[1] // LANES))).astype(CDT)
    acc_ref[...] = jnp.tile(alpha, (1, 2)) * acc_ref[...] + _dot(p, _with_ones(v))
    m_ref[...] = m_new


def _flash_result(acc):
    return acc[:, :LANES] / acc[:, LANES:]


def _tile_pipeline(n, tile_of, produce, consume, buf_a, buf_b):
    produce(buf_a, tile_of(0))

    def pair(j, c):
        t = 2 * j
        produce(buf_b, tile_of(t + 1))
        consume(buf_a, tile_of(t), False)
        produce(buf_a, tile_of(t + 2))
        consume(buf_b, tile_of(t + 1), False)
        return c

    lax.fori_loop(0, n // 2, pair, 0)

    @pl.when(n % 2 == 0)
    def _():
        consume(buf_a, tile_of(n), True)

    @pl.when(n % 2 == 1)
    def _():
        produce(buf_b, tile_of(n))
        consume(buf_a, tile_of(n - 1), False)
        consume(buf_b, tile_of(n), True)


def _diff_attn_kernel(slopes_ref, lam_ref, g_ref, q_ref, k_ref, v_ref, o_ref,
                      qq_ref, m_ref, acc_ref, sa_ref, sb_ref, ma_ref, mb_ref, *, tq, lam_init):
    h = pl.program_id(1)
    qi = pl.program_id(2)
    slope = slopes_ref[h]
    q = q_ref[0]
    lane = lax.broadcasted_iota(jnp.int32, q.shape, 1)
    zero = jnp.zeros_like(q)
    qq_ref[0:tq] = jnp.where(lane < DIFF_DH, q, zero)
    qq_ref[tq:2 * tq] = jnp.where(lane >= DIFF_DH, q, zero)
    _flash_reset(m_ref, acc_ref)
    col = lax.broadcasted_iota(jnp.int32, (1, tq), 1).astype(F32)

    def produce(buf, ki):
        k0 = pl.multiple_of(ki * tq, tq)
        s = _dot_nt(qq_ref[...], k_ref[0, pl.ds(k0, tq), :])
        _put_logits(buf, s + slope * (col + ((ki - qi) * tq).astype(F32)))

    def consume(buf, ki, masked):
        k0 = pl.multiple_of(ki * tq, tq)
        _flash_consume(buf, v_ref[0, pl.ds(k0, tq), :], m_ref, acc_ref, masked, tq)

    _tile_pipeline(qi, lambda t: t, produce, consume, (sa_ref, ma_ref), (sb_ref, mb_ref))

    lf = lam_ref[...]
    lam = (jnp.exp(jnp.sum(lf[0:1] * lf[1:2], axis=-1, keepdims=True))
           - jnp.exp(jnp.sum(lf[2:3] * lf[3:4], axis=-1, keepdims=True)) + lam_init)
    o = _flash_result(acc_ref[...])
    d = o[0:tq] - lam * o[tq:2 * tq]
    o_ref[0] = (_rms(d, g_ref[...]) * (1.0 - lam_init)).astype(o_ref.dtype)


def _diff_attention(proj3, diff_lambda, subln, lam_init):
    b, s, _ = proj3.shape
    tq = min(512, s)
    dv = 2 * DIFF_DH
    kern = functools.partial(_diff_attn_kernel, tq=tq, lam_init=lam_init)
    return pl.pallas_call(
        kern,
        grid=(b, DIFF_HEADS, s // tq),
        in_specs=[
            pl.BlockSpec(memory_space=pltpu.SMEM),
            pl.BlockSpec((4, DIFF_DH), lambda bi, h, qi: (0, 0)),
            pl.BlockSpec((1, dv), lambda bi, h, qi: (0, 0)),
            pl.BlockSpec((1, tq, LANES), lambda bi, h, qi: (bi, qi, PB_AQ + h)),
            pl.BlockSpec((1, s, LANES), lambda bi, h, qi: (bi, 0, PB_AK + h)),
            pl.BlockSpec((1, s, LANES), lambda bi, h, qi: (bi, 0, PB_AV + h)),
        ],
        out_specs=pl.BlockSpec((1, tq, dv), lambda bi, h, qi: (bi, qi, h)),
        out_shape=jax.ShapeDtypeStruct((b, s, DIFF_HEADS * dv), CDT),
        scratch_shapes=[pltpu.VMEM((2 * tq, LANES), CDT)] + _flash_scratch(2 * tq, tq),
        compiler_params=_cparams(("parallel", "parallel", "arbitrary")),
        name="diff_attention",
    )(jnp.asarray(_alibi_slopes(DIFF_HEADS)), diff_lambda, subln.reshape(1, dv), proj3, proj3, proj3)


def _mla_prep_kernel(cq_ref, ckv_ref, kr_ref, krs_ref, gq_ref, gkv_ref, wqm_ref, wqs_ref, wk_ref, wv_ref,
                     cosq_ref, sinq_ref, cosk_ref, sink_ref, q_ref, k_ref, v_ref):
    hq = _rms(cq_ref[0].astype(F32), gq_ref[...]).astype(CDT)
    qm = _dot(hq, wqm_ref[...])
    qs = _dot(hq, wqs_ref[...])
    cosq, sinq = cosq_ref[...], sinq_ref[...]
    hw = 2 * LANES
    for h in range(MLA_HEADS):
        sl = slice(h * hw, (h + 1) * hw)
        q_ref[0, :, sl] = (qm[:, sl] * cosq + qs[:, sl] * sinq).astype(q_ref.dtype)
    hkv = _rms(ckv_ref[0].astype(F32), gkv_ref[...]).astype(CDT)
    kn = _dot(hkv, wk_ref[...])
    v_ref[0] = _dot(hkv, wv_ref[...]).astype(v_ref.dtype)
    kpe = (kr_ref[0].astype(F32) * cosk_ref[...] + krs_ref[0].astype(F32) * sink_ref[...]).astype(k_ref.dtype)
    for h in range(MLA_HEADS):
        k_ref[0, :, h * hw:h * hw + LANES] = kn[:, h * LANES:(h + 1) * LANES].astype(k_ref.dtype)
        k_ref[0, :, h * hw + LANES:(h + 1) * hw] = kpe


def _mla_prep(proj3, gq, gkv, wqm, wqs, wk, wv, tabs):
    b, s, _ = proj3.shape
    tm = min(512, s)
    hw = 2 * LANES
    cosq, sinq, cosk, sink = tabs
    const = lambda shape: pl.BlockSpec(shape, lambda bi, i: (0,) * len(shape))
    return pl.pallas_call(
        _mla_prep_kernel,
        grid=(b, s // tm),
        in_specs=[
            pl.BlockSpec((1, tm, MLA_Q_LORA), lambda bi, i: (bi, i, PB_BCQ // 2)),
            pl.BlockSpec((1, tm, MLA_KV_LORA), lambda bi, i: (bi, i, PB_BCKV // 2)),
            pl.BlockSpec((1, tm, LANES), lambda bi, i: (bi, i, PB_BKR)),
            pl.BlockSpec((1, tm, LANES), lambda bi, i: (bi, i, PB_BKRS)),
            const((1, MLA_Q_LORA)), const((1, MLA_KV_LORA)),
            const((MLA_Q_LORA, MLA_HEADS * hw)), const((MLA_Q_LORA, MLA_HEADS * hw)),
            const((MLA_KV_LORA, MLA_HEADS * MLA_NOPE)), const((MLA_KV_LORA, MLA_HEADS * MLA_VDIM)),
            pl.BlockSpec((tm, hw), lambda bi, i: (i, 0)), pl.BlockSpec((tm, hw), lambda bi, i: (i, 0)),
            pl.BlockSpec((tm, LANES), lambda bi, i: (i, 0)), pl.BlockSpec((tm, LANES), lambda bi, i: (i, 0)),
        ],
        out_specs=[
            pl.BlockSpec((1, tm, MLA_HEADS * hw), lambda bi, i: (bi, i, 0)),
            pl.BlockSpec((1, tm, MLA_HEADS * hw), lambda bi, i: (bi, i, 0)),
            pl.BlockSpec((1, tm, MLA_HEADS * MLA_VDIM), lambda bi, i: (bi, i, 0)),
        ],
        out_shape=[
            jax.ShapeDtypeStruct((b, s, MLA_HEADS * hw), CDT),
            jax.ShapeDtypeStruct((b, s, MLA_HEADS * hw), CDT),
            jax.ShapeDtypeStruct((b, s, MLA_HEADS * MLA_VDIM), CDT),
        ],
        compiler_params=_cparams(("parallel", "parallel")),
        name="mla_prep",
    )(proj3, proj3, proj3, proj3, gq.reshape(1, -1), gkv.reshape(1, -1), wqm, wqs, wk, wv,
      cosq, sinq, cosk, sink)


def _plain_attn_kernel(q_ref, k_ref, v_ref, o_ref, m_ref, acc_ref, sa_ref, sb_ref, ma_ref, mb_ref, *, tq):
    qi = pl.program_id(2)
    _flash_reset(m_ref, acc_ref)

    def produce(buf, ki):
        k0 = pl.multiple_of(ki * tq, tq)
        _put_logits(buf, _dot_nt(q_ref[0], k_ref[0, pl.ds(k0, tq), :]))

    def consume(buf, ki, masked):
        k0 = pl.multiple_of(ki * tq, tq)
        _flash_consume(buf, v_ref[0, pl.ds(k0, tq), :], m_ref, acc_ref, masked, tq)

    _tile_pipeline(qi, lambda t: t, produce, consume, (sa_ref, ma_ref), (sb_ref, mb_ref))
    o_ref[0] = _flash_result(acc_ref[...]).astype(o_ref.dtype)


def _mla_attention(qc, kc, v):
    b, s, _ = qc.shape
    tq = min(512, s)
    hw = 2 * LANES
    return pl.pallas_call(
        functools.partial(_plain_attn_kernel, tq=tq),
        grid=(b, MLA_HEADS, s // tq),
        in_specs=[
            pl.BlockSpec((1, tq, hw), lambda bi, h, qi: (bi, qi, h)),
            pl.BlockSpec((1, s, hw), lambda bi, h, qi: (bi, 0, h)),
            pl.BlockSpec((1, s, MLA_VDIM), lambda bi, h, qi: (bi, 0, h)),
        ],
        out_specs=pl.BlockSpec((1, tq, MLA_VDIM), lambda bi, h, qi: (bi, qi, h)),
        out_shape=jax.ShapeDtypeStruct((b, s, MLA_HEADS * MLA_VDIM), CDT),
        scratch_shapes=_flash_scratch(tq, tq),
        compiler_params=_cparams(("parallel", "parallel", "arbitrary")),
        name="mla_attention",
    )(qc, kc, v)


def _fox_attn_kernel(c_ref, q_ref, k_ref, v_ref, o_ref, m_ref, acc_ref, sa_ref, sb_ref, ma_ref, mb_ref, *, tq):
    qi = pl.program_id(2)
    _flash_reset(m_ref, acc_ref)
    cbase = c_ref[0, 0, pl.ds(qi, 1), :][:, 0:1]

    def produce(buf, ki):
        k0 = pl.multiple_of(ki * tq, tq)
        s = _dot_nt(q_ref[0], k_ref[0, pl.ds(k0, tq), :])
        _put_logits(buf, s + LOG2E * (cbase - c_ref[0, 0, pl.ds(ki, 1), :]))

    def consume(buf, ki, masked):
        k0 = pl.multiple_of(ki * tq, tq)
        _flash_consume(buf, v_ref[0, pl.ds(k0, tq), :], m_ref, acc_ref, masked, tq)

    _tile_pipeline(qi, lambda t: t, produce, consume, (sa_ref, ma_ref), (sb_ref, mb_ref))
    o_ref[0] = _flash_result(acc_ref[...]).astype(o_ref.dtype)


def _fox_attention(proj3, c4):
    b, s, _ = proj3.shape
    tq = min(512, s)
    nk = s // tq
    return pl.pallas_call(
        functools.partial(_fox_attn_kernel, tq=tq),
        grid=(b, FOX_HEADS, s // tq),
        in_specs=[
            pl.BlockSpec((1, 1, nk, tq), lambda bi, h, qi: (bi, h, 0, 0)),
            pl.BlockSpec((1, tq, FOX_DH), lambda bi, h, qi: (bi, qi, PB_CQ + h)),
            pl.BlockSpec((1, s, FOX_DH), lambda bi, h, qi: (bi, 0, PB_CK + h)),
            pl.BlockSpec((1, s, FOX_DH), lambda bi, h, qi: (bi, 0, PB_CV + h)),
        ],
        out_specs=pl.BlockSpec((1, tq, FOX_DH), lambda bi, h, qi: (bi, qi, h)),
        out_shape=jax.ShapeDtypeStruct((b, s, FOX_HEADS * FOX_DH), CDT),
        scratch_shapes=_flash_scratch(tq, tq),
        compiler_params=_cparams(("parallel", "parallel", "arbitrary")),
        name="fox_attention",
    )(c4.reshape(b, FOX_HEADS, nk, tq), proj3, proj3, proj3)


def _nsa_compress_kernel(x_ref, w1a_ref, w1b_ref, pea_ref, peb_ref, w2_ref, o_ref):
    x = x_ref[0]
    n = x.shape[0]
    pa = _dot(x, w1a_ref[...])
    pb = _dot(x, w1b_ref[...])
    pe = _dot(pea_ref[...], w1a_ref[...]) + _dot(peb_ref[...], w1b_ref[...])
    hid = pa + pltpu.roll(pb, n - 1, axis=0) + pe[0:1]
    act = 0.5 * hid * (1.0 + jnp.tanh(math.sqrt(2.0 / math.pi) * (hid + 0.044715 * hid * hid * hid)))
    o_ref[0] = _dot(act.astype(CDT), w2_ref[...]).astype(o_ref.dtype)


def _nsa_compress(xc, w1a, w1b, pea, peb, w2):
    b, n, kdim = xc.shape
    hdim = w1a.shape[1]
    const = lambda shape: pl.BlockSpec(shape, lambda bi: (0,) * len(shape))
    return pl.pallas_call(
        _nsa_compress_kernel,
        grid=(b,),
        in_specs=[pl.BlockSpec((1, n, kdim), lambda bi: (bi, 0, 0)),
                  const((kdim, hdim)), const((kdim, hdim)), const((8, kdim)), const((8, kdim)),
                  const((hdim, w2.shape[1]))],
        out_specs=pl.BlockSpec((1, n, w2.shape[1]), lambda bi: (bi, 0, 0)),
        out_shape=jax.ShapeDtypeStruct((b, n, w2.shape[1]), CDT),
        compiler_params=_cparams(("parallel",)),
        name="nsa_compress",
    )(xc, w1a, w1b, pea, peb, w2)


def _nsa_cmp_kernel(slopes_ref, q_ref, kv_ref, oc_ref, sb_ref, used_ref, *, tq, n_topk):
    qi = pl.program_id(1)
    nblk = kv_ref.shape[1]
    q0 = qi * tq
    rowpos = q0 + lax.broadcasted_iota(jnp.int32, (tq, 1), 0)
    nidx = lax.broadcasted_iota(jnp.int32, (1, nblk), 1)
    disti = rowpos - (nidx * CMP_STRIDE + CMP_LEN - 1)
    valid = disti >= 0
    dist = disti.astype(F32)
    lane = lax.broadcasted_iota(jnp.int32, (tq, LANES), 1)
    low = lane < NSA_DH
    nn = lax.broadcasted_iota(jnp.int32, (nblk, LANES), 0) * CMP_STRIDE
    jj = (lax.broadcasted_iota(jnp.int32, (nblk, LANES), 1) & (NSA_DH - 1)) * SLC_LEN
    ov = (jnp.maximum(jnp.minimum(nn + CMP_LEN, jj + SLC_LEN) - jnp.maximum(nn, jj), 0).astype(F32)
          * (1.0 / CMP_LEN)).astype(CDT)
    jl = (lane & (NSA_DH - 1)).astype(F32)
    blk = (rowpos >> SLC_SHIFT).astype(F32)
    forced = (jl == 0.0) | (jl == blk) | (jl == blk - 1.0)
    outs = []
    bias = []
    for g in range(NSA_GROUPS):
        kc = kv_ref[0, :, g * LANES:(g + 1) * LANES]
        vc = kv_ref[0, :, (NSA_GROUPS + g) * LANES:(NSA_GROUPS + g + 1) * LANES]
        psum = jnp.zeros((tq, nblk), F32)
        for j in range(NSA_HPG):
            qb = q_ref[0, :, j * LANES:(j + 1) * LANES]
            qm = jnp.where(low if g == 0 else jnp.logical_not(low), qb, jnp.zeros_like(qb))
            s = _dot_nt(qm, kc) - slopes_ref[g * NSA_HPG + j] * dist
            s = jnp.where(valid, s, NEG)
            e = jnp.where(valid, jnp.exp2(s - jnp.max(s, axis=-1, keepdims=True)), 0.0)
            den = jnp.sum(e, axis=-1, keepdims=True)
            p = e / jnp.where(den > 0.0, den, 1.0)
            psum = psum + p
            outs.append(_dot(p.astype(CDT), vc))
        imp = _split_dot(psum, ov)
        imp = jnp.where(jl > blk, NEG_INF, jnp.where(forced, BIG, imp))
        sb = jnp.full((tq, LANES), NEG, F32)
        for _ in range(n_topk):
            mx = jnp.max(imp, axis=-1, keepdims=True)
            idx = jnp.min(jnp.where(imp == mx, jl, float(LANES)), axis=-1, keepdims=True)
            hit = jl == idx
            sb = jnp.where(hit, 0.0, sb)
            imp = jnp.where(hit, -jnp.inf, imp)
        bias.append(sb)
    sb = jnp.where(low, bias[1], bias[0])
    sb_ref[0] = sb.astype(sb_ref.dtype)
    used = jnp.max(jnp.where(sb == 0.0, 1.0, 0.0), axis=0, keepdims=True)
    used_ref[0, 0] = jnp.broadcast_to(used, used_ref.shape[2:])
    for blk_i in range(NSA_HEADS // 2):
        oc_ref[0, :, blk_i * LANES:(blk_i + 1) * LANES] = jnp.where(
            low, outs[2 * blk_i], outs[2 * blk_i + 1]).astype(oc_ref.dtype)


def _nsa_cmp_select(proj3, kvc, n_topk):
    b, s, _ = proj3.shape
    tq = min(256, s)
    nblk = kvc.shape[1]
    return pl.pallas_call(
        functools.partial(_nsa_cmp_kernel, tq=tq, n_topk=n_topk),
        grid=(b, s // tq),
        in_specs=[
            pl.BlockSpec(memory_space=pltpu.SMEM),
            pl.BlockSpec((1, tq, 4 * LANES), lambda bi, qi: (bi, qi, PB_DQ // 4)),
            pl.BlockSpec((1, nblk, kvc.shape[2]), lambda bi, qi: (bi, 0, 0)),
        ],
        out_specs=[
            pl.BlockSpec((1, tq, NSA_HEADS * NSA_DH), lambda bi, qi: (bi, qi, 0)),
            pl.BlockSpec((1, tq, LANES), lambda bi, qi: (bi, qi, 0)),
            pl.BlockSpec((1, 1, 8, LANES), lambda bi, qi: (bi, qi, 0, 0)),
        ],
        out_shape=[jax.ShapeDtypeStruct((b, s, NSA_HEADS * NSA_DH), CDT),
                   jax.ShapeDtypeStruct((b, s, LANES), CDT),
                   jax.ShapeDtypeStruct((b, s // tq, 8, LANES), F32)],
        compiler_params=_cparams(("parallel", "parallel")),
        name="nsa_cmp_select",
    )(jnp.asarray(_alibi_slopes(NSA_HEADS)), proj3, kvc)


def _compact_heads(heads, mine, low):
    both = [jnp.where(mine, a, pltpu.roll(a, NSA_DH, axis=1)) for a in heads]
    out = [jnp.where(low, both[2 * jj], both[2 * jj + 1]) for jj in range(NSA_HPG // 2)]
    return jnp.concatenate(out, axis=1)


def _nsa_win_kernel(slopes_ref, q_ref, kp_ref, kc_ref, vp_ref, vc_ref, o_ref, *, tq):
    g = pl.program_id(1)
    qi = pl.program_id(2)
    lane = lax.broadcasted_iota(jnp.int32, (tq, LANES), 1)
    low = lane < NSA_DH
    mine = (lane >> HALF_SHIFT) == g
    k = jnp.concatenate([kp_ref[0], kc_ref[0]], axis=0)
    v = jnp.concatenate([vp_ref[0], vc_ref[0]], axis=0)
    r = lax.broadcasted_iota(jnp.int32, (tq, 2 * tq), 0)
    c = lax.broadcasted_iota(jnp.int32, (tq, 2 * tq), 1)
    disti = r + tq - c
    valid = (disti >= 0) & (disti < WINDOW) & ((c >= tq) | (qi > 0))
    dist = disti.astype(F32)
    heads = []
    for j in range(NSA_HPG):
        qb = q_ref[0, :, j * LANES:(j + 1) * LANES]
        qm = jnp.where(mine, qb, jnp.zeros_like(qb))
        s = _dot_nt(qm, k) - slopes_ref[g * NSA_HPG + j] * dist
        s = jnp.where(valid, s, NEG)
        e = jnp.exp2(s - jnp.max(s, axis=-1, keepdims=True))
        p = e / jnp.sum(e, axis=-1, keepdims=True)
        heads.append(_dot(p.astype(CDT), v))
    o_ref[0] = _compact_heads(heads, mine, low).astype(o_ref.dtype)


def _nsa_window(proj3):
    b, s, _ = proj3.shape
    tq = WINDOW
    return pl.pallas_call(
        functools.partial(_nsa_win_kernel, tq=tq),
        grid=(b, NSA_GROUPS, s // tq),
        in_specs=[
            pl.BlockSpec(memory_space=pltpu.SMEM),
            pl.BlockSpec((1, tq, 4 * LANES), lambda bi, g, qi: (bi, qi, PB_DQ // 4)),
            pl.BlockSpec((1, tq, LANES), lambda bi, g, qi: (bi, jnp.maximum(qi - 1, 0), PB_WIN_K)),
            pl.BlockSpec((1, tq, LANES), lambda bi, g, qi: (bi, qi, PB_WIN_K)),
            pl.BlockSpec((1, tq, LANES), lambda bi, g, qi: (bi, jnp.maximum(qi - 1, 0), PB_WIN_V)),
            pl.BlockSpec((1, tq, LANES), lambda bi, g, qi: (bi, qi, PB_WIN_V)),
        ],
        out_specs=pl.BlockSpec((1, tq, NSA_HPG * NSA_DH), lambda bi, g, qi: (bi, qi, g)),
        out_shape=jax.ShapeDtypeStruct((b, s, NSA_HEADS * NSA_DH), CDT),
        compiler_params=_cparams(("parallel", "parallel", "parallel")),
        name="nsa_window",
    )(jnp.asarray(_alibi_slopes(NSA_HEADS)), proj3, proj3, proj3, proj3, proj3)


def _nsa_sel_kernel(slopes_ref, cnt_ref, idx_ref, q_ref, sb_ref, k_ref, v_ref, oc_ref, ow_ref, gl_ref, e_ref, o_ref,
                    qa_ref, m_ref, acc_ref, sa_ref, sb2_ref, ma_ref, mb_ref, *, tq):
    g = pl.program_id(1)
    qi = pl.program_id(2)
    lane = lax.broadcasted_iota(jnp.int32, (tq, LANES), 1)
    low = lane < NSA_DH
    mine = (lane >> HALF_SHIFT) == g
    sb = sb_ref[0]
    for j in range(NSA_HPG):
        qa_ref[j * tq:(j + 1) * tq] = jnp.where(mine, q_ref[0, :, j * LANES:(j + 1) * LANES], sb)
    _flash_reset(m_ref, acc_ref)
    col = lax.broadcasted_iota(jnp.int32, (1, tq), 1).astype(F32)
    jl = lane & (NSA_DH - 1)
    krow = lax.broadcasted_iota(jnp.int32, (tq, LANES), 0)

    def produce(buf, ki):
        k0 = pl.multiple_of(ki * tq, tq)
        k = k_ref[0, pl.ds(k0, tq), :]
        onehot = jnp.where(((k0 + krow) >> SLC_SHIFT) == jl, 1.0, 0.0).astype(k.dtype)
        s_all = _dot_nt(qa_ref[...], jnp.where(mine, k, onehot))
        rel = ((ki - qi) * tq).astype(F32)
        for j in range(NSA_HPG):
            rows = slice(j * tq, (j + 1) * tq)
            _put_logits(buf, s_all[rows] + slopes_ref[g * NSA_HPG + j] * (col + rel), rows)

    def consume(buf, ki, masked):
        k0 = pl.multiple_of(ki * tq, tq)
        _flash_consume(buf, v_ref[0, pl.ds(k0, tq), :], m_ref, acc_ref, masked, tq)

    nq = pl.num_programs(2)
    row = (pl.program_id(0) * NSA_GROUPS + g) * nq + qi
    _tile_pipeline(cnt_ref[row], lambda t: idx_ref[row * nq + t], produce, consume,
                   (sa_ref, ma_ref), (sb2_ref, mb_ref))

    o = _flash_result(acc_ref[...])
    o_s = _compact_heads([o[j * tq:(j + 1) * tq] for j in range(NSA_HPG)], mine, low)
    gates = _split_dot(jax.nn.sigmoid(gl_ref[0]), e_ref[0])
    w = NSA_HPG * NSA_DH
    y = (gates[:, 0:w] * oc_ref[0].astype(F32) + gates[:, w:2 * w] * o_s
         + gates[:, 2 * w:3 * w] * ow_ref[0].astype(F32))
    o_ref[0] = y.astype(o_ref.dtype)


def _nsa_selected(proj3, sbias, used, o_c, o_w, small3, expand):
    b, s, _ = proj3.shape
    tq = min(256, s)
    nq = s // tq
    w = NSA_HPG * NSA_DH
    u = used[:, :, 0, :].reshape(b, nq, NSA_GROUPS, NSA_DH)[:, :, ::-1, :nq * (tq // SLC_LEN)]
    flags = (u.reshape(b, nq, NSA_GROUPS, nq, tq // SLC_LEN).max(axis=-1) > 0.0).astype(jnp.int32)
    flags = flags.transpose(0, 2, 1, 3)
    qt = jnp.arange(nq, dtype=jnp.int32)
    flags = flags * (qt[None, :] < qt[:, None]).astype(jnp.int32)
    cnt = flags.sum(axis=-1)
    idx = jnp.argsort(1 - flags, axis=-1, stable=True).astype(jnp.int32)
    idx = jnp.where(qt == cnt[..., None], qt[:, None], idx)
    return pl.pallas_call(
        functools.partial(_nsa_sel_kernel, tq=tq),
        grid=(b, NSA_GROUPS, nq),
        in_specs=[
            pl.BlockSpec(memory_space=pltpu.SMEM),
            pl.BlockSpec(memory_space=pltpu.SMEM),
            pl.BlockSpec(memory_space=pltpu.SMEM),
            pl.BlockSpec((1, tq, 4 * LANES), lambda bi, g, qi: (bi, qi, PB_DQ // 4)),
            pl.BlockSpec((1, tq, LANES), lambda bi, g, qi: (bi, qi, 0)),
            pl.BlockSpec((1, s, LANES), lambda bi, g, qi: (bi, 0, PB_SEL_K)),
            pl.BlockSpec((1, s, LANES), lambda bi, g, qi: (bi, 0, PB_SEL_V)),
            pl.BlockSpec((1, tq, w), lambda bi, g, qi: (bi, qi, g)),
            pl.BlockSpec((1, tq, w), lambda bi, g, qi: (bi, qi, g)),
            pl.BlockSpec((1, tq, LANES), lambda bi, g, qi: (bi, qi, 0)),
            pl.BlockSpec((1, LANES, 3 * w), lambda bi, g, qi: (g, 0, 0)),
        ],
        out_specs=pl.BlockSpec((1, tq, w), lambda bi, g, qi: (bi, qi, g)),
        out_shape=jax.ShapeDtypeStruct((b, s, NSA_HEADS * NSA_DH), CDT),
        scratch_shapes=[pltpu.VMEM((NSA_HPG * tq, LANES), CDT)] + _flash_scratch(NSA_HPG * tq, tq),
        compiler_params=_cparams(("parallel", "parallel", "arbitrary")),
        name="nsa_selected",
    )(jnp.asarray(_alibi_slopes(NSA_HEADS)), cnt.reshape(-1), idx.reshape(-1), proj3, sbias, proj3, proj3,
      o_c, o_w, small3, expand)


def _merge_kernel(ya_ref, yb_ref, yc_ref, yd_ref, ga_ref, gb_ref, gc_ref, gd_ref, wb_ref, wo_ref, x_ref, o_ref):
    merged = None
    for n, (y_ref, g_ref) in enumerate(((ya_ref, ga_ref), (yb_ref, gb_ref), (yc_ref, gc_ref), (yd_ref, gd_ref))):
        t = jax.nn.sigmoid(g_ref[...].astype(F32)) * _dot(y_ref[...], wb_ref[n])
        merged = t if merged is None else merged + t
    o_ref[...] = x_ref[...] + _dot(merged.astype(CDT), wo_ref[...])


def _merge(ys, proj2, wb, wo, x2):
    t, d = x2.shape
    tm = min(512, t)
    gate_blk = PB_GATE * LANES // d
    yspec = pl.BlockSpec((tm, BRANCH_WIDTH), lambda i: (i, 0))
    gspecs = [pl.BlockSpec((tm, d), functools.partial(lambda i, n: (i, gate_blk + n), n=n)) for n in range(N_BRANCH)]
    return pl.pallas_call(
        _merge_kernel,
        grid=(t // tm,),
        in_specs=[yspec] * N_BRANCH + gspecs + [
            pl.BlockSpec((N_BRANCH, BRANCH_WIDTH, d), lambda i: (0, 0, 0)),
            pl.BlockSpec((d, d), lambda i: (0, 0)),
            pl.BlockSpec((tm, d), lambda i: (i, 0)),
        ],
        out_specs=pl.BlockSpec((tm, d), lambda i: (i, 0)),
        out_shape=jax.ShapeDtypeStruct((t, d), F32),
        compiler_params=_cparams(("parallel",)),
        name="merge",
    )(*ys, proj2, proj2, proj2, proj2, wb, wo, x2)


HALO = 16


def _ffn_kernel(x_ref, xh_ref, g_ref, wu_ref, cw_ref, cb_ref, wd_ref, gf_ref, o_ref, he_ref, ua_ref, ug_ref,
                *, tm, fc, final):
    i = pl.program_id(1)
    x = x_ref[0]
    g = g_ref[...]
    xh = xh_ref[0] * (i > 0).astype(F32)
    he_ref[0:HALO] = _rms(xh, g).astype(CDT)
    he_ref[HALO:HALO + tm] = _rms(x, g).astype(CDT)
    he = he_ref[...]
    acc = jnp.zeros((tm, x.shape[1]), F32)
    for c in range(D_FF // fc):
        outs = []
        for half, u_ref in ((0, ua_ref), (1, ug_ref)):
            lo = half * D_FF + c * fc
            u_ref[...] = _dot(he, wu_ref[:, lo:lo + fc])
            conv = cb_ref[:, lo:lo + fc]
            for kk in range(CONV_WIDTH):
                off = HALO - (CONV_WIDTH - 1) + kk
                conv = conv + cw_ref[kk:kk + 1, lo:lo + fc] * u_ref[off:off + tm, :]
            outs.append(conv)
        a, gg = outs
        act = (a * jax.nn.sigmoid(a) * gg).astype(CDT)
        acc = acc + _dot(act, wd_ref[c * fc:(c + 1) * fc, :])
    y = x + acc
    if final:
        y = _rms(y, gf_ref[...])
    o_ref[0] = y


def _ffn(x3, g, wu, cw, cb, wd, gf, final):
    b, s, d = x3.shape
    tm = min(512, s)
    fc = 256
    const = lambda shape: pl.BlockSpec(shape, lambda bi, i: (0,) * len(shape))
    return pl.pallas_call(
        functools.partial(_ffn_kernel, tm=tm, fc=fc, final=final),
        grid=(b, s // tm),
        in_specs=[
            pl.BlockSpec((1, tm, d), lambda bi, i: (bi, i, 0)),
            pl.BlockSpec((1, HALO, d), lambda bi, i: (bi, jnp.maximum(i * (tm // HALO) - 1, 0), 0)),
            const((1, d)), const((d, 2 * D_FF)), const((CONV_WIDTH, 2 * D_FF)), const((1, 2 * D_FF)),
            const((D_FF, d)), const((1, d)),
        ],
        out_specs=pl.BlockSpec((1, tm, d), lambda bi, i: (bi, i, 0)),
        out_shape=jax.ShapeDtypeStruct((b, s, d), F32),
        scratch_shapes=[pltpu.VMEM((tm + HALO, d), CDT), pltpu.VMEM((tm + HALO, fc), F32),
                        pltpu.VMEM((tm + HALO, fc), F32)],
        compiler_params=_cparams(("parallel", "arbitrary")),
        name="conv_glu_mlp",
    )(x3, x3, g.reshape(1, d), wu, cw, cb.reshape(1, -1), wd, gf.reshape(1, d))


def _prep_w_in(w):
    widths = (512, 512, 512, MLA_Q_LORA, MLA_KV_LORA, MLA_ROPE, 512, 512, 512, FOX_HEADS,
              512, 768, 3 * NSA_HEADS, N_BRANCH * D_MODEL)
    offs = np.cumsum((0,) + widths)
    (a_q, a_k, a_v, b_cq, b_ckv, b_kr, c_q, c_k, c_v, c_f, d_q, d_kv, d_g, gate) = [
        w[:, offs[i]:offs[i + 1]] for i in range(len(widths))]
    d = w.shape[0]
    d_q = d_q.reshape(d, NSA_GROUPS, NSA_HPG, NSA_DH).transpose(0, 2, 1, 3).reshape(d, 512)
    half = MLA_ROPE // 2
    kr_swap = jnp.concatenate([-b_kr[:, half:], b_kr[:, :half]], axis=1)
    z64 = jnp.zeros((d, LANES - MLA_ROPE), w.dtype)
    big = jnp.concatenate([
        a_q * (LOG2E * DIFF_DH ** -0.5), a_k, a_v,
        c_q * (LOG2E * FOX_DH ** -0.5), c_k, c_v,
        d_q * (LOG2E * NSA_DH ** -0.5), d_kv,
        b_cq, b_ckv, b_kr, z64, kr_swap, z64,
        gate], axis=1)
    small = jnp.concatenate([c_f, d_g, jnp.zeros((d, LANES - FOX_HEADS - 3 * NSA_HEADS), w.dtype)], axis=1)
    return big.astype(CDT), small.astype(CDT)


def _prep_mla(w_uq, w_ukv):
    r = w_uq.shape[0]
    hw = 2 * LANES
    half = MLA_ROPE // 2
    scale = LOG2E * (MLA_NOPE + MLA_ROPE) ** -0.5
    wq = (w_uq * scale).reshape(r, MLA_HEADS, MLA_NOPE + MLA_ROPE)
    nope, t1, t2 = wq[..., :MLA_NOPE], wq[..., MLA_NOPE:MLA_NOPE + half], wq[..., MLA_NOPE + half:]
    zpad = jnp.zeros((r, MLA_HEADS, hw - MLA_NOPE - MLA_ROPE), w_uq.dtype)
    wqm = jnp.concatenate([nope, t1, t2, zpad], axis=-1).reshape(r, MLA_HEADS * hw)
    wqs = jnp.concatenate([jnp.zeros_like(nope), -t2, t1, zpad], axis=-1).reshape(r, MLA_HEADS * hw)
    wkv = w_ukv.reshape(w_ukv.shape[0], MLA_HEADS, MLA_NOPE + MLA_VDIM)
    wk = wkv[..., :MLA_NOPE].reshape(-1, MLA_HEADS * MLA_NOPE)
    wv = wkv[..., MLA_NOPE:].reshape(-1, MLA_HEADS * MLA_VDIM)
    return wqm.astype(CDT), wqs.astype(CDT), wk.astype(CDT), wv.astype(CDT)


def _rope_tables(s):
    half = MLA_ROPE // 2
    inv_freq = ROPE_THETA ** (-jnp.arange(0, MLA_ROPE, 2, dtype=F32) / MLA_ROPE)
    ang = jnp.arange(s, dtype=F32)[:, None] * inv_freq[None, :]
    cos, sin = jnp.cos(ang), jnp.sin(ang)
    z = jnp.zeros((s, LANES - MLA_ROPE), F32)
    cosk = jnp.concatenate([cos, cos, z], axis=1)
    sink = jnp.concatenate([sin, sin, z], axis=1)
    cosq = jnp.concatenate([jnp.ones((s, MLA_NOPE), F32), cosk], axis=1)
    sinq = jnp.concatenate([jnp.zeros((s, MLA_NOPE), F32), sink], axis=1)
    return cosq, sinq, cosk, sink


def _prep_compress(pe, w1, w2):
    eye2 = jnp.eye(2, dtype=F32)
    w1r = w1.reshape(2, CMP_LEN, NSA_DH, CMP_HIDDEN)

    def expand(wpart):
        t = jnp.einsum('kpdh,kK,gG->pkgdKGh', wpart, eye2, eye2)
        return t.reshape(CMP_STRIDE * 4 * NSA_DH, 4 * CMP_HIDDEN)

    w1a, w1b = expand(w1r[:, :CMP_STRIDE]), expand(w1r[:, CMP_STRIDE:])

    def pe_row(p):
        t = jnp.broadcast_to(p.transpose(1, 0, 2)[:, :, None, :], (CMP_STRIDE, 2, NSA_GROUPS, NSA_DH))
        return jnp.pad(t.reshape(1, -1), ((0, 7), (0, 0)))

    pea, peb = pe_row(pe[:, :CMP_STRIDE]), pe_row(pe[:, CMP_STRIDE:])
    w2b = jnp.einsum('khd,kK,gG,u->kghKGud', w2, eye2, eye2, jnp.ones((2,), F32))
    w2b = w2b.reshape(4 * CMP_HIDDEN, 4 * 2 * NSA_DH)
    return w1a.astype(CDT), w1b.astype(CDT), pea.astype(CDT), peb.astype(CDT), w2b.astype(CDT)


def _gate_expand():
    e = np.zeros((NSA_GROUPS, LANES, 3, NSA_HPG, NSA_DH), np.float32)
    for g in range(NSA_GROUPS):
        for j in range(NSA_HPG):
            for br in range(3):
                e[g, SMALL_G + (g * NSA_HPG + j) * 3 + br, br, j, :] = 1.0
    return jnp.asarray(e.reshape(NSA_GROUPS, LANES, 3 * NSA_HPG * NSA_DH)).astype(CDT)


def _token_mixers(x3, l, norm_mix, w_in, diff_lambda, diff_subln, mla_norm_q, mla_w_uq, mla_norm_kv, mla_w_ukv,
                  fox_b_f, nsa_cmp_pe, nsa_cmp_w1, nsa_cmp_w2, w_branch, w_out, rope_tabs):
    b, s, d = x3.shape
    t = b * s
    x2 = x3.reshape(t, d)
    w_big, w_small = _prep_w_in(w_in)
    proj, small = _in_proj(x2, norm_mix, w_big, w_small)
    proj3 = proj.reshape(b, s, N_PROJ)
    small3 = small.reshape(b, s, LANES)

    lam_init = 0.8 - 0.6 * math.exp(-0.3 * l)
    y_a = _diff_attention(proj3, diff_lambda, diff_subln, lam_init)

    wqm, wqs, wk, wv = _prep_mla(mla_w_uq, mla_w_ukv)
    qc, kc, vv = _mla_prep(proj3, mla_norm_q, mla_norm_kv, wqm, wqs, wk, wv, rope_tabs)
    y_b = _mla_attention(qc, kc, vv)

    cf_rows = small3[:, :, SMALL_F:SMALL_F + FOX_HEADS].transpose(0, 2, 1).reshape(b * FOX_HEADS, s)
    bias_rows = jnp.tile(fox_b_f.astype(F32), b).reshape(b * FOX_HEADS, 1)
    c4 = _fox_cumsum(cf_rows, bias_rows)
    y_c = _fox_attention(proj3, c4)

    w1a, w1b, pea, peb, w2b = _prep_compress(nsa_cmp_pe, nsa_cmp_w1, nsa_cmp_w2)
    xc = proj3[:, :, PB_CMP_K * LANES:(PB_CMP_V + 1) * LANES].reshape(b, s // CMP_STRIDE, CMP_STRIDE * 2 * LANES)
    kvc = _nsa_compress(xc, w1a, w1b, pea, peb, w2b)
    n_topk = min(SLC_TOPK, s // SLC_LEN)
    o_c, sbias, used = _nsa_cmp_select(proj3, kvc, n_topk)
    o_w = _nsa_window(proj3)
    y_d = _nsa_selected(proj3, sbias, used, o_c, o_w, small3, _gate_expand())

    ys = [y.reshape(t, BRANCH_WIDTH) for y in (y_a, y_b, y_c, y_d)]
    return _merge(ys, proj, w_branch.astype(CDT), w_out.astype(CDT), x2).reshape(b, s, d)


def kernel(x, norm_mix, w_in, diff_lambda, diff_subln, mla_norm_q, mla_w_uq, mla_norm_kv, mla_w_ukv, fox_b_f,
           nsa_cmp_pe, nsa_cmp_w1, nsa_cmp_w2, w_branch, w_out, norm_ffn, w_up, conv_w, conv_b, w_down, norm_final):
    depth = w_in.shape[0]
    s = x.shape[1]
    rope_tabs = _rope_tables(s)
    for l in range(depth):
        x = _token_mixers(x, l, norm_mix[l], w_in[l], diff_lambda[l], diff_subln[l], mla_norm_q[l], mla_w_uq[l],
                          mla_norm_kv[l], mla_w_ukv[l], fox_b_f[l], nsa_cmp_pe[l], nsa_cmp_w1[l], nsa_cmp_w2[l],
                          w_branch[l], w_out[l], rope_tabs)
        x = _ffn(x, norm_ffn[l], w_up[l].astype(CDT), conv_w[l], conv_b[l], w_down[l].astype(CDT), norm_final,
                 final=(l == depth - 1))
    return x
```

```python
import functools
import math

import numpy as np
import jax
import jax.numpy as jnp
from jax import lax
from jax.experimental import pallas as pl
from jax.experimental.pallas import tpu as pltpu

F32 = jnp.float32
CDT = jnp.bfloat16

NEG = -1e30
NEG_INF = -1e30
BIG = 1e9
NORM_EPS = 1e-6
LOG2E = 1.4426950408889634
LANES = 128

D_MODEL = 1024
DIFF_HEADS, DIFF_DH = 4, 64
MLA_HEADS, MLA_NOPE, MLA_ROPE, MLA_VDIM = 4, 128, 64, 128
MLA_Q_LORA, MLA_KV_LORA = 256, 256
ROPE_THETA = 10000.0
FOX_HEADS, FOX_DH = 4, 128
NSA_HEADS, NSA_GROUPS, NSA_DH = 8, 2, 64
NSA_HPG = NSA_HEADS // NSA_GROUPS
CMP_STRIDE = 16
CMP_LEN = 2 * CMP_STRIDE
CMP_HIDDEN = 128
SLC_LEN = 64
SLC_SHIFT = 6
HALF_SHIFT = 6
SLC_TOPK = 8
WINDOW = 256
N_BRANCH = 4
BRANCH_WIDTH = 512
D_FF = 2816
CONV_WIDTH = 3

PB_AQ, PB_AK, PB_AV = 0, 4, 8
PB_CQ, PB_CK, PB_CV = 12, 16, 20
PB_DQ = 24
PB_CMP_K, PB_CMP_V, PB_SEL_K, PB_SEL_V, PB_WIN_K, PB_WIN_V = 28, 29, 30, 31, 32, 33
PB_BCQ, PB_BCKV, PB_BKR, PB_BKRS = 34, 36, 38, 39
PB_GATE = 40
N_PROJ = 72 * LANES
SMALL_F, SMALL_G = 0, 4

VMEM_LIMIT = 56 * 1024 * 1024


def _cparams(sem):
    return pltpu.CompilerParams(dimension_semantics=sem, vmem_limit_bytes=VMEM_LIMIT)


def _rms(xf, g):
    return xf * lax.rsqrt(jnp.mean(xf * xf, axis=-1, keepdims=True) + NORM_EPS) * g


def _dot(a, b):
    return jnp.dot(a, b, preferred_element_type=F32)


def _dot_nt(a, b):
    return lax.dot_general(a, b, (((1,), (1,)), ((), ())), preferred_element_type=F32)


def _split_dot(a, b):
    hi = a.astype(CDT)
    lo = (a - hi.astype(F32)).astype(CDT)
    return _dot(hi, b) + _dot(lo, b)


def _alibi_slopes(n):
    return (LOG2E * np.exp2(-8.0 * np.arange(1, n + 1) / n)).astype(np.float32)


def _inproj_kernel(x_ref, g_ref, w_ref, ws_ref, o_ref, os_ref, h_ref):
    @pl.when(pl.program_id(1) == 0)
    def _():
        h = _rms(x_ref[...], g_ref[...]).astype(CDT)
        h_ref[...] = h
        os_ref[...] = _dot(h, ws_ref[...])

    o_ref[...] = _dot(h_ref[...], w_ref[...]).astype(o_ref.dtype)


def _in_proj(x2, g, w, ws):
    t, d = x2.shape
    n = w.shape[1]
    tm = min(1024, t)
    tn = 1024
    return pl.pallas_call(
        _inproj_kernel,
        grid=(t // tm, n // tn),
        in_specs=[
            pl.BlockSpec((tm, d), lambda i, j: (i, 0)),
            pl.BlockSpec((1, d), lambda i, j: (0, 0)),
            pl.BlockSpec((d, tn), lambda i, j: (0, j)),
            pl.BlockSpec((d, LANES), lambda i, j: (0, 0)),
        ],
        out_specs=[
            pl.BlockSpec((tm, tn), lambda i, j: (i, j)),
            pl.BlockSpec((tm, LANES), lambda i, j: (i, 0)),
        ],
        out_shape=[jax.ShapeDtypeStruct((t, n), CDT), jax.ShapeDtypeStruct((t, LANES), F32)],
        scratch_shapes=[pltpu.VMEM((tm, d), CDT)],
        compiler_params=_cparams(("parallel", "arbitrary")),
        name="in_proj",
    )(x2, g.reshape(1, d), w, ws)


def _fox_cumsum_kernel(cf_ref, bf_ref, o_ref):
    rows, s = cf_ref.shape
    lane = lax.broadcasted_iota(jnp.int32, (rows, LANES), 1)
    carry = jnp.zeros((rows, 1), F32)
    for c in range(s // LANES):
        z = cf_ref[:, c * LANES:(c + 1) * LANES] + bf_ref[...]
        xs = jnp.minimum(z, 0.0) - jnp.log1p(jnp.exp(-jnp.abs(z)))
        d = 1
        while d < LANES:
            xs = xs + jnp.where(lane >= d, pltpu.roll(xs, d, axis=1), 0.0)
            d *= 2
        xs = xs + carry
        o_ref[:, c * LANES:(c + 1) * LANES] = xs
        carry = xs[:, LANES - 1:LANES]


def _fox_cumsum(cf_rows, bias_rows):
    return pl.pallas_call(
        _fox_cumsum_kernel,
        out_shape=jax.ShapeDtypeStruct(cf_rows.shape, F32),
        name="fox_cumsum",
    )(cf_rows, bias_rows)


def _flash_scratch(rows, tk):
    return [pltpu.VMEM((rows, LANES), F32), pltpu.VMEM((rows, 2 * LANES), F32),
            pltpu.VMEM((rows, tk), F32), pltpu.VMEM((rows, tk), F32),
            pltpu.VMEM((rows, LANES), F32), pltpu.VMEM((rows, LANES), F32)]


def _flash_reset(m_ref, acc_ref):
    m_ref[...] = jnp.full(m_ref.shape, NEG, F32)
    acc_ref[...] = jnp.zeros(acc_ref.shape, F32)


def _row_max(s):
    return jnp.broadcast_to(jnp.max(s, axis=-1, keepdims=True), (s.shape[0], LANES))


def _put_logits(buf, s, rows=slice(None)):
    buf[0][rows] = s
    buf[1][rows] = _row_max(s)


def _with_ones(v):
    return jnp.concatenate([v, jnp.ones((v.shape[0], LANES), v.dtype)], axis=1)


def _flash_consume(buf, v, m_ref, acc_ref, masked, tq):
    s = buf[0][...]
    m_cur = buf[1][...]
    if masked:
        r = lax.broadcasted_iota(jnp.int32, s.shape, 0) & (tq - 1)
        c = lax.broadcasted_iota(jnp.int32, s.shape, 1)
        s = jnp.where(c <= r, s, NEG)
        m_cur = _row_max(s)
    m_old = m_ref[...]
    m_new = jnp.maximum(m_old, m_cur)
    alpha = jnp.exp2(m_old - m_new)
    p = jnp.exp2(s - jnp.tile(m_new, (1, s.shape[1] // LANES))).astype(CDT)
    acc_ref[...] = jnp.tile(alpha, (1, 2)) * acc_ref[...] + _dot(p, _with_ones(v))
    m_ref[...] = m_new


def _flash_result(acc):
    return acc[:, :LANES] / acc[:, LANES:]


def _tile_pipeline(n, tile_of, produce, consume, buf_a, buf_b):
    produce(buf_a, tile_of(0))

    def pair(j, c):
        t = 2 * j
        produce(buf_b, tile_of(t + 1))
        consume(buf_a, tile_of(t), False)
        produce(buf_a, tile_of(t + 2))
        consume(buf_b, tile_of(t + 1), False)
        return c

    lax.fori_loop(0, n // 2, pair, 0)

    @pl.when(n % 2 == 0)
    def _():
        consume(buf_a, tile_of(n), True)

    @pl.when(n % 2 == 1)
    def _():
        produce(buf_b, tile_of(n))
        consume(buf_a, tile_of(n - 1), False)
        consume(buf_b, tile_of(n), True)


def _diff_attn_kernel(slopes_ref, lam_ref, g_ref, q_ref, k_ref, v_ref, o_ref,
                      qq_ref, m_ref, acc_ref, sa_ref, sb_ref, ma_ref, mb_ref, *, tq, lam_init):
    h = pl.program_id(1)
    qi = pl.program_id(2)
    slope = slopes_ref[h]
    q = q_ref[0]
    lane = lax.broadcasted_iota(jnp.int32, q.shape, 1)
    zero = jnp.zeros_like(q)
    qq_ref[0:tq] = jnp.where(lane < DIFF_DH, q, zero)
    qq_ref[tq:2 * tq] = jnp.where(lane >= DIFF_DH, q, zero)
    _flash_reset(m_ref, acc_ref)
    col = lax.broadcasted_iota(jnp.int32, (1, tq), 1).astype(F32)

    def produce(buf, ki):
        k0 = pl.multiple_of(ki * tq, tq)
        s = _dot_nt(qq_ref[...], k_ref[0, pl.ds(k0, tq), :])
        _put_logits(buf, s + slope * (col + ((ki - qi) * tq).astype(F32)))

    def consume(buf, ki, masked):
        k0 = pl.multiple_of(ki * tq, tq)
        _flash_consume(buf, v_ref[0, pl.ds(k0, tq), :], m_ref, acc_ref, masked, tq)

    _tile_pipeline(qi, lambda t: t, produce, consume, (sa_ref, ma_ref), (sb_ref, mb_ref))

    lf = lam_ref[...]
    lam = (jnp.exp(jnp.sum(lf[0:1] * lf[1:2], axis=-1, keepdims=True))
           - jnp.exp(jnp.sum(lf[2:3] * lf[3:4], axis=-1, keepdims=True)) + lam_init)
    o = _flash_result(acc_ref[...])
    d = o[0:tq] - lam * o[tq:2 * tq]
    o_ref[0] = (_rms(d, g_ref[...]) * (1.0 - lam_init)).astype(o_ref.dtype)


def _diff_attention(proj3, diff_lambda, subln, lam_init):
    b, s, _ = proj3.shape
    tq = min(512, s)
    dv = 2 * DIFF_DH
    kern = functools.partial(_diff_attn_kernel, tq=tq, lam_init=lam_init)
    return pl.pallas_call(
        kern,
        grid=(b, DIFF_HEADS, s // tq),
        in_specs=[
            pl.BlockSpec(memory_space=pltpu.SMEM),
            pl.BlockSpec((4, DIFF_DH), lambda bi, h, qi: (0, 0)),
            pl.BlockSpec((1, dv), lambda bi, h, qi: (0, 0)),
            pl.BlockSpec((1, tq, LANES), lambda bi, h, qi: (bi, qi, PB_AQ + h)),
            pl.BlockSpec((1, s, LANES), lambda bi, h, qi: (bi, 0, PB_AK + h)),
            pl.BlockSpec((1, s, LANES), lambda bi, h, qi: (bi, 0, PB_AV + h)),
        ],
        out_specs=pl.BlockSpec((1, tq, dv), lambda bi, h, qi: (bi, qi, h)),
        out_shape=jax.ShapeDtypeStruct((b, s, DIFF_HEADS * dv), CDT),
        scratch_shapes=[pltpu.VMEM((2 * tq, LANES), CDT)] + _flash_scratch(2 * tq, tq),
        compiler_params=_cparams(("parallel", "parallel", "arbitrary")),
        name="diff_attention",
    )(jnp.asarray(_alibi_slopes(DIFF_HEADS)), diff_lambda, subln.reshape(1, dv), proj3, proj3, proj3)


def _mla_prep_kernel(cq_ref, ckv_ref, kr_ref, krs_ref, gq_ref, gkv_ref, wqm_ref, wqs_ref, wk_ref, wv_ref,
                     cosq_ref, sinq_ref, cosk_ref, sink_ref, q_ref, k_ref, v_ref):
    hq = _rms(cq_ref[0].astype(F32), gq_ref[...]).astype(CDT)
    qm = _dot(hq, wqm_ref[...])
    qs = _dot(hq, wqs_ref[...])
    cosq, sinq = cosq_ref[...], sinq_ref[...]
    hw = 2 * LANES
    for h in range(MLA_HEADS):
        sl = slice(h * hw, (h + 1) * hw)
        q_ref[0, :, sl] = (qm[:, sl] * cosq + qs[:, sl] * sinq).astype(q_ref.dtype)
    hkv = _rms(ckv_ref[0].astype(F32), gkv_ref[...]).astype(CDT)
    kn = _dot(hkv, wk_ref[...])
    v_ref[0] = _dot(hkv, wv_ref[...]).astype(v_ref.dtype)
    kpe = (kr_ref[0].astype(F32) * cosk_ref[...] + krs_ref[0].astype(F32) * sink_ref[...]).astype(k_ref.dtype)
    for h in range(MLA_HEADS):
        k_ref[0, :, h * hw:h * hw + LANES] = kn[:, h * LANES:(h + 1) * LANES].astype(k_ref.dtype)
        k_ref[0, :, h * hw + LANES:(h + 1) * hw] = kpe


def _mla_prep(proj3, gq, gkv, wqm, wqs, wk, wv, tabs):
    b, s, _ = proj3.shape
    tm = min(512, s)
    hw = 2 * LANES
    cosq, sinq, cosk, sink = tabs
    const = lambda shape: pl.BlockSpec(shape, lambda bi, i: (0,) * len(shape))
    return pl.pallas_call(
        _mla_prep_kernel,
        grid=(b, s // tm),
        in_specs=[
            pl.BlockSpec((1, tm, MLA_Q_LORA), lambda bi, i: (bi, i, PB_BCQ // 2)),
            pl.BlockSpec((1, tm, MLA_KV_LORA), lambda bi, i: (bi, i, PB_BCKV // 2)),
            pl.BlockSpec((1, tm, LANES), lambda bi, i: (bi, i, PB_BKR)),
            pl.BlockSpec((1, tm, LANES), lambda bi, i: (bi, i, PB_BKRS)),
            const((1, MLA_Q_LORA)), const((1, MLA_KV_LORA)),
            const((MLA_Q_LORA, MLA_HEADS * hw)), const((MLA_Q_LORA, MLA_HEADS * hw)),
            const((MLA_KV_LORA, MLA_HEADS * MLA_NOPE)), const((MLA_KV_LORA, MLA_HEADS * MLA_VDIM)),
            pl.BlockSpec((tm, hw), lambda bi, i: (i, 0)), pl.BlockSpec((tm, hw), lambda bi, i: (i, 0)),
            pl.BlockSpec((tm, LANES), lambda bi, i: (i, 0)), pl.BlockSpec((tm, LANES), lambda bi, i: (i, 0)),
        ],
        out_specs=[
            pl.BlockSpec((1, tm, MLA_HEADS * hw), lambda bi, i: (bi, i, 0)),
            pl.BlockSpec((1, tm, MLA_HEADS * hw), lambda bi, i: (bi, i, 0)),
            pl.BlockSpec((1, tm, MLA_HEADS * MLA_VDIM), lambda bi, i: (bi, i, 0)),
        ],
        out_shape=[
            jax.ShapeDtypeStruct((b, s, MLA_HEADS * hw), CDT),
            jax.ShapeDtypeStruct((b, s, MLA_HEADS * hw), CDT),
            jax.ShapeDtypeStruct((b, s, MLA_HEADS * MLA_VDIM), CDT),
        ],
        compiler_params=_cparams(("parallel", "parallel")),
        name="mla_prep",
    )(proj3, proj3, proj3, proj3, gq.reshape(1, -1), gkv.reshape(1, -1), wqm, wqs, wk, wv,
      cosq, sinq, cosk, sink)


def _plain_attn_kernel(q_ref, k_ref, v_ref, o_ref, m_ref, acc_ref, sa_ref, sb_ref, ma_ref, mb_ref, *, tq):
    qi = pl.program_id(2)
    _flash_reset(m_ref, acc_ref)

    def produce(buf, ki):
        k0 = pl.multiple_of(ki * tq, tq)
        _put_logits(buf, _dot_nt(q_ref[0], k_ref[0, pl.ds(k0, tq), :]))

    def consume(buf, ki, masked):
        k0 = pl.multiple_of(ki * tq, tq)
        _flash_consume(buf, v_ref[0, pl.ds(k0, tq), :], m_ref, acc_ref, masked, tq)

    _tile_pipeline(qi, lambda t: t, produce, consume, (sa_ref, ma_ref), (sb_ref, mb_ref))
    o_ref[0] = _flash_result(acc_ref[...]).astype(o_ref.dtype)


def _mla_attention(qc, kc, v):
    b, s, _ = qc.shape
    tq = min(512, s)
    hw = 2 * LANES
    return pl.pallas_call(
        functools.partial(_plain_attn_kernel, tq=tq),
        grid=(b, MLA_HEADS, s // tq),
        in_specs=[
            pl.BlockSpec((1, tq, hw), lambda bi, h, qi: (bi, qi, h)),
            pl.BlockSpec((1, s, hw), lambda bi, h, qi: (bi, 0, h)),
            pl.BlockSpec((1, s, MLA_VDIM), lambda bi, h, qi: (bi, 0, h)),
        ],
        out_specs=pl.BlockSpec((1, tq, MLA_VDIM), lambda bi, h, qi: (bi, qi, h)),
        out_shape=jax.ShapeDtypeStruct((b, s, MLA_HEADS * MLA_VDIM), CDT),
        scratch_shapes=_flash_scratch(tq, tq),
        compiler_params=_cparams(("parallel", "parallel", "arbitrary")),
        name="mla_attention",
    )(qc, kc, v)


def _fox_attn_kernel(c_ref, q_ref, k_ref, v_ref, o_ref, m_ref, acc_ref, sa_ref, sb_ref, ma_ref, mb_ref, *, tq):
    qi = pl.program_id(2)
    _flash_reset(m_ref, acc_ref)
    cbase = c_ref[0, 0, pl.ds(qi, 1), :][:, 0:1]

    def produce(buf, ki):
        k0 = pl.multiple_of(ki * tq, tq)
        s = _dot_nt(q_ref[0], k_ref[0, pl.ds(k0, tq), :])
        _put_logits(buf, s + LOG2E * (cbase - c_ref[0, 0, pl.ds(ki, 1), :]))

    def consume(buf, ki, masked):
        k0 = pl.multiple_of(ki * tq, tq)
        _flash_consume(buf, v_ref[0, pl.ds(k0, tq), :], m_ref, acc_ref, masked, tq)

    _tile_pipeline(qi, lambda t: t, produce, consume, (sa_ref, ma_ref), (sb_ref, mb_ref))
    o_ref[0] = _flash_result(acc_ref[...]).astype(o_ref.dtype)


def _fox_attention(proj3, c4):
    b, s, _ = proj3.shape
    tq = min(512, s)
    nk = s // tq
    return pl.pallas_call(
        functools.partial(_fox_attn_kernel, tq=tq),
        grid=(b, FOX_HEADS, s // tq),
        in_specs=[
            pl.BlockSpec((1, 1, nk, tq), lambda bi, h, qi: (bi, h, 0, 0)),
            pl.BlockSpec((1, tq, FOX_DH), lambda bi, h, qi: (bi, qi, PB_CQ + h)),
            pl.BlockSpec((1, s, FOX_DH), lambda bi, h, qi: (bi, 0, PB_CK + h)),
            pl.BlockSpec((1, s, FOX_DH), lambda bi, h, qi: (bi, 0, PB_CV + h)),
        ],
        out_specs=pl.BlockSpec((1, tq, FOX_DH), lambda bi, h, qi: (bi, qi, h)),
        out_shape=jax.ShapeDtypeStruct((b, s, FOX_HEADS * FOX_DH), CDT),
        scratch_shapes=_flash_scratch(tq, tq),
        compiler_params=_cparams(("parallel", "parallel", "arbitrary")),
        name="fox_attention",
    )(c4.reshape(b, FOX_HEADS, nk, tq), proj3, proj3, proj3)


def _nsa_compress_kernel(x_ref, w1a_ref, w1b_ref, pea_ref, peb_ref, w2_ref, o_ref):
    x = x_ref[0]
    n = x.shape[0]
    pa = _dot(x, w1a_ref[...])
    pb = _dot(x, w1b_ref[...])
    pe = _dot(pea_ref[...], w1a_ref[...]) + _dot(peb_ref[...], w1b_ref[...])
    hid = pa + pltpu.roll(pb, n - 1, axis=0) + pe[0:1]
    act = 0.5 * hid * (1.0 + jnp.tanh(math.sqrt(2.0 / math.pi) * (hid + 0.044715 * hid * hid * hid)))
    o_ref[0] = _dot(act.astype(CDT), w2_ref[...]).astype(o_ref.dtype)


def _nsa_compress(xc, w1a, w1b, pea, peb, w2):
    b, n, kdim = xc.shape
    hdim = w1a.shape[1]
    const = lambda shape: pl.BlockSpec(shape, lambda bi: (0,) * len(shape))
    return pl.pallas_call(
        _nsa_compress_kernel,
        grid=(b,),
        in_specs=[pl.BlockSpec((1, n, kdim), lambda bi: (bi, 0, 0)),
                  const((kdim, hdim)), const((kdim, hdim)), const((8, kdim)), const((8, kdim)),
                  const((hdim, w2.shape[1]))],
        out_specs=pl.BlockSpec((1, n, w2.shape[1]), lambda bi: (bi, 0, 0)),
        out_shape=jax.ShapeDtypeStruct((b, n, w2.shape[1]), CDT),
        compiler_params=_cparams(("parallel",)),
        name="nsa_compress",
    )(xc, w1a, w1b, pea, peb, w2)


def _nsa_cmp_kernel(slopes_ref, q_ref, kv_ref, oc_ref, sb_ref, used_ref, *, tq, n_topk):
    qi = pl.program_id(1)
    nblk = kv_ref.shape[1]
    q0 = qi * tq
    rowpos = q0 + lax.broadcasted_iota(jnp.int32, (tq, 1), 0)
    nidx = lax.broadcasted_iota(jnp.int32, (1, nblk), 1)
    disti = rowpos - (nidx * CMP_STRIDE + CMP_LEN - 1)
    valid = disti >= 0
    dist = disti.astype(F32)
    lane = lax.broadcasted_iota(jnp.int32, (tq, LANES), 1)
    low = lane < NSA_DH
    nn = lax.broadcasted_iota(jnp.int32, (nblk, LANES), 0) * CMP_STRIDE
    jj = (lax.broadcasted_iota(jnp.int32, (nblk, LANES), 1) & (NSA_DH - 1)) * SLC_LEN
    ov = (jnp.maximum(jnp.minimum(nn + CMP_LEN, jj + SLC_LEN) - jnp.maximum(nn, jj), 0).astype(F32)
          * (1.0 / CMP_LEN)).astype(CDT)
    jl = (lane & (NSA_DH - 1)).astype(F32)
    blk = (rowpos >> SLC_SHIFT).astype(F32)
    forced = (jl == 0.0) | (jl == blk) | (jl == blk - 1.0)
    outs = []
    bias = []
    for g in range(NSA_GROUPS):
        kc = kv_ref[0, :, g * LANES:(g + 1) * LANES]
        vc = kv_ref[0, :, (NSA_GROUPS + g) * LANES:(NSA_GROUPS + g + 1) * LANES]
        psum = jnp.zeros((tq, nblk), F32)
        for j in range(NSA_HPG):
            qb = q_ref[0, :, j * LANES:(j + 1) * LANES]
            qm = jnp.where(low if g == 0 else jnp.logical_not(low), qb, jnp.zeros_like(qb))
            s = _dot_nt(qm, kc) - slopes_ref[g * NSA_HPG + j] * dist
            s = jnp.where(valid, s, NEG)
            e = jnp.where(valid, jnp.exp2(s - jnp.max(s, axis=-1, keepdims=True)), 0.0)
            den = jnp.sum(e, axis=-1, keepdims=True)
            p = e / jnp.where(den > 0.0, den, 1.0)
            psum = psum + p
            outs.append(_dot(p.astype(CDT), vc))
        imp = _split_dot(psum, ov)
        imp = jnp.where(jl > blk, NEG_INF, jnp.where(forced, BIG, imp))
        sb = jnp.full((tq, LANES), NEG, F32)
        for _ in range(n_topk):
            mx = jnp.max(imp, axis=-1, keepdims=True)
            idx = jnp.min(jnp.where(imp == mx, jl, float(LANES)), axis=-1, keepdims=True)
            hit = jl == idx
            sb = jnp.where(hit, 0.0, sb)
            imp = jnp.where(hit, -jnp.inf, imp)
        bias.append(sb)
    sb = jnp.where(low, bias[1], bias[0])
    sb_ref[0] = sb.astype(sb_ref.dtype)
    used = jnp.max(jnp.where(sb == 0.0, 1.0, 0.0), axis=0, keepdims=True)
    used_ref[0, 0] = jnp.broadcast_to(used, used_ref.shape[2:])
    for blk_i in range(NSA_HEADS // 2):
        oc_ref[0, :, blk_i * LANES:(blk_i + 1) * LANES] = jnp.where(
            low, outs[2 * blk_i], outs[2 * blk_i + 1]).astype(oc_ref.dtype)


def _nsa_cmp_select(proj3, kvc, n_topk):
    b, s, _ = proj3.shape
    tq = min(256, s)
    nblk = kvc.shape[1]
    return pl.pallas_call(
        functools.partial(_nsa_cmp_kernel, tq=tq, n_topk=n_topk),
        grid=(b, s // tq),
        in_specs=[
            pl.BlockSpec(memory_space=pltpu.SMEM),
            pl.BlockSpec((1, tq, 4 * LANES), lambda bi, qi: (bi, qi, PB_DQ // 4)),
            pl.BlockSpec((1, nblk, kvc.shape[2]), lambda bi, qi: (bi, 0, 0)),
        ],
        out_specs=[
            pl.BlockSpec((1, tq, NSA_HEADS * NSA_DH), lambda bi, qi: (bi, qi, 0)),
            pl.BlockSpec((1, tq, LANES), lambda bi, qi: (bi, qi, 0)),
            pl.BlockSpec((1, 1, 8, LANES), lambda bi, qi: (bi, qi, 0, 0)),
        ],
        out_shape=[jax.ShapeDtypeStruct((b, s, NSA_HEADS * NSA_DH), CDT),
                   jax.ShapeDtypeStruct((b, s, LANES), CDT),
                   jax.ShapeDtypeStruct((b, s // tq, 8, LANES), F32)],
        compiler_params=_cparams(("parallel", "parallel")),
        name="nsa_cmp_select",
    )(jnp.asarray(_alibi_slopes(NSA_HEADS)), proj3, kvc)


def _compact_heads(heads, mine, low):
    both = [jnp.where(mine, a, pltpu.roll(a, NSA_DH, axis=1)) for a in heads]
    out = [jnp.where(low, both[2 * jj], both[2 * jj + 1]) for jj in range(NSA_HPG // 2)]
    return jnp.concatenate(out, axis=1)


def _nsa_win_kernel(slopes_ref, q_ref, kp_ref, kc_ref, vp_ref, vc_ref, o_ref, *, tq):
    g = pl.program_id(1)
    qi = pl.program_id(2)
    lane = lax.broadcasted_iota(jnp.int32, (tq, LANES), 1)
    low = lane < NSA_DH
    mine = (lane >> HALF_SHIFT) == g
    k = jnp.concatenate([kp_ref[0], kc_ref[0]], axis=0)
    v = jnp.concatenate([vp_ref[0], vc_ref[0]], axis=0)
    r = lax.broadcasted_iota(jnp.int32, (tq, 2 * tq), 0)
    c = lax.broadcasted_iota(jnp.int32, (tq, 2 * tq), 1)
    disti = r + tq - c
    valid = (disti >= 0) & (disti < WINDOW) & ((c >= tq) | (qi > 0))
    dist = disti.astype(F32)
    heads = []
    for j in range(NSA_HPG):
        qb = q_ref[0, :, j * LANES:(j + 1) * LANES]
        qm = jnp.where(mine, qb, jnp.zeros_like(qb))
        s = _dot_nt(qm, k) - slopes_ref[g * NSA_HPG + j] * dist
        s = jnp.where(valid, s, NEG)
        e = jnp.exp2(s - jnp.max(s, axis=-1, keepdims=True))
        p = e / jnp.sum(e, axis=-1, keepdims=True)
        heads.append(_dot(p.astype(CDT), v))
    o_ref[0] = _compact_heads(heads, mine, low).astype(o_ref.dtype)


def _nsa_window(proj3):
    b, s, _ = proj3.shape
    tq = WINDOW
    return pl.pallas_call(
        functools.partial(_nsa_win_kernel, tq=tq),
        grid=(b, NSA_GROUPS, s // tq),
        in_specs=[
            pl.BlockSpec(memory_space=pltpu.SMEM),
            pl.BlockSpec((1, tq, 4 * LANES), lambda bi, g, qi: (bi, qi, PB_DQ // 4)),
            pl.BlockSpec((1, tq, LANES), lambda bi, g, qi: (bi, jnp.maximum(qi - 1, 0), PB_WIN_K)),
            pl.BlockSpec((1, tq, LANES), lambda bi, g, qi: (bi, qi, PB_WIN_K)),
            pl.BlockSpec((1, tq, LANES), lambda bi, g, qi: (bi, jnp.maximum(qi - 1, 0), PB_WIN_V)),
            pl.BlockSpec((1, tq, LANES), lambda bi, g, qi: (bi, qi, PB_WIN_V)),
        ],
        out_specs=pl.BlockSpec((1, tq, NSA_HPG * NSA_DH), lambda bi, g, qi: (bi, qi, g)),
        out_shape=jax.ShapeDtypeStruct((b, s, NSA_HEADS * NSA_DH), CDT),
        compiler_params=_cparams(("parallel", "parallel", "parallel")),
        name="nsa_window",
    )(jnp.asarray(_alibi_slopes(NSA_HEADS)), proj3, proj3, proj3, proj3, proj3)


def _nsa_sel_kernel(slopes_ref, cnt_ref, idx_ref, q_ref, sb_ref, k_ref, v_ref, oc_ref, ow_ref, gl_ref, e_ref, o_ref,
                    qa_ref, m_ref, acc_ref, sa_ref, sb2_ref, ma_ref, mb_ref, *, tq):
    g = pl.program_id(1)
    qi = pl.program_id(2)
    lane = lax.broadcasted_iota(jnp.int32, (tq, LANES), 1)
    low = lane < NSA_DH
    mine = (lane >> HALF_SHIFT) == g
    sb = sb_ref[0]
    for j in range(NSA_HPG):
        qa_ref[j * tq:(j + 1) * tq] = jnp.where(mine, q_ref[0, :, j * LANES:(j + 1) * LANES], sb)
    _flash_reset(m_ref, acc_ref)
    col = lax.broadcasted_iota(jnp.int32, (1, tq), 1).astype(F32)
    jl = lane & (NSA_DH - 1)
    krow = lax.broadcasted_iota(jnp.int32, (tq, LANES), 0)

    def produce(buf, ki):
        k0 = pl.multiple_of(ki * tq, tq)
        k = k_ref[0, pl.ds(k0, tq), :]
        onehot = jnp.where(((k0 + krow) >> SLC_SHIFT) == jl, 1.0, 0.0).astype(k.dtype)
        s_all = _dot_nt(qa_ref[...], jnp.where(mine, k, onehot))
        rel = ((ki - qi) * tq).astype(F32)
        for j in range(NSA_HPG):
            rows = slice(j * tq, (j + 1) * tq)
            _put_logits(buf, s_all[rows] + slopes_ref[g * NSA_HPG + j] * (col + rel), rows)

    def consume(buf, ki, masked):
        k0 = pl.multiple_of(ki * tq, tq)
        _flash_consume(buf, v_ref[0, pl.ds(k0, tq), :], m_ref, acc_ref, masked, tq)

    nq = pl.num_programs(2)
    row = (pl.program_id(0) * NSA_GROUPS + g) * nq + qi
    _tile_pipeline(cnt_ref[row], lambda t: idx_ref[row * nq + t], produce, consume,
                   (sa_ref, ma_ref), (sb2_ref, mb_ref))

    o = _flash_result(acc_ref[...])
    o_s = _compact_heads([o[j * tq:(j + 1) * tq] for j in range(NSA_HPG)], mine, low)
    gates = _split_dot(jax.nn.sigmoid(gl_ref[0]), e_ref[0])
    w = NSA_HPG * NSA_DH
    y = (gates[:, 0:w] * oc_ref[0].astype(F32) + gates[:, w:2 * w] * o_s
         + gates[:, 2 * w:3 * w] * ow_ref[0].astype(F32))
    o_ref[0] = y.astype(o_ref.dtype)


def _nsa_selected(proj3, sbias, used, o_c, o_w, small3, expand):
    b, s, _ = proj3.shape
    tq = min(256, s)
    nq = s // tq
    w = NSA_HPG * NSA_DH
    u = used[:, :, 0, :].reshape(b, nq, NSA_GROUPS, NSA_DH)[:, :, ::-1, :nq * (tq // SLC_LEN)]
    flags = (u.reshape(b, nq, NSA_GROUPS, nq, tq // SLC_LEN).max(axis=-1) > 0.0).astype(jnp.int32)
    flags = flags.transpose(0, 2, 1, 3)
    qt = jnp.arange(nq, dtype=jnp.int32)
    flags = flags * (qt[None, :] < qt[:, None]).astype(jnp.int32)
    cnt = flags.sum(axis=-1)
    idx = jnp.argsort(1 - flags, axis=-1, stable=True).astype(jnp.int32)
    idx = jnp.where(qt == cnt[..., None], qt[:, None], idx)
    return pl.pallas_call(
        functools.partial(_nsa_sel_kernel, tq=tq),
        grid=(b, NSA_GROUPS, nq),
        in_specs=[
            pl.BlockSpec(memory_space=pltpu.SMEM),
            pl.BlockSpec(memory_space=pltpu.SMEM),
            pl.BlockSpec(memory_space=pltpu.SMEM),
            pl.BlockSpec((1, tq, 4 * LANES), lambda bi, g, qi: (bi, qi, PB_DQ // 4)),
            pl.BlockSpec((1, tq, LANES), lambda bi, g, qi: (bi, qi, 0)),
            pl.BlockSpec((1, s, LANES), lambda bi, g, qi: (bi, 0, PB_SEL_K)),
            pl.BlockSpec((1, s, LANES), lambda bi, g, qi: (bi, 0, PB_SEL_V)),
            pl.BlockSpec((1, tq, w), lambda bi, g, qi: (bi, qi, g)),
            pl.BlockSpec((1, tq, w), lambda bi, g, qi: (bi, qi, g)),
            pl.BlockSpec((1, tq, LANES), lambda bi, g, qi: (bi, qi, 0)),
            pl.BlockSpec((1, LANES, 3 * w), lambda bi, g, qi: (g, 0, 0)),
        ],
        out_specs=pl.BlockSpec((1, tq, w), lambda bi, g, qi: (bi, qi, g)),
        out_shape=jax.ShapeDtypeStruct((b, s, NSA_HEADS * NSA_DH), CDT),
        scratch_shapes=[pltpu.VMEM((NSA_HPG * tq, LANES), CDT)] + _flash_scratch(NSA_HPG * tq, tq),
        compiler_params=_cparams(("parallel", "parallel", "arbitrary")),
        name="nsa_selected",
    )(jnp.asarray(_alibi_slopes(NSA_HEADS)), cnt.reshape(-1), idx.reshape(-1), proj3, sbias, proj3, proj3,
      o_c, o_w, small3, expand)


def _merge_kernel(ya_ref, yb_ref, yc_ref, yd_ref, ga_ref, gb_ref, gc_ref, gd_ref, wb_ref, wo_ref, x_ref, o_ref):
    merged = None
    for n, (y_ref, g_ref) in enumerate(((ya_ref, ga_ref), (yb_ref, gb_ref), (yc_ref, gc_ref), (yd_ref, gd_ref))):
        t = jax.nn.sigmoid(g_ref[...].astype(F32)) * _dot(y_ref[...], wb_ref[n])
        merged = t if merged is None else merged + t
    o_ref[...] = x_ref[...] + _dot(merged.astype(CDT), wo_ref[...])


def _merge(ys, proj2, wb, wo, x2):
    t, d = x2.shape
    tm = min(512, t)
    gate_blk = PB_GATE * LANES // d
    yspec = pl.BlockSpec((tm, BRANCH_WIDTH), lambda i: (i, 0))
    gspecs = [pl.BlockSpec((tm, d), functools.partial(lambda i, n: (i, gate_blk + n), n=n)) for n in range(N_BRANCH)]
    return pl.pallas_call(
        _merge_kernel,
        grid=(t // tm,),
        in_specs=[yspec] * N_BRANCH + gspecs + [
            pl.BlockSpec((N_BRANCH, BRANCH_WIDTH, d), lambda i: (0, 0, 0)),
            pl.BlockSpec((d, d), lambda i: (0, 0)),
            pl.BlockSpec((tm, d), lambda i: (i, 0)),
        ],
        out_specs=pl.BlockSpec((tm, d), lambda i: (i, 0)),
        out_shape=jax.ShapeDtypeStruct((t, d), F32),
        compiler_params=_cparams(("parallel",)),
        name="merge",
    )(*ys, proj2, proj2, proj2, proj2, wb, wo, x2)


HALO = 16


def _ffn_kernel(x_ref, xh_ref, g_ref, wu_ref, cw_ref, cb_ref, wd_ref, gf_ref, o_ref, he_ref, u_ref, act_ref,
                *, tm, fc, final):
    i = pl.program_id(1)
    x = x_ref[0]
    g = g_ref[...]
    xh = xh_ref[0] * (i > 0).astype(F32)
    he_ref[0:HALO] = _rms(xh, g).astype(CDT)
    he_ref[HALO:HALO + tm] = _rms(x, g).astype(CDT)
    he = he_ref[...]
    for c in range(D_FF // fc):
        outs = []
        for half in range(2):
            ub = u_ref.at[c % 2, half]
            lo = half * D_FF + c * fc
            ub[...] = _dot(he, wu_ref[:, lo:lo + fc])
            conv = cb_ref[:, lo:lo + fc]
            for kk in range(CONV_WIDTH):
                off = HALO - (CONV_WIDTH - 1) + kk
                conv = conv + cw_ref[kk:kk + 1, lo:lo + fc] * ub[off:off + tm, :]
            outs.append(conv)
        a, gg = outs
        act_ref[:, c * fc:(c + 1) * fc] = (a * jax.nn.sigmoid(a) * gg).astype(CDT)
    y = x + _dot(act_ref[...], wd_ref[...])
    if final:
        y = _rms(y, gf_ref[...])
    o_ref[0] = y


def _ffn(x3, g, wu, cw, cb, wd, gf, final):
    b, s, d = x3.shape
    tm = min(512, s)
    fc = 256
    const = lambda shape: pl.BlockSpec(shape, lambda bi, i: (0,) * len(shape), pipeline_mode=pl.Buffered(1))
    return pl.pallas_call(
        functools.partial(_ffn_kernel, tm=tm, fc=fc, final=final),
        grid=(b, s // tm),
        in_specs=[
            pl.BlockSpec((1, tm, d), lambda bi, i: (bi, i, 0)),
            pl.BlockSpec((1, HALO, d), lambda bi, i: (bi, jnp.maximum(i * (tm // HALO) - 1, 0), 0)),
            const((1, d)), const((d, 2 * D_FF)), const((CONV_WIDTH, 2 * D_FF)), const((1, 2 * D_FF)),
            const((D_FF, d)), const((1, d)),
        ],
        out_specs=pl.BlockSpec((1, tm, d), lambda bi, i: (bi, i, 0)),
        out_shape=jax.ShapeDtypeStruct((b, s, d), F32),
        scratch_shapes=[pltpu.VMEM((tm + HALO, d), CDT), pltpu.VMEM((2, 2, tm + HALO, fc), F32),
                        pltpu.VMEM((tm, D_FF), CDT)],
        compiler_params=_cparams(("parallel", "arbitrary")),
        name="conv_glu_mlp",
    )(x3, x3, g.reshape(1, d), wu, cw, cb.reshape(1, -1), wd, gf.reshape(1, d))


def _prep_w_in(w):
    widths = (512, 512, 512, MLA_Q_LORA, MLA_KV_LORA, MLA_ROPE, 512, 512, 512, FOX_HEADS,
              512, 768, 3 * NSA_HEADS, N_BRANCH * D_MODEL)
    offs = np.cumsum((0,) + widths)
    (a_q, a_k, a_v, b_cq, b_ckv, b_kr, c_q, c_k, c_v, c_f, d_q, d_kv, d_g, gate) = [
        w[:, offs[i]:offs[i + 1]] for i in range(len(widths))]
    d = w.shape[0]
    d_q = d_q.reshape(d, NSA_GROUPS, NSA_HPG, NSA_DH).transpose(0, 2, 1, 3).reshape(d, 512)
    half = MLA_ROPE // 2
    kr_swap = jnp.concatenate([-b_kr[:, half:], b_kr[:, :half]], axis=1)
    z64 = jnp.zeros((d, LANES - MLA_ROPE), w.dtype)
    big = jnp.concatenate([
        a_q * (LOG2E * DIFF_DH ** -0.5), a_k, a_v,
        c_q * (LOG2E * FOX_DH ** -0.5), c_k, c_v,
        d_q * (LOG2E * NSA_DH ** -0.5), d_kv,
        b_cq, b_ckv, b_kr, z64, kr_swap, z64,
        gate], axis=1)
    small = jnp.concatenate([c_f, d_g, jnp.zeros((d, LANES - FOX_HEADS - 3 * NSA_HEADS), w.dtype)], axis=1)
    return big.astype(CDT), small.astype(CDT)


def _prep_mla(w_uq, w_ukv):
    r = w_uq.shape[0]
    hw = 2 * LANES
    half = MLA_ROPE // 2
    scale = LOG2E * (MLA_NOPE + MLA_ROPE) ** -0.5
    wq = (w_uq * scale).reshape(r, MLA_HEADS, MLA_NOPE + MLA_ROPE)
    nope, t1, t2 = wq[..., :MLA_NOPE], wq[..., MLA_NOPE:MLA_NOPE + half], wq[..., MLA_NOPE + half:]
    zpad = jnp.zeros((r, MLA_HEADS, hw - MLA_NOPE - MLA_ROPE), w_uq.dtype)
    wqm = jnp.concatenate([nope, t1, t2, zpad], axis=-1).reshape(r, MLA_HEADS * hw)
    wqs = jnp.concatenate([jnp.zeros_like(nope), -t2, t1, zpad], axis=-1).reshape(r, MLA_HEADS * hw)
    wkv = w_ukv.reshape(w_ukv.shape[0], MLA_HEADS, MLA_NOPE + MLA_VDIM)
    wk = wkv[..., :MLA_NOPE].reshape(-1, MLA_HEADS * MLA_NOPE)
    wv = wkv[..., MLA_NOPE:].reshape(-1, MLA_HEADS * MLA_VDIM)
    return wqm.astype(CDT), wqs.astype(CDT), wk.astype(CDT), wv.astype(CDT)


def _rope_tables(s):
    half = MLA_ROPE // 2
    inv_freq = ROPE_THETA ** (-jnp.arange(0, MLA_ROPE, 2, dtype=F32) / MLA_ROPE)
    ang = jnp.arange(s, dtype=F32)[:, None] * inv_freq[None, :]
    cos, sin = jnp.cos(ang), jnp.sin(ang)
    z = jnp.zeros((s, LANES - MLA_ROPE), F32)
    cosk = jnp.concatenate([cos, cos, z], axis=1)
    sink = jnp.concatenate([sin, sin, z], axis=1)
    cosq = jnp.concatenate([jnp.ones((s, MLA_NOPE), F32), cosk], axis=1)
    sinq = jnp.concatenate([jnp.zeros((s, MLA_NOPE), F32), sink], axis=1)
    return cosq, sinq, cosk, sink


def _prep_compress(pe, w1, w2):
    eye2 = jnp.eye(2, dtype=F32)
    w1r = w1.reshape(2, CMP_LEN, NSA_DH, CMP_HIDDEN)

    def expand(wpart):
        t = jnp.einsum('kpdh,kK,gG->pkgdKGh', wpart, eye2, eye2)
        return t.reshape(CMP_STRIDE * 4 * NSA_DH, 4 * CMP_HIDDEN)

    w1a, w1b = expand(w1r[:, :CMP_STRIDE]), expand(w1r[:, CMP_STRIDE:])

    def pe_row(p):
        t = jnp.broadcast_to(p.transpose(1, 0, 2)[:, :, None, :], (CMP_STRIDE, 2, NSA_GROUPS, NSA_DH))
        return jnp.pad(t.reshape(1, -1), ((0, 7), (0, 0)))

    pea, peb = pe_row(pe[:, :CMP_STRIDE]), pe_row(pe[:, CMP_STRIDE:])
    w2b = jnp.einsum('khd,kK,gG,u->kghKGud', w2, eye2, eye2, jnp.ones((2,), F32))
    w2b = w2b.reshape(4 * CMP_HIDDEN, 4 * 2 * NSA_DH)
    return w1a.astype(CDT), w1b.astype(CDT), pea.astype(CDT), peb.astype(CDT), w2b.astype(CDT)


def _gate_expand():
    e = np.zeros((NSA_GROUPS, LANES, 3, NSA_HPG, NSA_DH), np.float32)
    for g in range(NSA_GROUPS):
        for j in range(NSA_HPG):
            for br in range(3):
                e[g, SMALL_G + (g * NSA_HPG + j) * 3 + br, br, j, :] = 1.0
    return jnp.asarray(e.reshape(NSA_GROUPS, LANES, 3 * NSA_HPG * NSA_DH)).astype(CDT)


def _token_mixers(x3, l, norm_mix, w_in, diff_lambda, diff_subln, mla_norm_q, mla_w_uq, mla_norm_kv, mla_w_ukv,
                  fox_b_f, nsa_cmp_pe, nsa_cmp_w1, nsa_cmp_w2, w_branch, w_out, rope_tabs):
    b, s, d = x3.shape
    t = b * s
    x2 = x3.reshape(t, d)
    w_big, w_small = _prep_w_in(w_in)
    proj, small = _in_proj(x2, norm_mix, w_big, w_small)
    proj3 = proj.reshape(b, s, N_PROJ)
    small3 = small.reshape(b, s, LANES)

    lam_init = 0.8 - 0.6 * math.exp(-0.3 * l)
    y_a = _diff_attention(proj3, diff_lambda, diff_subln, lam_init)

    wqm, wqs, wk, wv = _prep_mla(mla_w_uq, mla_w_ukv)
    qc, kc, vv = _mla_prep(proj3, mla_norm_q, mla_norm_kv, wqm, wqs, wk, wv, rope_tabs)
    y_b = _mla_attention(qc, kc, vv)

    cf_rows = small3[:, :, SMALL_F:SMALL_F + FOX_HEADS].transpose(0, 2, 1).reshape(b * FOX_HEADS, s)
    bias_rows = jnp.tile(fox_b_f.astype(F32), b).reshape(b * FOX_HEADS, 1)
    c4 = _fox_cumsum(cf_rows, bias_rows)
    y_c = _fox_attention(proj3, c4)

    w1a, w1b, pea, peb, w2b = _prep_compress(nsa_cmp_pe, nsa_cmp_w1, nsa_cmp_w2)
    xc = proj3[:, :, PB_CMP_K * LANES:(PB_CMP_V + 1) * LANES].reshape(b, s // CMP_STRIDE, CMP_STRIDE * 2 * LANES)
    kvc = _nsa_compress(xc, w1a, w1b, pea, peb, w2b)
    n_topk = min(SLC_TOPK, s // SLC_LEN)
    o_c, sbias, used = _nsa_cmp_select(proj3, kvc, n_topk)
    o_w = _nsa_window(proj3)
    y_d = _nsa_selected(proj3, sbias, used, o_c, o_w, small3, _gate_expand())

    ys = [y.reshape(t, BRANCH_WIDTH) for y in (y_a, y_b, y_c, y_d)]
    return _merge(ys, proj, w_branch.astype(CDT), w_out.astype(CDT), x2).reshape(b, s, d)


def kernel(x, norm_mix, w_in, diff_lambda, diff_subln, mla_norm_q, mla_w_uq, mla_norm_kv, mla_w_ukv, fox_b_f,
           nsa_cmp_pe, nsa_cmp_w1, nsa_cmp_w2, w_branch, w_out, norm_ffn, w_up, conv_w, conv_b, w_down, norm_final):
    depth = w_in.shape[0]
    s = x.shape[1]
    rope_tabs = _rope_tables(s)
    for l in range(depth):
        x = _token_mixers(x, l, norm_mix[l], w_in[l], diff_lambda[l], diff_subln[l], mla_norm_q[l], mla_w_uq[l],
                          mla_norm_kv[l], mla_w_ukv[l], fox_b_f[l], nsa_cmp_pe[l], nsa_cmp_w1[l], nsa_cmp_w2[l],
                          w_branch[l], w_out[l], rope_tabs)
        x = _ffn(x, norm_ffn[l], w_up[l].astype(CDT), conv_w[l], conv_b[l], w_down[l].astype(CDT), norm_final,
                 final=(l == depth - 1))
    return x
```

```python
import functools
import math

import numpy as np
import jax
import jax.numpy as jnp
from jax import lax
from jax.experimental import pallas as pl
from jax.experimental.pallas import tpu as pltpu

F32 = jnp.float32
CDT = jnp.bfloat16

NEG = -1e30
NEG_INF = -1e30
BIG = 1e9
NORM_EPS = 1e-6
LOG2E = 1.4426950408889634
LANES = 128

D_MODEL = 1024
DIFF_HEADS, DIFF_DH = 4, 64
MLA_HEADS, MLA_NOPE, MLA_ROPE, MLA_VDIM = 4, 128, 64, 128
MLA_Q_LORA, MLA_KV_LORA = 256, 256
ROPE_THETA = 10000.0
FOX_HEADS, FOX_DH = 4, 128
NSA_HEADS, NSA_GROUPS, NSA_DH = 8, 2, 64
NSA_HPG = NSA_HEADS // NSA_GROUPS
CMP_STRIDE = 16
CMP_LEN = 2 * CMP_STRIDE
CMP_HIDDEN = 128
SLC_LEN = 64
SLC_SHIFT = 6
HALF_SHIFT = 6
SLC_TOPK = 8
WINDOW = 256
N_BRANCH = 4
BRANCH_WIDTH = 512
D_FF = 2816
CONV_WIDTH = 3

PB_AQ, PB_AK, PB_AV = 0, 4, 8
PB_CQ, PB_CK, PB_CV = 12, 16, 20
PB_DQ = 24
PB_CMP_K, PB_CMP_V, PB_SEL_K, PB_SEL_V, PB_WIN_K, PB_WIN_V = 28, 29, 30, 31, 32, 33
PB_BCQ, PB_BCKV, PB_BKR, PB_BKRS = 34, 36, 38, 39
PB_GATE = 40
N_PROJ = 72 * LANES
SMALL_F, SMALL_G = 0, 4

VMEM_LIMIT = 56 * 1024 * 1024


def _cparams(sem):
    return pltpu.CompilerParams(dimension_semantics=sem, vmem_limit_bytes=VMEM_LIMIT)


def _rms(xf, g):
    return xf * lax.rsqrt(jnp.mean(xf * xf, axis=-1, keepdims=True) + NORM_EPS) * g


def _dot(a, b):
    return jnp.dot(a, b, preferred_element_type=F32)


def _dot_nt(a, b):
    return lax.dot_general(a, b, (((1,), (1,)), ((), ())), preferred_element_type=F32)


def _split_dot(a, b):
    hi = a.astype(CDT)
    lo = (a - hi.astype(F32)).astype(CDT)
    return _dot(hi, b) + _dot(lo, b)


def _alibi_slopes(n):
    return (LOG2E * np.exp2(-8.0 * np.arange(1, n + 1) / n)).astype(np.float32)


def _inproj_kernel(x_ref, g_ref, w_ref, ws_ref, o_ref, os_ref, h_ref):
    @pl.when(pl.program_id(1) == 0)
    def _():
        h = _rms(x_ref[...], g_ref[...]).astype(CDT)
        h_ref[...] = h
        os_ref[...] = _dot(h, ws_ref[...])

    o_ref[...] = _dot(h_ref[...], w_ref[...]).astype(o_ref.dtype)


def _in_proj(x2, g, w, ws):
    t, d = x2.shape
    n = w.shape[1]
    tm = min(1024, t)
    tn = 1024
    return pl.pallas_call(
        _inproj_kernel,
        grid=(t // tm, n // tn),
        in_specs=[
            pl.BlockSpec((tm, d), lambda i, j: (i, 0)),
            pl.BlockSpec((1, d), lambda i, j: (0, 0)),
            pl.BlockSpec((d, tn), lambda i, j: (0, j)),
            pl.BlockSpec((d, LANES), lambda i, j: (0, 0)),
        ],
        out_specs=[
            pl.BlockSpec((tm, tn), lambda i, j: (i, j)),
            pl.BlockSpec((tm, LANES), lambda i, j: (i, 0)),
        ],
        out_shape=[jax.ShapeDtypeStruct((t, n), CDT), jax.ShapeDtypeStruct((t, LANES), F32)],
        scratch_shapes=[pltpu.VMEM((tm, d), CDT)],
        compiler_params=_cparams(("parallel", "arbitrary")),
        name="in_proj",
    )(x2, g.reshape(1, d), w, ws)


def _fox_cumsum_kernel(cf_ref, bf_ref, o_ref):
    rows, s = cf_ref.shape
    lane = lax.broadcasted_iota(jnp.int32, (rows, LANES), 1)
    carry = jnp.zeros((rows, 1), F32)
    for c in range(s // LANES):
        z = cf_ref[:, c * LANES:(c + 1) * LANES] + bf_ref[...]
        xs = jnp.minimum(z, 0.0) - jnp.log1p(jnp.exp(-jnp.abs(z)))
        d = 1
        while d < LANES:
            xs = xs + jnp.where(lane >= d, pltpu.roll(xs, d, axis=1), 0.0)
            d *= 2
        xs = xs + carry
        o_ref[:, c * LANES:(c + 1) * LANES] = xs
        carry = xs[:, LANES - 1:LANES]


def _fox_cumsum(cf_rows, bias_rows):
    return pl.pallas_call(
        _fox_cumsum_kernel,
        out_shape=jax.ShapeDtypeStruct(cf_rows.shape, F32),
        name="fox_cumsum",
    )(cf_rows, bias_rows)


def _flash_scratch(rows, tk):
    return [pltpu.VMEM((rows, LANES), F32), pltpu.VMEM((rows, 2 * LANES), F32),
            pltpu.VMEM((rows, tk), F32), pltpu.VMEM((rows, tk), F32),
            pltpu.VMEM((rows, LANES), F32), pltpu.VMEM((rows, LANES), F32)]


def _flash_reset(m_ref, acc_ref):
    m_ref[...] = jnp.full(m_ref.shape, NEG, F32)
    acc_ref[...] = jnp.zeros(acc_ref.shape, F32)


def _row_max(s):
    return jnp.broadcast_to(jnp.max(s, axis=-1, keepdims=True), (s.shape[0], LANES))


def _put_logits(buf, s, rows=slice(None)):
    buf[0][rows] = s
    buf[1][rows] = _row_max(s)


def _with_ones(v):
    return jnp.concatenate([v, jnp.ones((v.shape[0], LANES), v.dtype)], axis=1)


def _flash_consume(buf, v, m_ref, acc_ref, masked, tq):
    s = buf[0][...]
    m_cur = buf[1][...]
    if masked:
        r = lax.broadcasted_iota(jnp.int32, s.shape, 0) & (tq - 1)
        c = lax.broadcasted_iota(jnp.int32, s.shape, 1)
        s = jnp.where(c <= r, s, NEG)
        m_cur = _row_max(s)
    m_old = m_ref[...]
    m_new = jnp.maximum(m_old, m_cur)
    alpha = jnp.exp2(m_old - m_new)
    p = jnp.exp2(s - jnp.tile(m_new, (1, s.shape[1] // LANES))).astype(CDT)
    acc_ref[...] = jnp.tile(alpha, (1, 2)) * acc_ref[...] + _dot(p, _with_ones(v))
    m_ref[...] = m_new


def _flash_result(acc):
    return acc[:, :LANES] / acc[:, LANES:]


def _causal_schedule(nq):
    ent = [(qi, ki, int(ki == qi)) for qi in range(nq) for ki in range(qi + 1)]
    n = len(ent)
    a = np.asarray(ent + [ent[-1]] * 2, np.int32)
    return n, tuple(jnp.asarray(a[:, i]) for i in range(3))


def _flash_stream(n, sched, base, produce, consume, finish, buf_a, buf_b):
    qt, kt, lt = sched

    def step(cur, nxt, t, diag, with_produce=True):
        if with_produce:
            produce(nxt, qt[base + t + 1], kt[base + t + 1])
        consume(cur, qt[base + t], kt[base + t], diag)
        if diag:
            finish(qt[base + t])

    produce(buf_a, qt[base], kt[base])

    def pair(j, c):
        t = 2 * j
        l0, l1 = lt[base + t], lt[base + t + 1]
        for d0 in (False, True):
            for d1 in (False, True):
                @pl.when(((l0 != 0) == d0) & ((l1 != 0) == d1))
                def _():
                    step(buf_a, buf_b, t, d0)
                    step(buf_b, buf_a, t + 1, d1)
        return c

    lax.fori_loop(0, n // 2, pair, 0)

    def tail():
        step(buf_a, buf_b, n - 1, True, with_produce=False)

    if isinstance(n, int):
        if n % 2 == 1:
            tail()
    else:
        pl.when(n % 2 == 1)(tail)


def _tile(ref, i, t):
    return ref[0, pl.ds(pl.multiple_of(i * t, t), t), :]


def _diff_attn_kernel(qt_ref, kt_ref, lt_ref, slopes_ref, lam_ref, g_ref, q_ref, k_ref, v_ref, o_ref,
                      m_ref, acc_ref, sa_ref, sb_ref, ma_ref, mb_ref, *, tq, n, lam_init):
    slope = slopes_ref[pl.program_id(1)]
    _flash_reset(m_ref, acc_ref)
    col = lax.broadcasted_iota(jnp.int32, (1, tq), 1).astype(F32)
    lane = lax.broadcasted_iota(jnp.int32, (tq, LANES), 1)
    lf = lam_ref[...]
    lam = (jnp.exp(jnp.sum(lf[0:1] * lf[1:2], axis=-1, keepdims=True))
           - jnp.exp(jnp.sum(lf[2:3] * lf[3:4], axis=-1, keepdims=True)) + lam_init)

    def produce(buf, qi, ki):
        q = _tile(q_ref, qi, tq)
        zero = jnp.zeros_like(q)
        qq = jnp.concatenate([jnp.where(lane < DIFF_DH, q, zero), jnp.where(lane >= DIFF_DH, q, zero)], axis=0)
        s = _dot_nt(qq, _tile(k_ref, ki, tq))
        _put_logits(buf, s + slope * (col + ((ki - qi) * tq).astype(F32)))

    def consume(buf, qi, ki, masked):
        _flash_consume(buf, _tile(v_ref, ki, tq), m_ref, acc_ref, masked, tq)

    def finish(qi):
        o = _flash_result(acc_ref[...])
        d = o[0:tq] - lam * o[tq:2 * tq]
        o_ref[0, pl.ds(pl.multiple_of(qi * tq, tq), tq), :] = (
            _rms(d, g_ref[...]) * (1.0 - lam_init)).astype(o_ref.dtype)
        _flash_reset(m_ref, acc_ref)

    _flash_stream(n, (qt_ref, kt_ref, lt_ref), 0, produce, consume, finish, (sa_ref, ma_ref), (sb_ref, mb_ref))


_SMEM = pl.BlockSpec(memory_space=pltpu.SMEM)


def _diff_attention(proj3, diff_lambda, subln, lam_init):
    b, s, _ = proj3.shape
    tq = min(512, s)
    dv = 2 * DIFF_DH
    n, sched = _causal_schedule(s // tq)
    kern = functools.partial(_diff_attn_kernel, tq=tq, n=n, lam_init=lam_init)
    return pl.pallas_call(
        kern,
        grid=(b, DIFF_HEADS),
        in_specs=[
            _SMEM, _SMEM, _SMEM, _SMEM,
            pl.BlockSpec((4, DIFF_DH), lambda bi, h: (0, 0)),
            pl.BlockSpec((1, dv), lambda bi, h: (0, 0)),
            pl.BlockSpec((1, s, LANES), lambda bi, h: (bi, 0, PB_AQ + h)),
            pl.BlockSpec((1, s, LANES), lambda bi, h: (bi, 0, PB_AK + h)),
            pl.BlockSpec((1, s, LANES), lambda bi, h: (bi, 0, PB_AV + h)),
        ],
        out_specs=pl.BlockSpec((1, s, dv), lambda bi, h: (bi, 0, h)),
        out_shape=jax.ShapeDtypeStruct((b, s, DIFF_HEADS * dv), CDT),
        scratch_shapes=_flash_scratch(2 * tq, tq),
        compiler_params=_cparams(("parallel", "parallel")),
        name="diff_attention",
    )(*sched, jnp.asarray(_alibi_slopes(DIFF_HEADS)), diff_lambda, subln.reshape(1, dv), proj3, proj3, proj3)


def _mla_prep_kernel(cq_ref, ckv_ref, kr_ref, krs_ref, gq_ref, gkv_ref, wqm_ref, wqs_ref, wk_ref, wv_ref,
                     cosq_ref, sinq_ref, cosk_ref, sink_ref, q_ref, k_ref, v_ref):
    hq = _rms(cq_ref[0].astype(F32), gq_ref[...]).astype(CDT)
    qm = _dot(hq, wqm_ref[...])
    qs = _dot(hq, wqs_ref[...])
    cosq, sinq = cosq_ref[...], sinq_ref[...]
    hw = 2 * LANES
    for h in range(MLA_HEADS):
        sl = slice(h * hw, (h + 1) * hw)
        q_ref[0, :, sl] = (qm[:, sl] * cosq + qs[:, sl] * sinq).astype(q_ref.dtype)
    hkv = _rms(ckv_ref[0].astype(F32), gkv_ref[...]).astype(CDT)
    kn = _dot(hkv, wk_ref[...])
    v_ref[0] = _dot(hkv, wv_ref[...]).astype(v_ref.dtype)
    kpe = (kr_ref[0].astype(F32) * cosk_ref[...] + krs_ref[0].astype(F32) * sink_ref[...]).astype(k_ref.dtype)
    for h in range(MLA_HEADS):
        k_ref[0, :, h * hw:h * hw + LANES] = kn[:, h * LANES:(h + 1) * LANES].astype(k_ref.dtype)
        k_ref[0, :, h * hw + LANES:(h + 1) * hw] = kpe


def _mla_prep(proj3, gq, gkv, wqm, wqs, wk, wv, tabs):
    b, s, _ = proj3.shape
    tm = min(512, s)
    hw = 2 * LANES
    cosq, sinq, cosk, sink = tabs
    const = lambda shape: pl.BlockSpec(shape, lambda bi, i: (0,) * len(shape))
    return pl.pallas_call(
        _mla_prep_kernel,
        grid=(b, s // tm),
        in_specs=[
            pl.BlockSpec((1, tm, MLA_Q_LORA), lambda bi, i: (bi, i, PB_BCQ // 2)),
            pl.BlockSpec((1, tm, MLA_KV_LORA), lambda bi, i: (bi, i, PB_BCKV // 2)),
            pl.BlockSpec((1, tm, LANES), lambda bi, i: (bi, i, PB_BKR)),
            pl.BlockSpec((1, tm, LANES), lambda bi, i: (bi, i, PB_BKRS)),
            const((1, MLA_Q_LORA)), const((1, MLA_KV_LORA)),
            const((MLA_Q_LORA, MLA_HEADS * hw)), const((MLA_Q_LORA, MLA_HEADS * hw)),
            const((MLA_KV_LORA, MLA_HEADS * MLA_NOPE)), const((MLA_KV_LORA, MLA_HEADS * MLA_VDIM)),
            pl.BlockSpec((tm, hw), lambda bi, i: (i, 0)), pl.BlockSpec((tm, hw), lambda bi, i: (i, 0)),
            pl.BlockSpec((tm, LANES), lambda bi, i: (i, 0)), pl.BlockSpec((tm, LANES), lambda bi, i: (i, 0)),
        ],
        out_specs=[
            pl.BlockSpec((1, tm, MLA_HEADS * hw), lambda bi, i: (bi, i, 0)),
            pl.BlockSpec((1, tm, MLA_HEADS * hw), lambda bi, i: (bi, i, 0)),
            pl.BlockSpec((1, tm, MLA_HEADS * MLA_VDIM), lambda bi, i: (bi, i, 0)),
        ],
        out_shape=[
            jax.ShapeDtypeStruct((b, s, MLA_HEADS * hw), CDT),
            jax.ShapeDtypeStruct((b, s, MLA_HEADS * hw), CDT),
            jax.ShapeDtypeStruct((b, s, MLA_HEADS * MLA_VDIM), CDT),
        ],
        compiler_params=_cparams(("parallel", "parallel")),
        name="mla_prep",
    )(proj3, proj3, proj3, proj3, gq.reshape(1, -1), gkv.reshape(1, -1), wqm, wqs, wk, wv,
      cosq, sinq, cosk, sink)


def _plain_attn_kernel(qt_ref, kt_ref, lt_ref, q_ref, k_ref, v_ref, o_ref,
                       m_ref, acc_ref, sa_ref, sb_ref, ma_ref, mb_ref, *, tq, n):
    _flash_reset(m_ref, acc_ref)

    def produce(buf, qi, ki):
        _put_logits(buf, _dot_nt(_tile(q_ref, qi, tq), _tile(k_ref, ki, tq)))

    def consume(buf, qi, ki, masked):
        _flash_consume(buf, _tile(v_ref, ki, tq), m_ref, acc_ref, masked, tq)

    def finish(qi):
        o_ref[0, pl.ds(pl.multiple_of(qi * tq, tq), tq), :] = _flash_result(acc_ref[...]).astype(o_ref.dtype)
        _flash_reset(m_ref, acc_ref)

    _flash_stream(n, (qt_ref, kt_ref, lt_ref), 0, produce, consume, finish, (sa_ref, ma_ref), (sb_ref, mb_ref))


def _mla_attention(qc, kc, v):
    b, s, _ = qc.shape
    tq = min(512, s)
    hw = 2 * LANES
    n, sched = _causal_schedule(s // tq)
    return pl.pallas_call(
        functools.partial(_plain_attn_kernel, tq=tq, n=n),
        grid=(b, MLA_HEADS),
        in_specs=[
            _SMEM, _SMEM, _SMEM,
            pl.BlockSpec((1, s, hw), lambda bi, h: (bi, 0, h)),
            pl.BlockSpec((1, s, hw), lambda bi, h: (bi, 0, h)),
            pl.BlockSpec((1, s, MLA_VDIM), lambda bi, h: (bi, 0, h)),
        ],
        out_specs=pl.BlockSpec((1, s, MLA_VDIM), lambda bi, h: (bi, 0, h)),
        out_shape=jax.ShapeDtypeStruct((b, s, MLA_HEADS * MLA_VDIM), CDT),
        scratch_shapes=_flash_scratch(tq, tq),
        compiler_params=_cparams(("parallel", "parallel")),
        name="mla_attention",
    )(*sched, qc, kc, v)


def _fox_attn_kernel(qt_ref, kt_ref, lt_ref, c_ref, q_ref, k_ref, v_ref, o_ref,
                     m_ref, acc_ref, sa_ref, sb_ref, ma_ref, mb_ref, *, tq, n):
    _flash_reset(m_ref, acc_ref)

    def produce(buf, qi, ki):
        s = _dot_nt(_tile(q_ref, qi, tq), _tile(k_ref, ki, tq))
        cbase = c_ref[0, 0, pl.ds(qi, 1), :][:, 0:1]
        _put_logits(buf, s + LOG2E * (cbase - c_ref[0, 0, pl.ds(ki, 1), :]))

    def consume(buf, qi, ki, masked):
        _flash_consume(buf, _tile(v_ref, ki, tq), m_ref, acc_ref, masked, tq)

    def finish(qi):
        o_ref[0, pl.ds(pl.multiple_of(qi * tq, tq), tq), :] = _flash_result(acc_ref[...]).astype(o_ref.dtype)
        _flash_reset(m_ref, acc_ref)

    _flash_stream(n, (qt_ref, kt_ref, lt_ref), 0, produce, consume, finish, (sa_ref, ma_ref), (sb_ref, mb_ref))


def _fox_attention(proj3, c4):
    b, s, _ = proj3.shape
    tq = min(512, s)
    nk = s // tq
    n, sched = _causal_schedule(nk)
    return pl.pallas_call(
        functools.partial(_fox_attn_kernel, tq=tq, n=n),
        grid=(b, FOX_HEADS),
        in_specs=[
            _SMEM, _SMEM, _SMEM,
            pl.BlockSpec((1, 1, nk, tq), lambda bi, h: (bi, h, 0, 0)),
            pl.BlockSpec((1, s, FOX_DH), lambda bi, h: (bi, 0, PB_CQ + h)),
            pl.BlockSpec((1, s, FOX_DH), lambda bi, h: (bi, 0, PB_CK + h)),
            pl.BlockSpec((1, s, FOX_DH), lambda bi, h: (bi, 0, PB_CV + h)),
        ],
        out_specs=pl.BlockSpec((1, s, FOX_DH), lambda bi, h: (bi, 0, h)),
        out_shape=jax.ShapeDtypeStruct((b, s, FOX_HEADS * FOX_DH), CDT),
        scratch_shapes=_flash_scratch(tq, tq),
        compiler_params=_cparams(("parallel", "parallel")),
        name="fox_attention",
    )(*sched, c4.reshape(b, FOX_HEADS, nk, tq), proj3, proj3, proj3)


def _nsa_compress_kernel(x_ref, w1a_ref, w1b_ref, pea_ref, peb_ref, w2_ref, o_ref):
    x = x_ref[0]
    n = x.shape[0]
    pa = _dot(x, w1a_ref[...])
    pb = _dot(x, w1b_ref[...])
    pe = _dot(pea_ref[...], w1a_ref[...]) + _dot(peb_ref[...], w1b_ref[...])
    hid = pa + pltpu.roll(pb, n - 1, axis=0) + pe[0:1]
    act = 0.5 * hid * (1.0 + jnp.tanh(math.sqrt(2.0 / math.pi) * (hid + 0.044715 * hid * hid * hid)))
    o_ref[0] = _dot(act.astype(CDT), w2_ref[...]).astype(o_ref.dtype)


def _nsa_compress(xc, w1a, w1b, pea, peb, w2):
    b, n, kdim = xc.shape
    hdim = w1a.shape[1]
    const = lambda shape: pl.BlockSpec(shape, lambda bi: (0,) * len(shape))
    return pl.pallas_call(
        _nsa_compress_kernel,
        grid=(b,),
        in_specs=[pl.BlockSpec((1, n, kdim), lambda bi: (bi, 0, 0)),
                  const((kdim, hdim)), const((kdim, hdim)), const((8, kdim)), const((8, kdim)),
                  const((hdim, w2.shape[1]))],
        out_specs=pl.BlockSpec((1, n, w2.shape[1]), lambda bi: (bi, 0, 0)),
        out_shape=jax.ShapeDtypeStruct((b, n, w2.shape[1]), CDT),
        compiler_params=_cparams(("parallel",)),
        name="nsa_compress",
    )(xc, w1a, w1b, pea, peb, w2)


def _nsa_cmp_kernel(slopes_ref, q_ref, kv_ref, oc_ref, sb_ref, used_ref, *, tq, n_topk):
    qi = pl.program_id(1)
    nblk = kv_ref.shape[1]
    q0 = qi * tq
    rowpos = q0 + lax.broadcasted_iota(jnp.int32, (tq, 1), 0)
    nidx = lax.broadcasted_iota(jnp.int32, (1, nblk), 1)
    disti = rowpos - (nidx * CMP_STRIDE + CMP_LEN - 1)
    valid = disti >= 0
    dist = disti.astype(F32)
    lane = lax.broadcasted_iota(jnp.int32, (tq, LANES), 1)
    low = lane < NSA_DH
    nn = lax.broadcasted_iota(jnp.int32, (nblk, LANES), 0) * CMP_STRIDE
    jj = (lax.broadcasted_iota(jnp.int32, (nblk, LANES), 1) & (NSA_DH - 1)) * SLC_LEN
    ov = (jnp.maximum(jnp.minimum(nn + CMP_LEN, jj + SLC_LEN) - jnp.maximum(nn, jj), 0).astype(F32)
          * (1.0 / CMP_LEN)).astype(CDT)
    jl = (lane & (NSA_DH - 1)).astype(F32)
    blk = (rowpos >> SLC_SHIFT).astype(F32)
    forced = (jl == 0.0) | (jl == blk) | (jl == blk - 1.0)
    outs = []
    bias = []
    for g in range(NSA_GROUPS):
        kc = kv_ref[0, :, g * LANES:(g + 1) * LANES]
        vc = kv_ref[0, :, (NSA_GROUPS + g) * LANES:(NSA_GROUPS + g + 1) * LANES]
        psum = jnp.zeros((tq, nblk), F32)
        for j in range(NSA_HPG):
            qb = q_ref[0, :, j * LANES:(j + 1) * LANES]
            qm = jnp.where(low if g == 0 else jnp.logical_not(low), qb, jnp.zeros_like(qb))
            s = _dot_nt(qm, kc) - slopes_ref[g * NSA_HPG + j] * dist
            s = jnp.where(valid, s, NEG)
            e = jnp.where(valid, jnp.exp2(s - jnp.max(s, axis=-1, keepdims=True)), 0.0)
            den = jnp.sum(e, axis=-1, keepdims=True)
            p = e / jnp.where(den > 0.0, den, 1.0)
            psum = psum + p
            outs.append(_dot(p.astype(CDT), vc))
        imp = _split_dot(psum, ov)
        imp = jnp.where(jl > blk, NEG_INF, jnp.where(forced, BIG, imp))
        sb = jnp.full((tq, LANES), NEG, F32)
        for _ in range(n_topk):
            mx = jnp.max(imp, axis=-1, keepdims=True)
            idx = jnp.min(jnp.where(imp == mx, jl, float(LANES)), axis=-1, keepdims=True)
            hit = jl == idx
            sb = jnp.where(hit, 0.0, sb)
            imp = jnp.where(hit, -jnp.inf, imp)
        bias.append(sb)
    sb = jnp.where(low, bias[1], bias[0])
    sb_ref[0] = sb.astype(sb_ref.dtype)
    used = jnp.max(jnp.where(sb == 0.0, 1.0, 0.0), axis=0, keepdims=True)
    used_ref[0, 0] = jnp.broadcast_to(used, used_ref.shape[2:])
    for blk_i in range(NSA_HEADS // 2):
        oc_ref[0, :, blk_i * LANES:(blk_i + 1) * LANES] = jnp.where(
            low, outs[2 * blk_i], outs[2 * blk_i + 1]).astype(oc_ref.dtype)


def _nsa_cmp_select(proj3, kvc, n_topk):
    b, s, _ = proj3.shape
    tq = min(256, s)
    nblk = kvc.shape[1]
    return pl.pallas_call(
        functools.partial(_nsa_cmp_kernel, tq=tq, n_topk=n_topk),
        grid=(b, s // tq),
        in_specs=[
            pl.BlockSpec(memory_space=pltpu.SMEM),
            pl.BlockSpec((1, tq, 4 * LANES), lambda bi, qi: (bi, qi, PB_DQ // 4)),
            pl.BlockSpec((1, nblk, kvc.shape[2]), lambda bi, qi: (bi, 0, 0)),
        ],
        out_specs=[
            pl.BlockSpec((1, tq, NSA_HEADS * NSA_DH), lambda bi, qi: (bi, qi, 0)),
            pl.BlockSpec((1, tq, LANES), lambda bi, qi: (bi, qi, 0)),
            pl.BlockSpec((1, 1, 8, LANES), lambda bi, qi: (bi, qi, 0, 0)),
        ],
        out_shape=[jax.ShapeDtypeStruct((b, s, NSA_HEADS * NSA_DH), CDT),
                   jax.ShapeDtypeStruct((b, s, LANES), CDT),
                   jax.ShapeDtypeStruct((b, s // tq, 8, LANES), F32)],
        compiler_params=_cparams(("parallel", "parallel")),
        name="nsa_cmp_select",
    )(jnp.asarray(_alibi_slopes(NSA_HEADS)), proj3, kvc)


def _compact_heads(heads, mine, low):
    both = [jnp.where(mine, a, pltpu.roll(a, NSA_DH, axis=1)) for a in heads]
    out = [jnp.where(low, both[2 * jj], both[2 * jj + 1]) for jj in range(NSA_HPG // 2)]
    return jnp.concatenate(out, axis=1)


def _nsa_win_kernel(slopes_ref, q_ref, kp_ref, kc_ref, vp_ref, vc_ref, o_ref, *, tq):
    g = pl.program_id(1)
    qi = pl.program_id(2)
    lane = lax.broadcasted_iota(jnp.int32, (tq, LANES), 1)
    low = lane < NSA_DH
    mine = (lane >> HALF_SHIFT) == g
    k = jnp.concatenate([kp_ref[0], kc_ref[0]], axis=0)
    v = jnp.concatenate([vp_ref[0], vc_ref[0]], axis=0)
    r = lax.broadcasted_iota(jnp.int32, (tq, 2 * tq), 0)
    c = lax.broadcasted_iota(jnp.int32, (tq, 2 * tq), 1)
    disti = r + tq - c
    valid = (disti >= 0) & (disti < WINDOW) & ((c >= tq) | (qi > 0))
    dist = disti.astype(F32)
    heads = []
    for j in range(NSA_HPG):
        qb = q_ref[0, :, j * LANES:(j + 1) * LANES]
        qm = jnp.where(mine, qb, jnp.zeros_like(qb))
        s = _dot_nt(qm, k) - slopes_ref[g * NSA_HPG + j] * dist
        s = jnp.where(valid, s, NEG)
        e = jnp.exp2(s - jnp.max(s, axis=-1, keepdims=True))
        p = e / jnp.sum(e, axis=-1, keepdims=True)
        heads.append(_dot(p.astype(CDT), v))
    o_ref[0] = _compact_heads(heads, mine, low).astype(o_ref.dtype)


def _nsa_window(proj3):
    b, s, _ = proj3.shape
    tq = WINDOW
    return pl.pallas_call(
        functools.partial(_nsa_win_kernel, tq=tq),
        grid=(b, NSA_GROUPS, s // tq),
        in_specs=[
            pl.BlockSpec(memory_space=pltpu.SMEM),
            pl.BlockSpec((1, tq, 4 * LANES), lambda bi, g, qi: (bi, qi, PB_DQ // 4)),
            pl.BlockSpec((1, tq, LANES), lambda bi, g, qi: (bi, jnp.maximum(qi - 1, 0), PB_WIN_K)),
            pl.BlockSpec((1, tq, LANES), lambda bi, g, qi: (bi, qi, PB_WIN_K)),
            pl.BlockSpec((1, tq, LANES), lambda bi, g, qi: (bi, jnp.maximum(qi - 1, 0), PB_WIN_V)),
            pl.BlockSpec((1, tq, LANES), lambda bi, g, qi: (bi, qi, PB_WIN_V)),
        ],
        out_specs=pl.BlockSpec((1, tq, NSA_HPG * NSA_DH), lambda bi, g, qi: (bi, qi, g)),
        out_shape=jax.ShapeDtypeStruct((b, s, NSA_HEADS * NSA_DH), CDT),
        compiler_params=_cparams(("parallel", "parallel", "parallel")),
        name="nsa_window",
    )(jnp.asarray(_alibi_slopes(NSA_HEADS)), proj3, proj3, proj3, proj3, proj3)


def _nsa_sel_kernel(cnt_ref, qt_ref, kt_ref, lt_ref, slopes_ref, q_ref, sb_ref, k_ref, v_ref, oc_ref, ow_ref, gl_ref,
                    e_ref, o_ref, m_ref, acc_ref, sa_ref, sb2_ref, ma_ref, mb_ref, *, tq, rows_per_problem):
    g = pl.program_id(1)
    lane = lax.broadcasted_iota(jnp.int32, (tq, LANES), 1)
    low = lane < NSA_DH
    mine = (lane >> HALF_SHIFT) == g
    _flash_reset(m_ref, acc_ref)
    col = lax.broadcasted_iota(jnp.int32, (1, tq), 1).astype(F32)
    jl = lane & (NSA_DH - 1)
    krow = lax.broadcasted_iota(jnp.int32, (tq, LANES), 0)
    w = NSA_HPG * NSA_DH

    def produce(buf, qi, ki):
        q = _tile(q_ref, qi, tq)
        sb = _tile(sb_ref, qi, tq)
        qa = jnp.concatenate([jnp.where(mine, q[:, j * LANES:(j + 1) * LANES], sb) for j in range(NSA_HPG)], axis=0)
        k = _tile(k_ref, ki, tq)
        onehot = jnp.where(((ki * tq + krow) >> SLC_SHIFT) == jl, 1.0, 0.0).astype(k.dtype)
        s_all = _dot_nt(qa, jnp.where(mine, k, onehot))
        rel = ((ki - qi) * tq).astype(F32)
        for j in range(NSA_HPG):
            rows = slice(j * tq, (j + 1) * tq)
            _put_logits(buf, s_all[rows] + slopes_ref[g * NSA_HPG + j] * (col + rel), rows)

    def consume(buf, qi, ki, masked):
        _flash_consume(buf, _tile(v_ref, ki, tq), m_ref, acc_ref, masked, tq)

    def finish(qi):
        o = _flash_result(acc_ref[...])
        o_s = _compact_heads([o[j * tq:(j + 1) * tq] for j in range(NSA_HPG)], mine, low)
        gates = _split_dot(jax.nn.sigmoid(_tile(gl_ref, qi, tq)), e_ref[0])
        y = (gates[:, 0:w] * _tile(oc_ref, qi, tq).astype(F32) + gates[:, w:2 * w] * o_s
             + gates[:, 2 * w:3 * w] * _tile(ow_ref, qi, tq).astype(F32))
        o_ref[0, pl.ds(pl.multiple_of(qi * tq, tq), tq), :] = y.astype(o_ref.dtype)
        _flash_reset(m_ref, acc_ref)

    prob = pl.program_id(0) * NSA_GROUPS + g
    _flash_stream(cnt_ref[prob], (qt_ref, kt_ref, lt_ref), prob * rows_per_problem, produce, consume, finish,
                  (sa_ref, ma_ref), (sb2_ref, mb_ref))


def _nsa_selected(proj3, sbias, used, o_c, o_w, small3, expand):
    b, s, _ = proj3.shape
    tq = min(256, s)
    nq = s // tq
    w = NSA_HPG * NSA_DH
    u = used[:, :, 0, :].reshape(b, nq, NSA_GROUPS, NSA_DH)[:, :, ::-1, :nq * (tq // SLC_LEN)]
    flags = (u.reshape(b, nq, NSA_GROUPS, nq, tq // SLC_LEN).max(axis=-1) > 0.0).astype(jnp.int32)
    flags = flags.transpose(0, 2, 1, 3)
    qt = jnp.arange(nq, dtype=jnp.int32)
    need = jnp.where(qt[None, :] < qt[:, None], flags, (qt[None, :] == qt[:, None]).astype(jnp.int32))
    need = need.reshape(b, NSA_GROUPS, nq * nq)
    cnt = need.sum(axis=-1).astype(jnp.int32)
    order = jnp.argsort(1 - need, axis=-1, stable=True).astype(jnp.int32)
    order = jnp.pad(order, ((0, 0), (0, 0), (0, 2)))
    rows = nq * nq + 2
    sched = (order // nq, order % nq, (order // nq == order % nq).astype(jnp.int32))
    return pl.pallas_call(
        functools.partial(_nsa_sel_kernel, tq=tq, rows_per_problem=rows),
        grid=(b, NSA_GROUPS),
        in_specs=[
            _SMEM, _SMEM, _SMEM, _SMEM, _SMEM,
            pl.BlockSpec((1, s, 4 * LANES), lambda bi, g: (bi, 0, PB_DQ // 4)),
            pl.BlockSpec((1, s, LANES), lambda bi, g: (bi, 0, 0)),
            pl.BlockSpec((1, s, LANES), lambda bi, g: (bi, 0, PB_SEL_K)),
            pl.BlockSpec((1, s, LANES), lambda bi, g: (bi, 0, PB_SEL_V)),
            pl.BlockSpec((1, s, w), lambda bi, g: (bi, 0, g)),
            pl.BlockSpec((1, s, w), lambda bi, g: (bi, 0, g)),
            pl.BlockSpec((1, s, LANES), lambda bi, g: (bi, 0, 0)),
            pl.BlockSpec((1, LANES, 3 * w), lambda bi, g: (g, 0, 0)),
        ],
        out_specs=pl.BlockSpec((1, s, w), lambda bi, g: (bi, 0, g)),
        out_shape=jax.ShapeDtypeStruct((b, s, NSA_HEADS * NSA_DH), CDT),
        scratch_shapes=_flash_scratch(NSA_HPG * tq, tq),
        compiler_params=_cparams(("parallel", "parallel")),
        name="nsa_selected",
    )(cnt.reshape(-1), *[t.reshape(-1) for t in sched], jnp.asarray(_alibi_slopes(NSA_HEADS)),
      proj3, sbias, proj3, proj3, o_c, o_w, small3, expand)


def _merge_kernel(ya_ref, yb_ref, yc_ref, yd_ref, ga_ref, gb_ref, gc_ref, gd_ref, wb_ref, wo_ref, x_ref, o_ref):
    merged = None
    for n, (y_ref, g_ref) in enumerate(((ya_ref, ga_ref), (yb_ref, gb_ref), (yc_ref, gc_ref), (yd_ref, gd_ref))):
        t = jax.nn.sigmoid(g_ref[...].astype(F32)) * _dot(y_ref[...], wb_ref[n])
        merged = t if merged is None else merged + t
    o_ref[...] = x_ref[...] + _dot(merged.astype(CDT), wo_ref[...])


def _merge(ys, proj2, wb, wo, x2):
    t, d = x2.shape
    tm = min(512, t)
    gate_blk = PB_GATE * LANES // d
    yspec = pl.BlockSpec((tm, BRANCH_WIDTH), lambda i: (i, 0))
    gspecs = [pl.BlockSpec((tm, d), functools.partial(lambda i, n: (i, gate_blk + n), n=n)) for n in range(N_BRANCH)]
    return pl.pallas_call(
        _merge_kernel,
        grid=(t // tm,),
        in_specs=[yspec] * N_BRANCH + gspecs + [
            pl.BlockSpec((N_BRANCH, BRANCH_WIDTH, d), lambda i: (0, 0, 0)),
            pl.BlockSpec((d, d), lambda i: (0, 0)),
            pl.BlockSpec((tm, d), lambda i: (i, 0)),
        ],
        out_specs=pl.BlockSpec((tm, d), lambda i: (i, 0)),
        out_shape=jax.ShapeDtypeStruct((t, d), F32),
        compiler_params=_cparams(("parallel",)),
        name="merge",
    )(*ys, proj2, proj2, proj2, proj2, wb, wo, x2)


HALO = 16


def _ffn_kernel(x_ref, xh_ref, g_ref, wu_ref, cw_ref, cb_ref, wd_ref, gf_ref, o_ref, he_ref, u_ref, act_ref,
                *, tm, fc, final):
    i = pl.program_id(1)
    x = x_ref[0]
    g = g_ref[...]
    xh = xh_ref[0] * (i > 0).astype(F32)
    he_ref[0:HALO] = _rms(xh, g).astype(CDT)
    he_ref[HALO:HALO + tm] = _rms(x, g).astype(CDT)
    he = he_ref[...]
    for c in range(D_FF // fc):
        outs = []
        for half in range(2):
            ub = u_ref.at[c % 2, half]
            lo = half * D_FF + c * fc
            ub[...] = _dot(he, wu_ref[:, lo:lo + fc])
            conv = cb_ref[:, lo:lo + fc]
            for kk in range(CONV_WIDTH):
                off = HALO - (CONV_WIDTH - 1) + kk
                conv = conv + cw_ref[kk:kk + 1, lo:lo + fc] * ub[off:off + tm, :]
            outs.append(conv)
        a, gg = outs
        act_ref[:, c * fc:(c + 1) * fc] = (a * jax.nn.sigmoid(a) * gg).astype(CDT)
    y = x + _dot(act_ref[...], wd_ref[...])
    if final:
        y = _rms(y, gf_ref[...])
    o_ref[0] = y


def _ffn(x3, g, wu, cw, cb, wd, gf, final):
    b, s, d = x3.shape
    tm = min(512, s)
    fc = 256
    const = lambda shape: pl.BlockSpec(shape, lambda bi, i: (0,) * len(shape), pipeline_mode=pl.Buffered(1))
    return pl.pallas_call(
        functools.partial(_ffn_kernel, tm=tm, fc=fc, final=final),
        grid=(b, s // tm),
        in_specs=[
            pl.BlockSpec((1, tm, d), lambda bi, i: (bi, i, 0)),
            pl.BlockSpec((1, HALO, d), lambda bi, i: (bi, jnp.maximum(i * (tm // HALO) - 1, 0), 0)),
            const((1, d)), const((d, 2 * D_FF)), const((CONV_WIDTH, 2 * D_FF)), const((1, 2 * D_FF)),
            const((D_FF, d)), const((1, d)),
        ],
        out_specs=pl.BlockSpec((1, tm, d), lambda bi, i: (bi, i, 0)),
        out_shape=jax.ShapeDtypeStruct((b, s, d), F32),
        scratch_shapes=[pltpu.VMEM((tm + HALO, d), CDT), pltpu.VMEM((2, 2, tm + HALO, fc), F32),
                        pltpu.VMEM((tm, D_FF), CDT)],
        compiler_params=_cparams(("parallel", "arbitrary")),
        name="conv_glu_mlp",
    )(x3, x3, g.reshape(1, d), wu, cw, cb.reshape(1, -1), wd, gf.reshape(1, d))


def _prep_w_in(w):
    widths = (512, 512, 512, MLA_Q_LORA, MLA_KV_LORA, MLA_ROPE, 512, 512, 512, FOX_HEADS,
              512, 768, 3 * NSA_HEADS, N_BRANCH * D_MODEL)
    offs = np.cumsum((0,) + widths)
    (a_q, a_k, a_v, b_cq, b_ckv, b_kr, c_q, c_k, c_v, c_f, d_q, d_kv, d_g, gate) = [
        w[:, offs[i]:offs[i + 1]] for i in range(len(widths))]
    d = w.shape[0]
    d_q = d_q.reshape(d, NSA_GROUPS, NSA_HPG, NSA_DH).transpose(0, 2, 1, 3).reshape(d, 512)
    half = MLA_ROPE // 2
    kr_swap = jnp.concatenate([-b_kr[:, half:], b_kr[:, :half]], axis=1)
    z64 = jnp.zeros((d, LANES - MLA_ROPE), w.dtype)
    big = jnp.concatenate([
        a_q * (LOG2E * DIFF_DH ** -0.5), a_k, a_v,
        c_q * (LOG2E * FOX_DH ** -0.5), c_k, c_v,
        d_q * (LOG2E * NSA_DH ** -0.5), d_kv,
        b_cq, b_ckv, b_kr, z64, kr_swap, z64,
        gate], axis=1)
    small = jnp.concatenate([c_f, d_g, jnp.zeros((d, LANES - FOX_HEADS - 3 * NSA_HEADS), w.dtype)], axis=1)
    return big.astype(CDT), small.astype(CDT)


def _prep_mla(w_uq, w_ukv):
    r = w_uq.shape[0]
    hw = 2 * LANES
    half = MLA_ROPE // 2
    scale = LOG2E * (MLA_NOPE + MLA_ROPE) ** -0.5
    wq = (w_uq * scale).reshape(r, MLA_HEADS, MLA_NOPE + MLA_ROPE)
    nope, t1, t2 = wq[..., :MLA_NOPE], wq[..., MLA_NOPE:MLA_NOPE + half], wq[..., MLA_NOPE + half:]
    zpad = jnp.zeros((r, MLA_HEADS, hw - MLA_NOPE - MLA_ROPE), w_uq.dtype)
    wqm = jnp.concatenate([nope, t1, t2, zpad], axis=-1).reshape(r, MLA_HEADS * hw)
    wqs = jnp.concatenate([jnp.zeros_like(nope), -t2, t1, zpad], axis=-1).reshape(r, MLA_HEADS * hw)
    wkv = w_ukv.reshape(w_ukv.shape[0], MLA_HEADS, MLA_NOPE + MLA_VDIM)
    wk = wkv[..., :MLA_NOPE].reshape(-1, MLA_HEADS * MLA_NOPE)
    wv = wkv[..., MLA_NOPE:].reshape(-1, MLA_HEADS * MLA_VDIM)
    return wqm.astype(CDT), wqs.astype(CDT), wk.astype(CDT), wv.astype(CDT)


def _rope_tables(s):
    half = MLA_ROPE // 2
    inv_freq = ROPE_THETA ** (-jnp.arange(0, MLA_ROPE, 2, dtype=F32) / MLA_ROPE)
    ang = jnp.arange(s, dtype=F32)[:, None] * inv_freq[None, :]
    cos, sin = jnp.cos(ang), jnp.sin(ang)
    z = jnp.zeros((s, LANES - MLA_ROPE), F32)
    cosk = jnp.concatenate([cos, cos, z], axis=1)
    sink = jnp.concatenate([sin, sin, z], axis=1)
    cosq = jnp.concatenate([jnp.ones((s, MLA_NOPE), F32), cosk], axis=1)
    sinq = jnp.concatenate([jnp.zeros((s, MLA_NOPE), F32), sink], axis=1)
    return cosq, sinq, cosk, sink


def _prep_compress(pe, w1, w2):
    eye2 = jnp.eye(2, dtype=F32)
    w1r = w1.reshape(2, CMP_LEN, NSA_DH, CMP_HIDDEN)

    def expand(wpart):
        t = jnp.einsum('kpdh,kK,gG->pkgdKGh', wpart, eye2, eye2)
        return t.reshape(CMP_STRIDE * 4 * NSA_DH, 4 * CMP_HIDDEN)

    w1a, w1b = expand(w1r[:, :CMP_STRIDE]), expand(w1r[:, CMP_STRIDE:])

    def pe_row(p):
        t = jnp.broadcast_to(p.transpose(1, 0, 2)[:, :, None, :], (CMP_STRIDE, 2, NSA_GROUPS, NSA_DH))
        return jnp.pad(t.reshape(1, -1), ((0, 7), (0, 0)))

    pea, peb = pe_row(pe[:, :CMP_STRIDE]), pe_row(pe[:, CMP_STRIDE:])
    w2b = jnp.einsum('khd,kK,gG,u->kghKGud', w2, eye2, eye2, jnp.ones((2,), F32))
    w2b = w2b.reshape(4 * CMP_HIDDEN, 4 * 2 * NSA_DH)
    return w1a.astype(CDT), w1b.astype(CDT), pea.astype(CDT), peb.astype(CDT), w2b.astype(CDT)


def _gate_expand():
    e = np.zeros((NSA_GROUPS, LANES, 3, NSA_HPG, NSA_DH), np.float32)
    for g in range(NSA_GROUPS):
        for j in range(NSA_HPG):
            for br in range(3):
                e[g, SMALL_G + (g * NSA_HPG + j) * 3 + br, br, j, :] = 1.0
    return jnp.asarray(e.reshape(NSA_GROUPS, LANES, 3 * NSA_HPG * NSA_DH)).astype(CDT)


def _token_mixers(x3, l, norm_mix, w_in, diff_lambda, diff_subln, mla_norm_q, mla_w_uq, mla_norm_kv, mla_w_ukv,
                  fox_b_f, nsa_cmp_pe, nsa_cmp_w1, nsa_cmp_w2, w_branch, w_out, rope_tabs):
    b, s, d = x3.shape
    t = b * s
    x2 = x3.reshape(t, d)
    w_big, w_small = _prep_w_in(w_in)
    proj, small = _in_proj(x2, norm_mix, w_big, w_small)
    proj3 = proj.reshape(b, s, N_PROJ)
    small3 = small.reshape(b, s, LANES)

    lam_init = 0.8 - 0.6 * math.exp(-0.3 * l)
    y_a = _diff_attention(proj3, diff_lambda, diff_subln, lam_init)

    wqm, wqs, wk, wv = _prep_mla(mla_w_uq, mla_w_ukv)
    qc, kc, vv = _mla_prep(proj3, mla_norm_q, mla_norm_kv, wqm, wqs, wk, wv, rope_tabs)
    y_b = _mla_attention(qc, kc, vv)

    cf_rows = small3[:, :, SMALL_F:SMALL_F + FOX_HEADS].transpose(0, 2, 1).reshape(b * FOX_HEADS, s)
    bias_rows = jnp.tile(fox_b_f.astype(F32), b).reshape(b * FOX_HEADS, 1)
    c4 = _fox_cumsum(cf_rows, bias_rows)
    y_c = _fox_attention(proj3, c4)

    w1a, w1b, pea, peb, w2b = _prep_compress(nsa_cmp_pe, nsa_cmp_w1, nsa_cmp_w2)
    xc = proj3[:, :, PB_CMP_K * LANES:(PB_CMP_V + 1) * LANES].reshape(b, s // CMP_STRIDE, CMP_STRIDE * 2 * LANES)
    kvc = _nsa_compress(xc, w1a, w1b, pea, peb, w2b)
    n_topk = min(SLC_TOPK, s // SLC_LEN)
    o_c, sbias, used = _nsa_cmp_select(proj3, kvc, n_topk)
    o_w = _nsa_window(proj3)
    y_d = _nsa_selected(proj3, sbias, used, o_c, o_w, small3, _gate_expand())

    ys = [y.reshape(t, BRANCH_WIDTH) for y in (y_a, y_b, y_c, y_d)]
    return _merge(ys, proj, w_branch.astype(CDT), w_out.astype(CDT), x2).reshape(b, s, d)


def kernel(x, norm_mix, w_in, diff_lambda, diff_subln, mla_norm_q, mla_w_uq, mla_norm_kv, mla_w_ukv, fox_b_f,
           nsa_cmp_pe, nsa_cmp_w1, nsa_cmp_w2, w_branch, w_out, norm_ffn, w_up, conv_w, conv_b, w_down, norm_final):
    depth = w_in.shape[0]
    s = x.shape[1]
    rope_tabs = _rope_tables(s)
    for l in range(depth):
        x = _token_mixers(x, l, norm_mix[l], w_in[l], diff_lambda[l], diff_subln[l], mla_norm_q[l], mla_w_uq[l],
                          mla_norm_kv[l], mla_w_ukv[l], fox_b_f[l], nsa_cmp_pe[l], nsa_cmp_w1[l], nsa_cmp_w2[l],
                          w_branch[l], w_out[l], rope_tabs)
        x = _ffn(x, norm_ffn[l], w_up[l].astype(CDT), conv_w[l], conv_b[l], w_down[l].astype(CDT), norm_final,
                 final=(l == depth - 1))
    return x
```

```python
import functools
import math

import numpy as np
import jax
import jax.numpy as jnp
from jax import lax
from jax.experimental import pallas as pl
from jax.experimental.pallas import tpu as pltpu

F32 = jnp.float32
CDT = jnp.bfloat16

NEG = -1e30
NEG_INF = -1e30
BIG = 1e9
NORM_EPS = 1e-6
LOG2E = 1.4426950408889634
LANES = 128

D_MODEL = 1024
DIFF_HEADS, DIFF_DH = 4, 64
MLA_HEADS, MLA_NOPE, MLA_ROPE, MLA_VDIM = 4, 128, 64, 128
MLA_Q_LORA, MLA_KV_LORA = 256, 256
ROPE_THETA = 10000.0
FOX_HEADS, FOX_DH = 4, 128
NSA_HEADS, NSA_GROUPS, NSA_DH = 8, 2, 64
NSA_HPG = NSA_HEADS // NSA_GROUPS
CMP_STRIDE = 16
CMP_LEN = 2 * CMP_STRIDE
CMP_HIDDEN = 128
SLC_LEN = 64
SLC_SHIFT = 6
HALF_SHIFT = 6
SLC_TOPK = 8
WINDOW = 256
N_BRANCH = 4
BRANCH_WIDTH = 512
D_FF = 2816
CONV_WIDTH = 3

PB_AQ, PB_AK, PB_AV = 0, 4, 8
PB_CQ, PB_CK, PB_CV = 12, 16, 20
PB_DQ = 24
PB_CMP_K, PB_CMP_V, PB_SEL_K, PB_SEL_V, PB_WIN_K, PB_WIN_V = 28, 29, 30, 31, 32, 33
PB_BCQ, PB_BCKV, PB_BKR, PB_BKRS = 34, 36, 38, 39
PB_GATE = 40
N_PROJ = 72 * LANES
SMALL_F, SMALL_G = 0, 4

VMEM_LIMIT = 56 * 1024 * 1024


def _cparams(sem):
    return pltpu.CompilerParams(dimension_semantics=sem, vmem_limit_bytes=VMEM_LIMIT)


def _rms(xf, g):
    return xf * lax.rsqrt(jnp.mean(xf * xf, axis=-1, keepdims=True) + NORM_EPS) * g


def _dot(a, b):
    return jnp.dot(a, b, preferred_element_type=F32)


def _dot_nt(a, b):
    return lax.dot_general(a, b, (((1,), (1,)), ((), ())), preferred_element_type=F32)


def _split_dot(a, b):
    hi = a.astype(CDT)
    lo = (a - hi.astype(F32)).astype(CDT)
    return _dot(hi, b) + _dot(lo, b)


def _alibi_slopes(n):
    return (LOG2E * np.exp2(-8.0 * np.arange(1, n + 1) / n)).astype(np.float32)


def _inproj_kernel(x_ref, g_ref, w_ref, ws_ref, o_ref, os_ref, h_ref):
    @pl.when(pl.program_id(1) == 0)
    def _():
        h = _rms(x_ref[...], g_ref[...]).astype(CDT)
        h_ref[...] = h
        os_ref[...] = _dot(h, ws_ref[...])

    o_ref[...] = _dot(h_ref[...], w_ref[...]).astype(o_ref.dtype)


def _in_proj(x2, g, w, ws):
    t, d = x2.shape
    n = w.shape[1]
    tm = min(1024, t)
    tn = 2304 if n % 2304 == 0 else 1024
    return pl.pallas_call(
        _inproj_kernel,
        grid=(t // tm, n // tn),
        in_specs=[
            pl.BlockSpec((tm, d), lambda i, j: (i, 0)),
            pl.BlockSpec((1, d), lambda i, j: (0, 0)),
            pl.BlockSpec((d, tn), lambda i, j: (0, j)),
            pl.BlockSpec((d, LANES), lambda i, j: (0, 0)),
        ],
        out_specs=[
            pl.BlockSpec((tm, tn), lambda i, j: (i, j)),
            pl.BlockSpec((tm, LANES), lambda i, j: (i, 0)),
        ],
        out_shape=[jax.ShapeDtypeStruct((t, n), CDT), jax.ShapeDtypeStruct((t, LANES), F32)],
        scratch_shapes=[pltpu.VMEM((tm, d), CDT)],
        compiler_params=_cparams(("parallel", "arbitrary")),
        name="in_proj",
    )(x2, g.reshape(1, d), w, ws)


def _fox_cumsum_kernel(cf_ref, bf_ref, o_ref):
    rows, s = cf_ref.shape
    lane = lax.broadcasted_iota(jnp.int32, (rows, LANES), 1)
    carry = jnp.zeros((rows, 1), F32)
    for c in range(s // LANES):
        z = cf_ref[:, c * LANES:(c + 1) * LANES] + bf_ref[...]
        xs = jnp.minimum(z, 0.0) - jnp.log1p(jnp.exp(-jnp.abs(z)))
        d = 1
        while d < LANES:
            xs = xs + jnp.where(lane >= d, pltpu.roll(xs, d, axis=1), 0.0)
            d *= 2
        xs = xs + carry
        o_ref[:, c * LANES:(c + 1) * LANES] = xs
        carry = xs[:, LANES - 1:LANES]


def _fox_cumsum(cf_rows, bias_rows):
    return pl.pallas_call(
        _fox_cumsum_kernel,
        out_shape=jax.ShapeDtypeStruct(cf_rows.shape, F32),
        name="fox_cumsum",
    )(cf_rows, bias_rows)


def _flash_scratch(rows, tk):
    return [pltpu.VMEM((rows, LANES), F32), pltpu.VMEM((rows, 2 * LANES), F32),
            pltpu.VMEM((rows, tk), F32), pltpu.VMEM((rows, tk), F32),
            pltpu.VMEM((rows, LANES), F32), pltpu.VMEM((rows, LANES), F32),
            pltpu.VMEM((rows, tk), F32)]


def _flash_reset(m_ref, acc_ref):
    m_ref[...] = jnp.full(m_ref.shape, NEG, F32)
    acc_ref[...] = jnp.zeros(acc_ref.shape, F32)


def _flash_begin(m_ref, acc_ref, cm_ref, tq):
    _flash_reset(m_ref, acc_ref)
    cm_ref[...] = _causal_bias(cm_ref.shape[0], cm_ref.shape[1], tq)


def _row_max(s):
    return jnp.broadcast_to(jnp.max(s, axis=-1, keepdims=True), (s.shape[0], LANES))


def _causal_bias(rows, tk, tq):
    r = lax.broadcasted_iota(jnp.int32, (rows, tk), 0) & (tq - 1)
    c = lax.broadcasted_iota(jnp.int32, (rows, tk), 1)
    return jnp.where(c <= r, 0.0, NEG)


def _put_logits(buf, s, diag, cm_ref, rows=slice(None)):
    if diag is True:
        s = s + cm_ref[rows]
    elif diag is not False:
        s = s + diag.astype(F32) * cm_ref[rows]
    buf[0][rows] = s
    buf[1][rows] = _row_max(s)


def _with_ones(v):
    return jnp.concatenate([v, jnp.ones((v.shape[0], LANES), v.dtype)], axis=1)


def _flash_consume(buf, v, m_ref, acc_ref, cm_ref=None):
    s = buf[0][...]
    m_cur = buf[1][...]
    if cm_ref is not None:
        s = s + cm_ref[...]
        m_cur = _row_max(s)
    m_old = m_ref[...]
    m_new = jnp.maximum(m_old, m_cur)
    alpha = jnp.exp2(m_old - m_new)
    p = jnp.exp2(s - jnp.tile(m_new, (1, s.shape[1] // LANES))).astype(CDT)
    acc_ref[...] = jnp.tile(alpha, (1, 2)) * acc_ref[...] + _dot(p, _with_ones(v))
    m_ref[...] = m_new


def _flash_result(acc):
    return acc[:, :LANES] / acc[:, LANES:]


def _causal_schedule(nq):
    ent = [(qi, ki, int(ki == qi)) for qi in range(nq) for ki in range(qi + 1)]
    n = len(ent)
    a = np.asarray(ent + [ent[-1]] * 2, np.int32)
    return n, tuple(jnp.asarray(a[:, i]) for i in range(3))


def _flash_stream(n, sched, base, produce, consume, finish, buf_a, buf_b, mask_at_produce):
    qt, kt, lt = sched

    def step(cur, nxt, t, diag, next_diag):
        if nxt is not None:
            produce(nxt, qt[base + t + 1], kt[base + t + 1], next_diag if mask_at_produce else False)
        consume(cur, kt[base + t], diag and not mask_at_produce)
        if diag:
            finish(qt[base + t])

    produce(buf_a, qt[base], kt[base], mask_at_produce)

    def pair(j, c):
        t = 2 * j
        l0, l1 = lt[base + t], lt[base + t + 1]
        for d0 in (False, True):
            for d1 in (False, True):
                @pl.when(((l0 != 0) == d0) & ((l1 != 0) == d1))
                def _():
                    step(buf_a, buf_b, t, d0, d1)
                    step(buf_b, buf_a, t + 1, d1, lt[base + t + 2])
        return c

    lax.fori_loop(0, n // 2, pair, 0)

    def tail():
        step(buf_a, None, n - 1, True, None)

    if isinstance(n, int):
        if n % 2 == 1:
            tail()
    else:
        pl.when(n % 2 == 1)(tail)


def _tile(ref, i, t):
    return ref[0, pl.ds(pl.multiple_of(i * t, t), t), :]


def _diff_attn_kernel(qt_ref, kt_ref, lt_ref, slopes_ref, lam_ref, g_ref, q_ref, k_ref, v_ref, o_ref,
                      m_ref, acc_ref, sa_ref, sb_ref, ma_ref, mb_ref, cm_ref, *, tq, n, lam_init):
    slope = slopes_ref[pl.program_id(1)]
    _flash_begin(m_ref, acc_ref, cm_ref, tq)
    col = lax.broadcasted_iota(jnp.int32, (1, tq), 1).astype(F32)
    lane = lax.broadcasted_iota(jnp.int32, (tq, LANES), 1)
    lf = lam_ref[...]
    lam = (jnp.exp(jnp.sum(lf[0:1] * lf[1:2], axis=-1, keepdims=True))
           - jnp.exp(jnp.sum(lf[2:3] * lf[3:4], axis=-1, keepdims=True)) + lam_init)

    def produce(buf, qi, ki, diag):
        q = _tile(q_ref, qi, tq)
        zero = jnp.zeros_like(q)
        qq = jnp.concatenate([jnp.where(lane < DIFF_DH, q, zero), jnp.where(lane >= DIFF_DH, q, zero)], axis=0)
        s = _dot_nt(qq, _tile(k_ref, ki, tq))
        _put_logits(buf, s + slope * (col + ((ki - qi) * tq).astype(F32)), diag, cm_ref)

    def consume(buf, ki, diag):
        _flash_consume(buf, _tile(v_ref, ki, tq), m_ref, acc_ref, cm_ref if diag else None)

    def finish(qi):
        o = _flash_result(acc_ref[...])
        d = o[0:tq] - lam * o[tq:2 * tq]
        o_ref[0, pl.ds(pl.multiple_of(qi * tq, tq), tq), :] = (
            _rms(d, g_ref[...]) * (1.0 - lam_init)).astype(o_ref.dtype)
        _flash_reset(m_ref, acc_ref)

    _flash_stream(n, (qt_ref, kt_ref, lt_ref), 0, produce, consume, finish, (sa_ref, ma_ref), (sb_ref, mb_ref),
                  mask_at_produce=False)


_SMEM = pl.BlockSpec(memory_space=pltpu.SMEM)


def _diff_attention(proj3, diff_lambda, subln, lam_init):
    b, s, _ = proj3.shape
    tq = min(512, s)
    dv = 2 * DIFF_DH
    n, sched = _causal_schedule(s // tq)
    kern = functools.partial(_diff_attn_kernel, tq=tq, n=n, lam_init=lam_init)
    return pl.pallas_call(
        kern,
        grid=(b, DIFF_HEADS),
        in_specs=[
            _SMEM, _SMEM, _SMEM, _SMEM,
            pl.BlockSpec((4, DIFF_DH), lambda bi, h: (0, 0)),
            pl.BlockSpec((1, dv), lambda bi, h: (0, 0)),
            pl.BlockSpec((1, s, LANES), lambda bi, h: (bi, 0, PB_AQ + h)),
            pl.BlockSpec((1, s, LANES), lambda bi, h: (bi, 0, PB_AK + h)),
            pl.BlockSpec((1, s, LANES), lambda bi, h: (bi, 0, PB_AV + h)),
        ],
        out_specs=pl.BlockSpec((1, s, dv), lambda bi, h: (bi, 0, h)),
        out_shape=jax.ShapeDtypeStruct((b, s, DIFF_HEADS * dv), CDT),
        scratch_shapes=_flash_scratch(2 * tq, tq),
        compiler_params=_cparams(("parallel", "parallel")),
        name="diff_attention",
    )(*sched, jnp.asarray(_alibi_slopes(DIFF_HEADS)), diff_lambda, subln.reshape(1, dv), proj3, proj3, proj3)


def _mla_prep_kernel(cq_ref, ckv_ref, kr_ref, krs_ref, gq_ref, gkv_ref, wqm_ref, wqs_ref, wk_ref, wv_ref,
                     cosq_ref, sinq_ref, cosk_ref, sink_ref, q_ref, k_ref, v_ref):
    hq = _rms(cq_ref[0].astype(F32), gq_ref[...]).astype(CDT)
    qm = _dot(hq, wqm_ref[...])
    qs = _dot(hq, wqs_ref[...])
    cosq, sinq = cosq_ref[...], sinq_ref[...]
    hw = 2 * LANES
    for h in range(MLA_HEADS):
        sl = slice(h * hw, (h + 1) * hw)
        q_ref[0, :, sl] = (qm[:, sl] * cosq + qs[:, sl] * sinq).astype(q_ref.dtype)
    hkv = _rms(ckv_ref[0].astype(F32), gkv_ref[...]).astype(CDT)
    kn = _dot(hkv, wk_ref[...])
    v_ref[0] = _dot(hkv, wv_ref[...]).astype(v_ref.dtype)
    kpe = (kr_ref[0].astype(F32) * cosk_ref[...] + krs_ref[0].astype(F32) * sink_ref[...]).astype(k_ref.dtype)
    for h in range(MLA_HEADS):
        k_ref[0, :, h * hw:h * hw + LANES] = kn[:, h * LANES:(h + 1) * LANES].astype(k_ref.dtype)
        k_ref[0, :, h * hw + LANES:(h + 1) * hw] = kpe


def _mla_prep(proj3, gq, gkv, wqm, wqs, wk, wv, tabs):
    b, s, _ = proj3.shape
    tm = min(512, s)
    hw = 2 * LANES
    cosq, sinq, cosk, sink = tabs
    const = lambda shape: pl.BlockSpec(shape, lambda bi, i: (0,) * len(shape))
    return pl.pallas_call(
        _mla_prep_kernel,
        grid=(b, s // tm),
        in_specs=[
            pl.BlockSpec((1, tm, MLA_Q_LORA), lambda bi, i: (bi, i, PB_BCQ // 2)),
            pl.BlockSpec((1, tm, MLA_KV_LORA), lambda bi, i: (bi, i, PB_BCKV // 2)),
            pl.BlockSpec((1, tm, LANES), lambda bi, i: (bi, i, PB_BKR)),
            pl.BlockSpec((1, tm, LANES), lambda bi, i: (bi, i, PB_BKRS)),
            const((1, MLA_Q_LORA)), const((1, MLA_KV_LORA)),
            const((MLA_Q_LORA, MLA_HEADS * hw)), const((MLA_Q_LORA, MLA_HEADS * hw)),
            const((MLA_KV_LORA, MLA_HEADS * MLA_NOPE)), const((MLA_KV_LORA, MLA_HEADS * MLA_VDIM)),
            pl.BlockSpec((tm, hw), lambda bi, i: (i, 0)), pl.BlockSpec((tm, hw), lambda bi, i: (i, 0)),
            pl.BlockSpec((tm, LANES), lambda bi, i: (i, 0)), pl.BlockSpec((tm, LANES), lambda bi, i: (i, 0)),
        ],
        out_specs=[
            pl.BlockSpec((1, tm, MLA_HEADS * hw), lambda bi, i: (bi, i, 0)),
            pl.BlockSpec((1, tm, MLA_HEADS * hw), lambda bi, i: (bi, i, 0)),
            pl.BlockSpec((1, tm, MLA_HEADS * MLA_VDIM), lambda bi, i: (bi, i, 0)),
        ],
        out_shape=[
            jax.ShapeDtypeStruct((b, s, MLA_HEADS * hw), CDT),
            jax.ShapeDtypeStruct((b, s, MLA_HEADS * hw), CDT),
            jax.ShapeDtypeStruct((b, s, MLA_HEADS * MLA_VDIM), CDT),
        ],
        compiler_params=_cparams(("parallel", "parallel")),
        name="mla_prep",
    )(proj3, proj3, proj3, proj3, gq.reshape(1, -1), gkv.reshape(1, -1), wqm, wqs, wk, wv,
      cosq, sinq, cosk, sink)


def _plain_attn_kernel(qt_ref, kt_ref, lt_ref, q_ref, k_ref, v_ref, o_ref,
                       m_ref, acc_ref, sa_ref, sb_ref, ma_ref, mb_ref, cm_ref, *, tq, n):
    _flash_begin(m_ref, acc_ref, cm_ref, tq)

    def produce(buf, qi, ki, diag):
        _put_logits(buf, _dot_nt(_tile(q_ref, qi, tq), _tile(k_ref, ki, tq)), diag, cm_ref)

    def consume(buf, ki, diag):
        _flash_consume(buf, _tile(v_ref, ki, tq), m_ref, acc_ref, cm_ref if diag else None)

    def finish(qi):
        o_ref[0, pl.ds(pl.multiple_of(qi * tq, tq), tq), :] = _flash_result(acc_ref[...]).astype(o_ref.dtype)
        _flash_reset(m_ref, acc_ref)

    _flash_stream(n, (qt_ref, kt_ref, lt_ref), 0, produce, consume, finish, (sa_ref, ma_ref), (sb_ref, mb_ref),
                  mask_at_produce=False)


def _mla_attention(qc, kc, v):
    b, s, _ = qc.shape
    tq = min(512, s)
    hw = 2 * LANES
    n, sched = _causal_schedule(s // tq)
    return pl.pallas_call(
        functools.partial(_plain_attn_kernel, tq=tq, n=n),
        grid=(b, MLA_HEADS),
        in_specs=[
            _SMEM, _SMEM, _SMEM,
            pl.BlockSpec((1, s, hw), lambda bi, h: (bi, 0, h)),
            pl.BlockSpec((1, s, hw), lambda bi, h: (bi, 0, h)),
            pl.BlockSpec((1, s, MLA_VDIM), lambda bi, h: (bi, 0, h)),
        ],
        out_specs=pl.BlockSpec((1, s, MLA_VDIM), lambda bi, h: (bi, 0, h)),
        out_shape=jax.ShapeDtypeStruct((b, s, MLA_HEADS * MLA_VDIM), CDT),
        scratch_shapes=_flash_scratch(tq, tq),
        compiler_params=_cparams(("parallel", "parallel")),
        name="mla_attention",
    )(*sched, qc, kc, v)


def _fox_attn_kernel(qt_ref, kt_ref, lt_ref, c_ref, q_ref, k_ref, v_ref, o_ref,
                     m_ref, acc_ref, sa_ref, sb_ref, ma_ref, mb_ref, cm_ref, *, tq, n):
    _flash_begin(m_ref, acc_ref, cm_ref, tq)

    def produce(buf, qi, ki, diag):
        s = _dot_nt(_tile(q_ref, qi, tq), _tile(k_ref, ki, tq))
        cbase = c_ref[0, 0, pl.ds(qi, 1), :][:, 0:1]
        _put_logits(buf, s + LOG2E * (cbase - c_ref[0, 0, pl.ds(ki, 1), :]), diag, cm_ref)

    def consume(buf, ki, diag):
        _flash_consume(buf, _tile(v_ref, ki, tq), m_ref, acc_ref, cm_ref if diag else None)

    def finish(qi):
        o_ref[0, pl.ds(pl.multiple_of(qi * tq, tq), tq), :] = _flash_result(acc_ref[...]).astype(o_ref.dtype)
        _flash_reset(m_ref, acc_ref)

    _flash_stream(n, (qt_ref, kt_ref, lt_ref), 0, produce, consume, finish, (sa_ref, ma_ref), (sb_ref, mb_ref),
                  mask_at_produce=False)


def _fox_attention(proj3, c4):
    b, s, _ = proj3.shape
    tq = min(512, s)
    nk = s // tq
    n, sched = _causal_schedule(nk)
    return pl.pallas_call(
        functools.partial(_fox_attn_kernel, tq=tq, n=n),
        grid=(b, FOX_HEADS),
        in_specs=[
            _SMEM, _SMEM, _SMEM,
            pl.BlockSpec((1, 1, nk, tq), lambda bi, h: (bi, h, 0, 0)),
            pl.BlockSpec((1, s, FOX_DH), lambda bi, h: (bi, 0, PB_CQ + h)),
            pl.BlockSpec((1, s, FOX_DH), lambda bi, h: (bi, 0, PB_CK + h)),
            pl.BlockSpec((1, s, FOX_DH), lambda bi, h: (bi, 0, PB_CV + h)),
        ],
        out_specs=pl.BlockSpec((1, s, FOX_DH), lambda bi, h: (bi, 0, h)),
        out_shape=jax.ShapeDtypeStruct((b, s, FOX_HEADS * FOX_DH), CDT),
        scratch_shapes=_flash_scratch(tq, tq),
        compiler_params=_cparams(("parallel", "parallel")),
        name="fox_attention",
    )(*sched, c4.reshape(b, FOX_HEADS, nk, tq), proj3, proj3, proj3)


def _nsa_compress_kernel(x_ref, w1a_ref, w1b_ref, pea_ref, peb_ref, w2_ref, o_ref):
    x = x_ref[0]
    n = x.shape[0]
    pa = _dot(x, w1a_ref[...])
    pb = _dot(x, w1b_ref[...])
    pe = _dot(pea_ref[...], w1a_ref[...]) + _dot(peb_ref[...], w1b_ref[...])
    hid = pa + pltpu.roll(pb, n - 1, axis=0) + pe[0:1]
    act = 0.5 * hid * (1.0 + jnp.tanh(math.sqrt(2.0 / math.pi) * (hid + 0.044715 * hid * hid * hid)))
    o_ref[0] = _dot(act.astype(CDT), w2_ref[...]).astype(o_ref.dtype)


def _nsa_compress(xc, w1a, w1b, pea, peb, w2):
    b, n, kdim = xc.shape
    hdim = w1a.shape[1]
    const = lambda shape: pl.BlockSpec(shape, lambda bi: (0,) * len(shape))
    return pl.pallas_call(
        _nsa_compress_kernel,
        grid=(b,),
        in_specs=[pl.BlockSpec((1, n, kdim), lambda bi: (bi, 0, 0)),
                  const((kdim, hdim)), const((kdim, hdim)), const((8, kdim)), const((8, kdim)),
                  const((hdim, w2.shape[1]))],
        out_specs=pl.BlockSpec((1, n, w2.shape[1]), lambda bi: (bi, 0, 0)),
        out_shape=jax.ShapeDtypeStruct((b, n, w2.shape[1]), CDT),
        compiler_params=_cparams(("parallel",)),
        name="nsa_compress",
    )(xc, w1a, w1b, pea, peb, w2)


def _nsa_cmp_kernel(slopes_ref, q_ref, kv_ref, oc_ref, sb_ref, used_ref, *, tq, n_topk):
    qi = pl.program_id(1)
    nblk = kv_ref.shape[1]
    q0 = qi * tq
    rowpos = q0 + lax.broadcasted_iota(jnp.int32, (tq, 1), 0)
    cmp_end = lax.broadcasted_iota(jnp.int32, (1, nblk), 1) * CMP_STRIDE + (CMP_LEN - 1)
    negmask = jnp.where(rowpos >= cmp_end, 0.0, NEG)
    end_rel = (cmp_end - q0).astype(F32)
    lane = lax.broadcasted_iota(jnp.int32, (tq, LANES), 1)
    low = lane < NSA_DH
    nn = lax.broadcasted_iota(jnp.int32, (NSA_DH, nblk), 1) * CMP_STRIDE
    jj = lax.broadcasted_iota(jnp.int32, (NSA_DH, nblk), 0) * SLC_LEN
    ovt = (jnp.maximum(jnp.minimum(nn + CMP_LEN, jj + SLC_LEN) - jnp.maximum(nn, jj), 0).astype(F32)
           * (1.0 / CMP_LEN)).astype(CDT)
    jt = lax.broadcasted_iota(jnp.int32, (NSA_DH, tq), 0).astype(F32)
    blk = ((q0 + lax.broadcasted_iota(jnp.int32, (1, tq), 1)) >> SLC_SHIFT).astype(F32)
    fixed = (jt == 0.0) | (jt == blk) | (jt == blk - 1.0)
    out_of_play = fixed | (jt > blk)
    row_ok = rowpos >= CMP_LEN - 1
    outs = []
    bias = []
    for g in range(NSA_GROUPS):
        kc = kv_ref[0, :, g * LANES:(g + 1) * LANES]
        vc = kv_ref[0, :, (NSA_GROUPS + g) * LANES:(NSA_GROUPS + g + 1) * LANES]
        psum = jnp.zeros((tq, nblk), F32)
        for j in range(NSA_HPG):
            qb = q_ref[0, :, j * LANES:(j + 1) * LANES]
            qm = jnp.where(low if g == 0 else jnp.logical_not(low), qb, jnp.zeros_like(qb))
            s = _dot_nt(qm, kc) + slopes_ref[g * NSA_HPG + j] * end_rel + negmask
            e = jnp.exp2(s - jnp.max(s, axis=-1, keepdims=True))
            den = jnp.sum(e, axis=-1, keepdims=True)
            p = e * jnp.where(row_ok, 1.0 / den, 0.0)
            psum = psum + p
            outs.append(_dot(p.astype(CDT), vc))
        hi = psum.astype(CDT)
        lo = (psum - hi.astype(F32)).astype(CDT)
        imp = _dot_nt(ovt, hi) + _dot_nt(ovt, lo)
        imp = jnp.where(out_of_play, -jnp.inf, imp)
        sbt = jnp.where(fixed, 0.0, NEG)
        for _ in range(n_topk - 3):
            mx = jnp.max(imp, axis=0, keepdims=True)
            idx = jnp.min(jnp.where(imp == mx, jt, float(LANES)), axis=0, keepdims=True)
            hit = jt == idx
            sbt = jnp.where(hit, 0.0, sbt)
            imp = jnp.where(hit, -jnp.inf, imp)
        bias.append(sbt)
    sb = jnp.concatenate([bias[1], bias[0]], axis=0).T
    sb_ref[0] = sb.astype(sb_ref.dtype)
    used = jnp.max(jnp.where(sb == 0.0, 1.0, 0.0), axis=0, keepdims=True)
    used_ref[0, 0] = jnp.broadcast_to(used, used_ref.shape[2:])
    for blk_i in range(NSA_HEADS // 2):
        oc_ref[0, :, blk_i * LANES:(blk_i + 1) * LANES] = jnp.where(
            low, outs[2 * blk_i], outs[2 * blk_i + 1]).astype(oc_ref.dtype)


def _nsa_cmp_select(proj3, kvc, n_topk):
    assert n_topk >= 3, "the three always-selected blocks must fit in the top-k budget"
    b, s, _ = proj3.shape
    tq = min(256, s)
    nblk = kvc.shape[1]
    return pl.pallas_call(
        functools.partial(_nsa_cmp_kernel, tq=tq, n_topk=n_topk),
        grid=(b, s // tq),
        in_specs=[
            pl.BlockSpec(memory_space=pltpu.SMEM),
            pl.BlockSpec((1, tq, 4 * LANES), lambda bi, qi: (bi, qi, PB_DQ // 4)),
            pl.BlockSpec((1, nblk, kvc.shape[2]), lambda bi, qi: (bi, 0, 0)),
        ],
        out_specs=[
            pl.BlockSpec((1, tq, NSA_HEADS * NSA_DH), lambda bi, qi: (bi, qi, 0)),
            pl.BlockSpec((1, tq, LANES), lambda bi, qi: (bi, qi, 0)),
            pl.BlockSpec((1, 1, 8, LANES), lambda bi, qi: (bi, qi, 0, 0)),
        ],
        out_shape=[jax.ShapeDtypeStruct((b, s, NSA_HEADS * NSA_DH), CDT),
                   jax.ShapeDtypeStruct((b, s, LANES), CDT),
                   jax.ShapeDtypeStruct((b, s // tq, 8, LANES), F32)],
        compiler_params=_cparams(("parallel", "parallel")),
        name="nsa_cmp_select",
    )(jnp.asarray(_alibi_slopes(NSA_HEADS)), proj3, kvc)


def _compact_heads(heads, mine, low):
    both = [jnp.where(mine, a, pltpu.roll(a, NSA_DH, axis=1)) for a in heads]
    out = [jnp.where(low, both[2 * jj], both[2 * jj + 1]) for jj in range(NSA_HPG // 2)]
    return jnp.concatenate(out, axis=1)


def _nsa_win_kernel(slopes_ref, q_ref, kp_ref, kc_ref, vp_ref, vc_ref, o_ref, *, tq):
    g = pl.program_id(1)
    qi = pl.program_id(2)
    lane = lax.broadcasted_iota(jnp.int32, (tq, LANES), 1)
    low = lane < NSA_DH
    mine = (lane >> HALF_SHIFT) == g
    r = lax.broadcasted_iota(jnp.int32, (tq, tq), 0)
    c = lax.broadcasted_iota(jnp.int32, (tq, tq), 1)
    own = c <= r
    ndist = jnp.where(own, c - r, c - r - tq).astype(F32)
    own_f = jnp.where(own, 1.0, 0.0).astype(CDT)
    prev_pen = jnp.where(qi > 0, 0.0, NEG)
    kc, kp = kc_ref[0], kp_ref[0]
    vc, vp = _with_ones(vc_ref[0]), _with_ones(vp_ref[0])
    heads = []
    for j in range(NSA_HPG):
        qb = q_ref[0, :, j * LANES:(j + 1) * LANES]
        qm = jnp.where(mine, qb, jnp.zeros_like(qb))
        s = jnp.where(own, _dot_nt(qm, kc), _dot_nt(qm, kp) + prev_pen) + slopes_ref[g * NSA_HPG + j] * ndist
        p = jnp.exp2(s - jnp.max(s, axis=-1, keepdims=True)).astype(CDT)
        p_own = p * own_f
        acc = _dot(p_own, vc) + _dot(p - p_own, vp)
        heads.append(_flash_result(acc))
    o_ref[0] = _compact_heads(heads, mine, low).astype(o_ref.dtype)


def _nsa_window(proj3):
    b, s, _ = proj3.shape
    tq = WINDOW
    return pl.pallas_call(
        functools.partial(_nsa_win_kernel, tq=tq),
        grid=(b, NSA_GROUPS, s // tq),
        in_specs=[
            pl.BlockSpec(memory_space=pltpu.SMEM),
            pl.BlockSpec((1, tq, 4 * LANES), lambda bi, g, qi: (bi, qi, PB_DQ // 4)),
            pl.BlockSpec((1, tq, LANES), lambda bi, g, qi: (bi, jnp.maximum(qi - 1, 0), PB_WIN_K)),
            pl.BlockSpec((1, tq, LANES), lambda bi, g, qi: (bi, qi, PB_WIN_K)),
            pl.BlockSpec((1, tq, LANES), lambda bi, g, qi: (bi, jnp.maximum(qi - 1, 0), PB_WIN_V)),
            pl.BlockSpec((1, tq, LANES), lambda bi, g, qi: (bi, qi, PB_WIN_V)),
        ],
        out_specs=pl.BlockSpec((1, tq, NSA_HPG * NSA_DH), lambda bi, g, qi: (bi, qi, g)),
        out_shape=jax.ShapeDtypeStruct((b, s, NSA_HEADS * NSA_DH), CDT),
        compiler_params=_cparams(("parallel", "parallel", "parallel")),
        name="nsa_window",
    )(jnp.asarray(_alibi_slopes(NSA_HEADS)), proj3, proj3, proj3, proj3, proj3)


def _nsa_sel_kernel(cnt_ref, qt_ref, kt_ref, lt_ref, slopes_ref, q_ref, sb_ref, k_ref, v_ref, oc_ref, ow_ref, gl_ref,
                    e_ref, o_ref, m_ref, acc_ref, sa_ref, sb2_ref, ma_ref, mb_ref, cm_ref, *, tq, rows_per_problem):
    g = pl.program_id(1)
    w = NSA_HPG * NSA_DH
    lane = lax.broadcasted_iota(jnp.int32, (tq, LANES), 1)
    low = lane < NSA_DH
    mine = (lane >> HALF_SHIFT) == g
    _flash_begin(m_ref, acc_ref, cm_ref, tq)
    col = lax.broadcasted_iota(jnp.int32, (1, tq), 1).astype(F32)
    jl = lane & (NSA_DH - 1)
    krow = lax.broadcasted_iota(jnp.int32, (tq, LANES), 0)

    def produce(buf, qi, ki, diag):
        q = _tile(q_ref, qi, tq)
        sb = _tile(sb_ref, qi, tq)
        qa = jnp.concatenate([jnp.where(mine, q[:, j * LANES:(j + 1) * LANES], sb) for j in range(NSA_HPG)], axis=0)
        k = _tile(k_ref, ki, tq)
        onehot = jnp.where(((ki * tq + krow) >> SLC_SHIFT) == jl, 1.0, 0.0).astype(k.dtype)
        s_all = _dot_nt(qa, jnp.where(mine, k, onehot))
        rel = ((ki - qi) * tq).astype(F32)
        for j in range(NSA_HPG):
            rows = slice(j * tq, (j + 1) * tq)
            _put_logits(buf, s_all[rows] + slopes_ref[g * NSA_HPG + j] * (col + rel), diag, cm_ref, rows)

    def consume(buf, ki, diag):
        _flash_consume(buf, _tile(v_ref, ki, tq), m_ref, acc_ref, cm_ref if diag else None)

    def finish(qi):
        o = _flash_result(acc_ref[...])
        o_s = _compact_heads([o[j * tq:(j + 1) * tq] for j in range(NSA_HPG)], mine, low)
        gates = _split_dot(jax.nn.sigmoid(_tile(gl_ref, qi, tq)), e_ref[0])
        y = (gates[:, 0:w] * _tile(oc_ref, qi, tq).astype(F32) + gates[:, w:2 * w] * o_s
             + gates[:, 2 * w:3 * w] * _tile(ow_ref, qi, tq).astype(F32))
        o_ref[0, pl.ds(pl.multiple_of(qi * tq, tq), tq), :] = y.astype(o_ref.dtype)
        _flash_reset(m_ref, acc_ref)

    prob = pl.program_id(0) * NSA_GROUPS + g
    _flash_stream(cnt_ref[prob], (qt_ref, kt_ref, lt_ref), prob * rows_per_problem, produce, consume, finish,
                  (sa_ref, ma_ref), (sb2_ref, mb_ref), mask_at_produce=True)


def _nsa_selected(proj3, sbias, used, o_c, o_w, small3, expand):
    b, s, _ = proj3.shape
    tq = min(256, s)
    nq = s // tq
    w = NSA_HPG * NSA_DH
    u = used[:, :, 0, :].reshape(b, nq, NSA_GROUPS, NSA_DH)[:, :, ::-1, :nq * (tq // SLC_LEN)]
    flags = (u.reshape(b, nq, NSA_GROUPS, nq, tq // SLC_LEN).max(axis=-1) > 0.0).astype(jnp.int32)
    flags = flags.transpose(0, 2, 1, 3)
    qt = jnp.arange(nq, dtype=jnp.int32)
    need = jnp.where(qt[None, :] < qt[:, None], flags, (qt[None, :] == qt[:, None]).astype(jnp.int32))
    need = need.reshape(b, NSA_GROUPS, nq * nq)
    cnt = need.sum(axis=-1).astype(jnp.int32)
    order = jnp.argsort(1 - need, axis=-1, stable=True).astype(jnp.int32)
    order = jnp.pad(order, ((0, 0), (0, 0), (0, 2)))
    rows = nq * nq + 2
    sched = (order // nq, order % nq, (order // nq == order % nq).astype(jnp.int32))
    return pl.pallas_call(
        functools.partial(_nsa_sel_kernel, tq=tq, rows_per_problem=rows),
        grid=(b, NSA_GROUPS),
        in_specs=[
            _SMEM, _SMEM, _SMEM, _SMEM, _SMEM,
            pl.BlockSpec((1, s, 4 * LANES), lambda bi, g: (bi, 0, PB_DQ // 4)),
            pl.BlockSpec((1, s, LANES), lambda bi, g: (bi, 0, 0)),
            pl.BlockSpec((1, s, LANES), lambda bi, g: (bi, 0, PB_SEL_K)),
            pl.BlockSpec((1, s, LANES), lambda bi, g: (bi, 0, PB_SEL_V)),
            pl.BlockSpec((1, s, w), lambda bi, g: (bi, 0, g)),
            pl.BlockSpec((1, s, w), lambda bi, g: (bi, 0, g)),
            pl.BlockSpec((1, s, LANES), lambda bi, g: (bi, 0, 0)),
            pl.BlockSpec((1, LANES, 3 * w), lambda bi, g: (g, 0, 0)),
        ],
        out_specs=pl.BlockSpec((1, s, w), lambda bi, g: (bi, 0, g)),
        out_shape=jax.ShapeDtypeStruct((b, s, NSA_HEADS * NSA_DH), CDT),
        scratch_shapes=_flash_scratch(NSA_HPG * tq, tq),
        compiler_params=_cparams(("parallel", "parallel")),
        name="nsa_selected",
    )(cnt.reshape(-1), *[t.reshape(-1) for t in sched], jnp.asarray(_alibi_slopes(NSA_HEADS)),
      proj3, sbias, proj3, proj3, o_c, o_w, small3, expand)


def _merge_kernel(ya_ref, yb_ref, yc_ref, yd_ref, ga_ref, gb_ref, gc_ref, gd_ref, wb_ref, wo_ref, x_ref, o_ref):
    merged = None
    for n, (y_ref, g_ref) in enumerate(((ya_ref, ga_ref), (yb_ref, gb_ref), (yc_ref, gc_ref), (yd_ref, gd_ref))):
        t = jax.nn.sigmoid(g_ref[...].astype(F32)) * _dot(y_ref[...], wb_ref[n])
        merged = t if merged is None else merged + t
    o_ref[...] = x_ref[...] + _dot(merged.astype(CDT), wo_ref[...])


def _merge(ys, proj2, wb, wo, x2):
    t, d = x2.shape
    tm = min(512, t)
    gate_blk = PB_GATE * LANES // d
    yspec = pl.BlockSpec((tm, BRANCH_WIDTH), lambda i: (i, 0))
    gspecs = [pl.BlockSpec((tm, d), functools.partial(lambda i, n: (i, gate_blk + n), n=n)) for n in range(N_BRANCH)]
    return pl.pallas_call(
        _merge_kernel,
        grid=(t // tm,),
        in_specs=[yspec] * N_BRANCH + gspecs + [
            pl.BlockSpec((N_BRANCH, BRANCH_WIDTH, d), lambda i: (0, 0, 0)),
            pl.BlockSpec((d, d), lambda i: (0, 0)),
            pl.BlockSpec((tm, d), lambda i: (i, 0)),
        ],
        out_specs=pl.BlockSpec((tm, d), lambda i: (i, 0)),
        out_shape=jax.ShapeDtypeStruct((t, d), F32),
        compiler_params=_cparams(("parallel",)),
        name="merge",
    )(*ys, proj2, proj2, proj2, proj2, wb, wo, x2)


HALO = 16


def _ffn_kernel(x_ref, xh_ref, g_ref, wu_ref, cw_ref, cb_ref, wd_ref, gf_ref, o_ref, he_ref, u_ref, act_ref,
                *, tm, fc, final):
    i = pl.program_id(1)
    x = x_ref[0]
    g = g_ref[...]
    xh = xh_ref[0] * (i > 0).astype(F32)
    he_ref[0:HALO] = _rms(xh, g).astype(CDT)
    he_ref[HALO:HALO + tm] = _rms(x, g).astype(CDT)
    he = he_ref[...]
    for c in range(D_FF // fc):
        outs = []
        for half in range(2):
            ub = u_ref.at[c % 2, half]
            lo = half * D_FF + c * fc
            ub[...] = _dot(he, wu_ref[:, lo:lo + fc])
            conv = cb_ref[:, lo:lo + fc]
            for kk in range(CONV_WIDTH):
                off = HALO - (CONV_WIDTH - 1) + kk
                conv = conv + cw_ref[kk:kk + 1, lo:lo + fc] * ub[off:off + tm, :]
            outs.append(conv)
        a, gg = outs
        act_ref[:, c * fc:(c + 1) * fc] = (a * jax.nn.sigmoid(a) * gg).astype(CDT)
    y = x + _dot(act_ref[...], wd_ref[...])
    if final:
        y = _rms(y, gf_ref[...])
    o_ref[0] = y


def _ffn(x3, g, wu, cw, cb, wd, gf, final):
    b, s, d = x3.shape
    tm = min(512, s)
    fc = 256
    const = lambda shape: pl.BlockSpec(shape, lambda bi, i: (0,) * len(shape), pipeline_mode=pl.Buffered(1))
    return pl.pallas_call(
        functools.partial(_ffn_kernel, tm=tm, fc=fc, final=final),
        grid=(b, s // tm),
        in_specs=[
            pl.BlockSpec((1, tm, d), lambda bi, i: (bi, i, 0)),
            pl.BlockSpec((1, HALO, d), lambda bi, i: (bi, jnp.maximum(i * (tm // HALO) - 1, 0), 0)),
            const((1, d)), const((d, 2 * D_FF)), const((CONV_WIDTH, 2 * D_FF)), const((1, 2 * D_FF)),
            const((D_FF, d)), const((1, d)),
        ],
        out_specs=pl.BlockSpec((1, tm, d), lambda bi, i: (bi, i, 0)),
        out_shape=jax.ShapeDtypeStruct((b, s, d), F32),
        scratch_shapes=[pltpu.VMEM((tm + HALO, d), CDT), pltpu.VMEM((2, 2, tm + HALO, fc), F32),
                        pltpu.VMEM((tm, D_FF), CDT)],
        compiler_params=_cparams(("parallel", "arbitrary")),
        name="conv_glu_mlp",
    )(x3, x3, g.reshape(1, d), wu, cw, cb.reshape(1, -1), wd, gf.reshape(1, d))


def _prep_w_in(w):
    widths = (512, 512, 512, MLA_Q_LORA, MLA_KV_LORA, MLA_ROPE, 512, 512, 512, FOX_HEADS,
              512, 768, 3 * NSA_HEADS, N_BRANCH * D_MODEL)
    offs = np.cumsum((0,) + widths)
    (a_q, a_k, a_v, b_cq, b_ckv, b_kr, c_q, c_k, c_v, c_f, d_q, d_kv, d_g, gate) = [
        w[:, offs[i]:offs[i + 1]] for i in range(len(widths))]
    d = w.shape[0]
    d_q = d_q.reshape(d, NSA_GROUPS, NSA_HPG, NSA_DH).transpose(0, 2, 1, 3).reshape(d, 512)
    half = MLA_ROPE // 2
    kr_swap = jnp.concatenate([-b_kr[:, half:], b_kr[:, :half]], axis=1)
    z64 = jnp.zeros((d, LANES - MLA_ROPE), w.dtype)
    big = jnp.concatenate([
        a_q * (LOG2E * DIFF_DH ** -0.5), a_k, a_v,
        c_q * (LOG2E * FOX_DH ** -0.5), c_k, c_v,
        d_q * (LOG2E * NSA_DH ** -0.5), d_kv,
        b_cq, b_ckv, b_kr, z64, kr_swap, z64,
        gate], axis=1)
    small = jnp.concatenate([c_f, d_g, jnp.zeros((d, LANES - FOX_HEADS - 3 * NSA_HEADS), w.dtype)], axis=1)
    return big.astype(CDT), small.astype(CDT)


def _prep_mla(w_uq, w_ukv):
    r = w_uq.shape[0]
    hw = 2 * LANES
    half = MLA_ROPE // 2
    scale = LOG2E * (MLA_NOPE + MLA_ROPE) ** -0.5
    wq = (w_uq * scale).reshape(r, MLA_HEADS, MLA_NOPE + MLA_ROPE)
    nope, t1, t2 = wq[..., :MLA_NOPE], wq[..., MLA_NOPE:MLA_NOPE + half], wq[..., MLA_NOPE + half:]
    zpad = jnp.zeros((r, MLA_HEADS, hw - MLA_NOPE - MLA_ROPE), w_uq.dtype)
    wqm = jnp.concatenate([nope, t1, t2, zpad], axis=-1).reshape(r, MLA_HEADS * hw)
    wqs = jnp.concatenate([jnp.zeros_like(nope), -t2, t1, zpad], axis=-1).reshape(r, MLA_HEADS * hw)
    wkv = w_ukv.reshape(w_ukv.shape[0], MLA_HEADS, MLA_NOPE + MLA_VDIM)
    wk = wkv[..., :MLA_NOPE].reshape(-1, MLA_HEADS * MLA_NOPE)
    wv = wkv[..., MLA_NOPE:].reshape(-1, MLA_HEADS * MLA_VDIM)
    return wqm.astype(CDT), wqs.astype(CDT), wk.astype(CDT), wv.astype(CDT)


def _rope_tables(s):
    half = MLA_ROPE // 2
    inv_freq = ROPE_THETA ** (-jnp.arange(0, MLA_ROPE, 2, dtype=F32) / MLA_ROPE)
    ang = jnp.arange(s, dtype=F32)[:, None] * inv_freq[None, :]
    cos, sin = jnp.cos(ang), jnp.sin(ang)
    z = jnp.zeros((s, LANES - MLA_ROPE), F32)
    cosk = jnp.concatenate([cos, cos, z], axis=1)
    sink = jnp.concatenate([sin, sin, z], axis=1)
    cosq = jnp.concatenate([jnp.ones((s, MLA_NOPE), F32), cosk], axis=1)
    sinq = jnp.concatenate([jnp.zeros((s, MLA_NOPE), F32), sink], axis=1)
    return cosq, sinq, cosk, sink


def _prep_compress(pe, w1, w2):
    eye2 = jnp.eye(2, dtype=F32)
    w1r = w1.reshape(2, CMP_LEN, NSA_DH, CMP_HIDDEN)

    def expand(wpart):
        t = jnp.einsum('kpdh,kK,gG->pkgdKGh', wpart, eye2, eye2)
        return t.reshape(CMP_STRIDE * 4 * NSA_DH, 4 * CMP_HIDDEN)

    w1a, w1b = expand(w1r[:, :CMP_STRIDE]), expand(w1r[:, CMP_STRIDE:])

    def pe_row(p):
        t = jnp.broadcast_to(p.transpose(1, 0, 2)[:, :, None, :], (CMP_STRIDE, 2, NSA_GROUPS, NSA_DH))
        return jnp.pad(t.reshape(1, -1), ((0, 7), (0, 0)))

    pea, peb = pe_row(pe[:, :CMP_STRIDE]), pe_row(pe[:, CMP_STRIDE:])
    w2b = jnp.einsum('khd,kK,gG,u->kghKGud', w2, eye2, eye2, jnp.ones((2,), F32))
    w2b = w2b.reshape(4 * CMP_HIDDEN, 4 * 2 * NSA_DH)
    return w1a.astype(CDT), w1b.astype(CDT), pea.astype(CDT), peb.astype(CDT), w2b.astype(CDT)


def _gate_expand():
    e = np.zeros((NSA_GROUPS, LANES, 3, NSA_HPG, NSA_DH), np.float32)
    for g in range(NSA_GROUPS):
        for j in range(NSA_HPG):
            for br in range(3):
                e[g, SMALL_G + (g * NSA_HPG + j) * 3 + br, br, j, :] = 1.0
    return jnp.asarray(e.reshape(NSA_GROUPS, LANES, 3 * NSA_HPG * NSA_DH)).astype(CDT)


def _token_mixers(x3, l, norm_mix, w_in, diff_lambda, diff_subln, mla_norm_q, mla_w_uq, mla_norm_kv, mla_w_ukv,
                  fox_b_f, nsa_cmp_pe, nsa_cmp_w1, nsa_cmp_w2, w_branch, w_out, rope_tabs):
    b, s, d = x3.shape
    t = b * s
    x2 = x3.reshape(t, d)
    w_big, w_small = _prep_w_in(w_in)
    proj, small = _in_proj(x2, norm_mix, w_big, w_small)
    proj3 = proj.reshape(b, s, N_PROJ)
    small3 = small.reshape(b, s, LANES)

    lam_init = 0.8 - 0.6 * math.exp(-0.3 * l)
    y_a = _diff_attention(proj3, diff_lambda, diff_subln, lam_init)

    wqm, wqs, wk, wv = _prep_mla(mla_w_uq, mla_w_ukv)
    qc, kc, vv = _mla_prep(proj3, mla_norm_q, mla_norm_kv, wqm, wqs, wk, wv, rope_tabs)
    y_b = _mla_attention(qc, kc, vv)

    cf_rows = small3[:, :, SMALL_F:SMALL_F + FOX_HEADS].transpose(0, 2, 1).reshape(b * FOX_HEADS, s)
    bias_rows = jnp.tile(fox_b_f.astype(F32), b).reshape(b * FOX_HEADS, 1)
    c4 = _fox_cumsum(cf_rows, bias_rows)
    y_c = _fox_attention(proj3, c4)

    w1a, w1b, pea, peb, w2b = _prep_compress(nsa_cmp_pe, nsa_cmp_w1, nsa_cmp_w2)
    xc = proj3[:, :, PB_CMP_K * LANES:(PB_CMP_V + 1) * LANES].reshape(b, s // CMP_STRIDE, CMP_STRIDE * 2 * LANES)
    kvc = _nsa_compress(xc, w1a, w1b, pea, peb, w2b)
    n_topk = min(SLC_TOPK, s // SLC_LEN)
    o_c, sbias, used = _nsa_cmp_select(proj3, kvc, n_topk)
    o_w = _nsa_window(proj3)
    y_d = _nsa_selected(proj3, sbias, used, o_c, o_w, small3, _gate_expand())

    ys = [y.reshape(t, BRANCH_WIDTH) for y in (y_a, y_b, y_c, y_d)]
    return _merge(ys, proj, w_branch.astype(CDT), w_out.astype(CDT), x2).reshape(b, s, d)


def kernel(x, norm_mix, w_in, diff_lambda, diff_subln, mla_norm_q, mla_w_uq, mla_norm_kv, mla_w_ukv, fox_b_f,
           nsa_cmp_pe, nsa_cmp_w1, nsa_cmp_w2, w_branch, w_out, norm_ffn, w_up, conv_w, conv_b, w_down, norm_final):
    depth = w_in.shape[0]
    s = x.shape[1]
    rope_tabs = _rope_tables(s)
    for l in range(depth):
        x = _token_mixers(x, l, norm_mix[l], w_in[l], diff_lambda[l], diff_subln[l], mla_norm_q[l], mla_w_uq[l],
                          mla_norm_kv[l], mla_w_ukv[l], fox_b_f[l], nsa_cmp_pe[l], nsa_cmp_w1[l], nsa_cmp_w2[l],
                          w_branch[l], w_out[l], rope_tabs)
        x = _ffn(x, norm_ffn[l], w_up[l].astype(CDT), conv_w[l], conv_b[l], w_down[l].astype(CDT), norm_final,
                 final=(l == depth - 1))
    return x
```

```python
import functools
import math

import numpy as np
import jax
import jax.numpy as jnp
from jax import lax
from jax.experimental import pallas as pl
from jax.experimental.pallas import tpu as pltpu

F32 = jnp.float32
CDT = jnp.bfloat16

NEG = -1e30
NEG_INF = -1e30
BIG = 1e9
NORM_EPS = 1e-6
LOG2E = 1.4426950408889634
LANES = 128

D_MODEL = 1024
DIFF_HEADS, DIFF_DH = 4, 64
MLA_HEADS, MLA_NOPE, MLA_ROPE, MLA_VDIM = 4, 128, 64, 128
MLA_Q_LORA, MLA_KV_LORA = 256, 256
ROPE_THETA = 10000.0
FOX_HEADS, FOX_DH = 4, 128
NSA_HEADS, NSA_GROUPS, NSA_DH = 8, 2, 64
NSA_HPG = NSA_HEADS // NSA_GROUPS
CMP_STRIDE = 16
CMP_LEN = 2 * CMP_STRIDE
CMP_HIDDEN = 128
SLC_LEN = 64
SLC_SHIFT = 6
HALF_SHIFT = 6
SLC_TOPK = 8
WINDOW = 256
N_BRANCH = 4
BRANCH_WIDTH = 512
D_FF = 2816
CONV_WIDTH = 3

PB_AQ, PB_AK, PB_AV = 0, 4, 8
PB_CQ, PB_CK, PB_CV = 12, 16, 20
PB_DQ = 24
PB_CMP_K, PB_CMP_V, PB_SEL_K, PB_SEL_V, PB_WIN_K, PB_WIN_V = 28, 29, 30, 31, 32, 33
PB_BCQ, PB_BCKV, PB_BKR, PB_BKRS = 34, 36, 38, 39
PB_GATE = 40
N_PROJ = 72 * LANES
SMALL_F, SMALL_G = 0, 4

VMEM_LIMIT = 56 * 1024 * 1024


def _cparams(sem):
    return pltpu.CompilerParams(dimension_semantics=sem, vmem_limit_bytes=VMEM_LIMIT)


def _rms(xf, g):
    return xf * lax.rsqrt(jnp.mean(xf * xf, axis=-1, keepdims=True) + NORM_EPS) * g


def _sigmoid(x):
    return 0.5 * jnp.tanh(0.5 * x) + 0.5


def _dot(a, b):
    return jnp.dot(a, b, preferred_element_type=F32)


def _dot_nt(a, b):
    return lax.dot_general(a, b, (((1,), (1,)), ((), ())), preferred_element_type=F32)


def _split_dot(a, b):
    hi = a.astype(CDT)
    lo = (a - hi.astype(F32)).astype(CDT)
    return _dot(hi, b) + _dot(lo, b)


def _alibi_slopes(n):
    return (LOG2E * np.exp2(-8.0 * np.arange(1, n + 1) / n)).astype(np.float32)


def _inproj_kernel(x_ref, g_ref, w_ref, ws_ref, o_ref, os_ref, h_ref):
    @pl.when(pl.program_id(1) == 0)
    def _():
        h = _rms(x_ref[...], g_ref[...]).astype(CDT)
        h_ref[...] = h
        os_ref[...] = _dot(h, ws_ref[...])

    o_ref[...] = _dot(h_ref[...], w_ref[...]).astype(o_ref.dtype)


def _in_proj(x2, g, w, ws):
    t, d = x2.shape
    n = w.shape[1]
    tm = min(1024, t)
    tn = 2304 if n % 2304 == 0 else 1024
    return pl.pallas_call(
        _inproj_kernel,
        grid=(t // tm, n // tn),
        in_specs=[
            pl.BlockSpec((tm, d), lambda i, j: (i, 0)),
            pl.BlockSpec((1, d), lambda i, j: (0, 0)),
            pl.BlockSpec((d, tn), lambda i, j: (0, j)),
            pl.BlockSpec((d, LANES), lambda i, j: (0, 0)),
        ],
        out_specs=[
            pl.BlockSpec((tm, tn), lambda i, j: (i, j)),
            pl.BlockSpec((tm, LANES), lambda i, j: (i, 0)),
        ],
        out_shape=[jax.ShapeDtypeStruct((t, n), CDT), jax.ShapeDtypeStruct((t, LANES), F32)],
        scratch_shapes=[pltpu.VMEM((tm, d), CDT)],
        compiler_params=_cparams(("parallel", "arbitrary")),
        name="in_proj",
    )(x2, g.reshape(1, d), w, ws)


def _fox_cumsum_kernel(cf_ref, bf_ref, o_ref):
    rows, s = cf_ref.shape
    lane = lax.broadcasted_iota(jnp.int32, (rows, LANES), 1)
    carry = jnp.zeros((rows, 1), F32)
    for c in range(s // LANES):
        z = cf_ref[:, c * LANES:(c + 1) * LANES] + bf_ref[...]
        xs = jnp.minimum(z, 0.0) - jnp.log1p(jnp.exp(-jnp.abs(z)))
        d = 1
        while d < LANES:
            xs = xs + jnp.where(lane >= d, pltpu.roll(xs, d, axis=1), 0.0)
            d *= 2
        xs = xs + carry
        o_ref[:, c * LANES:(c + 1) * LANES] = xs
        carry = xs[:, LANES - 1:LANES]


def _fox_cumsum(cf_rows, bias_rows):
    return pl.pallas_call(
        _fox_cumsum_kernel,
        out_shape=jax.ShapeDtypeStruct(cf_rows.shape, F32),
        name="fox_cumsum",
    )(cf_rows, bias_rows)


def _flash_scratch(rows, tk, mask_scratch=False):
    return [pltpu.VMEM((rows, LANES), F32), pltpu.VMEM((rows, 2 * LANES), F32),
            pltpu.VMEM((rows, tk), F32), pltpu.VMEM((rows, tk), F32),
            pltpu.VMEM((rows, LANES), F32), pltpu.VMEM((rows, LANES), F32)
            ] + ([pltpu.VMEM((rows, tk), F32)] if mask_scratch else [])


def _flash_reset(m_ref, acc_ref):
    m_ref[...] = jnp.full(m_ref.shape, NEG, F32)
    acc_ref[...] = jnp.zeros(acc_ref.shape, F32)


def _flash_begin(m_ref, acc_ref, cm_ref, tq):
    _flash_reset(m_ref, acc_ref)
    cm_ref[...] = _causal_bias(cm_ref.shape[0], cm_ref.shape[1], tq)


def _row_max(s):
    return jnp.broadcast_to(jnp.max(s, axis=-1, keepdims=True), (s.shape[0], LANES))


def _causal_bias(rows, tk, tq):
    r = lax.broadcasted_iota(jnp.int32, (rows, tk), 0) & (tq - 1)
    c = lax.broadcasted_iota(jnp.int32, (rows, tk), 1)
    return jnp.where(c <= r, 0.0, NEG)


def _put_logits(buf, s, diag, cm_ref, rows=slice(None)):
    if diag is True:
        s = s + cm_ref[rows]
    elif diag is not False:
        s = s + diag.astype(F32) * cm_ref[rows]
    buf[0][rows] = s
    buf[1][rows] = _row_max(s)


def _with_ones(v):
    return jnp.concatenate([v, jnp.ones((v.shape[0], LANES), v.dtype)], axis=1)


def _flash_consume(buf, v, m_ref, acc_ref, mask_tq=None):
    s = buf[0][...]
    m_cur = buf[1][...]
    if mask_tq is not None:
        s = s + _causal_bias(s.shape[0], s.shape[1], mask_tq)
        m_cur = _row_max(s)
    m_old = m_ref[...]
    m_new = jnp.maximum(m_old, m_cur)
    alpha = jnp.exp2(m_old - m_new)
    p = jnp.exp2(s - jnp.tile(m_new, (1, s.shape[1] // LANES))).astype(CDT)
    acc_ref[...] = jnp.tile(alpha, (1, 2)) * acc_ref[...] + _dot(p, _with_ones(v))
    m_ref[...] = m_new


def _flash_result(acc):
    return acc[:, :LANES] / acc[:, LANES:]


def _causal_schedule(nq):
    ent = [(qi, ki, int(ki == qi)) for qi in range(nq) for ki in range(qi + 1)]
    n = len(ent)
    a = np.asarray(ent + [ent[-1]] * 2, np.int32)
    return n, tuple(jnp.asarray(a[:, i]) for i in range(3))


def _flash_stream(n, sched, base, produce, consume, finish, buf_a, buf_b, mask_at_produce):
    qt, kt, lt = sched

    def step(cur, nxt, t, diag, next_diag):
        if nxt is not None:
            produce(nxt, qt[base + t + 1], kt[base + t + 1], next_diag if mask_at_produce else False)
        consume(cur, kt[base + t], diag and not mask_at_produce)
        if diag:
            finish(qt[base + t])

    produce(buf_a, qt[base], kt[base], mask_at_produce)

    def pair(j, c):
        t = 2 * j
        l0, l1 = lt[base + t], lt[base + t + 1]
        for d0 in (False, True):
            for d1 in (False, True):
                @pl.when(((l0 != 0) == d0) & ((l1 != 0) == d1))
                def _():
                    step(buf_a, buf_b, t, d0, d1)
                    step(buf_b, buf_a, t + 1, d1, lt[base + t + 2])
        return c

    lax.fori_loop(0, n // 2, pair, 0)

    def tail():
        step(buf_a, None, n - 1, True, None)

    if isinstance(n, int):
        if n % 2 == 1:
            tail()
    else:
        pl.when(n % 2 == 1)(tail)


def _tile(ref, i, t):
    return ref[0, pl.ds(pl.multiple_of(i * t, t), t), :]


def _diff_attn_kernel(qt_ref, kt_ref, lt_ref, slopes_ref, lam_ref, g_ref, q_ref, k_ref, v_ref, o_ref,
                      m_ref, acc_ref, sa_ref, sb_ref, ma_ref, mb_ref, *, tq, n, lam_init):
    slope = slopes_ref[pl.program_id(1)]
    _flash_reset(m_ref, acc_ref)
    col = lax.broadcasted_iota(jnp.int32, (1, tq), 1).astype(F32)
    lane = lax.broadcasted_iota(jnp.int32, (tq, LANES), 1)
    lf = lam_ref[...]
    lam = (jnp.exp(jnp.sum(lf[0:1] * lf[1:2], axis=-1, keepdims=True))
           - jnp.exp(jnp.sum(lf[2:3] * lf[3:4], axis=-1, keepdims=True)) + lam_init)

    def produce(buf, qi, ki, diag):
        q = _tile(q_ref, qi, tq)
        zero = jnp.zeros_like(q)
        qq = jnp.concatenate([jnp.where(lane < DIFF_DH, q, zero), jnp.where(lane >= DIFF_DH, q, zero)], axis=0)
        s = _dot_nt(qq, _tile(k_ref, ki, tq))
        _put_logits(buf, s + slope * (col + ((ki - qi) * tq).astype(F32)), diag, None)

    def consume(buf, ki, diag):
        _flash_consume(buf, _tile(v_ref, ki, tq), m_ref, acc_ref, tq if diag else None)

    def finish(qi):
        o = _flash_result(acc_ref[...])
        d = o[0:tq] - lam * o[tq:2 * tq]
        o_ref[0, pl.ds(pl.multiple_of(qi * tq, tq), tq), :] = (
            _rms(d, g_ref[...]) * (1.0 - lam_init)).astype(o_ref.dtype)
        _flash_reset(m_ref, acc_ref)

    _flash_stream(n, (qt_ref, kt_ref, lt_ref), 0, produce, consume, finish, (sa_ref, ma_ref), (sb_ref, mb_ref),
                  mask_at_produce=False)


_SMEM = pl.BlockSpec(memory_space=pltpu.SMEM)


def _diff_attention(proj3, diff_lambda, subln, lam_init):
    b, s, _ = proj3.shape
    tq = min(512, s)
    dv = 2 * DIFF_DH
    n, sched = _causal_schedule(s // tq)
    kern = functools.partial(_diff_attn_kernel, tq=tq, n=n, lam_init=lam_init)
    return pl.pallas_call(
        kern,
        grid=(b, DIFF_HEADS),
        in_specs=[
            _SMEM, _SMEM, _SMEM, _SMEM,
            pl.BlockSpec((4, DIFF_DH), lambda bi, h: (0, 0)),
            pl.BlockSpec((1, dv), lambda bi, h: (0, 0)),
            pl.BlockSpec((1, s, LANES), lambda bi, h: (bi, 0, PB_AQ + h)),
            pl.BlockSpec((1, s, LANES), lambda bi, h: (bi, 0, PB_AK + h)),
            pl.BlockSpec((1, s, LANES), lambda bi, h: (bi, 0, PB_AV + h)),
        ],
        out_specs=pl.BlockSpec((1, s, dv), lambda bi, h: (bi, 0, h)),
        out_shape=jax.ShapeDtypeStruct((b, s, DIFF_HEADS * dv), CDT),
        scratch_shapes=_flash_scratch(2 * tq, tq),
        compiler_params=_cparams(("parallel", "parallel")),
        name="diff_attention",
    )(*sched, jnp.asarray(_alibi_slopes(DIFF_HEADS)), diff_lambda, subln.reshape(1, dv), proj3, proj3, proj3)


def _mla_prep_kernel(cq_ref, ckv_ref, kr_ref, krs_ref, gq_ref, gkv_ref, wqm_ref, wqs_ref, wk_ref, wv_ref,
                     cosq_ref, sinq_ref, cosk_ref, sink_ref, q_ref, k_ref, v_ref):
    hq = _rms(cq_ref[0].astype(F32), gq_ref[...]).astype(CDT)
    qm = _dot(hq, wqm_ref[...])
    qs = _dot(hq, wqs_ref[...])
    cosq, sinq = cosq_ref[...], sinq_ref[...]
    hw = 2 * LANES
    for h in range(MLA_HEADS):
        sl = slice(h * hw, (h + 1) * hw)
        q_ref[0, :, sl] = (qm[:, sl] * cosq + qs[:, sl] * sinq).astype(q_ref.dtype)
    hkv = _rms(ckv_ref[0].astype(F32), gkv_ref[...]).astype(CDT)
    kn = _dot(hkv, wk_ref[...])
    v_ref[0] = _dot(hkv, wv_ref[...]).astype(v_ref.dtype)
    kpe = (kr_ref[0].astype(F32) * cosk_ref[...] + krs_ref[0].astype(F32) * sink_ref[...]).astype(k_ref.dtype)
    for h in range(MLA_HEADS):
        k_ref[0, :, h * hw:h * hw + LANES] = kn[:, h * LANES:(h + 1) * LANES].astype(k_ref.dtype)
        k_ref[0, :, h * hw + LANES:(h + 1) * hw] = kpe


def _mla_prep(proj3, gq, gkv, wqm, wqs, wk, wv, tabs):
    b, s, _ = proj3.shape
    tm = min(512, s)
    hw = 2 * LANES
    cosq, sinq, cosk, sink = tabs
    const = lambda shape: pl.BlockSpec(shape, lambda bi, i: (0,) * len(shape))
    return pl.pallas_call(
        _mla_prep_kernel,
        grid=(b, s // tm),
        in_specs=[
            pl.BlockSpec((1, tm, MLA_Q_LORA), lambda bi, i: (bi, i, PB_BCQ // 2)),
            pl.BlockSpec((1, tm, MLA_KV_LORA), lambda bi, i: (bi, i, PB_BCKV // 2)),
            pl.BlockSpec((1, tm, LANES), lambda bi, i: (bi, i, PB_BKR)),
            pl.BlockSpec((1, tm, LANES), lambda bi, i: (bi, i, PB_BKRS)),
            const((1, MLA_Q_LORA)), const((1, MLA_KV_LORA)),
            const((MLA_Q_LORA, MLA_HEADS * hw)), const((MLA_Q_LORA, MLA_HEADS * hw)),
            const((MLA_KV_LORA, MLA_HEADS * MLA_NOPE)), const((MLA_KV_LORA, MLA_HEADS * MLA_VDIM)),
            pl.BlockSpec((tm, hw), lambda bi, i: (i, 0)), pl.BlockSpec((tm, hw), lambda bi, i: (i, 0)),
            pl.BlockSpec((tm, LANES), lambda bi, i: (i, 0)), pl.BlockSpec((tm, LANES), lambda bi, i: (i, 0)),
        ],
        out_specs=[
            pl.BlockSpec((1, tm, MLA_HEADS * hw), lambda bi, i: (bi, i, 0)),
            pl.BlockSpec((1, tm, MLA_HEADS * hw), lambda bi, i: (bi, i, 0)),
            pl.BlockSpec((1, tm, MLA_HEADS * MLA_VDIM), lambda bi, i: (bi, i, 0)),
        ],
        out_shape=[
            jax.ShapeDtypeStruct((b, s, MLA_HEADS * hw), CDT),
            jax.ShapeDtypeStruct((b, s, MLA_HEADS * hw), CDT),
            jax.ShapeDtypeStruct((b, s, MLA_HEADS * MLA_VDIM), CDT),
        ],
        compiler_params=_cparams(("parallel", "parallel")),
        name="mla_prep",
    )(proj3, proj3, proj3, proj3, gq.reshape(1, -1), gkv.reshape(1, -1), wqm, wqs, wk, wv,
      cosq, sinq, cosk, sink)


def _plain_attn_kernel(qt_ref, kt_ref, lt_ref, q_ref, k_ref, v_ref, o_ref,
                       m_ref, acc_ref, sa_ref, sb_ref, ma_ref, mb_ref, *, tq, n):
    _flash_reset(m_ref, acc_ref)

    def produce(buf, qi, ki, diag):
        _put_logits(buf, _dot_nt(_tile(q_ref, qi, tq), _tile(k_ref, ki, tq)), diag, None)

    def consume(buf, ki, diag):
        _flash_consume(buf, _tile(v_ref, ki, tq), m_ref, acc_ref, tq if diag else None)

    def finish(qi):
        o_ref[0, pl.ds(pl.multiple_of(qi * tq, tq), tq), :] = _flash_result(acc_ref[...]).astype(o_ref.dtype)
        _flash_reset(m_ref, acc_ref)

    _flash_stream(n, (qt_ref, kt_ref, lt_ref), 0, produce, consume, finish, (sa_ref, ma_ref), (sb_ref, mb_ref),
                  mask_at_produce=False)


def _mla_attention(qc, kc, v):
    b, s, _ = qc.shape
    tq = min(512, s)
    hw = 2 * LANES
    n, sched = _causal_schedule(s // tq)
    return pl.pallas_call(
        functools.partial(_plain_attn_kernel, tq=tq, n=n),
        grid=(b, MLA_HEADS),
        in_specs=[
            _SMEM, _SMEM, _SMEM,
            pl.BlockSpec((1, s, hw), lambda bi, h: (bi, 0, h)),
            pl.BlockSpec((1, s, hw), lambda bi, h: (bi, 0, h)),
            pl.BlockSpec((1, s, MLA_VDIM), lambda bi, h: (bi, 0, h)),
        ],
        out_specs=pl.BlockSpec((1, s, MLA_VDIM), lambda bi, h: (bi, 0, h)),
        out_shape=jax.ShapeDtypeStruct((b, s, MLA_HEADS * MLA_VDIM), CDT),
        scratch_shapes=_flash_scratch(tq, tq),
        compiler_params=_cparams(("parallel", "parallel")),
        name="mla_attention",
    )(*sched, qc, kc, v)


def _fox_attn_kernel(qt_ref, kt_ref, lt_ref, c_ref, q_ref, k_ref, v_ref, o_ref,
                     m_ref, acc_ref, sa_ref, sb_ref, ma_ref, mb_ref, *, tq, n):
    _flash_reset(m_ref, acc_ref)

    def produce(buf, qi, ki, diag):
        s = _dot_nt(_tile(q_ref, qi, tq), _tile(k_ref, ki, tq))
        cbase = c_ref[0, 0, pl.ds(qi, 1), :][:, 0:1]
        _put_logits(buf, s + LOG2E * (cbase - c_ref[0, 0, pl.ds(ki, 1), :]), diag, None)

    def consume(buf, ki, diag):
        _flash_consume(buf, _tile(v_ref, ki, tq), m_ref, acc_ref, tq if diag else None)

    def finish(qi):
        o_ref[0, pl.ds(pl.multiple_of(qi * tq, tq), tq), :] = _flash_result(acc_ref[...]).astype(o_ref.dtype)
        _flash_reset(m_ref, acc_ref)

    _flash_stream(n, (qt_ref, kt_ref, lt_ref), 0, produce, consume, finish, (sa_ref, ma_ref), (sb_ref, mb_ref),
                  mask_at_produce=False)


def _fox_attention(proj3, c4):
    b, s, _ = proj3.shape
    tq = min(512, s)
    nk = s // tq
    n, sched = _causal_schedule(nk)
    return pl.pallas_call(
        functools.partial(_fox_attn_kernel, tq=tq, n=n),
        grid=(b, FOX_HEADS),
        in_specs=[
            _SMEM, _SMEM, _SMEM,
            pl.BlockSpec((1, 1, nk, tq), lambda bi, h: (bi, h, 0, 0)),
            pl.BlockSpec((1, s, FOX_DH), lambda bi, h: (bi, 0, PB_CQ + h)),
            pl.BlockSpec((1, s, FOX_DH), lambda bi, h: (bi, 0, PB_CK + h)),
            pl.BlockSpec((1, s, FOX_DH), lambda bi, h: (bi, 0, PB_CV + h)),
        ],
        out_specs=pl.BlockSpec((1, s, FOX_DH), lambda bi, h: (bi, 0, h)),
        out_shape=jax.ShapeDtypeStruct((b, s, FOX_HEADS * FOX_DH), CDT),
        scratch_shapes=_flash_scratch(tq, tq),
        compiler_params=_cparams(("parallel", "parallel")),
        name="fox_attention",
    )(*sched, c4.reshape(b, FOX_HEADS, nk, tq), proj3, proj3, proj3)


def _nsa_compress_kernel(x_ref, w1a_ref, w1b_ref, pea_ref, peb_ref, w2_ref, o_ref):
    x = x_ref[0]
    n = x.shape[0]
    pa = _dot(x, w1a_ref[...])
    pb = _dot(x, w1b_ref[...])
    pe = _dot(pea_ref[...], w1a_ref[...]) + _dot(peb_ref[...], w1b_ref[...])
    hid = pa + pltpu.roll(pb, n - 1, axis=0) + pe[0:1]
    act = 0.5 * hid * (1.0 + jnp.tanh(math.sqrt(2.0 / math.pi) * (hid + 0.044715 * hid * hid * hid)))
    o_ref[0] = _dot(act.astype(CDT), w2_ref[...]).astype(o_ref.dtype)


def _nsa_compress(xc, w1a, w1b, pea, peb, w2):
    b, n, kdim = xc.shape
    hdim = w1a.shape[1]
    const = lambda shape: pl.BlockSpec(shape, lambda bi: (0,) * len(shape))
    return pl.pallas_call(
        _nsa_compress_kernel,
        grid=(b,),
        in_specs=[pl.BlockSpec((1, n, kdim), lambda bi: (bi, 0, 0)),
                  const((kdim, hdim)), const((kdim, hdim)), const((8, kdim)), const((8, kdim)),
                  const((hdim, w2.shape[1]))],
        out_specs=pl.BlockSpec((1, n, w2.shape[1]), lambda bi: (bi, 0, 0)),
        out_shape=jax.ShapeDtypeStruct((b, n, w2.shape[1]), CDT),
        compiler_params=_cparams(("parallel",)),
        name="nsa_compress",
    )(xc, w1a, w1b, pea, peb, w2)


def _nsa_cmp_kernel(slopes_ref, q_ref, kv_ref, oc_ref, sb_ref, used_ref, *, tq, n_topk):
    qi = pl.program_id(1)
    nblk = kv_ref.shape[1]
    q0 = qi * tq
    rowpos = q0 + lax.broadcasted_iota(jnp.int32, (tq, 1), 0)
    cmp_end = lax.broadcasted_iota(jnp.int32, (1, nblk), 1) * CMP_STRIDE + (CMP_LEN - 1)
    negmask = jnp.where(rowpos >= cmp_end, 0.0, NEG)
    end_rel = (cmp_end - q0).astype(F32)
    lane = lax.broadcasted_iota(jnp.int32, (tq, LANES), 1)
    low = lane < NSA_DH
    nn = lax.broadcasted_iota(jnp.int32, (NSA_DH, nblk), 1) * CMP_STRIDE
    jj = lax.broadcasted_iota(jnp.int32, (NSA_DH, nblk), 0) * SLC_LEN
    ovt = (jnp.maximum(jnp.minimum(nn + CMP_LEN, jj + SLC_LEN) - jnp.maximum(nn, jj), 0).astype(F32)
           * (1.0 / CMP_LEN)).astype(CDT)
    jt = lax.broadcasted_iota(jnp.int32, (NSA_DH, tq), 0).astype(F32)
    blk = ((q0 + lax.broadcasted_iota(jnp.int32, (1, tq), 1)) >> SLC_SHIFT).astype(F32)
    fixed = (jt == 0.0) | (jt == blk) | (jt == blk - 1.0)
    out_of_play = fixed | (jt > blk)
    row_ok = rowpos >= CMP_LEN - 1
    outs = []
    bias = []
    for g in range(NSA_GROUPS):
        kc = kv_ref[0, :, g * LANES:(g + 1) * LANES]
        vc = kv_ref[0, :, (NSA_GROUPS + g) * LANES:(NSA_GROUPS + g + 1) * LANES]
        psum = jnp.zeros((tq, nblk), F32)
        for j in range(NSA_HPG):
            qb = q_ref[0, :, j * LANES:(j + 1) * LANES]
            qm = jnp.where(low if g == 0 else jnp.logical_not(low), qb, jnp.zeros_like(qb))
            s = _dot_nt(qm, kc) + slopes_ref[g * NSA_HPG + j] * end_rel + negmask
            e = jnp.exp2(s - jnp.max(s, axis=-1, keepdims=True))
            den = jnp.sum(e, axis=-1, keepdims=True)
            p = e * jnp.where(row_ok, 1.0 / den, 0.0)
            psum = psum + p
            outs.append(_dot(p.astype(CDT), vc))
        hi = psum.astype(CDT)
        lo = (psum - hi.astype(F32)).astype(CDT)
        imp = _dot_nt(ovt, hi) + _dot_nt(ovt, lo)
        imp = jnp.where(out_of_play, -jnp.inf, imp)
        sbt = jnp.where(fixed, 0.0, NEG)
        for _ in range(n_topk - 3):
            mx = jnp.max(imp, axis=0, keepdims=True)
            idx = jnp.min(jnp.where(imp == mx, jt, float(LANES)), axis=0, keepdims=True)
            hit = jt == idx
            sbt = jnp.where(hit, 0.0, sbt)
            imp = jnp.where(hit, -jnp.inf, imp)
        bias.append(sbt)
    sb = jnp.concatenate([bias[1], bias[0]], axis=0).T
    sb_ref[0] = sb.astype(sb_ref.dtype)
    used = jnp.max(jnp.where(sb == 0.0, 1.0, 0.0), axis=0, keepdims=True)
    used_ref[0, 0] = jnp.broadcast_to(used, used_ref.shape[2:])
    for blk_i in range(NSA_HEADS // 2):
        oc_ref[0, :, blk_i * LANES:(blk_i + 1) * LANES] = jnp.where(
            low, outs[2 * blk_i], outs[2 * blk_i + 1]).astype(oc_ref.dtype)


def _nsa_cmp_select(proj3, kvc, n_topk):
    assert n_topk >= 3, "the three always-selected blocks must fit in the top-k budget"
    b, s, _ = proj3.shape
    tq = min(256, s)
    nblk = kvc.shape[1]
    return pl.pallas_call(
        functools.partial(_nsa_cmp_kernel, tq=tq, n_topk=n_topk),
        grid=(b, s // tq),
        in_specs=[
            pl.BlockSpec(memory_space=pltpu.SMEM),
            pl.BlockSpec((1, tq, 4 * LANES), lambda bi, qi: (bi, qi, PB_DQ // 4)),
            pl.BlockSpec((1, nblk, kvc.shape[2]), lambda bi, qi: (bi, 0, 0)),
        ],
        out_specs=[
            pl.BlockSpec((1, tq, NSA_HEADS * NSA_DH), lambda bi, qi: (bi, qi, 0)),
            pl.BlockSpec((1, tq, LANES), lambda bi, qi: (bi, qi, 0)),
            pl.BlockSpec((1, 1, 8, LANES), lambda bi, qi: (bi, qi, 0, 0)),
        ],
        out_shape=[jax.ShapeDtypeStruct((b, s, NSA_HEADS * NSA_DH), CDT),
                   jax.ShapeDtypeStruct((b, s, LANES), CDT),
                   jax.ShapeDtypeStruct((b, s // tq, 8, LANES), F32)],
        compiler_params=_cparams(("parallel", "parallel")),
        name="nsa_cmp_select",
    )(jnp.asarray(_alibi_slopes(NSA_HEADS)), proj3, kvc)


def _compact_heads(heads, mine, low):
    both = [jnp.where(mine, a, pltpu.roll(a, NSA_DH, axis=1)) for a in heads]
    out = [jnp.where(low, both[2 * jj], both[2 * jj + 1]) for jj in range(NSA_HPG // 2)]
    return jnp.concatenate(out, axis=1)


def _nsa_win_kernel(slopes_ref, q_ref, kp_ref, kc_ref, vp_ref, vc_ref, o_ref, *, tq):
    g = pl.program_id(1)
    qi = pl.program_id(2)
    lane = lax.broadcasted_iota(jnp.int32, (tq, LANES), 1)
    low = lane < NSA_DH
    mine = (lane >> HALF_SHIFT) == g
    r = lax.broadcasted_iota(jnp.int32, (tq, tq), 0)
    c = lax.broadcasted_iota(jnp.int32, (tq, tq), 1)
    own = c <= r
    ndist = jnp.where(own, c - r, c - r - tq).astype(F32)
    own_f = jnp.where(own, 1.0, 0.0).astype(CDT)
    prev_pen = jnp.where(qi > 0, 0.0, NEG)
    kc, kp = kc_ref[0], kp_ref[0]
    vc, vp = _with_ones(vc_ref[0]), _with_ones(vp_ref[0])
    heads = []
    for j in range(NSA_HPG):
        qb = q_ref[0, :, j * LANES:(j + 1) * LANES]
        qm = jnp.where(mine, qb, jnp.zeros_like(qb))
        s = jnp.where(own, _dot_nt(qm, kc), _dot_nt(qm, kp) + prev_pen) + slopes_ref[g * NSA_HPG + j] * ndist
        p = jnp.exp2(s - jnp.max(s, axis=-1, keepdims=True)).astype(CDT)
        p_own = p * own_f
        acc = _dot(p_own, vc) + _dot(p - p_own, vp)
        heads.append(_flash_result(acc))
    o_ref[0] = _compact_heads(heads, mine, low).astype(o_ref.dtype)


def _nsa_window(proj3):
    b, s, _ = proj3.shape
    tq = WINDOW
    return pl.pallas_call(
        functools.partial(_nsa_win_kernel, tq=tq),
        grid=(b, NSA_GROUPS, s // tq),
        in_specs=[
            pl.BlockSpec(memory_space=pltpu.SMEM),
            pl.BlockSpec((1, tq, 4 * LANES), lambda bi, g, qi: (bi, qi, PB_DQ // 4)),
            pl.BlockSpec((1, tq, LANES), lambda bi, g, qi: (bi, jnp.maximum(qi - 1, 0), PB_WIN_K)),
            pl.BlockSpec((1, tq, LANES), lambda bi, g, qi: (bi, qi, PB_WIN_K)),
            pl.BlockSpec((1, tq, LANES), lambda bi, g, qi: (bi, jnp.maximum(qi - 1, 0), PB_WIN_V)),
            pl.BlockSpec((1, tq, LANES), lambda bi, g, qi: (bi, qi, PB_WIN_V)),
        ],
        out_specs=pl.BlockSpec((1, tq, NSA_HPG * NSA_DH), lambda bi, g, qi: (bi, qi, g)),
        out_shape=jax.ShapeDtypeStruct((b, s, NSA_HEADS * NSA_DH), CDT),
        compiler_params=_cparams(("parallel", "parallel", "parallel")),
        name="nsa_window",
    )(jnp.asarray(_alibi_slopes(NSA_HEADS)), proj3, proj3, proj3, proj3, proj3)


def _nsa_sel_kernel(cnt_ref, qt_ref, kt_ref, lt_ref, slopes_ref, q_ref, sb_ref, k_ref, v_ref, oc_ref, ow_ref, gl_ref,
                    e_ref, o_ref, m_ref, acc_ref, sa_ref, sb2_ref, ma_ref, mb_ref, cm_ref, *, tq, rows_per_problem):
    g = pl.program_id(1)
    w = NSA_HPG * NSA_DH
    lane = lax.broadcasted_iota(jnp.int32, (tq, LANES), 1)
    low = lane < NSA_DH
    mine = (lane >> HALF_SHIFT) == g
    _flash_begin(m_ref, acc_ref, cm_ref, tq)
    col = lax.broadcasted_iota(jnp.int32, (1, tq), 1).astype(F32)
    jl = lane & (NSA_DH - 1)
    krow = lax.broadcasted_iota(jnp.int32, (tq, LANES), 0)

    def produce(buf, qi, ki, diag):
        q = _tile(q_ref, qi, tq)
        sb = _tile(sb_ref, qi, tq)
        qa = jnp.concatenate([jnp.where(mine, q[:, j * LANES:(j + 1) * LANES], sb) for j in range(NSA_HPG)], axis=0)
        k = _tile(k_ref, ki, tq)
        onehot = jnp.where(((ki * tq + krow) >> SLC_SHIFT) == jl, 1.0, 0.0).astype(k.dtype)
        s_all = _dot_nt(qa, jnp.where(mine, k, onehot))
        rel = ((ki - qi) * tq).astype(F32)
        for j in range(NSA_HPG):
            rows = slice(j * tq, (j + 1) * tq)
            _put_logits(buf, s_all[rows] + slopes_ref[g * NSA_HPG + j] * (col + rel), diag, cm_ref, rows)

    def consume(buf, ki, diag):
        _flash_consume(buf, _tile(v_ref, ki, tq), m_ref, acc_ref, tq if diag else None)

    def finish(qi):
        o = _flash_result(acc_ref[...])
        o_s = _compact_heads([o[j * tq:(j + 1) * tq] for j in range(NSA_HPG)], mine, low)
        gates = _split_dot(_sigmoid(_tile(gl_ref, qi, tq)), e_ref[0])
        y = (gates[:, 0:w] * _tile(oc_ref, qi, tq).astype(F32) + gates[:, w:2 * w] * o_s
             + gates[:, 2 * w:3 * w] * _tile(ow_ref, qi, tq).astype(F32))
        o_ref[0, pl.ds(pl.multiple_of(qi * tq, tq), tq), :] = y.astype(o_ref.dtype)
        _flash_reset(m_ref, acc_ref)

    prob = pl.program_id(0) * NSA_GROUPS + g
    _flash_stream(cnt_ref[prob], (qt_ref, kt_ref, lt_ref), prob * rows_per_problem, produce, consume, finish,
                  (sa_ref, ma_ref), (sb2_ref, mb_ref), mask_at_produce=True)


def _nsa_selected(proj3, sbias, used, o_c, o_w, small3, expand):
    b, s, _ = proj3.shape
    tq = min(256, s)
    nq = s // tq
    w = NSA_HPG * NSA_DH
    u = used[:, :, 0, :].reshape(b, nq, NSA_GROUPS, NSA_DH)[:, :, ::-1, :nq * (tq // SLC_LEN)]
    flags = (u.reshape(b, nq, NSA_GROUPS, nq, tq // SLC_LEN).max(axis=-1) > 0.0).astype(jnp.int32)
    flags = flags.transpose(0, 2, 1, 3)
    qt = jnp.arange(nq, dtype=jnp.int32)
    need = jnp.where(qt[None, :] < qt[:, None], flags, (qt[None, :] == qt[:, None]).astype(jnp.int32))
    need = need.reshape(b, NSA_GROUPS, nq * nq)
    cnt = need.sum(axis=-1).astype(jnp.int32)
    order = jnp.argsort(1 - need, axis=-1, stable=True).astype(jnp.int32)
    order = jnp.pad(order, ((0, 0), (0, 0), (0, 2)))
    rows = nq * nq + 2
    sched = (order // nq, order % nq, (order // nq == order % nq).astype(jnp.int32))
    return pl.pallas_call(
        functools.partial(_nsa_sel_kernel, tq=tq, rows_per_problem=rows),
        grid=(b, NSA_GROUPS),
        in_specs=[
            _SMEM, _SMEM, _SMEM, _SMEM, _SMEM,
            pl.BlockSpec((1, s, 4 * LANES), lambda bi, g: (bi, 0, PB_DQ // 4)),
            pl.BlockSpec((1, s, LANES), lambda bi, g: (bi, 0, 0)),
            pl.BlockSpec((1, s, LANES), lambda bi, g: (bi, 0, PB_SEL_K)),
            pl.BlockSpec((1, s, LANES), lambda bi, g: (bi, 0, PB_SEL_V)),
            pl.BlockSpec((1, s, w), lambda bi, g: (bi, 0, g)),
            pl.BlockSpec((1, s, w), lambda bi, g: (bi, 0, g)),
            pl.BlockSpec((1, s, LANES), lambda bi, g: (bi, 0, 0)),
            pl.BlockSpec((1, LANES, 3 * w), lambda bi, g: (g, 0, 0)),
        ],
        out_specs=pl.BlockSpec((1, s, w), lambda bi, g: (bi, 0, g)),
        out_shape=jax.ShapeDtypeStruct((b, s, NSA_HEADS * NSA_DH), CDT),
        scratch_shapes=_flash_scratch(NSA_HPG * tq, tq, mask_scratch=True),
        compiler_params=_cparams(("parallel", "parallel")),
        name="nsa_selected",
    )(cnt.reshape(-1), *[t.reshape(-1) for t in sched], jnp.asarray(_alibi_slopes(NSA_HEADS)),
      proj3, sbias, proj3, proj3, o_c, o_w, small3, expand)


def _merge_kernel(ya_ref, yb_ref, yc_ref, yd_ref, ga_ref, gb_ref, gc_ref, gd_ref, wb_ref, wo_ref, x_ref, o_ref):
    merged = None
    for n, (y_ref, g_ref) in enumerate(((ya_ref, ga_ref), (yb_ref, gb_ref), (yc_ref, gc_ref), (yd_ref, gd_ref))):
        t = _sigmoid(g_ref[...].astype(F32)) * _dot(y_ref[...], wb_ref[n])
        merged = t if merged is None else merged + t
    o_ref[...] = x_ref[...] + _dot(merged.astype(CDT), wo_ref[...])


def _merge(ys, proj2, wb, wo, x2):
    t, d = x2.shape
    tm = min(512, t)
    gate_blk = PB_GATE * LANES // d
    yspec = pl.BlockSpec((tm, BRANCH_WIDTH), lambda i: (i, 0))
    gspecs = [pl.BlockSpec((tm, d), functools.partial(lambda i, n: (i, gate_blk + n), n=n)) for n in range(N_BRANCH)]
    return pl.pallas_call(
        _merge_kernel,
        grid=(t // tm,),
        in_specs=[yspec] * N_BRANCH + gspecs + [
            pl.BlockSpec((N_BRANCH, BRANCH_WIDTH, d), lambda i: (0, 0, 0)),
            pl.BlockSpec((d, d), lambda i: (0, 0)),
            pl.BlockSpec((tm, d), lambda i: (i, 0)),
        ],
        out_specs=pl.BlockSpec((tm, d), lambda i: (i, 0)),
        out_shape=jax.ShapeDtypeStruct((t, d), F32),
        compiler_params=_cparams(("parallel",)),
        name="merge",
    )(*ys, proj2, proj2, proj2, proj2, wb, wo, x2)


HALO = 16


def _ffn_kernel(x_ref, xh_ref, g_ref, wu_ref, cw_ref, cb_ref, wd_ref, gf_ref, o_ref, he_ref, u_ref, act_ref,
                *, tm, fc, final):
    i = pl.program_id(1)
    x = x_ref[0]
    g = g_ref[...]
    xh = xh_ref[0] * (i > 0).astype(F32)
    he_ref[0:HALO] = _rms(xh, g).astype(CDT)
    he_ref[HALO:HALO + tm] = _rms(x, g).astype(CDT)
    he = he_ref[...]
    for c in range(D_FF // fc):
        outs = []
        for half in range(2):
            ub = u_ref.at[c % 2, half]
            lo = half * D_FF + c * fc
            ub[...] = _dot(he, wu_ref[:, lo:lo + fc])
            conv = cb_ref[:, lo:lo + fc]
            for kk in range(CONV_WIDTH):
                off = HALO - (CONV_WIDTH - 1) + kk
                conv = conv + cw_ref[kk:kk + 1, lo:lo + fc] * ub[off:off + tm, :]
            outs.append(conv)
        a, gg = outs
        act_ref[:, c * fc:(c + 1) * fc] = (a * _sigmoid(a) * gg).astype(CDT)
    y = x + _dot(act_ref[...], wd_ref[...])
    if final:
        y = _rms(y, gf_ref[...])
    o_ref[0] = y


def _ffn(x3, g, wu, cw, cb, wd, gf, final):
    b, s, d = x3.shape
    tm = min(512, s)
    fc = 256
    const = lambda shape: pl.BlockSpec(shape, lambda bi, i: (0,) * len(shape), pipeline_mode=pl.Buffered(1))
    return pl.pallas_call(
        functools.partial(_ffn_kernel, tm=tm, fc=fc, final=final),
        grid=(b, s // tm),
        in_specs=[
            pl.BlockSpec((1, tm, d), lambda bi, i: (bi, i, 0)),
            pl.BlockSpec((1, HALO, d), lambda bi, i: (bi, jnp.maximum(i * (tm // HALO) - 1, 0), 0)),
            const((1, d)), const((d, 2 * D_FF)), const((CONV_WIDTH, 2 * D_FF)), const((1, 2 * D_FF)),
            const((D_FF, d)), const((1, d)),
        ],
        out_specs=pl.BlockSpec((1, tm, d), lambda bi, i: (bi, i, 0)),
        out_shape=jax.ShapeDtypeStruct((b, s, d), F32),
        scratch_shapes=[pltpu.VMEM((tm + HALO, d), CDT), pltpu.VMEM((2, 2, tm + HALO, fc), F32),
                        pltpu.VMEM((tm, D_FF), CDT)],
        compiler_params=_cparams(("parallel", "arbitrary")),
        name="conv_glu_mlp",
    )(x3, x3, g.reshape(1, d), wu, cw, cb.reshape(1, -1), wd, gf.reshape(1, d))


def _prep_w_in(w):
    widths = (512, 512, 512, MLA_Q_LORA, MLA_KV_LORA, MLA_ROPE, 512, 512, 512, FOX_HEADS,
              512, 768, 3 * NSA_HEADS, N_BRANCH * D_MODEL)
    offs = np.cumsum((0,) + widths)
    (a_q, a_k, a_v, b_cq, b_ckv, b_kr, c_q, c_k, c_v, c_f, d_q, d_kv, d_g, gate) = [
        w[:, offs[i]:offs[i + 1]] for i in range(len(widths))]
    d = w.shape[0]
    d_q = d_q.reshape(d, NSA_GROUPS, NSA_HPG, NSA_DH).transpose(0, 2, 1, 3).reshape(d, 512)
    half = MLA_ROPE // 2
    kr_swap = jnp.concatenate([-b_kr[:, half:], b_kr[:, :half]], axis=1)
    z64 = jnp.zeros((d, LANES - MLA_ROPE), w.dtype)
    big = jnp.concatenate([
        a_q * (LOG2E * DIFF_DH ** -0.5), a_k, a_v,
        c_q * (LOG2E * FOX_DH ** -0.5), c_k, c_v,
        d_q * (LOG2E * NSA_DH ** -0.5), d_kv,
        b_cq, b_ckv, b_kr, z64, kr_swap, z64,
        gate], axis=1)
    small = jnp.concatenate([c_f, d_g, jnp.zeros((d, LANES - FOX_HEADS - 3 * NSA_HEADS), w.dtype)], axis=1)
    return big.astype(CDT), small.astype(CDT)


def _prep_mla(w_uq, w_ukv):
    r = w_uq.shape[0]
    hw = 2 * LANES
    half = MLA_ROPE // 2
    scale = LOG2E * (MLA_NOPE + MLA_ROPE) ** -0.5
    wq = (w_uq * scale).reshape(r, MLA_HEADS, MLA_NOPE + MLA_ROPE)
    nope, t1, t2 = wq[..., :MLA_NOPE], wq[..., MLA_NOPE:MLA_NOPE + half], wq[..., MLA_NOPE + half:]
    zpad = jnp.zeros((r, MLA_HEADS, hw - MLA_NOPE - MLA_ROPE), w_uq.dtype)
    wqm = jnp.concatenate([nope, t1, t2, zpad], axis=-1).reshape(r, MLA_HEADS * hw)
    wqs = jnp.concatenate([jnp.zeros_like(nope), -t2, t1, zpad], axis=-1).reshape(r, MLA_HEADS * hw)
    wkv = w_ukv.reshape(w_ukv.shape[0], MLA_HEADS, MLA_NOPE + MLA_VDIM)
    wk = wkv[..., :MLA_NOPE].reshape(-1, MLA_HEADS * MLA_NOPE)
    wv = wkv[..., MLA_NOPE:].reshape(-1, MLA_HEADS * MLA_VDIM)
    return wqm.astype(CDT), wqs.astype(CDT), wk.astype(CDT), wv.astype(CDT)


def _rope_tables(s):
    half = MLA_ROPE // 2
    inv_freq = ROPE_THETA ** (-jnp.arange(0, MLA_ROPE, 2, dtype=F32) / MLA_ROPE)
    ang = jnp.arange(s, dtype=F32)[:, None] * inv_freq[None, :]
    cos, sin = jnp.cos(ang), jnp.sin(ang)
    z = jnp.zeros((s, LANES - MLA_ROPE), F32)
    cosk = jnp.concatenate([cos, cos, z], axis=1)
    sink = jnp.concatenate([sin, sin, z], axis=1)
    cosq = jnp.concatenate([jnp.ones((s, MLA_NOPE), F32), cosk], axis=1)
    sinq = jnp.concatenate([jnp.zeros((s, MLA_NOPE), F32), sink], axis=1)
    return cosq, sinq, cosk, sink


def _prep_compress(pe, w1, w2):
    eye2 = jnp.eye(2, dtype=F32)
    w1r = w1.reshape(2, CMP_LEN, NSA_DH, CMP_HIDDEN).astype(CDT)
    same = np.eye(2, dtype=bool)
    diag_kg = jnp.asarray(same[:, None, :, None] & same[None, :, None, :])

    def expand(wpart):
        src = wpart.transpose(1, 0, 2, 3)[:, :, None, :, None, None, :]
        t = jnp.where(diag_kg[None, :, :, None, :, :, None], src, jnp.zeros((), CDT))
        return t.reshape(CMP_STRIDE * 4 * NSA_DH, 4 * CMP_HIDDEN)

    w1a, w1b = expand(w1r[:, :CMP_STRIDE]), expand(w1r[:, CMP_STRIDE:])

    def pe_row(p):
        t = jnp.broadcast_to(p.transpose(1, 0, 2)[:, :, None, :], (CMP_STRIDE, 2, NSA_GROUPS, NSA_DH))
        return jnp.pad(t.reshape(1, -1), ((0, 7), (0, 0)))

    pea, peb = pe_row(pe[:, :CMP_STRIDE]), pe_row(pe[:, CMP_STRIDE:])
    w2b = jnp.einsum('khd,kK,gG,u->kghKGud', w2, eye2, eye2, jnp.ones((2,), F32))
    w2b = w2b.reshape(4 * CMP_HIDDEN, 4 * 2 * NSA_DH)
    return w1a.astype(CDT), w1b.astype(CDT), pea.astype(CDT), peb.astype(CDT), w2b.astype(CDT)


def _gate_expand():
    e = np.zeros((NSA_GROUPS, LANES, 3, NSA_HPG, NSA_DH), np.float32)
    for g in range(NSA_GROUPS):
        for j in range(NSA_HPG):
            for br in range(3):
                e[g, SMALL_G + (g * NSA_HPG + j) * 3 + br, br, j, :] = 1.0
    return jnp.asarray(e.reshape(NSA_GROUPS, LANES, 3 * NSA_HPG * NSA_DH)).astype(CDT)


def _token_mixers(x3, l, norm_mix, w_in, diff_lambda, diff_subln, mla_norm_q, mla_w_uq, mla_norm_kv, mla_w_ukv,
                  fox_b_f, nsa_cmp_pe, nsa_cmp_w1, nsa_cmp_w2, w_branch, w_out, rope_tabs):
    b, s, d = x3.shape
    t = b * s
    x2 = x3.reshape(t, d)
    w_big, w_small = _prep_w_in(w_in)
    proj, small = _in_proj(x2, norm_mix, w_big, w_small)
    proj3 = proj.reshape(b, s, N_PROJ)
    small3 = small.reshape(b, s, LANES)

    lam_init = 0.8 - 0.6 * math.exp(-0.3 * l)
    y_a = _diff_attention(proj3, diff_lambda, diff_subln, lam_init)

    wqm, wqs, wk, wv = _prep_mla(mla_w_uq, mla_w_ukv)
    qc, kc, vv = _mla_prep(proj3, mla_norm_q, mla_norm_kv, wqm, wqs, wk, wv, rope_tabs)
    y_b = _mla_attention(qc, kc, vv)

    cf_rows = small3[:, :, SMALL_F:SMALL_F + FOX_HEADS].transpose(0, 2, 1).reshape(b * FOX_HEADS, s)
    bias_rows = jnp.tile(fox_b_f.astype(F32), b).reshape(b * FOX_HEADS, 1)
    c4 = _fox_cumsum(cf_rows, bias_rows)
    y_c = _fox_attention(proj3, c4)

    w1a, w1b, pea, peb, w2b = _prep_compress(nsa_cmp_pe, nsa_cmp_w1, nsa_cmp_w2)
    xc = proj3[:, :, PB_CMP_K * LANES:(PB_CMP_V + 1) * LANES].reshape(b, s // CMP_STRIDE, CMP_STRIDE * 2 * LANES)
    kvc = _nsa_compress(xc, w1a, w1b, pea, peb, w2b)
    n_topk = min(SLC_TOPK, s // SLC_LEN)
    o_c, sbias, used = _nsa_cmp_select(proj3, kvc, n_topk)
    o_w = _nsa_window(proj3)
    y_d = _nsa_selected(proj3, sbias, used, o_c, o_w, small3, _gate_expand())

    ys = [y.reshape(t, BRANCH_WIDTH) for y in (y_a, y_b, y_c, y_d)]
    return _merge(ys, proj, w_branch.astype(CDT), w_out.astype(CDT), x2).reshape(b, s, d)


def kernel(x, norm_mix, w_in, diff_lambda, diff_subln, mla_norm_q, mla_w_uq, mla_norm_kv, mla_w_ukv, fox_b_f,
           nsa_cmp_pe, nsa_cmp_w1, nsa_cmp_w2, w_branch, w_out, norm_ffn, w_up, conv_w, conv_b, w_down, norm_final):
    depth = w_in.shape[0]
    s = x.shape[1]
    rope_tabs = _rope_tables(s)
    for l in range(depth):
        x = _token_mixers(x, l, norm_mix[l], w_in[l], diff_lambda[l], diff_subln[l], mla_norm_q[l], mla_w_uq[l],
                          mla_norm_kv[l], mla_w_ukv[l], fox_b_f[l], nsa_cmp_pe[l], nsa_cmp_w1[l], nsa_cmp_w2[l],
                          w_branch[l], w_out[l], rope_tabs)
        x = _ffn(x, norm_ffn[l], w_up[l].astype(CDT), conv_w[l], conv_b[l], w_down[l].astype(CDT), norm_final,
                 final=(l == depth - 1))
    return x
```

```python
import functools
import math

import numpy as np
import jax
import jax.numpy as jnp
from jax import lax
from jax.experimental import pallas as pl
from jax.experimental.pallas import tpu as pltpu

F32 = jnp.float32
CDT = jnp.bfloat16

NEG = -1e30
NEG_INF = -1e30
BIG = 1e9
NORM_EPS = 1e-6
LOG2E = 1.4426950408889634
LANES = 128

D_MODEL = 1024
DIFF_HEADS, DIFF_DH = 4, 64
MLA_HEADS, MLA_NOPE, MLA_ROPE, MLA_VDIM = 4, 128, 64, 128
MLA_Q_LORA, MLA_KV_LORA = 256, 256
ROPE_THETA = 10000.0
FOX_HEADS, FOX_DH = 4, 128
NSA_HEADS, NSA_GROUPS, NSA_DH = 8, 2, 64
NSA_HPG = NSA_HEADS // NSA_GROUPS
CMP_STRIDE = 16
CMP_LEN = 2 * CMP_STRIDE
CMP_HIDDEN = 128
SLC_LEN = 64
SLC_SHIFT = 6
HALF_SHIFT = 6
SLC_TOPK = 8
WINDOW = 256
N_BRANCH = 4
BRANCH_WIDTH = 512
D_FF = 2816
CONV_WIDTH = 3

PB_AQ, PB_AK, PB_AV = 0, 4, 8
PB_CQ, PB_CK, PB_CV = 12, 16, 20
PB_DQ = 24
PB_CMP_K, PB_CMP_V, PB_SEL_K, PB_SEL_V, PB_WIN_K, PB_WIN_V = 28, 29, 30, 31, 32, 33
PB_BCQ, PB_BCKV, PB_BKR, PB_BKRS = 34, 36, 38, 39
PB_GATE = 40
N_PROJ = 72 * LANES
SMALL_F, SMALL_G = 0, 4

VMEM_LIMIT = 56 * 1024 * 1024
FOX_HP = 2
MLA_HP = 2


def _cparams(sem):
    return pltpu.CompilerParams(dimension_semantics=sem, vmem_limit_bytes=VMEM_LIMIT)


def _rms(xf, g):
    return xf * lax.rsqrt(jnp.mean(xf * xf, axis=-1, keepdims=True) + NORM_EPS) * g


def _sigmoid(x):
    return 0.5 * jnp.tanh(0.5 * x) + 0.5


def _dot(a, b):
    return jnp.dot(a, b, preferred_element_type=F32)


def _dot_nt(a, b):
    return lax.dot_general(a, b, (((1,), (1,)), ((), ())), preferred_element_type=F32)


def _split_dot(a, b):
    hi = a.astype(CDT)
    lo = (a - hi.astype(F32)).astype(CDT)
    return _dot(hi, b) + _dot(lo, b)


def _alibi_slopes(n):
    return (LOG2E * np.exp2(-8.0 * np.arange(1, n + 1) / n)).astype(np.float32)


def _inproj_kernel(x_ref, g_ref, w_ref, ws_ref, o_ref, os_ref, h_ref):
    @pl.when(pl.program_id(1) == 0)
    def _():
        h = _rms(x_ref[...], g_ref[...]).astype(CDT)
        h_ref[...] = h
        os_ref[...] = _dot(h, ws_ref[...])

    o_ref[...] = _dot(h_ref[...], w_ref[...]).astype(o_ref.dtype)


def _in_proj(x2, g, w, ws):
    t, d = x2.shape
    n = w.shape[1]
    tm = min(1024, t)
    tn = 2304 if n % 2304 == 0 else 1024
    return pl.pallas_call(
        _inproj_kernel,
        grid=(t // tm, n // tn),
        in_specs=[
            pl.BlockSpec((tm, d), lambda i, j: (i, 0)),
            pl.BlockSpec((1, d), lambda i, j: (0, 0)),
            pl.BlockSpec((d, tn), lambda i, j: (0, j)),
            pl.BlockSpec((d, LANES), lambda i, j: (0, 0)),
        ],
        out_specs=[
            pl.BlockSpec((tm, tn), lambda i, j: (i, j)),
            pl.BlockSpec((tm, LANES), lambda i, j: (i, 0)),
        ],
        out_shape=[jax.ShapeDtypeStruct((t, n), CDT), jax.ShapeDtypeStruct((t, LANES), F32)],
        scratch_shapes=[pltpu.VMEM((tm, d), CDT)],
        compiler_params=_cparams(("parallel", "arbitrary")),
        name="in_proj",
    )(x2, g.reshape(1, d), w, ws)


def _fox_cumsum_kernel(cf_ref, bf_ref, o_ref):
    rows, s = cf_ref.shape
    lane = lax.broadcasted_iota(jnp.int32, (rows, LANES), 1)
    carry = jnp.zeros((rows, 1), F32)
    for c in range(s // LANES):
        z = cf_ref[:, c * LANES:(c + 1) * LANES] + bf_ref[...]
        xs = jnp.minimum(z, 0.0) - jnp.log1p(jnp.exp(-jnp.abs(z)))
        d = 1
        while d < LANES:
            xs = xs + jnp.where(lane >= d, pltpu.roll(xs, d, axis=1), 0.0)
            d *= 2
        xs = xs + carry
        o_ref[:, c * LANES:(c + 1) * LANES] = xs
        carry = xs[:, LANES - 1:LANES]


def _fox_cumsum(cf_rows, bias_rows):
    return pl.pallas_call(
        _fox_cumsum_kernel,
        out_shape=jax.ShapeDtypeStruct(cf_rows.shape, F32),
        name="fox_cumsum",
    )(cf_rows, bias_rows)


def _flash_scratch(rows, tk, mask_scratch=False):
    return [pltpu.VMEM((rows, LANES), F32), pltpu.VMEM((rows, 2 * LANES), F32),
            pltpu.VMEM((rows, tk), F32), pltpu.VMEM((rows, tk), F32),
            pltpu.VMEM((rows, LANES), F32), pltpu.VMEM((rows, LANES), F32)
            ] + ([pltpu.VMEM((rows, tk), F32)] if mask_scratch else [])


def _flash_reset(m_ref, acc_ref):
    m_ref[...] = jnp.full(m_ref.shape, NEG, F32)
    acc_ref[...] = jnp.zeros(acc_ref.shape, F32)


def _flash_begin(m_ref, acc_ref, cm_ref, tq):
    _flash_reset(m_ref, acc_ref)
    cm_ref[...] = _causal_bias(cm_ref.shape[0], cm_ref.shape[1], tq)


def _row_max(s):
    return jnp.broadcast_to(jnp.max(s, axis=-1, keepdims=True), (s.shape[0], LANES))


def _causal_bias(rows, tk, tq):
    r = lax.broadcasted_iota(jnp.int32, (rows, tk), 0) & (tq - 1)
    c = lax.broadcasted_iota(jnp.int32, (rows, tk), 1)
    return jnp.where(c <= r, 0.0, NEG)


def _put_logits(buf, s, diag, cm_ref, rows=slice(None)):
    if diag is True:
        s = s + cm_ref[rows]
    elif diag is not False:
        s = s + diag.astype(F32) * cm_ref[rows]
    buf[0][rows] = s
    buf[1][rows] = _row_max(s)


def _with_ones(v):
    return jnp.concatenate([v, jnp.ones((v.shape[0], LANES), v.dtype)], axis=1)


def _flash_consume(buf, v, m_ref, acc_ref, mask_tq=None, rows=slice(None)):
    s = buf[0][rows]
    m_cur = buf[1][rows]
    if mask_tq is not None:
        s = s + _causal_bias(s.shape[0], s.shape[1], mask_tq)
        m_cur = _row_max(s)
    m_old = m_ref[rows]
    m_new = jnp.maximum(m_old, m_cur)
    alpha = jnp.exp2(m_old - m_new)
    p = jnp.exp2(s - jnp.tile(m_new, (1, s.shape[1] // LANES))).astype(CDT)
    acc_ref[rows] = jnp.tile(alpha, (1, 2)) * acc_ref[rows] + _dot(p, _with_ones(v))
    m_ref[rows] = m_new


def _flash_result(acc):
    return acc[:, :LANES] / acc[:, LANES:]


def _causal_schedule(nq):
    ent = [(qi, ki, int(ki == qi)) for qi in range(nq) for ki in range(qi + 1)]
    n = len(ent)
    a = np.asarray(ent + [ent[-1]] * 2, np.int32)
    return n, tuple(jnp.asarray(a[:, i]) for i in range(3))


def _flash_stream(n, sched, base, produce, consume, finish, buf_a, buf_b, mask_at_produce):
    qt, kt, lt = sched

    def step(cur, nxt, t, diag, next_diag):
        if nxt is not None:
            produce(nxt, qt[base + t + 1], kt[base + t + 1], next_diag if mask_at_produce else False)
        consume(cur, kt[base + t], diag and not mask_at_produce)
        if diag:
            finish(qt[base + t])

    produce(buf_a, qt[base], kt[base], mask_at_produce)

    def pair(j, c):
        t = 2 * j
        l0, l1 = lt[base + t], lt[base + t + 1]
        for d0 in (False, True):
            for d1 in (False, True):
                @pl.when(((l0 != 0) == d0) & ((l1 != 0) == d1))
                def _():
                    step(buf_a, buf_b, t, d0, d1)
                    step(buf_b, buf_a, t + 1, d1, lt[base + t + 2])
        return c

    lax.fori_loop(0, n // 2, pair, 0)

    def tail():
        step(buf_a, None, n - 1, True, None)

    if isinstance(n, int):
        if n % 2 == 1:
            tail()
    else:
        pl.when(n % 2 == 1)(tail)


def _tile(ref, i, t):
    return ref[0, pl.ds(pl.multiple_of(i * t, t), t), :]


def _diff_attn_kernel(qt_ref, kt_ref, lt_ref, slopes_ref, lam_ref, g_ref, q_ref, k_ref, v_ref, o_ref,
                      m_ref, acc_ref, sa_ref, sb_ref, ma_ref, mb_ref, *, tq, n, lam_init):
    slope = slopes_ref[pl.program_id(1)]
    _flash_reset(m_ref, acc_ref)
    col = lax.broadcasted_iota(jnp.int32, (1, tq), 1).astype(F32)
    lane = lax.broadcasted_iota(jnp.int32, (tq, LANES), 1)
    lf = lam_ref[...]
    lam = (jnp.exp(jnp.sum(lf[0:1] * lf[1:2], axis=-1, keepdims=True))
           - jnp.exp(jnp.sum(lf[2:3] * lf[3:4], axis=-1, keepdims=True)) + lam_init)

    def produce(buf, qi, ki, diag):
        q = _tile(q_ref, qi, tq)
        zero = jnp.zeros_like(q)
        qq = jnp.concatenate([jnp.where(lane < DIFF_DH, q, zero), jnp.where(lane >= DIFF_DH, q, zero)], axis=0)
        s = _dot_nt(qq, _tile(k_ref, ki, tq))
        _put_logits(buf, s + slope * (col + ((ki - qi) * tq).astype(F32)), diag, None)

    def consume(buf, ki, diag):
        _flash_consume(buf, _tile(v_ref, ki, tq), m_ref, acc_ref, tq if diag else None)

    def finish(qi):
        o = _flash_result(acc_ref[...])
        d = o[0:tq] - lam * o[tq:2 * tq]
        o_ref[0, pl.ds(pl.multiple_of(qi * tq, tq), tq), :] = (
            _rms(d, g_ref[...]) * (1.0 - lam_init)).astype(o_ref.dtype)
        _flash_reset(m_ref, acc_ref)

    _flash_stream(n, (qt_ref, kt_ref, lt_ref), 0, produce, consume, finish, (sa_ref, ma_ref), (sb_ref, mb_ref),
                  mask_at_produce=False)


_SMEM = pl.BlockSpec(memory_space=pltpu.SMEM)


def _diff_attention(proj3, diff_lambda, subln, lam_init):
    b, s, _ = proj3.shape
    tq = min(512, s)
    dv = 2 * DIFF_DH
    n, sched = _causal_schedule(s // tq)
    kern = functools.partial(_diff_attn_kernel, tq=tq, n=n, lam_init=lam_init)
    return pl.pallas_call(
        kern,
        grid=(b, DIFF_HEADS),
        in_specs=[
            _SMEM, _SMEM, _SMEM, _SMEM,
            pl.BlockSpec((4, DIFF_DH), lambda bi, h: (0, 0)),
            pl.BlockSpec((1, dv), lambda bi, h: (0, 0)),
            pl.BlockSpec((1, s, LANES), lambda bi, h: (bi, 0, PB_AQ + h)),
            pl.BlockSpec((1, s, LANES), lambda bi, h: (bi, 0, PB_AK + h)),
            pl.BlockSpec((1, s, LANES), lambda bi, h: (bi, 0, PB_AV + h)),
        ],
        out_specs=pl.BlockSpec((1, s, dv), lambda bi, h: (bi, 0, h)),
        out_shape=jax.ShapeDtypeStruct((b, s, DIFF_HEADS * dv), CDT),
        scratch_shapes=_flash_scratch(2 * tq, tq),
        compiler_params=_cparams(("parallel", "parallel")),
        name="diff_attention",
    )(*sched, jnp.asarray(_alibi_slopes(DIFF_HEADS)), diff_lambda, subln.reshape(1, dv), proj3, proj3, proj3)


def _mla_prep_kernel(cq_ref, ckv_ref, kr_ref, krs_ref, gq_ref, gkv_ref, wqm_ref, wqs_ref, wk_ref, wv_ref,
                     cosq_ref, sinq_ref, cosk_ref, sink_ref, q_ref, k_ref, v_ref):
    hq = _rms(cq_ref[0].astype(F32), gq_ref[...]).astype(CDT)
    qm = _dot(hq, wqm_ref[...])
    qs = _dot(hq, wqs_ref[...])
    cosq, sinq = cosq_ref[...], sinq_ref[...]
    hw = 2 * LANES
    for h in range(MLA_HEADS):
        sl = slice(h * hw, (h + 1) * hw)
        q_ref[0, :, sl] = (qm[:, sl] * cosq + qs[:, sl] * sinq).astype(q_ref.dtype)
    hkv = _rms(ckv_ref[0].astype(F32), gkv_ref[...]).astype(CDT)
    kn = _dot(hkv, wk_ref[...])
    v_ref[0] = _dot(hkv, wv_ref[...]).astype(v_ref.dtype)
    kpe = (kr_ref[0].astype(F32) * cosk_ref[...] + krs_ref[0].astype(F32) * sink_ref[...]).astype(k_ref.dtype)
    for h in range(MLA_HEADS):
        k_ref[0, :, h * hw:h * hw + LANES] = kn[:, h * LANES:(h + 1) * LANES].astype(k_ref.dtype)
        k_ref[0, :, h * hw + LANES:(h + 1) * hw] = kpe


def _mla_prep(proj3, gq, gkv, wqm, wqs, wk, wv, tabs):
    b, s, _ = proj3.shape
    tm = min(512, s)
    hw = 2 * LANES
    cosq, sinq, cosk, sink = tabs
    const = lambda shape: pl.BlockSpec(shape, lambda bi, i: (0,) * len(shape))
    return pl.pallas_call(
        _mla_prep_kernel,
        grid=(b, s // tm),
        in_specs=[
            pl.BlockSpec((1, tm, MLA_Q_LORA), lambda bi, i: (bi, i, PB_BCQ // 2)),
            pl.BlockSpec((1, tm, MLA_KV_LORA), lambda bi, i: (bi, i, PB_BCKV // 2)),
            pl.BlockSpec((1, tm, LANES), lambda bi, i: (bi, i, PB_BKR)),
            pl.BlockSpec((1, tm, LANES), lambda bi, i: (bi, i, PB_BKRS)),
            const((1, MLA_Q_LORA)), const((1, MLA_KV_LORA)),
            const((MLA_Q_LORA, MLA_HEADS * hw)), const((MLA_Q_LORA, MLA_HEADS * hw)),
            const((MLA_KV_LORA, MLA_HEADS * MLA_NOPE)), const((MLA_KV_LORA, MLA_HEADS * MLA_VDIM)),
            pl.BlockSpec((tm, hw), lambda bi, i: (i, 0)), pl.BlockSpec((tm, hw), lambda bi, i: (i, 0)),
            pl.BlockSpec((tm, LANES), lambda bi, i: (i, 0)), pl.BlockSpec((tm, LANES), lambda bi, i: (i, 0)),
        ],
        out_specs=[
            pl.BlockSpec((1, tm, MLA_HEADS * hw), lambda bi, i: (bi, i, 0)),
            pl.BlockSpec((1, tm, MLA_HEADS * hw), lambda bi, i: (bi, i, 0)),
            pl.BlockSpec((1, tm, MLA_HEADS * MLA_VDIM), lambda bi, i: (bi, i, 0)),
        ],
        out_shape=[
            jax.ShapeDtypeStruct((b, s, MLA_HEADS * hw), CDT),
            jax.ShapeDtypeStruct((b, s, MLA_HEADS * hw), CDT),
            jax.ShapeDtypeStruct((b, s, MLA_HEADS * MLA_VDIM), CDT),
        ],
        compiler_params=_cparams(("parallel", "parallel")),
        name="mla_prep",
    )(proj3, proj3, proj3, proj3, gq.reshape(1, -1), gkv.reshape(1, -1), wqm, wqs, wk, wv,
      cosq, sinq, cosk, sink)


def _plain_attn_kernel(qt_ref, kt_ref, lt_ref, q_ref, k_ref, v_ref, o_ref,
                       m_ref, acc_ref, sa_ref, sb_ref, ma_ref, mb_ref, *, tq, n, hp, dk, dv):
    _flash_reset(m_ref, acc_ref)
    heads = [(slice(h * tq, (h + 1) * tq), slice(h * dk, (h + 1) * dk), slice(h * dv, (h + 1) * dv))
             for h in range(hp)]

    def produce(buf, qi, ki, diag):
        q, k = _tile(q_ref, qi, tq), _tile(k_ref, ki, tq)
        for rows, kcols, _ in heads:
            _put_logits(buf, _dot_nt(q[:, kcols], k[:, kcols]), diag, None, rows)

    def consume(buf, ki, diag):
        v = _tile(v_ref, ki, tq)
        for rows, _, vcols in heads:
            _flash_consume(buf, v[:, vcols], m_ref, acc_ref, tq if diag else None, rows)

    def finish(qi):
        for rows, _, vcols in heads:
            o_ref[0, pl.ds(pl.multiple_of(qi * tq, tq), tq), vcols] = _flash_result(acc_ref[rows]).astype(o_ref.dtype)
        _flash_reset(m_ref, acc_ref)

    _flash_stream(n, (qt_ref, kt_ref, lt_ref), 0, produce, consume, finish, (sa_ref, ma_ref), (sb_ref, mb_ref),
                  mask_at_produce=False)


def _mla_attention(qc, kc, v):
    b, s, _ = qc.shape
    tq = min(512, s)
    hw = 2 * LANES
    hp = MLA_HP
    n, sched = _causal_schedule(s // tq)
    return pl.pallas_call(
        functools.partial(_plain_attn_kernel, tq=tq, n=n, hp=hp, dk=hw, dv=MLA_VDIM),
        grid=(b, MLA_HEADS // hp),
        in_specs=[
            _SMEM, _SMEM, _SMEM,
            pl.BlockSpec((1, s, hp * hw), lambda bi, h: (bi, 0, h)),
            pl.BlockSpec((1, s, hp * hw), lambda bi, h: (bi, 0, h)),
            pl.BlockSpec((1, s, hp * MLA_VDIM), lambda bi, h: (bi, 0, h)),
        ],
        out_specs=pl.BlockSpec((1, s, hp * MLA_VDIM), lambda bi, h: (bi, 0, h)),
        out_shape=jax.ShapeDtypeStruct((b, s, MLA_HEADS * MLA_VDIM), CDT),
        scratch_shapes=_flash_scratch(hp * tq, tq),
        compiler_params=_cparams(("parallel", "parallel")),
        name="mla_attention",
    )(*sched, qc, kc, v)


def _fox_attn_kernel(qt_ref, kt_ref, lt_ref, c_ref, q_ref, k_ref, v_ref, o_ref,
                     m_ref, acc_ref, sa_ref, sb_ref, ma_ref, mb_ref, *, tq, n, hp):
    _flash_reset(m_ref, acc_ref)
    heads = [(slice(h * tq, (h + 1) * tq), slice(h * FOX_DH, (h + 1) * FOX_DH)) for h in range(hp)]

    def produce(buf, qi, ki, diag):
        q, k = _tile(q_ref, qi, tq), _tile(k_ref, ki, tq)
        for h, (rows, cols) in enumerate(heads):
            cbase = c_ref[0, h, pl.ds(qi, 1), :][:, 0:1]
            s = _dot_nt(q[:, cols], k[:, cols]) + LOG2E * (cbase - c_ref[0, h, pl.ds(ki, 1), :])
            _put_logits(buf, s, diag, None, rows)

    def consume(buf, ki, diag):
        v = _tile(v_ref, ki, tq)
        for rows, cols in heads:
            _flash_consume(buf, v[:, cols], m_ref, acc_ref, tq if diag else None, rows)

    def finish(qi):
        for rows, cols in heads:
            o_ref[0, pl.ds(pl.multiple_of(qi * tq, tq), tq), cols] = _flash_result(acc_ref[rows]).astype(o_ref.dtype)
        _flash_reset(m_ref, acc_ref)

    _flash_stream(n, (qt_ref, kt_ref, lt_ref), 0, produce, consume, finish, (sa_ref, ma_ref), (sb_ref, mb_ref),
                  mask_at_produce=False)


def _fox_attention(proj3, c4):
    b, s, _ = proj3.shape
    tq = min(512, s)
    nk = s // tq
    hp = FOX_HP
    w = hp * FOX_DH
    n, sched = _causal_schedule(nk)
    return pl.pallas_call(
        functools.partial(_fox_attn_kernel, tq=tq, n=n, hp=hp),
        grid=(b, FOX_HEADS // hp),
        in_specs=[
            _SMEM, _SMEM, _SMEM,
            pl.BlockSpec((1, hp, nk, tq), lambda bi, h: (bi, h, 0, 0)),
            pl.BlockSpec((1, s, w), lambda bi, h: (bi, 0, PB_CQ // hp + h)),
            pl.BlockSpec((1, s, w), lambda bi, h: (bi, 0, PB_CK // hp + h)),
            pl.BlockSpec((1, s, w), lambda bi, h: (bi, 0, PB_CV // hp + h)),
        ],
        out_specs=pl.BlockSpec((1, s, w), lambda bi, h: (bi, 0, h)),
        out_shape=jax.ShapeDtypeStruct((b, s, FOX_HEADS * FOX_DH), CDT),
        scratch_shapes=_flash_scratch(hp * tq, tq),
        compiler_params=_cparams(("parallel", "parallel")),
        name="fox_attention",
    )(*sched, c4.reshape(b, FOX_HEADS, nk, tq), proj3, proj3, proj3)


def _nsa_compress_kernel(x_ref, w1a_ref, w1b_ref, pea_ref, peb_ref, w2_ref, o_ref):
    x = x_ref[0]
    n = x.shape[0]
    pa = _dot(x, w1a_ref[...])
    pb = _dot(x, w1b_ref[...])
    pe = _dot(pea_ref[...], w1a_ref[...]) + _dot(peb_ref[...], w1b_ref[...])
    hid = pa + pltpu.roll(pb, n - 1, axis=0) + pe[0:1]
    act = 0.5 * hid * (1.0 + jnp.tanh(math.sqrt(2.0 / math.pi) * (hid + 0.044715 * hid * hid * hid)))
    o_ref[0] = _dot(act.astype(CDT), w2_ref[...]).astype(o_ref.dtype)


def _nsa_compress(xc, w1a, w1b, pea, peb, w2):
    b, n, kdim = xc.shape
    hdim = w1a.shape[1]
    const = lambda shape: pl.BlockSpec(shape, lambda bi: (0,) * len(shape))
    return pl.pallas_call(
        _nsa_compress_kernel,
        grid=(b,),
        in_specs=[pl.BlockSpec((1, n, kdim), lambda bi: (bi, 0, 0)),
                  const((kdim, hdim)), const((kdim, hdim)), const((8, kdim)), const((8, kdim)),
                  const((hdim, w2.shape[1]))],
        out_specs=pl.BlockSpec((1, n, w2.shape[1]), lambda bi: (bi, 0, 0)),
        out_shape=jax.ShapeDtypeStruct((b, n, w2.shape[1]), CDT),
        compiler_params=_cparams(("parallel",)),
        name="nsa_compress",
    )(xc, w1a, w1b, pea, peb, w2)


def _nsa_cmp_kernel(slopes_ref, q_ref, kv_ref, oc_ref, sb_ref, used_ref, *, tq, n_topk):
    qi = pl.program_id(1)
    nblk = kv_ref.shape[1]
    q0 = qi * tq
    rowpos = q0 + lax.broadcasted_iota(jnp.int32, (tq, 1), 0)
    cmp_end = lax.broadcasted_iota(jnp.int32, (1, nblk), 1) * CMP_STRIDE + (CMP_LEN - 1)
    negmask = jnp.where(rowpos >= cmp_end, 0.0, NEG)
    end_rel = (cmp_end - q0).astype(F32)
    lane = lax.broadcasted_iota(jnp.int32, (tq, LANES), 1)
    low = lane < NSA_DH
    nn = lax.broadcasted_iota(jnp.int32, (NSA_DH, nblk), 1) * CMP_STRIDE
    jj = lax.broadcasted_iota(jnp.int32, (NSA_DH, nblk), 0) * SLC_LEN
    ovt = (jnp.maximum(jnp.minimum(nn + CMP_LEN, jj + SLC_LEN) - jnp.maximum(nn, jj), 0).astype(F32)
           * (1.0 / CMP_LEN)).astype(CDT)
    jt = lax.broadcasted_iota(jnp.int32, (NSA_DH, tq), 0).astype(F32)
    blk = ((q0 + lax.broadcasted_iota(jnp.int32, (1, tq), 1)) >> SLC_SHIFT).astype(F32)
    fixed = (jt == 0.0) | (jt == blk) | (jt == blk - 1.0)
    out_of_play = fixed | (jt > blk)
    row_ok = rowpos >= CMP_LEN - 1
    outs = []
    bias = []
    for g in range(NSA_GROUPS):
        kc = kv_ref[0, :, g * LANES:(g + 1) * LANES]
        vc = kv_ref[0, :, (NSA_GROUPS + g) * LANES:(NSA_GROUPS + g + 1) * LANES]
        psum = jnp.zeros((tq, nblk), F32)
        for j in range(NSA_HPG):
            qb = q_ref[0, :, j * LANES:(j + 1) * LANES]
            qm = jnp.where(low if g == 0 else jnp.logical_not(low), qb, jnp.zeros_like(qb))
            s = _dot_nt(qm, kc) + slopes_ref[g * NSA_HPG + j] * end_rel + negmask
            e = jnp.exp2(s - jnp.max(s, axis=-1, keepdims=True))
            den = jnp.sum(e, axis=-1, keepdims=True)
            p = e * jnp.where(row_ok, 1.0 / den, 0.0)
            psum = psum + p
            outs.append(_dot(p.astype(CDT), vc))
        hi = psum.astype(CDT)
        lo = (psum - hi.astype(F32)).astype(CDT)
        imp = _dot_nt(ovt, hi) + _dot_nt(ovt, lo)
        imp = jnp.where(out_of_play, -jnp.inf, imp)
        sbt = jnp.where(fixed, 0.0, NEG)
        for _ in range(n_topk - 3):
            mx = jnp.max(imp, axis=0, keepdims=True)
            idx = jnp.min(jnp.where(imp == mx, jt, float(LANES)), axis=0, keepdims=True)
            hit = jt == idx
            sbt = jnp.where(hit, 0.0, sbt)
            imp = jnp.where(hit, -jnp.inf, imp)
        bias.append(sbt)
    sb = jnp.concatenate([bias[1], bias[0]], axis=0).T
    sb_ref[0] = sb.astype(sb_ref.dtype)
    used = jnp.max(jnp.where(sb == 0.0, 1.0, 0.0), axis=0, keepdims=True)
    used_ref[0, 0] = jnp.broadcast_to(used, used_ref.shape[2:])
    for blk_i in range(NSA_HEADS // 2):
        oc_ref[0, :, blk_i * LANES:(blk_i + 1) * LANES] = jnp.where(
            low, outs[2 * blk_i], outs[2 * blk_i + 1]).astype(oc_ref.dtype)


def _nsa_cmp_select(proj3, kvc, n_topk):
    assert n_topk >= 3, "the three always-selected blocks must fit in the top-k budget"
    b, s, _ = proj3.shape
    tq = min(256, s)
    nblk = kvc.shape[1]
    return pl.pallas_call(
        functools.partial(_nsa_cmp_kernel, tq=tq, n_topk=n_topk),
        grid=(b, s // tq),
        in_specs=[
            pl.BlockSpec(memory_space=pltpu.SMEM),
            pl.BlockSpec((1, tq, 4 * LANES), lambda bi, qi: (bi, qi, PB_DQ // 4)),
            pl.BlockSpec((1, nblk, kvc.shape[2]), lambda bi, qi: (bi, 0, 0)),
        ],
        out_specs=[
            pl.BlockSpec((1, tq, NSA_HEADS * NSA_DH), lambda bi, qi: (bi, qi, 0)),
            pl.BlockSpec((1, tq, LANES), lambda bi, qi: (bi, qi, 0)),
            pl.BlockSpec((1, 1, 8, LANES), lambda bi, qi: (bi, qi, 0, 0)),
        ],
        out_shape=[jax.ShapeDtypeStruct((b, s, NSA_HEADS * NSA_DH), CDT),
                   jax.ShapeDtypeStruct((b, s, LANES), CDT),
                   jax.ShapeDtypeStruct((b, s // tq, 8, LANES), F32)],
        compiler_params=_cparams(("parallel", "parallel")),
        name="nsa_cmp_select",
    )(jnp.asarray(_alibi_slopes(NSA_HEADS)), proj3, kvc)


def _compact_heads(heads, mine, low):
    both = [jnp.where(mine, a, pltpu.roll(a, NSA_DH, axis=1)) for a in heads]
    out = [jnp.where(low, both[2 * jj], both[2 * jj + 1]) for jj in range(NSA_HPG // 2)]
    return jnp.concatenate(out, axis=1)


def _nsa_win_kernel(slopes_ref, q_ref, kp_ref, kc_ref, vp_ref, vc_ref, o_ref, *, tq):
    g = pl.program_id(1)
    qi = pl.program_id(2)
    lane = lax.broadcasted_iota(jnp.int32, (tq, LANES), 1)
    low = lane < NSA_DH
    mine = (lane >> HALF_SHIFT) == g
    r = lax.broadcasted_iota(jnp.int32, (tq, tq), 0)
    c = lax.broadcasted_iota(jnp.int32, (tq, tq), 1)
    own = c <= r
    ndist = jnp.where(own, c - r, c - r - tq).astype(F32)
    own_f = jnp.where(own, 1.0, 0.0).astype(CDT)
    prev_pen = jnp.where(qi > 0, 0.0, NEG)
    kc, kp = kc_ref[0], kp_ref[0]
    vc, vp = _with_ones(vc_ref[0]), _with_ones(vp_ref[0])
    heads = []
    for j in range(NSA_HPG):
        qb = q_ref[0, :, j * LANES:(j + 1) * LANES]
        qm = jnp.where(mine, qb, jnp.zeros_like(qb))
        s = jnp.where(own, _dot_nt(qm, kc), _dot_nt(qm, kp) + prev_pen) + slopes_ref[g * NSA_HPG + j] * ndist
        p = jnp.exp2(s - jnp.max(s, axis=-1, keepdims=True)).astype(CDT)
        p_own = p * own_f
        acc = _dot(p_own, vc) + _dot(p - p_own, vp)
        heads.append(_flash_result(acc))
    o_ref[0] = _compact_heads(heads, mine, low).astype(o_ref.dtype)


def _nsa_window(proj3):
    b, s, _ = proj3.shape
    tq = WINDOW
    return pl.pallas_call(
        functools.partial(_nsa_win_kernel, tq=tq),
        grid=(b, NSA_GROUPS, s // tq),
        in_specs=[
            pl.BlockSpec(memory_space=pltpu.SMEM),
            pl.BlockSpec((1, tq, 4 * LANES), lambda bi, g, qi: (bi, qi, PB_DQ // 4)),
            pl.BlockSpec((1, tq, LANES), lambda bi, g, qi: (bi, jnp.maximum(qi - 1, 0), PB_WIN_K)),
            pl.BlockSpec((1, tq, LANES), lambda bi, g, qi: (bi, qi, PB_WIN_K)),
            pl.BlockSpec((1, tq, LANES), lambda bi, g, qi: (bi, jnp.maximum(qi - 1, 0), PB_WIN_V)),
            pl.BlockSpec((1, tq, LANES), lambda bi, g, qi: (bi, qi, PB_WIN_V)),
        ],
        out_specs=pl.BlockSpec((1, tq, NSA_HPG * NSA_DH), lambda bi, g, qi: (bi, qi, g)),
        out_shape=jax.ShapeDtypeStruct((b, s, NSA_HEADS * NSA_DH), CDT),
        compiler_params=_cparams(("parallel", "parallel", "parallel")),
        name="nsa_window",
    )(jnp.asarray(_alibi_slopes(NSA_HEADS)), proj3, proj3, proj3, proj3, proj3)


def _nsa_sel_kernel(cnt_ref, qt_ref, kt_ref, lt_ref, slopes_ref, q_ref, sb_ref, k_ref, v_ref, oc_ref, ow_ref, gl_ref,
                    e_ref, o_ref, m_ref, acc_ref, sa_ref, sb2_ref, ma_ref, mb_ref, cm_ref, *, tq, rows_per_problem):
    g = pl.program_id(1)
    w = NSA_HPG * NSA_DH
    lane = lax.broadcasted_iota(jnp.int32, (tq, LANES), 1)
    low = lane < NSA_DH
    mine = (lane >> HALF_SHIFT) == g
    _flash_begin(m_ref, acc_ref, cm_ref, tq)
    col = lax.broadcasted_iota(jnp.int32, (1, tq), 1).astype(F32)
    jl = lane & (NSA_DH - 1)
    krow = lax.broadcasted_iota(jnp.int32, (tq, LANES), 0)

    def produce(buf, qi, ki, diag):
        q = _tile(q_ref, qi, tq)
        sb = _tile(sb_ref, qi, tq)
        qa = jnp.concatenate([jnp.where(mine, q[:, j * LANES:(j + 1) * LANES], sb) for j in range(NSA_HPG)], axis=0)
        k = _tile(k_ref, ki, tq)
        onehot = jnp.where(((ki * tq + krow) >> SLC_SHIFT) == jl, 1.0, 0.0).astype(k.dtype)
        s_all = _dot_nt(qa, jnp.where(mine, k, onehot))
        rel = ((ki - qi) * tq).astype(F32)
        for j in range(NSA_HPG):
            rows = slice(j * tq, (j + 1) * tq)
            _put_logits(buf, s_all[rows] + slopes_ref[g * NSA_HPG + j] * (col + rel), diag, cm_ref, rows)

    def consume(buf, ki, diag):
        _flash_consume(buf, _tile(v_ref, ki, tq), m_ref, acc_ref, tq if diag else None)

    def finish(qi):
        o = _flash_result(acc_ref[...])
        o_s = _compact_heads([o[j * tq:(j + 1) * tq] for j in range(NSA_HPG)], mine, low)
        gates = _split_dot(_sigmoid(_tile(gl_ref, qi, tq)), e_ref[0])
        y = (gates[:, 0:w] * _tile(oc_ref, qi, tq).astype(F32) + gates[:, w:2 * w] * o_s
             + gates[:, 2 * w:3 * w] * _tile(ow_ref, qi, tq).astype(F32))
        o_ref[0, pl.ds(pl.multiple_of(qi * tq, tq), tq), :] = y.astype(o_ref.dtype)
        _flash_reset(m_ref, acc_ref)

    prob = pl.program_id(0) * NSA_GROUPS + g
    _flash_stream(cnt_ref[prob], (qt_ref, kt_ref, lt_ref), prob * rows_per_problem, produce, consume, finish,
                  (sa_ref, ma_ref), (sb2_ref, mb_ref), mask_at_produce=True)


def _nsa_selected(proj3, sbias, used, o_c, o_w, small3, expand):
    b, s, _ = proj3.shape
    tq = min(256, s)
    nq = s // tq
    w = NSA_HPG * NSA_DH
    u = used[:, :, 0, :].reshape(b, nq, NSA_GROUPS, NSA_DH)[:, :, ::-1, :nq * (tq // SLC_LEN)]
    flags = (u.reshape(b, nq, NSA_GROUPS, nq, tq // SLC_LEN).max(axis=-1) > 0.0).astype(jnp.int32)
    flags = flags.transpose(0, 2, 1, 3)
    qt = jnp.arange(nq, dtype=jnp.int32)
    need = jnp.where(qt[None, :] < qt[:, None], flags, (qt[None, :] == qt[:, None]).astype(jnp.int32))
    need = need.reshape(b, NSA_GROUPS, nq * nq)
    cnt = need.sum(axis=-1).astype(jnp.int32)
    order = jnp.argsort(1 - need, axis=-1, stable=True).astype(jnp.int32)
    order = jnp.pad(order, ((0, 0), (0, 0), (0, 2)))
    rows = nq * nq + 2
    sched = (order // nq, order % nq, (order // nq == order % nq).astype(jnp.int32))
    return pl.pallas_call(
        functools.partial(_nsa_sel_kernel, tq=tq, rows_per_problem=rows),
        grid=(b, NSA_GROUPS),
        in_specs=[
            _SMEM, _SMEM, _SMEM, _SMEM, _SMEM,
            pl.BlockSpec((1, s, 4 * LANES), lambda bi, g: (bi, 0, PB_DQ // 4)),
            pl.BlockSpec((1, s, LANES), lambda bi, g: (bi, 0, 0)),
            pl.BlockSpec((1, s, LANES), lambda bi, g: (bi, 0, PB_SEL_K)),
            pl.BlockSpec((1, s, LANES), lambda bi, g: (bi, 0, PB_SEL_V)),
            pl.BlockSpec((1, s, w), lambda bi, g: (bi, 0, g)),
            pl.BlockSpec((1, s, w), lambda bi, g: (bi, 0, g)),
            pl.BlockSpec((1, s, LANES), lambda bi, g: (bi, 0, 0)),
            pl.BlockSpec((1, LANES, 3 * w), lambda bi, g: (g, 0, 0)),
        ],
        out_specs=pl.BlockSpec((1, s, w), lambda bi, g: (bi, 0, g)),
        out_shape=jax.ShapeDtypeStruct((b, s, NSA_HEADS * NSA_DH), CDT),
        scratch_shapes=_flash_scratch(NSA_HPG * tq, tq, mask_scratch=True),
        compiler_params=_cparams(("parallel", "parallel")),
        name="nsa_selected",
    )(cnt.reshape(-1), *[t.reshape(-1) for t in sched], jnp.asarray(_alibi_slopes(NSA_HEADS)),
      proj3, sbias, proj3, proj3, o_c, o_w, small3, expand)


def _merge_kernel(ya_ref, yb_ref, yc_ref, yd_ref, ga_ref, gb_ref, gc_ref, gd_ref, wb_ref, wo_ref, x_ref, o_ref):
    merged = None
    for n, (y_ref, g_ref) in enumerate(((ya_ref, ga_ref), (yb_ref, gb_ref), (yc_ref, gc_ref), (yd_ref, gd_ref))):
        t = _sigmoid(g_ref[...].astype(F32)) * _dot(y_ref[...], wb_ref[n])
        merged = t if merged is None else merged + t
    o_ref[...] = x_ref[...] + _dot(merged.astype(CDT), wo_ref[...])


def _merge(ys, proj2, wb, wo, x2):
    t, d = x2.shape
    tm = min(512, t)
    gate_blk = PB_GATE * LANES // d
    yspec = pl.BlockSpec((tm, BRANCH_WIDTH), lambda i: (i, 0))
    gspecs = [pl.BlockSpec((tm, d), functools.partial(lambda i, n: (i, gate_blk + n), n=n)) for n in range(N_BRANCH)]
    return pl.pallas_call(
        _merge_kernel,
        grid=(t // tm,),
        in_specs=[yspec] * N_BRANCH + gspecs + [
            pl.BlockSpec((N_BRANCH, BRANCH_WIDTH, d), lambda i: (0, 0, 0)),
            pl.BlockSpec((d, d), lambda i: (0, 0)),
            pl.BlockSpec((tm, d), lambda i: (i, 0)),
        ],
        out_specs=pl.BlockSpec((tm, d), lambda i: (i, 0)),
        out_shape=jax.ShapeDtypeStruct((t, d), F32),
        compiler_params=_cparams(("parallel",)),
        name="merge",
    )(*ys, proj2, proj2, proj2, proj2, wb, wo, x2)


HALO = 16


def _ffn_kernel(x_ref, xh_ref, g_ref, wu_ref, cw_ref, cb_ref, wd_ref, gf_ref, o_ref, he_ref, u_ref, act_ref,
                *, tm, fc, final):
    i = pl.program_id(1)
    x = x_ref[0]
    g = g_ref[...]
    xh = xh_ref[0] * (i > 0).astype(F32)
    he_ref[0:HALO] = _rms(xh, g).astype(CDT)
    he_ref[HALO:HALO + tm] = _rms(x, g).astype(CDT)
    he = he_ref[...]
    for c in range(D_FF // fc):
        outs = []
        for half in range(2):
            ub = u_ref.at[c % 2, half]
            lo = half * D_FF + c * fc
            ub[...] = _dot(he, wu_ref[:, lo:lo + fc])
            conv = cb_ref[:, lo:lo + fc]
            for kk in range(CONV_WIDTH):
                off = HALO - (CONV_WIDTH - 1) + kk
                conv = conv + cw_ref[kk:kk + 1, lo:lo + fc] * ub[off:off + tm, :]
            outs.append(conv)
        a, gg = outs
        act_ref[:, c * fc:(c + 1) * fc] = (a * _sigmoid(a) * gg).astype(CDT)
    y = x + _dot(act_ref[...], wd_ref[...])
    if final:
        y = _rms(y, gf_ref[...])
    o_ref[0] = y


def _ffn(x3, g, wu, cw, cb, wd, gf, final):
    b, s, d = x3.shape
    tm = min(512, s)
    fc = 256
    const = lambda shape: pl.BlockSpec(shape, lambda bi, i: (0,) * len(shape), pipeline_mode=pl.Buffered(1))
    return pl.pallas_call(
        functools.partial(_ffn_kernel, tm=tm, fc=fc, final=final),
        grid=(b, s // tm),
        in_specs=[
            pl.BlockSpec((1, tm, d), lambda bi, i: (bi, i, 0)),
            pl.BlockSpec((1, HALO, d), lambda bi, i: (bi, jnp.maximum(i * (tm // HALO) - 1, 0), 0)),
            const((1, d)), const((d, 2 * D_FF)), const((CONV_WIDTH, 2 * D_FF)), const((1, 2 * D_FF)),
            const((D_FF, d)), const((1, d)),
        ],
        out_specs=pl.BlockSpec((1, tm, d), lambda bi, i: (bi, i, 0)),
        out_shape=jax.ShapeDtypeStruct((b, s, d), F32),
        scratch_shapes=[pltpu.VMEM((tm + HALO, d), CDT), pltpu.VMEM((2, 2, tm + HALO, fc), F32),
                        pltpu.VMEM((tm, D_FF), CDT)],
        compiler_params=_cparams(("parallel", "arbitrary")),
        name="conv_glu_mlp",
    )(x3, x3, g.reshape(1, d), wu, cw, cb.reshape(1, -1), wd, gf.reshape(1, d))


def _prep_w_in(w):
    widths = (512, 512, 512, MLA_Q_LORA, MLA_KV_LORA, MLA_ROPE, 512, 512, 512, FOX_HEADS,
              512, 768, 3 * NSA_HEADS, N_BRANCH * D_MODEL)
    offs = np.cumsum((0,) + widths)
    (a_q, a_k, a_v, b_cq, b_ckv, b_kr, c_q, c_k, c_v, c_f, d_q, d_kv, d_g, gate) = [
        w[:, offs[i]:offs[i + 1]] for i in range(len(widths))]
    d = w.shape[0]
    d_q = d_q.reshape(d, NSA_GROUPS, NSA_HPG, NSA_DH).transpose(0, 2, 1, 3).reshape(d, 512)
    half = MLA_ROPE // 2
    kr_swap = jnp.concatenate([-b_kr[:, half:], b_kr[:, :half]], axis=1)
    z64 = jnp.zeros((d, LANES - MLA_ROPE), w.dtype)
    big = jnp.concatenate([
        a_q * (LOG2E * DIFF_DH ** -0.5), a_k, a_v,
        c_q * (LOG2E * FOX_DH ** -0.5), c_k, c_v,
        d_q * (LOG2E * NSA_DH ** -0.5), d_kv,
        b_cq, b_ckv, b_kr, z64, kr_swap, z64,
        gate], axis=1)
    small = jnp.concatenate([c_f, d_g, jnp.zeros((d, LANES - FOX_HEADS - 3 * NSA_HEADS), w.dtype)], axis=1)
    return big.astype(CDT), small.astype(CDT)


def _prep_mla(w_uq, w_ukv):
    r = w_uq.shape[0]
    hw = 2 * LANES
    half = MLA_ROPE // 2
    scale = LOG2E * (MLA_NOPE + MLA_ROPE) ** -0.5
    wq = (w_uq * scale).reshape(r, MLA_HEADS, MLA_NOPE + MLA_ROPE)
    nope, t1, t2 = wq[..., :MLA_NOPE], wq[..., MLA_NOPE:MLA_NOPE + half], wq[..., MLA_NOPE + half:]
    zpad = jnp.zeros((r, MLA_HEADS, hw - MLA_NOPE - MLA_ROPE), w_uq.dtype)
    wqm = jnp.concatenate([nope, t1, t2, zpad], axis=-1).reshape(r, MLA_HEADS * hw)
    wqs = jnp.concatenate([jnp.zeros_like(nope), -t2, t1, zpad], axis=-1).reshape(r, MLA_HEADS * hw)
    wkv = w_ukv.reshape(w_ukv.shape[0], MLA_HEADS, MLA_NOPE + MLA_VDIM)
    wk = wkv[..., :MLA_NOPE].reshape(-1, MLA_HEADS * MLA_NOPE)
    wv = wkv[..., MLA_NOPE:].reshape(-1, MLA_HEADS * MLA_VDIM)
    return wqm.astype(CDT), wqs.astype(CDT), wk.astype(CDT), wv.astype(CDT)


def _rope_tables(s):
    half = MLA_ROPE // 2
    inv_freq = ROPE_THETA ** (-jnp.arange(0, MLA_ROPE, 2, dtype=F32) / MLA_ROPE)
    ang = jnp.arange(s, dtype=F32)[:, None] * inv_freq[None, :]
    cos, sin = jnp.cos(ang), jnp.sin(ang)
    z = jnp.zeros((s, LANES - MLA_ROPE), F32)
    cosk = jnp.concatenate([cos, cos, z], axis=1)
    sink = jnp.concatenate([sin, sin, z], axis=1)
    cosq = jnp.concatenate([jnp.ones((s, MLA_NOPE), F32), cosk], axis=1)
    sinq = jnp.concatenate([jnp.zeros((s, MLA_NOPE), F32), sink], axis=1)
    return cosq, sinq, cosk, sink


def _prep_compress(pe, w1, w2):
    eye2 = jnp.eye(2, dtype=F32)
    w1r = w1.reshape(2, CMP_LEN, NSA_DH, CMP_HIDDEN).astype(CDT)
    same = np.eye(2, dtype=bool)
    diag_kg = jnp.asarray(same[:, None, :, None] & same[None, :, None, :])

    def expand(wpart):
        src = wpart.transpose(1, 0, 2, 3)[:, :, None, :, None, None, :]
        t = jnp.where(diag_kg[None, :, :, None, :, :, None], src, jnp.zeros((), CDT))
        return t.reshape(CMP_STRIDE * 4 * NSA_DH, 4 * CMP_HIDDEN)

    w1a, w1b = expand(w1r[:, :CMP_STRIDE]), expand(w1r[:, CMP_STRIDE:])

    def pe_row(p):
        t = jnp.broadcast_to(p.transpose(1, 0, 2)[:, :, None, :], (CMP_STRIDE, 2, NSA_GROUPS, NSA_DH))
        return jnp.pad(t.reshape(1, -1), ((0, 7), (0, 0)))

    pea, peb = pe_row(pe[:, :CMP_STRIDE]), pe_row(pe[:, CMP_STRIDE:])
    w2b = jnp.einsum('khd,kK,gG,u->kghKGud', w2, eye2, eye2, jnp.ones((2,), F32))
    w2b = w2b.reshape(4 * CMP_HIDDEN, 4 * 2 * NSA_DH)
    return w1a.astype(CDT), w1b.astype(CDT), pea.astype(CDT), peb.astype(CDT), w2b.astype(CDT)


def _gate_expand():
    e = np.zeros((NSA_GROUPS, LANES, 3, NSA_HPG, NSA_DH), np.float32)
    for g in range(NSA_GROUPS):
        for j in range(NSA_HPG):
            for br in range(3):
                e[g, SMALL_G + (g * NSA_HPG + j) * 3 + br, br, j, :] = 1.0
    return jnp.asarray(e.reshape(NSA_GROUPS, LANES, 3 * NSA_HPG * NSA_DH)).astype(CDT)


def _token_mixers(x3, l, norm_mix, w_in, diff_lambda, diff_subln, mla_norm_q, mla_w_uq, mla_norm_kv, mla_w_ukv,
                  fox_b_f, nsa_cmp_pe, nsa_cmp_w1, nsa_cmp_w2, w_branch, w_out, rope_tabs):
    b, s, d = x3.shape
    t = b * s
    x2 = x3.reshape(t, d)
    w_big, w_small = _prep_w_in(w_in)
    proj, small = _in_proj(x2, norm_mix, w_big, w_small)
    proj3 = proj.reshape(b, s, N_PROJ)
    small3 = small.reshape(b, s, LANES)

    lam_init = 0.8 - 0.6 * math.exp(-0.3 * l)
    y_a = _diff_attention(proj3, diff_lambda, diff_subln, lam_init)

    wqm, wqs, wk, wv = _prep_mla(mla_w_uq, mla_w_ukv)
    qc, kc, vv = _mla_prep(proj3, mla_norm_q, mla_norm_kv, wqm, wqs, wk, wv, rope_tabs)
    y_b = _mla_attention(qc, kc, vv)

    cf_rows = small3[:, :, SMALL_F:SMALL_F + FOX_HEADS].transpose(0, 2, 1).reshape(b * FOX_HEADS, s)
    bias_rows = jnp.tile(fox_b_f.astype(F32), b).reshape(b * FOX_HEADS, 1)
    c4 = _fox_cumsum(cf_rows, bias_rows)
    y_c = _fox_attention(proj3, c4)

    w1a, w1b, pea, peb, w2b = _prep_compress(nsa_cmp_pe, nsa_cmp_w1, nsa_cmp_w2)
    xc = proj3[:, :, PB_CMP_K * LANES:(PB_CMP_V + 1) * LANES].reshape(b, s // CMP_STRIDE, CMP_STRIDE * 2 * LANES)
    kvc = _nsa_compress(xc, w1a, w1b, pea, peb, w2b)
    n_topk = min(SLC_TOPK, s // SLC_LEN)
    o_c, sbias, used = _nsa_cmp_select(proj3, kvc, n_topk)
    o_w = _nsa_window(proj3)
    y_d = _nsa_selected(proj3, sbias, used, o_c, o_w, small3, _gate_expand())

    ys = [y.reshape(t, BRANCH_WIDTH) for y in (y_a, y_b, y_c, y_d)]
    return _merge(ys, proj, w_branch.astype(CDT), w_out.astype(CDT), x2).reshape(b, s, d)


def kernel(x, norm_mix, w_in, diff_lambda, diff_subln, mla_norm_q, mla_w_uq, mla_norm_kv, mla_w_ukv, fox_b_f,
           nsa_cmp_pe, nsa_cmp_w1, nsa_cmp_w2, w_branch, w_out, norm_ffn, w_up, conv_w, conv_b, w_down, norm_final):
    depth = w_in.shape[0]
    s = x.shape[1]
    rope_tabs = _rope_tables(s)
    for l in range(depth):
        x = _token_mixers(x, l, norm_mix[l], w_in[l], diff_lambda[l], diff_subln[l], mla_norm_q[l], mla_w_uq[l],
                          mla_norm_kv[l], mla_w_ukv[l], fox_b_f[l], nsa_cmp_pe[l], nsa_cmp_w1[l], nsa_cmp_w2[l],
                          w_branch[l], w_out[l], rope_tabs)
        x = _ffn(x, norm_ffn[l], w_up[l].astype(CDT), conv_w[l], conv_b[l], w_down[l].astype(CDT), norm_final,
                 final=(l == depth - 1))
    return x
```

```python
import functools
import math

import numpy as np
import jax
import jax.numpy as jnp
from jax import lax
from jax.experimental import pallas as pl
from jax.experimental.pallas import tpu as pltpu

F32 = jnp.float32
CDT = jnp.bfloat16

NEG = -1e30
NEG_INF = -1e30
BIG = 1e9
NORM_EPS = 1e-6
LOG2E = 1.4426950408889634
LANES = 128

D_MODEL = 1024
DIFF_HEADS, DIFF_DH = 4, 64
MLA_HEADS, MLA_NOPE, MLA_ROPE, MLA_VDIM = 4, 128, 64, 128
MLA_Q_LORA, MLA_KV_LORA = 256, 256
ROPE_THETA = 10000.0
FOX_HEADS, FOX_DH = 4, 128
NSA_HEADS, NSA_GROUPS, NSA_DH = 8, 2, 64
NSA_HPG = NSA_HEADS // NSA_GROUPS
CMP_STRIDE = 16
CMP_LEN = 2 * CMP_STRIDE
CMP_HIDDEN = 128
SLC_LEN = 64
SLC_SHIFT = 6
HALF_SHIFT = 6
SLC_TOPK = 8
WINDOW = 256
N_BRANCH = 4
BRANCH_WIDTH = 512
D_FF = 2816
CONV_WIDTH = 3

PB_AQ, PB_AK, PB_AV = 0, 4, 8
PB_CQ, PB_CK, PB_CV = 12, 16, 20
PB_DQ = 24
PB_CMP_K, PB_CMP_V, PB_SEL_K, PB_SEL_V, PB_WIN_K, PB_WIN_V = 28, 29, 30, 31, 32, 33
PB_BCQ, PB_BCKV, PB_BKR, PB_BKRS = 34, 36, 38, 39
PB_GATE = 40
N_PROJ = 72 * LANES
SMALL_F, SMALL_G = 0, 4

VMEM_LIMIT = 56 * 1024 * 1024
MXU_TILE = 256
TQ_DENSE = 512
TQ_NSA = WINDOW
TM_PROJ, TN_PROJ = 1024, 9 * MXU_TILE
TM_ROWS = 512
FFN_CHUNK = MXU_TILE
FOX_HP = 2
MLA_HP = 2


def _cparams(sem):
    return pltpu.CompilerParams(dimension_semantics=sem, vmem_limit_bytes=VMEM_LIMIT)


def _rms(xf, g):
    return xf * lax.rsqrt(jnp.mean(xf * xf, axis=-1, keepdims=True) + NORM_EPS) * g


def _sigmoid(x):
    return 0.5 * jnp.tanh(0.5 * x) + 0.5


def _dot(a, b):
    return jnp.dot(a, b, preferred_element_type=F32)


def _dot_nt(a, b):
    return lax.dot_general(a, b, (((1,), (1,)), ((), ())), preferred_element_type=F32)


def _split_dot(a, b):
    hi = a.astype(CDT)
    lo = (a - hi.astype(F32)).astype(CDT)
    return _dot(hi, b) + _dot(lo, b)


def _alibi_slopes(n):
    return (LOG2E * np.exp2(-8.0 * np.arange(1, n + 1) / n)).astype(np.float32)


def _inproj_kernel(x_ref, g_ref, w_ref, ws_ref, o_ref, os_ref, h_ref):
    @pl.when(pl.program_id(1) == 0)
    def _():
        h = _rms(x_ref[...], g_ref[...]).astype(CDT)
        h_ref[...] = h
        os_ref[...] = _dot(h, ws_ref[...])

    o_ref[...] = _dot(h_ref[...], w_ref[...]).astype(o_ref.dtype)


def _in_proj(x2, g, w, ws):
    t, d = x2.shape
    n = w.shape[1]
    tm = min(TM_PROJ, t)
    tn = TN_PROJ
    assert n % tn == 0
    return pl.pallas_call(
        _inproj_kernel,
        grid=(t // tm, n // tn),
        in_specs=[
            pl.BlockSpec((tm, d), lambda i, j: (i, 0)),
            pl.BlockSpec((1, d), lambda i, j: (0, 0)),
            pl.BlockSpec((d, tn), lambda i, j: (0, j)),
            pl.BlockSpec((d, LANES), lambda i, j: (0, 0)),
        ],
        out_specs=[
            pl.BlockSpec((tm, tn), lambda i, j: (i, j)),
            pl.BlockSpec((tm, LANES), lambda i, j: (i, 0)),
        ],
        out_shape=[jax.ShapeDtypeStruct((t, n), CDT), jax.ShapeDtypeStruct((t, LANES), F32)],
        scratch_shapes=[pltpu.VMEM((tm, d), CDT)],
        compiler_params=_cparams(("parallel", "arbitrary")),
        name="in_proj",
    )(x2, g.reshape(1, d), w, ws)


def _fox_cumsum_kernel(cf_ref, bf_ref, o_ref):
    rows, s = cf_ref.shape
    lane = lax.broadcasted_iota(jnp.int32, (rows, LANES), 1)
    carry = jnp.zeros((rows, 1), F32)
    for c in range(s // LANES):
        z = cf_ref[:, c * LANES:(c + 1) * LANES] + bf_ref[...]
        xs = jnp.minimum(z, 0.0) - jnp.log1p(jnp.exp(-jnp.abs(z)))
        d = 1
        while d < LANES:
            xs = xs + jnp.where(lane >= d, pltpu.roll(xs, d, axis=1), 0.0)
            d *= 2
        xs = xs + carry
        o_ref[:, c * LANES:(c + 1) * LANES] = xs
        carry = xs[:, LANES - 1:LANES]


def _fox_cumsum(cf_rows, bias_rows):
    return pl.pallas_call(
        _fox_cumsum_kernel,
        out_shape=jax.ShapeDtypeStruct(cf_rows.shape, F32),
        name="fox_cumsum",
    )(cf_rows, bias_rows)


def _flash_scratch(rows, tk, mask_scratch=False):
    return [pltpu.VMEM((rows, LANES), F32), pltpu.VMEM((rows, 2 * LANES), F32),
            pltpu.VMEM((rows, tk), F32), pltpu.VMEM((rows, tk), F32),
            pltpu.VMEM((rows, LANES), F32), pltpu.VMEM((rows, LANES), F32)
            ] + ([pltpu.VMEM((rows, tk), F32)] if mask_scratch else [])


def _flash_reset(m_ref, acc_ref):
    m_ref[...] = jnp.full(m_ref.shape, NEG, F32)
    acc_ref[...] = jnp.zeros(acc_ref.shape, F32)


def _flash_begin(m_ref, acc_ref, cm_ref, tq):
    _flash_reset(m_ref, acc_ref)
    cm_ref[...] = _causal_bias(cm_ref.shape[0], cm_ref.shape[1], tq)


def _row_max(s):
    return jnp.broadcast_to(jnp.max(s, axis=-1, keepdims=True), (s.shape[0], LANES))


def _causal_bias(rows, tk, tq):
    r = lax.broadcasted_iota(jnp.int32, (rows, tk), 0) & (tq - 1)
    c = lax.broadcasted_iota(jnp.int32, (rows, tk), 1)
    return jnp.where(c <= r, 0.0, NEG)


def _put_logits(buf, s, diag, cm_ref, rows=slice(None)):
    if diag is True:
        s = s + cm_ref[rows]
    elif diag is not False:
        s = s + diag.astype(F32) * cm_ref[rows]
    buf[0][rows] = s
    buf[1][rows] = _row_max(s)


def _with_ones(v):
    return jnp.concatenate([v, jnp.ones((v.shape[0], LANES), v.dtype)], axis=1)


def _flash_consume(buf, v, m_ref, acc_ref, mask_tq=None, rows=slice(None)):
    s = buf[0][rows]
    m_cur = buf[1][rows]
    if mask_tq is not None:
        s = s + _causal_bias(s.shape[0], s.shape[1], mask_tq)
        m_cur = _row_max(s)
    m_old = m_ref[rows]
    m_new = jnp.maximum(m_old, m_cur)
    alpha = jnp.exp2(m_old - m_new)
    p = jnp.exp2(s - jnp.tile(m_new, (1, s.shape[1] // LANES))).astype(CDT)
    acc_ref[rows] = jnp.tile(alpha, (1, 2)) * acc_ref[rows] + _dot(p, _with_ones(v))
    m_ref[rows] = m_new


def _flash_result(acc):
    return acc[:, :LANES] / acc[:, LANES:]


def _causal_schedule(nq):
    ent = [(qi, ki, int(ki == qi)) for qi in range(nq) for ki in range(qi + 1)]
    n = len(ent)
    a = np.asarray(ent + [ent[-1]] * 2, np.int32)
    return n, tuple(jnp.asarray(a[:, i]) for i in range(3))


def _flash_stream(n, sched, base, produce, consume, finish, buf_a, buf_b, mask_at_produce):
    qt, kt, lt = sched

    def step(cur, nxt, t, diag, next_diag):
        if nxt is not None:
            produce(nxt, qt[base + t + 1], kt[base + t + 1], next_diag if mask_at_produce else False)
        consume(cur, kt[base + t], diag and not mask_at_produce)
        if diag:
            finish(qt[base + t])

    produce(buf_a, qt[base], kt[base], mask_at_produce)

    def pair(j, c):
        t = 2 * j
        l0, l1 = lt[base + t], lt[base + t + 1]
        for d0 in (False, True):
            for d1 in (False, True):
                @pl.when(((l0 != 0) == d0) & ((l1 != 0) == d1))
                def _():
                    step(buf_a, buf_b, t, d0, d1)
                    step(buf_b, buf_a, t + 1, d1, lt[base + t + 2])
        return c

    lax.fori_loop(0, n // 2, pair, 0)

    def tail():
        step(buf_a, None, n - 1, True, None)

    if isinstance(n, int):
        if n % 2 == 1:
            tail()
    else:
        pl.when(n % 2 == 1)(tail)


def _tile(ref, i, t):
    return ref[0, pl.ds(pl.multiple_of(i * t, t), t), :]


def _diff_attn_kernel(qt_ref, kt_ref, lt_ref, slopes_ref, lam_ref, g_ref, q_ref, k_ref, v_ref, o_ref,
                      m_ref, acc_ref, sa_ref, sb_ref, ma_ref, mb_ref, *, tq, n, lam_init):
    slope = slopes_ref[pl.program_id(1)]
    _flash_reset(m_ref, acc_ref)
    col = lax.broadcasted_iota(jnp.int32, (1, tq), 1).astype(F32)
    lane = lax.broadcasted_iota(jnp.int32, (tq, LANES), 1)
    lf = lam_ref[...]
    lam = (jnp.exp(jnp.sum(lf[0:1] * lf[1:2], axis=-1, keepdims=True))
           - jnp.exp(jnp.sum(lf[2:3] * lf[3:4], axis=-1, keepdims=True)) + lam_init)

    def produce(buf, qi, ki, diag):
        q = _tile(q_ref, qi, tq)
        zero = jnp.zeros_like(q)
        qq = jnp.concatenate([jnp.where(lane < DIFF_DH, q, zero), jnp.where(lane >= DIFF_DH, q, zero)], axis=0)
        s = _dot_nt(qq, _tile(k_ref, ki, tq))
        _put_logits(buf, s + slope * (col + ((ki - qi) * tq).astype(F32)), diag, None)

    def consume(buf, ki, diag):
        _flash_consume(buf, _tile(v_ref, ki, tq), m_ref, acc_ref, tq if diag else None)

    def finish(qi):
        o = _flash_result(acc_ref[...])
        d = o[0:tq] - lam * o[tq:2 * tq]
        o_ref[0, pl.ds(pl.multiple_of(qi * tq, tq), tq), :] = (
            _rms(d, g_ref[...]) * (1.0 - lam_init)).astype(o_ref.dtype)
        _flash_reset(m_ref, acc_ref)

    _flash_stream(n, (qt_ref, kt_ref, lt_ref), 0, produce, consume, finish, (sa_ref, ma_ref), (sb_ref, mb_ref),
                  mask_at_produce=False)


_SMEM = pl.BlockSpec(memory_space=pltpu.SMEM)


def _diff_attention(proj3, diff_lambda, subln, lam_init):
    b, s, _ = proj3.shape
    tq = min(TQ_DENSE, s)
    dv = 2 * DIFF_DH
    n, sched = _causal_schedule(s // tq)
    kern = functools.partial(_diff_attn_kernel, tq=tq, n=n, lam_init=lam_init)
    return pl.pallas_call(
        kern,
        grid=(b, DIFF_HEADS),
        in_specs=[
            _SMEM, _SMEM, _SMEM, _SMEM,
            pl.BlockSpec((4, DIFF_DH), lambda bi, h: (0, 0)),
            pl.BlockSpec((1, dv), lambda bi, h: (0, 0)),
            pl.BlockSpec((1, s, LANES), lambda bi, h: (bi, 0, PB_AQ + h)),
            pl.BlockSpec((1, s, LANES), lambda bi, h: (bi, 0, PB_AK + h)),
            pl.BlockSpec((1, s, LANES), lambda bi, h: (bi, 0, PB_AV + h)),
        ],
        out_specs=pl.BlockSpec((1, s, dv), lambda bi, h: (bi, 0, h)),
        out_shape=jax.ShapeDtypeStruct((b, s, DIFF_HEADS * dv), CDT),
        scratch_shapes=_flash_scratch(2 * tq, tq),
        compiler_params=_cparams(("parallel", "parallel")),
        name="diff_attention",
    )(*sched, jnp.asarray(_alibi_slopes(DIFF_HEADS)), diff_lambda, subln.reshape(1, dv), proj3, proj3, proj3)


def _mla_prep_kernel(cq_ref, ckv_ref, kr_ref, krs_ref, gq_ref, gkv_ref, wqm_ref, wqs_ref, wk_ref, wv_ref,
                     cosq_ref, sinq_ref, cosk_ref, sink_ref, q_ref, k_ref, v_ref):
    hq = _rms(cq_ref[0].astype(F32), gq_ref[...]).astype(CDT)
    qm = _dot(hq, wqm_ref[...])
    qs = _dot(hq, wqs_ref[...])
    cosq, sinq = cosq_ref[...], sinq_ref[...]
    hw = 2 * LANES
    for h in range(MLA_HEADS):
        sl = slice(h * hw, (h + 1) * hw)
        q_ref[0, :, sl] = (qm[:, sl] * cosq + qs[:, sl] * sinq).astype(q_ref.dtype)
    hkv = _rms(ckv_ref[0].astype(F32), gkv_ref[...]).astype(CDT)
    kn = _dot(hkv, wk_ref[...])
    v_ref[0] = _dot(hkv, wv_ref[...]).astype(v_ref.dtype)
    kpe = (kr_ref[0].astype(F32) * cosk_ref[...] + krs_ref[0].astype(F32) * sink_ref[...]).astype(k_ref.dtype)
    for h in range(MLA_HEADS):
        k_ref[0, :, h * hw:h * hw + LANES] = kn[:, h * LANES:(h + 1) * LANES].astype(k_ref.dtype)
        k_ref[0, :, h * hw + LANES:(h + 1) * hw] = kpe


def _mla_prep(proj3, gq, gkv, wqm, wqs, wk, wv, tabs):
    b, s, _ = proj3.shape
    tm = min(TM_ROWS, s)
    hw = 2 * LANES
    cosq, sinq, cosk, sink = tabs
    const = lambda shape: pl.BlockSpec(shape, lambda bi, i: (0,) * len(shape))
    return pl.pallas_call(
        _mla_prep_kernel,
        grid=(b, s // tm),
        in_specs=[
            pl.BlockSpec((1, tm, MLA_Q_LORA), lambda bi, i: (bi, i, PB_BCQ // 2)),
            pl.BlockSpec((1, tm, MLA_KV_LORA), lambda bi, i: (bi, i, PB_BCKV // 2)),
            pl.BlockSpec((1, tm, LANES), lambda bi, i: (bi, i, PB_BKR)),
            pl.BlockSpec((1, tm, LANES), lambda bi, i: (bi, i, PB_BKRS)),
            const((1, MLA_Q_LORA)), const((1, MLA_KV_LORA)),
            const((MLA_Q_LORA, MLA_HEADS * hw)), const((MLA_Q_LORA, MLA_HEADS * hw)),
            const((MLA_KV_LORA, MLA_HEADS * MLA_NOPE)), const((MLA_KV_LORA, MLA_HEADS * MLA_VDIM)),
            pl.BlockSpec((tm, hw), lambda bi, i: (i, 0)), pl.BlockSpec((tm, hw), lambda bi, i: (i, 0)),
            pl.BlockSpec((tm, LANES), lambda bi, i: (i, 0)), pl.BlockSpec((tm, LANES), lambda bi, i: (i, 0)),
        ],
        out_specs=[
            pl.BlockSpec((1, tm, MLA_HEADS * hw), lambda bi, i: (bi, i, 0)),
            pl.BlockSpec((1, tm, MLA_HEADS * hw), lambda bi, i: (bi, i, 0)),
            pl.BlockSpec((1, tm, MLA_HEADS * MLA_VDIM), lambda bi, i: (bi, i, 0)),
        ],
        out_shape=[
            jax.ShapeDtypeStruct((b, s, MLA_HEADS * hw), CDT),
            jax.ShapeDtypeStruct((b, s, MLA_HEADS * hw), CDT),
            jax.ShapeDtypeStruct((b, s, MLA_HEADS * MLA_VDIM), CDT),
        ],
        compiler_params=_cparams(("parallel", "parallel")),
        name="mla_prep",
    )(proj3, proj3, proj3, proj3, gq.reshape(1, -1), gkv.reshape(1, -1), wqm, wqs, wk, wv,
      cosq, sinq, cosk, sink)


def _plain_attn_kernel(qt_ref, kt_ref, lt_ref, q_ref, k_ref, v_ref, o_ref,
                       m_ref, acc_ref, sa_ref, sb_ref, ma_ref, mb_ref, *, tq, n, hp, dk, dv):
    _flash_reset(m_ref, acc_ref)
    heads = [(slice(h * tq, (h + 1) * tq), slice(h * dk, (h + 1) * dk), slice(h * dv, (h + 1) * dv))
             for h in range(hp)]

    def produce(buf, qi, ki, diag):
        q, k = _tile(q_ref, qi, tq), _tile(k_ref, ki, tq)
        for rows, kcols, _ in heads:
            _put_logits(buf, _dot_nt(q[:, kcols], k[:, kcols]), diag, None, rows)

    def consume(buf, ki, diag):
        v = _tile(v_ref, ki, tq)
        for rows, _, vcols in heads:
            _flash_consume(buf, v[:, vcols], m_ref, acc_ref, tq if diag else None, rows)

    def finish(qi):
        for rows, _, vcols in heads:
            o_ref[0, pl.ds(pl.multiple_of(qi * tq, tq), tq), vcols] = _flash_result(acc_ref[rows]).astype(o_ref.dtype)
        _flash_reset(m_ref, acc_ref)

    _flash_stream(n, (qt_ref, kt_ref, lt_ref), 0, produce, consume, finish, (sa_ref, ma_ref), (sb_ref, mb_ref),
                  mask_at_produce=False)


def _mla_attention(qc, kc, v):
    b, s, _ = qc.shape
    tq = min(TQ_DENSE, s)
    hw = 2 * LANES
    hp = MLA_HP
    n, sched = _causal_schedule(s // tq)
    return pl.pallas_call(
        functools.partial(_plain_attn_kernel, tq=tq, n=n, hp=hp, dk=hw, dv=MLA_VDIM),
        grid=(b, MLA_HEADS // hp),
        in_specs=[
            _SMEM, _SMEM, _SMEM,
            pl.BlockSpec((1, s, hp * hw), lambda bi, h: (bi, 0, h)),
            pl.BlockSpec((1, s, hp * hw), lambda bi, h: (bi, 0, h)),
            pl.BlockSpec((1, s, hp * MLA_VDIM), lambda bi, h: (bi, 0, h)),
        ],
        out_specs=pl.BlockSpec((1, s, hp * MLA_VDIM), lambda bi, h: (bi, 0, h)),
        out_shape=jax.ShapeDtypeStruct((b, s, MLA_HEADS * MLA_VDIM), CDT),
        scratch_shapes=_flash_scratch(hp * tq, tq),
        compiler_params=_cparams(("parallel", "parallel")),
        name="mla_attention",
    )(*sched, qc, kc, v)


def _fox_attn_kernel(qt_ref, kt_ref, lt_ref, c_ref, q_ref, k_ref, v_ref, o_ref,
                     m_ref, acc_ref, sa_ref, sb_ref, ma_ref, mb_ref, *, tq, n, hp):
    _flash_reset(m_ref, acc_ref)
    heads = [(slice(h * tq, (h + 1) * tq), slice(h * FOX_DH, (h + 1) * FOX_DH)) for h in range(hp)]

    def produce(buf, qi, ki, diag):
        q, k = _tile(q_ref, qi, tq), _tile(k_ref, ki, tq)
        for h, (rows, cols) in enumerate(heads):
            cbase = c_ref[0, h, pl.ds(qi, 1), :][:, 0:1]
            s = _dot_nt(q[:, cols], k[:, cols]) + LOG2E * (cbase - c_ref[0, h, pl.ds(ki, 1), :])
            _put_logits(buf, s, diag, None, rows)

    def consume(buf, ki, diag):
        v = _tile(v_ref, ki, tq)
        for rows, cols in heads:
            _flash_consume(buf, v[:, cols], m_ref, acc_ref, tq if diag else None, rows)

    def finish(qi):
        for rows, cols in heads:
            o_ref[0, pl.ds(pl.multiple_of(qi * tq, tq), tq), cols] = _flash_result(acc_ref[rows]).astype(o_ref.dtype)
        _flash_reset(m_ref, acc_ref)

    _flash_stream(n, (qt_ref, kt_ref, lt_ref), 0, produce, consume, finish, (sa_ref, ma_ref), (sb_ref, mb_ref),
                  mask_at_produce=False)


def _fox_attention(proj3, c4):
    b, s, _ = proj3.shape
    tq = min(TQ_DENSE, s)
    nk = s // tq
    hp = FOX_HP
    w = hp * FOX_DH
    n, sched = _causal_schedule(nk)
    return pl.pallas_call(
        functools.partial(_fox_attn_kernel, tq=tq, n=n, hp=hp),
        grid=(b, FOX_HEADS // hp),
        in_specs=[
            _SMEM, _SMEM, _SMEM,
            pl.BlockSpec((1, hp, nk, tq), lambda bi, h: (bi, h, 0, 0)),
            pl.BlockSpec((1, s, w), lambda bi, h: (bi, 0, PB_CQ // hp + h)),
            pl.BlockSpec((1, s, w), lambda bi, h: (bi, 0, PB_CK // hp + h)),
            pl.BlockSpec((1, s, w), lambda bi, h: (bi, 0, PB_CV // hp + h)),
        ],
        out_specs=pl.BlockSpec((1, s, w), lambda bi, h: (bi, 0, h)),
        out_shape=jax.ShapeDtypeStruct((b, s, FOX_HEADS * FOX_DH), CDT),
        scratch_shapes=_flash_scratch(hp * tq, tq),
        compiler_params=_cparams(("parallel", "parallel")),
        name="fox_attention",
    )(*sched, c4.reshape(b, FOX_HEADS, nk, tq), proj3, proj3, proj3)


def _nsa_compress_kernel(x_ref, w1a_ref, w1b_ref, pea_ref, peb_ref, w2_ref, o_ref):
    x = x_ref[0]
    n = x.shape[0]
    pa = _dot(x, w1a_ref[...])
    pb = _dot(x, w1b_ref[...])
    pe = _dot(pea_ref[...], w1a_ref[...]) + _dot(peb_ref[...], w1b_ref[...])
    hid = pa + pltpu.roll(pb, n - 1, axis=0) + pe[0:1]
    act = 0.5 * hid * (1.0 + jnp.tanh(math.sqrt(2.0 / math.pi) * (hid + 0.044715 * hid * hid * hid)))
    o_ref[0] = _dot(act.astype(CDT), w2_ref[...]).astype(o_ref.dtype)


def _nsa_compress(xc, w1a, w1b, pea, peb, w2):
    b, n, kdim = xc.shape
    hdim = w1a.shape[1]
    const = lambda shape: pl.BlockSpec(shape, lambda bi: (0,) * len(shape))
    return pl.pallas_call(
        _nsa_compress_kernel,
        grid=(b,),
        in_specs=[pl.BlockSpec((1, n, kdim), lambda bi: (bi, 0, 0)),
                  const((kdim, hdim)), const((kdim, hdim)), const((8, kdim)), const((8, kdim)),
                  const((hdim, w2.shape[1]))],
        out_specs=pl.BlockSpec((1, n, w2.shape[1]), lambda bi: (bi, 0, 0)),
        out_shape=jax.ShapeDtypeStruct((b, n, w2.shape[1]), CDT),
        compiler_params=_cparams(("parallel",)),
        name="nsa_compress",
    )(xc, w1a, w1b, pea, peb, w2)


def _nsa_cmp_kernel(slopes_ref, q_ref, kv_ref, oc_ref, sb_ref, used_ref, *, tq, n_topk):
    qi = pl.program_id(1)
    nblk = kv_ref.shape[1]
    q0 = qi * tq
    rowpos = q0 + lax.broadcasted_iota(jnp.int32, (tq, 1), 0)
    cmp_end = lax.broadcasted_iota(jnp.int32, (1, nblk), 1) * CMP_STRIDE + (CMP_LEN - 1)
    negmask = jnp.where(rowpos >= cmp_end, 0.0, NEG)
    end_rel = (cmp_end - q0).astype(F32)
    lane = lax.broadcasted_iota(jnp.int32, (tq, LANES), 1)
    low = lane < NSA_DH
    nn = lax.broadcasted_iota(jnp.int32, (NSA_DH, nblk), 1) * CMP_STRIDE
    jj = lax.broadcasted_iota(jnp.int32, (NSA_DH, nblk), 0) * SLC_LEN
    ovt = (jnp.maximum(jnp.minimum(nn + CMP_LEN, jj + SLC_LEN) - jnp.maximum(nn, jj), 0).astype(F32)
           * (1.0 / CMP_LEN)).astype(CDT)
    jt = lax.broadcasted_iota(jnp.int32, (NSA_DH, tq), 0).astype(F32)
    blk = ((q0 + lax.broadcasted_iota(jnp.int32, (1, tq), 1)) >> SLC_SHIFT).astype(F32)
    fixed = (jt == 0.0) | (jt == blk) | (jt == blk - 1.0)
    out_of_play = fixed | (jt > blk)
    row_ok = rowpos >= CMP_LEN - 1
    outs = []
    bias = []
    for g in range(NSA_GROUPS):
        kc = kv_ref[0, :, g * LANES:(g + 1) * LANES]
        vc = kv_ref[0, :, (NSA_GROUPS + g) * LANES:(NSA_GROUPS + g + 1) * LANES]
        psum = jnp.zeros((tq, nblk), F32)
        for j in range(NSA_HPG):
            qb = q_ref[0, :, j * LANES:(j + 1) * LANES]
            qm = jnp.where(low if g == 0 else jnp.logical_not(low), qb, jnp.zeros_like(qb))
            s = _dot_nt(qm, kc) + slopes_ref[g * NSA_HPG + j] * end_rel + negmask
            e = jnp.exp2(s - jnp.max(s, axis=-1, keepdims=True))
            den = jnp.sum(e, axis=-1, keepdims=True)
            p = e * jnp.where(row_ok, 1.0 / den, 0.0)
            psum = psum + p
            outs.append(_dot(p.astype(CDT), vc))
        hi = psum.astype(CDT)
        lo = (psum - hi.astype(F32)).astype(CDT)
        imp = _dot_nt(ovt, hi) + _dot_nt(ovt, lo)
        imp = jnp.where(out_of_play, -jnp.inf, imp)
        sbt = jnp.where(fixed, 0.0, NEG)
        for _ in range(n_topk - 3):
            mx = jnp.max(imp, axis=0, keepdims=True)
            idx = jnp.min(jnp.where(imp == mx, jt, float(LANES)), axis=0, keepdims=True)
            hit = jt == idx
            sbt = jnp.where(hit, 0.0, sbt)
            imp = jnp.where(hit, -jnp.inf, imp)
        bias.append(sbt)
    sb = jnp.concatenate([bias[1], bias[0]], axis=0).T
    sb_ref[0] = sb.astype(sb_ref.dtype)
    used = jnp.max(jnp.where(sb == 0.0, 1.0, 0.0), axis=0, keepdims=True)
    used_ref[0, 0] = jnp.broadcast_to(used, used_ref.shape[2:])
    for blk_i in range(NSA_HEADS // 2):
        oc_ref[0, :, blk_i * LANES:(blk_i + 1) * LANES] = jnp.where(
            low, outs[2 * blk_i], outs[2 * blk_i + 1]).astype(oc_ref.dtype)


def _nsa_cmp_select(proj3, kvc, n_topk):
    assert n_topk >= 3, "the three always-selected blocks must fit in the top-k budget"
    b, s, _ = proj3.shape
    tq = min(TQ_NSA, s)
    nblk = kvc.shape[1]
    return pl.pallas_call(
        functools.partial(_nsa_cmp_kernel, tq=tq, n_topk=n_topk),
        grid=(b, s // tq),
        in_specs=[
            pl.BlockSpec(memory_space=pltpu.SMEM),
            pl.BlockSpec((1, tq, 4 * LANES), lambda bi, qi: (bi, qi, PB_DQ // 4)),
            pl.BlockSpec((1, nblk, kvc.shape[2]), lambda bi, qi: (bi, 0, 0)),
        ],
        out_specs=[
            pl.BlockSpec((1, tq, NSA_HEADS * NSA_DH), lambda bi, qi: (bi, qi, 0)),
            pl.BlockSpec((1, tq, LANES), lambda bi, qi: (bi, qi, 0)),
            pl.BlockSpec((1, 1, 8, LANES), lambda bi, qi: (bi, qi, 0, 0)),
        ],
        out_shape=[jax.ShapeDtypeStruct((b, s, NSA_HEADS * NSA_DH), CDT),
                   jax.ShapeDtypeStruct((b, s, LANES), CDT),
                   jax.ShapeDtypeStruct((b, s // tq, 8, LANES), F32)],
        compiler_params=_cparams(("parallel", "parallel")),
        name="nsa_cmp_select",
    )(jnp.asarray(_alibi_slopes(NSA_HEADS)), proj3, kvc)


def _compact_heads(heads, mine, low):
    both = [jnp.where(mine, a, pltpu.roll(a, NSA_DH, axis=1)) for a in heads]
    out = [jnp.where(low, both[2 * jj], both[2 * jj + 1]) for jj in range(NSA_HPG // 2)]
    return jnp.concatenate(out, axis=1)


def _nsa_win_kernel(slopes_ref, q_ref, kp_ref, kc_ref, vp_ref, vc_ref, o_ref, *, tq):
    g = pl.program_id(1)
    qi = pl.program_id(2)
    lane = lax.broadcasted_iota(jnp.int32, (tq, LANES), 1)
    low = lane < NSA_DH
    mine = (lane >> HALF_SHIFT) == g
    r = lax.broadcasted_iota(jnp.int32, (tq, tq), 0)
    c = lax.broadcasted_iota(jnp.int32, (tq, tq), 1)
    own = c <= r
    ndist = jnp.where(own, c - r, c - r - tq).astype(F32)
    own_f = jnp.where(own, 1.0, 0.0).astype(CDT)
    prev_pen = jnp.where(qi > 0, 0.0, NEG)
    kc, kp = kc_ref[0], kp_ref[0]
    vc, vp = _with_ones(vc_ref[0]), _with_ones(vp_ref[0])
    heads = []
    for j in range(NSA_HPG):
        qb = q_ref[0, :, j * LANES:(j + 1) * LANES]
        qm = jnp.where(mine, qb, jnp.zeros_like(qb))
        s = jnp.where(own, _dot_nt(qm, kc), _dot_nt(qm, kp) + prev_pen) + slopes_ref[g * NSA_HPG + j] * ndist
        p = jnp.exp2(s - jnp.max(s, axis=-1, keepdims=True)).astype(CDT)
        p_own = p * own_f
        acc = _dot(p_own, vc) + _dot(p - p_own, vp)
        heads.append(_flash_result(acc))
    o_ref[0] = _compact_heads(heads, mine, low).astype(o_ref.dtype)


def _nsa_window(proj3):
    b, s, _ = proj3.shape
    tq = WINDOW
    return pl.pallas_call(
        functools.partial(_nsa_win_kernel, tq=tq),
        grid=(b, NSA_GROUPS, s // tq),
        in_specs=[
            pl.BlockSpec(memory_space=pltpu.SMEM),
            pl.BlockSpec((1, tq, 4 * LANES), lambda bi, g, qi: (bi, qi, PB_DQ // 4)),
            pl.BlockSpec((1, tq, LANES), lambda bi, g, qi: (bi, jnp.maximum(qi - 1, 0), PB_WIN_K)),
            pl.BlockSpec((1, tq, LANES), lambda bi, g, qi: (bi, qi, PB_WIN_K)),
            pl.BlockSpec((1, tq, LANES), lambda bi, g, qi: (bi, jnp.maximum(qi - 1, 0), PB_WIN_V)),
            pl.BlockSpec((1, tq, LANES), lambda bi, g, qi: (bi, qi, PB_WIN_V)),
        ],
        out_specs=pl.BlockSpec((1, tq, NSA_HPG * NSA_DH), lambda bi, g, qi: (bi, qi, g)),
        out_shape=jax.ShapeDtypeStruct((b, s, NSA_HEADS * NSA_DH), CDT),
        compiler_params=_cparams(("parallel", "parallel", "parallel")),
        name="nsa_window",
    )(jnp.asarray(_alibi_slopes(NSA_HEADS)), proj3, proj3, proj3, proj3, proj3)


def _nsa_sel_kernel(cnt_ref, qt_ref, kt_ref, lt_ref, slopes_ref, q_ref, sb_ref, k_ref, v_ref, oc_ref, ow_ref, gl_ref,
                    e_ref, o_ref, m_ref, acc_ref, sa_ref, sb2_ref, ma_ref, mb_ref, cm_ref, *, tq, rows_per_problem):
    g = pl.program_id(1)
    w = NSA_HPG * NSA_DH
    lane = lax.broadcasted_iota(jnp.int32, (tq, LANES), 1)
    low = lane < NSA_DH
    mine = (lane >> HALF_SHIFT) == g
    _flash_begin(m_ref, acc_ref, cm_ref, tq)
    col = lax.broadcasted_iota(jnp.int32, (1, tq), 1).astype(F32)
    jl = lane & (NSA_DH - 1)
    krow = lax.broadcasted_iota(jnp.int32, (tq, LANES), 0)

    def produce(buf, qi, ki, diag):
        q = _tile(q_ref, qi, tq)
        sb = _tile(sb_ref, qi, tq)
        qa = jnp.concatenate([jnp.where(mine, q[:, j * LANES:(j + 1) * LANES], sb) for j in range(NSA_HPG)], axis=0)
        k = _tile(k_ref, ki, tq)
        onehot = jnp.where(((ki * tq + krow) >> SLC_SHIFT) == jl, 1.0, 0.0).astype(k.dtype)
        s_all = _dot_nt(qa, jnp.where(mine, k, onehot))
        rel = ((ki - qi) * tq).astype(F32)
        for j in range(NSA_HPG):
            rows = slice(j * tq, (j + 1) * tq)
            _put_logits(buf, s_all[rows] + slopes_ref[g * NSA_HPG + j] * (col + rel), diag, cm_ref, rows)

    def consume(buf, ki, diag):
        _flash_consume(buf, _tile(v_ref, ki, tq), m_ref, acc_ref, tq if diag else None)

    def finish(qi):
        o = _flash_result(acc_ref[...])
        o_s = _compact_heads([o[j * tq:(j + 1) * tq] for j in range(NSA_HPG)], mine, low)
        gates = _split_dot(_sigmoid(_tile(gl_ref, qi, tq)), e_ref[0])
        y = (gates[:, 0:w] * _tile(oc_ref, qi, tq).astype(F32) + gates[:, w:2 * w] * o_s
             + gates[:, 2 * w:3 * w] * _tile(ow_ref, qi, tq).astype(F32))
        o_ref[0, pl.ds(pl.multiple_of(qi * tq, tq), tq), :] = y.astype(o_ref.dtype)
        _flash_reset(m_ref, acc_ref)

    prob = pl.program_id(0) * NSA_GROUPS + g
    _flash_stream(cnt_ref[prob], (qt_ref, kt_ref, lt_ref), prob * rows_per_problem, produce, consume, finish,
                  (sa_ref, ma_ref), (sb2_ref, mb_ref), mask_at_produce=True)


def _nsa_selected(proj3, sbias, used, o_c, o_w, small3, expand):
    b, s, _ = proj3.shape
    tq = min(TQ_NSA, s)
    nq = s // tq
    w = NSA_HPG * NSA_DH
    u = used[:, :, 0, :].reshape(b, nq, NSA_GROUPS, NSA_DH)[:, :, ::-1, :nq * (tq // SLC_LEN)]
    flags = (u.reshape(b, nq, NSA_GROUPS, nq, tq // SLC_LEN).max(axis=-1) > 0.0).astype(jnp.int32)
    flags = flags.transpose(0, 2, 1, 3)
    qt = jnp.arange(nq, dtype=jnp.int32)
    need = jnp.where(qt[None, :] < qt[:, None], flags, (qt[None, :] == qt[:, None]).astype(jnp.int32))
    need = need.reshape(b, NSA_GROUPS, nq * nq)
    cnt = need.sum(axis=-1).astype(jnp.int32)
    order = jnp.argsort(1 - need, axis=-1, stable=True).astype(jnp.int32)
    order = jnp.pad(order, ((0, 0), (0, 0), (0, 2)))
    rows = nq * nq + 2
    sched = (order // nq, order % nq, (order // nq == order % nq).astype(jnp.int32))
    return pl.pallas_call(
        functools.partial(_nsa_sel_kernel, tq=tq, rows_per_problem=rows),
        grid=(b, NSA_GROUPS),
        in_specs=[
            _SMEM, _SMEM, _SMEM, _SMEM, _SMEM,
            pl.BlockSpec((1, s, 4 * LANES), lambda bi, g: (bi, 0, PB_DQ // 4)),
            pl.BlockSpec((1, s, LANES), lambda bi, g: (bi, 0, 0)),
            pl.BlockSpec((1, s, LANES), lambda bi, g: (bi, 0, PB_SEL_K)),
            pl.BlockSpec((1, s, LANES), lambda bi, g: (bi, 0, PB_SEL_V)),
            pl.BlockSpec((1, s, w), lambda bi, g: (bi, 0, g)),
            pl.BlockSpec((1, s, w), lambda bi, g: (bi, 0, g)),
            pl.BlockSpec((1, s, LANES), lambda bi, g: (bi, 0, 0)),
            pl.BlockSpec((1, LANES, 3 * w), lambda bi, g: (g, 0, 0)),
        ],
        out_specs=pl.BlockSpec((1, s, w), lambda bi, g: (bi, 0, g)),
        out_shape=jax.ShapeDtypeStruct((b, s, NSA_HEADS * NSA_DH), CDT),
        scratch_shapes=_flash_scratch(NSA_HPG * tq, tq, mask_scratch=True),
        compiler_params=_cparams(("parallel", "parallel")),
        name="nsa_selected",
    )(cnt.reshape(-1), *[t.reshape(-1) for t in sched], jnp.asarray(_alibi_slopes(NSA_HEADS)),
      proj3, sbias, proj3, proj3, o_c, o_w, small3, expand)


def _merge_kernel(ya_ref, yb_ref, yc_ref, yd_ref, ga_ref, gb_ref, gc_ref, gd_ref, wb_ref, wo_ref, x_ref, o_ref):
    merged = None
    for n, (y_ref, g_ref) in enumerate(((ya_ref, ga_ref), (yb_ref, gb_ref), (yc_ref, gc_ref), (yd_ref, gd_ref))):
        t = _sigmoid(g_ref[...].astype(F32)) * _dot(y_ref[...], wb_ref[n])
        merged = t if merged is None else merged + t
    o_ref[...] = x_ref[...] + _dot(merged.astype(CDT), wo_ref[...])


def _merge(ys, proj2, wb, wo, x2):
    t, d = x2.shape
    tm = min(TM_ROWS, t)
    gate_blk = PB_GATE * LANES // d
    yspec = pl.BlockSpec((tm, BRANCH_WIDTH), lambda i: (i, 0))
    gspecs = [pl.BlockSpec((tm, d), functools.partial(lambda i, n: (i, gate_blk + n), n=n)) for n in range(N_BRANCH)]
    return pl.pallas_call(
        _merge_kernel,
        grid=(t // tm,),
        in_specs=[yspec] * N_BRANCH + gspecs + [
            pl.BlockSpec((N_BRANCH, BRANCH_WIDTH, d), lambda i: (0, 0, 0)),
            pl.BlockSpec((d, d), lambda i: (0, 0)),
            pl.BlockSpec((tm, d), lambda i: (i, 0)),
        ],
        out_specs=pl.BlockSpec((tm, d), lambda i: (i, 0)),
        out_shape=jax.ShapeDtypeStruct((t, d), F32),
        compiler_params=_cparams(("parallel",)),
        name="merge",
    )(*ys, proj2, proj2, proj2, proj2, wb, wo, x2)


HALO = 16


def _ffn_kernel(x_ref, xh_ref, g_ref, wu_ref, cw_ref, cb_ref, wd_ref, gf_ref, o_ref, he_ref, u_ref, act_ref,
                *, tm, fc, final):
    i = pl.program_id(1)
    x = x_ref[0]
    g = g_ref[...]
    xh = xh_ref[0] * (i > 0).astype(F32)
    he_ref[0:HALO] = _rms(xh, g).astype(CDT)
    he_ref[HALO:HALO + tm] = _rms(x, g).astype(CDT)
    he = he_ref[...]
    for c in range(D_FF // fc):
        outs = []
        for half in range(2):
            ub = u_ref.at[c % 2, half]
            lo = half * D_FF + c * fc
            ub[...] = _dot(he, wu_ref[:, lo:lo + fc])
            conv = cb_ref[:, lo:lo + fc]
            for kk in range(CONV_WIDTH):
                off = HALO - (CONV_WIDTH - 1) + kk
                conv = conv + cw_ref[kk:kk + 1, lo:lo + fc] * ub[off:off + tm, :]
            outs.append(conv)
        a, gg = outs
        act_ref[:, c * fc:(c + 1) * fc] = (a * _sigmoid(a) * gg).astype(CDT)
    y = x + _dot(act_ref[...], wd_ref[...])
    if final:
        y = _rms(y, gf_ref[...])
    o_ref[0] = y


def _ffn(x3, g, wu, cw, cb, wd, gf, final):
    b, s, d = x3.shape
    tm = min(TM_ROWS, s)
    fc = FFN_CHUNK
    assert D_FF % fc == 0
    const = lambda shape: pl.BlockSpec(shape, lambda bi, i: (0,) * len(shape), pipeline_mode=pl.Buffered(1))
    return pl.pallas_call(
        functools.partial(_ffn_kernel, tm=tm, fc=fc, final=final),
        grid=(b, s // tm),
        in_specs=[
            pl.BlockSpec((1, tm, d), lambda bi, i: (bi, i, 0)),
            pl.BlockSpec((1, HALO, d), lambda bi, i: (bi, jnp.maximum(i * (tm // HALO) - 1, 0), 0)),
            const((1, d)), const((d, 2 * D_FF)), const((CONV_WIDTH, 2 * D_FF)), const((1, 2 * D_FF)),
            const((D_FF, d)), const((1, d)),
        ],
        out_specs=pl.BlockSpec((1, tm, d), lambda bi, i: (bi, i, 0)),
        out_shape=jax.ShapeDtypeStruct((b, s, d), F32),
        scratch_shapes=[pltpu.VMEM((tm + HALO, d), CDT), pltpu.VMEM((2, 2, tm + HALO, fc), F32),
                        pltpu.VMEM((tm, D_FF), CDT)],
        compiler_params=_cparams(("parallel", "arbitrary")),
        name="conv_glu_mlp",
    )(x3, x3, g.reshape(1, d), wu, cw, cb.reshape(1, -1), wd, gf.reshape(1, d))


def _prep_w_in(w):
    widths = (512, 512, 512, MLA_Q_LORA, MLA_KV_LORA, MLA_ROPE, 512, 512, 512, FOX_HEADS,
              512, 768, 3 * NSA_HEADS, N_BRANCH * D_MODEL)
    offs = np.cumsum((0,) + widths)
    (a_q, a_k, a_v, b_cq, b_ckv, b_kr, c_q, c_k, c_v, c_f, d_q, d_kv, d_g, gate) = [
        w[..., offs[i]:offs[i + 1]] for i in range(len(widths))]
    lead = w.shape[:-1]
    d_q = jnp.swapaxes(d_q.reshape(lead + (NSA_GROUPS, NSA_HPG, NSA_DH)), -3, -2).reshape(lead + (512,))
    half = MLA_ROPE // 2
    z64 = jnp.zeros(lead + (LANES - MLA_ROPE,), w.dtype)
    big = jnp.concatenate([a_q, a_k, a_v, c_q, c_k, c_v, d_q, d_kv,
                           b_cq, b_ckv, b_kr, z64, b_kr[..., half:], b_kr[..., :half], z64, gate], axis=-1)
    scale = np.ones((N_PROJ,), np.float32)
    scale[PB_AQ * LANES:PB_AK * LANES] = LOG2E * DIFF_DH ** -0.5
    scale[PB_CQ * LANES:PB_CK * LANES] = LOG2E * FOX_DH ** -0.5
    scale[PB_DQ * LANES:PB_CMP_K * LANES] = LOG2E * NSA_DH ** -0.5
    scale[PB_BKRS * LANES:PB_BKRS * LANES + half] = -1.0
    small = jnp.concatenate([c_f, d_g, jnp.zeros(lead + (LANES - FOX_HEADS - 3 * NSA_HEADS,), w.dtype)], axis=-1)
    return (big * jnp.asarray(scale)).astype(CDT), small.astype(CDT)


def _prep_mla(w_uq, w_ukv):
    r = w_uq.shape[0]
    hw = 2 * LANES
    half = MLA_ROPE // 2
    scale = LOG2E * (MLA_NOPE + MLA_ROPE) ** -0.5
    wq = (w_uq * scale).reshape(r, MLA_HEADS, MLA_NOPE + MLA_ROPE)
    nope, t1, t2 = wq[..., :MLA_NOPE], wq[..., MLA_NOPE:MLA_NOPE + half], wq[..., MLA_NOPE + half:]
    zpad = jnp.zeros((r, MLA_HEADS, hw - MLA_NOPE - MLA_ROPE), w_uq.dtype)
    wqm = jnp.concatenate([nope, t1, t2, zpad], axis=-1).reshape(r, MLA_HEADS * hw)
    wqs = jnp.concatenate([jnp.zeros_like(nope), -t2, t1, zpad], axis=-1).reshape(r, MLA_HEADS * hw)
    wkv = w_ukv.reshape(w_ukv.shape[0], MLA_HEADS, MLA_NOPE + MLA_VDIM)
    wk = wkv[..., :MLA_NOPE].reshape(-1, MLA_HEADS * MLA_NOPE)
    wv = wkv[..., MLA_NOPE:].reshape(-1, MLA_HEADS * MLA_VDIM)
    return wqm.astype(CDT), wqs.astype(CDT), wk.astype(CDT), wv.astype(CDT)


def _rope_tables(s):
    half = MLA_ROPE // 2
    inv_freq = ROPE_THETA ** (-jnp.arange(0, MLA_ROPE, 2, dtype=F32) / MLA_ROPE)
    ang = jnp.arange(s, dtype=F32)[:, None] * inv_freq[None, :]
    cos, sin = jnp.cos(ang), jnp.sin(ang)
    z = jnp.zeros((s, LANES - MLA_ROPE), F32)
    cosk = jnp.concatenate([cos, cos, z], axis=1)
    sink = jnp.concatenate([sin, sin, z], axis=1)
    cosq = jnp.concatenate([jnp.ones((s, MLA_NOPE), F32), cosk], axis=1)
    sinq = jnp.concatenate([jnp.zeros((s, MLA_NOPE), F32), sink], axis=1)
    return cosq, sinq, cosk, sink


def _prep_compress(pe, w1, w2):
    eye2 = jnp.eye(2, dtype=F32)
    w1r = w1.reshape(2, CMP_LEN, NSA_DH, CMP_HIDDEN).astype(CDT)
    same = np.eye(2, dtype=bool)
    diag_kg = jnp.asarray(same[:, None, :, None] & same[None, :, None, :])

    def expand(wpart):
        src = wpart.transpose(1, 0, 2, 3)[:, :, None, :, None, None, :]
        t = jnp.where(diag_kg[None, :, :, None, :, :, None], src, jnp.zeros((), CDT))
        return t.reshape(CMP_STRIDE * 4 * NSA_DH, 4 * CMP_HIDDEN)

    w1a, w1b = expand(w1r[:, :CMP_STRIDE]), expand(w1r[:, CMP_STRIDE:])

    def pe_row(p):
        t = jnp.broadcast_to(p.transpose(1, 0, 2)[:, :, None, :], (CMP_STRIDE, 2, NSA_GROUPS, NSA_DH))
        return jnp.pad(t.reshape(1, -1), ((0, 7), (0, 0)))

    pea, peb = pe_row(pe[:, :CMP_STRIDE]), pe_row(pe[:, CMP_STRIDE:])
    w2b = jnp.einsum('khd,kK,gG,u->kghKGud', w2, eye2, eye2, jnp.ones((2,), F32))
    w2b = w2b.reshape(4 * CMP_HIDDEN, 4 * 2 * NSA_DH)
    return w1a.astype(CDT), w1b.astype(CDT), pea.astype(CDT), peb.astype(CDT), w2b.astype(CDT)


def _gate_expand():
    e = np.zeros((NSA_GROUPS, LANES, 3, NSA_HPG, NSA_DH), np.float32)
    for g in range(NSA_GROUPS):
        for j in range(NSA_HPG):
            for br in range(3):
                e[g, SMALL_G + (g * NSA_HPG + j) * 3 + br, br, j, :] = 1.0
    return jnp.asarray(e.reshape(NSA_GROUPS, LANES, 3 * NSA_HPG * NSA_DH)).astype(CDT)


def _token_mixers(x3, l, norm_mix, w_in, diff_lambda, diff_subln, mla_norm_q, mla_w_uq, mla_norm_kv, mla_w_ukv,
                  fox_b_f, nsa_cmp_pe, nsa_cmp_w1, nsa_cmp_w2, w_branch, w_out, rope_tabs):
    b, s, d = x3.shape
    t = b * s
    x2 = x3.reshape(t, d)
    proj, small = _in_proj(x2, norm_mix, *w_in)
    proj3 = proj.reshape(b, s, N_PROJ)
    small3 = small.reshape(b, s, LANES)

    lam_init = 0.8 - 0.6 * math.exp(-0.3 * l)
    y_a = _diff_attention(proj3, diff_lambda, diff_subln, lam_init)

    wqm, wqs, wk, wv = _prep_mla(mla_w_uq, mla_w_ukv)
    qc, kc, vv = _mla_prep(proj3, mla_norm_q, mla_norm_kv, wqm, wqs, wk, wv, rope_tabs)
    y_b = _mla_attention(qc, kc, vv)

    cf_rows = small3[:, :, SMALL_F:SMALL_F + FOX_HEADS].transpose(0, 2, 1).reshape(b * FOX_HEADS, s)
    bias_rows = jnp.tile(fox_b_f.astype(F32), b).reshape(b * FOX_HEADS, 1)
    c4 = _fox_cumsum(cf_rows, bias_rows)
    y_c = _fox_attention(proj3, c4)

    w1a, w1b, pea, peb, w2b = _prep_compress(nsa_cmp_pe, nsa_cmp_w1, nsa_cmp_w2)
    xc = proj3[:, :, PB_CMP_K * LANES:(PB_CMP_V + 1) * LANES].reshape(b, s // CMP_STRIDE, CMP_STRIDE * 2 * LANES)
    kvc = _nsa_compress(xc, w1a, w1b, pea, peb, w2b)
    n_topk = min(SLC_TOPK, s // SLC_LEN)
    o_c, sbias, used = _nsa_cmp_select(proj3, kvc, n_topk)
    o_w = _nsa_window(proj3)
    y_d = _nsa_selected(proj3, sbias, used, o_c, o_w, small3, _gate_expand())

    ys = [y.reshape(t, BRANCH_WIDTH) for y in (y_a, y_b, y_c, y_d)]
    return _merge(ys, proj, w_branch.astype(CDT), w_out.astype(CDT), x2).reshape(b, s, d)


def kernel(x, norm_mix, w_in, diff_lambda, diff_subln, mla_norm_q, mla_w_uq, mla_norm_kv, mla_w_ukv, fox_b_f,
           nsa_cmp_pe, nsa_cmp_w1, nsa_cmp_w2, w_branch, w_out, norm_ffn, w_up, conv_w, conv_b, w_down, norm_final):
    depth = w_in.shape[0]
    s = x.shape[1]
    rope_tabs = _rope_tables(s)
    w_big, w_small = _prep_w_in(w_in)
    for l in range(depth):
        x = _token_mixers(x, l, norm_mix[l], (w_big[l], w_small[l]), diff_lambda[l], diff_subln[l], mla_norm_q[l], mla_w_uq[l],
                          mla_norm_kv[l], mla_w_ukv[l], fox_b_f[l], nsa_cmp_pe[l], nsa_cmp_w1[l], nsa_cmp_w2[l],
                          w_branch[l], w_out[l], rope_tabs)
        x = _ffn(x, norm_ffn[l], w_up[l].astype(CDT), conv_w[l], conv_b[l], w_down[l].astype(CDT), norm_final,
                 final=(l == depth - 1))
    return x
```

```python
import functools
import math

import numpy as np
import jax
import jax.numpy as jnp
from jax import lax
from jax.experimental import pallas as pl
from jax.experimental.pallas import tpu as pltpu

F32 = jnp.float32
CDT = jnp.bfloat16

NEG = -1e30
NEG_INF = -1e30
BIG = 1e9
NORM_EPS = 1e-6
LOG2E = 1.4426950408889634
LANES = 128

D_MODEL = 1024
DIFF_HEADS, DIFF_DH = 4, 64
MLA_HEADS, MLA_NOPE, MLA_ROPE, MLA_VDIM = 4, 128, 64, 128
MLA_Q_LORA, MLA_KV_LORA = 256, 256
ROPE_THETA = 10000.0
FOX_HEADS, FOX_DH = 4, 128
NSA_HEADS, NSA_GROUPS, NSA_DH = 8, 2, 64
NSA_HPG = NSA_HEADS // NSA_GROUPS
CMP_STRIDE = 16
CMP_LEN = 2 * CMP_STRIDE
CMP_HIDDEN = 128
SLC_LEN = 64
SLC_SHIFT = 6
HALF_SHIFT = 6
SLC_TOPK = 8
WINDOW = 256
N_BRANCH = 4
BRANCH_WIDTH = 512
D_FF = 2816
CONV_WIDTH = 3

PB_AQ, PB_AK, PB_AV = 0, 4, 8
PB_CQ, PB_CK, PB_CV = 12, 16, 20
PB_DQ = 24
PB_CMP_K, PB_CMP_V, PB_SEL_K, PB_SEL_V, PB_WIN_K, PB_WIN_V = 28, 29, 30, 31, 32, 33
PB_BCQ, PB_BCKV, PB_BKR, PB_BKRS = 34, 36, 38, 39
PB_GATE = 40
N_PROJ = 72 * LANES
SMALL_F, SMALL_G = 0, 4

VMEM_LIMIT = 56 * 1024 * 1024
MXU_TILE = 256
TQ_DENSE = 512
TQ_NSA = WINDOW
TM_PROJ, TN_PROJ = 1024, 9 * MXU_TILE
TM_ROWS = 512
FFN_CHUNK = MXU_TILE
FOX_HP = 2
MLA_HP = 2


def _cparams(sem):
    return pltpu.CompilerParams(dimension_semantics=sem, vmem_limit_bytes=VMEM_LIMIT)


def _rms(xf, g):
    return xf * lax.rsqrt(jnp.mean(xf * xf, axis=-1, keepdims=True) + NORM_EPS) * g


def _sigmoid(x):
    return 0.5 * jnp.tanh(0.5 * x) + 0.5


def _dot(a, b):
    return jnp.dot(a, b, preferred_element_type=F32)


def _dot_nt(a, b):
    return lax.dot_general(a, b, (((1,), (1,)), ((), ())), preferred_element_type=F32)


def _split_dot(a, b):
    hi = a.astype(CDT)
    lo = (a - hi.astype(F32)).astype(CDT)
    return _dot(hi, b) + _dot(lo, b)


def _alibi_slopes(n):
    return (LOG2E * np.exp2(-8.0 * np.arange(1, n + 1) / n)).astype(np.float32)


def _inproj_kernel(x_ref, g_ref, w_ref, ws_ref, o_ref, os_ref, h_ref):
    @pl.when(pl.program_id(1) == 0)
    def _():
        h = _rms(x_ref[...], g_ref[...]).astype(CDT)
        h_ref[...] = h
        os_ref[...] = _dot(h, ws_ref[...])

    o_ref[...] = _dot(h_ref[...], w_ref[...]).astype(o_ref.dtype)


def _in_proj(x2, g, w, ws):
    t, d = x2.shape
    n = w.shape[1]
    tm = min(TM_PROJ, t)
    tn = TN_PROJ
    assert n % tn == 0
    return pl.pallas_call(
        _inproj_kernel,
        grid=(t // tm, n // tn),
        in_specs=[
            pl.BlockSpec((tm, d), lambda i, j: (i, 0)),
            pl.BlockSpec((1, d), lambda i, j: (0, 0)),
            pl.BlockSpec((d, tn), lambda i, j: (0, j)),
            pl.BlockSpec((d, LANES), lambda i, j: (0, 0)),
        ],
        out_specs=[
            pl.BlockSpec((tm, tn), lambda i, j: (i, j)),
            pl.BlockSpec((tm, LANES), lambda i, j: (i, 0)),
        ],
        out_shape=[jax.ShapeDtypeStruct((t, n), CDT), jax.ShapeDtypeStruct((t, LANES), F32)],
        scratch_shapes=[pltpu.VMEM((tm, d), CDT)],
        compiler_params=_cparams(("parallel", "arbitrary")),
        name="in_proj",
    )(x2, g.reshape(1, d), w, ws)


def _fox_cumsum_kernel(cf_ref, bf_ref, o_ref):
    rows, s = cf_ref.shape
    lane = lax.broadcasted_iota(jnp.int32, (rows, LANES), 1)
    carry = jnp.zeros((rows, 1), F32)
    for c in range(s // LANES):
        z = cf_ref[:, c * LANES:(c + 1) * LANES] + bf_ref[...]
        xs = jnp.minimum(z, 0.0) - jnp.log1p(jnp.exp(-jnp.abs(z)))
        d = 1
        while d < LANES:
            xs = xs + jnp.where(lane >= d, pltpu.roll(xs, d, axis=1), 0.0)
            d *= 2
        xs = xs + carry
        o_ref[:, c * LANES:(c + 1) * LANES] = xs
        carry = xs[:, LANES - 1:LANES]


def _fox_cumsum(cf_rows, bias_rows):
    return pl.pallas_call(
        _fox_cumsum_kernel,
        out_shape=jax.ShapeDtypeStruct(cf_rows.shape, F32),
        name="fox_cumsum",
    )(cf_rows, bias_rows)


def _flash_scratch(rows, tk, mask_scratch=False):
    return [pltpu.VMEM((rows, LANES), F32), pltpu.VMEM((rows, 2 * LANES), F32),
            pltpu.VMEM((rows, tk), F32), pltpu.VMEM((rows, tk), F32),
            pltpu.VMEM((rows, LANES), F32), pltpu.VMEM((rows, LANES), F32)
            ] + ([pltpu.VMEM((rows, tk), F32)] if mask_scratch else [])


def _flash_reset(m_ref, acc_ref):
    m_ref[...] = jnp.full(m_ref.shape, NEG, F32)
    acc_ref[...] = jnp.zeros(acc_ref.shape, F32)


def _flash_begin(m_ref, acc_ref, cm_ref, tq):
    _flash_reset(m_ref, acc_ref)
    cm_ref[...] = _causal_bias(cm_ref.shape[0], cm_ref.shape[1], tq)


def _row_max(s):
    return jnp.broadcast_to(jnp.max(s, axis=-1, keepdims=True), (s.shape[0], LANES))


def _causal_bias(rows, tk, tq):
    r = lax.broadcasted_iota(jnp.int32, (rows, tk), 0) & (tq - 1)
    c = lax.broadcasted_iota(jnp.int32, (rows, tk), 1)
    return jnp.where(c <= r, 0.0, NEG)


def _put_logits(buf, s, diag, cm_ref, rows=slice(None)):
    if diag is True:
        s = s + cm_ref[rows]
    elif diag is not False:
        s = s + diag.astype(F32) * cm_ref[rows]
    buf[0][rows] = s
    buf[1][rows] = _row_max(s)


def _with_ones(v):
    return jnp.concatenate([v, jnp.ones((v.shape[0], LANES), v.dtype)], axis=1)


def _flash_consume(buf, v, m_ref, acc_ref, mask_tq=None, rows=slice(None)):
    s = buf[0][rows]
    m_cur = buf[1][rows]
    if mask_tq is not None:
        s = s + _causal_bias(s.shape[0], s.shape[1], mask_tq)
        m_cur = _row_max(s)
    m_old = m_ref[rows]
    m_new = jnp.maximum(m_old, m_cur)
    alpha = jnp.exp2(m_old - m_new)
    p = jnp.exp2(s - jnp.tile(m_new, (1, s.shape[1] // LANES))).astype(CDT)
    acc_ref[rows] = jnp.tile(alpha, (1, 2)) * acc_ref[rows] + _dot(p, _with_ones(v))
    m_ref[rows] = m_new


def _flash_result(acc):
    return acc[:, :LANES] / acc[:, LANES:]


def _causal_schedule(nq):
    ent = [(qi, ki, int(ki == qi)) for qi in range(nq) for ki in range(qi + 1)]
    n = len(ent)
    a = np.asarray(ent + [ent[-1]] * 2, np.int32)
    return n, tuple(jnp.asarray(a[:, i]) for i in range(3))


def _flash_stream(n, sched, base, produce, consume, finish, buf_a, buf_b, mask_at_produce):
    qt, kt, lt = sched

    def step(cur, nxt, t, diag, next_diag):
        if nxt is not None:
            produce(nxt, qt[base + t + 1], kt[base + t + 1], next_diag if mask_at_produce else False)
        consume(cur, kt[base + t], diag and not mask_at_produce)
        if diag:
            finish(qt[base + t])

    produce(buf_a, qt[base], kt[base], mask_at_produce)

    def pair(j, c):
        t = 2 * j
        l0, l1 = lt[base + t], lt[base + t + 1]
        for d0 in (False, True):
            for d1 in (False, True):
                @pl.when(((l0 != 0) == d0) & ((l1 != 0) == d1))
                def _():
                    step(buf_a, buf_b, t, d0, d1)
                    step(buf_b, buf_a, t + 1, d1, lt[base + t + 2])
        return c

    lax.fori_loop(0, n // 2, pair, 0)

    def tail():
        step(buf_a, None, n - 1, True, None)

    if isinstance(n, int):
        if n % 2 == 1:
            tail()
    else:
        pl.when(n % 2 == 1)(tail)


def _tile(ref, i, t):
    return ref[0, pl.ds(pl.multiple_of(i * t, t), t), :]


def _diff_attn_kernel(qt_ref, kt_ref, lt_ref, slopes_ref, lam_ref, g_ref, q_ref, k_ref, v_ref, o_ref,
                      m_ref, acc_ref, sa_ref, sb_ref, ma_ref, mb_ref, *, tq, n, lam_init):
    slope = slopes_ref[pl.program_id(1)]
    _flash_reset(m_ref, acc_ref)
    col = lax.broadcasted_iota(jnp.int32, (1, tq), 1).astype(F32)
    lane = lax.broadcasted_iota(jnp.int32, (tq, LANES), 1)
    lf = lam_ref[...]
    lam = (jnp.exp(jnp.sum(lf[0:1] * lf[1:2], axis=-1, keepdims=True))
           - jnp.exp(jnp.sum(lf[2:3] * lf[3:4], axis=-1, keepdims=True)) + lam_init)

    def produce(buf, qi, ki, diag):
        q = _tile(q_ref, qi, tq)
        zero = jnp.zeros_like(q)
        qq = jnp.concatenate([jnp.where(lane < DIFF_DH, q, zero), jnp.where(lane >= DIFF_DH, q, zero)], axis=0)
        s = _dot_nt(qq, _tile(k_ref, ki, tq))
        _put_logits(buf, s + slope * (col + ((ki - qi) * tq).astype(F32)), diag, None)

    def consume(buf, ki, diag):
        _flash_consume(buf, _tile(v_ref, ki, tq), m_ref, acc_ref, tq if diag else None)

    def finish(qi):
        o = _flash_result(acc_ref[...])
        d = o[0:tq] - lam * o[tq:2 * tq]
        o_ref[0, pl.ds(pl.multiple_of(qi * tq, tq), tq), :] = (
            _rms(d, g_ref[...]) * (1.0 - lam_init)).astype(o_ref.dtype)
        _flash_reset(m_ref, acc_ref)

    _flash_stream(n, (qt_ref, kt_ref, lt_ref), 0, produce, consume, finish, (sa_ref, ma_ref), (sb_ref, mb_ref),
                  mask_at_produce=False)


_SMEM = pl.BlockSpec(memory_space=pltpu.SMEM)


def _diff_attention(proj3, diff_lambda, subln, lam_init):
    b, s, _ = proj3.shape
    tq = min(TQ_DENSE, s)
    dv = 2 * DIFF_DH
    n, sched = _causal_schedule(s // tq)
    kern = functools.partial(_diff_attn_kernel, tq=tq, n=n, lam_init=lam_init)
    return pl.pallas_call(
        kern,
        grid=(b, DIFF_HEADS),
        in_specs=[
            _SMEM, _SMEM, _SMEM, _SMEM,
            pl.BlockSpec((4, DIFF_DH), lambda bi, h: (0, 0)),
            pl.BlockSpec((1, dv), lambda bi, h: (0, 0)),
            pl.BlockSpec((1, s, LANES), lambda bi, h: (bi, 0, PB_AQ + h)),
            pl.BlockSpec((1, s, LANES), lambda bi, h: (bi, 0, PB_AK + h)),
            pl.BlockSpec((1, s, LANES), lambda bi, h: (bi, 0, PB_AV + h)),
        ],
        out_specs=pl.BlockSpec((1, s, dv), lambda bi, h: (bi, 0, h)),
        out_shape=jax.ShapeDtypeStruct((b, s, DIFF_HEADS * dv), CDT),
        scratch_shapes=_flash_scratch(2 * tq, tq),
        compiler_params=_cparams(("parallel", "parallel")),
        name="diff_attention",
    )(*sched, jnp.asarray(_alibi_slopes(DIFF_HEADS)), diff_lambda, subln.reshape(1, dv), proj3, proj3, proj3)


def _mla_prep_kernel(cq_ref, ckv_ref, kr_ref, krs_ref, gq_ref, gkv_ref, wqm_ref, wqs_ref, wk_ref, wv_ref,
                     cosq_ref, sinq_ref, cosk_ref, sink_ref, q_ref, k_ref, v_ref):
    hq = _rms(cq_ref[0].astype(F32), gq_ref[...]).astype(CDT)
    qm = _dot(hq, wqm_ref[...])
    qs = _dot(hq, wqs_ref[...])
    cosq, sinq = cosq_ref[...], sinq_ref[...]
    hw = 2 * LANES
    for h in range(MLA_HEADS):
        sl = slice(h * hw, (h + 1) * hw)
        q_ref[0, :, sl] = (qm[:, sl] * cosq + qs[:, sl] * sinq).astype(q_ref.dtype)
    hkv = _rms(ckv_ref[0].astype(F32), gkv_ref[...]).astype(CDT)
    kn = _dot(hkv, wk_ref[...])
    v_ref[0] = _dot(hkv, wv_ref[...]).astype(v_ref.dtype)
    kpe = (kr_ref[0].astype(F32) * cosk_ref[...] + krs_ref[0].astype(F32) * sink_ref[...]).astype(k_ref.dtype)
    for h in range(MLA_HEADS):
        k_ref[0, :, h * hw:h * hw + LANES] = kn[:, h * LANES:(h + 1) * LANES].astype(k_ref.dtype)
        k_ref[0, :, h * hw + LANES:(h + 1) * hw] = kpe


def _mla_prep(proj3, gq, gkv, wqm, wqs, wk, wv, tabs):
    b, s, _ = proj3.shape
    tm = min(TM_ROWS, s)
    hw = 2 * LANES
    cosq, sinq, cosk, sink = tabs
    const = lambda shape: pl.BlockSpec(shape, lambda bi, i: (0,) * len(shape))
    return pl.pallas_call(
        _mla_prep_kernel,
        grid=(b, s // tm),
        in_specs=[
            pl.BlockSpec((1, tm, MLA_Q_LORA), lambda bi, i: (bi, i, PB_BCQ // 2)),
            pl.BlockSpec((1, tm, MLA_KV_LORA), lambda bi, i: (bi, i, PB_BCKV // 2)),
            pl.BlockSpec((1, tm, LANES), lambda bi, i: (bi, i, PB_BKR)),
            pl.BlockSpec((1, tm, LANES), lambda bi, i: (bi, i, PB_BKRS)),
            const((1, MLA_Q_LORA)), const((1, MLA_KV_LORA)),
            const((MLA_Q_LORA, MLA_HEADS * hw)), const((MLA_Q_LORA, MLA_HEADS * hw)),
            const((MLA_KV_LORA, MLA_HEADS * MLA_NOPE)), const((MLA_KV_LORA, MLA_HEADS * MLA_VDIM)),
            pl.BlockSpec((tm, hw), lambda bi, i: (i, 0)), pl.BlockSpec((tm, hw), lambda bi, i: (i, 0)),
            pl.BlockSpec((tm, LANES), lambda bi, i: (i, 0)), pl.BlockSpec((tm, LANES), lambda bi, i: (i, 0)),
        ],
        out_specs=[
            pl.BlockSpec((1, tm, MLA_HEADS * hw), lambda bi, i: (bi, i, 0)),
            pl.BlockSpec((1, tm, MLA_HEADS * hw), lambda bi, i: (bi, i, 0)),
            pl.BlockSpec((1, tm, MLA_HEADS * MLA_VDIM), lambda bi, i: (bi, i, 0)),
        ],
        out_shape=[
            jax.ShapeDtypeStruct((b, s, MLA_HEADS * hw), CDT),
            jax.ShapeDtypeStruct((b, s, MLA_HEADS * hw), CDT),
            jax.ShapeDtypeStruct((b, s, MLA_HEADS * MLA_VDIM), CDT),
        ],
        compiler_params=_cparams(("parallel", "parallel")),
        name="mla_prep",
    )(proj3, proj3, proj3, proj3, gq.reshape(1, -1), gkv.reshape(1, -1), wqm, wqs, wk, wv,
      cosq, sinq, cosk, sink)


def _plain_attn_kernel(qt_ref, kt_ref, lt_ref, q_ref, k_ref, v_ref, o_ref,
                       m_ref, acc_ref, sa_ref, sb_ref, ma_ref, mb_ref, *, tq, n, hp, dk, dv):
    _flash_reset(m_ref, acc_ref)
    heads = [(slice(h * tq, (h + 1) * tq), slice(h * dk, (h + 1) * dk), slice(h * dv, (h + 1) * dv))
             for h in range(hp)]

    def produce(buf, qi, ki, diag):
        q, k = _tile(q_ref, qi, tq), _tile(k_ref, ki, tq)
        for rows, kcols, _ in heads:
            _put_logits(buf, _dot_nt(q[:, kcols], k[:, kcols]), diag, None, rows)

    def consume(buf, ki, diag):
        v = _tile(v_ref, ki, tq)
        for rows, _, vcols in heads:
            _flash_consume(buf, v[:, vcols], m_ref, acc_ref, tq if diag else None, rows)

    def finish(qi):
        for rows, _, vcols in heads:
            o_ref[0, pl.ds(pl.multiple_of(qi * tq, tq), tq), vcols] = _flash_result(acc_ref[rows]).astype(o_ref.dtype)
        _flash_reset(m_ref, acc_ref)

    _flash_stream(n, (qt_ref, kt_ref, lt_ref), 0, produce, consume, finish, (sa_ref, ma_ref), (sb_ref, mb_ref),
                  mask_at_produce=False)


def _mla_attention(qc, kc, v):
    b, s, _ = qc.shape
    tq = min(TQ_DENSE, s)
    hw = 2 * LANES
    hp = MLA_HP
    n, sched = _causal_schedule(s // tq)
    return pl.pallas_call(
        functools.partial(_plain_attn_kernel, tq=tq, n=n, hp=hp, dk=hw, dv=MLA_VDIM),
        grid=(b, MLA_HEADS // hp),
        in_specs=[
            _SMEM, _SMEM, _SMEM,
            pl.BlockSpec((1, s, hp * hw), lambda bi, h: (bi, 0, h)),
            pl.BlockSpec((1, s, hp * hw), lambda bi, h: (bi, 0, h)),
            pl.BlockSpec((1, s, hp * MLA_VDIM), lambda bi, h: (bi, 0, h)),
        ],
        out_specs=pl.BlockSpec((1, s, hp * MLA_VDIM), lambda bi, h: (bi, 0, h)),
        out_shape=jax.ShapeDtypeStruct((b, s, MLA_HEADS * MLA_VDIM), CDT),
        scratch_shapes=_flash_scratch(hp * tq, tq),
        compiler_params=_cparams(("parallel", "parallel")),
        name="mla_attention",
    )(*sched, qc, kc, v)


def _fox_attn_kernel(qt_ref, kt_ref, lt_ref, c_ref, q_ref, k_ref, v_ref, o_ref,
                     m_ref, acc_ref, sa_ref, sb_ref, ma_ref, mb_ref, *, tq, n, hp):
    _flash_reset(m_ref, acc_ref)
    heads = [(slice(h * tq, (h + 1) * tq), slice(h * FOX_DH, (h + 1) * FOX_DH)) for h in range(hp)]

    def produce(buf, qi, ki, diag):
        q, k = _tile(q_ref, qi, tq), _tile(k_ref, ki, tq)
        for h, (rows, cols) in enumerate(heads):
            cbase = c_ref[0, h, pl.ds(qi, 1), :][:, 0:1]
            s = _dot_nt(q[:, cols], k[:, cols]) + LOG2E * (cbase - c_ref[0, h, pl.ds(ki, 1), :])
            _put_logits(buf, s, diag, None, rows)

    def consume(buf, ki, diag):
        v = _tile(v_ref, ki, tq)
        for rows, cols in heads:
            _flash_consume(buf, v[:, cols], m_ref, acc_ref, tq if diag else None, rows)

    def finish(qi):
        for rows, cols in heads:
            o_ref[0, pl.ds(pl.multiple_of(qi * tq, tq), tq), cols] = _flash_result(acc_ref[rows]).astype(o_ref.dtype)
        _flash_reset(m_ref, acc_ref)

    _flash_stream(n, (qt_ref, kt_ref, lt_ref), 0, produce, consume, finish, (sa_ref, ma_ref), (sb_ref, mb_ref),
                  mask_at_produce=False)


def _fox_attention(proj3, c4):
    b, s, _ = proj3.shape
    tq = min(TQ_DENSE, s)
    nk = s // tq
    hp = FOX_HP
    w = hp * FOX_DH
    n, sched = _causal_schedule(nk)
    return pl.pallas_call(
        functools.partial(_fox_attn_kernel, tq=tq, n=n, hp=hp),
        grid=(b, FOX_HEADS // hp),
        in_specs=[
            _SMEM, _SMEM, _SMEM,
            pl.BlockSpec((1, hp, nk, tq), lambda bi, h: (bi, h, 0, 0)),
            pl.BlockSpec((1, s, w), lambda bi, h: (bi, 0, PB_CQ // hp + h)),
            pl.BlockSpec((1, s, w), lambda bi, h: (bi, 0, PB_CK // hp + h)),
            pl.BlockSpec((1, s, w), lambda bi, h: (bi, 0, PB_CV // hp + h)),
        ],
        out_specs=pl.BlockSpec((1, s, w), lambda bi, h: (bi, 0, h)),
        out_shape=jax.ShapeDtypeStruct((b, s, FOX_HEADS * FOX_DH), CDT),
        scratch_shapes=_flash_scratch(hp * tq, tq),
        compiler_params=_cparams(("parallel", "parallel")),
        name="fox_attention",
    )(*sched, c4.reshape(b, FOX_HEADS, nk, tq), proj3, proj3, proj3)


def _nsa_compress_kernel(x_ref, w1a_ref, w1b_ref, pea_ref, peb_ref, w2_ref, o_ref):
    x = x_ref[0]
    n = x.shape[0]
    pa = _dot(x, w1a_ref[...])
    pb = _dot(x, w1b_ref[...])
    pe = _dot(pea_ref[...], w1a_ref[...]) + _dot(peb_ref[...], w1b_ref[...])
    hid = pa + pltpu.roll(pb, n - 1, axis=0) + pe[0:1]
    act = 0.5 * hid * (1.0 + jnp.tanh(math.sqrt(2.0 / math.pi) * (hid + 0.044715 * hid * hid * hid)))
    o_ref[0] = _dot(act.astype(CDT), w2_ref[...]).astype(o_ref.dtype)


def _nsa_compress(xc, w1a, w1b, pea, peb, w2):
    b, n, kdim = xc.shape
    hdim = w1a.shape[1]
    const = lambda shape: pl.BlockSpec(shape, lambda bi: (0,) * len(shape))
    return pl.pallas_call(
        _nsa_compress_kernel,
        grid=(b,),
        in_specs=[pl.BlockSpec((1, n, kdim), lambda bi: (bi, 0, 0)),
                  const((kdim, hdim)), const((kdim, hdim)), const((8, kdim)), const((8, kdim)),
                  const((hdim, w2.shape[1]))],
        out_specs=pl.BlockSpec((1, n, w2.shape[1]), lambda bi: (bi, 0, 0)),
        out_shape=jax.ShapeDtypeStruct((b, n, w2.shape[1]), CDT),
        compiler_params=_cparams(("parallel",)),
        name="nsa_compress",
    )(xc, w1a, w1b, pea, peb, w2)


def _nsa_cmp_kernel(slopes_ref, q_ref, kv_ref, oc_ref, sb_ref, used_ref, *, tq, n_topk):
    qi = pl.program_id(1)
    nblk = kv_ref.shape[1]
    q0 = qi * tq
    rowpos = q0 + lax.broadcasted_iota(jnp.int32, (tq, 1), 0)
    cmp_end = lax.broadcasted_iota(jnp.int32, (1, nblk), 1) * CMP_STRIDE + (CMP_LEN - 1)
    negmask = jnp.where(rowpos >= cmp_end, 0.0, NEG)
    end_rel = (cmp_end - q0).astype(F32)
    lane = lax.broadcasted_iota(jnp.int32, (tq, LANES), 1)
    low = lane < NSA_DH
    nn = lax.broadcasted_iota(jnp.int32, (NSA_DH, nblk), 1) * CMP_STRIDE
    jj = lax.broadcasted_iota(jnp.int32, (NSA_DH, nblk), 0) * SLC_LEN
    ovt = (jnp.maximum(jnp.minimum(nn + CMP_LEN, jj + SLC_LEN) - jnp.maximum(nn, jj), 0).astype(F32)
           * (1.0 / CMP_LEN)).astype(CDT)
    jt = lax.broadcasted_iota(jnp.int32, (NSA_DH, tq), 0).astype(F32)
    blk = ((q0 + lax.broadcasted_iota(jnp.int32, (1, tq), 1)) >> SLC_SHIFT).astype(F32)
    fixed = (jt == 0.0) | (jt == blk) | (jt == blk - 1.0)
    out_of_play = fixed | (jt > blk)
    row_ok = rowpos >= CMP_LEN - 1
    outs = []
    bias = []
    for g in range(NSA_GROUPS):
        kc = kv_ref[0, :, g * LANES:(g + 1) * LANES]
        vc = kv_ref[0, :, (NSA_GROUPS + g) * LANES:(NSA_GROUPS + g + 1) * LANES]
        psum = jnp.zeros((tq, nblk), F32)
        for j in range(NSA_HPG):
            qb = q_ref[0, :, j * LANES:(j + 1) * LANES]
            qm = jnp.where(low if g == 0 else jnp.logical_not(low), qb, jnp.zeros_like(qb))
            s = _dot_nt(qm, kc) + slopes_ref[g * NSA_HPG + j] * end_rel + negmask
            e = jnp.exp2(s - jnp.max(s, axis=-1, keepdims=True))
            den = jnp.sum(e, axis=-1, keepdims=True)
            p = e * jnp.where(row_ok, 1.0 / den, 0.0)
            psum = psum + p
            outs.append(_dot(p.astype(CDT), vc))
        hi = psum.astype(CDT)
        lo = (psum - hi.astype(F32)).astype(CDT)
        imp = _dot_nt(ovt, hi) + _dot_nt(ovt, lo)
        imp = jnp.where(out_of_play, -jnp.inf, imp)
        sbt = jnp.where(fixed, 0.0, NEG)
        for _ in range(n_topk - 3):
            mx = jnp.max(imp, axis=0, keepdims=True)
            idx = jnp.min(jnp.where(imp == mx, jt, float(LANES)), axis=0, keepdims=True)
            hit = jt == idx
            sbt = jnp.where(hit, 0.0, sbt)
            imp = jnp.where(hit, -jnp.inf, imp)
        bias.append(sbt)
    sb = jnp.concatenate([bias[1], bias[0]], axis=0).T
    sb_ref[0] = sb.astype(sb_ref.dtype)
    used = jnp.max(jnp.where(sb == 0.0, 1.0, 0.0), axis=0, keepdims=True)
    used_ref[0, 0] = jnp.broadcast_to(used, used_ref.shape[2:])
    for blk_i in range(NSA_HEADS // 2):
        oc_ref[0, :, blk_i * LANES:(blk_i + 1) * LANES] = jnp.where(
            low, outs[2 * blk_i], outs[2 * blk_i + 1]).astype(oc_ref.dtype)


def _nsa_cmp_select(proj3, kvc, n_topk):
    assert n_topk >= 3, "the three always-selected blocks must fit in the top-k budget"
    b, s, _ = proj3.shape
    tq = min(TQ_NSA, s)
    nblk = kvc.shape[1]
    return pl.pallas_call(
        functools.partial(_nsa_cmp_kernel, tq=tq, n_topk=n_topk),
        grid=(b, s // tq),
        in_specs=[
            pl.BlockSpec(memory_space=pltpu.SMEM),
            pl.BlockSpec((1, tq, 4 * LANES), lambda bi, qi: (bi, qi, PB_DQ // 4)),
            pl.BlockSpec((1, nblk, kvc.shape[2]), lambda bi, qi: (bi, 0, 0)),
        ],
        out_specs=[
            pl.BlockSpec((1, tq, NSA_HEADS * NSA_DH), lambda bi, qi: (bi, qi, 0)),
            pl.BlockSpec((1, tq, LANES), lambda bi, qi: (bi, qi, 0)),
            pl.BlockSpec((1, 1, 8, LANES), lambda bi, qi: (bi, qi, 0, 0)),
        ],
        out_shape=[jax.ShapeDtypeStruct((b, s, NSA_HEADS * NSA_DH), CDT),
                   jax.ShapeDtypeStruct((b, s, LANES), CDT),
                   jax.ShapeDtypeStruct((b, s // tq, 8, LANES), F32)],
        compiler_params=_cparams(("parallel", "parallel")),
        name="nsa_cmp_select",
    )(jnp.asarray(_alibi_slopes(NSA_HEADS)), proj3, kvc)


def _compact_heads(heads, mine, low):
    both = [jnp.where(mine, a, pltpu.roll(a, NSA_DH, axis=1)) for a in heads]
    out = [jnp.where(low, both[2 * jj], both[2 * jj + 1]) for jj in range(NSA_HPG // 2)]
    return jnp.concatenate(out, axis=1)


def _nsa_win_kernel(slopes_ref, q_ref, kp_ref, kc_ref, vp_ref, vc_ref, o_ref, *, tq):
    qi = pl.program_id(1)
    lane = lax.broadcasted_iota(jnp.int32, (tq, LANES), 1)
    low = lane < NSA_DH
    r = lax.broadcasted_iota(jnp.int32, (tq, tq), 0)
    c = lax.broadcasted_iota(jnp.int32, (tq, tq), 1)
    own = c <= r
    ndist = jnp.where(own, c - r, c - r - tq).astype(F32)
    own_f = jnp.where(own, 1.0, 0.0).astype(CDT)
    prev_pen = jnp.where(qi > 0, 0.0, NEG)
    q = q_ref[0]
    zero = jnp.zeros((tq, LANES), q.dtype)
    mine = (low, jnp.logical_not(low))
    qs = jnp.concatenate([jnp.where(mine[g], q[:, j * LANES:(j + 1) * LANES], zero)
                          for g in range(NSA_GROUPS) for j in range(NSA_HPG)], axis=0)
    s_own, s_prev = _dot_nt(qs, kc_ref[0]), _dot_nt(qs, kp_ref[0])
    ps = []
    for hd in range(NSA_HEADS):
        rows = slice(hd * tq, (hd + 1) * tq)
        s = jnp.where(own, s_own[rows], s_prev[rows] + prev_pen) + slopes_ref[hd] * ndist
        ps.append(jnp.exp2(s - jnp.max(s, axis=-1, keepdims=True)).astype(CDT))
    p = jnp.concatenate(ps, axis=0)
    p_own = p * jnp.tile(own_f, (NSA_HEADS, 1))
    o = _flash_result(_dot(p_own, _with_ones(vc_ref[0])) + _dot(p - p_own, _with_ones(vp_ref[0])))
    for g in range(NSA_GROUPS):
        heads = [o[(g * NSA_HPG + j) * tq:(g * NSA_HPG + j + 1) * tq] for j in range(NSA_HPG)]
        w = NSA_HPG * NSA_DH
        o_ref[0, :, g * w:(g + 1) * w] = _compact_heads(heads, mine[g], low).astype(o_ref.dtype)


def _nsa_window(proj3):
    b, s, _ = proj3.shape
    tq = WINDOW
    return pl.pallas_call(
        functools.partial(_nsa_win_kernel, tq=tq),
        grid=(b, s // tq),
        in_specs=[
            pl.BlockSpec(memory_space=pltpu.SMEM),
            pl.BlockSpec((1, tq, 4 * LANES), lambda bi, qi: (bi, qi, PB_DQ // 4)),
            pl.BlockSpec((1, tq, LANES), lambda bi, qi: (bi, jnp.maximum(qi - 1, 0), PB_WIN_K)),
            pl.BlockSpec((1, tq, LANES), lambda bi, qi: (bi, qi, PB_WIN_K)),
            pl.BlockSpec((1, tq, LANES), lambda bi, qi: (bi, jnp.maximum(qi - 1, 0), PB_WIN_V)),
            pl.BlockSpec((1, tq, LANES), lambda bi, qi: (bi, qi, PB_WIN_V)),
        ],
        out_specs=pl.BlockSpec((1, tq, NSA_HEADS * NSA_DH), lambda bi, qi: (bi, qi, 0)),
        out_shape=jax.ShapeDtypeStruct((b, s, NSA_HEADS * NSA_DH), CDT),
        compiler_params=_cparams(("parallel", "parallel")),
        name="nsa_window",
    )(jnp.asarray(_alibi_slopes(NSA_HEADS)), proj3, proj3, proj3, proj3, proj3)


def _nsa_sel_kernel(cnt_ref, qt_ref, kt_ref, lt_ref, slopes_ref, q_ref, sb_ref, k_ref, v_ref, oc_ref, ow_ref, gl_ref,
                    e_ref, o_ref, m_ref, acc_ref, sa_ref, sb2_ref, ma_ref, mb_ref, cm_ref, *, tq, rows_per_problem):
    g = pl.program_id(1)
    w = NSA_HPG * NSA_DH
    lane = lax.broadcasted_iota(jnp.int32, (tq, LANES), 1)
    low = lane < NSA_DH
    mine = (lane >> HALF_SHIFT) == g
    _flash_begin(m_ref, acc_ref, cm_ref, tq)
    col = lax.broadcasted_iota(jnp.int32, (1, tq), 1).astype(F32)
    jl = lane & (NSA_DH - 1)
    krow = lax.broadcasted_iota(jnp.int32, (tq, LANES), 0)

    def produce(buf, qi, ki, diag):
        q = _tile(q_ref, qi, tq)
        sb = _tile(sb_ref, qi, tq)
        qa = jnp.concatenate([jnp.where(mine, q[:, j * LANES:(j + 1) * LANES], sb) for j in range(NSA_HPG)], axis=0)
        k = _tile(k_ref, ki, tq)
        onehot = jnp.where(((ki * tq + krow) >> SLC_SHIFT) == jl, 1.0, 0.0).astype(k.dtype)
        s_all = _dot_nt(qa, jnp.where(mine, k, onehot))
        rel = ((ki - qi) * tq).astype(F32)
        for j in range(NSA_HPG):
            rows = slice(j * tq, (j + 1) * tq)
            _put_logits(buf, s_all[rows] + slopes_ref[g * NSA_HPG + j] * (col + rel), diag, cm_ref, rows)

    def consume(buf, ki, diag):
        _flash_consume(buf, _tile(v_ref, ki, tq), m_ref, acc_ref, tq if diag else None)

    def finish(qi):
        o = _flash_result(acc_ref[...])
        o_s = _compact_heads([o[j * tq:(j + 1) * tq] for j in range(NSA_HPG)], mine, low)
        gates = _split_dot(_sigmoid(_tile(gl_ref, qi, tq)), e_ref[0])
        y = (gates[:, 0:w] * _tile(oc_ref, qi, tq).astype(F32) + gates[:, w:2 * w] * o_s
             + gates[:, 2 * w:3 * w] * _tile(ow_ref, qi, tq).astype(F32))
        o_ref[0, pl.ds(pl.multiple_of(qi * tq, tq), tq), :] = y.astype(o_ref.dtype)
        _flash_reset(m_ref, acc_ref)

    prob = pl.program_id(0) * NSA_GROUPS + g
    _flash_stream(cnt_ref[prob], (qt_ref, kt_ref, lt_ref), prob * rows_per_problem, produce, consume, finish,
                  (sa_ref, ma_ref), (sb2_ref, mb_ref), mask_at_produce=True)


def _nsa_selected(proj3, sbias, used, o_c, o_w, small3, expand):
    b, s, _ = proj3.shape
    tq = min(TQ_NSA, s)
    nq = s // tq
    w = NSA_HPG * NSA_DH
    u = used[:, :, 0, :].reshape(b, nq, NSA_GROUPS, NSA_DH)[:, :, ::-1, :nq * (tq // SLC_LEN)]
    flags = (u.reshape(b, nq, NSA_GROUPS, nq, tq // SLC_LEN).max(axis=-1) > 0.0).astype(jnp.int32)
    flags = flags.transpose(0, 2, 1, 3)
    qt = jnp.arange(nq, dtype=jnp.int32)
    need = jnp.where(qt[None, :] < qt[:, None], flags, (qt[None, :] == qt[:, None]).astype(jnp.int32))
    need = need.reshape(b, NSA_GROUPS, nq * nq)
    cnt = need.sum(axis=-1).astype(jnp.int32)
    order = jnp.argsort(1 - need, axis=-1, stable=True).astype(jnp.int32)
    order = jnp.pad(order, ((0, 0), (0, 0), (0, 2)))
    rows = nq * nq + 2
    sched = (order // nq, order % nq, (order // nq == order % nq).astype(jnp.int32))
    return pl.pallas_call(
        functools.partial(_nsa_sel_kernel, tq=tq, rows_per_problem=rows),
        grid=(b, NSA_GROUPS),
        in_specs=[
            _SMEM, _SMEM, _SMEM, _SMEM, _SMEM,
            pl.BlockSpec((1, s, 4 * LANES), lambda bi, g: (bi, 0, PB_DQ // 4)),
            pl.BlockSpec((1, s, LANES), lambda bi, g: (bi, 0, 0)),
            pl.BlockSpec((1, s, LANES), lambda bi, g: (bi, 0, PB_SEL_K)),
            pl.BlockSpec((1, s, LANES), lambda bi, g: (bi, 0, PB_SEL_V)),
            pl.BlockSpec((1, s, w), lambda bi, g: (bi, 0, g)),
            pl.BlockSpec((1, s, w), lambda bi, g: (bi, 0, g)),
            pl.BlockSpec((1, s, LANES), lambda bi, g: (bi, 0, 0)),
            pl.BlockSpec((1, LANES, 3 * w), lambda bi, g: (g, 0, 0)),
        ],
        out_specs=pl.BlockSpec((1, s, w), lambda bi, g: (bi, 0, g)),
        out_shape=jax.ShapeDtypeStruct((b, s, NSA_HEADS * NSA_DH), CDT),
        scratch_shapes=_flash_scratch(NSA_HPG * tq, tq, mask_scratch=True),
        compiler_params=_cparams(("parallel", "parallel")),
        name="nsa_selected",
    )(cnt.reshape(-1), *[t.reshape(-1) for t in sched], jnp.asarray(_alibi_slopes(NSA_HEADS)),
      proj3, sbias, proj3, proj3, o_c, o_w, small3, expand)


def _merge_kernel(ya_ref, yb_ref, yc_ref, yd_ref, ga_ref, gb_ref, gc_ref, gd_ref, wb_ref, wo_ref, x_ref, o_ref):
    merged = None
    for n, (y_ref, g_ref) in enumerate(((ya_ref, ga_ref), (yb_ref, gb_ref), (yc_ref, gc_ref), (yd_ref, gd_ref))):
        t = _sigmoid(g_ref[...].astype(F32)) * _dot(y_ref[...], wb_ref[n])
        merged = t if merged is None else merged + t
    o_ref[...] = x_ref[...] + _dot(merged.astype(CDT), wo_ref[...])


def _merge(ys, proj2, wb, wo, x2):
    t, d = x2.shape
    tm = min(TM_ROWS, t)
    gate_blk = PB_GATE * LANES // d
    yspec = pl.BlockSpec((tm, BRANCH_WIDTH), lambda i: (i, 0))
    gspecs = [pl.BlockSpec((tm, d), functools.partial(lambda i, n: (i, gate_blk + n), n=n)) for n in range(N_BRANCH)]
    return pl.pallas_call(
        _merge_kernel,
        grid=(t // tm,),
        in_specs=[yspec] * N_BRANCH + gspecs + [
            pl.BlockSpec((N_BRANCH, BRANCH_WIDTH, d), lambda i: (0, 0, 0)),
            pl.BlockSpec((d, d), lambda i: (0, 0)),
            pl.BlockSpec((tm, d), lambda i: (i, 0)),
        ],
        out_specs=pl.BlockSpec((tm, d), lambda i: (i, 0)),
        out_shape=jax.ShapeDtypeStruct((t, d), F32),
        compiler_params=_cparams(("parallel",)),
        name="merge",
    )(*ys, proj2, proj2, proj2, proj2, wb, wo, x2)


HALO = 16


def _ffn_kernel(x_ref, xh_ref, g_ref, wu_ref, cw_ref, cb_ref, wd_ref, gf_ref, o_ref, he_ref, u_ref, act_ref,
                *, tm, fc, final):
    i = pl.program_id(1)
    x = x_ref[0]
    g = g_ref[...]
    xh = xh_ref[0] * (i > 0).astype(F32)
    he_ref[0:HALO] = _rms(xh, g).astype(CDT)
    he_ref[HALO:HALO + tm] = _rms(x, g).astype(CDT)
    he = he_ref[...]
    for c in range(D_FF // fc):
        outs = []
        for half in range(2):
            ub = u_ref.at[c % 2, half]
            lo = half * D_FF + c * fc
            ub[...] = _dot(he, wu_ref[:, lo:lo + fc])
            conv = cb_ref[:, lo:lo + fc]
            for kk in range(CONV_WIDTH):
                off = HALO - (CONV_WIDTH - 1) + kk
                conv = conv + cw_ref[kk:kk + 1, lo:lo + fc] * ub[off:off + tm, :]
            outs.append(conv)
        a, gg = outs
        act_ref[:, c * fc:(c + 1) * fc] = (a * _sigmoid(a) * gg).astype(CDT)
    y = x + _dot(act_ref[...], wd_ref[...])
    if final:
        y = _rms(y, gf_ref[...])
    o_ref[0] = y


def _ffn(x3, g, wu, cw, cb, wd, gf, final):
    b, s, d = x3.shape
    tm = min(TM_ROWS, s)
    fc = FFN_CHUNK
    assert D_FF % fc == 0
    const = lambda shape: pl.BlockSpec(shape, lambda bi, i: (0,) * len(shape), pipeline_mode=pl.Buffered(1))
    return pl.pallas_call(
        functools.partial(_ffn_kernel, tm=tm, fc=fc, final=final),
        grid=(b, s // tm),
        in_specs=[
            pl.BlockSpec((1, tm, d), lambda bi, i: (bi, i, 0)),
            pl.BlockSpec((1, HALO, d), lambda bi, i: (bi, jnp.maximum(i * (tm // HALO) - 1, 0), 0)),
            const((1, d)), const((d, 2 * D_FF)), const((CONV_WIDTH, 2 * D_FF)), const((1, 2 * D_FF)),
            const((D_FF, d)), const((1, d)),
        ],
        out_specs=pl.BlockSpec((1, tm, d), lambda bi, i: (bi, i, 0)),
        out_shape=jax.ShapeDtypeStruct((b, s, d), F32),
        scratch_shapes=[pltpu.VMEM((tm + HALO, d), CDT), pltpu.VMEM((2, 2, tm + HALO, fc), F32),
                        pltpu.VMEM((tm, D_FF), CDT)],
        compiler_params=_cparams(("parallel", "arbitrary")),
        name="conv_glu_mlp",
    )(x3, x3, g.reshape(1, d), wu, cw, cb.reshape(1, -1), wd, gf.reshape(1, d))


def _prep_w_in(w):
    widths = (512, 512, 512, MLA_Q_LORA, MLA_KV_LORA, MLA_ROPE, 512, 512, 512, FOX_HEADS,
              512, 768, 3 * NSA_HEADS, N_BRANCH * D_MODEL)
    offs = np.cumsum((0,) + widths)
    (a_q, a_k, a_v, b_cq, b_ckv, b_kr, c_q, c_k, c_v, c_f, d_q, d_kv, d_g, gate) = [
        w[:, offs[i]:offs[i + 1]] for i in range(len(widths))]
    d = w.shape[0]
    d_q = d_q.reshape(d, NSA_GROUPS, NSA_HPG, NSA_DH).transpose(0, 2, 1, 3).reshape(d, 512)
    half = MLA_ROPE // 2
    kr_swap = jnp.concatenate([-b_kr[:, half:], b_kr[:, :half]], axis=1)
    z64 = jnp.zeros((d, LANES - MLA_ROPE), w.dtype)
    big = jnp.concatenate([
        a_q * (LOG2E * DIFF_DH ** -0.5), a_k, a_v,
        c_q * (LOG2E * FOX_DH ** -0.5), c_k, c_v,
        d_q * (LOG2E * NSA_DH ** -0.5), d_kv,
        b_cq, b_ckv, b_kr, z64, kr_swap, z64,
        gate], axis=1)
    small = jnp.concatenate([c_f, d_g, jnp.zeros((d, LANES - FOX_HEADS - 3 * NSA_HEADS), w.dtype)], axis=1)
    return big.astype(CDT), small.astype(CDT)


def _prep_mla(w_uq, w_ukv):
    r = w_uq.shape[0]
    hw = 2 * LANES
    half = MLA_ROPE // 2
    scale = LOG2E * (MLA_NOPE + MLA_ROPE) ** -0.5
    wq = (w_uq * scale).reshape(r, MLA_HEADS, MLA_NOPE + MLA_ROPE)
    nope, t1, t2 = wq[..., :MLA_NOPE], wq[..., MLA_NOPE:MLA_NOPE + half], wq[..., MLA_NOPE + half:]
    zpad = jnp.zeros((r, MLA_HEADS, hw - MLA_NOPE - MLA_ROPE), w_uq.dtype)
    wqm = jnp.concatenate([nope, t1, t2, zpad], axis=-1).reshape(r, MLA_HEADS * hw)
    wqs = jnp.concatenate([jnp.zeros_like(nope), -t2, t1, zpad], axis=-1).reshape(r, MLA_HEADS * hw)
    wkv = w_ukv.reshape(w_ukv.shape[0], MLA_HEADS, MLA_NOPE + MLA_VDIM)
    wk = wkv[..., :MLA_NOPE].reshape(-1, MLA_HEADS * MLA_NOPE)
    wv = wkv[..., MLA_NOPE:].reshape(-1, MLA_HEADS * MLA_VDIM)
    return wqm.astype(CDT), wqs.astype(CDT), wk.astype(CDT), wv.astype(CDT)


def _rope_tables(s):
    half = MLA_ROPE // 2
    inv_freq = ROPE_THETA ** (-jnp.arange(0, MLA_ROPE, 2, dtype=F32) / MLA_ROPE)
    ang = jnp.arange(s, dtype=F32)[:, None] * inv_freq[None, :]
    cos, sin = jnp.cos(ang), jnp.sin(ang)
    z = jnp.zeros((s, LANES - MLA_ROPE), F32)
    cosk = jnp.concatenate([cos, cos, z], axis=1)
    sink = jnp.concatenate([sin, sin, z], axis=1)
    cosq = jnp.concatenate([jnp.ones((s, MLA_NOPE), F32), cosk], axis=1)
    sinq = jnp.concatenate([jnp.zeros((s, MLA_NOPE), F32), sink], axis=1)
    return cosq, sinq, cosk, sink


def _prep_compress(pe, w1, w2):
    eye2 = jnp.eye(2, dtype=F32)
    w1r = w1.reshape(2, CMP_LEN, NSA_DH, CMP_HIDDEN).astype(CDT)
    same = np.eye(2, dtype=bool)
    diag_kg = jnp.asarray(same[:, None, :, None] & same[None, :, None, :])

    def expand(wpart):
        src = wpart.transpose(1, 0, 2, 3)[:, :, None, :, None, None, :]
        t = jnp.where(diag_kg[None, :, :, None, :, :, None], src, jnp.zeros((), CDT))
        return t.reshape(CMP_STRIDE * 4 * NSA_DH, 4 * CMP_HIDDEN)

    w1a, w1b = expand(w1r[:, :CMP_STRIDE]), expand(w1r[:, CMP_STRIDE:])

    def pe_row(p):
        t = jnp.broadcast_to(p.transpose(1, 0, 2)[:, :, None, :], (CMP_STRIDE, 2, NSA_GROUPS, NSA_DH))
        return jnp.pad(t.reshape(1, -1), ((0, 7), (0, 0)))

    pea, peb = pe_row(pe[:, :CMP_STRIDE]), pe_row(pe[:, CMP_STRIDE:])
    w2b = jnp.einsum('khd,kK,gG,u->kghKGud', w2, eye2, eye2, jnp.ones((2,), F32))
    w2b = w2b.reshape(4 * CMP_HIDDEN, 4 * 2 * NSA_DH)
    return w1a.astype(CDT), w1b.astype(CDT), pea.astype(CDT), peb.astype(CDT), w2b.astype(CDT)


def _gate_expand():
    e = np.zeros((NSA_GROUPS, LANES, 3, NSA_HPG, NSA_DH), np.float32)
    for g in range(NSA_GROUPS):
        for j in range(NSA_HPG):
            for br in range(3):
                e[g, SMALL_G + (g * NSA_HPG + j) * 3 + br, br, j, :] = 1.0
    return jnp.asarray(e.reshape(NSA_GROUPS, LANES, 3 * NSA_HPG * NSA_DH)).astype(CDT)


def _token_mixers(x3, l, norm_mix, w_in, diff_lambda, diff_subln, mla_norm_q, mla_w_uq, mla_norm_kv, mla_w_ukv,
                  fox_b_f, nsa_cmp_pe, nsa_cmp_w1, nsa_cmp_w2, w_branch, w_out, rope_tabs):
    b, s, d = x3.shape
    t = b * s
    x2 = x3.reshape(t, d)
    proj, small = _in_proj(x2, norm_mix, *_prep_w_in(w_in))
    proj3 = proj.reshape(b, s, N_PROJ)
    small3 = small.reshape(b, s, LANES)

    lam_init = 0.8 - 0.6 * math.exp(-0.3 * l)
    y_a = _diff_attention(proj3, diff_lambda, diff_subln, lam_init)

    wqm, wqs, wk, wv = _prep_mla(mla_w_uq, mla_w_ukv)
    qc, kc, vv = _mla_prep(proj3, mla_norm_q, mla_norm_kv, wqm, wqs, wk, wv, rope_tabs)
    y_b = _mla_attention(qc, kc, vv)

    cf_rows = small3[:, :, SMALL_F:SMALL_F + FOX_HEADS].transpose(0, 2, 1).reshape(b * FOX_HEADS, s)
    bias_rows = jnp.tile(fox_b_f.astype(F32), b).reshape(b * FOX_HEADS, 1)
    c4 = _fox_cumsum(cf_rows, bias_rows)
    y_c = _fox_attention(proj3, c4)

    w1a, w1b, pea, peb, w2b = _prep_compress(nsa_cmp_pe, nsa_cmp_w1, nsa_cmp_w2)
    xc = proj3[:, :, PB_CMP_K * LANES:(PB_CMP_V + 1) * LANES].reshape(b, s // CMP_STRIDE, CMP_STRIDE * 2 * LANES)
    kvc = _nsa_compress(xc, w1a, w1b, pea, peb, w2b)
    n_topk = min(SLC_TOPK, s // SLC_LEN)
    o_c, sbias, used = _nsa_cmp_select(proj3, kvc, n_topk)
    o_w = _nsa_window(proj3)
    y_d = _nsa_selected(proj3, sbias, used, o_c, o_w, small3, _gate_expand())

    ys = [y.reshape(t, BRANCH_WIDTH) for y in (y_a, y_b, y_c, y_d)]
    return _merge(ys, proj, w_branch.astype(CDT), w_out.astype(CDT), x2).reshape(b, s, d)


def kernel(x, norm_mix, w_in, diff_lambda, diff_subln, mla_norm_q, mla_w_uq, mla_norm_kv, mla_w_ukv, fox_b_f,
           nsa_cmp_pe, nsa_cmp_w1, nsa_cmp_w2, w_branch, w_out, norm_ffn, w_up, conv_w, conv_b, w_down, norm_final):
    depth = w_in.shape[0]
    s = x.shape[1]
    rope_tabs = _rope_tables(s)
    for l in range(depth):
        x = _token_mixers(x, l, norm_mix[l], w_in[l], diff_lambda[l], diff_subln[l], mla_norm_q[l], mla_w_uq[l],
                          mla_norm_kv[l], mla_w_ukv[l], fox_b_f[l], nsa_cmp_pe[l], nsa_cmp_w1[l], nsa_cmp_w2[l],
                          w_branch[l], w_out[l], rope_tabs)
        x = _ffn(x, norm_ffn[l], w_up[l].astype(CDT), conv_w[l], conv_b[l], w_down[l].astype(CDT), norm_final,
                 final=(l == depth - 1))
    return x
```

```python
import functools
import math

import numpy as np
import jax
import jax.numpy as jnp
from jax import lax
from jax.experimental import pallas as pl
from jax.experimental.pallas import tpu as pltpu

F32 = jnp.float32
CDT = jnp.bfloat16

NEG = -1e30
NEG_INF = -1e30
BIG = 1e9
NORM_EPS = 1e-6
LOG2E = 1.4426950408889634
LANES = 128

D_MODEL = 1024
DIFF_HEADS, DIFF_DH = 4, 64
MLA_HEADS, MLA_NOPE, MLA_ROPE, MLA_VDIM = 4, 128, 64, 128
MLA_Q_LORA, MLA_KV_LORA = 256, 256
ROPE_THETA = 10000.0
FOX_HEADS, FOX_DH = 4, 128
NSA_HEADS, NSA_GROUPS, NSA_DH = 8, 2, 64
NSA_HPG = NSA_HEADS // NSA_GROUPS
CMP_STRIDE = 16
CMP_LEN = 2 * CMP_STRIDE
CMP_HIDDEN = 128
SLC_LEN = 64
SLC_SHIFT = 6
HALF_SHIFT = 6
SLC_TOPK = 8
WINDOW = 256
N_BRANCH = 4
BRANCH_WIDTH = 512
D_FF = 2816
CONV_WIDTH = 3

PB_AQ, PB_AK, PB_AV = 0, 4, 8
PB_CQ, PB_CK, PB_CV = 12, 16, 20
PB_DQ = 24
PB_CMP_K, PB_CMP_V, PB_SEL_K, PB_SEL_V, PB_WIN_K, PB_WIN_V = 28, 29, 30, 31, 32, 33
PB_BCQ, PB_BCKV, PB_BKR, PB_BKRS = 34, 36, 38, 39
PB_GATE = 40
N_PROJ = 72 * LANES
SMALL_F, SMALL_G = 0, 4

VMEM_LIMIT = 56 * 1024 * 1024
MXU_TILE = 256
TQ_DENSE = 512
TQ_NSA = WINDOW
TM_PROJ, TN_PROJ = 1024, 9 * MXU_TILE
TM_ROWS = 512
FFN_CHUNK = MXU_TILE
FOX_HP = 2
MLA_HP = 2


def _cparams(sem):
    return pltpu.CompilerParams(dimension_semantics=sem, vmem_limit_bytes=VMEM_LIMIT)


def _rms(xf, g):
    return xf * lax.rsqrt(jnp.mean(xf * xf, axis=-1, keepdims=True) + NORM_EPS) * g


def _sigmoid(x):
    return 0.5 * jnp.tanh(0.5 * x) + 0.5


def _dot(a, b):
    return jnp.dot(a, b, preferred_element_type=F32)


def _dot_nt(a, b):
    return lax.dot_general(a, b, (((1,), (1,)), ((), ())), preferred_element_type=F32)


def _split_dot(a, b):
    hi = a.astype(CDT)
    lo = (a - hi.astype(F32)).astype(CDT)
    return _dot(hi, b) + _dot(lo, b)


def _alibi_slopes(n):
    return (LOG2E * np.exp2(-8.0 * np.arange(1, n + 1) / n)).astype(np.float32)


def _inproj_kernel(x_ref, g_ref, w_ref, ws_ref, o_ref, os_ref, h_ref):
    @pl.when(pl.program_id(1) == 0)
    def _():
        h = _rms(x_ref[...], g_ref[...]).astype(CDT)
        h_ref[...] = h
        os_ref[...] = _dot(h, ws_ref[...])

    o_ref[...] = _dot(h_ref[...], w_ref[...]).astype(o_ref.dtype)


def _in_proj(x2, g, w, ws):
    t, d = x2.shape
    n = w.shape[1]
    tm = min(TM_PROJ, t)
    tn = TN_PROJ
    assert n % tn == 0
    return pl.pallas_call(
        _inproj_kernel,
        grid=(t // tm, n // tn),
        in_specs=[
            pl.BlockSpec((tm, d), lambda i, j: (i, 0)),
            pl.BlockSpec((1, d), lambda i, j: (0, 0)),
            pl.BlockSpec((d, tn), lambda i, j: (0, j)),
            pl.BlockSpec((d, LANES), lambda i, j: (0, 0)),
        ],
        out_specs=[
            pl.BlockSpec((tm, tn), lambda i, j: (i, j)),
            pl.BlockSpec((tm, LANES), lambda i, j: (i, 0)),
        ],
        out_shape=[jax.ShapeDtypeStruct((t, n), CDT), jax.ShapeDtypeStruct((t, LANES), F32)],
        scratch_shapes=[pltpu.VMEM((tm, d), CDT)],
        compiler_params=_cparams(("parallel", "arbitrary")),
        name="in_proj",
    )(x2, g.reshape(1, d), w, ws)


def _fox_cumsum_kernel(cf_ref, bf_ref, o_ref):
    rows, s = cf_ref.shape
    lane = lax.broadcasted_iota(jnp.int32, (rows, LANES), 1)
    carry = jnp.zeros((rows, 1), F32)
    for c in range(s // LANES):
        z = cf_ref[:, c * LANES:(c + 1) * LANES] + bf_ref[...]
        xs = jnp.minimum(z, 0.0) - jnp.log1p(jnp.exp(-jnp.abs(z)))
        d = 1
        while d < LANES:
            xs = xs + jnp.where(lane >= d, pltpu.roll(xs, d, axis=1), 0.0)
            d *= 2
        xs = xs + carry
        o_ref[:, c * LANES:(c + 1) * LANES] = xs
        carry = xs[:, LANES - 1:LANES]


def _fox_cumsum(cf_rows, bias_rows):
    return pl.pallas_call(
        _fox_cumsum_kernel,
        out_shape=jax.ShapeDtypeStruct(cf_rows.shape, F32),
        name="fox_cumsum",
    )(cf_rows, bias_rows)


def _flash_scratch(rows, tk, mask_scratch=False):
    return [pltpu.VMEM((rows, LANES), F32), pltpu.VMEM((rows, 2 * LANES), F32),
            pltpu.VMEM((rows, tk), F32), pltpu.VMEM((rows, tk), F32),
            pltpu.VMEM((rows, LANES), F32), pltpu.VMEM((rows, LANES), F32)
            ] + ([pltpu.VMEM((rows, tk), F32)] if mask_scratch else [])


def _flash_reset(m_ref, acc_ref):
    m_ref[...] = jnp.full(m_ref.shape, NEG, F32)
    acc_ref[...] = jnp.zeros(acc_ref.shape, F32)


def _flash_begin(m_ref, acc_ref, cm_ref, tq):
    _flash_reset(m_ref, acc_ref)
    cm_ref[...] = _causal_bias(cm_ref.shape[0], cm_ref.shape[1], tq)


def _row_max(s):
    return jnp.broadcast_to(jnp.max(s, axis=-1, keepdims=True), (s.shape[0], LANES))


def _causal_bias(rows, tk, tq):
    r = lax.broadcasted_iota(jnp.int32, (rows, tk), 0) & (tq - 1)
    c = lax.broadcasted_iota(jnp.int32, (rows, tk), 1)
    return jnp.where(c <= r, 0.0, NEG)


def _put_logits(buf, s, diag, cm_ref, rows=slice(None)):
    if diag is True:
        s = s + cm_ref[rows]
    elif diag is not False:
        s = s + diag.astype(F32) * cm_ref[rows]
    buf[0][rows] = s
    buf[1][rows] = _row_max(s)


def _with_ones(v):
    return jnp.concatenate([v, jnp.ones((v.shape[0], LANES), v.dtype)], axis=1)


def _flash_consume(buf, v, m_ref, acc_ref, mask_tq=None, rows=slice(None)):
    s = buf[0][rows]
    m_cur = buf[1][rows]
    if mask_tq is not None:
        s = s + _causal_bias(s.shape[0], s.shape[1], mask_tq)
        m_cur = _row_max(s)
    m_old = m_ref[rows]
    m_new = jnp.maximum(m_old, m_cur)
    alpha = jnp.exp2(m_old - m_new)
    p = jnp.exp2(s - jnp.tile(m_new, (1, s.shape[1] // LANES))).astype(CDT)
    acc_ref[rows] = jnp.tile(alpha, (1, 2)) * acc_ref[rows] + _dot(p, _with_ones(v))
    m_ref[rows] = m_new


def _flash_result(acc):
    return acc[:, :LANES] / acc[:, LANES:]


def _causal_schedule(nq):
    ent = [(qi, ki, int(ki == qi)) for qi in range(nq) for ki in range(qi + 1)]
    n = len(ent)
    a = np.asarray(ent + [ent[-1]] * 2, np.int32)
    return n, tuple(jnp.asarray(a[:, i]) for i in range(3))


def _flash_stream(n, sched, base, produce, consume, finish, buf_a, buf_b, mask_at_produce):
    qt, kt, lt = sched

    def step(cur, nxt, t, diag, next_diag):
        if nxt is not None:
            produce(nxt, qt[base + t + 1], kt[base + t + 1], next_diag if mask_at_produce else False)
        consume(cur, kt[base + t], diag and not mask_at_produce)
        if diag:
            finish(qt[base + t])

    produce(buf_a, qt[base], kt[base], mask_at_produce)

    def pair(j, c):
        t = 2 * j
        l0, l1 = lt[base + t], lt[base + t + 1]
        for d0 in (False, True):
            for d1 in (False, True):
                @pl.when(((l0 != 0) == d0) & ((l1 != 0) == d1))
                def _():
                    step(buf_a, buf_b, t, d0, d1)
                    step(buf_b, buf_a, t + 1, d1, lt[base + t + 2])
        return c

    lax.fori_loop(0, n // 2, pair, 0)

    def tail():
        step(buf_a, None, n - 1, True, None)

    if isinstance(n, int):
        if n % 2 == 1:
            tail()
    else:
        pl.when(n % 2 == 1)(tail)


def _tile(ref, i, t):
    return ref[0, pl.ds(pl.multiple_of(i * t, t), t), :]


def _diff_attn_kernel(qt_ref, kt_ref, lt_ref, slopes_ref, lam_ref, g_ref, q_ref, k_ref, v_ref, o_ref,
                      m_ref, acc_ref, sa_ref, sb_ref, ma_ref, mb_ref, *, tq, n, lam_init):
    slope = slopes_ref[pl.program_id(1)]
    _flash_reset(m_ref, acc_ref)
    col = lax.broadcasted_iota(jnp.int32, (1, tq), 1).astype(F32)
    lane = lax.broadcasted_iota(jnp.int32, (tq, LANES), 1)
    lf = lam_ref[...]
    lam = (jnp.exp(jnp.sum(lf[0:1] * lf[1:2], axis=-1, keepdims=True))
           - jnp.exp(jnp.sum(lf[2:3] * lf[3:4], axis=-1, keepdims=True)) + lam_init)

    def produce(buf, qi, ki, diag):
        q = _tile(q_ref, qi, tq)
        zero = jnp.zeros_like(q)
        qq = jnp.concatenate([jnp.where(lane < DIFF_DH, q, zero), jnp.where(lane >= DIFF_DH, q, zero)], axis=0)
        s = _dot_nt(qq, _tile(k_ref, ki, tq))
        _put_logits(buf, s + slope * (col + ((ki - qi) * tq).astype(F32)), diag, None)

    def consume(buf, ki, diag):
        _flash_consume(buf, _tile(v_ref, ki, tq), m_ref, acc_ref, tq if diag else None)

    def finish(qi):
        o = _flash_result(acc_ref[...])
        d = o[0:tq] - lam * o[tq:2 * tq]
        o_ref[0, pl.ds(pl.multiple_of(qi * tq, tq), tq), :] = (
            _rms(d, g_ref[...]) * (1.0 - lam_init)).astype(o_ref.dtype)
        _flash_reset(m_ref, acc_ref)

    _flash_stream(n, (qt_ref, kt_ref, lt_ref), 0, produce, consume, finish, (sa_ref, ma_ref), (sb_ref, mb_ref),
                  mask_at_produce=False)


_SMEM = pl.BlockSpec(memory_space=pltpu.SMEM)


def _diff_attention(proj3, diff_lambda, subln, lam_init):
    b, s, _ = proj3.shape
    tq = min(TQ_DENSE, s)
    dv = 2 * DIFF_DH
    n, sched = _causal_schedule(s // tq)
    kern = functools.partial(_diff_attn_kernel, tq=tq, n=n, lam_init=lam_init)
    return pl.pallas_call(
        kern,
        grid=(b, DIFF_HEADS),
        in_specs=[
            _SMEM, _SMEM, _SMEM, _SMEM,
            pl.BlockSpec((4, DIFF_DH), lambda bi, h: (0, 0)),
            pl.BlockSpec((1, dv), lambda bi, h: (0, 0)),
            pl.BlockSpec((1, s, LANES), lambda bi, h: (bi, 0, PB_AQ + h)),
            pl.BlockSpec((1, s, LANES), lambda bi, h: (bi, 0, PB_AK + h)),
            pl.BlockSpec((1, s, LANES), lambda bi, h: (bi, 0, PB_AV + h)),
        ],
        out_specs=pl.BlockSpec((1, s, dv), lambda bi, h: (bi, 0, h)),
        out_shape=jax.ShapeDtypeStruct((b, s, DIFF_HEADS * dv), CDT),
        scratch_shapes=_flash_scratch(2 * tq, tq),
        compiler_params=_cparams(("parallel", "parallel")),
        name="diff_attention",
    )(*sched, jnp.asarray(_alibi_slopes(DIFF_HEADS)), diff_lambda, subln.reshape(1, dv), proj3, proj3, proj3)


def _mla_prep_kernel(cq_ref, ckv_ref, kr_ref, krs_ref, gq_ref, gkv_ref, wqm_ref, wqs_ref, wk_ref, wv_ref,
                     cosq_ref, sinq_ref, cosk_ref, sink_ref, q_ref, k_ref, v_ref):
    hq = _rms(cq_ref[0].astype(F32), gq_ref[...]).astype(CDT)
    qm = _dot(hq, wqm_ref[...])
    qs = _dot(hq, wqs_ref[...])
    cosq, sinq = cosq_ref[...], sinq_ref[...]
    hw = 2 * LANES
    for h in range(MLA_HEADS):
        sl = slice(h * hw, (h + 1) * hw)
        q_ref[0, :, sl] = (qm[:, sl] * cosq + qs[:, sl] * sinq).astype(q_ref.dtype)
    hkv = _rms(ckv_ref[0].astype(F32), gkv_ref[...]).astype(CDT)
    kn = _dot(hkv, wk_ref[...])
    v_ref[0] = _dot(hkv, wv_ref[...]).astype(v_ref.dtype)
    kpe = (kr_ref[0].astype(F32) * cosk_ref[...] + krs_ref[0].astype(F32) * sink_ref[...]).astype(k_ref.dtype)
    for h in range(MLA_HEADS):
        k_ref[0, :, h * hw:h * hw + LANES] = kn[:, h * LANES:(h + 1) * LANES].astype(k_ref.dtype)
        k_ref[0, :, h * hw + LANES:(h + 1) * hw] = kpe


def _mla_prep(proj3, gq, gkv, wqm, wqs, wk, wv, tabs):
    b, s, _ = proj3.shape
    tm = min(TM_PROJ, s)
    hw = 2 * LANES
    cosq, sinq, cosk, sink = tabs
    const = lambda shape: pl.BlockSpec(shape, lambda bi, i: (0,) * len(shape))
    return pl.pallas_call(
        _mla_prep_kernel,
        grid=(b, s // tm),
        in_specs=[
            pl.BlockSpec((1, tm, MLA_Q_LORA), lambda bi, i: (bi, i, PB_BCQ // 2)),
            pl.BlockSpec((1, tm, MLA_KV_LORA), lambda bi, i: (bi, i, PB_BCKV // 2)),
            pl.BlockSpec((1, tm, LANES), lambda bi, i: (bi, i, PB_BKR)),
            pl.BlockSpec((1, tm, LANES), lambda bi, i: (bi, i, PB_BKRS)),
            const((1, MLA_Q_LORA)), const((1, MLA_KV_LORA)),
            const((MLA_Q_LORA, MLA_HEADS * hw)), const((MLA_Q_LORA, MLA_HEADS * hw)),
            const((MLA_KV_LORA, MLA_HEADS * MLA_NOPE)), const((MLA_KV_LORA, MLA_HEADS * MLA_VDIM)),
            pl.BlockSpec((tm, hw), lambda bi, i: (i, 0)), pl.BlockSpec((tm, hw), lambda bi, i: (i, 0)),
            pl.BlockSpec((tm, LANES), lambda bi, i: (i, 0)), pl.BlockSpec((tm, LANES), lambda bi, i: (i, 0)),
        ],
        out_specs=[
            pl.BlockSpec((1, tm, MLA_HEADS * hw), lambda bi, i: (bi, i, 0)),
            pl.BlockSpec((1, tm, MLA_HEADS * hw), lambda bi, i: (bi, i, 0)),
            pl.BlockSpec((1, tm, MLA_HEADS * MLA_VDIM), lambda bi, i: (bi, i, 0)),
        ],
        out_shape=[
            jax.ShapeDtypeStruct((b, s, MLA_HEADS * hw), CDT),
            jax.ShapeDtypeStruct((b, s, MLA_HEADS * hw), CDT),
            jax.ShapeDtypeStruct((b, s, MLA_HEADS * MLA_VDIM), CDT),
        ],
        compiler_params=_cparams(("parallel", "parallel")),
        name="mla_prep",
    )(proj3, proj3, proj3, proj3, gq.reshape(1, -1), gkv.reshape(1, -1), wqm, wqs, wk, wv,
      cosq, sinq, cosk, sink)


def _plain_attn_kernel(qt_ref, kt_ref, lt_ref, q_ref, k_ref, v_ref, o_ref,
                       m_ref, acc_ref, sa_ref, sb_ref, ma_ref, mb_ref, *, tq, n, hp, dk, dv):
    _flash_reset(m_ref, acc_ref)
    heads = [(slice(h * tq, (h + 1) * tq), slice(h * dk, (h + 1) * dk), slice(h * dv, (h + 1) * dv))
             for h in range(hp)]

    def produce(buf, qi, ki, diag):
        q, k = _tile(q_ref, qi, tq), _tile(k_ref, ki, tq)
        for rows, kcols, _ in heads:
            _put_logits(buf, _dot_nt(q[:, kcols], k[:, kcols]), diag, None, rows)

    def consume(buf, ki, diag):
        v = _tile(v_ref, ki, tq)
        for rows, _, vcols in heads:
            _flash_consume(buf, v[:, vcols], m_ref, acc_ref, tq if diag else None, rows)

    def finish(qi):
        for rows, _, vcols in heads:
            o_ref[0, pl.ds(pl.multiple_of(qi * tq, tq), tq), vcols] = _flash_result(acc_ref[rows]).astype(o_ref.dtype)
        _flash_reset(m_ref, acc_ref)

    _flash_stream(n, (qt_ref, kt_ref, lt_ref), 0, produce, consume, finish, (sa_ref, ma_ref), (sb_ref, mb_ref),
                  mask_at_produce=False)


def _mla_attention(qc, kc, v):
    b, s, _ = qc.shape
    tq = min(TQ_DENSE, s)
    hw = 2 * LANES
    hp = MLA_HP
    n, sched = _causal_schedule(s // tq)
    return pl.pallas_call(
        functools.partial(_plain_attn_kernel, tq=tq, n=n, hp=hp, dk=hw, dv=MLA_VDIM),
        grid=(b, MLA_HEADS // hp),
        in_specs=[
            _SMEM, _SMEM, _SMEM,
            pl.BlockSpec((1, s, hp * hw), lambda bi, h: (bi, 0, h)),
            pl.BlockSpec((1, s, hp * hw), lambda bi, h: (bi, 0, h)),
            pl.BlockSpec((1, s, hp * MLA_VDIM), lambda bi, h: (bi, 0, h)),
        ],
        out_specs=pl.BlockSpec((1, s, hp * MLA_VDIM), lambda bi, h: (bi, 0, h)),
        out_shape=jax.ShapeDtypeStruct((b, s, MLA_HEADS * MLA_VDIM), CDT),
        scratch_shapes=_flash_scratch(hp * tq, tq),
        compiler_params=_cparams(("parallel", "parallel")),
        name="mla_attention",
    )(*sched, qc, kc, v)


def _fox_attn_kernel(qt_ref, kt_ref, lt_ref, c_ref, q_ref, k_ref, v_ref, o_ref,
                     m_ref, acc_ref, sa_ref, sb_ref, ma_ref, mb_ref, *, tq, n, hp):
    _flash_reset(m_ref, acc_ref)
    heads = [(slice(h * tq, (h + 1) * tq), slice(h * FOX_DH, (h + 1) * FOX_DH)) for h in range(hp)]

    def produce(buf, qi, ki, diag):
        q, k = _tile(q_ref, qi, tq), _tile(k_ref, ki, tq)
        for h, (rows, cols) in enumerate(heads):
            cbase = c_ref[0, h, pl.ds(qi, 1), :][:, 0:1]
            s = _dot_nt(q[:, cols], k[:, cols]) + LOG2E * (cbase - c_ref[0, h, pl.ds(ki, 1), :])
            _put_logits(buf, s, diag, None, rows)

    def consume(buf, ki, diag):
        v = _tile(v_ref, ki, tq)
        for rows, cols in heads:
            _flash_consume(buf, v[:, cols], m_ref, acc_ref, tq if diag else None, rows)

    def finish(qi):
        for rows, cols in heads:
            o_ref[0, pl.ds(pl.multiple_of(qi * tq, tq), tq), cols] = _flash_result(acc_ref[rows]).astype(o_ref.dtype)
        _flash_reset(m_ref, acc_ref)

    _flash_stream(n, (qt_ref, kt_ref, lt_ref), 0, produce, consume, finish, (sa_ref, ma_ref), (sb_ref, mb_ref),
                  mask_at_produce=False)


def _fox_attention(proj3, c4):
    b, s, _ = proj3.shape
    tq = min(TQ_DENSE, s)
    nk = s // tq
    hp = FOX_HP
    w = hp * FOX_DH
    n, sched = _causal_schedule(nk)
    return pl.pallas_call(
        functools.partial(_fox_attn_kernel, tq=tq, n=n, hp=hp),
        grid=(b, FOX_HEADS // hp),
        in_specs=[
            _SMEM, _SMEM, _SMEM,
            pl.BlockSpec((1, hp, nk, tq), lambda bi, h: (bi, h, 0, 0)),
            pl.BlockSpec((1, s, w), lambda bi, h: (bi, 0, PB_CQ // hp + h)),
            pl.BlockSpec((1, s, w), lambda bi, h: (bi, 0, PB_CK // hp + h)),
            pl.BlockSpec((1, s, w), lambda bi, h: (bi, 0, PB_CV // hp + h)),
        ],
        out_specs=pl.BlockSpec((1, s, w), lambda bi, h: (bi, 0, h)),
        out_shape=jax.ShapeDtypeStruct((b, s, FOX_HEADS * FOX_DH), CDT),
        scratch_shapes=_flash_scratch(hp * tq, tq),
        compiler_params=_cparams(("parallel", "parallel")),
        name="fox_attention",
    )(*sched, c4.reshape(b, FOX_HEADS, nk, tq), proj3, proj3, proj3)


def _nsa_compress_kernel(x_ref, w1a_ref, w1b_ref, pea_ref, peb_ref, w2_ref, o_ref):
    x = x_ref[0]
    n = x.shape[0]
    pa = _dot(x, w1a_ref[...])
    pb = _dot(x, w1b_ref[...])
    pe = _dot(pea_ref[...], w1a_ref[...]) + _dot(peb_ref[...], w1b_ref[...])
    hid = pa + pltpu.roll(pb, n - 1, axis=0) + pe[0:1]
    act = 0.5 * hid * (1.0 + jnp.tanh(math.sqrt(2.0 / math.pi) * (hid + 0.044715 * hid * hid * hid)))
    o_ref[0] = _dot(act.astype(CDT), w2_ref[...]).astype(o_ref.dtype)


def _nsa_compress(xc, w1a, w1b, pea, peb, w2):
    b, n, kdim = xc.shape
    hdim = w1a.shape[1]
    const = lambda shape: pl.BlockSpec(shape, lambda bi: (0,) * len(shape))
    return pl.pallas_call(
        _nsa_compress_kernel,
        grid=(b,),
        in_specs=[pl.BlockSpec((1, n, kdim), lambda bi: (bi, 0, 0)),
                  const((kdim, hdim)), const((kdim, hdim)), const((8, kdim)), const((8, kdim)),
                  const((hdim, w2.shape[1]))],
        out_specs=pl.BlockSpec((1, n, w2.shape[1]), lambda bi: (bi, 0, 0)),
        out_shape=jax.ShapeDtypeStruct((b, n, w2.shape[1]), CDT),
        compiler_params=_cparams(("parallel",)),
        name="nsa_compress",
    )(xc, w1a, w1b, pea, peb, w2)


def _nsa_cmp_kernel(slopes_ref, q_ref, kv_ref, oc_ref, sb_ref, used_ref, *, tq, n_topk):
    qi = pl.program_id(1)
    nblk = kv_ref.shape[1]
    q0 = qi * tq
    rowpos = q0 + lax.broadcasted_iota(jnp.int32, (tq, 1), 0)
    cmp_end = lax.broadcasted_iota(jnp.int32, (1, nblk), 1) * CMP_STRIDE + (CMP_LEN - 1)
    negmask = jnp.where(rowpos >= cmp_end, 0.0, NEG)
    end_rel = (cmp_end - q0).astype(F32)
    lane = lax.broadcasted_iota(jnp.int32, (tq, LANES), 1)
    low = lane < NSA_DH
    nn = lax.broadcasted_iota(jnp.int32, (NSA_DH, nblk), 1) * CMP_STRIDE
    jj = lax.broadcasted_iota(jnp.int32, (NSA_DH, nblk), 0) * SLC_LEN
    ovt = (jnp.maximum(jnp.minimum(nn + CMP_LEN, jj + SLC_LEN) - jnp.maximum(nn, jj), 0).astype(F32)
           * (1.0 / CMP_LEN)).astype(CDT)
    jt = lax.broadcasted_iota(jnp.int32, (NSA_DH, tq), 0).astype(F32)
    blk = ((q0 + lax.broadcasted_iota(jnp.int32, (1, tq), 1)) >> SLC_SHIFT).astype(F32)
    fixed = (jt == 0.0) | (jt == blk) | (jt == blk - 1.0)
    out_of_play = fixed | (jt > blk)
    row_ok = rowpos >= CMP_LEN - 1
    outs = []
    bias = []
    for g in range(NSA_GROUPS):
        kc = kv_ref[0, :, g * LANES:(g + 1) * LANES]
        vc = kv_ref[0, :, (NSA_GROUPS + g) * LANES:(NSA_GROUPS + g + 1) * LANES]
        psum = jnp.zeros((tq, nblk), F32)
        mine = low if g == 0 else jnp.logical_not(low)
        zero = jnp.zeros((tq, LANES), q_ref.dtype)
        qs = jnp.concatenate([jnp.where(mine, q_ref[0, :, j * LANES:(j + 1) * LANES], zero)
                              for j in range(NSA_HPG)], axis=0)
        s_all = _dot_nt(qs, kc)
        ps = []
        for j in range(NSA_HPG):
            s = s_all[j * tq:(j + 1) * tq] + slopes_ref[g * NSA_HPG + j] * end_rel + negmask
            e = jnp.exp2(s - jnp.max(s, axis=-1, keepdims=True))
            den = jnp.sum(e, axis=-1, keepdims=True)
            p = e * jnp.where(row_ok, 1.0 / den, 0.0)
            psum = psum + p
            ps.append(p.astype(CDT))
        o_all = _dot(jnp.concatenate(ps, axis=0), vc)
        outs.extend(o_all[j * tq:(j + 1) * tq] for j in range(NSA_HPG))
        hi = psum.astype(CDT)
        lo = (psum - hi.astype(F32)).astype(CDT)
        imp = _dot_nt(ovt, hi) + _dot_nt(ovt, lo)
        imp = jnp.where(out_of_play, -jnp.inf, imp)
        sbt = jnp.where(fixed, 0.0, NEG)
        for _ in range(n_topk - 3):
            mx = jnp.max(imp, axis=0, keepdims=True)
            idx = jnp.min(jnp.where(imp == mx, jt, float(LANES)), axis=0, keepdims=True)
            hit = jt == idx
            sbt = jnp.where(hit, 0.0, sbt)
            imp = jnp.where(hit, -jnp.inf, imp)
        bias.append(sbt)
    sb = jnp.concatenate([bias[1], bias[0]], axis=0).T
    sb_ref[0] = sb.astype(sb_ref.dtype)
    used = jnp.max(jnp.where(sb == 0.0, 1.0, 0.0), axis=0, keepdims=True)
    used_ref[0, 0] = jnp.broadcast_to(used, used_ref.shape[2:])
    for blk_i in range(NSA_HEADS // 2):
        oc_ref[0, :, blk_i * LANES:(blk_i + 1) * LANES] = jnp.where(
            low, outs[2 * blk_i], outs[2 * blk_i + 1]).astype(oc_ref.dtype)


def _nsa_cmp_select(proj3, kvc, n_topk):
    assert n_topk >= 3, "the three always-selected blocks must fit in the top-k budget"
    b, s, _ = proj3.shape
    tq = min(TQ_NSA, s)
    nblk = kvc.shape[1]
    return pl.pallas_call(
        functools.partial(_nsa_cmp_kernel, tq=tq, n_topk=n_topk),
        grid=(b, s // tq),
        in_specs=[
            pl.BlockSpec(memory_space=pltpu.SMEM),
            pl.BlockSpec((1, tq, 4 * LANES), lambda bi, qi: (bi, qi, PB_DQ // 4)),
            pl.BlockSpec((1, nblk, kvc.shape[2]), lambda bi, qi: (bi, 0, 0)),
        ],
        out_specs=[
            pl.BlockSpec((1, tq, NSA_HEADS * NSA_DH), lambda bi, qi: (bi, qi, 0)),
            pl.BlockSpec((1, tq, LANES), lambda bi, qi: (bi, qi, 0)),
            pl.BlockSpec((1, 1, 8, LANES), lambda bi, qi: (bi, qi, 0, 0)),
        ],
        out_shape=[jax.ShapeDtypeStruct((b, s, NSA_HEADS * NSA_DH), CDT),
                   jax.ShapeDtypeStruct((b, s, LANES), CDT),
                   jax.ShapeDtypeStruct((b, s // tq, 8, LANES), F32)],
        compiler_params=_cparams(("parallel", "parallel")),
        name="nsa_cmp_select",
    )(jnp.asarray(_alibi_slopes(NSA_HEADS)), proj3, kvc)


def _compact_heads(heads, mine, low):
    both = [jnp.where(mine, a, pltpu.roll(a, NSA_DH, axis=1)) for a in heads]
    out = [jnp.where(low, both[2 * jj], both[2 * jj + 1]) for jj in range(NSA_HPG // 2)]
    return jnp.concatenate(out, axis=1)


def _nsa_win_kernel(slopes_ref, q_ref, kp_ref, kc_ref, vp_ref, vc_ref, o_ref, *, tq):
    qi = pl.program_id(1)
    lane = lax.broadcasted_iota(jnp.int32, (tq, LANES), 1)
    low = lane < NSA_DH
    r = lax.broadcasted_iota(jnp.int32, (tq, tq), 0)
    c = lax.broadcasted_iota(jnp.int32, (tq, tq), 1)
    own = c <= r
    ndist = jnp.where(own, c - r, c - r - tq).astype(F32)
    own_f = jnp.where(own, 1.0, 0.0).astype(CDT)
    prev_pen = jnp.where(qi > 0, 0.0, NEG)
    q = q_ref[0]
    zero = jnp.zeros((tq, LANES), q.dtype)
    mine = (low, jnp.logical_not(low))
    qs = jnp.concatenate([jnp.where(mine[g], q[:, j * LANES:(j + 1) * LANES], zero)
                          for g in range(NSA_GROUPS) for j in range(NSA_HPG)], axis=0)
    s_own, s_prev = _dot_nt(qs, kc_ref[0]), _dot_nt(qs, kp_ref[0])
    ps = []
    for hd in range(NSA_HEADS):
        rows = slice(hd * tq, (hd + 1) * tq)
        s = jnp.where(own, s_own[rows], s_prev[rows] + prev_pen) + slopes_ref[hd] * ndist
        ps.append(jnp.exp2(s - jnp.max(s, axis=-1, keepdims=True)).astype(CDT))
    p = jnp.concatenate(ps, axis=0)
    p_own = p * jnp.tile(own_f, (NSA_HEADS, 1))
    o = _flash_result(_dot(p_own, _with_ones(vc_ref[0])) + _dot(p - p_own, _with_ones(vp_ref[0])))
    for g in range(NSA_GROUPS):
        heads = [o[(g * NSA_HPG + j) * tq:(g * NSA_HPG + j + 1) * tq] for j in range(NSA_HPG)]
        w = NSA_HPG * NSA_DH
        o_ref[0, :, g * w:(g + 1) * w] = _compact_heads(heads, mine[g], low).astype(o_ref.dtype)


def _nsa_window(proj3):
    b, s, _ = proj3.shape
    tq = WINDOW
    return pl.pallas_call(
        functools.partial(_nsa_win_kernel, tq=tq),
        grid=(b, s // tq),
        in_specs=[
            pl.BlockSpec(memory_space=pltpu.SMEM),
            pl.BlockSpec((1, tq, 4 * LANES), lambda bi, qi: (bi, qi, PB_DQ // 4)),
            pl.BlockSpec((1, tq, LANES), lambda bi, qi: (bi, jnp.maximum(qi - 1, 0), PB_WIN_K)),
            pl.BlockSpec((1, tq, LANES), lambda bi, qi: (bi, qi, PB_WIN_K)),
            pl.BlockSpec((1, tq, LANES), lambda bi, qi: (bi, jnp.maximum(qi - 1, 0), PB_WIN_V)),
            pl.BlockSpec((1, tq, LANES), lambda bi, qi: (bi, qi, PB_WIN_V)),
        ],
        out_specs=pl.BlockSpec((1, tq, NSA_HEADS * NSA_DH), lambda bi, qi: (bi, qi, 0)),
        out_shape=jax.ShapeDtypeStruct((b, s, NSA_HEADS * NSA_DH), CDT),
        compiler_params=_cparams(("parallel", "parallel")),
        name="nsa_window",
    )(jnp.asarray(_alibi_slopes(NSA_HEADS)), proj3, proj3, proj3, proj3, proj3)


def _nsa_sel_kernel(cnt_ref, qt_ref, kt_ref, lt_ref, slopes_ref, q_ref, sb_ref, k_ref, v_ref, oc_ref, ow_ref, gl_ref,
                    e_ref, o_ref, m_ref, acc_ref, sa_ref, sb2_ref, ma_ref, mb_ref, cm_ref, *, tq, rows_per_problem):
    g = pl.program_id(1)
    w = NSA_HPG * NSA_DH
    lane = lax.broadcasted_iota(jnp.int32, (tq, LANES), 1)
    low = lane < NSA_DH
    mine = (lane >> HALF_SHIFT) == g
    _flash_begin(m_ref, acc_ref, cm_ref, tq)
    col = lax.broadcasted_iota(jnp.int32, (1, tq), 1).astype(F32)
    jl = lane & (NSA_DH - 1)
    krow = lax.broadcasted_iota(jnp.int32, (tq, LANES), 0)

    def produce(buf, qi, ki, diag):
        q = _tile(q_ref, qi, tq)
        sb = _tile(sb_ref, qi, tq)
        qa = jnp.concatenate([jnp.where(mine, q[:, j * LANES:(j + 1) * LANES], sb) for j in range(NSA_HPG)], axis=0)
        k = _tile(k_ref, ki, tq)
        onehot = jnp.where(((ki * tq + krow) >> SLC_SHIFT) == jl, 1.0, 0.0).astype(k.dtype)
        s_all = _dot_nt(qa, jnp.where(mine, k, onehot))
        rel = ((ki - qi) * tq).astype(F32)
        for j in range(NSA_HPG):
            rows = slice(j * tq, (j + 1) * tq)
            _put_logits(buf, s_all[rows] + slopes_ref[g * NSA_HPG + j] * (col + rel), diag, cm_ref, rows)

    def consume(buf, ki, diag):
        _flash_consume(buf, _tile(v_ref, ki, tq), m_ref, acc_ref, tq if diag else None)

    def finish(qi):
        o = _flash_result(acc_ref[...])
        o_s = _compact_heads([o[j * tq:(j + 1) * tq] for j in range(NSA_HPG)], mine, low)
        gates = _split_dot(_sigmoid(_tile(gl_ref, qi, tq)), e_ref[0])
        y = (gates[:, 0:w] * _tile(oc_ref, qi, tq).astype(F32) + gates[:, w:2 * w] * o_s
             + gates[:, 2 * w:3 * w] * _tile(ow_ref, qi, tq).astype(F32))
        o_ref[0, pl.ds(pl.multiple_of(qi * tq, tq), tq), :] = y.astype(o_ref.dtype)
        _flash_reset(m_ref, acc_ref)

    prob = pl.program_id(0) * NSA_GROUPS + g
    _flash_stream(cnt_ref[prob], (qt_ref, kt_ref, lt_ref), prob * rows_per_problem, produce, consume, finish,
                  (sa_ref, ma_ref), (sb2_ref, mb_ref), mask_at_produce=True)


def _nsa_selected(proj3, sbias, used, o_c, o_w, small3, expand):
    b, s, _ = proj3.shape
    tq = min(TQ_NSA, s)
    nq = s // tq
    w = NSA_HPG * NSA_DH
    u = used[:, :, 0, :].reshape(b, nq, NSA_GROUPS, NSA_DH)[:, :, ::-1, :nq * (tq // SLC_LEN)]
    flags = (u.reshape(b, nq, NSA_GROUPS, nq, tq // SLC_LEN).max(axis=-1) > 0.0).astype(jnp.int32)
    flags = flags.transpose(0, 2, 1, 3)
    qt = jnp.arange(nq, dtype=jnp.int32)
    need = jnp.where(qt[None, :] < qt[:, None], flags, (qt[None, :] == qt[:, None]).astype(jnp.int32))
    need = need.reshape(b, NSA_GROUPS, nq * nq)
    cnt = need.sum(axis=-1).astype(jnp.int32)
    order = jnp.argsort(1 - need, axis=-1, stable=True).astype(jnp.int32)
    order = jnp.pad(order, ((0, 0), (0, 0), (0, 2)))
    rows = nq * nq + 2
    sched = (order // nq, order % nq, (order // nq == order % nq).astype(jnp.int32))
    return pl.pallas_call(
        functools.partial(_nsa_sel_kernel, tq=tq, rows_per_problem=rows),
        grid=(b, NSA_GROUPS),
        in_specs=[
            _SMEM, _SMEM, _SMEM, _SMEM, _SMEM,
            pl.BlockSpec((1, s, 4 * LANES), lambda bi, g: (bi, 0, PB_DQ // 4)),
            pl.BlockSpec((1, s, LANES), lambda bi, g: (bi, 0, 0)),
            pl.BlockSpec((1, s, LANES), lambda bi, g: (bi, 0, PB_SEL_K)),
            pl.BlockSpec((1, s, LANES), lambda bi, g: (bi, 0, PB_SEL_V)),
            pl.BlockSpec((1, s, w), lambda bi, g: (bi, 0, g)),
            pl.BlockSpec((1, s, w), lambda bi, g: (bi, 0, g)),
            pl.BlockSpec((1, s, LANES), lambda bi, g: (bi, 0, 0)),
            pl.BlockSpec((1, LANES, 3 * w), lambda bi, g: (g, 0, 0)),
        ],
        out_specs=pl.BlockSpec((1, s, w), lambda bi, g: (bi, 0, g)),
        out_shape=jax.ShapeDtypeStruct((b, s, NSA_HEADS * NSA_DH), CDT),
        scratch_shapes=_flash_scratch(NSA_HPG * tq, tq, mask_scratch=True),
        compiler_params=_cparams(("parallel", "parallel")),
        name="nsa_selected",
    )(cnt.reshape(-1), *[t.reshape(-1) for t in sched], jnp.asarray(_alibi_slopes(NSA_HEADS)),
      proj3, sbias, proj3, proj3, o_c, o_w, small3, expand)


def _merge_kernel(ya_ref, yb_ref, yc_ref, yd_ref, ga_ref, gb_ref, gc_ref, gd_ref, wb_ref, wo_ref, x_ref, o_ref):
    merged = None
    for n, (y_ref, g_ref) in enumerate(((ya_ref, ga_ref), (yb_ref, gb_ref), (yc_ref, gc_ref), (yd_ref, gd_ref))):
        t = _sigmoid(g_ref[...].astype(F32)) * _dot(y_ref[...], wb_ref[n])
        merged = t if merged is None else merged + t
    o_ref[...] = x_ref[...] + _dot(merged.astype(CDT), wo_ref[...])


def _merge(ys, proj2, wb, wo, x2):
    t, d = x2.shape
    tm = min(TM_ROWS, t)
    gate_blk = PB_GATE * LANES // d
    yspec = pl.BlockSpec((tm, BRANCH_WIDTH), lambda i: (i, 0))
    gspecs = [pl.BlockSpec((tm, d), functools.partial(lambda i, n: (i, gate_blk + n), n=n)) for n in range(N_BRANCH)]
    return pl.pallas_call(
        _merge_kernel,
        grid=(t // tm,),
        in_specs=[yspec] * N_BRANCH + gspecs + [
            pl.BlockSpec((N_BRANCH, BRANCH_WIDTH, d), lambda i: (0, 0, 0)),
            pl.BlockSpec((d, d), lambda i: (0, 0)),
            pl.BlockSpec((tm, d), lambda i: (i, 0)),
        ],
        out_specs=pl.BlockSpec((tm, d), lambda i: (i, 0)),
        out_shape=jax.ShapeDtypeStruct((t, d), F32),
        compiler_params=_cparams(("parallel",)),
        name="merge",
    )(*ys, proj2, proj2, proj2, proj2, wb, wo, x2)


HALO = 16


def _ffn_kernel(x_ref, xh_ref, g_ref, wu_ref, cw_ref, cb_ref, wd_ref, gf_ref, o_ref, he_ref, u_ref, act_ref,
                *, tm, fc, final):
    i = pl.program_id(1)
    x = x_ref[0]
    g = g_ref[...]
    xh = xh_ref[0] * (i > 0).astype(F32)
    he_ref[0:HALO] = _rms(xh, g).astype(CDT)
    he_ref[HALO:HALO + tm] = _rms(x, g).astype(CDT)
    he = he_ref[...]
    for c in range(D_FF // fc):
        outs = []
        for half in range(2):
            ub = u_ref.at[c % 2, half]
            lo = half * D_FF + c * fc
            ub[...] = _dot(he, wu_ref[:, lo:lo + fc])
            conv = cb_ref[:, lo:lo + fc]
            for kk in range(CONV_WIDTH):
                off = HALO - (CONV_WIDTH - 1) + kk
                conv = conv + cw_ref[kk:kk + 1, lo:lo + fc] * ub[off:off + tm, :]
            outs.append(conv)
        a, gg = outs
        act_ref[:, c * fc:(c + 1) * fc] = (a * _sigmoid(a) * gg).astype(CDT)
    y = x + _dot(act_ref[...], wd_ref[...])
    if final:
        y = _rms(y, gf_ref[...])
    o_ref[0] = y


def _ffn(x3, g, wu, cw, cb, wd, gf, final):
    b, s, d = x3.shape
    tm = min(TM_ROWS, s)
    fc = FFN_CHUNK
    assert D_FF % fc == 0
    const = lambda shape: pl.BlockSpec(shape, lambda bi, i: (0,) * len(shape), pipeline_mode=pl.Buffered(1))
    return pl.pallas_call(
        functools.partial(_ffn_kernel, tm=tm, fc=fc, final=final),
        grid=(b, s // tm),
        in_specs=[
            pl.BlockSpec((1, tm, d), lambda bi, i: (bi, i, 0)),
            pl.BlockSpec((1, HALO, d), lambda bi, i: (bi, jnp.maximum(i * (tm // HALO) - 1, 0), 0)),
            const((1, d)), const((d, 2 * D_FF)), const((CONV_WIDTH, 2 * D_FF)), const((1, 2 * D_FF)),
            const((D_FF, d)), const((1, d)),
        ],
        out_specs=pl.BlockSpec((1, tm, d), lambda bi, i: (bi, i, 0)),
        out_shape=jax.ShapeDtypeStruct((b, s, d), F32),
        scratch_shapes=[pltpu.VMEM((tm + HALO, d), CDT), pltpu.VMEM((2, 2, tm + HALO, fc), F32),
                        pltpu.VMEM((tm, D_FF), CDT)],
        compiler_params=_cparams(("parallel", "arbitrary")),
        name="conv_glu_mlp",
    )(x3, x3, g.reshape(1, d), wu, cw, cb.reshape(1, -1), wd, gf.reshape(1, d))


def _prep_w_in(w):
    widths = (512, 512, 512, MLA_Q_LORA, MLA_KV_LORA, MLA_ROPE, 512, 512, 512, FOX_HEADS,
              512, 768, 3 * NSA_HEADS, N_BRANCH * D_MODEL)
    offs = np.cumsum((0,) + widths)
    (a_q, a_k, a_v, b_cq, b_ckv, b_kr, c_q, c_k, c_v, c_f, d_q, d_kv, d_g, gate) = [
        w[:, offs[i]:offs[i + 1]] for i in range(len(widths))]
    d = w.shape[0]
    d_q = d_q.reshape(d, NSA_GROUPS, NSA_HPG, NSA_DH).transpose(0, 2, 1, 3).reshape(d, 512)
    half = MLA_ROPE // 2
    kr_swap = jnp.concatenate([-b_kr[:, half:], b_kr[:, :half]], axis=1)
    z64 = jnp.zeros((d, LANES - MLA_ROPE), w.dtype)
    pieces = [a_q * (LOG2E * DIFF_DH ** -0.5), a_k, a_v,
              c_q * (LOG2E * FOX_DH ** -0.5), c_k, c_v,
              d_q * (LOG2E * NSA_DH ** -0.5), d_kv,
              b_cq, b_ckv, b_kr, z64, kr_swap, z64,
              gate]
    big = jnp.concatenate([p.astype(CDT) for p in pieces], axis=1)
    small = jnp.concatenate([c_f, d_g, jnp.zeros((d, LANES - FOX_HEADS - 3 * NSA_HEADS), w.dtype)], axis=1)
    return big, small.astype(CDT)


def _prep_mla(w_uq, w_ukv):
    r = w_uq.shape[0]
    hw = 2 * LANES
    half = MLA_ROPE // 2
    scale = LOG2E * (MLA_NOPE + MLA_ROPE) ** -0.5
    wq = (w_uq * scale).reshape(r, MLA_HEADS, MLA_NOPE + MLA_ROPE)
    nope, t1, t2 = wq[..., :MLA_NOPE], wq[..., MLA_NOPE:MLA_NOPE + half], wq[..., MLA_NOPE + half:]
    zpad = jnp.zeros((r, MLA_HEADS, hw - MLA_NOPE - MLA_ROPE), w_uq.dtype)
    wqm = jnp.concatenate([nope, t1, t2, zpad], axis=-1).reshape(r, MLA_HEADS * hw)
    wqs = jnp.concatenate([jnp.zeros_like(nope), -t2, t1, zpad], axis=-1).reshape(r, MLA_HEADS * hw)
    wkv = w_ukv.reshape(w_ukv.shape[0], MLA_HEADS, MLA_NOPE + MLA_VDIM)
    wk = wkv[..., :MLA_NOPE].reshape(-1, MLA_HEADS * MLA_NOPE)
    wv = wkv[..., MLA_NOPE:].reshape(-1, MLA_HEADS * MLA_VDIM)
    return wqm.astype(CDT), wqs.astype(CDT), wk.astype(CDT), wv.astype(CDT)


def _rope_tables(s):
    half = MLA_ROPE // 2
    inv_freq = ROPE_THETA ** (-jnp.arange(0, MLA_ROPE, 2, dtype=F32) / MLA_ROPE)
    ang = jnp.arange(s, dtype=F32)[:, None] * inv_freq[None, :]
    cos, sin = jnp.cos(ang), jnp.sin(ang)
    z = jnp.zeros((s, LANES - MLA_ROPE), F32)
    cosk = jnp.concatenate([cos, cos, z], axis=1)
    sink = jnp.concatenate([sin, sin, z], axis=1)
    cosq = jnp.concatenate([jnp.ones((s, MLA_NOPE), F32), cosk], axis=1)
    sinq = jnp.concatenate([jnp.zeros((s, MLA_NOPE), F32), sink], axis=1)
    return cosq, sinq, cosk, sink


def _prep_compress(pe, w1, w2):
    eye2 = jnp.eye(2, dtype=F32)
    w1r = w1.reshape(2, CMP_LEN, NSA_DH, CMP_HIDDEN).astype(CDT)
    same = np.eye(2, dtype=bool)
    diag_kg = jnp.asarray(same[:, None, :, None] & same[None, :, None, :])

    def expand(wpart):
        src = wpart.transpose(1, 0, 2, 3)[:, :, None, :, None, None, :]
        t = jnp.where(diag_kg[None, :, :, None, :, :, None], src, jnp.zeros((), CDT))
        return t.reshape(CMP_STRIDE * 4 * NSA_DH, 4 * CMP_HIDDEN)

    w1a, w1b = expand(w1r[:, :CMP_STRIDE]), expand(w1r[:, CMP_STRIDE:])

    def pe_row(p):
        t = jnp.broadcast_to(p.transpose(1, 0, 2)[:, :, None, :], (CMP_STRIDE, 2, NSA_GROUPS, NSA_DH))
        return jnp.pad(t.reshape(1, -1), ((0, 7), (0, 0)))

    pea, peb = pe_row(pe[:, :CMP_STRIDE]), pe_row(pe[:, CMP_STRIDE:])
    w2b = jnp.einsum('khd,kK,gG,u->kghKGud', w2, eye2, eye2, jnp.ones((2,), F32))
    w2b = w2b.reshape(4 * CMP_HIDDEN, 4 * 2 * NSA_DH)
    return w1a.astype(CDT), w1b.astype(CDT), pea.astype(CDT), peb.astype(CDT), w2b.astype(CDT)


def _gate_expand():
    e = np.zeros((NSA_GROUPS, LANES, 3, NSA_HPG, NSA_DH), np.float32)
    for g in range(NSA_GROUPS):
        for j in range(NSA_HPG):
            for br in range(3):
                e[g, SMALL_G + (g * NSA_HPG + j) * 3 + br, br, j, :] = 1.0
    return jnp.asarray(e.reshape(NSA_GROUPS, LANES, 3 * NSA_HPG * NSA_DH)).astype(CDT)


def _token_mixers(x3, l, norm_mix, w_in, diff_lambda, diff_subln, mla_norm_q, mla_w_uq, mla_norm_kv, mla_w_ukv,
                  fox_b_f, nsa_cmp_pe, nsa_cmp_w1, nsa_cmp_w2, w_branch, w_out, rope_tabs):
    b, s, d = x3.shape
    t = b * s
    x2 = x3.reshape(t, d)
    proj, small = _in_proj(x2, norm_mix, *_prep_w_in(w_in))
    proj3 = proj.reshape(b, s, N_PROJ)
    small3 = small.reshape(b, s, LANES)

    lam_init = 0.8 - 0.6 * math.exp(-0.3 * l)
    y_a = _diff_attention(proj3, diff_lambda, diff_subln, lam_init)

    wqm, wqs, wk, wv = _prep_mla(mla_w_uq, mla_w_ukv)
    qc, kc, vv = _mla_prep(proj3, mla_norm_q, mla_norm_kv, wqm, wqs, wk, wv, rope_tabs)
    y_b = _mla_attention(qc, kc, vv)

    cf_rows = small3[:, :, SMALL_F:SMALL_F + FOX_HEADS].transpose(0, 2, 1).reshape(b * FOX_HEADS, s)
    bias_rows = jnp.tile(fox_b_f.astype(F32), b).reshape(b * FOX_HEADS, 1)
    c4 = _fox_cumsum(cf_rows, bias_rows)
    y_c = _fox_attention(proj3, c4)

    w1a, w1b, pea, peb, w2b = _prep_compress(nsa_cmp_pe, nsa_cmp_w1, nsa_cmp_w2)
    xc = proj3[:, :, PB_CMP_K * LANES:(PB_CMP_V + 1) * LANES].reshape(b, s // CMP_STRIDE, CMP_STRIDE * 2 * LANES)
    kvc = _nsa_compress(xc, w1a, w1b, pea, peb, w2b)
    n_topk = min(SLC_TOPK, s // SLC_LEN)
    o_c, sbias, used = _nsa_cmp_select(proj3, kvc, n_topk)
    o_w = _nsa_window(proj3)
    y_d = _nsa_selected(proj3, sbias, used, o_c, o_w, small3, _gate_expand())

    ys = [y.reshape(t, BRANCH_WIDTH) for y in (y_a, y_b, y_c, y_d)]
    return _merge(ys, proj, w_branch.astype(CDT), w_out.astype(CDT), x2).reshape(b, s, d)


def kernel(x, norm_mix, w_in, diff_lambda, diff_subln, mla_norm_q, mla_w_uq, mla_norm_kv, mla_w_ukv, fox_b_f,
           nsa_cmp_pe, nsa_cmp_w1, nsa_cmp_w2, w_branch, w_out, norm_ffn, w_up, conv_w, conv_b, w_down, norm_final):
    depth = w_in.shape[0]
    s = x.shape[1]
    rope_tabs = _rope_tables(s)
    for l in range(depth):
        x = _token_mixers(x, l, norm_mix[l], w_in[l], diff_lambda[l], diff_subln[l], mla_norm_q[l], mla_w_uq[l],
                          mla_norm_kv[l], mla_w_ukv[l], fox_b_f[l], nsa_cmp_pe[l], nsa_cmp_w1[l], nsa_cmp_w2[l],
                          w_branch[l], w_out[l], rope_tabs)
        x = _ffn(x, norm_ffn[l], w_up[l].astype(CDT), conv_w[l], conv_b[l], w_down[l].astype(CDT), norm_final,
                 final=(l == depth - 1))
    return x
```

```python
import functools
import math

import numpy as np
import jax
import jax.numpy as jnp
from jax import lax
from jax.experimental import pallas as pl
from jax.experimental.pallas import tpu as pltpu

F32 = jnp.float32
CDT = jnp.bfloat16

NEG = -1e30
NEG_INF = -1e30
BIG = 1e9
NORM_EPS = 1e-6
LOG2E = 1.4426950408889634
LANES = 128

D_MODEL = 1024
DIFF_HEADS, DIFF_DH = 4, 64
MLA_HEADS, MLA_NOPE, MLA_ROPE, MLA_VDIM = 4, 128, 64, 128
MLA_Q_LORA, MLA_KV_LORA = 256, 256
ROPE_THETA = 10000.0
FOX_HEADS, FOX_DH = 4, 128
NSA_HEADS, NSA_GROUPS, NSA_DH = 8, 2, 64
NSA_HPG = NSA_HEADS // NSA_GROUPS
CMP_STRIDE = 16
CMP_LEN = 2 * CMP_STRIDE
CMP_HIDDEN = 128
SLC_LEN = 64
SLC_SHIFT = 6
HALF_SHIFT = 6
SLC_TOPK = 8
WINDOW = 256
N_BRANCH = 4
BRANCH_WIDTH = 512
D_FF = 2816
CONV_WIDTH = 3

PB_AQ, PB_AK, PB_AV = 0, 4, 8
PB_CQ, PB_CK, PB_CV = 12, 16, 20
PB_DQ = 24
PB_CMP_K, PB_CMP_V, PB_SEL_K, PB_SEL_V, PB_WIN_K, PB_WIN_V = 28, 29, 30, 31, 32, 33
PB_BCQ, PB_BCKV, PB_BKR, PB_BKRS = 34, 36, 38, 39
PB_GATE = 40
N_PROJ = 72 * LANES
SMALL_F, SMALL_G = 0, 4

VMEM_LIMIT = 56 * 1024 * 1024
MXU_TILE = 256
TQ_DENSE = 512
TQ_NSA = WINDOW
TM_PROJ, TN_PROJ = 1024, 9 * MXU_TILE
TM_ROWS = 512
FFN_CHUNK = MXU_TILE
FOX_HP = 2
MLA_HP = 2


def _cparams(sem):
    return pltpu.CompilerParams(dimension_semantics=sem, vmem_limit_bytes=VMEM_LIMIT)


def _rms(xf, g):
    return xf * lax.rsqrt(jnp.mean(xf * xf, axis=-1, keepdims=True) + NORM_EPS) * g


def _sigmoid(x):
    return 0.5 * jnp.tanh(0.5 * x) + 0.5


def _dot(a, b):
    return jnp.dot(a, b, preferred_element_type=F32)


def _dot_nt(a, b):
    return lax.dot_general(a, b, (((1,), (1,)), ((), ())), preferred_element_type=F32)


def _split_dot(a, b):
    hi = a.astype(CDT)
    lo = (a - hi.astype(F32)).astype(CDT)
    return _dot(hi, b) + _dot(lo, b)


def _alibi_slopes(n):
    return (LOG2E * np.exp2(-8.0 * np.arange(1, n + 1) / n)).astype(np.float32)


def _inproj_kernel(x_ref, g_ref, w_ref, ws_ref, o_ref, os_ref, h_ref):
    @pl.when(pl.program_id(1) == 0)
    def _():
        h = _rms(x_ref[...], g_ref[...]).astype(CDT)
        h_ref[...] = h
        os_ref[...] = _dot(h, ws_ref[...])

    o_ref[...] = _dot(h_ref[...], w_ref[...]).astype(o_ref.dtype)


def _in_proj(x2, g, w, ws):
    t, d = x2.shape
    n = w.shape[1]
    tm = min(TM_PROJ, t)
    tn = TN_PROJ
    assert n % tn == 0
    return pl.pallas_call(
        _inproj_kernel,
        grid=(t // tm, n // tn),
        in_specs=[
            pl.BlockSpec((tm, d), lambda i, j: (i, 0)),
            pl.BlockSpec((1, d), lambda i, j: (0, 0)),
            pl.BlockSpec((d, tn), lambda i, j: (0, j)),
            pl.BlockSpec((d, LANES), lambda i, j: (0, 0)),
        ],
        out_specs=[
            pl.BlockSpec((tm, tn), lambda i, j: (i, j)),
            pl.BlockSpec((tm, LANES), lambda i, j: (i, 0)),
        ],
        out_shape=[jax.ShapeDtypeStruct((t, n), CDT), jax.ShapeDtypeStruct((t, LANES), F32)],
        scratch_shapes=[pltpu.VMEM((tm, d), CDT)],
        compiler_params=_cparams(("parallel", "arbitrary")),
        name="in_proj",
    )(x2, g.reshape(1, d), w, ws)


def _fox_cumsum_kernel(cf_ref, bf_ref, o_ref):
    rows, s = cf_ref.shape
    lane = lax.broadcasted_iota(jnp.int32, (rows, LANES), 1)
    carry = jnp.zeros((rows, 1), F32)
    for c in range(s // LANES):
        z = cf_ref[:, c * LANES:(c + 1) * LANES] + bf_ref[...]
        xs = jnp.minimum(z, 0.0) - jnp.log1p(jnp.exp(-jnp.abs(z)))
        d = 1
        while d < LANES:
            xs = xs + jnp.where(lane >= d, pltpu.roll(xs, d, axis=1), 0.0)
            d *= 2
        xs = xs + carry
        o_ref[:, c * LANES:(c + 1) * LANES] = xs
        carry = xs[:, LANES - 1:LANES]


def _fox_cumsum(cf_rows, bias_rows):
    return pl.pallas_call(
        _fox_cumsum_kernel,
        out_shape=jax.ShapeDtypeStruct(cf_rows.shape, F32),
        name="fox_cumsum",
    )(cf_rows, bias_rows)


def _flash_scratch(rows, tk, mask_scratch=False):
    return [pltpu.VMEM((rows, LANES), F32), pltpu.VMEM((rows, 2 * LANES), F32),
            pltpu.VMEM((rows, tk), F32), pltpu.VMEM((rows, tk), F32),
            pltpu.VMEM((rows, LANES), F32), pltpu.VMEM((rows, LANES), F32)
            ] + ([pltpu.VMEM((rows, tk), F32)] if mask_scratch else [])


def _flash_reset(m_ref, acc_ref):
    m_ref[...] = jnp.full(m_ref.shape, NEG, F32)
    acc_ref[...] = jnp.zeros(acc_ref.shape, F32)


def _flash_begin(m_ref, acc_ref, cm_ref, tq):
    _flash_reset(m_ref, acc_ref)
    cm_ref[...] = _causal_bias(cm_ref.shape[0], cm_ref.shape[1], tq)


def _row_max(s):
    return jnp.broadcast_to(jnp.max(s, axis=-1, keepdims=True), (s.shape[0], LANES))


def _causal_bias(rows, tk, tq):
    r = lax.broadcasted_iota(jnp.int32, (rows, tk), 0) & (tq - 1)
    c = lax.broadcasted_iota(jnp.int32, (rows, tk), 1)
    return jnp.where(c <= r, 0.0, NEG)


def _put_logits(buf, s, rows=slice(None), diag=False, cm_ref=None):
    if diag is True:
        s = s + cm_ref[rows]
    elif diag is not False:
        s = s + diag.astype(F32) * cm_ref[rows]
    buf[0][rows] = s
    buf[1][rows] = _row_max(s)


def _with_ones(v):
    return jnp.concatenate([v, jnp.ones((v.shape[0], LANES), v.dtype)], axis=1)


def _flash_consume(buf, v, m_ref, acc_ref, mask_tq=None, rows=slice(None)):
    s = buf[0][rows]
    m_cur = buf[1][rows]
    if mask_tq is not None:
        s = s + _causal_bias(s.shape[0], s.shape[1], mask_tq)
        m_cur = _row_max(s)
    m_old = m_ref[rows]
    m_new = jnp.maximum(m_old, m_cur)
    alpha = jnp.exp2(m_old - m_new)
    p = jnp.exp2(s - jnp.tile(m_new, (1, s.shape[1] // LANES))).astype(CDT)
    acc_ref[rows] = jnp.tile(alpha, (1, 2)) * acc_ref[rows] + _dot(p, _with_ones(v))
    m_ref[rows] = m_new


def _flash_result(acc):
    return acc[:, :LANES] / acc[:, LANES:]


def _causal_schedule(nq):
    ent = [(qi, ki, int(ki == qi)) for qi in range(nq) for ki in range(qi + 1)]
    n = len(ent)
    a = np.asarray(ent + [ent[-1]] * 2, np.int32)
    return n, tuple(jnp.asarray(a[:, i]) for i in range(3))


def _flash_stream(n, sched, base, produce, consume, finish, buf_a, buf_b, mask_at_produce):
    qt, kt, lt = sched

    def step(cur, nxt, t, diag, next_diag):
        if nxt is not None:
            produce(nxt, qt[base + t + 1], kt[base + t + 1], next_diag if mask_at_produce else False)
        consume(cur, kt[base + t], diag and not mask_at_produce)
        if diag:
            finish(qt[base + t])

    produce(buf_a, qt[base], kt[base], mask_at_produce)

    def pair(j, c):
        t = 2 * j
        l0, l1 = lt[base + t], lt[base + t + 1]
        for d0 in (False, True):
            for d1 in (False, True):
                @pl.when(((l0 != 0) == d0) & ((l1 != 0) == d1))
                def _():
                    step(buf_a, buf_b, t, d0, d1)
                    step(buf_b, buf_a, t + 1, d1, lt[base + t + 2])
        return c

    lax.fori_loop(0, n // 2, pair, 0)

    def tail():
        step(buf_a, None, n - 1, True, None)

    if isinstance(n, int):
        if n % 2 == 1:
            tail()
    else:
        pl.when(n % 2 == 1)(tail)


def _tile(ref, i, t):
    return ref[0, pl.ds(pl.multiple_of(i * t, t), t), :]


def _diff_attn_kernel(qt_ref, kt_ref, lt_ref, slopes_ref, lam_ref, g_ref, q_ref, k_ref, v_ref, o_ref,
                      m_ref, acc_ref, sa_ref, sb_ref, ma_ref, mb_ref, *, tq, n, lam_init):
    slope = slopes_ref[pl.program_id(1)]
    _flash_reset(m_ref, acc_ref)
    col = lax.broadcasted_iota(jnp.int32, (1, tq), 1).astype(F32)
    lane = lax.broadcasted_iota(jnp.int32, (tq, LANES), 1)
    lf = lam_ref[...]
    lam = (jnp.exp(jnp.sum(lf[0:1] * lf[1:2], axis=-1, keepdims=True))
           - jnp.exp(jnp.sum(lf[2:3] * lf[3:4], axis=-1, keepdims=True)) + lam_init)

    def produce(buf, qi, ki, diag):
        q = _tile(q_ref, qi, tq)
        zero = jnp.zeros_like(q)
        qq = jnp.concatenate([jnp.where(lane < DIFF_DH, q, zero), jnp.where(lane >= DIFF_DH, q, zero)], axis=0)
        s = _dot_nt(qq, _tile(k_ref, ki, tq))
        _put_logits(buf, s + slope * (col + ((ki - qi) * tq).astype(F32)))

    def consume(buf, ki, diag):
        _flash_consume(buf, _tile(v_ref, ki, tq), m_ref, acc_ref, tq if diag else None)

    def finish(qi):
        o = _flash_result(acc_ref[...])
        d = o[0:tq] - lam * o[tq:2 * tq]
        o_ref[0, pl.ds(pl.multiple_of(qi * tq, tq), tq), :] = (
            _rms(d, g_ref[...]) * (1.0 - lam_init)).astype(o_ref.dtype)
        _flash_reset(m_ref, acc_ref)

    _flash_stream(n, (qt_ref, kt_ref, lt_ref), 0, produce, consume, finish, (sa_ref, ma_ref), (sb_ref, mb_ref),
                  mask_at_produce=False)


_SMEM = pl.BlockSpec(memory_space=pltpu.SMEM)


def _diff_attention(proj3, diff_lambda, subln, lam_init):
    b, s, _ = proj3.shape
    tq = min(TQ_DENSE, s)
    dv = 2 * DIFF_DH
    n, sched = _causal_schedule(s // tq)
    kern = functools.partial(_diff_attn_kernel, tq=tq, n=n, lam_init=lam_init)
    return pl.pallas_call(
        kern,
        grid=(b, DIFF_HEADS),
        in_specs=[
            _SMEM, _SMEM, _SMEM, _SMEM,
            pl.BlockSpec((4, DIFF_DH), lambda bi, h: (0, 0)),
            pl.BlockSpec((1, dv), lambda bi, h: (0, 0)),
            pl.BlockSpec((1, s, LANES), lambda bi, h: (bi, 0, PB_AQ + h)),
            pl.BlockSpec((1, s, LANES), lambda bi, h: (bi, 0, PB_AK + h)),
            pl.BlockSpec((1, s, LANES), lambda bi, h: (bi, 0, PB_AV + h)),
        ],
        out_specs=pl.BlockSpec((1, s, dv), lambda bi, h: (bi, 0, h)),
        out_shape=jax.ShapeDtypeStruct((b, s, DIFF_HEADS * dv), CDT),
        scratch_shapes=_flash_scratch(2 * tq, tq),
        compiler_params=_cparams(("parallel", "parallel")),
        name="diff_attention",
    )(*sched, jnp.asarray(_alibi_slopes(DIFF_HEADS)), diff_lambda, subln.reshape(1, dv), proj3, proj3, proj3)


def _mla_prep_kernel(cq_ref, ckv_ref, kr_ref, krs_ref, gq_ref, gkv_ref, wqm_ref, wqs_ref, wk_ref, wv_ref,
                     cosq_ref, sinq_ref, cosk_ref, sink_ref, q_ref, k_ref, v_ref):
    hq = _rms(cq_ref[0].astype(F32), gq_ref[...]).astype(CDT)
    qm = _dot(hq, wqm_ref[...])
    qs = _dot(hq, wqs_ref[...])
    cosq, sinq = cosq_ref[...], sinq_ref[...]
    hw = 2 * LANES
    for h in range(MLA_HEADS):
        sl = slice(h * hw, (h + 1) * hw)
        q_ref[0, :, sl] = (qm[:, sl] * cosq + qs[:, sl] * sinq).astype(q_ref.dtype)
    hkv = _rms(ckv_ref[0].astype(F32), gkv_ref[...]).astype(CDT)
    kn = _dot(hkv, wk_ref[...])
    v_ref[0] = _dot(hkv, wv_ref[...]).astype(v_ref.dtype)
    kpe = (kr_ref[0].astype(F32) * cosk_ref[...] + krs_ref[0].astype(F32) * sink_ref[...]).astype(k_ref.dtype)
    for h in range(MLA_HEADS):
        k_ref[0, :, h * hw:h * hw + LANES] = kn[:, h * LANES:(h + 1) * LANES].astype(k_ref.dtype)
        k_ref[0, :, h * hw + LANES:(h + 1) * hw] = kpe


def _mla_prep(proj3, gq, gkv, wqm, wqs, wk, wv, tabs):
    b, s, _ = proj3.shape
    tm = min(TM_PROJ, s)
    hw = 2 * LANES
    cosq, sinq, cosk, sink = tabs
    const = lambda shape: pl.BlockSpec(shape, lambda bi, i: (0,) * len(shape))
    return pl.pallas_call(
        _mla_prep_kernel,
        grid=(b, s // tm),
        in_specs=[
            pl.BlockSpec((1, tm, MLA_Q_LORA), lambda bi, i: (bi, i, PB_BCQ // 2)),
            pl.BlockSpec((1, tm, MLA_KV_LORA), lambda bi, i: (bi, i, PB_BCKV // 2)),
            pl.BlockSpec((1, tm, LANES), lambda bi, i: (bi, i, PB_BKR)),
            pl.BlockSpec((1, tm, LANES), lambda bi, i: (bi, i, PB_BKRS)),
            const((1, MLA_Q_LORA)), const((1, MLA_KV_LORA)),
            const((MLA_Q_LORA, MLA_HEADS * hw)), const((MLA_Q_LORA, MLA_HEADS * hw)),
            const((MLA_KV_LORA, MLA_HEADS * MLA_NOPE)), const((MLA_KV_LORA, MLA_HEADS * MLA_VDIM)),
            pl.BlockSpec((tm, hw), lambda bi, i: (i, 0)), pl.BlockSpec((tm, hw), lambda bi, i: (i, 0)),
            pl.BlockSpec((tm, LANES), lambda bi, i: (i, 0)), pl.BlockSpec((tm, LANES), lambda bi, i: (i, 0)),
        ],
        out_specs=[
            pl.BlockSpec((1, tm, MLA_HEADS * hw), lambda bi, i: (bi, i, 0)),
            pl.BlockSpec((1, tm, MLA_HEADS * hw), lambda bi, i: (bi, i, 0)),
            pl.BlockSpec((1, tm, MLA_HEADS * MLA_VDIM), lambda bi, i: (bi, i, 0)),
        ],
        out_shape=[
            jax.ShapeDtypeStruct((b, s, MLA_HEADS * hw), CDT),
            jax.ShapeDtypeStruct((b, s, MLA_HEADS * hw), CDT),
            jax.ShapeDtypeStruct((b, s, MLA_HEADS * MLA_VDIM), CDT),
        ],
        compiler_params=_cparams(("parallel", "parallel")),
        name="mla_prep",
    )(proj3, proj3, proj3, proj3, gq.reshape(1, -1), gkv.reshape(1, -1), wqm, wqs, wk, wv,
      cosq, sinq, cosk, sink)


def _plain_attn_kernel(qt_ref, kt_ref, lt_ref, q_ref, k_ref, v_ref, o_ref,
                       m_ref, acc_ref, sa_ref, sb_ref, ma_ref, mb_ref, *, tq, n, hp, dk, dv):
    _flash_reset(m_ref, acc_ref)
    heads = [(slice(h * tq, (h + 1) * tq), slice(h * dk, (h + 1) * dk), slice(h * dv, (h + 1) * dv))
             for h in range(hp)]

    def produce(buf, qi, ki, diag):
        q, k = _tile(q_ref, qi, tq), _tile(k_ref, ki, tq)
        for rows, kcols, _ in heads:
            _put_logits(buf, _dot_nt(q[:, kcols], k[:, kcols]), rows)

    def consume(buf, ki, diag):
        v = _tile(v_ref, ki, tq)
        for rows, _, vcols in heads:
            _flash_consume(buf, v[:, vcols], m_ref, acc_ref, tq if diag else None, rows)

    def finish(qi):
        for rows, _, vcols in heads:
            o_ref[0, pl.ds(pl.multiple_of(qi * tq, tq), tq), vcols] = _flash_result(acc_ref[rows]).astype(o_ref.dtype)
        _flash_reset(m_ref, acc_ref)

    _flash_stream(n, (qt_ref, kt_ref, lt_ref), 0, produce, consume, finish, (sa_ref, ma_ref), (sb_ref, mb_ref),
                  mask_at_produce=False)


def _mla_attention(qc, kc, v):
    b, s, _ = qc.shape
    tq = min(TQ_DENSE, s)
    hw = 2 * LANES
    hp = MLA_HP
    n, sched = _causal_schedule(s // tq)
    return pl.pallas_call(
        functools.partial(_plain_attn_kernel, tq=tq, n=n, hp=hp, dk=hw, dv=MLA_VDIM),
        grid=(b, MLA_HEADS // hp),
        in_specs=[
            _SMEM, _SMEM, _SMEM,
            pl.BlockSpec((1, s, hp * hw), lambda bi, h: (bi, 0, h)),
            pl.BlockSpec((1, s, hp * hw), lambda bi, h: (bi, 0, h)),
            pl.BlockSpec((1, s, hp * MLA_VDIM), lambda bi, h: (bi, 0, h)),
        ],
        out_specs=pl.BlockSpec((1, s, hp * MLA_VDIM), lambda bi, h: (bi, 0, h)),
        out_shape=jax.ShapeDtypeStruct((b, s, MLA_HEADS * MLA_VDIM), CDT),
        scratch_shapes=_flash_scratch(hp * tq, tq),
        compiler_params=_cparams(("parallel", "parallel")),
        name="mla_attention",
    )(*sched, qc, kc, v)


def _fox_attn_kernel(qt_ref, kt_ref, lt_ref, c_ref, q_ref, k_ref, v_ref, o_ref,
                     m_ref, acc_ref, sa_ref, sb_ref, ma_ref, mb_ref, *, tq, n, hp):
    _flash_reset(m_ref, acc_ref)
    heads = [(slice(h * tq, (h + 1) * tq), slice(h * FOX_DH, (h + 1) * FOX_DH)) for h in range(hp)]

    def produce(buf, qi, ki, diag):
        q, k = _tile(q_ref, qi, tq), _tile(k_ref, ki, tq)
        for h, (rows, cols) in enumerate(heads):
            cbase = c_ref[0, h, pl.ds(qi, 1), :][:, 0:1]
            s = _dot_nt(q[:, cols], k[:, cols]) + LOG2E * (cbase - c_ref[0, h, pl.ds(ki, 1), :])
            _put_logits(buf, s, rows)

    def consume(buf, ki, diag):
        v = _tile(v_ref, ki, tq)
        for rows, cols in heads:
            _flash_consume(buf, v[:, cols], m_ref, acc_ref, tq if diag else None, rows)

    def finish(qi):
        for rows, cols in heads:
            o_ref[0, pl.ds(pl.multiple_of(qi * tq, tq), tq), cols] = _flash_result(acc_ref[rows]).astype(o_ref.dtype)
        _flash_reset(m_ref, acc_ref)

    _flash_stream(n, (qt_ref, kt_ref, lt_ref), 0, produce, consume, finish, (sa_ref, ma_ref), (sb_ref, mb_ref),
                  mask_at_produce=False)


def _fox_attention(proj3, c4):
    b, s, _ = proj3.shape
    tq = min(TQ_DENSE, s)
    nk = s // tq
    hp = FOX_HP
    w = hp * FOX_DH
    n, sched = _causal_schedule(nk)
    return pl.pallas_call(
        functools.partial(_fox_attn_kernel, tq=tq, n=n, hp=hp),
        grid=(b, FOX_HEADS // hp),
        in_specs=[
            _SMEM, _SMEM, _SMEM,
            pl.BlockSpec((1, hp, nk, tq), lambda bi, h: (bi, h, 0, 0)),
            pl.BlockSpec((1, s, w), lambda bi, h: (bi, 0, PB_CQ // hp + h)),
            pl.BlockSpec((1, s, w), lambda bi, h: (bi, 0, PB_CK // hp + h)),
            pl.BlockSpec((1, s, w), lambda bi, h: (bi, 0, PB_CV // hp + h)),
        ],
        out_specs=pl.BlockSpec((1, s, w), lambda bi, h: (bi, 0, h)),
        out_shape=jax.ShapeDtypeStruct((b, s, FOX_HEADS * FOX_DH), CDT),
        scratch_shapes=_flash_scratch(hp * tq, tq),
        compiler_params=_cparams(("parallel", "parallel")),
        name="fox_attention",
    )(*sched, c4.reshape(b, FOX_HEADS, nk, tq), proj3, proj3, proj3)


def _nsa_compress_kernel(x_ref, w1a_ref, w1b_ref, pea_ref, peb_ref, w2_ref, o_ref):
    x = x_ref[0]
    n = x.shape[0]
    pa = _dot(x, w1a_ref[...])
    pb = _dot(x, w1b_ref[...])
    pe = _dot(pea_ref[...], w1a_ref[...]) + _dot(peb_ref[...], w1b_ref[...])
    hid = pa + pltpu.roll(pb, n - 1, axis=0) + pe[0:1]
    act = 0.5 * hid * (1.0 + jnp.tanh(math.sqrt(2.0 / math.pi) * (hid + 0.044715 * hid * hid * hid)))
    o_ref[0] = _dot(act.astype(CDT), w2_ref[...]).astype(o_ref.dtype)


def _nsa_compress(xc, w1a, w1b, pea, peb, w2):
    b, n, kdim = xc.shape
    hdim = w1a.shape[1]
    const = lambda shape: pl.BlockSpec(shape, lambda bi: (0,) * len(shape))
    return pl.pallas_call(
        _nsa_compress_kernel,
        grid=(b,),
        in_specs=[pl.BlockSpec((1, n, kdim), lambda bi: (bi, 0, 0)),
                  const((kdim, hdim)), const((kdim, hdim)), const((8, kdim)), const((8, kdim)),
                  const((hdim, w2.shape[1]))],
        out_specs=pl.BlockSpec((1, n, w2.shape[1]), lambda bi: (bi, 0, 0)),
        out_shape=jax.ShapeDtypeStruct((b, n, w2.shape[1]), CDT),
        compiler_params=_cparams(("parallel",)),
        name="nsa_compress",
    )(xc, w1a, w1b, pea, peb, w2)


def _nsa_cmp_kernel(slopes_ref, q_ref, kv_ref, oc_ref, sb_ref, used_ref, *, tq, n_topk):
    qi = pl.program_id(1)
    nblk = kv_ref.shape[1]
    q0 = qi * tq
    rowpos = q0 + lax.broadcasted_iota(jnp.int32, (tq, 1), 0)
    cmp_end = lax.broadcasted_iota(jnp.int32, (1, nblk), 1) * CMP_STRIDE + (CMP_LEN - 1)
    negmask = jnp.where(rowpos >= cmp_end, 0.0, NEG)
    end_rel = (cmp_end - q0).astype(F32)
    lane = lax.broadcasted_iota(jnp.int32, (tq, LANES), 1)
    low = lane < NSA_DH
    nn = lax.broadcasted_iota(jnp.int32, (NSA_DH, nblk), 1) * CMP_STRIDE
    jj = lax.broadcasted_iota(jnp.int32, (NSA_DH, nblk), 0) * SLC_LEN
    ovt = (jnp.maximum(jnp.minimum(nn + CMP_LEN, jj + SLC_LEN) - jnp.maximum(nn, jj), 0).astype(F32)
           * (1.0 / CMP_LEN)).astype(CDT)
    jt = lax.broadcasted_iota(jnp.int32, (NSA_DH, tq), 0).astype(F32)
    blk = ((q0 + lax.broadcasted_iota(jnp.int32, (1, tq), 1)) >> SLC_SHIFT).astype(F32)
    fixed = (jt == 0.0) | (jt == blk) | (jt == blk - 1.0)
    out_of_play = fixed | (jt > blk)
    row_ok = rowpos >= CMP_LEN - 1
    outs = []
    bias = []
    for g in range(NSA_GROUPS):
        kc = kv_ref[0, :, g * LANES:(g + 1) * LANES]
        vc = kv_ref[0, :, (NSA_GROUPS + g) * LANES:(NSA_GROUPS + g + 1) * LANES]
        psum = jnp.zeros((tq, nblk), F32)
        mine = low if g == 0 else jnp.logical_not(low)
        zero = jnp.zeros((tq, LANES), q_ref.dtype)
        qs = jnp.concatenate([jnp.where(mine, q_ref[0, :, j * LANES:(j + 1) * LANES], zero)
                              for j in range(NSA_HPG)], axis=0)
        s_all = _dot_nt(qs, kc)
        ps = []
        for j in range(NSA_HPG):
            s = s_all[j * tq:(j + 1) * tq] + slopes_ref[g * NSA_HPG + j] * end_rel + negmask
            e = jnp.exp2(s - jnp.max(s, axis=-1, keepdims=True))
            den = jnp.sum(e, axis=-1, keepdims=True)
            p = e * jnp.where(row_ok, 1.0 / den, 0.0)
            psum = psum + p
            ps.append(p.astype(CDT))
        o_all = _dot(jnp.concatenate(ps, axis=0), vc)
        outs.extend(o_all[j * tq:(j + 1) * tq] for j in range(NSA_HPG))
        hi = psum.astype(CDT)
        lo = (psum - hi.astype(F32)).astype(CDT)
        imp = _dot_nt(ovt, hi) + _dot_nt(ovt, lo)
        imp = jnp.where(out_of_play, -jnp.inf, imp)
        sbt = jnp.where(fixed, 0.0, NEG)
        for _ in range(n_topk - 3):
            mx = jnp.max(imp, axis=0, keepdims=True)
            idx = jnp.min(jnp.where(imp == mx, jt, float(LANES)), axis=0, keepdims=True)
            hit = jt == idx
            sbt = jnp.where(hit, 0.0, sbt)
            imp = jnp.where(hit, -jnp.inf, imp)
        bias.append(sbt)
    sb = jnp.concatenate([bias[1], bias[0]], axis=0).T
    sb_ref[0] = sb.astype(sb_ref.dtype)
    used = jnp.max(jnp.where(sb == 0.0, 1.0, 0.0), axis=0, keepdims=True)
    used_ref[0, 0] = jnp.broadcast_to(used, used_ref.shape[2:])
    for blk_i in range(NSA_HEADS // 2):
        oc_ref[0, :, blk_i * LANES:(blk_i + 1) * LANES] = jnp.where(
            low, outs[2 * blk_i], outs[2 * blk_i + 1]).astype(oc_ref.dtype)


def _nsa_cmp_select(proj3, kvc, n_topk):
    assert n_topk >= 3, "the three always-selected blocks must fit in the top-k budget"
    b, s, _ = proj3.shape
    tq = min(TQ_NSA, s)
    nblk = kvc.shape[1]
    return pl.pallas_call(
        functools.partial(_nsa_cmp_kernel, tq=tq, n_topk=n_topk),
        grid=(b, s // tq),
        in_specs=[
            pl.BlockSpec(memory_space=pltpu.SMEM),
            pl.BlockSpec((1, tq, 4 * LANES), lambda bi, qi: (bi, qi, PB_DQ // 4)),
            pl.BlockSpec((1, nblk, kvc.shape[2]), lambda bi, qi: (bi, 0, 0)),
        ],
        out_specs=[
            pl.BlockSpec((1, tq, NSA_HEADS * NSA_DH), lambda bi, qi: (bi, qi, 0)),
            pl.BlockSpec((1, tq, LANES), lambda bi, qi: (bi, qi, 0)),
            pl.BlockSpec((1, 1, 8, LANES), lambda bi, qi: (bi, qi, 0, 0)),
        ],
        out_shape=[jax.ShapeDtypeStruct((b, s, NSA_HEADS * NSA_DH), CDT),
                   jax.ShapeDtypeStruct((b, s, LANES), CDT),
                   jax.ShapeDtypeStruct((b, s // tq, 8, LANES), F32)],
        compiler_params=_cparams(("parallel", "parallel")),
        name="nsa_cmp_select",
    )(jnp.asarray(_alibi_slopes(NSA_HEADS)), proj3, kvc)


def _compact_heads(heads, mine, low):
    both = [jnp.where(mine, a, pltpu.roll(a, NSA_DH, axis=1)) for a in heads]
    out = [jnp.where(low, both[2 * jj], both[2 * jj + 1]) for jj in range(NSA_HPG // 2)]
    return jnp.concatenate(out, axis=1)


def _nsa_win_kernel(slopes_ref, q_ref, kp_ref, kc_ref, vp_ref, vc_ref, o_ref, *, tq):
    qi = pl.program_id(1)
    lane = lax.broadcasted_iota(jnp.int32, (tq, LANES), 1)
    low = lane < NSA_DH
    r = lax.broadcasted_iota(jnp.int32, (tq, tq), 0)
    c = lax.broadcasted_iota(jnp.int32, (tq, tq), 1)
    own = c <= r
    ndist = jnp.where(own, c - r, c - r - tq).astype(F32)
    own_f = jnp.where(own, 1.0, 0.0).astype(CDT)
    prev_pen = jnp.where(qi > 0, 0.0, NEG)
    q = q_ref[0]
    zero = jnp.zeros((tq, LANES), q.dtype)
    mine = (low, jnp.logical_not(low))
    qs = jnp.concatenate([jnp.where(mine[g], q[:, j * LANES:(j + 1) * LANES], zero)
                          for g in range(NSA_GROUPS) for j in range(NSA_HPG)], axis=0)
    s_own, s_prev = _dot_nt(qs, kc_ref[0]), _dot_nt(qs, kp_ref[0])
    ps = []
    for hd in range(NSA_HEADS):
        rows = slice(hd * tq, (hd + 1) * tq)
        s = jnp.where(own, s_own[rows], s_prev[rows] + prev_pen) + slopes_ref[hd] * ndist
        ps.append(jnp.exp2(s - jnp.max(s, axis=-1, keepdims=True)).astype(CDT))
    p = jnp.concatenate(ps, axis=0)
    p_own = p * jnp.tile(own_f, (NSA_HEADS, 1))
    o = _flash_result(_dot(p_own, _with_ones(vc_ref[0])) + _dot(p - p_own, _with_ones(vp_ref[0])))
    for g in range(NSA_GROUPS):
        heads = [o[(g * NSA_HPG + j) * tq:(g * NSA_HPG + j + 1) * tq] for j in range(NSA_HPG)]
        w = NSA_HPG * NSA_DH
        o_ref[0, :, g * w:(g + 1) * w] = _compact_heads(heads, mine[g], low).astype(o_ref.dtype)


def _nsa_window(proj3):
    b, s, _ = proj3.shape
    tq = WINDOW
    return pl.pallas_call(
        functools.partial(_nsa_win_kernel, tq=tq),
        grid=(b, s // tq),
        in_specs=[
            pl.BlockSpec(memory_space=pltpu.SMEM),
            pl.BlockSpec((1, tq, 4 * LANES), lambda bi, qi: (bi, qi, PB_DQ // 4)),
            pl.BlockSpec((1, tq, LANES), lambda bi, qi: (bi, jnp.maximum(qi - 1, 0), PB_WIN_K)),
            pl.BlockSpec((1, tq, LANES), lambda bi, qi: (bi, qi, PB_WIN_K)),
            pl.BlockSpec((1, tq, LANES), lambda bi, qi: (bi, jnp.maximum(qi - 1, 0), PB_WIN_V)),
            pl.BlockSpec((1, tq, LANES), lambda bi, qi: (bi, qi, PB_WIN_V)),
        ],
        out_specs=pl.BlockSpec((1, tq, NSA_HEADS * NSA_DH), lambda bi, qi: (bi, qi, 0)),
        out_shape=jax.ShapeDtypeStruct((b, s, NSA_HEADS * NSA_DH), CDT),
        compiler_params=_cparams(("parallel", "parallel")),
        name="nsa_window",
    )(jnp.asarray(_alibi_slopes(NSA_HEADS)), proj3, proj3, proj3, proj3, proj3)


def _nsa_sel_kernel(cnt_ref, qt_ref, kt_ref, lt_ref, slopes_ref, q_ref, sb_ref, k_ref, v_ref, oc_ref, ow_ref, gl_ref,
                    e_ref, o_ref, m_ref, acc_ref, sa_ref, sb2_ref, ma_ref, mb_ref, cm_ref, *, tq, rows_per_problem):
    g = pl.program_id(1)
    w = NSA_HPG * NSA_DH
    lane = lax.broadcasted_iota(jnp.int32, (tq, LANES), 1)
    low = lane < NSA_DH
    mine = (lane >> HALF_SHIFT) == g
    _flash_begin(m_ref, acc_ref, cm_ref, tq)
    col = lax.broadcasted_iota(jnp.int32, (1, tq), 1).astype(F32)
    jl = lane & (NSA_DH - 1)
    krow = lax.broadcasted_iota(jnp.int32, (tq, LANES), 0)

    def produce(buf, qi, ki, diag):
        q = _tile(q_ref, qi, tq)
        sb = _tile(sb_ref, qi, tq)
        qa = jnp.concatenate([jnp.where(mine, q[:, j * LANES:(j + 1) * LANES], sb) for j in range(NSA_HPG)], axis=0)
        k = _tile(k_ref, ki, tq)
        onehot = jnp.where(((ki * tq + krow) >> SLC_SHIFT) == jl, 1.0, 0.0).astype(k.dtype)
        s_all = _dot_nt(qa, jnp.where(mine, k, onehot))
        rel = ((ki - qi) * tq).astype(F32)
        for j in range(NSA_HPG):
            rows = slice(j * tq, (j + 1) * tq)
            _put_logits(buf, s_all[rows] + slopes_ref[g * NSA_HPG + j] * (col + rel), rows, diag, cm_ref)

    def consume(buf, ki, diag):
        _flash_consume(buf, _tile(v_ref, ki, tq), m_ref, acc_ref, tq if diag else None)

    def finish(qi):
        o = _flash_result(acc_ref[...])
        o_s = _compact_heads([o[j * tq:(j + 1) * tq] for j in range(NSA_HPG)], mine, low)
        gates = _split_dot(_sigmoid(_tile(gl_ref, qi, tq)), e_ref[0])
        y = (gates[:, 0:w] * _tile(oc_ref, qi, tq).astype(F32) + gates[:, w:2 * w] * o_s
             + gates[:, 2 * w:3 * w] * _tile(ow_ref, qi, tq).astype(F32))
        o_ref[0, pl.ds(pl.multiple_of(qi * tq, tq), tq), :] = y.astype(o_ref.dtype)
        _flash_reset(m_ref, acc_ref)

    prob = pl.program_id(0) * NSA_GROUPS + g
    _flash_stream(cnt_ref[prob], (qt_ref, kt_ref, lt_ref), prob * rows_per_problem, produce, consume, finish,
                  (sa_ref, ma_ref), (sb2_ref, mb_ref), mask_at_produce=True)


def _nsa_selected(proj3, sbias, used, o_c, o_w, small3, expand):
    b, s, _ = proj3.shape
    tq = min(TQ_NSA, s)
    nq = s // tq
    w = NSA_HPG * NSA_DH
    u = used[:, :, 0, :].reshape(b, nq, NSA_GROUPS, NSA_DH)[:, :, ::-1, :nq * (tq // SLC_LEN)]
    flags = (u.reshape(b, nq, NSA_GROUPS, nq, tq // SLC_LEN).max(axis=-1) > 0.0).astype(jnp.int32)
    flags = flags.transpose(0, 2, 1, 3)
    qt = jnp.arange(nq, dtype=jnp.int32)
    need = jnp.where(qt[None, :] < qt[:, None], flags, (qt[None, :] == qt[:, None]).astype(jnp.int32))
    need = need.reshape(b, NSA_GROUPS, nq * nq)
    cnt = need.sum(axis=-1).astype(jnp.int32)
    order = jnp.argsort(1 - need, axis=-1, stable=True).astype(jnp.int32)
    order = jnp.pad(order, ((0, 0), (0, 0), (0, 2)))
    rows = nq * nq + 2
    sched = (order // nq, order % nq, (order // nq == order % nq).astype(jnp.int32))
    return pl.pallas_call(
        functools.partial(_nsa_sel_kernel, tq=tq, rows_per_problem=rows),
        grid=(b, NSA_GROUPS),
        in_specs=[
            _SMEM, _SMEM, _SMEM, _SMEM, _SMEM,
            pl.BlockSpec((1, s, 4 * LANES), lambda bi, g: (bi, 0, PB_DQ // 4)),
            pl.BlockSpec((1, s, LANES), lambda bi, g: (bi, 0, 0)),
            pl.BlockSpec((1, s, LANES), lambda bi, g: (bi, 0, PB_SEL_K)),
            pl.BlockSpec((1, s, LANES), lambda bi, g: (bi, 0, PB_SEL_V)),
            pl.BlockSpec((1, s, w), lambda bi, g: (bi, 0, g)),
            pl.BlockSpec((1, s, w), lambda bi, g: (bi, 0, g)),
            pl.BlockSpec((1, s, LANES), lambda bi, g: (bi, 0, 0)),
            pl.BlockSpec((1, LANES, 3 * w), lambda bi, g: (g, 0, 0)),
        ],
        out_specs=pl.BlockSpec((1, s, w), lambda bi, g: (bi, 0, g)),
        out_shape=jax.ShapeDtypeStruct((b, s, NSA_HEADS * NSA_DH), CDT),
        scratch_shapes=_flash_scratch(NSA_HPG * tq, tq, mask_scratch=True),
        compiler_params=_cparams(("parallel", "parallel")),
        name="nsa_selected",
    )(cnt.reshape(-1), *[t.reshape(-1) for t in sched], jnp.asarray(_alibi_slopes(NSA_HEADS)),
      proj3, sbias, proj3, proj3, o_c, o_w, small3, expand)


def _merge_kernel(ya_ref, yb_ref, yc_ref, yd_ref, ga_ref, gb_ref, gc_ref, gd_ref, wb_ref, wo_ref, x_ref, o_ref):
    merged = None
    for n, (y_ref, g_ref) in enumerate(((ya_ref, ga_ref), (yb_ref, gb_ref), (yc_ref, gc_ref), (yd_ref, gd_ref))):
        t = _sigmoid(g_ref[...].astype(F32)) * _dot(y_ref[...], wb_ref[n])
        merged = t if merged is None else merged + t
    o_ref[...] = x_ref[...] + _dot(merged.astype(CDT), wo_ref[...])


def _merge(ys, proj2, wb, wo, x2):
    t, d = x2.shape
    tm = min(TM_ROWS, t)
    gate_blk = PB_GATE * LANES // d
    yspec = pl.BlockSpec((tm, BRANCH_WIDTH), lambda i: (i, 0))
    gspecs = [pl.BlockSpec((tm, d), functools.partial(lambda i, n: (i, gate_blk + n), n=n)) for n in range(N_BRANCH)]
    return pl.pallas_call(
        _merge_kernel,
        grid=(t // tm,),
        in_specs=[yspec] * N_BRANCH + gspecs + [
            pl.BlockSpec((N_BRANCH, BRANCH_WIDTH, d), lambda i: (0, 0, 0)),
            pl.BlockSpec((d, d), lambda i: (0, 0)),
            pl.BlockSpec((tm, d), lambda i: (i, 0)),
        ],
        out_specs=pl.BlockSpec((tm, d), lambda i: (i, 0)),
        out_shape=jax.ShapeDtypeStruct((t, d), F32),
        compiler_params=_cparams(("parallel",)),
        name="merge",
    )(*ys, proj2, proj2, proj2, proj2, wb, wo, x2)


HALO = 16


def _ffn_kernel(x_ref, xh_ref, g_ref, wu_ref, cw_ref, cb_ref, wd_ref, gf_ref, o_ref, he_ref, u_ref, act_ref,
                *, tm, fc, final):
    i = pl.program_id(1)
    x = x_ref[0]
    g = g_ref[...]
    xh = xh_ref[0] * (i > 0).astype(F32)
    he_ref[0:HALO] = _rms(xh, g).astype(CDT)
    he_ref[HALO:HALO + tm] = _rms(x, g).astype(CDT)
    he = he_ref[...]
    for c in range(D_FF // fc):
        outs = []
        for half in range(2):
            ub = u_ref.at[c % 2, half]
            lo = half * D_FF + c * fc
            ub[...] = _dot(he, wu_ref[:, lo:lo + fc])
            conv = cb_ref[:, lo:lo + fc]
            for kk in range(CONV_WIDTH):
                off = HALO - (CONV_WIDTH - 1) + kk
                conv = conv + cw_ref[kk:kk + 1, lo:lo + fc] * ub[off:off + tm, :]
            outs.append(conv)
        a, gg = outs
        act_ref[:, c * fc:(c + 1) * fc] = (a * _sigmoid(a) * gg).astype(CDT)
    y = x + _dot(act_ref[...], wd_ref[...])
    if final:
        y = _rms(y, gf_ref[...])
    o_ref[0] = y


def _ffn(x3, g, wu, cw, cb, wd, gf, final):
    b, s, d = x3.shape
    tm = min(TM_ROWS, s)
    fc = FFN_CHUNK
    assert D_FF % fc == 0
    const = lambda shape: pl.BlockSpec(shape, lambda bi, i: (0,) * len(shape), pipeline_mode=pl.Buffered(1))
    return pl.pallas_call(
        functools.partial(_ffn_kernel, tm=tm, fc=fc, final=final),
        grid=(b, s // tm),
        in_specs=[
            pl.BlockSpec((1, tm, d), lambda bi, i: (bi, i, 0)),
            pl.BlockSpec((1, HALO, d), lambda bi, i: (bi, jnp.maximum(i * (tm // HALO) - 1, 0), 0)),
            const((1, d)), const((d, 2 * D_FF)), const((CONV_WIDTH, 2 * D_FF)), const((1, 2 * D_FF)),
            const((D_FF, d)), const((1, d)),
        ],
        out_specs=pl.BlockSpec((1, tm, d), lambda bi, i: (bi, i, 0)),
        out_shape=jax.ShapeDtypeStruct((b, s, d), F32),
        scratch_shapes=[pltpu.VMEM((tm + HALO, d), CDT), pltpu.VMEM((2, 2, tm + HALO, fc), F32),
                        pltpu.VMEM((tm, D_FF), CDT)],
        compiler_params=_cparams(("parallel", "arbitrary")),
        name="conv_glu_mlp",
    )(x3, x3, g.reshape(1, d), wu, cw, cb.reshape(1, -1), wd, gf.reshape(1, d))


def _prep_w_in(w):
    widths = (512, 512, 512, MLA_Q_LORA, MLA_KV_LORA, MLA_ROPE, 512, 512, 512, FOX_HEADS,
              512, 768, 3 * NSA_HEADS, N_BRANCH * D_MODEL)
    offs = np.cumsum((0,) + widths)
    (a_q, a_k, a_v, b_cq, b_ckv, b_kr, c_q, c_k, c_v, c_f, d_q, d_kv, d_g, gate) = [
        w[:, offs[i]:offs[i + 1]] for i in range(len(widths))]
    d = w.shape[0]
    d_q = d_q.reshape(d, NSA_GROUPS, NSA_HPG, NSA_DH).transpose(0, 2, 1, 3).reshape(d, 512)
    half = MLA_ROPE // 2
    kr_swap = jnp.concatenate([-b_kr[:, half:], b_kr[:, :half]], axis=1)
    z64 = jnp.zeros((d, LANES - MLA_ROPE), w.dtype)
    pieces = [a_q * (LOG2E * DIFF_DH ** -0.5), a_k, a_v,
              c_q * (LOG2E * FOX_DH ** -0.5), c_k, c_v,
              d_q * (LOG2E * NSA_DH ** -0.5), d_kv,
              b_cq, b_ckv, b_kr, z64, kr_swap, z64,
              gate]
    big = jnp.concatenate([p.astype(CDT) for p in pieces], axis=1)
    small = jnp.concatenate([c_f, d_g, jnp.zeros((d, LANES - FOX_HEADS - 3 * NSA_HEADS), w.dtype)], axis=1)
    return big, small.astype(CDT)


def _prep_mla(w_uq, w_ukv):
    r = w_uq.shape[0]
    hw = 2 * LANES
    half = MLA_ROPE // 2
    scale = LOG2E * (MLA_NOPE + MLA_ROPE) ** -0.5
    wq = (w_uq * scale).reshape(r, MLA_HEADS, MLA_NOPE + MLA_ROPE)
    nope, t1, t2 = wq[..., :MLA_NOPE], wq[..., MLA_NOPE:MLA_NOPE + half], wq[..., MLA_NOPE + half:]
    zpad = jnp.zeros((r, MLA_HEADS, hw - MLA_NOPE - MLA_ROPE), w_uq.dtype)
    wqm = jnp.concatenate([nope, t1, t2, zpad], axis=-1).reshape(r, MLA_HEADS * hw)
    wqs = jnp.concatenate([jnp.zeros_like(nope), -t2, t1, zpad], axis=-1).reshape(r, MLA_HEADS * hw)
    wkv = w_ukv.reshape(w_ukv.shape[0], MLA_HEADS, MLA_NOPE + MLA_VDIM)
    wk = wkv[..., :MLA_NOPE].reshape(-1, MLA_HEADS * MLA_NOPE)
    wv = wkv[..., MLA_NOPE:].reshape(-1, MLA_HEADS * MLA_VDIM)
    return wqm.astype(CDT), wqs.astype(CDT), wk.astype(CDT), wv.astype(CDT)


def _rope_tables(s):
    half = MLA_ROPE // 2
    inv_freq = ROPE_THETA ** (-jnp.arange(0, MLA_ROPE, 2, dtype=F32) / MLA_ROPE)
    ang = jnp.arange(s, dtype=F32)[:, None] * inv_freq[None, :]
    cos, sin = jnp.cos(ang), jnp.sin(ang)
    z = jnp.zeros((s, LANES - MLA_ROPE), F32)
    cosk = jnp.concatenate([cos, cos, z], axis=1)
    sink = jnp.concatenate([sin, sin, z], axis=1)
    cosq = jnp.concatenate([jnp.ones((s, MLA_NOPE), F32), cosk], axis=1)
    sinq = jnp.concatenate([jnp.zeros((s, MLA_NOPE), F32), sink], axis=1)
    return cosq, sinq, cosk, sink


def _prep_compress(pe, w1, w2):
    eye2 = jnp.eye(2, dtype=F32)
    w1r = w1.reshape(2, CMP_LEN, NSA_DH, CMP_HIDDEN).astype(CDT)
    same = np.eye(2, dtype=bool)
    diag_kg = jnp.asarray(same[:, None, :, None] & same[None, :, None, :])

    def expand(wpart):
        src = wpart.transpose(1, 0, 2, 3)[:, :, None, :, None, None, :]
        t = jnp.where(diag_kg[None, :, :, None, :, :, None], src, jnp.zeros((), CDT))
        return t.reshape(CMP_STRIDE * 4 * NSA_DH, 4 * CMP_HIDDEN)

    w1a, w1b = expand(w1r[:, :CMP_STRIDE]), expand(w1r[:, CMP_STRIDE:])

    def pe_row(p):
        t = jnp.broadcast_to(p.transpose(1, 0, 2)[:, :, None, :], (CMP_STRIDE, 2, NSA_GROUPS, NSA_DH))
        return jnp.pad(t.reshape(1, -1), ((0, 7), (0, 0)))

    pea, peb = pe_row(pe[:, :CMP_STRIDE]), pe_row(pe[:, CMP_STRIDE:])
    w2b = jnp.einsum('khd,kK,gG,u->kghKGud', w2, eye2, eye2, jnp.ones((2,), F32))
    w2b = w2b.reshape(4 * CMP_HIDDEN, 4 * 2 * NSA_DH)
    return w1a.astype(CDT), w1b.astype(CDT), pea.astype(CDT), peb.astype(CDT), w2b.astype(CDT)


def _gate_expand():
    e = np.zeros((NSA_GROUPS, LANES, 3, NSA_HPG, NSA_DH), np.float32)
    for g in range(NSA_GROUPS):
        for j in range(NSA_HPG):
            for br in range(3):
                e[g, SMALL_G + (g * NSA_HPG + j) * 3 + br, br, j, :] = 1.0
    return jnp.asarray(e.reshape(NSA_GROUPS, LANES, 3 * NSA_HPG * NSA_DH)).astype(CDT)


def _token_mixers(x3, l, norm_mix, w_in, diff_lambda, diff_subln, mla_norm_q, mla_w_uq, mla_norm_kv, mla_w_ukv,
                  fox_b_f, nsa_cmp_pe, nsa_cmp_w1, nsa_cmp_w2, w_branch, w_out, rope_tabs):
    b, s, d = x3.shape
    t = b * s
    x2 = x3.reshape(t, d)
    proj, small = _in_proj(x2, norm_mix, *_prep_w_in(w_in))
    proj3 = proj.reshape(b, s, N_PROJ)
    small3 = small.reshape(b, s, LANES)

    lam_init = 0.8 - 0.6 * math.exp(-0.3 * l)
    y_a = _diff_attention(proj3, diff_lambda, diff_subln, lam_init)

    wqm, wqs, wk, wv = _prep_mla(mla_w_uq, mla_w_ukv)
    qc, kc, vv = _mla_prep(proj3, mla_norm_q, mla_norm_kv, wqm, wqs, wk, wv, rope_tabs)
    y_b = _mla_attention(qc, kc, vv)

    cf_rows = small3[:, :, SMALL_F:SMALL_F + FOX_HEADS].transpose(0, 2, 1).reshape(b * FOX_HEADS, s)
    bias_rows = jnp.tile(fox_b_f.astype(F32), b).reshape(b * FOX_HEADS, 1)
    c4 = _fox_cumsum(cf_rows, bias_rows)
    y_c = _fox_attention(proj3, c4)

    w1a, w1b, pea, peb, w2b = _prep_compress(nsa_cmp_pe, nsa_cmp_w1, nsa_cmp_w2)
    xc = proj3[:, :, PB_CMP_K * LANES:(PB_CMP_V + 1) * LANES].reshape(b, s // CMP_STRIDE, CMP_STRIDE * 2 * LANES)
    kvc = _nsa_compress(xc, w1a, w1b, pea, peb, w2b)
    n_topk = min(SLC_TOPK, s // SLC_LEN)
    o_c, sbias, used = _nsa_cmp_select(proj3, kvc, n_topk)
    o_w = _nsa_window(proj3)
    y_d = _nsa_selected(proj3, sbias, used, o_c, o_w, small3, _gate_expand())

    ys = [y.reshape(t, BRANCH_WIDTH) for y in (y_a, y_b, y_c, y_d)]
    return _merge(ys, proj, w_branch.astype(CDT), w_out.astype(CDT), x2).reshape(b, s, d)


def kernel(x, norm_mix, w_in, diff_lambda, diff_subln, mla_norm_q, mla_w_uq, mla_norm_kv, mla_w_ukv, fox_b_f,
           nsa_cmp_pe, nsa_cmp_w1, nsa_cmp_w2, w_branch, w_out, norm_ffn, w_up, conv_w, conv_b, w_down, norm_final):
    depth = w_in.shape[0]
    s = x.shape[1]
    rope_tabs = _rope_tables(s)
    for l in range(depth):
        x = _token_mixers(x, l, norm_mix[l], w_in[l], diff_lambda[l], diff_subln[l], mla_norm_q[l], mla_w_uq[l],
                          mla_norm_kv[l], mla_w_ukv[l], fox_b_f[l], nsa_cmp_pe[l], nsa_cmp_w1[l], nsa_cmp_w2[l],
                          w_branch[l], w_out[l], rope_tabs)
        x = _ffn(x, norm_ffn[l], w_up[l].astype(CDT), conv_w[l], conv_b[l], w_down[l].astype(CDT), norm_final,
                 final=(l == depth - 1))
    return x
```

```python
import functools
import math

import numpy as np
import jax
import jax.numpy as jnp
from jax import lax
from jax.experimental import pallas as pl
from jax.experimental.pallas import tpu as pltpu

F32 = jnp.float32
CDT = jnp.bfloat16

NEG = -1e30
NEG_INF = -1e30
BIG = 1e9
NORM_EPS = 1e-6
LOG2E = 1.4426950408889634
LANES = 128

D_MODEL = 1024
DIFF_HEADS, DIFF_DH = 4, 64
MLA_HEADS, MLA_NOPE, MLA_ROPE, MLA_VDIM = 4, 128, 64, 128
MLA_Q_LORA, MLA_KV_LORA = 256, 256
ROPE_THETA = 10000.0
FOX_HEADS, FOX_DH = 4, 128
NSA_HEADS, NSA_GROUPS, NSA_DH = 8, 2, 64
NSA_HPG = NSA_HEADS // NSA_GROUPS
CMP_STRIDE = 16
CMP_LEN = 2 * CMP_STRIDE
CMP_HIDDEN = 128
SLC_LEN = 64
SLC_SHIFT = 6
HALF_SHIFT = 6
SLC_TOPK = 8
WINDOW = 256
N_BRANCH = 4
BRANCH_WIDTH = 512
D_FF = 2816
CONV_WIDTH = 3

PB_AQ, PB_AK, PB_AV = 0, 4, 8
PB_CQ, PB_CK, PB_CV = 12, 16, 20
PB_DQ = 24
PB_CMP_K, PB_CMP_V, PB_SEL_K, PB_SEL_V, PB_WIN_K, PB_WIN_V = 28, 29, 30, 31, 32, 33
PB_BCQ, PB_BCKV, PB_BKR, PB_BKRS = 34, 36, 38, 39
PB_GATE = 40
N_PROJ = 72 * LANES
SMALL_F, SMALL_G = 0, 4

VMEM_LIMIT = 56 * 1024 * 1024
MXU_TILE = 256
TQ_DENSE = 512
TQ_NSA = WINDOW
TM_PROJ, TN_PROJ = 1024, 9 * MXU_TILE
TM_ROWS = 512
FFN_CHUNK = MXU_TILE
FOX_HP = 2
MLA_HP = 2


def _cparams(sem):
    return pltpu.CompilerParams(dimension_semantics=sem, vmem_limit_bytes=VMEM_LIMIT)


def _rms(xf, g):
    return xf * lax.rsqrt(jnp.mean(xf * xf, axis=-1, keepdims=True) + NORM_EPS) * g


def _sigmoid(x):
    return 0.5 * jnp.tanh(0.5 * x) + 0.5


def _dot(a, b):
    return jnp.dot(a, b, preferred_element_type=F32)


def _dot_nt(a, b):
    return lax.dot_general(a, b, (((1,), (1,)), ((), ())), preferred_element_type=F32)


def _split_dot(a, b):
    hi = a.astype(CDT)
    lo = (a - hi.astype(F32)).astype(CDT)
    return _dot(hi, b) + _dot(lo, b)


def _alibi_slopes(n):
    return (LOG2E * np.exp2(-8.0 * np.arange(1, n + 1) / n)).astype(np.float32)


def _inproj_kernel(x_ref, g_ref, w_ref, ws_ref, o_ref, os_ref, h_ref):
    @pl.when(pl.program_id(1) == 0)
    def _():
        h = _rms(x_ref[...], g_ref[...]).astype(CDT)
        h_ref[...] = h
        os_ref[...] = _dot(h, ws_ref[...])

    o_ref[...] = _dot(h_ref[...], w_ref[...]).astype(o_ref.dtype)


def _in_proj(x2, g, w, ws, layer):
    t, d = x2.shape
    n = w.shape[2]
    tm = min(TM_PROJ, t)
    tn = TN_PROJ
    assert n % tn == 0
    return pl.pallas_call(
        _inproj_kernel,
        grid=(t // tm, n // tn),
        in_specs=[
            pl.BlockSpec((tm, d), lambda i, j: (i, 0)),
            pl.BlockSpec((1, d), lambda i, j: (0, 0)),
            pl.BlockSpec((None, d, tn), lambda i, j: (layer, 0, j)),
            pl.BlockSpec((None, d, LANES), lambda i, j: (layer, 0, 0)),
        ],
        out_specs=[
            pl.BlockSpec((tm, tn), lambda i, j: (i, j)),
            pl.BlockSpec((tm, LANES), lambda i, j: (i, 0)),
        ],
        out_shape=[jax.ShapeDtypeStruct((t, n), CDT), jax.ShapeDtypeStruct((t, LANES), F32)],
        scratch_shapes=[pltpu.VMEM((tm, d), CDT)],
        compiler_params=_cparams(("parallel", "arbitrary")),
        name="in_proj",
    )(x2, g.reshape(1, d), w, ws)


def _fox_cumsum_kernel(cf_ref, bf_ref, o_ref):
    rows, s = cf_ref.shape
    lane = lax.broadcasted_iota(jnp.int32, (rows, LANES), 1)
    carry = jnp.zeros((rows, 1), F32)
    for c in range(s // LANES):
        z = cf_ref[:, c * LANES:(c + 1) * LANES] + bf_ref[...]
        xs = jnp.minimum(z, 0.0) - jnp.log1p(jnp.exp(-jnp.abs(z)))
        d = 1
        while d < LANES:
            xs = xs + jnp.where(lane >= d, pltpu.roll(xs, d, axis=1), 0.0)
            d *= 2
        xs = xs + carry
        o_ref[:, c * LANES:(c + 1) * LANES] = xs
        carry = xs[:, LANES - 1:LANES]


def _fox_cumsum(cf_rows, bias_rows):
    return pl.pallas_call(
        _fox_cumsum_kernel,
        out_shape=jax.ShapeDtypeStruct(cf_rows.shape, F32),
        name="fox_cumsum",
    )(cf_rows, bias_rows)


def _flash_scratch(rows, tk, mask_scratch=False):
    return [pltpu.VMEM((rows, LANES), F32), pltpu.VMEM((rows, 2 * LANES), F32),
            pltpu.VMEM((rows, tk), F32), pltpu.VMEM((rows, tk), F32),
            pltpu.VMEM((rows, LANES), F32), pltpu.VMEM((rows, LANES), F32)
            ] + ([pltpu.VMEM((rows, tk), F32)] if mask_scratch else [])


def _flash_reset(m_ref, acc_ref):
    m_ref[...] = jnp.full(m_ref.shape, NEG, F32)
    acc_ref[...] = jnp.zeros(acc_ref.shape, F32)


def _flash_begin(m_ref, acc_ref, cm_ref, tq):
    _flash_reset(m_ref, acc_ref)
    cm_ref[...] = _causal_bias(cm_ref.shape[0], cm_ref.shape[1], tq)


def _row_max(s):
    return jnp.broadcast_to(jnp.max(s, axis=-1, keepdims=True), (s.shape[0], LANES))


def _causal_bias(rows, tk, tq):
    r = lax.broadcasted_iota(jnp.int32, (rows, tk), 0) & (tq - 1)
    c = lax.broadcasted_iota(jnp.int32, (rows, tk), 1)
    return jnp.where(c <= r, 0.0, NEG)


def _put_logits(buf, s, rows=slice(None), diag=False, cm_ref=None):
    if diag is True:
        s = s + cm_ref[rows]
    elif diag is not False:
        s = s + diag.astype(F32) * cm_ref[rows]
    buf[0][rows] = s
    buf[1][rows] = _row_max(s)


def _with_ones(v):
    return jnp.concatenate([v, jnp.ones((v.shape[0], LANES), v.dtype)], axis=1)


def _flash_consume(buf, v, m_ref, acc_ref, mask_tq=None, rows=slice(None)):
    s = buf[0][rows]
    m_cur = buf[1][rows]
    if mask_tq is not None:
        s = s + _causal_bias(s.shape[0], s.shape[1], mask_tq)
        m_cur = _row_max(s)
    m_old = m_ref[rows]
    m_new = jnp.maximum(m_old, m_cur)
    alpha = jnp.exp2(m_old - m_new)
    p = jnp.exp2(s - jnp.tile(m_new, (1, s.shape[1] // LANES))).astype(CDT)
    acc_ref[rows] = jnp.tile(alpha, (1, 2)) * acc_ref[rows] + _dot(p, _with_ones(v))
    m_ref[rows] = m_new


def _flash_result(acc):
    return acc[:, :LANES] / acc[:, LANES:]


def _causal_schedule(nq):
    ent = [(qi, ki, int(ki == qi)) for qi in range(nq) for ki in range(qi + 1)]
    n = len(ent)
    a = np.asarray(ent + [ent[-1]] * 2, np.int32)
    return n, tuple(jnp.asarray(a[:, i]) for i in range(3))


def _flash_stream(n, sched, base, produce, consume, finish, buf_a, buf_b, mask_at_produce):
    qt, kt, lt = sched

    def step(cur, nxt, t, diag, next_diag):
        if nxt is not None:
            produce(nxt, qt[base + t + 1], kt[base + t + 1], next_diag if mask_at_produce else False)
        consume(cur, kt[base + t], diag and not mask_at_produce)
        if diag:
            finish(qt[base + t])

    produce(buf_a, qt[base], kt[base], mask_at_produce)

    def pair(j, c):
        t = 2 * j
        l0, l1 = lt[base + t], lt[base + t + 1]
        for d0 in (False, True):
            for d1 in (False, True):
                @pl.when(((l0 != 0) == d0) & ((l1 != 0) == d1))
                def _():
                    step(buf_a, buf_b, t, d0, d1)
                    step(buf_b, buf_a, t + 1, d1, lt[base + t + 2])
        return c

    lax.fori_loop(0, n // 2, pair, 0)

    def tail():
        step(buf_a, None, n - 1, True, None)

    if isinstance(n, int):
        if n % 2 == 1:
            tail()
    else:
        pl.when(n % 2 == 1)(tail)


def _tile(ref, i, t):
    return ref[0, pl.ds(pl.multiple_of(i * t, t), t), :]


def _diff_attn_kernel(qt_ref, kt_ref, lt_ref, slopes_ref, lam_ref, g_ref, q_ref, k_ref, v_ref, o_ref,
                      m_ref, acc_ref, sa_ref, sb_ref, ma_ref, mb_ref, *, tq, n, lam_init):
    slope = slopes_ref[pl.program_id(1)]
    _flash_reset(m_ref, acc_ref)
    col = lax.broadcasted_iota(jnp.int32, (1, tq), 1).astype(F32)
    lane = lax.broadcasted_iota(jnp.int32, (tq, LANES), 1)
    lf = lam_ref[...]
    lam = (jnp.exp(jnp.sum(lf[0:1] * lf[1:2], axis=-1, keepdims=True))
           - jnp.exp(jnp.sum(lf[2:3] * lf[3:4], axis=-1, keepdims=True)) + lam_init)

    def produce(buf, qi, ki, diag):
        q = _tile(q_ref, qi, tq)
        zero = jnp.zeros_like(q)
        qq = jnp.concatenate([jnp.where(lane < DIFF_DH, q, zero), jnp.where(lane >= DIFF_DH, q, zero)], axis=0)
        s = _dot_nt(qq, _tile(k_ref, ki, tq))
        _put_logits(buf, s + slope * (col + ((ki - qi) * tq).astype(F32)))

    def consume(buf, ki, diag):
        _flash_consume(buf, _tile(v_ref, ki, tq), m_ref, acc_ref, tq if diag else None)

    def finish(qi):
        o = _flash_result(acc_ref[...])
        d = o[0:tq] - lam * o[tq:2 * tq]
        o_ref[0, pl.ds(pl.multiple_of(qi * tq, tq), tq), :] = (
            _rms(d, g_ref[...]) * (1.0 - lam_init)).astype(o_ref.dtype)
        _flash_reset(m_ref, acc_ref)

    _flash_stream(n, (qt_ref, kt_ref, lt_ref), 0, produce, consume, finish, (sa_ref, ma_ref), (sb_ref, mb_ref),
                  mask_at_produce=False)


_SMEM = pl.BlockSpec(memory_space=pltpu.SMEM)


def _diff_attention(proj3, diff_lambda, subln, lam_init):
    b, s, _ = proj3.shape
    tq = min(TQ_DENSE, s)
    dv = 2 * DIFF_DH
    n, sched = _causal_schedule(s // tq)
    kern = functools.partial(_diff_attn_kernel, tq=tq, n=n, lam_init=lam_init)
    return pl.pallas_call(
        kern,
        grid=(b, DIFF_HEADS),
        in_specs=[
            _SMEM, _SMEM, _SMEM, _SMEM,
            pl.BlockSpec((4, DIFF_DH), lambda bi, h: (0, 0)),
            pl.BlockSpec((1, dv), lambda bi, h: (0, 0)),
            pl.BlockSpec((1, s, LANES), lambda bi, h: (bi, 0, PB_AQ + h)),
            pl.BlockSpec((1, s, LANES), lambda bi, h: (bi, 0, PB_AK + h)),
            pl.BlockSpec((1, s, LANES), lambda bi, h: (bi, 0, PB_AV + h)),
        ],
        out_specs=pl.BlockSpec((1, s, dv), lambda bi, h: (bi, 0, h)),
        out_shape=jax.ShapeDtypeStruct((b, s, DIFF_HEADS * dv), CDT),
        scratch_shapes=_flash_scratch(2 * tq, tq),
        compiler_params=_cparams(("parallel", "parallel")),
        name="diff_attention",
    )(*sched, jnp.asarray(_alibi_slopes(DIFF_HEADS)), diff_lambda, subln.reshape(1, dv), proj3, proj3, proj3)


def _mla_prep_kernel(cq_ref, ckv_ref, kr_ref, krs_ref, gq_ref, gkv_ref, wqm_ref, wqs_ref, wk_ref, wv_ref,
                     cosq_ref, sinq_ref, cosk_ref, sink_ref, q_ref, k_ref, v_ref):
    hq = _rms(cq_ref[0].astype(F32), gq_ref[...]).astype(CDT)
    qm = _dot(hq, wqm_ref[...])
    qs = _dot(hq, wqs_ref[...])
    cosq, sinq = cosq_ref[...], sinq_ref[...]
    hw = 2 * LANES
    for h in range(MLA_HEADS):
        sl = slice(h * hw, (h + 1) * hw)
        q_ref[0, :, sl] = (qm[:, sl] * cosq + qs[:, sl] * sinq).astype(q_ref.dtype)
    hkv = _rms(ckv_ref[0].astype(F32), gkv_ref[...]).astype(CDT)
    kn = _dot(hkv, wk_ref[...])
    v_ref[0] = _dot(hkv, wv_ref[...]).astype(v_ref.dtype)
    kpe = (kr_ref[0].astype(F32) * cosk_ref[...] + krs_ref[0].astype(F32) * sink_ref[...]).astype(k_ref.dtype)
    for h in range(MLA_HEADS):
        k_ref[0, :, h * hw:h * hw + LANES] = kn[:, h * LANES:(h + 1) * LANES].astype(k_ref.dtype)
        k_ref[0, :, h * hw + LANES:(h + 1) * hw] = kpe


def _mla_prep(proj3, gq, gkv, wqm, wqs, wk, wv, tabs):
    b, s, _ = proj3.shape
    tm = min(TM_PROJ, s)
    hw = 2 * LANES
    cosq, sinq, cosk, sink = tabs
    const = lambda shape: pl.BlockSpec(shape, lambda bi, i: (0,) * len(shape))
    return pl.pallas_call(
        _mla_prep_kernel,
        grid=(b, s // tm),
        in_specs=[
            pl.BlockSpec((1, tm, MLA_Q_LORA), lambda bi, i: (bi, i, PB_BCQ // 2)),
            pl.BlockSpec((1, tm, MLA_KV_LORA), lambda bi, i: (bi, i, PB_BCKV // 2)),
            pl.BlockSpec((1, tm, LANES), lambda bi, i: (bi, i, PB_BKR)),
            pl.BlockSpec((1, tm, LANES), lambda bi, i: (bi, i, PB_BKRS)),
            const((1, MLA_Q_LORA)), const((1, MLA_KV_LORA)),
            const((MLA_Q_LORA, MLA_HEADS * hw)), const((MLA_Q_LORA, MLA_HEADS * hw)),
            const((MLA_KV_LORA, MLA_HEADS * MLA_NOPE)), const((MLA_KV_LORA, MLA_HEADS * MLA_VDIM)),
            pl.BlockSpec((tm, hw), lambda bi, i: (i, 0)), pl.BlockSpec((tm, hw), lambda bi, i: (i, 0)),
            pl.BlockSpec((tm, LANES), lambda bi, i: (i, 0)), pl.BlockSpec((tm, LANES), lambda bi, i: (i, 0)),
        ],
        out_specs=[
            pl.BlockSpec((1, tm, MLA_HEADS * hw), lambda bi, i: (bi, i, 0)),
            pl.BlockSpec((1, tm, MLA_HEADS * hw), lambda bi, i: (bi, i, 0)),
            pl.BlockSpec((1, tm, MLA_HEADS * MLA_VDIM), lambda bi, i: (bi, i, 0)),
        ],
        out_shape=[
            jax.ShapeDtypeStruct((b, s, MLA_HEADS * hw), CDT),
            jax.ShapeDtypeStruct((b, s, MLA_HEADS * hw), CDT),
            jax.ShapeDtypeStruct((b, s, MLA_HEADS * MLA_VDIM), CDT),
        ],
        compiler_params=_cparams(("parallel", "parallel")),
        name="mla_prep",
    )(proj3, proj3, proj3, proj3, gq.reshape(1, -1), gkv.reshape(1, -1), wqm, wqs, wk, wv,
      cosq, sinq, cosk, sink)


def _plain_attn_kernel(qt_ref, kt_ref, lt_ref, q_ref, k_ref, v_ref, o_ref,
                       m_ref, acc_ref, sa_ref, sb_ref, ma_ref, mb_ref, *, tq, n, hp, dk, dv):
    _flash_reset(m_ref, acc_ref)
    heads = [(slice(h * tq, (h + 1) * tq), slice(h * dk, (h + 1) * dk), slice(h * dv, (h + 1) * dv))
             for h in range(hp)]

    def produce(buf, qi, ki, diag):
        q, k = _tile(q_ref, qi, tq), _tile(k_ref, ki, tq)
        for rows, kcols, _ in heads:
            _put_logits(buf, _dot_nt(q[:, kcols], k[:, kcols]), rows)

    def consume(buf, ki, diag):
        v = _tile(v_ref, ki, tq)
        for rows, _, vcols in heads:
            _flash_consume(buf, v[:, vcols], m_ref, acc_ref, tq if diag else None, rows)

    def finish(qi):
        for rows, _, vcols in heads:
            o_ref[0, pl.ds(pl.multiple_of(qi * tq, tq), tq), vcols] = _flash_result(acc_ref[rows]).astype(o_ref.dtype)
        _flash_reset(m_ref, acc_ref)

    _flash_stream(n, (qt_ref, kt_ref, lt_ref), 0, produce, consume, finish, (sa_ref, ma_ref), (sb_ref, mb_ref),
                  mask_at_produce=False)


def _mla_attention(qc, kc, v):
    b, s, _ = qc.shape
    tq = min(TQ_DENSE, s)
    hw = 2 * LANES
    hp = MLA_HP
    n, sched = _causal_schedule(s // tq)
    return pl.pallas_call(
        functools.partial(_plain_attn_kernel, tq=tq, n=n, hp=hp, dk=hw, dv=MLA_VDIM),
        grid=(b, MLA_HEADS // hp),
        in_specs=[
            _SMEM, _SMEM, _SMEM,
            pl.BlockSpec((1, s, hp * hw), lambda bi, h: (bi, 0, h)),
            pl.BlockSpec((1, s, hp * hw), lambda bi, h: (bi, 0, h)),
            pl.BlockSpec((1, s, hp * MLA_VDIM), lambda bi, h: (bi, 0, h)),
        ],
        out_specs=pl.BlockSpec((1, s, hp * MLA_VDIM), lambda bi, h: (bi, 0, h)),
        out_shape=jax.ShapeDtypeStruct((b, s, MLA_HEADS * MLA_VDIM), CDT),
        scratch_shapes=_flash_scratch(hp * tq, tq),
        compiler_params=_cparams(("parallel", "parallel")),
        name="mla_attention",
    )(*sched, qc, kc, v)


def _fox_attn_kernel(qt_ref, kt_ref, lt_ref, c_ref, q_ref, k_ref, v_ref, o_ref,
                     m_ref, acc_ref, sa_ref, sb_ref, ma_ref, mb_ref, *, tq, n, hp):
    _flash_reset(m_ref, acc_ref)
    heads = [(slice(h * tq, (h + 1) * tq), slice(h * FOX_DH, (h + 1) * FOX_DH)) for h in range(hp)]

    def produce(buf, qi, ki, diag):
        q, k = _tile(q_ref, qi, tq), _tile(k_ref, ki, tq)
        for h, (rows, cols) in enumerate(heads):
            cbase = c_ref[0, h, pl.ds(qi, 1), :][:, 0:1]
            s = _dot_nt(q[:, cols], k[:, cols]) + LOG2E * (cbase - c_ref[0, h, pl.ds(ki, 1), :])
            _put_logits(buf, s, rows)

    def consume(buf, ki, diag):
        v = _tile(v_ref, ki, tq)
        for rows, cols in heads:
            _flash_consume(buf, v[:, cols], m_ref, acc_ref, tq if diag else None, rows)

    def finish(qi):
        for rows, cols in heads:
            o_ref[0, pl.ds(pl.multiple_of(qi * tq, tq), tq), cols] = _flash_result(acc_ref[rows]).astype(o_ref.dtype)
        _flash_reset(m_ref, acc_ref)

    _flash_stream(n, (qt_ref, kt_ref, lt_ref), 0, produce, consume, finish, (sa_ref, ma_ref), (sb_ref, mb_ref),
                  mask_at_produce=False)


def _fox_attention(proj3, c4):
    b, s, _ = proj3.shape
    tq = min(TQ_DENSE, s)
    nk = s // tq
    hp = FOX_HP
    w = hp * FOX_DH
    n, sched = _causal_schedule(nk)
    return pl.pallas_call(
        functools.partial(_fox_attn_kernel, tq=tq, n=n, hp=hp),
        grid=(b, FOX_HEADS // hp),
        in_specs=[
            _SMEM, _SMEM, _SMEM,
            pl.BlockSpec((1, hp, nk, tq), lambda bi, h: (bi, h, 0, 0)),
            pl.BlockSpec((1, s, w), lambda bi, h: (bi, 0, PB_CQ // hp + h)),
            pl.BlockSpec((1, s, w), lambda bi, h: (bi, 0, PB_CK // hp + h)),
            pl.BlockSpec((1, s, w), lambda bi, h: (bi, 0, PB_CV // hp + h)),
        ],
        out_specs=pl.BlockSpec((1, s, w), lambda bi, h: (bi, 0, h)),
        out_shape=jax.ShapeDtypeStruct((b, s, FOX_HEADS * FOX_DH), CDT),
        scratch_shapes=_flash_scratch(hp * tq, tq),
        compiler_params=_cparams(("parallel", "parallel")),
        name="fox_attention",
    )(*sched, c4.reshape(b, FOX_HEADS, nk, tq), proj3, proj3, proj3)


def _nsa_compress_kernel(x_ref, w1a_ref, w1b_ref, pea_ref, peb_ref, w2_ref, o_ref):
    x = x_ref[0]
    n = x.shape[0]
    pa = _dot(x, w1a_ref[...])
    pb = _dot(x, w1b_ref[...])
    pe = _dot(pea_ref[...], w1a_ref[...]) + _dot(peb_ref[...], w1b_ref[...])
    hid = pa + pltpu.roll(pb, n - 1, axis=0) + pe[0:1]
    act = 0.5 * hid * (1.0 + jnp.tanh(math.sqrt(2.0 / math.pi) * (hid + 0.044715 * hid * hid * hid)))
    o_ref[0] = _dot(act.astype(CDT), w2_ref[...]).astype(o_ref.dtype)


def _nsa_compress(xc, w1a, w1b, pea, peb, w2):
    b, n, kdim = xc.shape
    hdim = w1a.shape[1]
    const = lambda shape: pl.BlockSpec(shape, lambda bi: (0,) * len(shape))
    return pl.pallas_call(
        _nsa_compress_kernel,
        grid=(b,),
        in_specs=[pl.BlockSpec((1, n, kdim), lambda bi: (bi, 0, 0)),
                  const((kdim, hdim)), const((kdim, hdim)), const((8, kdim)), const((8, kdim)),
                  const((hdim, w2.shape[1]))],
        out_specs=pl.BlockSpec((1, n, w2.shape[1]), lambda bi: (bi, 0, 0)),
        out_shape=jax.ShapeDtypeStruct((b, n, w2.shape[1]), CDT),
        compiler_params=_cparams(("parallel",)),
        name="nsa_compress",
    )(xc, w1a, w1b, pea, peb, w2)


def _nsa_cmp_kernel(slopes_ref, q_ref, kv_ref, oc_ref, sb_ref, used_ref, *, tq, n_topk):
    qi = pl.program_id(1)
    nblk = kv_ref.shape[1]
    q0 = qi * tq
    rowpos = q0 + lax.broadcasted_iota(jnp.int32, (tq, 1), 0)
    cmp_end = lax.broadcasted_iota(jnp.int32, (1, nblk), 1) * CMP_STRIDE + (CMP_LEN - 1)
    negmask = jnp.where(rowpos >= cmp_end, 0.0, NEG)
    end_rel = (cmp_end - q0).astype(F32)
    lane = lax.broadcasted_iota(jnp.int32, (tq, LANES), 1)
    low = lane < NSA_DH
    nn = lax.broadcasted_iota(jnp.int32, (NSA_DH, nblk), 1) * CMP_STRIDE
    jj = lax.broadcasted_iota(jnp.int32, (NSA_DH, nblk), 0) * SLC_LEN
    ovt = (jnp.maximum(jnp.minimum(nn + CMP_LEN, jj + SLC_LEN) - jnp.maximum(nn, jj), 0).astype(F32)
           * (1.0 / CMP_LEN)).astype(CDT)
    jt = lax.broadcasted_iota(jnp.int32, (NSA_DH, tq), 0).astype(F32)
    blk = ((q0 + lax.broadcasted_iota(jnp.int32, (1, tq), 1)) >> SLC_SHIFT).astype(F32)
    fixed = (jt == 0.0) | (jt == blk) | (jt == blk - 1.0)
    out_of_play = fixed | (jt > blk)
    row_ok = rowpos >= CMP_LEN - 1
    outs = []
    bias = []
    for g in range(NSA_GROUPS):
        kc = kv_ref[0, :, g * LANES:(g + 1) * LANES]
        vc = kv_ref[0, :, (NSA_GROUPS + g) * LANES:(NSA_GROUPS + g + 1) * LANES]
        psum = jnp.zeros((tq, nblk), F32)
        mine = low if g == 0 else jnp.logical_not(low)
        zero = jnp.zeros((tq, LANES), q_ref.dtype)
        qs = jnp.concatenate([jnp.where(mine, q_ref[0, :, j * LANES:(j + 1) * LANES], zero)
                              for j in range(NSA_HPG)], axis=0)
        s_all = _dot_nt(qs, kc)
        ps = []
        for j in range(NSA_HPG):
            s = s_all[j * tq:(j + 1) * tq] + slopes_ref[g * NSA_HPG + j] * end_rel + negmask
            e = jnp.exp2(s - jnp.max(s, axis=-1, keepdims=True))
            den = jnp.sum(e, axis=-1, keepdims=True)
            p = e * jnp.where(row_ok, 1.0 / den, 0.0)
            psum = psum + p
            ps.append(p.astype(CDT))
        o_all = _dot(jnp.concatenate(ps, axis=0), vc)
        outs.extend(o_all[j * tq:(j + 1) * tq] for j in range(NSA_HPG))
        hi = psum.astype(CDT)
        lo = (psum - hi.astype(F32)).astype(CDT)
        imp = _dot_nt(ovt, hi) + _dot_nt(ovt, lo)
        imp = jnp.where(out_of_play, -jnp.inf, imp)
        sbt = jnp.where(fixed, 0.0, NEG)
        for _ in range(n_topk - 3):
            mx = jnp.max(imp, axis=0, keepdims=True)
            idx = jnp.min(jnp.where(imp == mx, jt, float(LANES)), axis=0, keepdims=True)
            hit = jt == idx
            sbt = jnp.where(hit, 0.0, sbt)
            imp = jnp.where(hit, -jnp.inf, imp)
        bias.append(sbt)
    sb = jnp.concatenate([bias[1], bias[0]], axis=0).T
    sb_ref[0] = sb.astype(sb_ref.dtype)
    used = jnp.max(jnp.where(sb == 0.0, 1.0, 0.0), axis=0, keepdims=True)
    used_ref[0, 0] = jnp.broadcast_to(used, used_ref.shape[2:])
    for blk_i in range(NSA_HEADS // 2):
        oc_ref[0, :, blk_i * LANES:(blk_i + 1) * LANES] = jnp.where(
            low, outs[2 * blk_i], outs[2 * blk_i + 1]).astype(oc_ref.dtype)


def _nsa_cmp_select(proj3, kvc, n_topk):
    assert n_topk >= 3, "the three always-selected blocks must fit in the top-k budget"
    b, s, _ = proj3.shape
    tq = min(TQ_NSA, s)
    nblk = kvc.shape[1]
    return pl.pallas_call(
        functools.partial(_nsa_cmp_kernel, tq=tq, n_topk=n_topk),
        grid=(b, s // tq),
        in_specs=[
            pl.BlockSpec(memory_space=pltpu.SMEM),
            pl.BlockSpec((1, tq, 4 * LANES), lambda bi, qi: (bi, qi, PB_DQ // 4)),
            pl.BlockSpec((1, nblk, kvc.shape[2]), lambda bi, qi: (bi, 0, 0)),
        ],
        out_specs=[
            pl.BlockSpec((1, tq, NSA_HEADS * NSA_DH), lambda bi, qi: (bi, qi, 0)),
            pl.BlockSpec((1, tq, LANES), lambda bi, qi: (bi, qi, 0)),
            pl.BlockSpec((1, 1, 8, LANES), lambda bi, qi: (bi, qi, 0, 0)),
        ],
        out_shape=[jax.ShapeDtypeStruct((b, s, NSA_HEADS * NSA_DH), CDT),
                   jax.ShapeDtypeStruct((b, s, LANES), CDT),
                   jax.ShapeDtypeStruct((b, s // tq, 8, LANES), F32)],
        compiler_params=_cparams(("parallel", "parallel")),
        name="nsa_cmp_select",
    )(jnp.asarray(_alibi_slopes(NSA_HEADS)), proj3, kvc)


def _compact_heads(heads, mine, low):
    both = [jnp.where(mine, a, pltpu.roll(a, NSA_DH, axis=1)) for a in heads]
    out = [jnp.where(low, both[2 * jj], both[2 * jj + 1]) for jj in range(NSA_HPG // 2)]
    return jnp.concatenate(out, axis=1)


def _nsa_win_kernel(slopes_ref, q_ref, kp_ref, kc_ref, vp_ref, vc_ref, o_ref, *, tq):
    qi = pl.program_id(1)
    lane = lax.broadcasted_iota(jnp.int32, (tq, LANES), 1)
    low = lane < NSA_DH
    r = lax.broadcasted_iota(jnp.int32, (tq, tq), 0)
    c = lax.broadcasted_iota(jnp.int32, (tq, tq), 1)
    own = c <= r
    ndist = jnp.where(own, c - r, c - r - tq).astype(F32)
    own_f = jnp.where(own, 1.0, 0.0).astype(CDT)
    prev_pen = jnp.where(qi > 0, 0.0, NEG)
    q = q_ref[0]
    zero = jnp.zeros((tq, LANES), q.dtype)
    mine = (low, jnp.logical_not(low))
    qs = jnp.concatenate([jnp.where(mine[g], q[:, j * LANES:(j + 1) * LANES], zero)
                          for g in range(NSA_GROUPS) for j in range(NSA_HPG)], axis=0)
    s_own, s_prev = _dot_nt(qs, kc_ref[0]), _dot_nt(qs, kp_ref[0])
    ps = []
    for hd in range(NSA_HEADS):
        rows = slice(hd * tq, (hd + 1) * tq)
        s = jnp.where(own, s_own[rows], s_prev[rows] + prev_pen) + slopes_ref[hd] * ndist
        ps.append(jnp.exp2(s - jnp.max(s, axis=-1, keepdims=True)).astype(CDT))
    p = jnp.concatenate(ps, axis=0)
    p_own = p * jnp.tile(own_f, (NSA_HEADS, 1))
    o = _flash_result(_dot(p_own, _with_ones(vc_ref[0])) + _dot(p - p_own, _with_ones(vp_ref[0])))
    for g in range(NSA_GROUPS):
        heads = [o[(g * NSA_HPG + j) * tq:(g * NSA_HPG + j + 1) * tq] for j in range(NSA_HPG)]
        w = NSA_HPG * NSA_DH
        o_ref[0, :, g * w:(g + 1) * w] = _compact_heads(heads, mine[g], low).astype(o_ref.dtype)


def _nsa_window(proj3):
    b, s, _ = proj3.shape
    tq = WINDOW
    return pl.pallas_call(
        functools.partial(_nsa_win_kernel, tq=tq),
        grid=(b, s // tq),
        in_specs=[
            pl.BlockSpec(memory_space=pltpu.SMEM),
            pl.BlockSpec((1, tq, 4 * LANES), lambda bi, qi: (bi, qi, PB_DQ // 4)),
            pl.BlockSpec((1, tq, LANES), lambda bi, qi: (bi, jnp.maximum(qi - 1, 0), PB_WIN_K)),
            pl.BlockSpec((1, tq, LANES), lambda bi, qi: (bi, qi, PB_WIN_K)),
            pl.BlockSpec((1, tq, LANES), lambda bi, qi: (bi, jnp.maximum(qi - 1, 0), PB_WIN_V)),
            pl.BlockSpec((1, tq, LANES), lambda bi, qi: (bi, qi, PB_WIN_V)),
        ],
        out_specs=pl.BlockSpec((1, tq, NSA_HEADS * NSA_DH), lambda bi, qi: (bi, qi, 0)),
        out_shape=jax.ShapeDtypeStruct((b, s, NSA_HEADS * NSA_DH), CDT),
        compiler_params=_cparams(("parallel", "parallel")),
        name="nsa_window",
    )(jnp.asarray(_alibi_slopes(NSA_HEADS)), proj3, proj3, proj3, proj3, proj3)


def _nsa_sel_kernel(cnt_ref, qt_ref, kt_ref, lt_ref, slopes_ref, q_ref, sb_ref, k_ref, v_ref, oc_ref, ow_ref, gl_ref,
                    e_ref, o_ref, m_ref, acc_ref, sa_ref, sb2_ref, ma_ref, mb_ref, cm_ref, *, tq, rows_per_problem):
    g = pl.program_id(1)
    w = NSA_HPG * NSA_DH
    lane = lax.broadcasted_iota(jnp.int32, (tq, LANES), 1)
    low = lane < NSA_DH
    mine = (lane >> HALF_SHIFT) == g
    _flash_begin(m_ref, acc_ref, cm_ref, tq)
    col = lax.broadcasted_iota(jnp.int32, (1, tq), 1).astype(F32)
    jl = lane & (NSA_DH - 1)
    krow = lax.broadcasted_iota(jnp.int32, (tq, LANES), 0)

    def produce(buf, qi, ki, diag):
        q = _tile(q_ref, qi, tq)
        sb = _tile(sb_ref, qi, tq)
        qa = jnp.concatenate([jnp.where(mine, q[:, j * LANES:(j + 1) * LANES], sb) for j in range(NSA_HPG)], axis=0)
        k = _tile(k_ref, ki, tq)
        onehot = jnp.where(((ki * tq + krow) >> SLC_SHIFT) == jl, 1.0, 0.0).astype(k.dtype)
        s_all = _dot_nt(qa, jnp.where(mine, k, onehot))
        rel = ((ki - qi) * tq).astype(F32)
        for j in range(NSA_HPG):
            rows = slice(j * tq, (j + 1) * tq)
            _put_logits(buf, s_all[rows] + slopes_ref[g * NSA_HPG + j] * (col + rel), rows, diag, cm_ref)

    def consume(buf, ki, diag):
        _flash_consume(buf, _tile(v_ref, ki, tq), m_ref, acc_ref, tq if diag else None)

    def finish(qi):
        o = _flash_result(acc_ref[...])
        o_s = _compact_heads([o[j * tq:(j + 1) * tq] for j in range(NSA_HPG)], mine, low)
        gates = _split_dot(_sigmoid(_tile(gl_ref, qi, tq)), e_ref[0])
        y = (gates[:, 0:w] * _tile(oc_ref, qi, tq).astype(F32) + gates[:, w:2 * w] * o_s
             + gates[:, 2 * w:3 * w] * _tile(ow_ref, qi, tq).astype(F32))
        o_ref[0, pl.ds(pl.multiple_of(qi * tq, tq), tq), :] = y.astype(o_ref.dtype)
        _flash_reset(m_ref, acc_ref)

    prob = pl.program_id(0) * NSA_GROUPS + g
    _flash_stream(cnt_ref[prob], (qt_ref, kt_ref, lt_ref), prob * rows_per_problem, produce, consume, finish,
                  (sa_ref, ma_ref), (sb2_ref, mb_ref), mask_at_produce=True)


def _nsa_selected(proj3, sbias, used, o_c, o_w, small3, expand):
    b, s, _ = proj3.shape
    tq = min(TQ_NSA, s)
    nq = s // tq
    w = NSA_HPG * NSA_DH
    u = used[:, :, 0, :].reshape(b, nq, NSA_GROUPS, NSA_DH)[:, :, ::-1, :nq * (tq // SLC_LEN)]
    flags = (u.reshape(b, nq, NSA_GROUPS, nq, tq // SLC_LEN).max(axis=-1) > 0.0).astype(jnp.int32)
    flags = flags.transpose(0, 2, 1, 3)
    qt = jnp.arange(nq, dtype=jnp.int32)
    need = jnp.where(qt[None, :] < qt[:, None], flags, (qt[None, :] == qt[:, None]).astype(jnp.int32))
    need = need.reshape(b, NSA_GROUPS, nq * nq)
    cnt = need.sum(axis=-1).astype(jnp.int32)
    order = jnp.argsort(1 - need, axis=-1, stable=True).astype(jnp.int32)
    order = jnp.pad(order, ((0, 0), (0, 0), (0, 2)))
    rows = nq * nq + 2
    sched = (order // nq, order % nq, (order // nq == order % nq).astype(jnp.int32))
    return pl.pallas_call(
        functools.partial(_nsa_sel_kernel, tq=tq, rows_per_problem=rows),
        grid=(b, NSA_GROUPS),
        in_specs=[
            _SMEM, _SMEM, _SMEM, _SMEM, _SMEM,
            pl.BlockSpec((1, s, 4 * LANES), lambda bi, g: (bi, 0, PB_DQ // 4)),
            pl.BlockSpec((1, s, LANES), lambda bi, g: (bi, 0, 0)),
            pl.BlockSpec((1, s, LANES), lambda bi, g: (bi, 0, PB_SEL_K)),
            pl.BlockSpec((1, s, LANES), lambda bi, g: (bi, 0, PB_SEL_V)),
            pl.BlockSpec((1, s, w), lambda bi, g: (bi, 0, g)),
            pl.BlockSpec((1, s, w), lambda bi, g: (bi, 0, g)),
            pl.BlockSpec((1, s, LANES), lambda bi, g: (bi, 0, 0)),
            pl.BlockSpec((1, LANES, 3 * w), lambda bi, g: (g, 0, 0)),
        ],
        out_specs=pl.BlockSpec((1, s, w), lambda bi, g: (bi, 0, g)),
        out_shape=jax.ShapeDtypeStruct((b, s, NSA_HEADS * NSA_DH), CDT),
        scratch_shapes=_flash_scratch(NSA_HPG * tq, tq, mask_scratch=True),
        compiler_params=_cparams(("parallel", "parallel")),
        name="nsa_selected",
    )(cnt.reshape(-1), *[t.reshape(-1) for t in sched], jnp.asarray(_alibi_slopes(NSA_HEADS)),
      proj3, sbias, proj3, proj3, o_c, o_w, small3, expand)


def _merge_kernel(ya_ref, yb_ref, yc_ref, yd_ref, ga_ref, gb_ref, gc_ref, gd_ref, wb_ref, wo_ref, x_ref, o_ref):
    merged = None
    for n, (y_ref, g_ref) in enumerate(((ya_ref, ga_ref), (yb_ref, gb_ref), (yc_ref, gc_ref), (yd_ref, gd_ref))):
        t = _sigmoid(g_ref[...].astype(F32)) * _dot(y_ref[...], wb_ref[n])
        merged = t if merged is None else merged + t
    o_ref[...] = x_ref[...] + _dot(merged.astype(CDT), wo_ref[...])


def _merge(ys, proj2, wb, wo, x2, layer):
    t, d = x2.shape
    tm = min(TM_ROWS, t)
    gate_blk = PB_GATE * LANES // d
    yspec = pl.BlockSpec((tm, BRANCH_WIDTH), lambda i: (i, 0))
    gspecs = [pl.BlockSpec((tm, d), functools.partial(lambda i, n: (i, gate_blk + n), n=n)) for n in range(N_BRANCH)]
    return pl.pallas_call(
        _merge_kernel,
        grid=(t // tm,),
        in_specs=[yspec] * N_BRANCH + gspecs + [
            pl.BlockSpec((None, N_BRANCH, BRANCH_WIDTH, d), lambda i: (layer, 0, 0, 0)),
            pl.BlockSpec((None, d, d), lambda i: (layer, 0, 0)),
            pl.BlockSpec((tm, d), lambda i: (i, 0)),
        ],
        out_specs=pl.BlockSpec((tm, d), lambda i: (i, 0)),
        out_shape=jax.ShapeDtypeStruct((t, d), F32),
        compiler_params=_cparams(("parallel",)),
        name="merge",
    )(*ys, proj2, proj2, proj2, proj2, wb, wo, x2)


HALO = 16


def _ffn_kernel(x_ref, xh_ref, g_ref, wu_ref, cw_ref, cb_ref, wd_ref, gf_ref, o_ref, he_ref, u_ref, act_ref,
                *, tm, fc, final):
    i = pl.program_id(1)
    x = x_ref[0]
    g = g_ref[...]
    xh = xh_ref[0] * (i > 0).astype(F32)
    he_ref[0:HALO] = _rms(xh, g).astype(CDT)
    he_ref[HALO:HALO + tm] = _rms(x, g).astype(CDT)
    he = he_ref[...]
    for c in range(D_FF // fc):
        outs = []
        for half in range(2):
            ub = u_ref.at[c % 2, half]
            lo = half * D_FF + c * fc
            ub[...] = _dot(he, wu_ref[:, lo:lo + fc])
            conv = cb_ref[:, lo:lo + fc]
            for kk in range(CONV_WIDTH):
                off = HALO - (CONV_WIDTH - 1) + kk
                conv = conv + cw_ref[kk:kk + 1, lo:lo + fc] * ub[off:off + tm, :]
            outs.append(conv)
        a, gg = outs
        act_ref[:, c * fc:(c + 1) * fc] = (a * _sigmoid(a) * gg).astype(CDT)
    y = x + _dot(act_ref[...], wd_ref[...])
    if final:
        y = _rms(y, gf_ref[...])
    o_ref[0] = y


def _ffn(x3, g, wu, cw, cb, wd, gf, layer, final):
    b, s, d = x3.shape
    tm = min(TM_ROWS, s)
    fc = FFN_CHUNK
    assert D_FF % fc == 0
    const = lambda shape: pl.BlockSpec(shape, lambda bi, i: (0,) * len(shape), pipeline_mode=pl.Buffered(1))
    stacked = lambda shape: pl.BlockSpec((None,) + shape, lambda bi, i: (layer,) + (0,) * len(shape),
                                         pipeline_mode=pl.Buffered(1))
    return pl.pallas_call(
        functools.partial(_ffn_kernel, tm=tm, fc=fc, final=final),
        grid=(b, s // tm),
        in_specs=[
            pl.BlockSpec((1, tm, d), lambda bi, i: (bi, i, 0)),
            pl.BlockSpec((1, HALO, d), lambda bi, i: (bi, jnp.maximum(i * (tm // HALO) - 1, 0), 0)),
            const((1, d)), stacked((d, 2 * D_FF)), const((CONV_WIDTH, 2 * D_FF)), const((1, 2 * D_FF)),
            stacked((D_FF, d)), const((1, d)),
        ],
        out_specs=pl.BlockSpec((1, tm, d), lambda bi, i: (bi, i, 0)),
        out_shape=jax.ShapeDtypeStruct((b, s, d), F32),
        scratch_shapes=[pltpu.VMEM((tm + HALO, d), CDT), pltpu.VMEM((2, 2, tm + HALO, fc), F32),
                        pltpu.VMEM((tm, D_FF), CDT)],
        compiler_params=_cparams(("parallel", "arbitrary")),
        name="conv_glu_mlp",
    )(x3, x3, g.reshape(1, d), wu, cw, cb.reshape(1, -1), wd, gf.reshape(1, d))


def _w_in_plan():
    widths = (512, 512, 512, MLA_Q_LORA, MLA_KV_LORA, MLA_ROPE, 512, 512, 512, FOX_HEADS,
              512, 768, 3 * NSA_HEADS, N_BRANCH * D_MODEL)
    (a_q, a_k, a_v, b_cq, b_ckv, b_kr, c_q, c_k, c_v, c_f, d_q, d_kv, d_g, gate, _) = np.cumsum((0,) + widths).tolist()
    half = MLA_ROPE // 2

    def run(src, nblocks, scale=1.0):
        return [[(src + i * LANES, LANES, scale)] for i in range(nblocks)]

    blocks = (run(a_q, 4, LOG2E * DIFF_DH ** -0.5) + run(a_k, 4) + run(a_v, 4)
              + run(c_q, 4, LOG2E * FOX_DH ** -0.5) + run(c_k, 4) + run(c_v, 4))
    sd = LOG2E * NSA_DH ** -0.5
    blocks += [[(d_q + j * NSA_DH, NSA_DH, sd), (d_q + (NSA_HPG + j) * NSA_DH, NSA_DH, sd)] for j in range(NSA_HPG)]
    blocks += run(d_kv, 6) + run(b_cq, 2) + run(b_ckv, 2)
    blocks += [[(b_kr, MLA_ROPE, 1.0), None],
               [(b_kr + half, half, -1.0), (b_kr, half, 1.0), None]]
    blocks += run(gate, N_BRANCH * D_MODEL // LANES)
    assert len(blocks) * LANES == N_PROJ
    small = [(c_f, FOX_HEADS, 1.0), (d_g, 3 * NSA_HEADS, 1.0), None]
    return blocks, small


def _w_in_relayout_kernel(w_ref, big_ref, small_ref):
    rows = w_ref.shape[1]
    blocks, small = _w_in_plan()

    def assemble(pieces):
        parts, used = [], 0
        for p in pieces:
            if p is None:
                parts.append(jnp.zeros((rows, LANES - used), F32))
            else:
                src, width, scale = p
                v = w_ref[0, :, src:src + width]
                parts.append(v if scale == 1.0 else v * scale)
                used += width
        return parts[0] if len(parts) == 1 else jnp.concatenate(parts, axis=1)

    for j, pieces in enumerate(blocks):
        big_ref[0, :, j * LANES:(j + 1) * LANES] = assemble(pieces).astype(big_ref.dtype)
    small_ref[0] = assemble(small).astype(small_ref.dtype)


def _w_in_relayout(w):
    nl, d, n = w.shape
    tr = min(256, d)
    return pl.pallas_call(
        _w_in_relayout_kernel,
        grid=(nl, d // tr),
        in_specs=[pl.BlockSpec((1, tr, n), lambda l, i: (l, i, 0))],
        out_specs=[pl.BlockSpec((1, tr, N_PROJ), lambda l, i: (l, i, 0)),
                   pl.BlockSpec((1, tr, LANES), lambda l, i: (l, i, 0))],
        out_shape=[jax.ShapeDtypeStruct((nl, d, N_PROJ), CDT), jax.ShapeDtypeStruct((nl, d, LANES), CDT)],
        compiler_params=_cparams(("parallel", "parallel")),
        name="w_in_relayout",
    )(w)


def _prep_mla(w_uq, w_ukv):
    r = w_uq.shape[0]
    hw = 2 * LANES
    half = MLA_ROPE // 2
    scale = LOG2E * (MLA_NOPE + MLA_ROPE) ** -0.5
    wq = (w_uq * scale).reshape(r, MLA_HEADS, MLA_NOPE + MLA_ROPE)
    nope, t1, t2 = wq[..., :MLA_NOPE], wq[..., MLA_NOPE:MLA_NOPE + half], wq[..., MLA_NOPE + half:]
    zpad = jnp.zeros((r, MLA_HEADS, hw - MLA_NOPE - MLA_ROPE), w_uq.dtype)
    wqm = jnp.concatenate([nope, t1, t2, zpad], axis=-1).reshape(r, MLA_HEADS * hw)
    wqs = jnp.concatenate([jnp.zeros_like(nope), -t2, t1, zpad], axis=-1).reshape(r, MLA_HEADS * hw)
    wkv = w_ukv.reshape(w_ukv.shape[0], MLA_HEADS, MLA_NOPE + MLA_VDIM)
    wk = wkv[..., :MLA_NOPE].reshape(-1, MLA_HEADS * MLA_NOPE)
    wv = wkv[..., MLA_NOPE:].reshape(-1, MLA_HEADS * MLA_VDIM)
    return wqm.astype(CDT), wqs.astype(CDT), wk.astype(CDT), wv.astype(CDT)


def _rope_tables(s):
    half = MLA_ROPE // 2
    inv_freq = ROPE_THETA ** (-jnp.arange(0, MLA_ROPE, 2, dtype=F32) / MLA_ROPE)
    ang = jnp.arange(s, dtype=F32)[:, None] * inv_freq[None, :]
    cos, sin = jnp.cos(ang), jnp.sin(ang)
    z = jnp.zeros((s, LANES - MLA_ROPE), F32)
    cosk = jnp.concatenate([cos, cos, z], axis=1)
    sink = jnp.concatenate([sin, sin, z], axis=1)
    cosq = jnp.concatenate([jnp.ones((s, MLA_NOPE), F32), cosk], axis=1)
    sinq = jnp.concatenate([jnp.zeros((s, MLA_NOPE), F32), sink], axis=1)
    return cosq, sinq, cosk, sink


def _prep_compress(pe, w1, w2):
    eye2 = jnp.eye(2, dtype=F32)
    w1r = w1.reshape(2, CMP_LEN, NSA_DH, CMP_HIDDEN).astype(CDT)
    same = np.eye(2, dtype=bool)
    diag_kg = jnp.asarray(same[:, None, :, None] & same[None, :, None, :])

    def expand(wpart):
        src = wpart.transpose(1, 0, 2, 3)[:, :, None, :, None, None, :]
        t = jnp.where(diag_kg[None, :, :, None, :, :, None], src, jnp.zeros((), CDT))
        return t.reshape(CMP_STRIDE * 4 * NSA_DH, 4 * CMP_HIDDEN)

    w1a, w1b = expand(w1r[:, :CMP_STRIDE]), expand(w1r[:, CMP_STRIDE:])

    def pe_row(p):
        t = jnp.broadcast_to(p.transpose(1, 0, 2)[:, :, None, :], (CMP_STRIDE, 2, NSA_GROUPS, NSA_DH))
        return jnp.pad(t.reshape(1, -1), ((0, 7), (0, 0)))

    pea, peb = pe_row(pe[:, :CMP_STRIDE]), pe_row(pe[:, CMP_STRIDE:])
    w2b = jnp.einsum('khd,kK,gG,u->kghKGud', w2, eye2, eye2, jnp.ones((2,), F32))
    w2b = w2b.reshape(4 * CMP_HIDDEN, 4 * 2 * NSA_DH)
    return w1a.astype(CDT), w1b.astype(CDT), pea.astype(CDT), peb.astype(CDT), w2b.astype(CDT)


def _gate_expand():
    e = np.zeros((NSA_GROUPS, LANES, 3, NSA_HPG, NSA_DH), np.float32)
    for g in range(NSA_GROUPS):
        for j in range(NSA_HPG):
            for br in range(3):
                e[g, SMALL_G + (g * NSA_HPG + j) * 3 + br, br, j, :] = 1.0
    return jnp.asarray(e.reshape(NSA_GROUPS, LANES, 3 * NSA_HPG * NSA_DH)).astype(CDT)


def _token_mixers(x3, l, norm_mix, w_in, diff_lambda, diff_subln, mla_norm_q, mla_w_uq, mla_norm_kv, mla_w_ukv,
                  fox_b_f, nsa_cmp_pe, nsa_cmp_w1, nsa_cmp_w2, w_branch, w_out, rope_tabs):
    b, s, d = x3.shape
    t = b * s
    x2 = x3.reshape(t, d)
    proj, small = _in_proj(x2, norm_mix, *w_in, l)
    proj3 = proj.reshape(b, s, N_PROJ)
    small3 = small.reshape(b, s, LANES)

    lam_init = 0.8 - 0.6 * math.exp(-0.3 * l)
    y_a = _diff_attention(proj3, diff_lambda, diff_subln, lam_init)

    wqm, wqs, wk, wv = _prep_mla(mla_w_uq, mla_w_ukv)
    qc, kc, vv = _mla_prep(proj3, mla_norm_q, mla_norm_kv, wqm, wqs, wk, wv, rope_tabs)
    y_b = _mla_attention(qc, kc, vv)

    cf_rows = small3[:, :, SMALL_F:SMALL_F + FOX_HEADS].transpose(0, 2, 1).reshape(b * FOX_HEADS, s)
    bias_rows = jnp.tile(fox_b_f.astype(F32), b).reshape(b * FOX_HEADS, 1)
    c4 = _fox_cumsum(cf_rows, bias_rows)
    y_c = _fox_attention(proj3, c4)

    w1a, w1b, pea, peb, w2b = _prep_compress(nsa_cmp_pe, nsa_cmp_w1, nsa_cmp_w2)
    xc = proj3[:, :, PB_CMP_K * LANES:(PB_CMP_V + 1) * LANES].reshape(b, s // CMP_STRIDE, CMP_STRIDE * 2 * LANES)
    kvc = _nsa_compress(xc, w1a, w1b, pea, peb, w2b)
    n_topk = min(SLC_TOPK, s // SLC_LEN)
    o_c, sbias, used = _nsa_cmp_select(proj3, kvc, n_topk)
    o_w = _nsa_window(proj3)
    y_d = _nsa_selected(proj3, sbias, used, o_c, o_w, small3, _gate_expand())

    ys = [y.reshape(t, BRANCH_WIDTH) for y in (y_a, y_b, y_c, y_d)]
    return _merge(ys, proj, w_branch, w_out, x2, l).reshape(b, s, d)


def kernel(x, norm_mix, w_in, diff_lambda, diff_subln, mla_norm_q, mla_w_uq, mla_norm_kv, mla_w_ukv, fox_b_f,
           nsa_cmp_pe, nsa_cmp_w1, nsa_cmp_w2, w_branch, w_out, norm_ffn, w_up, conv_w, conv_b, w_down, norm_final):
    depth = w_in.shape[0]
    s = x.shape[1]
    rope_tabs = _rope_tables(s)
    w_in = _w_in_relayout(w_in)
    w_up16, w_down16 = w_up.astype(CDT), w_down.astype(CDT)
    w_branch16, w_out16 = w_branch.astype(CDT), w_out.astype(CDT)
    for l in range(depth):
        x = _token_mixers(x, l, norm_mix[l], w_in, diff_lambda[l], diff_subln[l], mla_norm_q[l], mla_w_uq[l],
                          mla_norm_kv[l], mla_w_ukv[l], fox_b_f[l], nsa_cmp_pe[l], nsa_cmp_w1[l], nsa_cmp_w2[l],
                          w_branch16, w_out16, rope_tabs)
        x = _ffn(x, norm_ffn[l], w_up16, conv_w[l], conv_b[l], w_down16, norm_final, l, final=(l == depth - 1))
    return x
```

```python
import functools
import math

import numpy as np
import jax
import jax.numpy as jnp
from jax import lax
from jax.experimental import pallas as pl
from jax.experimental.pallas import tpu as pltpu

F32 = jnp.float32
CDT = jnp.bfloat16

NEG = -1e30
NEG_INF = -1e30
BIG = 1e9
NORM_EPS = 1e-6
LOG2E = 1.4426950408889634
LANES = 128

D_MODEL = 1024
DIFF_HEADS, DIFF_DH = 4, 64
MLA_HEADS, MLA_NOPE, MLA_ROPE, MLA_VDIM = 4, 128, 64, 128
MLA_Q_LORA, MLA_KV_LORA = 256, 256
ROPE_THETA = 10000.0
FOX_HEADS, FOX_DH = 4, 128
NSA_HEADS, NSA_GROUPS, NSA_DH = 8, 2, 64
NSA_HPG = NSA_HEADS // NSA_GROUPS
CMP_STRIDE = 16
CMP_LEN = 2 * CMP_STRIDE
CMP_HIDDEN = 128
SLC_LEN = 64
SLC_SHIFT = 6
HALF_SHIFT = 6
SLC_TOPK = 8
WINDOW = 256
N_BRANCH = 4
BRANCH_WIDTH = 512
D_FF = 2816
CONV_WIDTH = 3

PB_AQ, PB_AK, PB_AV = 0, 4, 8
PB_CQ, PB_CK, PB_CV = 12, 16, 20
PB_DQ = 24
PB_CMP_K, PB_CMP_V, PB_SEL_K, PB_SEL_V, PB_WIN_K, PB_WIN_V = 28, 29, 30, 31, 32, 33
PB_BCQ, PB_BCKV, PB_BKR, PB_BKRS = 34, 36, 38, 39
PB_GATE = 40
N_PROJ = 72 * LANES
SMALL_F, SMALL_G = 0, 4

VMEM_LIMIT = 56 * 1024 * 1024
MXU_TILE = 256
TQ_DENSE = 512
TQ_NSA = WINDOW
TM_PROJ, TN_PROJ = 1024, 9 * MXU_TILE
TM_ROWS = 512
FFN_CHUNK = MXU_TILE
FOX_HP = 2
MLA_HP = 2


def _cparams(sem):
    return pltpu.CompilerParams(dimension_semantics=sem, vmem_limit_bytes=VMEM_LIMIT)


def _rms(xf, g):
    return xf * lax.rsqrt(jnp.mean(xf * xf, axis=-1, keepdims=True) + NORM_EPS) * g


def _sigmoid(x):
    return 0.5 * jnp.tanh(0.5 * x) + 0.5


def _dot(a, b):
    return jnp.dot(a, b, preferred_element_type=F32)


def _dot_nt(a, b):
    return lax.dot_general(a, b, (((1,), (1,)), ((), ())), preferred_element_type=F32)


def _split_dot(a, b):
    hi = a.astype(CDT)
    lo = (a - hi.astype(F32)).astype(CDT)
    return _dot(hi, b) + _dot(lo, b)


def _alibi_slopes(n):
    return (LOG2E * np.exp2(-8.0 * np.arange(1, n + 1) / n)).astype(np.float32)


def _inproj_kernel(x_ref, g_ref, w_ref, ws_ref, o_ref, os_ref, h_ref):
    @pl.when(pl.program_id(1) == 0)
    def _():
        h = _rms(x_ref[...], g_ref[...]).astype(CDT)
        h_ref[...] = h
        os_ref[...] = _dot_nt(h, ws_ref[...])

    o_ref[...] = _dot_nt(h_ref[...], w_ref[...]).astype(o_ref.dtype)


def _in_proj(x2, g, w, ws, layer):
    t, d = x2.shape
    n = w.shape[1]
    tm = min(TM_PROJ, t)
    tn = TN_PROJ
    assert n % tn == 0
    return pl.pallas_call(
        _inproj_kernel,
        grid=(t // tm, n // tn),
        in_specs=[
            pl.BlockSpec((tm, d), lambda i, j: (i, 0)),
            pl.BlockSpec((1, d), lambda i, j: (0, 0)),
            pl.BlockSpec((None, tn, d), lambda i, j: (layer, j, 0)),
            pl.BlockSpec((None, LANES, d), lambda i, j: (layer, 0, 0)),
        ],
        out_specs=[
            pl.BlockSpec((tm, tn), lambda i, j: (i, j)),
            pl.BlockSpec((tm, LANES), lambda i, j: (i, 0)),
        ],
        out_shape=[jax.ShapeDtypeStruct((t, n), CDT), jax.ShapeDtypeStruct((t, LANES), F32)],
        scratch_shapes=[pltpu.VMEM((tm, d), CDT)],
        compiler_params=_cparams(("parallel", "arbitrary")),
        name="in_proj",
    )(x2, g.reshape(1, d), w, ws)


def _fox_cumsum_kernel(cf_ref, bf_ref, o_ref):
    rows, s = cf_ref.shape
    lane = lax.broadcasted_iota(jnp.int32, (rows, LANES), 1)
    carry = jnp.zeros((rows, 1), F32)
    for c in range(s // LANES):
        z = cf_ref[:, c * LANES:(c + 1) * LANES] + bf_ref[...]
        xs = jnp.minimum(z, 0.0) - jnp.log1p(jnp.exp(-jnp.abs(z)))
        d = 1
        while d < LANES:
            xs = xs + jnp.where(lane >= d, pltpu.roll(xs, d, axis=1), 0.0)
            d *= 2
        xs = xs + carry
        o_ref[:, c * LANES:(c + 1) * LANES] = xs
        carry = xs[:, LANES - 1:LANES]


def _fox_cumsum(cf_rows, bias_rows):
    return pl.pallas_call(
        _fox_cumsum_kernel,
        out_shape=jax.ShapeDtypeStruct(cf_rows.shape, F32),
        name="fox_cumsum",
    )(cf_rows, bias_rows)


def _flash_scratch(rows, tk, mask_scratch=False):
    return [pltpu.VMEM((rows, LANES), F32), pltpu.VMEM((rows, 2 * LANES), F32),
            pltpu.VMEM((rows, tk), F32), pltpu.VMEM((rows, tk), F32),
            pltpu.VMEM((rows, LANES), F32), pltpu.VMEM((rows, LANES), F32)
            ] + ([pltpu.VMEM((rows, tk), F32)] if mask_scratch else [])


def _flash_reset(m_ref, acc_ref):
    m_ref[...] = jnp.full(m_ref.shape, NEG, F32)
    acc_ref[...] = jnp.zeros(acc_ref.shape, F32)


def _flash_begin(m_ref, acc_ref, cm_ref, tq):
    _flash_reset(m_ref, acc_ref)
    cm_ref[...] = _causal_bias(cm_ref.shape[0], cm_ref.shape[1], tq)


def _row_max(s):
    return jnp.broadcast_to(jnp.max(s, axis=-1, keepdims=True), (s.shape[0], LANES))


def _causal_bias(rows, tk, tq):
    r = lax.broadcasted_iota(jnp.int32, (rows, tk), 0) & (tq - 1)
    c = lax.broadcasted_iota(jnp.int32, (rows, tk), 1)
    return jnp.where(c <= r, 0.0, NEG)


def _put_logits(buf, s, rows=slice(None), diag=False, cm_ref=None):
    if diag is True:
        s = s + cm_ref[rows]
    elif diag is not False:
        s = s + diag.astype(F32) * cm_ref[rows]
    buf[0][rows] = s
    buf[1][rows] = _row_max(s)


def _with_ones(v):
    return jnp.concatenate([v, jnp.ones((v.shape[0], LANES), v.dtype)], axis=1)


def _flash_consume(buf, v, m_ref, acc_ref, mask_tq=None, rows=slice(None)):
    s = buf[0][rows]
    m_cur = buf[1][rows]
    if mask_tq is not None:
        s = s + _causal_bias(s.shape[0], s.shape[1], mask_tq)
        m_cur = _row_max(s)
    m_old = m_ref[rows]
    m_new = jnp.maximum(m_old, m_cur)
    alpha = jnp.exp2(m_old - m_new)
    p = jnp.exp2(s - jnp.tile(m_new, (1, s.shape[1] // LANES))).astype(CDT)
    acc_ref[rows] = jnp.tile(alpha, (1, 2)) * acc_ref[rows] + _dot(p, _with_ones(v))
    m_ref[rows] = m_new


def _flash_result(acc):
    return acc[:, :LANES] / acc[:, LANES:]


def _causal_schedule(nq):
    ent = [(qi, ki, int(ki == qi)) for qi in range(nq) for ki in range(qi + 1)]
    n = len(ent)
    a = np.asarray(ent + [ent[-1]] * 2, np.int32)
    return n, tuple(jnp.asarray(a[:, i]) for i in range(3))


def _flash_stream(n, sched, base, produce, consume, finish, buf_a, buf_b, mask_at_produce):
    qt, kt, lt = sched

    def step(cur, nxt, t, diag, next_diag):
        if nxt is not None:
            produce(nxt, qt[base + t + 1], kt[base + t + 1], next_diag if mask_at_produce else False)
        consume(cur, kt[base + t], diag and not mask_at_produce)
        if diag:
            finish(qt[base + t])

    produce(buf_a, qt[base], kt[base], mask_at_produce)

    def pair(j, c):
        t = 2 * j
        l0, l1 = lt[base + t], lt[base + t + 1]
        for d0 in (False, True):
            for d1 in (False, True):
                @pl.when(((l0 != 0) == d0) & ((l1 != 0) == d1))
                def _():
                    step(buf_a, buf_b, t, d0, d1)
                    step(buf_b, buf_a, t + 1, d1, lt[base + t + 2])
        return c

    lax.fori_loop(0, n // 2, pair, 0)

    def tail():
        step(buf_a, None, n - 1, True, None)

    if isinstance(n, int):
        if n % 2 == 1:
            tail()
    else:
        pl.when(n % 2 == 1)(tail)


def _tile(ref, i, t):
    return ref[0, pl.ds(pl.multiple_of(i * t, t), t), :]


def _diff_attn_kernel(qt_ref, kt_ref, lt_ref, slopes_ref, lam_ref, g_ref, q_ref, k_ref, v_ref, o_ref,
                      m_ref, acc_ref, sa_ref, sb_ref, ma_ref, mb_ref, *, tq, n, lam_init):
    slope = slopes_ref[pl.program_id(1)]
    _flash_reset(m_ref, acc_ref)
    col = lax.broadcasted_iota(jnp.int32, (1, tq), 1).astype(F32)
    lane = lax.broadcasted_iota(jnp.int32, (tq, LANES), 1)
    lf = lam_ref[...]
    lam = (jnp.exp(jnp.sum(lf[0:1] * lf[1:2], axis=-1, keepdims=True))
           - jnp.exp(jnp.sum(lf[2:3] * lf[3:4], axis=-1, keepdims=True)) + lam_init)

    def produce(buf, qi, ki, diag):
        q = _tile(q_ref, qi, tq)
        zero = jnp.zeros_like(q)
        qq = jnp.concatenate([jnp.where(lane < DIFF_DH, q, zero), jnp.where(lane >= DIFF_DH, q, zero)], axis=0)
        s = _dot_nt(qq, _tile(k_ref, ki, tq))
        _put_logits(buf, s + slope * (col + ((ki - qi) * tq).astype(F32)))

    def consume(buf, ki, diag):
        _flash_consume(buf, _tile(v_ref, ki, tq), m_ref, acc_ref, tq if diag else None)

    def finish(qi):
        o = _flash_result(acc_ref[...])
        d = o[0:tq] - lam * o[tq:2 * tq]
        o_ref[0, pl.ds(pl.multiple_of(qi * tq, tq), tq), :] = (
            _rms(d, g_ref[...]) * (1.0 - lam_init)).astype(o_ref.dtype)
        _flash_reset(m_ref, acc_ref)

    _flash_stream(n, (qt_ref, kt_ref, lt_ref), 0, produce, consume, finish, (sa_ref, ma_ref), (sb_ref, mb_ref),
                  mask_at_produce=False)


_SMEM = pl.BlockSpec(memory_space=pltpu.SMEM)


def _diff_attention(proj3, diff_lambda, subln, lam_init):
    b, s, _ = proj3.shape
    tq = min(TQ_DENSE, s)
    dv = 2 * DIFF_DH
    n, sched = _causal_schedule(s // tq)
    kern = functools.partial(_diff_attn_kernel, tq=tq, n=n, lam_init=lam_init)
    return pl.pallas_call(
        kern,
        grid=(b, DIFF_HEADS),
        in_specs=[
            _SMEM, _SMEM, _SMEM, _SMEM,
            pl.BlockSpec((4, DIFF_DH), lambda bi, h: (0, 0)),
            pl.BlockSpec((1, dv), lambda bi, h: (0, 0)),
            pl.BlockSpec((1, s, LANES), lambda bi, h: (bi, 0, PB_AQ + h)),
            pl.BlockSpec((1, s, LANES), lambda bi, h: (bi, 0, PB_AK + h)),
            pl.BlockSpec((1, s, LANES), lambda bi, h: (bi, 0, PB_AV + h)),
        ],
        out_specs=pl.BlockSpec((1, s, dv), lambda bi, h: (bi, 0, h)),
        out_shape=jax.ShapeDtypeStruct((b, s, DIFF_HEADS * dv), CDT),
        scratch_shapes=_flash_scratch(2 * tq, tq),
        compiler_params=_cparams(("parallel", "parallel")),
        name="diff_attention",
    )(*sched, jnp.asarray(_alibi_slopes(DIFF_HEADS)), diff_lambda, subln.reshape(1, dv), proj3, proj3, proj3)


def _mla_prep_kernel(cq_ref, ckv_ref, kr_ref, krs_ref, gq_ref, gkv_ref, wqm_ref, wqs_ref, wk_ref, wv_ref,
                     cosq_ref, sinq_ref, cosk_ref, sink_ref, q_ref, k_ref, v_ref):
    hq = _rms(cq_ref[0].astype(F32), gq_ref[...]).astype(CDT)
    qm = _dot(hq, wqm_ref[...])
    qs = _dot(hq, wqs_ref[...])
    cosq, sinq = cosq_ref[...], sinq_ref[...]
    hw = 2 * LANES
    for h in range(MLA_HEADS):
        sl = slice(h * hw, (h + 1) * hw)
        q_ref[0, :, sl] = (qm[:, sl] * cosq + qs[:, sl] * sinq).astype(q_ref.dtype)
    hkv = _rms(ckv_ref[0].astype(F32), gkv_ref[...]).astype(CDT)
    kn = _dot(hkv, wk_ref[...])
    v_ref[0] = _dot(hkv, wv_ref[...]).astype(v_ref.dtype)
    kpe = (kr_ref[0].astype(F32) * cosk_ref[...] + krs_ref[0].astype(F32) * sink_ref[...]).astype(k_ref.dtype)
    for h in range(MLA_HEADS):
        k_ref[0, :, h * hw:h * hw + LANES] = kn[:, h * LANES:(h + 1) * LANES].astype(k_ref.dtype)
        k_ref[0, :, h * hw + LANES:(h + 1) * hw] = kpe


def _mla_prep(proj3, gq, gkv, wqm, wqs, wk, wv, tabs):
    b, s, _ = proj3.shape
    tm = min(TM_PROJ, s)
    hw = 2 * LANES
    cosq, sinq, cosk, sink = tabs
    const = lambda shape: pl.BlockSpec(shape, lambda bi, i: (0,) * len(shape))
    return pl.pallas_call(
        _mla_prep_kernel,
        grid=(b, s // tm),
        in_specs=[
            pl.BlockSpec((1, tm, MLA_Q_LORA), lambda bi, i: (bi, i, PB_BCQ // 2)),
            pl.BlockSpec((1, tm, MLA_KV_LORA), lambda bi, i: (bi, i, PB_BCKV // 2)),
            pl.BlockSpec((1, tm, LANES), lambda bi, i: (bi, i, PB_BKR)),
            pl.BlockSpec((1, tm, LANES), lambda bi, i: (bi, i, PB_BKRS)),
            const((1, MLA_Q_LORA)), const((1, MLA_KV_LORA)),
            const((MLA_Q_LORA, MLA_HEADS * hw)), const((MLA_Q_LORA, MLA_HEADS * hw)),
            const((MLA_KV_LORA, MLA_HEADS * MLA_NOPE)), const((MLA_KV_LORA, MLA_HEADS * MLA_VDIM)),
            pl.BlockSpec((tm, hw), lambda bi, i: (i, 0)), pl.BlockSpec((tm, hw), lambda bi, i: (i, 0)),
            pl.BlockSpec((tm, LANES), lambda bi, i: (i, 0)), pl.BlockSpec((tm, LANES), lambda bi, i: (i, 0)),
        ],
        out_specs=[
            pl.BlockSpec((1, tm, MLA_HEADS * hw), lambda bi, i: (bi, i, 0)),
            pl.BlockSpec((1, tm, MLA_HEADS * hw), lambda bi, i: (bi, i, 0)),
            pl.BlockSpec((1, tm, MLA_HEADS * MLA_VDIM), lambda bi, i: (bi, i, 0)),
        ],
        out_shape=[
            jax.ShapeDtypeStruct((b, s, MLA_HEADS * hw), CDT),
            jax.ShapeDtypeStruct((b, s, MLA_HEADS * hw), CDT),
            jax.ShapeDtypeStruct((b, s, MLA_HEADS * MLA_VDIM), CDT),
        ],
        compiler_params=_cparams(("parallel", "parallel")),
        name="mla_prep",
    )(proj3, proj3, proj3, proj3, gq.reshape(1, -1), gkv.reshape(1, -1), wqm, wqs, wk, wv,
      cosq, sinq, cosk, sink)


def _plain_attn_kernel(qt_ref, kt_ref, lt_ref, q_ref, k_ref, v_ref, o_ref,
                       m_ref, acc_ref, sa_ref, sb_ref, ma_ref, mb_ref, *, tq, n, hp, dk, dv):
    _flash_reset(m_ref, acc_ref)
    heads = [(slice(h * tq, (h + 1) * tq), slice(h * dk, (h + 1) * dk), slice(h * dv, (h + 1) * dv))
             for h in range(hp)]

    def produce(buf, qi, ki, diag):
        q, k = _tile(q_ref, qi, tq), _tile(k_ref, ki, tq)
        for rows, kcols, _ in heads:
            _put_logits(buf, _dot_nt(q[:, kcols], k[:, kcols]), rows)

    def consume(buf, ki, diag):
        v = _tile(v_ref, ki, tq)
        for rows, _, vcols in heads:
            _flash_consume(buf, v[:, vcols], m_ref, acc_ref, tq if diag else None, rows)

    def finish(qi):
        for rows, _, vcols in heads:
            o_ref[0, pl.ds(pl.multiple_of(qi * tq, tq), tq), vcols] = _flash_result(acc_ref[rows]).astype(o_ref.dtype)
        _flash_reset(m_ref, acc_ref)

    _flash_stream(n, (qt_ref, kt_ref, lt_ref), 0, produce, consume, finish, (sa_ref, ma_ref), (sb_ref, mb_ref),
                  mask_at_produce=False)


def _mla_attention(qc, kc, v):
    b, s, _ = qc.shape
    tq = min(TQ_DENSE, s)
    hw = 2 * LANES
    hp = MLA_HP
    n, sched = _causal_schedule(s // tq)
    return pl.pallas_call(
        functools.partial(_plain_attn_kernel, tq=tq, n=n, hp=hp, dk=hw, dv=MLA_VDIM),
        grid=(b, MLA_HEADS // hp),
        in_specs=[
            _SMEM, _SMEM, _SMEM,
            pl.BlockSpec((1, s, hp * hw), lambda bi, h: (bi, 0, h)),
            pl.BlockSpec((1, s, hp * hw), lambda bi, h: (bi, 0, h)),
            pl.BlockSpec((1, s, hp * MLA_VDIM), lambda bi, h: (bi, 0, h)),
        ],
        out_specs=pl.BlockSpec((1, s, hp * MLA_VDIM), lambda bi, h: (bi, 0, h)),
        out_shape=jax.ShapeDtypeStruct((b, s, MLA_HEADS * MLA_VDIM), CDT),
        scratch_shapes=_flash_scratch(hp * tq, tq),
        compiler_params=_cparams(("parallel", "parallel")),
        name="mla_attention",
    )(*sched, qc, kc, v)


def _fox_attn_kernel(qt_ref, kt_ref, lt_ref, c_ref, q_ref, k_ref, v_ref, o_ref,
                     m_ref, acc_ref, sa_ref, sb_ref, ma_ref, mb_ref, *, tq, n, hp):
    _flash_reset(m_ref, acc_ref)
    heads = [(slice(h * tq, (h + 1) * tq), slice(h * FOX_DH, (h + 1) * FOX_DH)) for h in range(hp)]

    def produce(buf, qi, ki, diag):
        q, k = _tile(q_ref, qi, tq), _tile(k_ref, ki, tq)
        for h, (rows, cols) in enumerate(heads):
            cbase = c_ref[0, h, pl.ds(qi, 1), :][:, 0:1]
            s = _dot_nt(q[:, cols], k[:, cols]) + LOG2E * (cbase - c_ref[0, h, pl.ds(ki, 1), :])
            _put_logits(buf, s, rows)

    def consume(buf, ki, diag):
        v = _tile(v_ref, ki, tq)
        for rows, cols in heads:
            _flash_consume(buf, v[:, cols], m_ref, acc_ref, tq if diag else None, rows)

    def finish(qi):
        for rows, cols in heads:
            o_ref[0, pl.ds(pl.multiple_of(qi * tq, tq), tq), cols] = _flash_result(acc_ref[rows]).astype(o_ref.dtype)
        _flash_reset(m_ref, acc_ref)

    _flash_stream(n, (qt_ref, kt_ref, lt_ref), 0, produce, consume, finish, (sa_ref, ma_ref), (sb_ref, mb_ref),
                  mask_at_produce=False)


def _fox_attention(proj3, c4):
    b, s, _ = proj3.shape
    tq = min(TQ_DENSE, s)
    nk = s // tq
    hp = FOX_HP
    w = hp * FOX_DH
    n, sched = _causal_schedule(nk)
    return pl.pallas_call(
        functools.partial(_fox_attn_kernel, tq=tq, n=n, hp=hp),
        grid=(b, FOX_HEADS // hp),
        in_specs=[
            _SMEM, _SMEM, _SMEM,
            pl.BlockSpec((1, hp, nk, tq), lambda bi, h: (bi, h, 0, 0)),
            pl.BlockSpec((1, s, w), lambda bi, h: (bi, 0, PB_CQ // hp + h)),
            pl.BlockSpec((1, s, w), lambda bi, h: (bi, 0, PB_CK // hp + h)),
            pl.BlockSpec((1, s, w), lambda bi, h: (bi, 0, PB_CV // hp + h)),
        ],
        out_specs=pl.BlockSpec((1, s, w), lambda bi, h: (bi, 0, h)),
        out_shape=jax.ShapeDtypeStruct((b, s, FOX_HEADS * FOX_DH), CDT),
        scratch_shapes=_flash_scratch(hp * tq, tq),
        compiler_params=_cparams(("parallel", "parallel")),
        name="fox_attention",
    )(*sched, c4.reshape(b, FOX_HEADS, nk, tq), proj3, proj3, proj3)


def _nsa_compress_kernel(x_ref, w1a_ref, w1b_ref, pea_ref, peb_ref, w2_ref, o_ref):
    x = x_ref[0]
    n = x.shape[0]
    pa = _dot(x, w1a_ref[...])
    pb = _dot(x, w1b_ref[...])
    pe = _dot(pea_ref[...], w1a_ref[...]) + _dot(peb_ref[...], w1b_ref[...])
    hid = pa + pltpu.roll(pb, n - 1, axis=0) + pe[0:1]
    act = 0.5 * hid * (1.0 + jnp.tanh(math.sqrt(2.0 / math.pi) * (hid + 0.044715 * hid * hid * hid)))
    o_ref[0] = _dot(act.astype(CDT), w2_ref[...]).astype(o_ref.dtype)


def _nsa_compress(xc, w1a, w1b, pea, peb, w2):
    b, n, kdim = xc.shape
    hdim = w1a.shape[1]
    const = lambda shape: pl.BlockSpec(shape, lambda bi: (0,) * len(shape))
    return pl.pallas_call(
        _nsa_compress_kernel,
        grid=(b,),
        in_specs=[pl.BlockSpec((1, n, kdim), lambda bi: (bi, 0, 0)),
                  const((kdim, hdim)), const((kdim, hdim)), const((8, kdim)), const((8, kdim)),
                  const((hdim, w2.shape[1]))],
        out_specs=pl.BlockSpec((1, n, w2.shape[1]), lambda bi: (bi, 0, 0)),
        out_shape=jax.ShapeDtypeStruct((b, n, w2.shape[1]), CDT),
        compiler_params=_cparams(("parallel",)),
        name="nsa_compress",
    )(xc, w1a, w1b, pea, peb, w2)


def _nsa_cmp_kernel(slopes_ref, q_ref, kv_ref, oc_ref, sb_ref, used_ref, *, tq, n_topk):
    qi = pl.program_id(1)
    nblk = kv_ref.shape[1]
    q0 = qi * tq
    rowpos = q0 + lax.broadcasted_iota(jnp.int32, (tq, 1), 0)
    cmp_end = lax.broadcasted_iota(jnp.int32, (1, nblk), 1) * CMP_STRIDE + (CMP_LEN - 1)
    negmask = jnp.where(rowpos >= cmp_end, 0.0, NEG)
    end_rel = (cmp_end - q0).astype(F32)
    lane = lax.broadcasted_iota(jnp.int32, (tq, LANES), 1)
    low = lane < NSA_DH
    nn = lax.broadcasted_iota(jnp.int32, (NSA_DH, nblk), 1) * CMP_STRIDE
    jj = lax.broadcasted_iota(jnp.int32, (NSA_DH, nblk), 0) * SLC_LEN
    ovt = (jnp.maximum(jnp.minimum(nn + CMP_LEN, jj + SLC_LEN) - jnp.maximum(nn, jj), 0).astype(F32)
           * (1.0 / CMP_LEN)).astype(CDT)
    jt = lax.broadcasted_iota(jnp.int32, (NSA_DH, tq), 0).astype(F32)
    blk = ((q0 + lax.broadcasted_iota(jnp.int32, (1, tq), 1)) >> SLC_SHIFT).astype(F32)
    fixed = (jt == 0.0) | (jt == blk) | (jt == blk - 1.0)
    out_of_play = fixed | (jt > blk)
    row_ok = rowpos >= CMP_LEN - 1
    outs = []
    bias = []
    for g in range(NSA_GROUPS):
        kc = kv_ref[0, :, g * LANES:(g + 1) * LANES]
        vc = kv_ref[0, :, (NSA_GROUPS + g) * LANES:(NSA_GROUPS + g + 1) * LANES]
        psum = jnp.zeros((tq, nblk), F32)
        mine = low if g == 0 else jnp.logical_not(low)
        zero = jnp.zeros((tq, LANES), q_ref.dtype)
        qs = jnp.concatenate([jnp.where(mine, q_ref[0, :, j * LANES:(j + 1) * LANES], zero)
                              for j in range(NSA_HPG)], axis=0)
        s_all = _dot_nt(qs, kc)
        ps = []
        for j in range(NSA_HPG):
            s = s_all[j * tq:(j + 1) * tq] + slopes_ref[g * NSA_HPG + j] * end_rel + negmask
            e = jnp.exp2(s - jnp.max(s, axis=-1, keepdims=True))
            den = jnp.sum(e, axis=-1, keepdims=True)
            p = e * jnp.where(row_ok, 1.0 / den, 0.0)
            psum = psum + p
            ps.append(p.astype(CDT))
        o_all = _dot(jnp.concatenate(ps, axis=0), vc)
        outs.extend(o_all[j * tq:(j + 1) * tq] for j in range(NSA_HPG))
        hi = psum.astype(CDT)
        lo = (psum - hi.astype(F32)).astype(CDT)
        imp = _dot_nt(ovt, hi) + _dot_nt(ovt, lo)
        imp = jnp.where(out_of_play, -jnp.inf, imp)
        sbt = jnp.where(fixed, 0.0, NEG)
        for _ in range(n_topk - 3):
            mx = jnp.max(imp, axis=0, keepdims=True)
            idx = jnp.min(jnp.where(imp == mx, jt, float(LANES)), axis=0, keepdims=True)
            hit = jt == idx
            sbt = jnp.where(hit, 0.0, sbt)
            imp = jnp.where(hit, -jnp.inf, imp)
        bias.append(sbt)
    sb = jnp.concatenate([bias[1], bias[0]], axis=0).T
    sb_ref[0] = sb.astype(sb_ref.dtype)
    used = jnp.max(jnp.where(sb == 0.0, 1.0, 0.0), axis=0, keepdims=True)
    used_ref[0, 0] = jnp.broadcast_to(used, used_ref.shape[2:])
    for blk_i in range(NSA_HEADS // 2):
        oc_ref[0, :, blk_i * LANES:(blk_i + 1) * LANES] = jnp.where(
            low, outs[2 * blk_i], outs[2 * blk_i + 1]).astype(oc_ref.dtype)


def _nsa_cmp_select(proj3, kvc, n_topk):
    assert n_topk >= 3, "the three always-selected blocks must fit in the top-k budget"
    b, s, _ = proj3.shape
    tq = min(TQ_NSA, s)
    nblk = kvc.shape[1]
    return pl.pallas_call(
        functools.partial(_nsa_cmp_kernel, tq=tq, n_topk=n_topk),
        grid=(b, s // tq),
        in_specs=[
            pl.BlockSpec(memory_space=pltpu.SMEM),
            pl.BlockSpec((1, tq, 4 * LANES), lambda bi, qi: (bi, qi, PB_DQ // 4)),
            pl.BlockSpec((1, nblk, kvc.shape[2]), lambda bi, qi: (bi, 0, 0)),
        ],
        out_specs=[
            pl.BlockSpec((1, tq, NSA_HEADS * NSA_DH), lambda bi, qi: (bi, qi, 0)),
            pl.BlockSpec((1, tq, LANES), lambda bi, qi: (bi, qi, 0)),
            pl.BlockSpec((1, 1, 8, LANES), lambda bi, qi: (bi, qi, 0, 0)),
        ],
        out_shape=[jax.ShapeDtypeStruct((b, s, NSA_HEADS * NSA_DH), CDT),
                   jax.ShapeDtypeStruct((b, s, LANES), CDT),
                   jax.ShapeDtypeStruct((b, s // tq, 8, LANES), F32)],
        compiler_params=_cparams(("parallel", "parallel")),
        name="nsa_cmp_select",
    )(jnp.asarray(_alibi_slopes(NSA_HEADS)), proj3, kvc)


def _compact_heads(heads, mine, low):
    both = [jnp.where(mine, a, pltpu.roll(a, NSA_DH, axis=1)) for a in heads]
    out = [jnp.where(low, both[2 * jj], both[2 * jj + 1]) for jj in range(NSA_HPG // 2)]
    return jnp.concatenate(out, axis=1)


def _nsa_win_kernel(slopes_ref, q_ref, kp_ref, kc_ref, vp_ref, vc_ref, o_ref, *, tq):
    qi = pl.program_id(1)
    lane = lax.broadcasted_iota(jnp.int32, (tq, LANES), 1)
    low = lane < NSA_DH
    r = lax.broadcasted_iota(jnp.int32, (tq, tq), 0)
    c = lax.broadcasted_iota(jnp.int32, (tq, tq), 1)
    own = c <= r
    ndist = jnp.where(own, c - r, c - r - tq).astype(F32)
    own_f = jnp.where(own, 1.0, 0.0).astype(CDT)
    prev_pen = jnp.where(qi > 0, 0.0, NEG)
    q = q_ref[0]
    zero = jnp.zeros((tq, LANES), q.dtype)
    mine = (low, jnp.logical_not(low))
    qs = jnp.concatenate([jnp.where(mine[g], q[:, j * LANES:(j + 1) * LANES], zero)
                          for g in range(NSA_GROUPS) for j in range(NSA_HPG)], axis=0)
    s_own, s_prev = _dot_nt(qs, kc_ref[0]), _dot_nt(qs, kp_ref[0])
    ps = []
    for hd in range(NSA_HEADS):
        rows = slice(hd * tq, (hd + 1) * tq)
        s = jnp.where(own, s_own[rows], s_prev[rows] + prev_pen) + slopes_ref[hd] * ndist
        ps.append(jnp.exp2(s - jnp.max(s, axis=-1, keepdims=True)).astype(CDT))
    p = jnp.concatenate(ps, axis=0)
    p_own = p * jnp.tile(own_f, (NSA_HEADS, 1))
    o = _flash_result(_dot(p_own, _with_ones(vc_ref[0])) + _dot(p - p_own, _with_ones(vp_ref[0])))
    for g in range(NSA_GROUPS):
        heads = [o[(g * NSA_HPG + j) * tq:(g * NSA_HPG + j + 1) * tq] for j in range(NSA_HPG)]
        w = NSA_HPG * NSA_DH
        o_ref[0, :, g * w:(g + 1) * w] = _compact_heads(heads, mine[g], low).astype(o_ref.dtype)


def _nsa_window(proj3):
    b, s, _ = proj3.shape
    tq = WINDOW
    return pl.pallas_call(
        functools.partial(_nsa_win_kernel, tq=tq),
        grid=(b, s // tq),
        in_specs=[
            pl.BlockSpec(memory_space=pltpu.SMEM),
            pl.BlockSpec((1, tq, 4 * LANES), lambda bi, qi: (bi, qi, PB_DQ // 4)),
            pl.BlockSpec((1, tq, LANES), lambda bi, qi: (bi, jnp.maximum(qi - 1, 0), PB_WIN_K)),
            pl.BlockSpec((1, tq, LANES), lambda bi, qi: (bi, qi, PB_WIN_K)),
            pl.BlockSpec((1, tq, LANES), lambda bi, qi: (bi, jnp.maximum(qi - 1, 0), PB_WIN_V)),
            pl.BlockSpec((1, tq, LANES), lambda bi, qi: (bi, qi, PB_WIN_V)),
        ],
        out_specs=pl.BlockSpec((1, tq, NSA_HEADS * NSA_DH), lambda bi, qi: (bi, qi, 0)),
        out_shape=jax.ShapeDtypeStruct((b, s, NSA_HEADS * NSA_DH), CDT),
        compiler_params=_cparams(("parallel", "parallel")),
        name="nsa_window",
    )(jnp.asarray(_alibi_slopes(NSA_HEADS)), proj3, proj3, proj3, proj3, proj3)


def _nsa_sel_kernel(cnt_ref, qt_ref, kt_ref, lt_ref, slopes_ref, q_ref, sb_ref, k_ref, v_ref, oc_ref, ow_ref, gl_ref,
                    e_ref, o_ref, m_ref, acc_ref, sa_ref, sb2_ref, ma_ref, mb_ref, cm_ref, *, tq, rows_per_problem):
    g = pl.program_id(1)
    w = NSA_HPG * NSA_DH
    lane = lax.broadcasted_iota(jnp.int32, (tq, LANES), 1)
    low = lane < NSA_DH
    mine = (lane >> HALF_SHIFT) == g
    _flash_begin(m_ref, acc_ref, cm_ref, tq)
    col = lax.broadcasted_iota(jnp.int32, (1, tq), 1).astype(F32)
    jl = lane & (NSA_DH - 1)
    krow = lax.broadcasted_iota(jnp.int32, (tq, LANES), 0)

    def produce(buf, qi, ki, diag):
        q = _tile(q_ref, qi, tq)
        sb = _tile(sb_ref, qi, tq)
        qa = jnp.concatenate([jnp.where(mine, q[:, j * LANES:(j + 1) * LANES], sb) for j in range(NSA_HPG)], axis=0)
        k = _tile(k_ref, ki, tq)
        onehot = jnp.where(((ki * tq + krow) >> SLC_SHIFT) == jl, 1.0, 0.0).astype(k.dtype)
        s_all = _dot_nt(qa, jnp.where(mine, k, onehot))
        rel = ((ki - qi) * tq).astype(F32)
        for j in range(NSA_HPG):
            rows = slice(j * tq, (j + 1) * tq)
            _put_logits(buf, s_all[rows] + slopes_ref[g * NSA_HPG + j] * (col + rel), rows, diag, cm_ref)

    def consume(buf, ki, diag):
        _flash_consume(buf, _tile(v_ref, ki, tq), m_ref, acc_ref, tq if diag else None)

    def finish(qi):
        o = _flash_result(acc_ref[...])
        o_s = _compact_heads([o[j * tq:(j + 1) * tq] for j in range(NSA_HPG)], mine, low)
        gates = _split_dot(_sigmoid(_tile(gl_ref, qi, tq)), e_ref[0])
        y = (gates[:, 0:w] * _tile(oc_ref, qi, tq).astype(F32) + gates[:, w:2 * w] * o_s
             + gates[:, 2 * w:3 * w] * _tile(ow_ref, qi, tq).astype(F32))
        o_ref[0, pl.ds(pl.multiple_of(qi * tq, tq), tq), :] = y.astype(o_ref.dtype)
        _flash_reset(m_ref, acc_ref)

    prob = pl.program_id(0) * NSA_GROUPS + g
    _flash_stream(cnt_ref[prob], (qt_ref, kt_ref, lt_ref), prob * rows_per_problem, produce, consume, finish,
                  (sa_ref, ma_ref), (sb2_ref, mb_ref), mask_at_produce=True)


def _nsa_selected(proj3, sbias, used, o_c, o_w, small3, expand):
    b, s, _ = proj3.shape
    tq = min(TQ_NSA, s)
    nq = s // tq
    w = NSA_HPG * NSA_DH
    u = used[:, :, 0, :].reshape(b, nq, NSA_GROUPS, NSA_DH)[:, :, ::-1, :nq * (tq // SLC_LEN)]
    flags = (u.reshape(b, nq, NSA_GROUPS, nq, tq // SLC_LEN).max(axis=-1) > 0.0).astype(jnp.int32)
    flags = flags.transpose(0, 2, 1, 3)
    qt = jnp.arange(nq, dtype=jnp.int32)
    need = jnp.where(qt[None, :] < qt[:, None], flags, (qt[None, :] == qt[:, None]).astype(jnp.int32))
    need = need.reshape(b, NSA_GROUPS, nq * nq)
    cnt = need.sum(axis=-1).astype(jnp.int32)
    order = jnp.argsort(1 - need, axis=-1, stable=True).astype(jnp.int32)
    order = jnp.pad(order, ((0, 0), (0, 0), (0, 2)))
    rows = nq * nq + 2
    sched = (order // nq, order % nq, (order // nq == order % nq).astype(jnp.int32))
    return pl.pallas_call(
        functools.partial(_nsa_sel_kernel, tq=tq, rows_per_problem=rows),
        grid=(b, NSA_GROUPS),
        in_specs=[
            _SMEM, _SMEM, _SMEM, _SMEM, _SMEM,
            pl.BlockSpec((1, s, 4 * LANES), lambda bi, g: (bi, 0, PB_DQ // 4)),
            pl.BlockSpec((1, s, LANES), lambda bi, g: (bi, 0, 0)),
            pl.BlockSpec((1, s, LANES), lambda bi, g: (bi, 0, PB_SEL_K)),
            pl.BlockSpec((1, s, LANES), lambda bi, g: (bi, 0, PB_SEL_V)),
            pl.BlockSpec((1, s, w), lambda bi, g: (bi, 0, g)),
            pl.BlockSpec((1, s, w), lambda bi, g: (bi, 0, g)),
            pl.BlockSpec((1, s, LANES), lambda bi, g: (bi, 0, 0)),
            pl.BlockSpec((1, LANES, 3 * w), lambda bi, g: (g, 0, 0)),
        ],
        out_specs=pl.BlockSpec((1, s, w), lambda bi, g: (bi, 0, g)),
        out_shape=jax.ShapeDtypeStruct((b, s, NSA_HEADS * NSA_DH), CDT),
        scratch_shapes=_flash_scratch(NSA_HPG * tq, tq, mask_scratch=True),
        compiler_params=_cparams(("parallel", "parallel")),
        name="nsa_selected",
    )(cnt.reshape(-1), *[t.reshape(-1) for t in sched], jnp.asarray(_alibi_slopes(NSA_HEADS)),
      proj3, sbias, proj3, proj3, o_c, o_w, small3, expand)


def _merge_kernel(ya_ref, yb_ref, yc_ref, yd_ref, ga_ref, gb_ref, gc_ref, gd_ref, wb_ref, wo_ref, x_ref, o_ref):
    merged = None
    for n, (y_ref, g_ref) in enumerate(((ya_ref, ga_ref), (yb_ref, gb_ref), (yc_ref, gc_ref), (yd_ref, gd_ref))):
        t = _sigmoid(g_ref[...].astype(F32)) * _dot(y_ref[...], wb_ref[n])
        merged = t if merged is None else merged + t
    o_ref[...] = x_ref[...] + _dot(merged.astype(CDT), wo_ref[...])


def _merge(ys, proj2, wb, wo, x2, layer):
    t, d = x2.shape
    tm = min(TM_ROWS, t)
    gate_blk = PB_GATE * LANES // d
    yspec = pl.BlockSpec((tm, BRANCH_WIDTH), lambda i: (i, 0))
    gspecs = [pl.BlockSpec((tm, d), functools.partial(lambda i, n: (i, gate_blk + n), n=n)) for n in range(N_BRANCH)]
    return pl.pallas_call(
        _merge_kernel,
        grid=(t // tm,),
        in_specs=[yspec] * N_BRANCH + gspecs + [
            pl.BlockSpec((None, N_BRANCH, BRANCH_WIDTH, d), lambda i: (layer, 0, 0, 0)),
            pl.BlockSpec((None, d, d), lambda i: (layer, 0, 0)),
            pl.BlockSpec((tm, d), lambda i: (i, 0)),
        ],
        out_specs=pl.BlockSpec((tm, d), lambda i: (i, 0)),
        out_shape=jax.ShapeDtypeStruct((t, d), F32),
        compiler_params=_cparams(("parallel",)),
        name="merge",
    )(*ys, proj2, proj2, proj2, proj2, wb, wo, x2)


HALO = 16


def _ffn_kernel(x_ref, xh_ref, g_ref, wu_ref, cw_ref, cb_ref, wd_ref, gf_ref, o_ref, he_ref, u_ref, act_ref,
                *, tm, fc, final):
    i = pl.program_id(1)
    x = x_ref[0]
    g = g_ref[...]
    xh = xh_ref[0] * (i > 0).astype(F32)
    he_ref[0:HALO] = _rms(xh, g).astype(CDT)
    he_ref[HALO:HALO + tm] = _rms(x, g).astype(CDT)
    he = he_ref[...]
    for c in range(D_FF // fc):
        outs = []
        for half in range(2):
            ub = u_ref.at[c % 2, half]
            lo = half * D_FF + c * fc
            ub[...] = _dot(he, wu_ref[:, lo:lo + fc])
            conv = cb_ref[:, lo:lo + fc]
            for kk in range(CONV_WIDTH):
                off = HALO - (CONV_WIDTH - 1) + kk
                conv = conv + cw_ref[kk:kk + 1, lo:lo + fc] * ub[off:off + tm, :]
            outs.append(conv)
        a, gg = outs
        act_ref[:, c * fc:(c + 1) * fc] = (a * _sigmoid(a) * gg).astype(CDT)
    y = x + _dot(act_ref[...], wd_ref[...])
    if final:
        y = _rms(y, gf_ref[...])
    o_ref[0] = y


def _ffn(x3, g, wu, cw, cb, wd, gf, layer, final):
    b, s, d = x3.shape
    tm = min(TM_ROWS, s)
    fc = FFN_CHUNK
    assert D_FF % fc == 0
    const = lambda shape: pl.BlockSpec(shape, lambda bi, i: (0,) * len(shape), pipeline_mode=pl.Buffered(1))
    stacked = lambda shape: pl.BlockSpec((None,) + shape, lambda bi, i: (layer,) + (0,) * len(shape),
                                         pipeline_mode=pl.Buffered(1))
    return pl.pallas_call(
        functools.partial(_ffn_kernel, tm=tm, fc=fc, final=final),
        grid=(b, s // tm),
        in_specs=[
            pl.BlockSpec((1, tm, d), lambda bi, i: (bi, i, 0)),
            pl.BlockSpec((1, HALO, d), lambda bi, i: (bi, jnp.maximum(i * (tm // HALO) - 1, 0), 0)),
            const((1, d)), stacked((d, 2 * D_FF)), const((CONV_WIDTH, 2 * D_FF)), const((1, 2 * D_FF)),
            stacked((D_FF, d)), const((1, d)),
        ],
        out_specs=pl.BlockSpec((1, tm, d), lambda bi, i: (bi, i, 0)),
        out_shape=jax.ShapeDtypeStruct((b, s, d), F32),
        scratch_shapes=[pltpu.VMEM((tm + HALO, d), CDT), pltpu.VMEM((2, 2, tm + HALO, fc), F32),
                        pltpu.VMEM((tm, D_FF), CDT)],
        compiler_params=_cparams(("parallel", "arbitrary")),
        name="conv_glu_mlp",
    )(x3, x3, g.reshape(1, d), wu, cw, cb.reshape(1, -1), wd, gf.reshape(1, d))


def _w_in_plan():
    widths = (512, 512, 512, MLA_Q_LORA, MLA_KV_LORA, MLA_ROPE, 512, 512, 512, FOX_HEADS,
              512, 768, 3 * NSA_HEADS, N_BRANCH * D_MODEL)
    (a_q, a_k, a_v, b_cq, b_ckv, b_kr, c_q, c_k, c_v, c_f, d_q, d_kv, d_g, gate, _) = np.cumsum((0,) + widths).tolist()
    half = MLA_ROPE // 2

    def run(src, nblocks, scale=1.0):
        return [[(src + i * LANES, LANES, scale)] for i in range(nblocks)]

    blocks = (run(a_q, 4, LOG2E * DIFF_DH ** -0.5) + run(a_k, 4) + run(a_v, 4)
              + run(c_q, 4, LOG2E * FOX_DH ** -0.5) + run(c_k, 4) + run(c_v, 4))
    sd = LOG2E * NSA_DH ** -0.5
    blocks += [[(d_q + j * NSA_DH, NSA_DH, sd), (d_q + (NSA_HPG + j) * NSA_DH, NSA_DH, sd)] for j in range(NSA_HPG)]
    blocks += run(d_kv, 6) + run(b_cq, 2) + run(b_ckv, 2)
    blocks += [[(b_kr, MLA_ROPE, 1.0), None],
               [(b_kr + half, half, -1.0), (b_kr, half, 1.0), None]]
    blocks += run(gate, N_BRANCH * D_MODEL // LANES)
    assert len(blocks) * LANES == N_PROJ
    small = [(c_f, FOX_HEADS, 1.0), (d_g, 3 * NSA_HEADS, 1.0), None]
    return blocks, small


def _w_in_relayout_kernel(w_ref, big_ref, small_ref):
    cols = w_ref.shape[1]
    blocks, small = _w_in_plan()
    for j, pieces in enumerate(blocks):
        row = j * LANES
        for p in pieces:
            if p is None:
                big_ref[row:(j + 1) * LANES, :] = jnp.zeros(((j + 1) * LANES - row, cols), big_ref.dtype)
            else:
                src, n, scale = p
                v = w_ref[src:src + n, :]
                big_ref[row:row + n, :] = (v if scale == 1.0 else v * scale).astype(big_ref.dtype)
                row += n
    (sf, nf, _), (sg, ng, _), _ = small
    r = lax.broadcasted_iota(jnp.int32, (LANES, cols), 0)
    side = jnp.where(r < nf, w_ref[sf:sf + LANES, :],
                     jnp.where(r < nf + ng, w_ref[sg - nf:sg - nf + LANES, :], 0.0))
    small_ref[...] = side.astype(small_ref.dtype)


def _w_in_relayout(w):
    nl, d, n = w.shape
    wt = jnp.transpose(w, (2, 0, 1)).reshape(n, nl * d)
    tc = min(256, d)
    return pl.pallas_call(
        _w_in_relayout_kernel,
        grid=(nl, d // tc),
        in_specs=[pl.BlockSpec((n, tc), lambda l, i: (0, l * (d // tc) + i))],
        out_specs=[pl.BlockSpec((None, N_PROJ, tc), lambda l, i: (l, 0, i)),
                   pl.BlockSpec((None, LANES, tc), lambda l, i: (l, 0, i))],
        out_shape=[jax.ShapeDtypeStruct((nl, N_PROJ, d), CDT), jax.ShapeDtypeStruct((nl, LANES, d), CDT)],
        compiler_params=_cparams(("parallel", "parallel")),
        name="w_in_relayout",
    )(wt)


def _prep_mla(w_uq, w_ukv):
    r = w_uq.shape[0]
    hw = 2 * LANES
    half = MLA_ROPE // 2
    scale = LOG2E * (MLA_NOPE + MLA_ROPE) ** -0.5
    wq = (w_uq * scale).reshape(r, MLA_HEADS, MLA_NOPE + MLA_ROPE)
    nope, t1, t2 = wq[..., :MLA_NOPE], wq[..., MLA_NOPE:MLA_NOPE + half], wq[..., MLA_NOPE + half:]
    zpad = jnp.zeros((r, MLA_HEADS, hw - MLA_NOPE - MLA_ROPE), w_uq.dtype)
    wqm = jnp.concatenate([nope, t1, t2, zpad], axis=-1).reshape(r, MLA_HEADS * hw)
    wqs = jnp.concatenate([jnp.zeros_like(nope), -t2, t1, zpad], axis=-1).reshape(r, MLA_HEADS * hw)
    wkv = w_ukv.reshape(w_ukv.shape[0], MLA_HEADS, MLA_NOPE + MLA_VDIM)
    wk = wkv[..., :MLA_NOPE].reshape(-1, MLA_HEADS * MLA_NOPE)
    wv = wkv[..., MLA_NOPE:].reshape(-1, MLA_HEADS * MLA_VDIM)
    return wqm.astype(CDT), wqs.astype(CDT), wk.astype(CDT), wv.astype(CDT)


def _rope_tables(s):
    half = MLA_ROPE // 2
    inv_freq = ROPE_THETA ** (-jnp.arange(0, MLA_ROPE, 2, dtype=F32) / MLA_ROPE)
    ang = jnp.arange(s, dtype=F32)[:, None] * inv_freq[None, :]
    cos, sin = jnp.cos(ang), jnp.sin(ang)
    z = jnp.zeros((s, LANES - MLA_ROPE), F32)
    cosk = jnp.concatenate([cos, cos, z], axis=1)
    sink = jnp.concatenate([sin, sin, z], axis=1)
    cosq = jnp.concatenate([jnp.ones((s, MLA_NOPE), F32), cosk], axis=1)
    sinq = jnp.concatenate([jnp.zeros((s, MLA_NOPE), F32), sink], axis=1)
    return cosq, sinq, cosk, sink


def _prep_compress(pe, w1, w2):
    eye2 = jnp.eye(2, dtype=F32)
    w1r = w1.reshape(2, CMP_LEN, NSA_DH, CMP_HIDDEN).astype(CDT)
    same = np.eye(2, dtype=bool)
    diag_kg = jnp.asarray(same[:, None, :, None] & same[None, :, None, :])

    def expand(wpart):
        src = wpart.transpose(1, 0, 2, 3)[:, :, None, :, None, None, :]
        t = jnp.where(diag_kg[None, :, :, None, :, :, None], src, jnp.zeros((), CDT))
        return t.reshape(CMP_STRIDE * 4 * NSA_DH, 4 * CMP_HIDDEN)

    w1a, w1b = expand(w1r[:, :CMP_STRIDE]), expand(w1r[:, CMP_STRIDE:])

    def pe_row(p):
        t = jnp.broadcast_to(p.transpose(1, 0, 2)[:, :, None, :], (CMP_STRIDE, 2, NSA_GROUPS, NSA_DH))
        return jnp.pad(t.reshape(1, -1), ((0, 7), (0, 0)))

    pea, peb = pe_row(pe[:, :CMP_STRIDE]), pe_row(pe[:, CMP_STRIDE:])
    w2b = jnp.einsum('khd,kK,gG,u->kghKGud', w2, eye2, eye2, jnp.ones((2,), F32))
    w2b = w2b.reshape(4 * CMP_HIDDEN, 4 * 2 * NSA_DH)
    return w1a.astype(CDT), w1b.astype(CDT), pea.astype(CDT), peb.astype(CDT), w2b.astype(CDT)


def _gate_expand():
    e = np.zeros((NSA_GROUPS, LANES, 3, NSA_HPG, NSA_DH), np.float32)
    for g in range(NSA_GROUPS):
        for j in range(NSA_HPG):
            for br in range(3):
                e[g, SMALL_G + (g * NSA_HPG + j) * 3 + br, br, j, :] = 1.0
    return jnp.asarray(e.reshape(NSA_GROUPS, LANES, 3 * NSA_HPG * NSA_DH)).astype(CDT)


def _token_mixers(x3, l, norm_mix, w_in, diff_lambda, diff_subln, mla_norm_q, mla_w_uq, mla_norm_kv, mla_w_ukv,
                  fox_b_f, nsa_cmp_pe, nsa_cmp_w1, nsa_cmp_w2, w_branch, w_out, rope_tabs):
    b, s, d = x3.shape
    t = b * s
    x2 = x3.reshape(t, d)
    proj, small = _in_proj(x2, norm_mix, *w_in, l)
    proj3 = proj.reshape(b, s, N_PROJ)
    small3 = small.reshape(b, s, LANES)

    lam_init = 0.8 - 0.6 * math.exp(-0.3 * l)
    y_a = _diff_attention(proj3, diff_lambda, diff_subln, lam_init)

    wqm, wqs, wk, wv = _prep_mla(mla_w_uq, mla_w_ukv)
    qc, kc, vv = _mla_prep(proj3, mla_norm_q, mla_norm_kv, wqm, wqs, wk, wv, rope_tabs)
    y_b = _mla_attention(qc, kc, vv)

    cf_rows = small3[:, :, SMALL_F:SMALL_F + FOX_HEADS].transpose(0, 2, 1).reshape(b * FOX_HEADS, s)
    bias_rows = jnp.tile(fox_b_f.astype(F32), b).reshape(b * FOX_HEADS, 1)
    c4 = _fox_cumsum(cf_rows, bias_rows)
    y_c = _fox_attention(proj3, c4)

    w1a, w1b, pea, peb, w2b = _prep_compress(nsa_cmp_pe, nsa_cmp_w1, nsa_cmp_w2)
    xc = proj3[:, :, PB_CMP_K * LANES:(PB_CMP_V + 1) * LANES].reshape(b, s // CMP_STRIDE, CMP_STRIDE * 2 * LANES)
    kvc = _nsa_compress(xc, w1a, w1b, pea, peb, w2b)
    n_topk = min(SLC_TOPK, s // SLC_LEN)
    o_c, sbias, used = _nsa_cmp_select(proj3, kvc, n_topk)
    o_w = _nsa_window(proj3)
    y_d = _nsa_selected(proj3, sbias, used, o_c, o_w, small3, _gate_expand())

    ys = [y.reshape(t, BRANCH_WIDTH) for y in (y_a, y_b, y_c, y_d)]
    return _merge(ys, proj, w_branch, w_out, x2, l).reshape(b, s, d)


def kernel(x, norm_mix, w_in, diff_lambda, diff_subln, mla_norm_q, mla_w_uq, mla_norm_kv, mla_w_ukv, fox_b_f,
           nsa_cmp_pe, nsa_cmp_w1, nsa_cmp_w2, w_branch, w_out, norm_ffn, w_up, conv_w, conv_b, w_down, norm_final):
    depth = w_in.shape[0]
    s = x.shape[1]
    rope_tabs = _rope_tables(s)
    w_in = _w_in_relayout(w_in)
    w_up16, w_down16 = w_up.astype(CDT), w_down.astype(CDT)
    w_branch16, w_out16 = w_branch.astype(CDT), w_out.astype(CDT)
    for l in range(depth):
        x = _token_mixers(x, l, norm_mix[l], w_in, diff_lambda[l], diff_subln[l], mla_norm_q[l], mla_w_uq[l],
                          mla_norm_kv[l], mla_w_ukv[l], fox_b_f[l], nsa_cmp_pe[l], nsa_cmp_w1[l], nsa_cmp_w2[l],
                          w_branch16, w_out16, rope_tabs)
        x = _ffn(x, norm_ffn[l], w_up16, conv_w[l], conv_b[l], w_down16, norm_final, l, final=(l == depth - 1))
    return x
```

```python
import functools
import math

import numpy as np
import jax
import jax.numpy as jnp
from jax import lax
from jax.experimental import pallas as pl
from jax.experimental.pallas import tpu as pltpu

F32 = jnp.float32
CDT = jnp.bfloat16

NEG = -1e30
NEG_INF = -1e30
BIG = 1e9
NORM_EPS = 1e-6
LOG2E = 1.4426950408889634
LANES = 128

D_MODEL = 1024
DIFF_HEADS, DIFF_DH = 4, 64
MLA_HEADS, MLA_NOPE, MLA_ROPE, MLA_VDIM = 4, 128, 64, 128
MLA_Q_LORA, MLA_KV_LORA = 256, 256
ROPE_THETA = 10000.0
FOX_HEADS, FOX_DH = 4, 128
NSA_HEADS, NSA_GROUPS, NSA_DH = 8, 2, 64
NSA_HPG = NSA_HEADS // NSA_GROUPS
CMP_STRIDE = 16
CMP_LEN = 2 * CMP_STRIDE
CMP_HIDDEN = 128
SLC_LEN = 64
SLC_SHIFT = 6
HALF_SHIFT = 6
SLC_TOPK = 8
WINDOW = 256
N_BRANCH = 4
BRANCH_WIDTH = 512
D_FF = 2816
CONV_WIDTH = 3

PB_AQ, PB_AK, PB_AV = 0, 4, 8
PB_CQ, PB_CK, PB_CV = 12, 16, 20
PB_DQ = 24
PB_CMP_K, PB_CMP_V, PB_SEL_K, PB_SEL_V, PB_WIN_K, PB_WIN_V = 28, 29, 30, 31, 32, 33
PB_BCQ, PB_BCKV, PB_BKR, PB_BKRS = 34, 36, 38, 39
PB_GATE = 40
N_PROJ = 72 * LANES
SMALL_F, SMALL_G = 0, 4

VMEM_LIMIT = 56 * 1024 * 1024
MXU_TILE = 256
TQ_DENSE = 512
TQ_NSA = WINDOW
TM_PROJ, TN_PROJ = 1024, 9 * MXU_TILE
TM_ROWS = 512
FFN_CHUNK = MXU_TILE
FOX_HP = 2
MLA_HP = 2


def _cparams(sem):
    return pltpu.CompilerParams(dimension_semantics=sem, vmem_limit_bytes=VMEM_LIMIT)


def _rms(xf, g):
    return xf * lax.rsqrt(jnp.mean(xf * xf, axis=-1, keepdims=True) + NORM_EPS) * g


def _sigmoid(x):
    return 0.5 * jnp.tanh(0.5 * x) + 0.5


def _dot(a, b):
    return jnp.dot(a, b, preferred_element_type=F32)


def _dot_nt(a, b):
    return lax.dot_general(a, b, (((1,), (1,)), ((), ())), preferred_element_type=F32)


def _split_dot(a, b):
    hi = a.astype(CDT)
    lo = (a - hi.astype(F32)).astype(CDT)
    return _dot(hi, b) + _dot(lo, b)


def _alibi_slopes(n):
    return (LOG2E * np.exp2(-8.0 * np.arange(1, n + 1) / n)).astype(np.float32)


def _inproj_kernel(x_ref, g_ref, w_ref, ws_ref, o_ref, os_ref, h_ref):
    @pl.when(pl.program_id(1) == 0)
    def _():
        h = _rms(x_ref[...], g_ref[...]).astype(CDT)
        h_ref[...] = h
        os_ref[...] = _dot_nt(h, ws_ref[...])

    o_ref[...] = _dot_nt(h_ref[...], w_ref[...]).astype(o_ref.dtype)


def _in_proj(x2, g, w, ws, layer):
    t, d = x2.shape
    n = w.shape[1]
    tm = min(TM_PROJ, t)
    tn = TN_PROJ
    assert n % tn == 0
    return pl.pallas_call(
        _inproj_kernel,
        grid=(t // tm, n // tn),
        in_specs=[
            pl.BlockSpec((tm, d), lambda i, j: (i, 0)),
            pl.BlockSpec((1, d), lambda i, j: (0, 0)),
            pl.BlockSpec((None, tn, d), lambda i, j: (layer, j, 0)),
            pl.BlockSpec((None, LANES, d), lambda i, j: (layer, 0, 0)),
        ],
        out_specs=[
            pl.BlockSpec((tm, tn), lambda i, j: (i, j)),
            pl.BlockSpec((tm, LANES), lambda i, j: (i, 0)),
        ],
        out_shape=[jax.ShapeDtypeStruct((t, n), CDT), jax.ShapeDtypeStruct((t, LANES), F32)],
        scratch_shapes=[pltpu.VMEM((tm, d), CDT)],
        compiler_params=_cparams(("parallel", "arbitrary")),
        name="in_proj",
    )(x2, g.reshape(1, d), w, ws)


def _fox_cumsum_kernel(cf_ref, bf_ref, o_ref):
    rows, s = cf_ref.shape
    lane = lax.broadcasted_iota(jnp.int32, (rows, LANES), 1)
    carry = jnp.zeros((rows, 1), F32)
    for c in range(s // LANES):
        z = cf_ref[:, c * LANES:(c + 1) * LANES] + bf_ref[...]
        xs = jnp.minimum(z, 0.0) - jnp.log1p(jnp.exp(-jnp.abs(z)))
        d = 1
        while d < LANES:
            xs = xs + jnp.where(lane >= d, pltpu.roll(xs, d, axis=1), 0.0)
            d *= 2
        xs = xs + carry
        o_ref[:, c * LANES:(c + 1) * LANES] = xs
        carry = xs[:, LANES - 1:LANES]


def _fox_cumsum(cf_rows, bias_rows):
    return pl.pallas_call(
        _fox_cumsum_kernel,
        out_shape=jax.ShapeDtypeStruct(cf_rows.shape, F32),
        name="fox_cumsum",
    )(cf_rows, bias_rows)


def _flash_scratch(rows, tk, mask_scratch=False):
    return [pltpu.VMEM((rows, LANES), F32), pltpu.VMEM((rows, 2 * LANES), F32),
            pltpu.VMEM((rows, tk), F32), pltpu.VMEM((rows, tk), F32),
            pltpu.VMEM((rows, LANES), F32), pltpu.VMEM((rows, LANES), F32)
            ] + ([pltpu.VMEM((rows, tk), F32)] if mask_scratch else [])


def _flash_reset(m_ref, acc_ref):
    m_ref[...] = jnp.full(m_ref.shape, NEG, F32)
    acc_ref[...] = jnp.zeros(acc_ref.shape, F32)


def _flash_begin(m_ref, acc_ref, cm_ref, tq):
    _flash_reset(m_ref, acc_ref)
    cm_ref[...] = _causal_bias(cm_ref.shape[0], cm_ref.shape[1], tq)


def _row_max(s):
    return jnp.broadcast_to(jnp.max(s, axis=-1, keepdims=True), (s.shape[0], LANES))


def _causal_bias(rows, tk, tq):
    r = lax.broadcasted_iota(jnp.int32, (rows, tk), 0) & (tq - 1)
    c = lax.broadcasted_iota(jnp.int32, (rows, tk), 1)
    return jnp.where(c <= r, 0.0, NEG)


def _put_logits(buf, s, rows=slice(None), diag=False, cm_ref=None):
    if diag is True:
        s = s + cm_ref[rows]
    elif diag is not False:
        s = s + diag.astype(F32) * cm_ref[rows]
    buf[0][rows] = s
    buf[1][rows] = _row_max(s)


def _with_ones(v):
    return jnp.concatenate([v, jnp.ones((v.shape[0], LANES), v.dtype)], axis=1)


def _flash_consume(buf, v, m_ref, acc_ref, mask_tq=None, rows=slice(None)):
    s = buf[0][rows]
    m_cur = buf[1][rows]
    if mask_tq is not None:
        s = s + _causal_bias(s.shape[0], s.shape[1], mask_tq)
        m_cur = _row_max(s)
    m_old = m_ref[rows]
    m_new = jnp.maximum(m_old, m_cur)
    alpha = jnp.exp2(m_old - m_new)
    p = jnp.exp2(s - jnp.tile(m_new, (1, s.shape[1] // LANES))).astype(CDT)
    acc_ref[rows] = jnp.tile(alpha, (1, 2)) * acc_ref[rows] + _dot(p, _with_ones(v))
    m_ref[rows] = m_new


def _flash_result(acc):
    return acc[:, :LANES] / acc[:, LANES:]


def _causal_schedule(nq):
    ent = [(qi, ki, int(ki == qi)) for qi in range(nq) for ki in range(qi + 1)]
    n = len(ent)
    a = np.asarray(ent + [ent[-1]] * 2, np.int32)
    return n, tuple(jnp.asarray(a[:, i]) for i in range(3))


def _flash_stream(n, sched, base, produce, consume, finish, buf_a, buf_b, mask_at_produce):
    qt, kt, lt = sched

    def step(cur, nxt, t, diag, next_diag):
        if nxt is not None:
            produce(nxt, qt[base + t + 1], kt[base + t + 1], next_diag if mask_at_produce else False)
        consume(cur, kt[base + t], diag and not mask_at_produce)
        if diag:
            finish(qt[base + t])

    produce(buf_a, qt[base], kt[base], mask_at_produce)

    def pair(j, c):
        t = 2 * j
        l0, l1 = lt[base + t], lt[base + t + 1]
        for d0 in (False, True):
            for d1 in (False, True):
                @pl.when(((l0 != 0) == d0) & ((l1 != 0) == d1))
                def _():
                    step(buf_a, buf_b, t, d0, d1)
                    step(buf_b, buf_a, t + 1, d1, lt[base + t + 2])
        return c

    lax.fori_loop(0, n // 2, pair, 0)

    def tail():
        step(buf_a, None, n - 1, True, None)

    if isinstance(n, int):
        if n % 2 == 1:
            tail()
    else:
        pl.when(n % 2 == 1)(tail)


def _tile(ref, i, t):
    return ref[0, pl.ds(pl.multiple_of(i * t, t), t), :]


def _diff_attn_kernel(qt_ref, kt_ref, lt_ref, slopes_ref, lam_ref, g_ref, q_ref, k_ref, v_ref, o_ref,
                      m_ref, acc_ref, sa_ref, sb_ref, ma_ref, mb_ref, *, tq, n, lam_init):
    slope = slopes_ref[pl.program_id(1)]
    _flash_reset(m_ref, acc_ref)
    col = lax.broadcasted_iota(jnp.int32, (1, tq), 1).astype(F32)
    lane = lax.broadcasted_iota(jnp.int32, (tq, LANES), 1)
    lf = lam_ref[...]
    lam = (jnp.exp(jnp.sum(lf[0:1] * lf[1:2], axis=-1, keepdims=True))
           - jnp.exp(jnp.sum(lf[2:3] * lf[3:4], axis=-1, keepdims=True)) + lam_init)

    def produce(buf, qi, ki, diag):
        q = _tile(q_ref, qi, tq)
        zero = jnp.zeros_like(q)
        qq = jnp.concatenate([jnp.where(lane < DIFF_DH, q, zero), jnp.where(lane >= DIFF_DH, q, zero)], axis=0)
        s = _dot_nt(qq, _tile(k_ref, ki, tq))
        _put_logits(buf, s + slope * (col + ((ki - qi) * tq).astype(F32)))

    def consume(buf, ki, diag):
        _flash_consume(buf, _tile(v_ref, ki, tq), m_ref, acc_ref, tq if diag else None)

    def finish(qi):
        o = _flash_result(acc_ref[...])
        d = o[0:tq] - lam * o[tq:2 * tq]
        o_ref[0, pl.ds(pl.multiple_of(qi * tq, tq), tq), :] = (
            _rms(d, g_ref[...]) * (1.0 - lam_init)).astype(o_ref.dtype)
        _flash_reset(m_ref, acc_ref)

    _flash_stream(n, (qt_ref, kt_ref, lt_ref), 0, produce, consume, finish, (sa_ref, ma_ref), (sb_ref, mb_ref),
                  mask_at_produce=False)


_SMEM = pl.BlockSpec(memory_space=pltpu.SMEM)


def _diff_attention(proj3, diff_lambda, subln, lam_init):
    b, s, _ = proj3.shape
    tq = min(TQ_DENSE, s)
    dv = 2 * DIFF_DH
    n, sched = _causal_schedule(s // tq)
    kern = functools.partial(_diff_attn_kernel, tq=tq, n=n, lam_init=lam_init)
    return pl.pallas_call(
        kern,
        grid=(b, DIFF_HEADS),
        in_specs=[
            _SMEM, _SMEM, _SMEM, _SMEM,
            pl.BlockSpec((4, DIFF_DH), lambda bi, h: (0, 0)),
            pl.BlockSpec((1, dv), lambda bi, h: (0, 0)),
            pl.BlockSpec((1, s, LANES), lambda bi, h: (bi, 0, PB_AQ + h)),
            pl.BlockSpec((1, s, LANES), lambda bi, h: (bi, 0, PB_AK + h)),
            pl.BlockSpec((1, s, LANES), lambda bi, h: (bi, 0, PB_AV + h)),
        ],
        out_specs=pl.BlockSpec((1, s, dv), lambda bi, h: (bi, 0, h)),
        out_shape=jax.ShapeDtypeStruct((b, s, DIFF_HEADS * dv), CDT),
        scratch_shapes=_flash_scratch(2 * tq, tq),
        compiler_params=_cparams(("parallel", "parallel")),
        name="diff_attention",
    )(*sched, jnp.asarray(_alibi_slopes(DIFF_HEADS)), diff_lambda, subln.reshape(1, dv), proj3, proj3, proj3)


def _mla_prep_kernel(cq_ref, ckv_ref, kr_ref, krs_ref, gq_ref, gkv_ref, wqm_ref, wqs_ref, wk_ref, wv_ref,
                     cosq_ref, sinq_ref, cosk_ref, sink_ref, q_ref, k_ref, v_ref):
    hq = _rms(cq_ref[0].astype(F32), gq_ref[...]).astype(CDT)
    qm = _dot(hq, wqm_ref[...])
    qs = _dot(hq, wqs_ref[...])
    cosq, sinq = cosq_ref[...], sinq_ref[...]
    hw = 2 * LANES
    for h in range(MLA_HEADS):
        sl = slice(h * hw, (h + 1) * hw)
        q_ref[0, :, sl] = (qm[:, sl] * cosq + qs[:, sl] * sinq).astype(q_ref.dtype)
    hkv = _rms(ckv_ref[0].astype(F32), gkv_ref[...]).astype(CDT)
    kn = _dot(hkv, wk_ref[...])
    v_ref[0] = _dot(hkv, wv_ref[...]).astype(v_ref.dtype)
    kpe = (kr_ref[0].astype(F32) * cosk_ref[...] + krs_ref[0].astype(F32) * sink_ref[...]).astype(k_ref.dtype)
    for h in range(MLA_HEADS):
        k_ref[0, :, h * hw:h * hw + LANES] = kn[:, h * LANES:(h + 1) * LANES].astype(k_ref.dtype)
        k_ref[0, :, h * hw + LANES:(h + 1) * hw] = kpe


def _mla_prep(proj3, gq, gkv, wqm, wqs, wk, wv, tabs):
    b, s, _ = proj3.shape
    tm = min(TM_PROJ, s)
    hw = 2 * LANES
    cosq, sinq, cosk, sink = tabs
    const = lambda shape: pl.BlockSpec(shape, lambda bi, i: (0,) * len(shape))
    return pl.pallas_call(
        _mla_prep_kernel,
        grid=(b, s // tm),
        in_specs=[
            pl.BlockSpec((1, tm, MLA_Q_LORA), lambda bi, i: (bi, i, PB_BCQ // 2)),
            pl.BlockSpec((1, tm, MLA_KV_LORA), lambda bi, i: (bi, i, PB_BCKV // 2)),
            pl.BlockSpec((1, tm, LANES), lambda bi, i: (bi, i, PB_BKR)),
            pl.BlockSpec((1, tm, LANES), lambda bi, i: (bi, i, PB_BKRS)),
            const((1, MLA_Q_LORA)), const((1, MLA_KV_LORA)),
            const((MLA_Q_LORA, MLA_HEADS * hw)), const((MLA_Q_LORA, MLA_HEADS * hw)),
            const((MLA_KV_LORA, MLA_HEADS * MLA_NOPE)), const((MLA_KV_LORA, MLA_HEADS * MLA_VDIM)),
            pl.BlockSpec((tm, hw), lambda bi, i: (i, 0)), pl.BlockSpec((tm, hw), lambda bi, i: (i, 0)),
            pl.BlockSpec((tm, LANES), lambda bi, i: (i, 0)), pl.BlockSpec((tm, LANES), lambda bi, i: (i, 0)),
        ],
        out_specs=[
            pl.BlockSpec((1, tm, MLA_HEADS * hw), lambda bi, i: (bi, i, 0)),
            pl.BlockSpec((1, tm, MLA_HEADS * hw), lambda bi, i: (bi, i, 0)),
            pl.BlockSpec((1, tm, MLA_HEADS * MLA_VDIM), lambda bi, i: (bi, i, 0)),
        ],
        out_shape=[
            jax.ShapeDtypeStruct((b, s, MLA_HEADS * hw), CDT),
            jax.ShapeDtypeStruct((b, s, MLA_HEADS * hw), CDT),
            jax.ShapeDtypeStruct((b, s, MLA_HEADS * MLA_VDIM), CDT),
        ],
        compiler_params=_cparams(("parallel", "parallel")),
        name="mla_prep",
    )(proj3, proj3, proj3, proj3, gq.reshape(1, -1), gkv.reshape(1, -1), wqm, wqs, wk, wv,
      cosq, sinq, cosk, sink)


def _plain_attn_kernel(qt_ref, kt_ref, lt_ref, q_ref, k_ref, v_ref, o_ref,
                       m_ref, acc_ref, sa_ref, sb_ref, ma_ref, mb_ref, *, tq, n, hp, dk, dv):
    _flash_reset(m_ref, acc_ref)
    heads = [(slice(h * tq, (h + 1) * tq), slice(h * dk, (h + 1) * dk), slice(h * dv, (h + 1) * dv))
             for h in range(hp)]

    def produce(buf, qi, ki, diag):
        q, k = _tile(q_ref, qi, tq), _tile(k_ref, ki, tq)
        for rows, kcols, _ in heads:
            _put_logits(buf, _dot_nt(q[:, kcols], k[:, kcols]), rows)

    def consume(buf, ki, diag):
        v = _tile(v_ref, ki, tq)
        for rows, _, vcols in heads:
            _flash_consume(buf, v[:, vcols], m_ref, acc_ref, tq if diag else None, rows)

    def finish(qi):
        for rows, _, vcols in heads:
            o_ref[0, pl.ds(pl.multiple_of(qi * tq, tq), tq), vcols] = _flash_result(acc_ref[rows]).astype(o_ref.dtype)
        _flash_reset(m_ref, acc_ref)

    _flash_stream(n, (qt_ref, kt_ref, lt_ref), 0, produce, consume, finish, (sa_ref, ma_ref), (sb_ref, mb_ref),
                  mask_at_produce=False)


def _mla_attention(qc, kc, v):
    b, s, _ = qc.shape
    tq = min(TQ_DENSE, s)
    hw = 2 * LANES
    hp = MLA_HP
    n, sched = _causal_schedule(s // tq)
    return pl.pallas_call(
        functools.partial(_plain_attn_kernel, tq=tq, n=n, hp=hp, dk=hw, dv=MLA_VDIM),
        grid=(b, MLA_HEADS // hp),
        in_specs=[
            _SMEM, _SMEM, _SMEM,
            pl.BlockSpec((1, s, hp * hw), lambda bi, h: (bi, 0, h)),
            pl.BlockSpec((1, s, hp * hw), lambda bi, h: (bi, 0, h)),
            pl.BlockSpec((1, s, hp * MLA_VDIM), lambda bi, h: (bi, 0, h)),
        ],
        out_specs=pl.BlockSpec((1, s, hp * MLA_VDIM), lambda bi, h: (bi, 0, h)),
        out_shape=jax.ShapeDtypeStruct((b, s, MLA_HEADS * MLA_VDIM), CDT),
        scratch_shapes=_flash_scratch(hp * tq, tq),
        compiler_params=_cparams(("parallel", "parallel")),
        name="mla_attention",
    )(*sched, qc, kc, v)


def _fox_attn_kernel(qt_ref, kt_ref, lt_ref, c_ref, q_ref, k_ref, v_ref, o_ref,
                     m_ref, acc_ref, sa_ref, sb_ref, ma_ref, mb_ref, *, tq, n, hp):
    _flash_reset(m_ref, acc_ref)
    heads = [(slice(h * tq, (h + 1) * tq), slice(h * FOX_DH, (h + 1) * FOX_DH)) for h in range(hp)]

    def produce(buf, qi, ki, diag):
        q, k = _tile(q_ref, qi, tq), _tile(k_ref, ki, tq)
        for h, (rows, cols) in enumerate(heads):
            cbase = c_ref[0, h, pl.ds(qi, 1), :][:, 0:1]
            s = _dot_nt(q[:, cols], k[:, cols]) + LOG2E * (cbase - c_ref[0, h, pl.ds(ki, 1), :])
            _put_logits(buf, s, rows)

    def consume(buf, ki, diag):
        v = _tile(v_ref, ki, tq)
        for rows, cols in heads:
            _flash_consume(buf, v[:, cols], m_ref, acc_ref, tq if diag else None, rows)

    def finish(qi):
        for rows, cols in heads:
            o_ref[0, pl.ds(pl.multiple_of(qi * tq, tq), tq), cols] = _flash_result(acc_ref[rows]).astype(o_ref.dtype)
        _flash_reset(m_ref, acc_ref)

    _flash_stream(n, (qt_ref, kt_ref, lt_ref), 0, produce, consume, finish, (sa_ref, ma_ref), (sb_ref, mb_ref),
                  mask_at_produce=False)


def _fox_attention(proj3, c4):
    b, s, _ = proj3.shape
    tq = min(TQ_DENSE, s)
    nk = s // tq
    hp = FOX_HP
    w = hp * FOX_DH
    n, sched = _causal_schedule(nk)
    return pl.pallas_call(
        functools.partial(_fox_attn_kernel, tq=tq, n=n, hp=hp),
        grid=(b, FOX_HEADS // hp),
        in_specs=[
            _SMEM, _SMEM, _SMEM,
            pl.BlockSpec((1, hp, nk, tq), lambda bi, h: (bi, h, 0, 0)),
            pl.BlockSpec((1, s, w), lambda bi, h: (bi, 0, PB_CQ // hp + h)),
            pl.BlockSpec((1, s, w), lambda bi, h: (bi, 0, PB_CK // hp + h)),
            pl.BlockSpec((1, s, w), lambda bi, h: (bi, 0, PB_CV // hp + h)),
        ],
        out_specs=pl.BlockSpec((1, s, w), lambda bi, h: (bi, 0, h)),
        out_shape=jax.ShapeDtypeStruct((b, s, FOX_HEADS * FOX_DH), CDT),
        scratch_shapes=_flash_scratch(hp * tq, tq),
        compiler_params=_cparams(("parallel", "parallel")),
        name="fox_attention",
    )(*sched, c4.reshape(b, FOX_HEADS, nk, tq), proj3, proj3, proj3)


def _nsa_compress_kernel(x_ref, w1a_ref, w1b_ref, pea_ref, peb_ref, w2_ref, o_ref):
    x = x_ref[0]
    n = x.shape[0]
    pa = _dot(x, w1a_ref[...])
    pb = _dot(x, w1b_ref[...])
    pe = _dot(pea_ref[...], w1a_ref[...]) + _dot(peb_ref[...], w1b_ref[...])
    hid = pa + pltpu.roll(pb, n - 1, axis=0) + pe[0:1]
    act = 0.5 * hid * (1.0 + jnp.tanh(math.sqrt(2.0 / math.pi) * (hid + 0.044715 * hid * hid * hid)))
    o_ref[0] = _dot(act.astype(CDT), w2_ref[...]).astype(o_ref.dtype)


def _nsa_compress(xc, w1a, w1b, pea, peb, w2):
    b, n, kdim = xc.shape
    hdim = w1a.shape[1]
    const = lambda shape: pl.BlockSpec(shape, lambda bi: (0,) * len(shape))
    return pl.pallas_call(
        _nsa_compress_kernel,
        grid=(b,),
        in_specs=[pl.BlockSpec((1, n, kdim), lambda bi: (bi, 0, 0)),
                  const((kdim, hdim)), const((kdim, hdim)), const((8, kdim)), const((8, kdim)),
                  const((hdim, w2.shape[1]))],
        out_specs=pl.BlockSpec((1, n, w2.shape[1]), lambda bi: (bi, 0, 0)),
        out_shape=jax.ShapeDtypeStruct((b, n, w2.shape[1]), CDT),
        compiler_params=_cparams(("parallel",)),
        name="nsa_compress",
    )(xc, w1a, w1b, pea, peb, w2)


def _nsa_cmp_kernel(slopes_ref, q_ref, kv_ref, oc_ref, sb_ref, used_ref, *, tq, n_topk):
    qi = pl.program_id(1)
    nblk = kv_ref.shape[1]
    q0 = qi * tq
    rowpos = q0 + lax.broadcasted_iota(jnp.int32, (tq, 1), 0)
    cmp_end = lax.broadcasted_iota(jnp.int32, (1, nblk), 1) * CMP_STRIDE + (CMP_LEN - 1)
    negmask = jnp.where(rowpos >= cmp_end, 0.0, NEG)
    end_rel = (cmp_end - q0).astype(F32)
    lane = lax.broadcasted_iota(jnp.int32, (tq, LANES), 1)
    low = lane < NSA_DH
    nn = lax.broadcasted_iota(jnp.int32, (NSA_DH, nblk), 1) * CMP_STRIDE
    jj = lax.broadcasted_iota(jnp.int32, (NSA_DH, nblk), 0) * SLC_LEN
    ovt = (jnp.maximum(jnp.minimum(nn + CMP_LEN, jj + SLC_LEN) - jnp.maximum(nn, jj), 0).astype(F32)
           * (1.0 / CMP_LEN)).astype(CDT)
    jt = lax.broadcasted_iota(jnp.int32, (NSA_DH, tq), 0).astype(F32)
    blk = ((q0 + lax.broadcasted_iota(jnp.int32, (1, tq), 1)) >> SLC_SHIFT).astype(F32)
    fixed = (jt == 0.0) | (jt == blk) | (jt == blk - 1.0)
    beyond = jt > blk
    row_ok = rowpos >= CMP_LEN - 1
    outs = []
    bias = []
    for g in range(NSA_GROUPS):
        kc = kv_ref[0, :, g * LANES:(g + 1) * LANES]
        vc = kv_ref[0, :, (NSA_GROUPS + g) * LANES:(NSA_GROUPS + g + 1) * LANES]
        psum = jnp.zeros((tq, nblk), F32)
        mine = low if g == 0 else jnp.logical_not(low)
        zero = jnp.zeros((tq, LANES), q_ref.dtype)
        qs = jnp.concatenate([jnp.where(mine, q_ref[0, :, j * LANES:(j + 1) * LANES], zero)
                              for j in range(NSA_HPG)], axis=0)
        s_all = _dot_nt(qs, kc)
        ps = []
        for j in range(NSA_HPG):
            s = s_all[j * tq:(j + 1) * tq] + slopes_ref[g * NSA_HPG + j] * end_rel + negmask
            e = jnp.exp2(s - jnp.max(s, axis=-1, keepdims=True))
            den = jnp.sum(e, axis=-1, keepdims=True)
            p = e * jnp.where(row_ok, 1.0 / den, 0.0)
            psum = psum + p
            ps.append(p.astype(CDT))
        o_all = _dot(jnp.concatenate(ps, axis=0), vc)
        outs.extend(o_all[j * tq:(j + 1) * tq] for j in range(NSA_HPG))
        hi = psum.astype(CDT)
        lo = (psum - hi.astype(F32)).astype(CDT)
        imp = _dot_nt(ovt, hi) + _dot_nt(ovt, lo)
        imp = jnp.where(fixed, -jnp.inf, jnp.where(beyond, NEG_INF, imp))
        sbt = jnp.where(fixed, 0.0, NEG)
        for _ in range(n_topk - 3):
            mx = jnp.max(imp, axis=0, keepdims=True)
            idx = jnp.min(jnp.where(imp == mx, jt, float(LANES)), axis=0, keepdims=True)
            hit = jt == idx
            sbt = jnp.where(hit, 0.0, sbt)
            imp = jnp.where(hit, -jnp.inf, imp)
        bias.append(sbt)
    sb = jnp.concatenate([bias[1], bias[0]], axis=0).T
    sb_ref[0] = sb.astype(sb_ref.dtype)
    used = jnp.max(jnp.where(sb == 0.0, 1.0, 0.0), axis=0, keepdims=True)
    used_ref[0, 0] = jnp.broadcast_to(used, used_ref.shape[2:])
    for blk_i in range(NSA_HEADS // 2):
        oc_ref[0, :, blk_i * LANES:(blk_i + 1) * LANES] = jnp.where(
            low, outs[2 * blk_i], outs[2 * blk_i + 1]).astype(oc_ref.dtype)


def _nsa_cmp_select(proj3, kvc, n_topk):
    assert n_topk >= 3, "the three always-selected blocks must fit in the top-k budget"
    b, s, _ = proj3.shape
    tq = min(TQ_NSA, s)
    nblk = kvc.shape[1]
    return pl.pallas_call(
        functools.partial(_nsa_cmp_kernel, tq=tq, n_topk=n_topk),
        grid=(b, s // tq),
        in_specs=[
            pl.BlockSpec(memory_space=pltpu.SMEM),
            pl.BlockSpec((1, tq, 4 * LANES), lambda bi, qi: (bi, qi, PB_DQ // 4)),
            pl.BlockSpec((1, nblk, kvc.shape[2]), lambda bi, qi: (bi, 0, 0)),
        ],
        out_specs=[
            pl.BlockSpec((1, tq, NSA_HEADS * NSA_DH), lambda bi, qi: (bi, qi, 0)),
            pl.BlockSpec((1, tq, LANES), lambda bi, qi: (bi, qi, 0)),
            pl.BlockSpec((1, 1, 8, LANES), lambda bi, qi: (bi, qi, 0, 0)),
        ],
        out_shape=[jax.ShapeDtypeStruct((b, s, NSA_HEADS * NSA_DH), CDT),
                   jax.ShapeDtypeStruct((b, s, LANES), CDT),
                   jax.ShapeDtypeStruct((b, s // tq, 8, LANES), F32)],
        compiler_params=_cparams(("parallel", "parallel")),
        name="nsa_cmp_select",
    )(jnp.asarray(_alibi_slopes(NSA_HEADS)), proj3, kvc)


def _compact_heads(heads, mine, low):
    both = [jnp.where(mine, a, pltpu.roll(a, NSA_DH, axis=1)) for a in heads]
    out = [jnp.where(low, both[2 * jj], both[2 * jj + 1]) for jj in range(NSA_HPG // 2)]
    return jnp.concatenate(out, axis=1)


def _nsa_win_kernel(slopes_ref, q_ref, kp_ref, kc_ref, vp_ref, vc_ref, o_ref, *, tq):
    qi = pl.program_id(1)
    lane = lax.broadcasted_iota(jnp.int32, (tq, LANES), 1)
    low = lane < NSA_DH
    r = lax.broadcasted_iota(jnp.int32, (tq, tq), 0)
    c = lax.broadcasted_iota(jnp.int32, (tq, tq), 1)
    own = c <= r
    ndist = jnp.where(own, c - r, c - r - tq).astype(F32)
    own_f = jnp.where(own, 1.0, 0.0).astype(CDT)
    prev_pen = jnp.where(qi > 0, 0.0, NEG)
    q = q_ref[0]
    zero = jnp.zeros((tq, LANES), q.dtype)
    mine = (low, jnp.logical_not(low))
    qs = jnp.concatenate([jnp.where(mine[g], q[:, j * LANES:(j + 1) * LANES], zero)
                          for g in range(NSA_GROUPS) for j in range(NSA_HPG)], axis=0)
    s_own, s_prev = _dot_nt(qs, kc_ref[0]), _dot_nt(qs, kp_ref[0])
    ps = []
    for hd in range(NSA_HEADS):
        rows = slice(hd * tq, (hd + 1) * tq)
        s = jnp.where(own, s_own[rows], s_prev[rows] + prev_pen) + slopes_ref[hd] * ndist
        ps.append(jnp.exp2(s - jnp.max(s, axis=-1, keepdims=True)).astype(CDT))
    p = jnp.concatenate(ps, axis=0)
    p_own = p * jnp.tile(own_f, (NSA_HEADS, 1))
    o = _flash_result(_dot(p_own, _with_ones(vc_ref[0])) + _dot(p - p_own, _with_ones(vp_ref[0])))
    for g in range(NSA_GROUPS):
        heads = [o[(g * NSA_HPG + j) * tq:(g * NSA_HPG + j + 1) * tq] for j in range(NSA_HPG)]
        w = NSA_HPG * NSA_DH
        o_ref[0, :, g * w:(g + 1) * w] = _compact_heads(heads, mine[g], low).astype(o_ref.dtype)


def _nsa_window(proj3):
    b, s, _ = proj3.shape
    tq = WINDOW
    return pl.pallas_call(
        functools.partial(_nsa_win_kernel, tq=tq),
        grid=(b, s // tq),
        in_specs=[
            pl.BlockSpec(memory_space=pltpu.SMEM),
            pl.BlockSpec((1, tq, 4 * LANES), lambda bi, qi: (bi, qi, PB_DQ // 4)),
            pl.BlockSpec((1, tq, LANES), lambda bi, qi: (bi, jnp.maximum(qi - 1, 0), PB_WIN_K)),
            pl.BlockSpec((1, tq, LANES), lambda bi, qi: (bi, qi, PB_WIN_K)),
            pl.BlockSpec((1, tq, LANES), lambda bi, qi: (bi, jnp.maximum(qi - 1, 0), PB_WIN_V)),
            pl.BlockSpec((1, tq, LANES), lambda bi, qi: (bi, qi, PB_WIN_V)),
        ],
        out_specs=pl.BlockSpec((1, tq, NSA_HEADS * NSA_DH), lambda bi, qi: (bi, qi, 0)),
        out_shape=jax.ShapeDtypeStruct((b, s, NSA_HEADS * NSA_DH), CDT),
        compiler_params=_cparams(("parallel", "parallel")),
        name="nsa_window",
    )(jnp.asarray(_alibi_slopes(NSA_HEADS)), proj3, proj3, proj3, proj3, proj3)


def _nsa_sel_kernel(cnt_ref, qt_ref, kt_ref, lt_ref, slopes_ref, q_ref, sb_ref, k_ref, v_ref, oc_ref, ow_ref, gl_ref,
                    e_ref, o_ref, m_ref, acc_ref, sa_ref, sb2_ref, ma_ref, mb_ref, cm_ref, *, tq, rows_per_problem):
    g = pl.program_id(1)
    w = NSA_HPG * NSA_DH
    lane = lax.broadcasted_iota(jnp.int32, (tq, LANES), 1)
    low = lane < NSA_DH
    mine = (lane >> HALF_SHIFT) == g
    _flash_begin(m_ref, acc_ref, cm_ref, tq)
    col = lax.broadcasted_iota(jnp.int32, (1, tq), 1).astype(F32)
    jl = lane & (NSA_DH - 1)
    krow = lax.broadcasted_iota(jnp.int32, (tq, LANES), 0)

    def produce(buf, qi, ki, diag):
        q = _tile(q_ref, qi, tq)
        sb = _tile(sb_ref, qi, tq)
        qa = jnp.concatenate([jnp.where(mine, q[:, j * LANES:(j + 1) * LANES], sb) for j in range(NSA_HPG)], axis=0)
        k = _tile(k_ref, ki, tq)
        onehot = jnp.where(((ki * tq + krow) >> SLC_SHIFT) == jl, 1.0, 0.0).astype(k.dtype)
        s_all = _dot_nt(qa, jnp.where(mine, k, onehot))
        rel = ((ki - qi) * tq).astype(F32)
        for j in range(NSA_HPG):
            rows = slice(j * tq, (j + 1) * tq)
            _put_logits(buf, s_all[rows] + slopes_ref[g * NSA_HPG + j] * (col + rel), rows, diag, cm_ref)

    def consume(buf, ki, diag):
        _flash_consume(buf, _tile(v_ref, ki, tq), m_ref, acc_ref, tq if diag else None)

    def finish(qi):
        o = _flash_result(acc_ref[...])
        o_s = _compact_heads([o[j * tq:(j + 1) * tq] for j in range(NSA_HPG)], mine, low)
        gates = _split_dot(_sigmoid(_tile(gl_ref, qi, tq)), e_ref[0])
        y = (gates[:, 0:w] * _tile(oc_ref, qi, tq).astype(F32) + gates[:, w:2 * w] * o_s
             + gates[:, 2 * w:3 * w] * _tile(ow_ref, qi, tq).astype(F32))
        o_ref[0, pl.ds(pl.multiple_of(qi * tq, tq), tq), :] = y.astype(o_ref.dtype)
        _flash_reset(m_ref, acc_ref)

    prob = pl.program_id(0) * NSA_GROUPS + g
    _flash_stream(cnt_ref[prob], (qt_ref, kt_ref, lt_ref), prob * rows_per_problem, produce, consume, finish,
                  (sa_ref, ma_ref), (sb2_ref, mb_ref), mask_at_produce=True)


def _nsa_selected(proj3, sbias, used, o_c, o_w, small3, expand):
    b, s, _ = proj3.shape
    tq = min(TQ_NSA, s)
    nq = s // tq
    w = NSA_HPG * NSA_DH
    u = used[:, :, 0, :].reshape(b, nq, NSA_GROUPS, NSA_DH)[:, :, ::-1, :nq * (tq // SLC_LEN)]
    flags = (u.reshape(b, nq, NSA_GROUPS, nq, tq // SLC_LEN).max(axis=-1) > 0.0).astype(jnp.int32)
    flags = flags.transpose(0, 2, 1, 3)
    qt = jnp.arange(nq, dtype=jnp.int32)
    need = jnp.where(qt[None, :] < qt[:, None], flags, (qt[None, :] == qt[:, None]).astype(jnp.int32))
    need = need.reshape(b, NSA_GROUPS, nq * nq)
    cnt = need.sum(axis=-1).astype(jnp.int32)
    order = jnp.argsort(1 - need, axis=-1, stable=True).astype(jnp.int32)
    order = jnp.pad(order, ((0, 0), (0, 0), (0, 2)))
    rows = nq * nq + 2
    sched = (order // nq, order % nq, (order // nq == order % nq).astype(jnp.int32))
    return pl.pallas_call(
        functools.partial(_nsa_sel_kernel, tq=tq, rows_per_problem=rows),
        grid=(b, NSA_GROUPS),
        in_specs=[
            _SMEM, _SMEM, _SMEM, _SMEM, _SMEM,
            pl.BlockSpec((1, s, 4 * LANES), lambda bi, g: (bi, 0, PB_DQ // 4)),
            pl.BlockSpec((1, s, LANES), lambda bi, g: (bi, 0, 0)),
            pl.BlockSpec((1, s, LANES), lambda bi, g: (bi, 0, PB_SEL_K)),
            pl.BlockSpec((1, s, LANES), lambda bi, g: (bi, 0, PB_SEL_V)),
            pl.BlockSpec((1, s, w), lambda bi, g: (bi, 0, g)),
            pl.BlockSpec((1, s, w), lambda bi, g: (bi, 0, g)),
            pl.BlockSpec((1, s, LANES), lambda bi, g: (bi, 0, 0)),
            pl.BlockSpec((1, LANES, 3 * w), lambda bi, g: (g, 0, 0)),
        ],
        out_specs=pl.BlockSpec((1, s, w), lambda bi, g: (bi, 0, g)),
        out_shape=jax.ShapeDtypeStruct((b, s, NSA_HEADS * NSA_DH), CDT),
        scratch_shapes=_flash_scratch(NSA_HPG * tq, tq, mask_scratch=True),
        compiler_params=_cparams(("parallel", "parallel")),
        name="nsa_selected",
    )(cnt.reshape(-1), *[t.reshape(-1) for t in sched], jnp.asarray(_alibi_slopes(NSA_HEADS)),
      proj3, sbias, proj3, proj3, o_c, o_w, small3, expand)


def _merge_kernel(ya_ref, yb_ref, yc_ref, yd_ref, ga_ref, gb_ref, gc_ref, gd_ref, wb_ref, wo_ref, x_ref, o_ref):
    merged = None
    for n, (y_ref, g_ref) in enumerate(((ya_ref, ga_ref), (yb_ref, gb_ref), (yc_ref, gc_ref), (yd_ref, gd_ref))):
        t = _sigmoid(g_ref[...].astype(F32)) * _dot(y_ref[...], wb_ref[n])
        merged = t if merged is None else merged + t
    o_ref[...] = x_ref[...] + _dot(merged.astype(CDT), wo_ref[...])


def _merge(ys, proj2, wb, wo, x2, layer):
    t, d = x2.shape
    tm = min(TM_ROWS, t)
    gate_blk = PB_GATE * LANES // d
    yspec = pl.BlockSpec((tm, BRANCH_WIDTH), lambda i: (i, 0))
    gspecs = [pl.BlockSpec((tm, d), functools.partial(lambda i, n: (i, gate_blk + n), n=n)) for n in range(N_BRANCH)]
    return pl.pallas_call(
        _merge_kernel,
        grid=(t // tm,),
        in_specs=[yspec] * N_BRANCH + gspecs + [
            pl.BlockSpec((None, N_BRANCH, BRANCH_WIDTH, d), lambda i: (layer, 0, 0, 0)),
            pl.BlockSpec((None, d, d), lambda i: (layer, 0, 0)),
            pl.BlockSpec((tm, d), lambda i: (i, 0)),
        ],
        out_specs=pl.BlockSpec((tm, d), lambda i: (i, 0)),
        out_shape=jax.ShapeDtypeStruct((t, d), F32),
        compiler_params=_cparams(("parallel",)),
        name="merge",
    )(*ys, proj2, proj2, proj2, proj2, wb, wo, x2)


HALO = 16


def _ffn_kernel(x_ref, xh_ref, g_ref, wu_ref, cw_ref, cb_ref, wd_ref, gf_ref, o_ref, he_ref, u_ref, act_ref,
                *, tm, fc, final):
    i = pl.program_id(1)
    x = x_ref[0]
    g = g_ref[...]
    xh = xh_ref[0] * (i > 0).astype(F32)
    he_ref[0:HALO] = _rms(xh, g).astype(CDT)
    he_ref[HALO:HALO + tm] = _rms(x, g).astype(CDT)
    he = he_ref[...]
    for c in range(D_FF // fc):
        outs = []
        for half in range(2):
            ub = u_ref.at[c % 2, half]
            lo = half * D_FF + c * fc
            ub[...] = _dot(he, wu_ref[:, lo:lo + fc])
            conv = cb_ref[:, lo:lo + fc]
            for kk in range(CONV_WIDTH):
                off = HALO - (CONV_WIDTH - 1) + kk
                conv = conv + cw_ref[kk:kk + 1, lo:lo + fc] * ub[off:off + tm, :]
            outs.append(conv)
        a, gg = outs
        act_ref[:, c * fc:(c + 1) * fc] = (a * _sigmoid(a) * gg).astype(CDT)
    y = x + _dot(act_ref[...], wd_ref[...])
    if final:
        y = _rms(y, gf_ref[...])
    o_ref[0] = y


def _ffn(x3, g, wu, cw, cb, wd, gf, layer, final):
    b, s, d = x3.shape
    tm = min(TM_ROWS, s)
    fc = FFN_CHUNK
    assert D_FF % fc == 0
    const = lambda shape: pl.BlockSpec(shape, lambda bi, i: (0,) * len(shape), pipeline_mode=pl.Buffered(1))
    stacked = lambda shape: pl.BlockSpec((None,) + shape, lambda bi, i: (layer,) + (0,) * len(shape),
                                         pipeline_mode=pl.Buffered(1))
    return pl.pallas_call(
        functools.partial(_ffn_kernel, tm=tm, fc=fc, final=final),
        grid=(b, s // tm),
        in_specs=[
            pl.BlockSpec((1, tm, d), lambda bi, i: (bi, i, 0)),
            pl.BlockSpec((1, HALO, d), lambda bi, i: (bi, jnp.maximum(i * (tm // HALO) - 1, 0), 0)),
            const((1, d)), stacked((d, 2 * D_FF)), const((CONV_WIDTH, 2 * D_FF)), const((1, 2 * D_FF)),
            stacked((D_FF, d)), const((1, d)),
        ],
        out_specs=pl.BlockSpec((1, tm, d), lambda bi, i: (bi, i, 0)),
        out_shape=jax.ShapeDtypeStruct((b, s, d), F32),
        scratch_shapes=[pltpu.VMEM((tm + HALO, d), CDT), pltpu.VMEM((2, 2, tm + HALO, fc), F32),
                        pltpu.VMEM((tm, D_FF), CDT)],
        compiler_params=_cparams(("parallel", "arbitrary")),
        name="conv_glu_mlp",
    )(x3, x3, g.reshape(1, d), wu, cw, cb.reshape(1, -1), wd, gf.reshape(1, d))


def _w_in_plan():
    widths = (512, 512, 512, MLA_Q_LORA, MLA_KV_LORA, MLA_ROPE, 512, 512, 512, FOX_HEADS,
              512, 768, 3 * NSA_HEADS, N_BRANCH * D_MODEL)
    (a_q, a_k, a_v, b_cq, b_ckv, b_kr, c_q, c_k, c_v, c_f, d_q, d_kv, d_g, gate, _) = np.cumsum((0,) + widths).tolist()
    half = MLA_ROPE // 2

    def run(src, nblocks, scale=1.0):
        return [[(src + i * LANES, LANES, scale)] for i in range(nblocks)]

    blocks = (run(a_q, 4, LOG2E * DIFF_DH ** -0.5) + run(a_k, 4) + run(a_v, 4)
              + run(c_q, 4, LOG2E * FOX_DH ** -0.5) + run(c_k, 4) + run(c_v, 4))
    sd = LOG2E * NSA_DH ** -0.5
    blocks += [[(d_q + j * NSA_DH, NSA_DH, sd), (d_q + (NSA_HPG + j) * NSA_DH, NSA_DH, sd)] for j in range(NSA_HPG)]
    blocks += run(d_kv, 6) + run(b_cq, 2) + run(b_ckv, 2)
    blocks += [[(b_kr, MLA_ROPE, 1.0), None],
               [(b_kr + half, half, -1.0), (b_kr, half, 1.0), None]]
    blocks += run(gate, N_BRANCH * D_MODEL // LANES)
    assert len(blocks) * LANES == N_PROJ
    small = [(c_f, FOX_HEADS, 1.0), (d_g, 3 * NSA_HEADS, 1.0), None]
    return blocks, small


def _w_in_relayout_kernel(w_ref, big_ref, small_ref):
    cols = w_ref.shape[1]
    blocks, small = _w_in_plan()
    for j, pieces in enumerate(blocks):
        row = j * LANES
        for p in pieces:
            if p is None:
                big_ref[row:(j + 1) * LANES, :] = jnp.zeros(((j + 1) * LANES - row, cols), big_ref.dtype)
            else:
                src, n, scale = p
                v = w_ref[src:src + n, :]
                big_ref[row:row + n, :] = (v if scale == 1.0 else v * scale).astype(big_ref.dtype)
                row += n
    (sf, nf, _), (sg, ng, _), _ = small
    r = lax.broadcasted_iota(jnp.int32, (LANES, cols), 0)
    side = jnp.where(r < nf, w_ref[sf:sf + LANES, :],
                     jnp.where(r < nf + ng, w_ref[sg - nf:sg - nf + LANES, :], 0.0))
    small_ref[...] = side.astype(small_ref.dtype)


def _w_in_relayout(w):
    nl, d, n = w.shape
    wt = jnp.transpose(w, (2, 0, 1)).reshape(n, nl * d)
    tc = min(256, d)
    return pl.pallas_call(
        _w_in_relayout_kernel,
        grid=(nl, d // tc),
        in_specs=[pl.BlockSpec((n, tc), lambda l, i: (0, l * (d // tc) + i))],
        out_specs=[pl.BlockSpec((None, N_PROJ, tc), lambda l, i: (l, 0, i)),
                   pl.BlockSpec((None, LANES, tc), lambda l, i: (l, 0, i))],
        out_shape=[jax.ShapeDtypeStruct((nl, N_PROJ, d), CDT), jax.ShapeDtypeStruct((nl, LANES, d), CDT)],
        compiler_params=_cparams(("parallel", "parallel")),
        name="w_in_relayout",
    )(wt)


def _prep_mla(w_uq, w_ukv):
    r = w_uq.shape[0]
    hw = 2 * LANES
    half = MLA_ROPE // 2
    scale = LOG2E * (MLA_NOPE + MLA_ROPE) ** -0.5
    wq = (w_uq * scale).reshape(r, MLA_HEADS, MLA_NOPE + MLA_ROPE)
    nope, t1, t2 = wq[..., :MLA_NOPE], wq[..., MLA_NOPE:MLA_NOPE + half], wq[..., MLA_NOPE + half:]
    zpad = jnp.zeros((r, MLA_HEADS, hw - MLA_NOPE - MLA_ROPE), w_uq.dtype)
    wqm = jnp.concatenate([nope, t1, t2, zpad], axis=-1).reshape(r, MLA_HEADS * hw)
    wqs = jnp.concatenate([jnp.zeros_like(nope), -t2, t1, zpad], axis=-1).reshape(r, MLA_HEADS * hw)
    wkv = w_ukv.reshape(w_ukv.shape[0], MLA_HEADS, MLA_NOPE + MLA_VDIM)
    wk = wkv[..., :MLA_NOPE].reshape(-1, MLA_HEADS * MLA_NOPE)
    wv = wkv[..., MLA_NOPE:].reshape(-1, MLA_HEADS * MLA_VDIM)
    return wqm.astype(CDT), wqs.astype(CDT), wk.astype(CDT), wv.astype(CDT)


def _rope_tables(s):
    half = MLA_ROPE // 2
    inv_freq = ROPE_THETA ** (-jnp.arange(0, MLA_ROPE, 2, dtype=F32) / MLA_ROPE)
    ang = jnp.arange(s, dtype=F32)[:, None] * inv_freq[None, :]
    cos, sin = jnp.cos(ang), jnp.sin(ang)
    z = jnp.zeros((s, LANES - MLA_ROPE), F32)
    cosk = jnp.concatenate([cos, cos, z], axis=1)
    sink = jnp.concatenate([sin, sin, z], axis=1)
    cosq = jnp.concatenate([jnp.ones((s, MLA_NOPE), F32), cosk], axis=1)
    sinq = jnp.concatenate([jnp.zeros((s, MLA_NOPE), F32), sink], axis=1)
    return cosq, sinq, cosk, sink


def _prep_compress(pe, w1, w2):
    eye2 = jnp.eye(2, dtype=F32)
    w1r = w1.reshape(2, CMP_LEN, NSA_DH, CMP_HIDDEN).astype(CDT)
    same = np.eye(2, dtype=bool)
    diag_kg = jnp.asarray(same[:, None, :, None] & same[None, :, None, :])

    def expand(wpart):
        src = wpart.transpose(1, 0, 2, 3)[:, :, None, :, None, None, :]
        t = jnp.where(diag_kg[None, :, :, None, :, :, None], src, jnp.zeros((), CDT))
        return t.reshape(CMP_STRIDE * 4 * NSA_DH, 4 * CMP_HIDDEN)

    w1a, w1b = expand(w1r[:, :CMP_STRIDE]), expand(w1r[:, CMP_STRIDE:])

    def pe_row(p):
        t = jnp.broadcast_to(p.transpose(1, 0, 2)[:, :, None, :], (CMP_STRIDE, 2, NSA_GROUPS, NSA_DH))
        return jnp.pad(t.reshape(1, -1), ((0, 7), (0, 0)))

    pea, peb = pe_row(pe[:, :CMP_STRIDE]), pe_row(pe[:, CMP_STRIDE:])
    w2b = jnp.einsum('khd,kK,gG,u->kghKGud', w2, eye2, eye2, jnp.ones((2,), F32))
    w2b = w2b.reshape(4 * CMP_HIDDEN, 4 * 2 * NSA_DH)
    return w1a.astype(CDT), w1b.astype(CDT), pea.astype(CDT), peb.astype(CDT), w2b.astype(CDT)


def _gate_expand():
    e = np.zeros((NSA_GROUPS, LANES, 3, NSA_HPG, NSA_DH), np.float32)
    for g in range(NSA_GROUPS):
        for j in range(NSA_HPG):
            for br in range(3):
                e[g, SMALL_G + (g * NSA_HPG + j) * 3 + br, br, j, :] = 1.0
    return jnp.asarray(e.reshape(NSA_GROUPS, LANES, 3 * NSA_HPG * NSA_DH)).astype(CDT)


def _token_mixers(x3, l, norm_mix, w_in, diff_lambda, diff_subln, mla_norm_q, mla_w_uq, mla_norm_kv, mla_w_ukv,
                  fox_b_f, nsa_cmp_pe, nsa_cmp_w1, nsa_cmp_w2, w_branch, w_out, rope_tabs):
    b, s, d = x3.shape
    t = b * s
    x2 = x3.reshape(t, d)
    proj, small = _in_proj(x2, norm_mix, *w_in, l)
    proj3 = proj.reshape(b, s, N_PROJ)
    small3 = small.reshape(b, s, LANES)

    lam_init = 0.8 - 0.6 * math.exp(-0.3 * l)
    y_a = _diff_attention(proj3, diff_lambda, diff_subln, lam_init)

    wqm, wqs, wk, wv = _prep_mla(mla_w_uq, mla_w_ukv)
    qc, kc, vv = _mla_prep(proj3, mla_norm_q, mla_norm_kv, wqm, wqs, wk, wv, rope_tabs)
    y_b = _mla_attention(qc, kc, vv)

    cf_rows = small3[:, :, SMALL_F:SMALL_F + FOX_HEADS].transpose(0, 2, 1).reshape(b * FOX_HEADS, s)
    bias_rows = jnp.tile(fox_b_f.astype(F32), b).reshape(b * FOX_HEADS, 1)
    c4 = _fox_cumsum(cf_rows, bias_rows)
    y_c = _fox_attention(proj3, c4)

    w1a, w1b, pea, peb, w2b = _prep_compress(nsa_cmp_pe, nsa_cmp_w1, nsa_cmp_w2)
    xc = proj3[:, :, PB_CMP_K * LANES:(PB_CMP_V + 1) * LANES].reshape(b, s // CMP_STRIDE, CMP_STRIDE * 2 * LANES)
    kvc = _nsa_compress(xc, w1a, w1b, pea, peb, w2b)
    n_topk = min(SLC_TOPK, s // SLC_LEN)
    o_c, sbias, used = _nsa_cmp_select(proj3, kvc, n_topk)
    o_w = _nsa_window(proj3)
    y_d = _nsa_selected(proj3, sbias, used, o_c, o_w, small3, _gate_expand())

    ys = [y.reshape(t, BRANCH_WIDTH) for y in (y_a, y_b, y_c, y_d)]
    return _merge(ys, proj, w_branch, w_out, x2, l).reshape(b, s, d)


def kernel(x, norm_mix, w_in, diff_lambda, diff_subln, mla_norm_q, mla_w_uq, mla_norm_kv, mla_w_ukv, fox_b_f,
           nsa_cmp_pe, nsa_cmp_w1, nsa_cmp_w2, w_branch, w_out, norm_ffn, w_up, conv_w, conv_b, w_down, norm_final):
    depth = w_in.shape[0]
    s = x.shape[1]
    rope_tabs = _rope_tables(s)
    w_in = _w_in_relayout(w_in)
    w_up16, w_down16 = w_up.astype(CDT), w_down.astype(CDT)
    w_branch16, w_out16 = w_branch.astype(CDT), w_out.astype(CDT)
    for l in range(depth):
        x = _token_mixers(x, l, norm_mix[l], w_in, diff_lambda[l], diff_subln[l], mla_norm_q[l], mla_w_uq[l],
                          mla_norm_kv[l], mla_w_ukv[l], fox_b_f[l], nsa_cmp_pe[l], nsa_cmp_w1[l], nsa_cmp_w2[l],
                          w_branch16, w_out16, rope_tabs)
        x = _ffn(x, norm_ffn[l], w_up16, conv_w[l], conv_b[l], w_down16, norm_final, l, final=(l == depth - 1))
    return x
```

```python
import functools
import math

import numpy as np
import jax
import jax.numpy as jnp
from jax import lax
from jax.experimental import pallas as pl
from jax.experimental.pallas import tpu as pltpu

F32 = jnp.float32
CDT = jnp.bfloat16

NEG = -1e30
NEG_INF = -1e30
NORM_EPS = 1e-6
LOG2E = 1.4426950408889634
LANES = 128

D_MODEL = 1024
DIFF_HEADS, DIFF_DH = 4, 64
MLA_HEADS, MLA_NOPE, MLA_ROPE, MLA_VDIM = 4, 128, 64, 128
MLA_Q_LORA, MLA_KV_LORA = 256, 256
ROPE_THETA = 10000.0
FOX_HEADS, FOX_DH = 4, 128
NSA_HEADS, NSA_GROUPS, NSA_DH = 8, 2, 64
NSA_HPG = NSA_HEADS // NSA_GROUPS
CMP_STRIDE = 16
CMP_LEN = 2 * CMP_STRIDE
CMP_HIDDEN = 128
SLC_LEN = 64
SLC_SHIFT = 6
HALF_SHIFT = 6
SLC_TOPK = 8
WINDOW = 256
N_BRANCH = 4
BRANCH_WIDTH = 512
D_FF = 2816
CONV_WIDTH = 3

PB_AQ, PB_AK, PB_AV = 0, 4, 8
PB_CQ, PB_CK, PB_CV = 12, 16, 20
PB_DQ = 24
PB_CMP_K, PB_CMP_V, PB_SEL_K, PB_SEL_V, PB_WIN_K, PB_WIN_V = 28, 29, 30, 31, 32, 33
PB_BCQ, PB_BCKV, PB_BKR, PB_BKRS = 34, 36, 38, 39
PB_GATE = 40
N_PROJ = 72 * LANES
SMALL_F, SMALL_G = 0, 4

VMEM_LIMIT = 56 * 1024 * 1024
MXU_TILE = 256
TQ_DENSE = 512
TQ_NSA = WINDOW
TM_PROJ, TN_PROJ = 1024, 9 * MXU_TILE
TM_ROWS = 512
FFN_CHUNK = MXU_TILE
FOX_HP = 2
MLA_HP = 2


def _cparams(sem):
    return pltpu.CompilerParams(dimension_semantics=sem, vmem_limit_bytes=VMEM_LIMIT)


def _rms(xf, g):
    return xf * lax.rsqrt(jnp.mean(xf * xf, axis=-1, keepdims=True) + NORM_EPS) * g


def _sigmoid(x):
    return 0.5 * jnp.tanh(0.5 * x) + 0.5


def _dot(a, b):
    return jnp.dot(a, b, preferred_element_type=F32)


def _dot_nt(a, b):
    return lax.dot_general(a, b, (((1,), (1,)), ((), ())), preferred_element_type=F32)


def _split_dot(a, b):
    hi = a.astype(CDT)
    lo = (a - hi.astype(F32)).astype(CDT)
    return _dot(hi, b) + _dot(lo, b)


def _alibi_slopes(n):
    return (LOG2E * np.exp2(-8.0 * np.arange(1, n + 1) / n)).astype(np.float32)


def _inproj_kernel(x_ref, g_ref, w_ref, ws_ref, o_ref, os_ref, h_ref):
    @pl.when(pl.program_id(1) == 0)
    def _():
        h = _rms(x_ref[...], g_ref[...]).astype(CDT)
        h_ref[...] = h
        os_ref[...] = _dot_nt(h, ws_ref[...])

    o_ref[...] = _dot_nt(h_ref[...], w_ref[...]).astype(o_ref.dtype)


def _in_proj(x2, g, w, ws, layer):
    t, d = x2.shape
    n = w.shape[1]
    tm = min(TM_PROJ, t)
    tn = TN_PROJ
    assert n % tn == 0
    return pl.pallas_call(
        _inproj_kernel,
        grid=(t // tm, n // tn),
        in_specs=[
            pl.BlockSpec((tm, d), lambda i, j: (i, 0)),
            pl.BlockSpec((1, d), lambda i, j: (0, 0)),
            pl.BlockSpec((None, tn, d), lambda i, j: (layer, j, 0)),
            pl.BlockSpec((None, LANES, d), lambda i, j: (layer, 0, 0)),
        ],
        out_specs=[
            pl.BlockSpec((tm, tn), lambda i, j: (i, j)),
            pl.BlockSpec((tm, LANES), lambda i, j: (i, 0)),
        ],
        out_shape=[jax.ShapeDtypeStruct((t, n), CDT), jax.ShapeDtypeStruct((t, LANES), F32)],
        scratch_shapes=[pltpu.VMEM((tm, d), CDT)],
        compiler_params=_cparams(("parallel", "arbitrary")),
        name="in_proj",
    )(x2, g.reshape(1, d), w, ws)


def _fox_cumsum_kernel(cf_ref, bf_ref, o_ref):
    rows, s = cf_ref.shape
    lane = lax.broadcasted_iota(jnp.int32, (rows, LANES), 1)
    carry = jnp.zeros((rows, 1), F32)
    for c in range(s // LANES):
        z = cf_ref[:, c * LANES:(c + 1) * LANES] + bf_ref[...]
        xs = jnp.minimum(z, 0.0) - jnp.log1p(jnp.exp(-jnp.abs(z)))
        d = 1
        while d < LANES:
            xs = xs + jnp.where(lane >= d, pltpu.roll(xs, d, axis=1), 0.0)
            d *= 2
        xs = xs + carry
        o_ref[:, c * LANES:(c + 1) * LANES] = xs
        carry = xs[:, LANES - 1:LANES]


def _fox_cumsum(cf_rows, bias_rows):
    return pl.pallas_call(
        _fox_cumsum_kernel,
        out_shape=jax.ShapeDtypeStruct(cf_rows.shape, F32),
        name="fox_cumsum",
    )(cf_rows, bias_rows)


def _flash_scratch(rows, tk, mask_scratch=False):
    return [pltpu.VMEM((rows, LANES), F32), pltpu.VMEM((rows, 2 * LANES), F32),
            pltpu.VMEM((rows, tk), F32), pltpu.VMEM((rows, tk), F32),
            pltpu.VMEM((rows, LANES), F32), pltpu.VMEM((rows, LANES), F32)
            ] + ([pltpu.VMEM((rows, tk), F32)] if mask_scratch else [])


def _flash_reset(m_ref, acc_ref):
    m_ref[...] = jnp.full(m_ref.shape, NEG, F32)
    acc_ref[...] = jnp.zeros(acc_ref.shape, F32)


def _flash_begin(m_ref, acc_ref, cm_ref, tq):
    _flash_reset(m_ref, acc_ref)
    cm_ref[...] = _causal_bias(cm_ref.shape[0], cm_ref.shape[1], tq)


def _row_max(s):
    return jnp.broadcast_to(jnp.max(s, axis=-1, keepdims=True), (s.shape[0], LANES))


def _causal_bias(rows, tk, tq):
    r = lax.broadcasted_iota(jnp.int32, (rows, tk), 0) & (tq - 1)
    c = lax.broadcasted_iota(jnp.int32, (rows, tk), 1)
    return jnp.where(c <= r, 0.0, NEG)


def _put_logits(buf, s, rows=slice(None), diag=False, cm_ref=None):
    if diag is True:
        s = s + cm_ref[rows]
    elif diag is not False:
        s = s + diag.astype(F32) * cm_ref[rows]
    buf[0][rows] = s
    buf[1][rows] = _row_max(s)


def _with_ones(v):
    return jnp.concatenate([v, jnp.ones((v.shape[0], LANES), v.dtype)], axis=1)


def _flash_consume(buf, v, m_ref, acc_ref, mask_tq=None, rows=slice(None)):
    s = buf[0][rows]
    m_cur = buf[1][rows]
    if mask_tq is not None:
        s = s + _causal_bias(s.shape[0], s.shape[1], mask_tq)
        m_cur = _row_max(s)
    m_old = m_ref[rows]
    m_new = jnp.maximum(m_old, m_cur)
    alpha = jnp.exp2(m_old - m_new)
    p = jnp.exp2(s - jnp.tile(m_new, (1, s.shape[1] // LANES))).astype(CDT)
    acc_ref[rows] = jnp.tile(alpha, (1, 2)) * acc_ref[rows] + _dot(p, _with_ones(v))
    m_ref[rows] = m_new


def _flash_result(acc):
    return acc[:, :LANES] / acc[:, LANES:]


def _causal_schedule(nq):
    ent = [(qi, ki, int(ki == qi)) for qi in range(nq) for ki in range(qi + 1)]
    n = len(ent)
    a = np.asarray(ent + [ent[-1]] * 2, np.int32)
    return n, tuple(jnp.asarray(a[:, i]) for i in range(3))


def _flash_stream(n, sched, base, produce, consume, finish, buf_a, buf_b, mask_at_produce):
    qt, kt, lt = sched

    def step(cur, nxt, t, diag, next_diag):
        if nxt is not None:
            produce(nxt, qt[base + t + 1], kt[base + t + 1], next_diag if mask_at_produce else False)
        consume(cur, kt[base + t], diag and not mask_at_produce)
        if diag:
            finish(qt[base + t])

    produce(buf_a, qt[base], kt[base], mask_at_produce)

    def pair(j, c):
        t = 2 * j
        l0, l1 = lt[base + t], lt[base + t + 1]
        for d0 in (False, True):
            for d1 in (False, True):
                @pl.when(((l0 != 0) == d0) & ((l1 != 0) == d1))
                def _():
                    step(buf_a, buf_b, t, d0, d1)
                    step(buf_b, buf_a, t + 1, d1, lt[base + t + 2])
        return c

    lax.fori_loop(0, n // 2, pair, 0)

    def tail():
        step(buf_a, None, n - 1, True, None)

    if isinstance(n, int):
        if n % 2 == 1:
            tail()
    else:
        pl.when(n % 2 == 1)(tail)


def _tile(ref, i, t):
    return ref[0, pl.ds(pl.multiple_of(i * t, t), t), :]


def _diff_attn_kernel(qt_ref, kt_ref, lt_ref, slopes_ref, lam_ref, g_ref, q_ref, k_ref, v_ref, o_ref,
                      m_ref, acc_ref, sa_ref, sb_ref, ma_ref, mb_ref, *, tq, n, lam_init):
    slope = slopes_ref[pl.program_id(1)]
    _flash_reset(m_ref, acc_ref)
    col = lax.broadcasted_iota(jnp.int32, (1, tq), 1).astype(F32)
    lane = lax.broadcasted_iota(jnp.int32, (tq, LANES), 1)
    lf = lam_ref[...]
    lam = (jnp.exp(jnp.sum(lf[0:1] * lf[1:2], axis=-1, keepdims=True))
           - jnp.exp(jnp.sum(lf[2:3] * lf[3:4], axis=-1, keepdims=True)) + lam_init)

    def produce(buf, qi, ki, diag):
        q = _tile(q_ref, qi, tq)
        zero = jnp.zeros_like(q)
        qq = jnp.concatenate([jnp.where(lane < DIFF_DH, q, zero), jnp.where(lane >= DIFF_DH, q, zero)], axis=0)
        s = _dot_nt(qq, _tile(k_ref, ki, tq))
        _put_logits(buf, s + slope * (col + ((ki - qi) * tq).astype(F32)))

    def consume(buf, ki, diag):
        _flash_consume(buf, _tile(v_ref, ki, tq), m_ref, acc_ref, tq if diag else None)

    def finish(qi):
        o = _flash_result(acc_ref[...])
        d = o[0:tq] - lam * o[tq:2 * tq]
        o_ref[0, pl.ds(pl.multiple_of(qi * tq, tq), tq), :] = (
            _rms(d, g_ref[...]) * (1.0 - lam_init)).astype(o_ref.dtype)
        _flash_reset(m_ref, acc_ref)

    _flash_stream(n, (qt_ref, kt_ref, lt_ref), 0, produce, consume, finish, (sa_ref, ma_ref), (sb_ref, mb_ref),
                  mask_at_produce=False)


_SMEM = pl.BlockSpec(memory_space=pltpu.SMEM)


def _diff_attention(proj3, diff_lambda, subln, lam_init):
    b, s, _ = proj3.shape
    tq = min(TQ_DENSE, s)
    dv = 2 * DIFF_DH
    n, sched = _causal_schedule(s // tq)
    kern = functools.partial(_diff_attn_kernel, tq=tq, n=n, lam_init=lam_init)
    return pl.pallas_call(
        kern,
        grid=(b, DIFF_HEADS),
        in_specs=[
            _SMEM, _SMEM, _SMEM, _SMEM,
            pl.BlockSpec((4, DIFF_DH), lambda bi, h: (0, 0)),
            pl.BlockSpec((1, dv), lambda bi, h: (0, 0)),
            pl.BlockSpec((1, s, LANES), lambda bi, h: (bi, 0, PB_AQ + h)),
            pl.BlockSpec((1, s, LANES), lambda bi, h: (bi, 0, PB_AK + h)),
            pl.BlockSpec((1, s, LANES), lambda bi, h: (bi, 0, PB_AV + h)),
        ],
        out_specs=pl.BlockSpec((1, s, dv), lambda bi, h: (bi, 0, h)),
        out_shape=jax.ShapeDtypeStruct((b, s, DIFF_HEADS * dv), CDT),
        scratch_shapes=_flash_scratch(2 * tq, tq),
        compiler_params=_cparams(("parallel", "parallel")),
        name="diff_attention",
    )(*sched, jnp.asarray(_alibi_slopes(DIFF_HEADS)), diff_lambda, subln.reshape(1, dv), proj3, proj3, proj3)


def _mla_prep_kernel(cq_ref, ckv_ref, kr_ref, krs_ref, gq_ref, gkv_ref, wqm_ref, wqs_ref, wk_ref, wv_ref,
                     cosq_ref, sinq_ref, cosk_ref, sink_ref, q_ref, k_ref, v_ref):
    hq = _rms(cq_ref[0].astype(F32), gq_ref[...]).astype(CDT)
    qm = _dot(hq, wqm_ref[...])
    qs = _dot(hq, wqs_ref[...])
    cosq, sinq = cosq_ref[...], sinq_ref[...]
    hw = 2 * LANES
    for h in range(MLA_HEADS):
        sl = slice(h * hw, (h + 1) * hw)
        q_ref[0, :, sl] = (qm[:, sl] * cosq + qs[:, sl] * sinq).astype(q_ref.dtype)
    hkv = _rms(ckv_ref[0].astype(F32), gkv_ref[...]).astype(CDT)
    kn = _dot(hkv, wk_ref[...])
    v_ref[0] = _dot(hkv, wv_ref[...]).astype(v_ref.dtype)
    kpe = (kr_ref[0].astype(F32) * cosk_ref[...] + krs_ref[0].astype(F32) * sink_ref[...]).astype(k_ref.dtype)
    for h in range(MLA_HEADS):
        k_ref[0, :, h * hw:h * hw + LANES] = kn[:, h * LANES:(h + 1) * LANES].astype(k_ref.dtype)
        k_ref[0, :, h * hw + LANES:(h + 1) * hw] = kpe


def _mla_prep(proj3, gq, gkv, wqm, wqs, wk, wv, tabs):
    b, s, _ = proj3.shape
    tm = min(TM_PROJ, s)
    hw = 2 * LANES
    cosq, sinq, cosk, sink = tabs
    const = lambda shape: pl.BlockSpec(shape, lambda bi, i: (0,) * len(shape))
    return pl.pallas_call(
        _mla_prep_kernel,
        grid=(b, s // tm),
        in_specs=[
            pl.BlockSpec((1, tm, MLA_Q_LORA), lambda bi, i: (bi, i, PB_BCQ // 2)),
            pl.BlockSpec((1, tm, MLA_KV_LORA), lambda bi, i: (bi, i, PB_BCKV // 2)),
            pl.BlockSpec((1, tm, LANES), lambda bi, i: (bi, i, PB_BKR)),
            pl.BlockSpec((1, tm, LANES), lambda bi, i: (bi, i, PB_BKRS)),
            const((1, MLA_Q_LORA)), const((1, MLA_KV_LORA)),
            const((MLA_Q_LORA, MLA_HEADS * hw)), const((MLA_Q_LORA, MLA_HEADS * hw)),
            const((MLA_KV_LORA, MLA_HEADS * MLA_NOPE)), const((MLA_KV_LORA, MLA_HEADS * MLA_VDIM)),
            pl.BlockSpec((tm, hw), lambda bi, i: (i, 0)), pl.BlockSpec((tm, hw), lambda bi, i: (i, 0)),
            pl.BlockSpec((tm, LANES), lambda bi, i: (i, 0)), pl.BlockSpec((tm, LANES), lambda bi, i: (i, 0)),
        ],
        out_specs=[
            pl.BlockSpec((1, tm, MLA_HEADS * hw), lambda bi, i: (bi, i, 0)),
            pl.BlockSpec((1, tm, MLA_HEADS * hw), lambda bi, i: (bi, i, 0)),
            pl.BlockSpec((1, tm, MLA_HEADS * MLA_VDIM), lambda bi, i: (bi, i, 0)),
        ],
        out_shape=[
            jax.ShapeDtypeStruct((b, s, MLA_HEADS * hw), CDT),
            jax.ShapeDtypeStruct((b, s, MLA_HEADS * hw), CDT),
            jax.ShapeDtypeStruct((b, s, MLA_HEADS * MLA_VDIM), CDT),
        ],
        compiler_params=_cparams(("parallel", "parallel")),
        name="mla_prep",
    )(proj3, proj3, proj3, proj3, gq.reshape(1, -1), gkv.reshape(1, -1), wqm, wqs, wk, wv,
      cosq, sinq, cosk, sink)


def _plain_attn_kernel(qt_ref, kt_ref, lt_ref, q_ref, k_ref, v_ref, o_ref,
                       m_ref, acc_ref, sa_ref, sb_ref, ma_ref, mb_ref, *, tq, n, hp, dk, dv):
    _flash_reset(m_ref, acc_ref)
    heads = [(slice(h * tq, (h + 1) * tq), slice(h * dk, (h + 1) * dk), slice(h * dv, (h + 1) * dv))
             for h in range(hp)]

    def produce(buf, qi, ki, diag):
        q, k = _tile(q_ref, qi, tq), _tile(k_ref, ki, tq)
        for rows, kcols, _ in heads:
            _put_logits(buf, _dot_nt(q[:, kcols], k[:, kcols]), rows)

    def consume(buf, ki, diag):
        v = _tile(v_ref, ki, tq)
        for rows, _, vcols in heads:
            _flash_consume(buf, v[:, vcols], m_ref, acc_ref, tq if diag else None, rows)

    def finish(qi):
        for rows, _, vcols in heads:
            o_ref[0, pl.ds(pl.multiple_of(qi * tq, tq), tq), vcols] = _flash_result(acc_ref[rows]).astype(o_ref.dtype)
        _flash_reset(m_ref, acc_ref)

    _flash_stream(n, (qt_ref, kt_ref, lt_ref), 0, produce, consume, finish, (sa_ref, ma_ref), (sb_ref, mb_ref),
                  mask_at_produce=False)


def _mla_attention(qc, kc, v):
    b, s, _ = qc.shape
    tq = min(TQ_DENSE, s)
    hw = 2 * LANES
    hp = MLA_HP
    n, sched = _causal_schedule(s // tq)
    return pl.pallas_call(
        functools.partial(_plain_attn_kernel, tq=tq, n=n, hp=hp, dk=hw, dv=MLA_VDIM),
        grid=(b, MLA_HEADS // hp),
        in_specs=[
            _SMEM, _SMEM, _SMEM,
            pl.BlockSpec((1, s, hp * hw), lambda bi, h: (bi, 0, h)),
            pl.BlockSpec((1, s, hp * hw), lambda bi, h: (bi, 0, h)),
            pl.BlockSpec((1, s, hp * MLA_VDIM), lambda bi, h: (bi, 0, h)),
        ],
        out_specs=pl.BlockSpec((1, s, hp * MLA_VDIM), lambda bi, h: (bi, 0, h)),
        out_shape=jax.ShapeDtypeStruct((b, s, MLA_HEADS * MLA_VDIM), CDT),
        scratch_shapes=_flash_scratch(hp * tq, tq),
        compiler_params=_cparams(("parallel", "parallel")),
        name="mla_attention",
    )(*sched, qc, kc, v)


def _fox_attn_kernel(qt_ref, kt_ref, lt_ref, c_ref, q_ref, k_ref, v_ref, o_ref,
                     m_ref, acc_ref, sa_ref, sb_ref, ma_ref, mb_ref, *, tq, n, hp):
    _flash_reset(m_ref, acc_ref)
    heads = [(slice(h * tq, (h + 1) * tq), slice(h * FOX_DH, (h + 1) * FOX_DH)) for h in range(hp)]

    def produce(buf, qi, ki, diag):
        q, k = _tile(q_ref, qi, tq), _tile(k_ref, ki, tq)
        for h, (rows, cols) in enumerate(heads):
            cbase = c_ref[0, h, pl.ds(qi, 1), :][:, 0:1]
            s = _dot_nt(q[:, cols], k[:, cols]) + LOG2E * (cbase - c_ref[0, h, pl.ds(ki, 1), :])
            _put_logits(buf, s, rows)

    def consume(buf, ki, diag):
        v = _tile(v_ref, ki, tq)
        for rows, cols in heads:
            _flash_consume(buf, v[:, cols], m_ref, acc_ref, tq if diag else None, rows)

    def finish(qi):
        for rows, cols in heads:
            o_ref[0, pl.ds(pl.multiple_of(qi * tq, tq), tq), cols] = _flash_result(acc_ref[rows]).astype(o_ref.dtype)
        _flash_reset(m_ref, acc_ref)

    _flash_stream(n, (qt_ref, kt_ref, lt_ref), 0, produce, consume, finish, (sa_ref, ma_ref), (sb_ref, mb_ref),
                  mask_at_produce=False)


def _fox_attention(proj3, c4):
    b, s, _ = proj3.shape
    tq = min(TQ_DENSE, s)
    nk = s // tq
    hp = FOX_HP
    w = hp * FOX_DH
    n, sched = _causal_schedule(nk)
    return pl.pallas_call(
        functools.partial(_fox_attn_kernel, tq=tq, n=n, hp=hp),
        grid=(b, FOX_HEADS // hp),
        in_specs=[
            _SMEM, _SMEM, _SMEM,
            pl.BlockSpec((1, hp, nk, tq), lambda bi, h: (bi, h, 0, 0)),
            pl.BlockSpec((1, s, w), lambda bi, h: (bi, 0, PB_CQ // hp + h)),
            pl.BlockSpec((1, s, w), lambda bi, h: (bi, 0, PB_CK // hp + h)),
            pl.BlockSpec((1, s, w), lambda bi, h: (bi, 0, PB_CV // hp + h)),
        ],
        out_specs=pl.BlockSpec((1, s, w), lambda bi, h: (bi, 0, h)),
        out_shape=jax.ShapeDtypeStruct((b, s, FOX_HEADS * FOX_DH), CDT),
        scratch_shapes=_flash_scratch(hp * tq, tq),
        compiler_params=_cparams(("parallel", "parallel")),
        name="fox_attention",
    )(*sched, c4.reshape(b, FOX_HEADS, nk, tq), proj3, proj3, proj3)


def _nsa_compress_kernel(x_ref, w1a_ref, w1b_ref, pea_ref, peb_ref, w2_ref, o_ref):
    x = x_ref[0]
    n = x.shape[0]
    pa = _dot(x, w1a_ref[...])
    pb = _dot(x, w1b_ref[...])
    pe = _dot(pea_ref[...], w1a_ref[...]) + _dot(peb_ref[...], w1b_ref[...])
    hid = pa + pltpu.roll(pb, n - 1, axis=0) + pe[0:1]
    act = 0.5 * hid * (1.0 + jnp.tanh(math.sqrt(2.0 / math.pi) * (hid + 0.044715 * hid * hid * hid)))
    o_ref[0] = _dot(act.astype(CDT), w2_ref[...]).astype(o_ref.dtype)


def _nsa_compress(xc, w1a, w1b, pea, peb, w2):
    b, n, kdim = xc.shape
    hdim = w1a.shape[1]
    const = lambda shape: pl.BlockSpec(shape, lambda bi: (0,) * len(shape))
    return pl.pallas_call(
        _nsa_compress_kernel,
        grid=(b,),
        in_specs=[pl.BlockSpec((1, n, kdim), lambda bi: (bi, 0, 0)),
                  const((kdim, hdim)), const((kdim, hdim)), const((8, kdim)), const((8, kdim)),
                  const((hdim, w2.shape[1]))],
        out_specs=pl.BlockSpec((1, n, w2.shape[1]), lambda bi: (bi, 0, 0)),
        out_shape=jax.ShapeDtypeStruct((b, n, w2.shape[1]), CDT),
        compiler_params=_cparams(("parallel",)),
        name="nsa_compress",
    )(xc, w1a, w1b, pea, peb, w2)


def _nsa_cmp_kernel(slopes_ref, q_ref, kv_ref, oc_ref, sb_ref, used_ref, *, tq, n_topk):
    qi = pl.program_id(1)
    nblk = kv_ref.shape[1]
    q0 = qi * tq
    rowpos = q0 + lax.broadcasted_iota(jnp.int32, (tq, 1), 0)
    cmp_end = lax.broadcasted_iota(jnp.int32, (1, nblk), 1) * CMP_STRIDE + (CMP_LEN - 1)
    negmask = jnp.where(rowpos >= cmp_end, 0.0, NEG)
    end_rel = (cmp_end - q0).astype(F32)
    lane = lax.broadcasted_iota(jnp.int32, (tq, LANES), 1)
    low = lane < NSA_DH
    nn = lax.broadcasted_iota(jnp.int32, (NSA_DH, nblk), 1) * CMP_STRIDE
    jj = lax.broadcasted_iota(jnp.int32, (NSA_DH, nblk), 0) * SLC_LEN
    ovt = (jnp.maximum(jnp.minimum(nn + CMP_LEN, jj + SLC_LEN) - jnp.maximum(nn, jj), 0).astype(F32)
           * (1.0 / CMP_LEN)).astype(CDT)
    jt = lax.broadcasted_iota(jnp.int32, (NSA_DH, tq), 0).astype(F32)
    blk = ((q0 + lax.broadcasted_iota(jnp.int32, (1, tq), 1)) >> SLC_SHIFT).astype(F32)
    fixed = (jt == 0.0) | (jt == blk) | (jt == blk - 1.0)
    beyond = jt > blk
    row_ok = rowpos >= CMP_LEN - 1
    outs = []
    bias = []
    for g in range(NSA_GROUPS):
        kc = kv_ref[0, :, g * LANES:(g + 1) * LANES]
        vc = kv_ref[0, :, (NSA_GROUPS + g) * LANES:(NSA_GROUPS + g + 1) * LANES]
        psum = jnp.zeros((tq, nblk), F32)
        mine = low if g == 0 else jnp.logical_not(low)
        zero = jnp.zeros((tq, LANES), q_ref.dtype)
        qs = jnp.concatenate([jnp.where(mine, q_ref[0, :, j * LANES:(j + 1) * LANES], zero)
                              for j in range(NSA_HPG)], axis=0)
        s_all = _dot_nt(qs, kc)
        ps = []
        for j in range(NSA_HPG):
            s = s_all[j * tq:(j + 1) * tq] + slopes_ref[g * NSA_HPG + j] * end_rel + negmask
            e = jnp.exp2(s - jnp.max(s, axis=-1, keepdims=True))
            den = jnp.sum(e, axis=-1, keepdims=True)
            p = e * jnp.where(row_ok, 1.0 / den, 0.0)
            psum = psum + p
            ps.append(p.astype(CDT))
        o_all = _dot(jnp.concatenate(ps, axis=0), vc)
        outs.extend(o_all[j * tq:(j + 1) * tq] for j in range(NSA_HPG))
        hi = psum.astype(CDT)
        lo = (psum - hi.astype(F32)).astype(CDT)
        imp = _dot_nt(ovt, hi) + _dot_nt(ovt, lo)
        imp = jnp.where(fixed, -jnp.inf, jnp.where(beyond, NEG_INF, imp))
        sbt = jnp.where(fixed, 0.0, NEG)
        for _ in range(n_topk - 3):
            mx = jnp.max(imp, axis=0, keepdims=True)
            idx = jnp.min(jnp.where(imp == mx, jt, float(LANES)), axis=0, keepdims=True)
            hit = jt == idx
            sbt = jnp.where(hit, 0.0, sbt)
            imp = jnp.where(hit, -jnp.inf, imp)
        bias.append(sbt)
    sb = jnp.concatenate([bias[1], bias[0]], axis=0).T
    sb_ref[0] = sb.astype(sb_ref.dtype)
    used = jnp.max(jnp.where(sb == 0.0, 1.0, 0.0), axis=0, keepdims=True)
    used_ref[0, 0] = jnp.broadcast_to(used, used_ref.shape[2:])
    for blk_i in range(NSA_HEADS // 2):
        oc_ref[0, :, blk_i * LANES:(blk_i + 1) * LANES] = jnp.where(
            low, outs[2 * blk_i], outs[2 * blk_i + 1]).astype(oc_ref.dtype)


def _nsa_cmp_select(proj3, kvc, n_topk):
    assert n_topk >= 3, "the three always-selected blocks must fit in the top-k budget"
    b, s, _ = proj3.shape
    tq = min(TQ_NSA, s)
    nblk = kvc.shape[1]
    return pl.pallas_call(
        functools.partial(_nsa_cmp_kernel, tq=tq, n_topk=n_topk),
        grid=(b, s // tq),
        in_specs=[
            pl.BlockSpec(memory_space=pltpu.SMEM),
            pl.BlockSpec((1, tq, 4 * LANES), lambda bi, qi: (bi, qi, PB_DQ // 4)),
            pl.BlockSpec((1, nblk, kvc.shape[2]), lambda bi, qi: (bi, 0, 0)),
        ],
        out_specs=[
            pl.BlockSpec((1, tq, NSA_HEADS * NSA_DH), lambda bi, qi: (bi, qi, 0)),
            pl.BlockSpec((1, tq, LANES), lambda bi, qi: (bi, qi, 0)),
            pl.BlockSpec((1, 1, 8, LANES), lambda bi, qi: (bi, qi, 0, 0)),
        ],
        out_shape=[jax.ShapeDtypeStruct((b, s, NSA_HEADS * NSA_DH), CDT),
                   jax.ShapeDtypeStruct((b, s, LANES), CDT),
                   jax.ShapeDtypeStruct((b, s // tq, 8, LANES), F32)],
        compiler_params=_cparams(("parallel", "parallel")),
        name="nsa_cmp_select",
    )(jnp.asarray(_alibi_slopes(NSA_HEADS)), proj3, kvc)


def _compact_heads(heads, mine, low):
    both = [jnp.where(mine, a, pltpu.roll(a, NSA_DH, axis=1)) for a in heads]
    out = [jnp.where(low, both[2 * jj], both[2 * jj + 1]) for jj in range(NSA_HPG // 2)]
    return jnp.concatenate(out, axis=1)


def _nsa_win_kernel(slopes_ref, q_ref, kp_ref, kc_ref, vp_ref, vc_ref, o_ref, *, tq):
    qi = pl.program_id(1)
    lane = lax.broadcasted_iota(jnp.int32, (tq, LANES), 1)
    low = lane < NSA_DH
    r = lax.broadcasted_iota(jnp.int32, (tq, tq), 0)
    c = lax.broadcasted_iota(jnp.int32, (tq, tq), 1)
    own = c <= r
    ndist = jnp.where(own, c - r, c - r - tq).astype(F32)
    own_f = jnp.where(own, 1.0, 0.0).astype(CDT)
    prev_pen = jnp.where(qi > 0, 0.0, NEG)
    q = q_ref[0]
    zero = jnp.zeros((tq, LANES), q.dtype)
    mine = (low, jnp.logical_not(low))
    qs = jnp.concatenate([jnp.where(mine[g], q[:, j * LANES:(j + 1) * LANES], zero)
                          for g in range(NSA_GROUPS) for j in range(NSA_HPG)], axis=0)
    s_own, s_prev = _dot_nt(qs, kc_ref[0]), _dot_nt(qs, kp_ref[0])
    ps = []
    for hd in range(NSA_HEADS):
        rows = slice(hd * tq, (hd + 1) * tq)
        s = jnp.where(own, s_own[rows], s_prev[rows] + prev_pen) + slopes_ref[hd] * ndist
        ps.append(jnp.exp2(s - jnp.max(s, axis=-1, keepdims=True)).astype(CDT))
    p = jnp.concatenate(ps, axis=0)
    p_own = p * jnp.tile(own_f, (NSA_HEADS, 1))
    o = _flash_result(_dot(p_own, _with_ones(vc_ref[0])) + _dot(p - p_own, _with_ones(vp_ref[0])))
    for g in range(NSA_GROUPS):
        heads = [o[(g * NSA_HPG + j) * tq:(g * NSA_HPG + j + 1) * tq] for j in range(NSA_HPG)]
        w = NSA_HPG * NSA_DH
        o_ref[0, :, g * w:(g + 1) * w] = _compact_heads(heads, mine[g], low).astype(o_ref.dtype)


def _nsa_window(proj3):
    b, s, _ = proj3.shape
    tq = WINDOW
    return pl.pallas_call(
        functools.partial(_nsa_win_kernel, tq=tq),
        grid=(b, s // tq),
        in_specs=[
            pl.BlockSpec(memory_space=pltpu.SMEM),
            pl.BlockSpec((1, tq, 4 * LANES), lambda bi, qi: (bi, qi, PB_DQ // 4)),
            pl.BlockSpec((1, tq, LANES), lambda bi, qi: (bi, jnp.maximum(qi - 1, 0), PB_WIN_K)),
            pl.BlockSpec((1, tq, LANES), lambda bi, qi: (bi, qi, PB_WIN_K)),
            pl.BlockSpec((1, tq, LANES), lambda bi, qi: (bi, jnp.maximum(qi - 1, 0), PB_WIN_V)),
            pl.BlockSpec((1, tq, LANES), lambda bi, qi: (bi, qi, PB_WIN_V)),
        ],
        out_specs=pl.BlockSpec((1, tq, NSA_HEADS * NSA_DH), lambda bi, qi: (bi, qi, 0)),
        out_shape=jax.ShapeDtypeStruct((b, s, NSA_HEADS * NSA_DH), CDT),
        compiler_params=_cparams(("parallel", "parallel")),
        name="nsa_window",
    )(jnp.asarray(_alibi_slopes(NSA_HEADS)), proj3, proj3, proj3, proj3, proj3)


def _nsa_sel_kernel(cnt_ref, qt_ref, kt_ref, lt_ref, slopes_ref, q_ref, sb_ref, k_ref, v_ref, oc_ref, ow_ref, gl_ref,
                    e_ref, o_ref, m_ref, acc_ref, sa_ref, sb2_ref, ma_ref, mb_ref, cm_ref, *, tq, rows_per_problem):
    g = pl.program_id(1)
    w = NSA_HPG * NSA_DH
    lane = lax.broadcasted_iota(jnp.int32, (tq, LANES), 1)
    low = lane < NSA_DH
    mine = (lane >> HALF_SHIFT) == g
    _flash_begin(m_ref, acc_ref, cm_ref, tq)
    col = lax.broadcasted_iota(jnp.int32, (1, tq), 1).astype(F32)
    jl = lane & (NSA_DH - 1)
    krow = lax.broadcasted_iota(jnp.int32, (tq, LANES), 0)

    def produce(buf, qi, ki, diag):
        q = _tile(q_ref, qi, tq)
        sb = _tile(sb_ref, qi, tq)
        qa = jnp.concatenate([jnp.where(mine, q[:, j * LANES:(j + 1) * LANES], sb) for j in range(NSA_HPG)], axis=0)
        k = _tile(k_ref, ki, tq)
        onehot = jnp.where(((ki * tq + krow) >> SLC_SHIFT) == jl, 1.0, 0.0).astype(k.dtype)
        s_all = _dot_nt(qa, jnp.where(mine, k, onehot))
        rel = ((ki - qi) * tq).astype(F32)
        for j in range(NSA_HPG):
            rows = slice(j * tq, (j + 1) * tq)
            _put_logits(buf, s_all[rows] + slopes_ref[g * NSA_HPG + j] * (col + rel), rows, diag, cm_ref)

    def consume(buf, ki, diag):
        _flash_consume(buf, _tile(v_ref, ki, tq), m_ref, acc_ref, tq if diag else None)

    def finish(qi):
        o = _flash_result(acc_ref[...])
        o_s = _compact_heads([o[j * tq:(j + 1) * tq] for j in range(NSA_HPG)], mine, low)
        gates = _split_dot(_sigmoid(_tile(gl_ref, qi, tq)), e_ref[0])
        y = (gates[:, 0:w] * _tile(oc_ref, qi, tq).astype(F32) + gates[:, w:2 * w] * o_s
             + gates[:, 2 * w:3 * w] * _tile(ow_ref, qi, tq).astype(F32))
        o_ref[0, pl.ds(pl.multiple_of(qi * tq, tq), tq), :] = y.astype(o_ref.dtype)
        _flash_reset(m_ref, acc_ref)

    prob = pl.program_id(0) * NSA_GROUPS + g
    _flash_stream(cnt_ref[prob], (qt_ref, kt_ref, lt_ref), prob * rows_per_problem, produce, consume, finish,
                  (sa_ref, ma_ref), (sb2_ref, mb_ref), mask_at_produce=True)


def _nsa_selected(proj3, sbias, used, o_c, o_w, small3, expand):
    b, s, _ = proj3.shape
    tq = min(TQ_NSA, s)
    nq = s // tq
    w = NSA_HPG * NSA_DH
    u = used[:, :, 0, :].reshape(b, nq, NSA_GROUPS, NSA_DH)[:, :, ::-1, :nq * (tq // SLC_LEN)]
    flags = (u.reshape(b, nq, NSA_GROUPS, nq, tq // SLC_LEN).max(axis=-1) > 0.0).astype(jnp.int32)
    flags = flags.transpose(0, 2, 1, 3)
    qt = jnp.arange(nq, dtype=jnp.int32)
    need = jnp.where(qt[None, :] < qt[:, None], flags, (qt[None, :] == qt[:, None]).astype(jnp.int32))
    need = need.reshape(b, NSA_GROUPS, nq * nq)
    cnt = need.sum(axis=-1).astype(jnp.int32)
    order = jnp.argsort(1 - need, axis=-1, stable=True).astype(jnp.int32)
    order = jnp.pad(order, ((0, 0), (0, 0), (0, 2)))
    rows = nq * nq + 2
    sched = (order // nq, order % nq, (order // nq == order % nq).astype(jnp.int32))
    return pl.pallas_call(
        functools.partial(_nsa_sel_kernel, tq=tq, rows_per_problem=rows),
        grid=(b, NSA_GROUPS),
        in_specs=[
            _SMEM, _SMEM, _SMEM, _SMEM, _SMEM,
            pl.BlockSpec((1, s, 4 * LANES), lambda bi, g: (bi, 0, PB_DQ // 4)),
            pl.BlockSpec((1, s, LANES), lambda bi, g: (bi, 0, 0)),
            pl.BlockSpec((1, s, LANES), lambda bi, g: (bi, 0, PB_SEL_K)),
            pl.BlockSpec((1, s, LANES), lambda bi, g: (bi, 0, PB_SEL_V)),
            pl.BlockSpec((1, s, w), lambda bi, g: (bi, 0, g)),
            pl.BlockSpec((1, s, w), lambda bi, g: (bi, 0, g)),
            pl.BlockSpec((1, s, LANES), lambda bi, g: (bi, 0, 0)),
            pl.BlockSpec((1, LANES, 3 * w), lambda bi, g: (g, 0, 0)),
        ],
        out_specs=pl.BlockSpec((1, s, w), lambda bi, g: (bi, 0, g)),
        out_shape=jax.ShapeDtypeStruct((b, s, NSA_HEADS * NSA_DH), CDT),
        scratch_shapes=_flash_scratch(NSA_HPG * tq, tq, mask_scratch=True),
        compiler_params=_cparams(("parallel", "parallel")),
        name="nsa_selected",
    )(cnt.reshape(-1), *[t.reshape(-1) for t in sched], jnp.asarray(_alibi_slopes(NSA_HEADS)),
      proj3, sbias, proj3, proj3, o_c, o_w, small3, expand)


def _merge_kernel(ya_ref, yb_ref, yc_ref, yd_ref, ga_ref, gb_ref, gc_ref, gd_ref, wb_ref, wo_ref, x_ref, o_ref):
    merged = None
    for n, (y_ref, g_ref) in enumerate(((ya_ref, ga_ref), (yb_ref, gb_ref), (yc_ref, gc_ref), (yd_ref, gd_ref))):
        t = _sigmoid(g_ref[...].astype(F32)) * _dot(y_ref[...], wb_ref[n])
        merged = t if merged is None else merged + t
    o_ref[...] = x_ref[...] + _dot(merged.astype(CDT), wo_ref[...])


def _merge(ys, proj2, wb, wo, x2, layer):
    t, d = x2.shape
    tm = min(TM_ROWS, t)
    gate_blk = PB_GATE * LANES // d
    yspec = pl.BlockSpec((tm, BRANCH_WIDTH), lambda i: (i, 0))
    gspecs = [pl.BlockSpec((tm, d), functools.partial(lambda i, n: (i, gate_blk + n), n=n)) for n in range(N_BRANCH)]
    return pl.pallas_call(
        _merge_kernel,
        grid=(t // tm,),
        in_specs=[yspec] * N_BRANCH + gspecs + [
            pl.BlockSpec((None, N_BRANCH, BRANCH_WIDTH, d), lambda i: (layer, 0, 0, 0)),
            pl.BlockSpec((None, d, d), lambda i: (layer, 0, 0)),
            pl.BlockSpec((tm, d), lambda i: (i, 0)),
        ],
        out_specs=pl.BlockSpec((tm, d), lambda i: (i, 0)),
        out_shape=jax.ShapeDtypeStruct((t, d), F32),
        compiler_params=_cparams(("parallel",)),
        name="merge",
    )(*ys, proj2, proj2, proj2, proj2, wb, wo, x2)


HALO = 16


def _ffn_kernel(x_ref, xh_ref, g_ref, wu_ref, cw_ref, cb_ref, wd_ref, gf_ref, o_ref, he_ref, u_ref, act_ref,
                *, tm, fc, final):
    i = pl.program_id(1)
    x = x_ref[0]
    g = g_ref[...]
    xh = xh_ref[0] * (i > 0).astype(F32)
    he_ref[0:HALO] = _rms(xh, g).astype(CDT)
    he_ref[HALO:HALO + tm] = _rms(x, g).astype(CDT)
    he = he_ref[...]
    for c in range(D_FF // fc):
        outs = []
        for half in range(2):
            ub = u_ref.at[c % 2, half]
            lo = half * D_FF + c * fc
            ub[...] = _dot(he, wu_ref[:, lo:lo + fc])
            conv = cb_ref[:, lo:lo + fc]
            for kk in range(CONV_WIDTH):
                off = HALO - (CONV_WIDTH - 1) + kk
                conv = conv + cw_ref[kk:kk + 1, lo:lo + fc] * ub[off:off + tm, :]
            outs.append(conv)
        a, gg = outs
        act_ref[:, c * fc:(c + 1) * fc] = (a * _sigmoid(a) * gg).astype(CDT)
    y = x + _dot(act_ref[...], wd_ref[...])
    if final:
        y = _rms(y, gf_ref[...])
    o_ref[0] = y


def _ffn(x3, g, wu, cw, cb, wd, gf, layer, final):
    b, s, d = x3.shape
    tm = min(TM_ROWS, s)
    fc = FFN_CHUNK
    assert D_FF % fc == 0
    const = lambda shape: pl.BlockSpec(shape, lambda bi, i: (0,) * len(shape), pipeline_mode=pl.Buffered(1))
    stacked = lambda shape: pl.BlockSpec((None,) + shape, lambda bi, i: (layer,) + (0,) * len(shape),
                                         pipeline_mode=pl.Buffered(1))
    return pl.pallas_call(
        functools.partial(_ffn_kernel, tm=tm, fc=fc, final=final),
        grid=(b, s // tm),
        in_specs=[
            pl.BlockSpec((1, tm, d), lambda bi, i: (bi, i, 0)),
            pl.BlockSpec((1, HALO, d), lambda bi, i: (bi, jnp.maximum(i * (tm // HALO) - 1, 0), 0)),
            const((1, d)), stacked((d, 2 * D_FF)), const((CONV_WIDTH, 2 * D_FF)), const((1, 2 * D_FF)),
            stacked((D_FF, d)), const((1, d)),
        ],
        out_specs=pl.BlockSpec((1, tm, d), lambda bi, i: (bi, i, 0)),
        out_shape=jax.ShapeDtypeStruct((b, s, d), F32),
        scratch_shapes=[pltpu.VMEM((tm + HALO, d), CDT), pltpu.VMEM((2, 2, tm + HALO, fc), F32),
                        pltpu.VMEM((tm, D_FF), CDT)],
        compiler_params=_cparams(("parallel", "arbitrary")),
        name="conv_glu_mlp",
    )(x3, x3, g.reshape(1, d), wu, cw, cb.reshape(1, -1), wd, gf.reshape(1, d))


def _w_in_plan():
    widths = (512, 512, 512, MLA_Q_LORA, MLA_KV_LORA, MLA_ROPE, 512, 512, 512, FOX_HEADS,
              512, 768, 3 * NSA_HEADS, N_BRANCH * D_MODEL)
    (a_q, a_k, a_v, b_cq, b_ckv, b_kr, c_q, c_k, c_v, c_f, d_q, d_kv, d_g, gate, _) = np.cumsum((0,) + widths).tolist()
    half = MLA_ROPE // 2

    def run(src, nblocks, scale=1.0):
        return [[(src + i * LANES, LANES, scale)] for i in range(nblocks)]

    blocks = (run(a_q, 4, LOG2E * DIFF_DH ** -0.5) + run(a_k, 4) + run(a_v, 4)
              + run(c_q, 4, LOG2E * FOX_DH ** -0.5) + run(c_k, 4) + run(c_v, 4))
    sd = LOG2E * NSA_DH ** -0.5
    blocks += [[(d_q + j * NSA_DH, NSA_DH, sd), (d_q + (NSA_HPG + j) * NSA_DH, NSA_DH, sd)] for j in range(NSA_HPG)]
    blocks += run(d_kv, 6) + run(b_cq, 2) + run(b_ckv, 2)
    blocks += [[(b_kr, MLA_ROPE, 1.0), None],
               [(b_kr + half, half, -1.0), (b_kr, half, 1.0), None]]
    blocks += run(gate, N_BRANCH * D_MODEL // LANES)
    assert len(blocks) * LANES == N_PROJ
    small = [(c_f, FOX_HEADS, 1.0), (d_g, 3 * NSA_HEADS, 1.0), None]
    return blocks, small


def _w_in_relayout_kernel(w_ref, big_ref, small_ref):
    cols = w_ref.shape[1]
    blocks, small = _w_in_plan()
    for j, pieces in enumerate(blocks):
        row = j * LANES
        for p in pieces:
            if p is None:
                big_ref[row:(j + 1) * LANES, :] = jnp.zeros(((j + 1) * LANES - row, cols), big_ref.dtype)
            else:
                src, n, scale = p
                v = w_ref[src:src + n, :]
                big_ref[row:row + n, :] = (v if scale == 1.0 else v * scale).astype(big_ref.dtype)
                row += n
    (sf, nf, _), (sg, ng, _), _ = small
    r = lax.broadcasted_iota(jnp.int32, (LANES, cols), 0)
    side = jnp.where(r < nf, w_ref[sf:sf + LANES, :],
                     jnp.where(r < nf + ng, w_ref[sg - nf:sg - nf + LANES, :], 0.0))
    small_ref[...] = side.astype(small_ref.dtype)


def _w_in_relayout(w):
    nl, d, n = w.shape
    wt = jnp.transpose(w, (2, 0, 1)).reshape(n, nl * d)
    tc = min(256, d)
    return pl.pallas_call(
        _w_in_relayout_kernel,
        grid=(nl, d // tc),
        in_specs=[pl.BlockSpec((n, tc), lambda l, i: (0, l * (d // tc) + i))],
        out_specs=[pl.BlockSpec((None, N_PROJ, tc), lambda l, i: (l, 0, i)),
                   pl.BlockSpec((None, LANES, tc), lambda l, i: (l, 0, i))],
        out_shape=[jax.ShapeDtypeStruct((nl, N_PROJ, d), CDT), jax.ShapeDtypeStruct((nl, LANES, d), CDT)],
        compiler_params=_cparams(("parallel", "parallel")),
        name="w_in_relayout",
    )(wt)


def _prep_mla(w_uq, w_ukv):
    r = w_uq.shape[0]
    hw = 2 * LANES
    half = MLA_ROPE // 2
    scale = LOG2E * (MLA_NOPE + MLA_ROPE) ** -0.5
    wq = (w_uq * scale).reshape(r, MLA_HEADS, MLA_NOPE + MLA_ROPE)
    nope, t1, t2 = wq[..., :MLA_NOPE], wq[..., MLA_NOPE:MLA_NOPE + half], wq[..., MLA_NOPE + half:]
    zpad = jnp.zeros((r, MLA_HEADS, hw - MLA_NOPE - MLA_ROPE), w_uq.dtype)
    wqm = jnp.concatenate([nope, t1, t2, zpad], axis=-1).reshape(r, MLA_HEADS * hw)
    wqs = jnp.concatenate([jnp.zeros_like(nope), -t2, t1, zpad], axis=-1).reshape(r, MLA_HEADS * hw)
    wkv = w_ukv.reshape(w_ukv.shape[0], MLA_HEADS, MLA_NOPE + MLA_VDIM)
    wk = wkv[..., :MLA_NOPE].reshape(-1, MLA_HEADS * MLA_NOPE)
    wv = wkv[..., MLA_NOPE:].reshape(-1, MLA_HEADS * MLA_VDIM)
    return wqm.astype(CDT), wqs.astype(CDT), wk.astype(CDT), wv.astype(CDT)


def _rope_tables(s):
    half = MLA_ROPE // 2
    inv_freq = ROPE_THETA ** (-jnp.arange(0, MLA_ROPE, 2, dtype=F32) / MLA_ROPE)
    ang = jnp.arange(s, dtype=F32)[:, None] * inv_freq[None, :]
    cos, sin = jnp.cos(ang), jnp.sin(ang)
    z = jnp.zeros((s, LANES - MLA_ROPE), F32)
    cosk = jnp.concatenate([cos, cos, z], axis=1)
    sink = jnp.concatenate([sin, sin, z], axis=1)
    cosq = jnp.concatenate([jnp.ones((s, MLA_NOPE), F32), cosk], axis=1)
    sinq = jnp.concatenate([jnp.zeros((s, MLA_NOPE), F32), sink], axis=1)
    return cosq, sinq, cosk, sink


def _prep_compress(pe, w1, w2):
    eye2 = jnp.eye(2, dtype=F32)
    w1r = w1.reshape(2, CMP_LEN, NSA_DH, CMP_HIDDEN).astype(CDT)
    same = np.eye(2, dtype=bool)
    diag_kg = jnp.asarray(same[:, None, :, None] & same[None, :, None, :])

    def expand(wpart):
        src = wpart.transpose(1, 0, 2, 3)[:, :, None, :, None, None, :]
        t = jnp.where(diag_kg[None, :, :, None, :, :, None], src, jnp.zeros((), CDT))
        return t.reshape(CMP_STRIDE * 4 * NSA_DH, 4 * CMP_HIDDEN)

    w1a, w1b = expand(w1r[:, :CMP_STRIDE]), expand(w1r[:, CMP_STRIDE:])

    def pe_row(p):
        t = jnp.broadcast_to(p.transpose(1, 0, 2)[:, :, None, :], (CMP_STRIDE, 2, NSA_GROUPS, NSA_DH))
        return jnp.pad(t.reshape(1, -1), ((0, 7), (0, 0)))

    pea, peb = pe_row(pe[:, :CMP_STRIDE]), pe_row(pe[:, CMP_STRIDE:])
    w2b = jnp.einsum('khd,kK,gG,u->kghKGud', w2, eye2, eye2, jnp.ones((2,), F32))
    w2b = w2b.reshape(4 * CMP_HIDDEN, 4 * 2 * NSA_DH)
    return w1a.astype(CDT), w1b.astype(CDT), pea.astype(CDT), peb.astype(CDT), w2b.astype(CDT)


def _gate_expand():
    e = np.zeros((NSA_GROUPS, LANES, 3, NSA_HPG, NSA_DH), np.float32)
    for g in range(NSA_GROUPS):
        for j in range(NSA_HPG):
            for br in range(3):
                e[g, SMALL_G + (g * NSA_HPG + j) * 3 + br, br, j, :] = 1.0
    return jnp.asarray(e.reshape(NSA_GROUPS, LANES, 3 * NSA_HPG * NSA_DH)).astype(CDT)


def _token_mixers(x3, l, norm_mix, w_in, diff_lambda, diff_subln, mla_norm_q, mla_w_uq, mla_norm_kv, mla_w_ukv,
                  fox_b_f, nsa_cmp_pe, nsa_cmp_w1, nsa_cmp_w2, w_branch, w_out, rope_tabs):
    b, s, d = x3.shape
    t = b * s
    x2 = x3.reshape(t, d)
    proj, small = _in_proj(x2, norm_mix, *w_in, l)
    proj3 = proj.reshape(b, s, N_PROJ)
    small3 = small.reshape(b, s, LANES)

    lam_init = 0.8 - 0.6 * math.exp(-0.3 * l)
    y_a = _diff_attention(proj3, diff_lambda, diff_subln, lam_init)

    wqm, wqs, wk, wv = _prep_mla(mla_w_uq, mla_w_ukv)
    qc, kc, vv = _mla_prep(proj3, mla_norm_q, mla_norm_kv, wqm, wqs, wk, wv, rope_tabs)
    y_b = _mla_attention(qc, kc, vv)

    cf_rows = small3[:, :, SMALL_F:SMALL_F + FOX_HEADS].transpose(0, 2, 1).reshape(b * FOX_HEADS, s)
    bias_rows = jnp.tile(fox_b_f.astype(F32), b).reshape(b * FOX_HEADS, 1)
    c4 = _fox_cumsum(cf_rows, bias_rows)
    y_c = _fox_attention(proj3, c4)

    w1a, w1b, pea, peb, w2b = _prep_compress(nsa_cmp_pe, nsa_cmp_w1, nsa_cmp_w2)
    xc = proj3[:, :, PB_CMP_K * LANES:(PB_CMP_V + 1) * LANES].reshape(b, s // CMP_STRIDE, CMP_STRIDE * 2 * LANES)
    kvc = _nsa_compress(xc, w1a, w1b, pea, peb, w2b)
    n_topk = min(SLC_TOPK, s // SLC_LEN)
    o_c, sbias, used = _nsa_cmp_select(proj3, kvc, n_topk)
    o_w = _nsa_window(proj3)
    y_d = _nsa_selected(proj3, sbias, used, o_c, o_w, small3, _gate_expand())

    ys = [y.reshape(t, BRANCH_WIDTH) for y in (y_a, y_b, y_c, y_d)]
    return _merge(ys, proj, w_branch, w_out, x2, l).reshape(b, s, d)


def kernel(x, norm_mix, w_in, diff_lambda, diff_subln, mla_norm_q, mla_w_uq, mla_norm_kv, mla_w_ukv, fox_b_f,
           nsa_cmp_pe, nsa_cmp_w1, nsa_cmp_w2, w_branch, w_out, norm_ffn, w_up, conv_w, conv_b, w_down, norm_final):
    depth = w_in.shape[0]
    s = x.shape[1]
    rope_tabs = _rope_tables(s)
    w_in = _w_in_relayout(w_in)
    w_up16, w_down16 = w_up.astype(CDT), w_down.astype(CDT)
    w_branch16, w_out16 = w_branch.astype(CDT), w_out.astype(CDT)
    for l in range(depth):
        x = _token_mixers(x, l, norm_mix[l], w_in, diff_lambda[l], diff_subln[l], mla_norm_q[l], mla_w_uq[l],
                          mla_norm_kv[l], mla_w_ukv[l], fox_b_f[l], nsa_cmp_pe[l], nsa_cmp_w1[l], nsa_cmp_w2[l],
                          w_branch16, w_out16, rope_tabs)
        x = _ffn(x, norm_ffn[l], w_up16, conv_w[l], conv_b[l], w_down16, norm_final, l, final=(l == depth - 1))
    return x
```

```python
import functools
import math

import numpy as np
import jax
import jax.numpy as jnp
from jax import lax
from jax.experimental import pallas as pl
from jax.experimental.pallas import tpu as pltpu

F32 = jnp.float32
CDT = jnp.bfloat16

NEG = -1e30
NEG_INF = -1e30
NORM_EPS = 1e-6
LOG2E = 1.4426950408889634
LANES = 128

D_MODEL = 1024
DIFF_HEADS, DIFF_DH = 4, 64
MLA_HEADS, MLA_NOPE, MLA_ROPE, MLA_VDIM = 4, 128, 64, 128
MLA_Q_LORA, MLA_KV_LORA = 256, 256
ROPE_THETA = 10000.0
FOX_HEADS, FOX_DH = 4, 128
NSA_HEADS, NSA_GROUPS, NSA_DH = 8, 2, 64
NSA_HPG = NSA_HEADS // NSA_GROUPS
CMP_STRIDE = 16
CMP_LEN = 2 * CMP_STRIDE
CMP_HIDDEN = 128
SLC_LEN = 64
SLC_SHIFT = 6
HALF_SHIFT = 6
SLC_TOPK = 8
WINDOW = 256
N_BRANCH = 4
BRANCH_WIDTH = 512
D_FF = 2816
CONV_WIDTH = 3

PB_AQ, PB_AK, PB_AV = 0, 4, 8
PB_CQ, PB_CK, PB_CV = 12, 16, 20
PB_DQ = 24
PB_CMP_K, PB_CMP_V, PB_SEL_K, PB_SEL_V, PB_WIN_K, PB_WIN_V = 28, 29, 30, 31, 32, 33
PB_BCQ, PB_BCKV, PB_BKR, PB_BKRS = 34, 36, 38, 39
PB_GATE = 40
N_PROJ = 72 * LANES
SMALL_F, SMALL_G = 0, 4

VMEM_LIMIT = 56 * 1024 * 1024
MXU_TILE = 256
TQ_DENSE = 512
TQ_NSA = WINDOW
TM_PROJ, TN_PROJ = 1024, 9 * MXU_TILE
TM_ROWS = 512
FFN_CHUNK = MXU_TILE
FOX_HP = 2
MLA_HP = 2


def _cparams(sem):
    return pltpu.CompilerParams(dimension_semantics=sem, vmem_limit_bytes=VMEM_LIMIT)


def _rms(xf, g):
    return xf * lax.rsqrt(jnp.mean(xf * xf, axis=-1, keepdims=True) + NORM_EPS) * g


def _sigmoid(x):
    return 0.5 * jnp.tanh(0.5 * x) + 0.5


def _dot(a, b):
    return jnp.dot(a, b, preferred_element_type=F32)


def _dot_nt(a, b):
    return lax.dot_general(a, b, (((1,), (1,)), ((), ())), preferred_element_type=F32)


def _split_dot(a, b):
    hi = a.astype(CDT)
    lo = (a - hi.astype(F32)).astype(CDT)
    return _dot(hi, b) + _dot(lo, b)


def _alibi_slopes(n):
    return (LOG2E * np.exp2(-8.0 * np.arange(1, n + 1) / n)).astype(np.float32)


def _inproj_kernel(x_ref, g_ref, w_ref, ws_ref, o_ref, os_ref, h_ref):
    @pl.when(pl.program_id(1) == 0)
    def _():
        h = _rms(x_ref[...], g_ref[...]).astype(CDT)
        h_ref[...] = h
        os_ref[...] = _dot_nt(h, ws_ref[...])

    o_ref[...] = _dot_nt(h_ref[...], w_ref[...]).astype(o_ref.dtype)


def _in_proj(x2, g, w, ws, layer):
    t, d = x2.shape
    n = w.shape[1]
    tm = min(TM_PROJ, t)
    tn = TN_PROJ
    assert n % tn == 0
    return pl.pallas_call(
        _inproj_kernel,
        grid=(t // tm, n // tn),
        in_specs=[
            pl.BlockSpec((tm, d), lambda i, j: (i, 0)),
            pl.BlockSpec((1, d), lambda i, j: (0, 0)),
            pl.BlockSpec((None, tn, d), lambda i, j: (layer, j, 0)),
            pl.BlockSpec((None, LANES, d), lambda i, j: (layer, 0, 0)),
        ],
        out_specs=[
            pl.BlockSpec((tm, tn), lambda i, j: (i, j)),
            pl.BlockSpec((tm, LANES), lambda i, j: (i, 0)),
        ],
        out_shape=[jax.ShapeDtypeStruct((t, n), CDT), jax.ShapeDtypeStruct((t, LANES), F32)],
        scratch_shapes=[pltpu.VMEM((tm, d), CDT)],
        compiler_params=_cparams(("parallel", "arbitrary")),
        name="in_proj",
    )(x2, g.reshape(1, d), w, ws)


def _fox_cumsum_kernel(cf_ref, bf_ref, o_ref):
    rows, s = cf_ref.shape
    lane = lax.broadcasted_iota(jnp.int32, (rows, LANES), 1)
    carry = jnp.zeros((rows, 1), F32)
    for c in range(s // LANES):
        z = cf_ref[:, c * LANES:(c + 1) * LANES] + bf_ref[...]
        xs = jnp.minimum(z, 0.0) - jnp.log1p(jnp.exp(-jnp.abs(z)))
        d = 1
        while d < LANES:
            xs = xs + jnp.where(lane >= d, pltpu.roll(xs, d, axis=1), 0.0)
            d *= 2
        xs = xs + carry
        o_ref[:, c * LANES:(c + 1) * LANES] = xs
        carry = xs[:, LANES - 1:LANES]


def _fox_cumsum(cf_rows, bias_rows):
    return pl.pallas_call(
        _fox_cumsum_kernel,
        out_shape=jax.ShapeDtypeStruct(cf_rows.shape, F32),
        name="fox_cumsum",
    )(cf_rows, bias_rows)


def _flash_scratch(rows, tk, mask_scratch=False):
    return [pltpu.VMEM((rows, LANES), F32), pltpu.VMEM((rows, 2 * LANES), F32),
            pltpu.VMEM((rows, tk), F32), pltpu.VMEM((rows, tk), F32),
            pltpu.VMEM((rows, LANES), F32), pltpu.VMEM((rows, LANES), F32)
            ] + ([pltpu.VMEM((rows, tk), F32)] if mask_scratch else [])


def _flash_reset(m_ref, acc_ref):
    m_ref[...] = jnp.full(m_ref.shape, NEG, F32)
    acc_ref[...] = jnp.zeros(acc_ref.shape, F32)


def _flash_begin(m_ref, acc_ref, cm_ref, tq):
    _flash_reset(m_ref, acc_ref)
    cm_ref[...] = _causal_bias(cm_ref.shape[0], cm_ref.shape[1], tq)


def _row_max(s):
    return jnp.broadcast_to(jnp.max(s, axis=-1, keepdims=True), (s.shape[0], LANES))


def _causal_bias(rows, tk, tq):
    r = lax.broadcasted_iota(jnp.int32, (rows, tk), 0) & (tq - 1)
    c = lax.broadcasted_iota(jnp.int32, (rows, tk), 1)
    return jnp.where(c <= r, 0.0, NEG)


def _put_logits(buf, s, rows=slice(None), diag=False, cm_ref=None, tq=None):
    if diag is True:
        s = s + (cm_ref[rows] if cm_ref is not None else _causal_bias(s.shape[0], s.shape[1], tq))
    elif diag is not False:
        s = s + diag.astype(F32) * cm_ref[rows]
    buf[0][rows] = s
    buf[1][rows] = _row_max(s)


def _with_ones(v):
    return jnp.concatenate([v, jnp.ones((v.shape[0], LANES), v.dtype)], axis=1)


def _flash_consume(buf, v, m_ref, acc_ref, mask_tq=None, rows=slice(None)):
    s = buf[0][rows]
    m_cur = buf[1][rows]
    if mask_tq is not None:
        s = s + _causal_bias(s.shape[0], s.shape[1], mask_tq)
        m_cur = _row_max(s)
    m_old = m_ref[rows]
    m_new = jnp.maximum(m_old, m_cur)
    alpha = jnp.exp2(m_old - m_new)
    p = jnp.exp2(s - jnp.tile(m_new, (1, s.shape[1] // LANES))).astype(CDT)
    acc_ref[rows] = jnp.tile(alpha, (1, 2)) * acc_ref[rows] + _dot(p, _with_ones(v))
    m_ref[rows] = m_new


def _flash_result(acc):
    return acc[:, :LANES] / acc[:, LANES:]


def _causal_schedule(nq):
    ent = [(qi, ki, int(ki == qi)) for qi in range(nq) for ki in range(qi + 1)]
    n = len(ent)
    a = np.asarray(ent + [ent[-1]] * 2, np.int32)
    return n, tuple(jnp.asarray(a[:, i]) for i in range(3))


def _flash_stream(n, sched, base, produce, consume, finish, buf_a, buf_b, mask_at):
    qt, kt, lt = sched
    assert mask_at in ("produce", "consume", "mixed")
    always, never = mask_at == "produce", mask_at == "consume"

    def step(cur, nxt, t, diag, produce_diag, masked_already):
        if nxt is not None:
            produce(nxt, qt[base + t + 1], kt[base + t + 1], produce_diag)
        consume(cur, kt[base + t], diag and not masked_already)
        if diag:
            finish(qt[base + t])

    produce(buf_a, qt[base], kt[base], always)

    def pair(j, c):
        t = 2 * j
        l0, l1 = lt[base + t], lt[base + t + 1]
        for d0 in (False, True):
            for d1 in (False, True):
                @pl.when(((l0 != 0) == d0) & ((l1 != 0) == d1))
                def _():
                    step(buf_a, buf_b, t, d0, d1 and not never, always)
                    step(buf_b, buf_a, t + 1, d1, lt[base + t + 2] if always else False, not never)
        return c

    lax.fori_loop(0, n // 2, pair, 0)

    def tail():
        step(buf_a, None, n - 1, True, None, always)

    if isinstance(n, int):
        if n % 2 == 1:
            tail()
    else:
        pl.when(n % 2 == 1)(tail)


def _tile(ref, i, t):
    return ref[0, pl.ds(pl.multiple_of(i * t, t), t), :]


def _diff_attn_kernel(qt_ref, kt_ref, lt_ref, slopes_ref, lam_ref, g_ref, q_ref, k_ref, v_ref, o_ref,
                      m_ref, acc_ref, sa_ref, sb_ref, ma_ref, mb_ref, *, tq, n, lam_init):
    slope = slopes_ref[pl.program_id(1)]
    _flash_reset(m_ref, acc_ref)
    col = lax.broadcasted_iota(jnp.int32, (1, tq), 1).astype(F32)
    lane = lax.broadcasted_iota(jnp.int32, (tq, LANES), 1)
    lf = lam_ref[...]
    lam = (jnp.exp(jnp.sum(lf[0:1] * lf[1:2], axis=-1, keepdims=True))
           - jnp.exp(jnp.sum(lf[2:3] * lf[3:4], axis=-1, keepdims=True)) + lam_init)

    def produce(buf, qi, ki, diag):
        q = _tile(q_ref, qi, tq)
        zero = jnp.zeros_like(q)
        qq = jnp.concatenate([jnp.where(lane < DIFF_DH, q, zero), jnp.where(lane >= DIFF_DH, q, zero)], axis=0)
        s = _dot_nt(qq, _tile(k_ref, ki, tq))
        _put_logits(buf, s + slope * (col + ((ki - qi) * tq).astype(F32)), diag=diag, tq=tq)

    def consume(buf, ki, diag):
        _flash_consume(buf, _tile(v_ref, ki, tq), m_ref, acc_ref, tq if diag else None)

    def finish(qi):
        o = _flash_result(acc_ref[...])
        d = o[0:tq] - lam * o[tq:2 * tq]
        o_ref[0, pl.ds(pl.multiple_of(qi * tq, tq), tq), :] = (
            _rms(d, g_ref[...]) * (1.0 - lam_init)).astype(o_ref.dtype)
        _flash_reset(m_ref, acc_ref)

    _flash_stream(n, (qt_ref, kt_ref, lt_ref), 0, produce, consume, finish, (sa_ref, ma_ref), (sb_ref, mb_ref),
                  mask_at="consume")


_SMEM = pl.BlockSpec(memory_space=pltpu.SMEM)


def _diff_attention(proj3, diff_lambda, subln, lam_init):
    b, s, _ = proj3.shape
    tq = min(TQ_DENSE, s)
    dv = 2 * DIFF_DH
    n, sched = _causal_schedule(s // tq)
    kern = functools.partial(_diff_attn_kernel, tq=tq, n=n, lam_init=lam_init)
    return pl.pallas_call(
        kern,
        grid=(b, DIFF_HEADS),
        in_specs=[
            _SMEM, _SMEM, _SMEM, _SMEM,
            pl.BlockSpec((4, DIFF_DH), lambda bi, h: (0, 0)),
            pl.BlockSpec((1, dv), lambda bi, h: (0, 0)),
            pl.BlockSpec((1, s, LANES), lambda bi, h: (bi, 0, PB_AQ + h)),
            pl.BlockSpec((1, s, LANES), lambda bi, h: (bi, 0, PB_AK + h)),
            pl.BlockSpec((1, s, LANES), lambda bi, h: (bi, 0, PB_AV + h)),
        ],
        out_specs=pl.BlockSpec((1, s, dv), lambda bi, h: (bi, 0, h)),
        out_shape=jax.ShapeDtypeStruct((b, s, DIFF_HEADS * dv), CDT),
        scratch_shapes=_flash_scratch(2 * tq, tq),
        compiler_params=_cparams(("parallel", "parallel")),
        name="diff_attention",
    )(*sched, jnp.asarray(_alibi_slopes(DIFF_HEADS)), diff_lambda, subln.reshape(1, dv), proj3, proj3, proj3)


def _mla_prep_kernel(cq_ref, ckv_ref, kr_ref, krs_ref, gq_ref, gkv_ref, wqm_ref, wqs_ref, wk_ref, wv_ref,
                     cosq_ref, sinq_ref, cosk_ref, sink_ref, q_ref, k_ref, v_ref):
    hq = _rms(cq_ref[0].astype(F32), gq_ref[...]).astype(CDT)
    qm = _dot(hq, wqm_ref[...])
    qs = _dot(hq, wqs_ref[...])
    cosq, sinq = cosq_ref[...], sinq_ref[...]
    hw = 2 * LANES
    for h in range(MLA_HEADS):
        sl = slice(h * hw, (h + 1) * hw)
        q_ref[0, :, sl] = (qm[:, sl] * cosq + qs[:, sl] * sinq).astype(q_ref.dtype)
    hkv = _rms(ckv_ref[0].astype(F32), gkv_ref[...]).astype(CDT)
    kn = _dot(hkv, wk_ref[...])
    v_ref[0] = _dot(hkv, wv_ref[...]).astype(v_ref.dtype)
    kpe = (kr_ref[0].astype(F32) * cosk_ref[...] + krs_ref[0].astype(F32) * sink_ref[...]).astype(k_ref.dtype)
    for h in range(MLA_HEADS):
        k_ref[0, :, h * hw:h * hw + LANES] = kn[:, h * LANES:(h + 1) * LANES].astype(k_ref.dtype)
        k_ref[0, :, h * hw + LANES:(h + 1) * hw] = kpe


def _mla_prep(proj3, gq, gkv, wqm, wqs, wk, wv, tabs):
    b, s, _ = proj3.shape
    tm = min(TM_PROJ, s)
    hw = 2 * LANES
    cosq, sinq, cosk, sink = tabs
    const = lambda shape: pl.BlockSpec(shape, lambda bi, i: (0,) * len(shape))
    return pl.pallas_call(
        _mla_prep_kernel,
        grid=(b, s // tm),
        in_specs=[
            pl.BlockSpec((1, tm, MLA_Q_LORA), lambda bi, i: (bi, i, PB_BCQ // 2)),
            pl.BlockSpec((1, tm, MLA_KV_LORA), lambda bi, i: (bi, i, PB_BCKV // 2)),
            pl.BlockSpec((1, tm, LANES), lambda bi, i: (bi, i, PB_BKR)),
            pl.BlockSpec((1, tm, LANES), lambda bi, i: (bi, i, PB_BKRS)),
            const((1, MLA_Q_LORA)), const((1, MLA_KV_LORA)),
            const((MLA_Q_LORA, MLA_HEADS * hw)), const((MLA_Q_LORA, MLA_HEADS * hw)),
            const((MLA_KV_LORA, MLA_HEADS * MLA_NOPE)), const((MLA_KV_LORA, MLA_HEADS * MLA_VDIM)),
            pl.BlockSpec((tm, hw), lambda bi, i: (i, 0)), pl.BlockSpec((tm, hw), lambda bi, i: (i, 0)),
            pl.BlockSpec((tm, LANES), lambda bi, i: (i, 0)), pl.BlockSpec((tm, LANES), lambda bi, i: (i, 0)),
        ],
        out_specs=[
            pl.BlockSpec((1, tm, MLA_HEADS * hw), lambda bi, i: (bi, i, 0)),
            pl.BlockSpec((1, tm, MLA_HEADS * hw), lambda bi, i: (bi, i, 0)),
            pl.BlockSpec((1, tm, MLA_HEADS * MLA_VDIM), lambda bi, i: (bi, i, 0)),
        ],
        out_shape=[
            jax.ShapeDtypeStruct((b, s, MLA_HEADS * hw), CDT),
            jax.ShapeDtypeStruct((b, s, MLA_HEADS * hw), CDT),
            jax.ShapeDtypeStruct((b, s, MLA_HEADS * MLA_VDIM), CDT),
        ],
        compiler_params=_cparams(("parallel", "parallel")),
        name="mla_prep",
    )(proj3, proj3, proj3, proj3, gq.reshape(1, -1), gkv.reshape(1, -1), wqm, wqs, wk, wv,
      cosq, sinq, cosk, sink)


def _plain_attn_kernel(qt_ref, kt_ref, lt_ref, q_ref, k_ref, v_ref, o_ref,
                       m_ref, acc_ref, sa_ref, sb_ref, ma_ref, mb_ref, *, tq, n, hp, dk, dv):
    _flash_reset(m_ref, acc_ref)
    heads = [(slice(h * tq, (h + 1) * tq), slice(h * dk, (h + 1) * dk), slice(h * dv, (h + 1) * dv))
             for h in range(hp)]

    def produce(buf, qi, ki, diag):
        q, k = _tile(q_ref, qi, tq), _tile(k_ref, ki, tq)
        for rows, kcols, _ in heads:
            _put_logits(buf, _dot_nt(q[:, kcols], k[:, kcols]), rows, diag, tq=tq)

    def consume(buf, ki, diag):
        v = _tile(v_ref, ki, tq)
        for rows, _, vcols in heads:
            _flash_consume(buf, v[:, vcols], m_ref, acc_ref, tq if diag else None, rows)

    def finish(qi):
        for rows, _, vcols in heads:
            o_ref[0, pl.ds(pl.multiple_of(qi * tq, tq), tq), vcols] = _flash_result(acc_ref[rows]).astype(o_ref.dtype)
        _flash_reset(m_ref, acc_ref)

    _flash_stream(n, (qt_ref, kt_ref, lt_ref), 0, produce, consume, finish, (sa_ref, ma_ref), (sb_ref, mb_ref),
                  mask_at="mixed")


def _mla_attention(qc, kc, v):
    b, s, _ = qc.shape
    tq = min(TQ_DENSE, s)
    hw = 2 * LANES
    hp = MLA_HP
    n, sched = _causal_schedule(s // tq)
    return pl.pallas_call(
        functools.partial(_plain_attn_kernel, tq=tq, n=n, hp=hp, dk=hw, dv=MLA_VDIM),
        grid=(b, MLA_HEADS // hp),
        in_specs=[
            _SMEM, _SMEM, _SMEM,
            pl.BlockSpec((1, s, hp * hw), lambda bi, h: (bi, 0, h)),
            pl.BlockSpec((1, s, hp * hw), lambda bi, h: (bi, 0, h)),
            pl.BlockSpec((1, s, hp * MLA_VDIM), lambda bi, h: (bi, 0, h)),
        ],
        out_specs=pl.BlockSpec((1, s, hp * MLA_VDIM), lambda bi, h: (bi, 0, h)),
        out_shape=jax.ShapeDtypeStruct((b, s, MLA_HEADS * MLA_VDIM), CDT),
        scratch_shapes=_flash_scratch(hp * tq, tq),
        compiler_params=_cparams(("parallel", "parallel")),
        name="mla_attention",
    )(*sched, qc, kc, v)


def _fox_attn_kernel(qt_ref, kt_ref, lt_ref, c_ref, q_ref, k_ref, v_ref, o_ref,
                     m_ref, acc_ref, sa_ref, sb_ref, ma_ref, mb_ref, *, tq, n, hp):
    _flash_reset(m_ref, acc_ref)
    heads = [(slice(h * tq, (h + 1) * tq), slice(h * FOX_DH, (h + 1) * FOX_DH)) for h in range(hp)]

    def produce(buf, qi, ki, diag):
        q, k = _tile(q_ref, qi, tq), _tile(k_ref, ki, tq)
        for h, (rows, cols) in enumerate(heads):
            cbase = c_ref[0, h, pl.ds(qi, 1), :][:, 0:1]
            s = _dot_nt(q[:, cols], k[:, cols]) + LOG2E * (cbase - c_ref[0, h, pl.ds(ki, 1), :])
            _put_logits(buf, s, rows, diag, tq=tq)

    def consume(buf, ki, diag):
        v = _tile(v_ref, ki, tq)
        for rows, cols in heads:
            _flash_consume(buf, v[:, cols], m_ref, acc_ref, tq if diag else None, rows)

    def finish(qi):
        for rows, cols in heads:
            o_ref[0, pl.ds(pl.multiple_of(qi * tq, tq), tq), cols] = _flash_result(acc_ref[rows]).astype(o_ref.dtype)
        _flash_reset(m_ref, acc_ref)

    _flash_stream(n, (qt_ref, kt_ref, lt_ref), 0, produce, consume, finish, (sa_ref, ma_ref), (sb_ref, mb_ref),
                  mask_at="mixed")


def _fox_attention(proj3, c4):
    b, s, _ = proj3.shape
    tq = min(TQ_DENSE, s)
    nk = s // tq
    hp = FOX_HP
    w = hp * FOX_DH
    n, sched = _causal_schedule(nk)
    return pl.pallas_call(
        functools.partial(_fox_attn_kernel, tq=tq, n=n, hp=hp),
        grid=(b, FOX_HEADS // hp),
        in_specs=[
            _SMEM, _SMEM, _SMEM,
            pl.BlockSpec((1, hp, nk, tq), lambda bi, h: (bi, h, 0, 0)),
            pl.BlockSpec((1, s, w), lambda bi, h: (bi, 0, PB_CQ // hp + h)),
            pl.BlockSpec((1, s, w), lambda bi, h: (bi, 0, PB_CK // hp + h)),
            pl.BlockSpec((1, s, w), lambda bi, h: (bi, 0, PB_CV // hp + h)),
        ],
        out_specs=pl.BlockSpec((1, s, w), lambda bi, h: (bi, 0, h)),
        out_shape=jax.ShapeDtypeStruct((b, s, FOX_HEADS * FOX_DH), CDT),
        scratch_shapes=_flash_scratch(hp * tq, tq),
        compiler_params=_cparams(("parallel", "parallel")),
        name="fox_attention",
    )(*sched, c4.reshape(b, FOX_HEADS, nk, tq), proj3, proj3, proj3)


def _nsa_compress_kernel(x_ref, w1a_ref, w1b_ref, pea_ref, peb_ref, w2_ref, o_ref):
    x = x_ref[0]
    n = x.shape[0]
    pa = _dot(x, w1a_ref[...])
    pb = _dot(x, w1b_ref[...])
    pe = _dot(pea_ref[...], w1a_ref[...]) + _dot(peb_ref[...], w1b_ref[...])
    hid = pa + pltpu.roll(pb, n - 1, axis=0) + pe[0:1]
    act = 0.5 * hid * (1.0 + jnp.tanh(math.sqrt(2.0 / math.pi) * (hid + 0.044715 * hid * hid * hid)))
    o_ref[0] = _dot(act.astype(CDT), w2_ref[...]).astype(o_ref.dtype)


def _nsa_compress(xc, w1a, w1b, pea, peb, w2):
    b, n, kdim = xc.shape
    hdim = w1a.shape[1]
    const = lambda shape: pl.BlockSpec(shape, lambda bi: (0,) * len(shape))
    return pl.pallas_call(
        _nsa_compress_kernel,
        grid=(b,),
        in_specs=[pl.BlockSpec((1, n, kdim), lambda bi: (bi, 0, 0)),
                  const((kdim, hdim)), const((kdim, hdim)), const((8, kdim)), const((8, kdim)),
                  const((hdim, w2.shape[1]))],
        out_specs=pl.BlockSpec((1, n, w2.shape[1]), lambda bi: (bi, 0, 0)),
        out_shape=jax.ShapeDtypeStruct((b, n, w2.shape[1]), CDT),
        compiler_params=_cparams(("parallel",)),
        name="nsa_compress",
    )(xc, w1a, w1b, pea, peb, w2)


def _nsa_cmp_kernel(slopes_ref, q_ref, kv_ref, oc_ref, sb_ref, used_ref, *, tq, n_topk):
    qi = pl.program_id(1)
    nblk = kv_ref.shape[1]
    q0 = qi * tq
    rowpos = q0 + lax.broadcasted_iota(jnp.int32, (tq, 1), 0)
    cmp_end = lax.broadcasted_iota(jnp.int32, (1, nblk), 1) * CMP_STRIDE + (CMP_LEN - 1)
    negmask = jnp.where(rowpos >= cmp_end, 0.0, NEG)
    end_rel = (cmp_end - q0).astype(F32)
    lane = lax.broadcasted_iota(jnp.int32, (tq, LANES), 1)
    low = lane < NSA_DH
    nn = lax.broadcasted_iota(jnp.int32, (NSA_DH, nblk), 1) * CMP_STRIDE
    jj = lax.broadcasted_iota(jnp.int32, (NSA_DH, nblk), 0) * SLC_LEN
    ovt = (jnp.maximum(jnp.minimum(nn + CMP_LEN, jj + SLC_LEN) - jnp.maximum(nn, jj), 0).astype(F32)
           * (1.0 / CMP_LEN)).astype(CDT)
    jt = lax.broadcasted_iota(jnp.int32, (NSA_DH, tq), 0).astype(F32)
    blk = ((q0 + lax.broadcasted_iota(jnp.int32, (1, tq), 1)) >> SLC_SHIFT).astype(F32)
    fixed = (jt == 0.0) | (jt == blk) | (jt == blk - 1.0)
    beyond = jt > blk
    row_ok = rowpos >= CMP_LEN - 1
    outs = []
    bias = []
    for g in range(NSA_GROUPS):
        kc = kv_ref[0, :, g * LANES:(g + 1) * LANES]
        vc = kv_ref[0, :, (NSA_GROUPS + g) * LANES:(NSA_GROUPS + g + 1) * LANES]
        psum = jnp.zeros((tq, nblk), F32)
        mine = low if g == 0 else jnp.logical_not(low)
        zero = jnp.zeros((tq, LANES), q_ref.dtype)
        qs = jnp.concatenate([jnp.where(mine, q_ref[0, :, j * LANES:(j + 1) * LANES], zero)
                              for j in range(NSA_HPG)], axis=0)
        s_all = _dot_nt(qs, kc)
        ps = []
        for j in range(NSA_HPG):
            s = s_all[j * tq:(j + 1) * tq] + slopes_ref[g * NSA_HPG + j] * end_rel + negmask
            e = jnp.exp2(s - jnp.max(s, axis=-1, keepdims=True))
            den = jnp.sum(e, axis=-1, keepdims=True)
            p = e * jnp.where(row_ok, 1.0 / den, 0.0)
            psum = psum + p
            ps.append(p.astype(CDT))
        o_all = _dot(jnp.concatenate(ps, axis=0), vc)
        outs.extend(o_all[j * tq:(j + 1) * tq] for j in range(NSA_HPG))
        hi = psum.astype(CDT)
        lo = (psum - hi.astype(F32)).astype(CDT)
        imp = _dot_nt(ovt, hi) + _dot_nt(ovt, lo)
        imp = jnp.where(fixed, -jnp.inf, jnp.where(beyond, NEG_INF, imp))
        sbt = jnp.where(fixed, 0.0, NEG)
        for _ in range(n_topk - 3):
            mx = jnp.max(imp, axis=0, keepdims=True)
            idx = jnp.min(jnp.where(imp == mx, jt, float(LANES)), axis=0, keepdims=True)
            hit = jt == idx
            sbt = jnp.where(hit, 0.0, sbt)
            imp = jnp.where(hit, -jnp.inf, imp)
        bias.append(sbt)
    sb = jnp.concatenate([bias[1], bias[0]], axis=0).T
    sb_ref[0] = sb.astype(sb_ref.dtype)
    used = jnp.max(jnp.where(sb == 0.0, 1.0, 0.0), axis=0, keepdims=True)
    used_ref[0, 0] = jnp.broadcast_to(used, used_ref.shape[2:])
    for blk_i in range(NSA_HEADS // 2):
        oc_ref[0, :, blk_i * LANES:(blk_i + 1) * LANES] = jnp.where(
            low, outs[2 * blk_i], outs[2 * blk_i + 1]).astype(oc_ref.dtype)


def _nsa_cmp_select(proj3, kvc, n_topk):
    assert n_topk >= 3, "the three always-selected blocks must fit in the top-k budget"
    b, s, _ = proj3.shape
    tq = min(TQ_NSA, s)
    nblk = kvc.shape[1]
    return pl.pallas_call(
        functools.partial(_nsa_cmp_kernel, tq=tq, n_topk=n_topk),
        grid=(b, s // tq),
        in_specs=[
            pl.BlockSpec(memory_space=pltpu.SMEM),
            pl.BlockSpec((1, tq, 4 * LANES), lambda bi, qi: (bi, qi, PB_DQ // 4)),
            pl.BlockSpec((1, nblk, kvc.shape[2]), lambda bi, qi: (bi, 0, 0)),
        ],
        out_specs=[
            pl.BlockSpec((1, tq, NSA_HEADS * NSA_DH), lambda bi, qi: (bi, qi, 0)),
            pl.BlockSpec((1, tq, LANES), lambda bi, qi: (bi, qi, 0)),
            pl.BlockSpec((1, 1, 8, LANES), lambda bi, qi: (bi, qi, 0, 0)),
        ],
        out_shape=[jax.ShapeDtypeStruct((b, s, NSA_HEADS * NSA_DH), CDT),
                   jax.ShapeDtypeStruct((b, s, LANES), CDT),
                   jax.ShapeDtypeStruct((b, s // tq, 8, LANES), F32)],
        compiler_params=_cparams(("parallel", "parallel")),
        name="nsa_cmp_select",
    )(jnp.asarray(_alibi_slopes(NSA_HEADS)), proj3, kvc)


def _compact_heads(heads, mine, low):
    both = [jnp.where(mine, a, pltpu.roll(a, NSA_DH, axis=1)) for a in heads]
    out = [jnp.where(low, both[2 * jj], both[2 * jj + 1]) for jj in range(NSA_HPG // 2)]
    return jnp.concatenate(out, axis=1)


def _nsa_win_kernel(slopes_ref, q_ref, kp_ref, kc_ref, vp_ref, vc_ref, o_ref, *, tq):
    qi = pl.program_id(1)
    lane = lax.broadcasted_iota(jnp.int32, (tq, LANES), 1)
    low = lane < NSA_DH
    r = lax.broadcasted_iota(jnp.int32, (tq, tq), 0)
    c = lax.broadcasted_iota(jnp.int32, (tq, tq), 1)
    own = c <= r
    ndist = jnp.where(own, c - r, c - r - tq).astype(F32)
    own_f = jnp.where(own, 1.0, 0.0).astype(CDT)
    prev_pen = jnp.where(qi > 0, 0.0, NEG)
    q = q_ref[0]
    zero = jnp.zeros((tq, LANES), q.dtype)
    mine = (low, jnp.logical_not(low))
    qs = jnp.concatenate([jnp.where(mine[g], q[:, j * LANES:(j + 1) * LANES], zero)
                          for g in range(NSA_GROUPS) for j in range(NSA_HPG)], axis=0)
    s_own, s_prev = _dot_nt(qs, kc_ref[0]), _dot_nt(qs, kp_ref[0])
    ps = []
    for hd in range(NSA_HEADS):
        rows = slice(hd * tq, (hd + 1) * tq)
        s = jnp.where(own, s_own[rows], s_prev[rows] + prev_pen) + slopes_ref[hd] * ndist
        ps.append(jnp.exp2(s - jnp.max(s, axis=-1, keepdims=True)).astype(CDT))
    p = jnp.concatenate(ps, axis=0)
    p_own = p * jnp.tile(own_f, (NSA_HEADS, 1))
    o = _flash_result(_dot(p_own, _with_ones(vc_ref[0])) + _dot(p - p_own, _with_ones(vp_ref[0])))
    for g in range(NSA_GROUPS):
        heads = [o[(g * NSA_HPG + j) * tq:(g * NSA_HPG + j + 1) * tq] for j in range(NSA_HPG)]
        w = NSA_HPG * NSA_DH
        o_ref[0, :, g * w:(g + 1) * w] = _compact_heads(heads, mine[g], low).astype(o_ref.dtype)


def _nsa_window(proj3):
    b, s, _ = proj3.shape
    tq = WINDOW
    return pl.pallas_call(
        functools.partial(_nsa_win_kernel, tq=tq),
        grid=(b, s // tq),
        in_specs=[
            pl.BlockSpec(memory_space=pltpu.SMEM),
            pl.BlockSpec((1, tq, 4 * LANES), lambda bi, qi: (bi, qi, PB_DQ // 4)),
            pl.BlockSpec((1, tq, LANES), lambda bi, qi: (bi, jnp.maximum(qi - 1, 0), PB_WIN_K)),
            pl.BlockSpec((1, tq, LANES), lambda bi, qi: (bi, qi, PB_WIN_K)),
            pl.BlockSpec((1, tq, LANES), lambda bi, qi: (bi, jnp.maximum(qi - 1, 0), PB_WIN_V)),
            pl.BlockSpec((1, tq, LANES), lambda bi, qi: (bi, qi, PB_WIN_V)),
        ],
        out_specs=pl.BlockSpec((1, tq, NSA_HEADS * NSA_DH), lambda bi, qi: (bi, qi, 0)),
        out_shape=jax.ShapeDtypeStruct((b, s, NSA_HEADS * NSA_DH), CDT),
        compiler_params=_cparams(("parallel", "parallel")),
        name="nsa_window",
    )(jnp.asarray(_alibi_slopes(NSA_HEADS)), proj3, proj3, proj3, proj3, proj3)


def _nsa_sel_kernel(cnt_ref, qt_ref, kt_ref, lt_ref, slopes_ref, q_ref, sb_ref, k_ref, v_ref, oc_ref, ow_ref, gl_ref,
                    e_ref, o_ref, m_ref, acc_ref, sa_ref, sb2_ref, ma_ref, mb_ref, cm_ref, *, tq, rows_per_problem):
    g = pl.program_id(1)
    w = NSA_HPG * NSA_DH
    lane = lax.broadcasted_iota(jnp.int32, (tq, LANES), 1)
    low = lane < NSA_DH
    mine = (lane >> HALF_SHIFT) == g
    _flash_begin(m_ref, acc_ref, cm_ref, tq)
    col = lax.broadcasted_iota(jnp.int32, (1, tq), 1).astype(F32)
    jl = lane & (NSA_DH - 1)
    krow = lax.broadcasted_iota(jnp.int32, (tq, LANES), 0)

    def produce(buf, qi, ki, diag):
        q = _tile(q_ref, qi, tq)
        sb = _tile(sb_ref, qi, tq)
        qa = jnp.concatenate([jnp.where(mine, q[:, j * LANES:(j + 1) * LANES], sb) for j in range(NSA_HPG)], axis=0)
        k = _tile(k_ref, ki, tq)
        onehot = jnp.where(((ki * tq + krow) >> SLC_SHIFT) == jl, 1.0, 0.0).astype(k.dtype)
        s_all = _dot_nt(qa, jnp.where(mine, k, onehot))
        rel = ((ki - qi) * tq).astype(F32)
        for j in range(NSA_HPG):
            rows = slice(j * tq, (j + 1) * tq)
            _put_logits(buf, s_all[rows] + slopes_ref[g * NSA_HPG + j] * (col + rel), rows, diag, cm_ref)

    def consume(buf, ki, diag):
        _flash_consume(buf, _tile(v_ref, ki, tq), m_ref, acc_ref, tq if diag else None)

    def finish(qi):
        o = _flash_result(acc_ref[...])
        o_s = _compact_heads([o[j * tq:(j + 1) * tq] for j in range(NSA_HPG)], mine, low)
        gates = _split_dot(_sigmoid(_tile(gl_ref, qi, tq)), e_ref[0])
        y = (gates[:, 0:w] * _tile(oc_ref, qi, tq).astype(F32) + gates[:, w:2 * w] * o_s
             + gates[:, 2 * w:3 * w] * _tile(ow_ref, qi, tq).astype(F32))
        o_ref[0, pl.ds(pl.multiple_of(qi * tq, tq), tq), :] = y.astype(o_ref.dtype)
        _flash_reset(m_ref, acc_ref)

    prob = pl.program_id(0) * NSA_GROUPS + g
    _flash_stream(cnt_ref[prob], (qt_ref, kt_ref, lt_ref), prob * rows_per_problem, produce, consume, finish,
                  (sa_ref, ma_ref), (sb2_ref, mb_ref), mask_at="produce")


def _nsa_selected(proj3, sbias, used, o_c, o_w, small3, expand):
    b, s, _ = proj3.shape
    tq = min(TQ_NSA, s)
    nq = s // tq
    w = NSA_HPG * NSA_DH
    u = used[:, :, 0, :].reshape(b, nq, NSA_GROUPS, NSA_DH)[:, :, ::-1, :nq * (tq // SLC_LEN)]
    flags = (u.reshape(b, nq, NSA_GROUPS, nq, tq // SLC_LEN).max(axis=-1) > 0.0).astype(jnp.int32)
    flags = flags.transpose(0, 2, 1, 3)
    qt = jnp.arange(nq, dtype=jnp.int32)
    need = jnp.where(qt[None, :] < qt[:, None], flags, (qt[None, :] == qt[:, None]).astype(jnp.int32))
    need = need.reshape(b, NSA_GROUPS, nq * nq)
    cnt = need.sum(axis=-1).astype(jnp.int32)
    order = jnp.argsort(1 - need, axis=-1, stable=True).astype(jnp.int32)
    order = jnp.pad(order, ((0, 0), (0, 0), (0, 2)))
    rows = nq * nq + 2
    sched = (order // nq, order % nq, (order // nq == order % nq).astype(jnp.int32))
    return pl.pallas_call(
        functools.partial(_nsa_sel_kernel, tq=tq, rows_per_problem=rows),
        grid=(b, NSA_GROUPS),
        in_specs=[
            _SMEM, _SMEM, _SMEM, _SMEM, _SMEM,
            pl.BlockSpec((1, s, 4 * LANES), lambda bi, g: (bi, 0, PB_DQ // 4)),
            pl.BlockSpec((1, s, LANES), lambda bi, g: (bi, 0, 0)),
            pl.BlockSpec((1, s, LANES), lambda bi, g: (bi, 0, PB_SEL_K)),
            pl.BlockSpec((1, s, LANES), lambda bi, g: (bi, 0, PB_SEL_V)),
            pl.BlockSpec((1, s, w), lambda bi, g: (bi, 0, g)),
            pl.BlockSpec((1, s, w), lambda bi, g: (bi, 0, g)),
            pl.BlockSpec((1, s, LANES), lambda bi, g: (bi, 0, 0)),
            pl.BlockSpec((1, LANES, 3 * w), lambda bi, g: (g, 0, 0)),
        ],
        out_specs=pl.BlockSpec((1, s, w), lambda bi, g: (bi, 0, g)),
        out_shape=jax.ShapeDtypeStruct((b, s, NSA_HEADS * NSA_DH), CDT),
        scratch_shapes=_flash_scratch(NSA_HPG * tq, tq, mask_scratch=True),
        compiler_params=_cparams(("parallel", "parallel")),
        name="nsa_selected",
    )(cnt.reshape(-1), *[t.reshape(-1) for t in sched], jnp.asarray(_alibi_slopes(NSA_HEADS)),
      proj3, sbias, proj3, proj3, o_c, o_w, small3, expand)


def _merge_kernel(ya_ref, yb_ref, yc_ref, yd_ref, ga_ref, gb_ref, gc_ref, gd_ref, wb_ref, wo_ref, x_ref, o_ref):
    merged = None
    for n, (y_ref, g_ref) in enumerate(((ya_ref, ga_ref), (yb_ref, gb_ref), (yc_ref, gc_ref), (yd_ref, gd_ref))):
        t = _sigmoid(g_ref[...].astype(F32)) * _dot(y_ref[...], wb_ref[n])
        merged = t if merged is None else merged + t
    o_ref[...] = x_ref[...] + _dot(merged.astype(CDT), wo_ref[...])


def _merge(ys, proj2, wb, wo, x2, layer):
    t, d = x2.shape
    tm = min(TM_ROWS, t)
    gate_blk = PB_GATE * LANES // d
    yspec = pl.BlockSpec((tm, BRANCH_WIDTH), lambda i: (i, 0))
    gspecs = [pl.BlockSpec((tm, d), functools.partial(lambda i, n: (i, gate_blk + n), n=n)) for n in range(N_BRANCH)]
    return pl.pallas_call(
        _merge_kernel,
        grid=(t // tm,),
        in_specs=[yspec] * N_BRANCH + gspecs + [
            pl.BlockSpec((None, N_BRANCH, BRANCH_WIDTH, d), lambda i: (layer, 0, 0, 0)),
            pl.BlockSpec((None, d, d), lambda i: (layer, 0, 0)),
            pl.BlockSpec((tm, d), lambda i: (i, 0)),
        ],
        out_specs=pl.BlockSpec((tm, d), lambda i: (i, 0)),
        out_shape=jax.ShapeDtypeStruct((t, d), F32),
        compiler_params=_cparams(("parallel",)),
        name="merge",
    )(*ys, proj2, proj2, proj2, proj2, wb, wo, x2)


HALO = 16


def _ffn_kernel(x_ref, xh_ref, g_ref, wu_ref, cw_ref, cb_ref, wd_ref, gf_ref, o_ref, he_ref, u_ref, act_ref,
                *, tm, fc, final):
    i = pl.program_id(1)
    x = x_ref[0]
    g = g_ref[...]
    xh = xh_ref[0] * (i > 0).astype(F32)
    he_ref[0:HALO] = _rms(xh, g).astype(CDT)
    he_ref[HALO:HALO + tm] = _rms(x, g).astype(CDT)
    he = he_ref[...]
    for c in range(D_FF // fc):
        outs = []
        for half in range(2):
            ub = u_ref.at[c % 2, half]
            lo = half * D_FF + c * fc
            ub[...] = _dot(he, wu_ref[:, lo:lo + fc])
            conv = cb_ref[:, lo:lo + fc]
            for kk in range(CONV_WIDTH):
                off = HALO - (CONV_WIDTH - 1) + kk
                conv = conv + cw_ref[kk:kk + 1, lo:lo + fc] * ub[off:off + tm, :]
            outs.append(conv)
        a, gg = outs
        act_ref[:, c * fc:(c + 1) * fc] = (a * _sigmoid(a) * gg).astype(CDT)
    y = x + _dot(act_ref[...], wd_ref[...])
    if final:
        y = _rms(y, gf_ref[...])
    o_ref[0] = y


def _ffn(x3, g, wu, cw, cb, wd, gf, layer, final):
    b, s, d = x3.shape
    tm = min(TM_ROWS, s)
    fc = FFN_CHUNK
    assert D_FF % fc == 0
    const = lambda shape: pl.BlockSpec(shape, lambda bi, i: (0,) * len(shape), pipeline_mode=pl.Buffered(1))
    stacked = lambda shape: pl.BlockSpec((None,) + shape, lambda bi, i: (layer,) + (0,) * len(shape),
                                         pipeline_mode=pl.Buffered(1))
    return pl.pallas_call(
        functools.partial(_ffn_kernel, tm=tm, fc=fc, final=final),
        grid=(b, s // tm),
        in_specs=[
            pl.BlockSpec((1, tm, d), lambda bi, i: (bi, i, 0)),
            pl.BlockSpec((1, HALO, d), lambda bi, i: (bi, jnp.maximum(i * (tm // HALO) - 1, 0), 0)),
            const((1, d)), stacked((d, 2 * D_FF)), const((CONV_WIDTH, 2 * D_FF)), const((1, 2 * D_FF)),
            stacked((D_FF, d)), const((1, d)),
        ],
        out_specs=pl.BlockSpec((1, tm, d), lambda bi, i: (bi, i, 0)),
        out_shape=jax.ShapeDtypeStruct((b, s, d), F32),
        scratch_shapes=[pltpu.VMEM((tm + HALO, d), CDT), pltpu.VMEM((2, 2, tm + HALO, fc), F32),
                        pltpu.VMEM((tm, D_FF), CDT)],
        compiler_params=_cparams(("parallel", "arbitrary")),
        name="conv_glu_mlp",
    )(x3, x3, g.reshape(1, d), wu, cw, cb.reshape(1, -1), wd, gf.reshape(1, d))


def _w_in_plan():
    widths = (512, 512, 512, MLA_Q_LORA, MLA_KV_LORA, MLA_ROPE, 512, 512, 512, FOX_HEADS,
              512, 768, 3 * NSA_HEADS, N_BRANCH * D_MODEL)
    (a_q, a_k, a_v, b_cq, b_ckv, b_kr, c_q, c_k, c_v, c_f, d_q, d_kv, d_g, gate, _) = np.cumsum((0,) + widths).tolist()
    half = MLA_ROPE // 2

    def run(src, nblocks, scale=1.0):
        return [[(src + i * LANES, LANES, scale)] for i in range(nblocks)]

    blocks = (run(a_q, 4, LOG2E * DIFF_DH ** -0.5) + run(a_k, 4) + run(a_v, 4)
              + run(c_q, 4, LOG2E * FOX_DH ** -0.5) + run(c_k, 4) + run(c_v, 4))
    sd = LOG2E * NSA_DH ** -0.5
    blocks += [[(d_q + j * NSA_DH, NSA_DH, sd), (d_q + (NSA_HPG + j) * NSA_DH, NSA_DH, sd)] for j in range(NSA_HPG)]
    blocks += run(d_kv, 6) + run(b_cq, 2) + run(b_ckv, 2)
    blocks += [[(b_kr, MLA_ROPE, 1.0), None],
               [(b_kr + half, half, -1.0), (b_kr, half, 1.0), None]]
    blocks += run(gate, N_BRANCH * D_MODEL // LANES)
    assert len(blocks) * LANES == N_PROJ
    small = [(c_f, FOX_HEADS, 1.0), (d_g, 3 * NSA_HEADS, 1.0), None]
    return blocks, small


def _w_in_relayout_kernel(w_ref, big_ref, small_ref):
    cols = w_ref.shape[1]
    blocks, small = _w_in_plan()
    for j, pieces in enumerate(blocks):
        row = j * LANES
        for p in pieces:
            if p is None:
                big_ref[row:(j + 1) * LANES, :] = jnp.zeros(((j + 1) * LANES - row, cols), big_ref.dtype)
            else:
                src, n, scale = p
                v = w_ref[src:src + n, :]
                big_ref[row:row + n, :] = (v if scale == 1.0 else v * scale).astype(big_ref.dtype)
                row += n
    (sf, nf, _), (sg, ng, _), _ = small
    r = lax.broadcasted_iota(jnp.int32, (LANES, cols), 0)
    side = jnp.where(r < nf, w_ref[sf:sf + LANES, :],
                     jnp.where(r < nf + ng, w_ref[sg - nf:sg - nf + LANES, :], 0.0))
    small_ref[...] = side.astype(small_ref.dtype)


def _w_in_relayout(w):
    nl, d, n = w.shape
    wt = jnp.transpose(w, (2, 0, 1)).reshape(n, nl * d)
    tc = min(256, d)
    return pl.pallas_call(
        _w_in_relayout_kernel,
        grid=(nl, d // tc),
        in_specs=[pl.BlockSpec((n, tc), lambda l, i: (0, l * (d // tc) + i))],
        out_specs=[pl.BlockSpec((None, N_PROJ, tc), lambda l, i: (l, 0, i)),
                   pl.BlockSpec((None, LANES, tc), lambda l, i: (l, 0, i))],
        out_shape=[jax.ShapeDtypeStruct((nl, N_PROJ, d), CDT), jax.ShapeDtypeStruct((nl, LANES, d), CDT)],
        compiler_params=_cparams(("parallel", "parallel")),
        name="w_in_relayout",
    )(wt)


def _prep_mla(w_uq, w_ukv):
    r = w_uq.shape[0]
    hw = 2 * LANES
    half = MLA_ROPE // 2
    scale = LOG2E * (MLA_NOPE + MLA_ROPE) ** -0.5
    wq = (w_uq * scale).reshape(r, MLA_HEADS, MLA_NOPE + MLA_ROPE)
    nope, t1, t2 = wq[..., :MLA_NOPE], wq[..., MLA_NOPE:MLA_NOPE + half], wq[..., MLA_NOPE + half:]
    zpad = jnp.zeros((r, MLA_HEADS, hw - MLA_NOPE - MLA_ROPE), w_uq.dtype)
    wqm = jnp.concatenate([nope, t1, t2, zpad], axis=-1).reshape(r, MLA_HEADS * hw)
    wqs = jnp.concatenate([jnp.zeros_like(nope), -t2, t1, zpad], axis=-1).reshape(r, MLA_HEADS * hw)
    wkv = w_ukv.reshape(w_ukv.shape[0], MLA_HEADS, MLA_NOPE + MLA_VDIM)
    wk = wkv[..., :MLA_NOPE].reshape(-1, MLA_HEADS * MLA_NOPE)
    wv = wkv[..., MLA_NOPE:].reshape(-1, MLA_HEADS * MLA_VDIM)
    return wqm.astype(CDT), wqs.astype(CDT), wk.astype(CDT), wv.astype(CDT)


def _rope_tables(s):
    half = MLA_ROPE // 2
    inv_freq = ROPE_THETA ** (-jnp.arange(0, MLA_ROPE, 2, dtype=F32) / MLA_ROPE)
    ang = jnp.arange(s, dtype=F32)[:, None] * inv_freq[None, :]
    cos, sin = jnp.cos(ang), jnp.sin(ang)
    z = jnp.zeros((s, LANES - MLA_ROPE), F32)
    cosk = jnp.concatenate([cos, cos, z], axis=1)
    sink = jnp.concatenate([sin, sin, z], axis=1)
    cosq = jnp.concatenate([jnp.ones((s, MLA_NOPE), F32), cosk], axis=1)
    sinq = jnp.concatenate([jnp.zeros((s, MLA_NOPE), F32), sink], axis=1)
    return cosq, sinq, cosk, sink


def _prep_compress(pe, w1, w2):
    eye2 = jnp.eye(2, dtype=F32)
    w1r = w1.reshape(2, CMP_LEN, NSA_DH, CMP_HIDDEN).astype(CDT)
    same = np.eye(2, dtype=bool)
    diag_kg = jnp.asarray(same[:, None, :, None] & same[None, :, None, :])

    def expand(wpart):
        src = wpart.transpose(1, 0, 2, 3)[:, :, None, :, None, None, :]
        t = jnp.where(diag_kg[None, :, :, None, :, :, None], src, jnp.zeros((), CDT))
        return t.reshape(CMP_STRIDE * 4 * NSA_DH, 4 * CMP_HIDDEN)

    w1a, w1b = expand(w1r[:, :CMP_STRIDE]), expand(w1r[:, CMP_STRIDE:])

    def pe_row(p):
        t = jnp.broadcast_to(p.transpose(1, 0, 2)[:, :, None, :], (CMP_STRIDE, 2, NSA_GROUPS, NSA_DH))
        return jnp.pad(t.reshape(1, -1), ((0, 7), (0, 0)))

    pea, peb = pe_row(pe[:, :CMP_STRIDE]), pe_row(pe[:, CMP_STRIDE:])
    w2b = jnp.einsum('khd,kK,gG,u->kghKGud', w2, eye2, eye2, jnp.ones((2,), F32))
    w2b = w2b.reshape(4 * CMP_HIDDEN, 4 * 2 * NSA_DH)
    return w1a.astype(CDT), w1b.astype(CDT), pea.astype(CDT), peb.astype(CDT), w2b.astype(CDT)


def _gate_expand():
    e = np.zeros((NSA_GROUPS, LANES, 3, NSA_HPG, NSA_DH), np.float32)
    for g in range(NSA_GROUPS):
        for j in range(NSA_HPG):
            for br in range(3):
                e[g, SMALL_G + (g * NSA_HPG + j) * 3 + br, br, j, :] = 1.0
    return jnp.asarray(e.reshape(NSA_GROUPS, LANES, 3 * NSA_HPG * NSA_DH)).astype(CDT)


def _token_mixers(x3, l, norm_mix, w_in, diff_lambda, diff_subln, mla_norm_q, mla_w_uq, mla_norm_kv, mla_w_ukv,
                  fox_b_f, nsa_cmp_pe, nsa_cmp_w1, nsa_cmp_w2, w_branch, w_out, rope_tabs):
    b, s, d = x3.shape
    t = b * s
    x2 = x3.reshape(t, d)
    proj, small = _in_proj(x2, norm_mix, *w_in, l)
    proj3 = proj.reshape(b, s, N_PROJ)
    small3 = small.reshape(b, s, LANES)

    lam_init = 0.8 - 0.6 * math.exp(-0.3 * l)
    y_a = _diff_attention(proj3, diff_lambda, diff_subln, lam_init)

    wqm, wqs, wk, wv = _prep_mla(mla_w_uq, mla_w_ukv)
    qc, kc, vv = _mla_prep(proj3, mla_norm_q, mla_norm_kv, wqm, wqs, wk, wv, rope_tabs)
    y_b = _mla_attention(qc, kc, vv)

    cf_rows = small3[:, :, SMALL_F:SMALL_F + FOX_HEADS].transpose(0, 2, 1).reshape(b * FOX_HEADS, s)
    bias_rows = jnp.tile(fox_b_f.astype(F32), b).reshape(b * FOX_HEADS, 1)
    c4 = _fox_cumsum(cf_rows, bias_rows)
    y_c = _fox_attention(proj3, c4)

    w1a, w1b, pea, peb, w2b = _prep_compress(nsa_cmp_pe, nsa_cmp_w1, nsa_cmp_w2)
    xc = proj3[:, :, PB_CMP_K * LANES:(PB_CMP_V + 1) * LANES].reshape(b, s // CMP_STRIDE, CMP_STRIDE * 2 * LANES)
    kvc = _nsa_compress(xc, w1a, w1b, pea, peb, w2b)
    n_topk = min(SLC_TOPK, s // SLC_LEN)
    o_c, sbias, used = _nsa_cmp_select(proj3, kvc, n_topk)
    o_w = _nsa_window(proj3)
    y_d = _nsa_selected(proj3, sbias, used, o_c, o_w, small3, _gate_expand())

    ys = [y.reshape(t, BRANCH_WIDTH) for y in (y_a, y_b, y_c, y_d)]
    return _merge(ys, proj, w_branch, w_out, x2, l).reshape(b, s, d)


def kernel(x, norm_mix, w_in, diff_lambda, diff_subln, mla_norm_q, mla_w_uq, mla_norm_kv, mla_w_ukv, fox_b_f,
           nsa_cmp_pe, nsa_cmp_w1, nsa_cmp_w2, w_branch, w_out, norm_ffn, w_up, conv_w, conv_b, w_down, norm_final):
    depth = w_in.shape[0]
    s = x.shape[1]
    rope_tabs = _rope_tables(s)
    w_in = _w_in_relayout(w_in)
    w_up16, w_down16 = w_up.astype(CDT), w_down.astype(CDT)
    w_branch16, w_out16 = w_branch.astype(CDT), w_out.astype(CDT)
    for l in range(depth):
        x = _token_mixers(x, l, norm_mix[l], w_in, diff_lambda[l], diff_subln[l], mla_norm_q[l], mla_w_uq[l],
                          mla_norm_kv[l], mla_w_ukv[l], fox_b_f[l], nsa_cmp_pe[l], nsa_cmp_w1[l], nsa_cmp_w2[l],
                          w_branch16, w_out16, rope_tabs)
        x = _ffn(x, norm_ffn[l], w_up16, conv_w[l], conv_b[l], w_down16, norm_final, l, final=(l == depth - 1))
    return x
```

```python
import functools
import math

import numpy as np
import jax
import jax.numpy as jnp
from jax import lax
from jax.experimental import pallas as pl
from jax.experimental.pallas import tpu as pltpu

F32 = jnp.float32
CDT = jnp.bfloat16

NEG = -1e30
NEG_INF = -1e30
NORM_EPS = 1e-6
LOG2E = 1.4426950408889634
LANES = 128

D_MODEL = 1024
DIFF_HEADS, DIFF_DH = 4, 64
MLA_HEADS, MLA_NOPE, MLA_ROPE, MLA_VDIM = 4, 128, 64, 128
MLA_Q_LORA, MLA_KV_LORA = 256, 256
ROPE_THETA = 10000.0
FOX_HEADS, FOX_DH = 4, 128
NSA_HEADS, NSA_GROUPS, NSA_DH = 8, 2, 64
NSA_HPG = NSA_HEADS // NSA_GROUPS
CMP_STRIDE = 16
CMP_LEN = 2 * CMP_STRIDE
CMP_HIDDEN = 128
SLC_LEN = 64
SLC_SHIFT = 6
HALF_SHIFT = 6
SLC_TOPK = 8
WINDOW = 256
N_BRANCH = 4
BRANCH_WIDTH = 512
D_FF = 2816
CONV_WIDTH = 3

PB_AQ, PB_AK, PB_AV = 0, 4, 8
PB_CQ, PB_CK, PB_CV = 12, 16, 20
PB_DQ = 24
PB_CMP_K, PB_CMP_V, PB_SEL_K, PB_SEL_V, PB_WIN_K, PB_WIN_V = 28, 29, 30, 31, 32, 33
PB_BCQ, PB_BCKV, PB_BKR, PB_BKRS = 34, 36, 38, 39
PB_GATE = 40
N_PROJ = 72 * LANES
SMALL_F, SMALL_G = 0, 4

VMEM_LIMIT = 56 * 1024 * 1024
MXU_TILE = 256
TQ_DENSE = 512
TQ_NSA = WINDOW
TM_PROJ, TN_PROJ = 1024, 9 * MXU_TILE
TM_ROWS = 512
FFN_CHUNK = MXU_TILE
FOX_HP = 2
MLA_HP = 2


def _cparams(sem):
    return pltpu.CompilerParams(dimension_semantics=sem, vmem_limit_bytes=VMEM_LIMIT)


def _rms(xf, g):
    return xf * lax.rsqrt(jnp.mean(xf * xf, axis=-1, keepdims=True) + NORM_EPS) * g


def _sigmoid(x):
    return 0.5 * jnp.tanh(0.5 * x) + 0.5


def _dot(a, b):
    return jnp.dot(a, b, preferred_element_type=F32)


def _dot_nt(a, b):
    return lax.dot_general(a, b, (((1,), (1,)), ((), ())), preferred_element_type=F32)


def _split_dot(a, b):
    hi = a.astype(CDT)
    lo = (a - hi.astype(F32)).astype(CDT)
    return _dot(hi, b) + _dot(lo, b)


def _alibi_slopes(n):
    return (LOG2E * np.exp2(-8.0 * np.arange(1, n + 1) / n)).astype(np.float32)


def _inproj_kernel(x_ref, g_ref, w_ref, ws_ref, o_ref, os_ref, h_ref):
    @pl.when(pl.program_id(1) == 0)
    def _():
        h = _rms(x_ref[...], g_ref[...]).astype(CDT)
        h_ref[...] = h
        os_ref[...] = _dot_nt(h, ws_ref[...])

    o_ref[...] = _dot_nt(h_ref[...], w_ref[...]).astype(o_ref.dtype)


def _in_proj(x2, g, w, ws, layer):
    t, d = x2.shape
    n = w.shape[1]
    tm = min(TM_PROJ, t)
    tn = TN_PROJ
    assert n % tn == 0
    return pl.pallas_call(
        _inproj_kernel,
        grid=(t // tm, n // tn),
        in_specs=[
            pl.BlockSpec((tm, d), lambda i, j: (i, 0)),
            pl.BlockSpec((1, d), lambda i, j: (0, 0)),
            pl.BlockSpec((None, tn, d), lambda i, j: (layer, j, 0)),
            pl.BlockSpec((None, LANES, d), lambda i, j: (layer, 0, 0)),
        ],
        out_specs=[
            pl.BlockSpec((tm, tn), lambda i, j: (i, j)),
            pl.BlockSpec((tm, LANES), lambda i, j: (i, 0)),
        ],
        out_shape=[jax.ShapeDtypeStruct((t, n), CDT), jax.ShapeDtypeStruct((t, LANES), F32)],
        scratch_shapes=[pltpu.VMEM((tm, d), CDT)],
        compiler_params=_cparams(("parallel", "arbitrary")),
        name="in_proj",
    )(x2, g.reshape(1, d), w, ws)


def _fox_cumsum_kernel(cf_ref, bf_ref, o_ref):
    rows, s = cf_ref.shape
    lane = lax.broadcasted_iota(jnp.int32, (rows, LANES), 1)
    carry = jnp.zeros((rows, 1), F32)
    for c in range(s // LANES):
        z = cf_ref[:, c * LANES:(c + 1) * LANES] + bf_ref[...]
        xs = jnp.minimum(z, 0.0) - jnp.log1p(jnp.exp(-jnp.abs(z)))
        d = 1
        while d < LANES:
            xs = xs + jnp.where(lane >= d, pltpu.roll(xs, d, axis=1), 0.0)
            d *= 2
        xs = xs + carry
        o_ref[:, c * LANES:(c + 1) * LANES] = xs
        carry = xs[:, LANES - 1:LANES]


def _fox_cumsum(cf_rows, bias_rows):
    return pl.pallas_call(
        _fox_cumsum_kernel,
        out_shape=jax.ShapeDtypeStruct(cf_rows.shape, F32),
        name="fox_cumsum",
    )(cf_rows, bias_rows)


def _flash_scratch(rows, tk, mask_scratch=False):
    return [pltpu.VMEM((rows, LANES), F32), pltpu.VMEM((rows, 2 * LANES), F32),
            pltpu.VMEM((rows, tk), F32), pltpu.VMEM((rows, tk), F32),
            pltpu.VMEM((rows, LANES), F32), pltpu.VMEM((rows, LANES), F32)
            ] + ([pltpu.VMEM((rows, tk), F32)] if mask_scratch else [])


def _flash_reset(m_ref, acc_ref):
    m_ref[...] = jnp.full(m_ref.shape, NEG, F32)
    acc_ref[...] = jnp.zeros(acc_ref.shape, F32)


def _flash_begin(m_ref, acc_ref, cm_ref, tq):
    _flash_reset(m_ref, acc_ref)
    cm_ref[...] = _causal_bias(cm_ref.shape[0], cm_ref.shape[1], tq)


def _row_max(s):
    return jnp.broadcast_to(jnp.max(s, axis=-1, keepdims=True), (s.shape[0], LANES))


def _causal_bias(rows, tk, tq):
    r = lax.broadcasted_iota(jnp.int32, (rows, tk), 0) & (tq - 1)
    c = lax.broadcasted_iota(jnp.int32, (rows, tk), 1)
    return jnp.where(c <= r, 0.0, NEG)


def _put_logits(buf, s, rows=slice(None), diag=False, cm_ref=None, tq=None):
    if diag is True:
        s = s + (cm_ref[rows] if cm_ref is not None else _causal_bias(s.shape[0], s.shape[1], tq))
    elif diag is not False:
        s = s + diag.astype(F32) * cm_ref[rows]
    buf[0][rows] = s
    buf[1][rows] = _row_max(s)


def _with_ones(v):
    return jnp.concatenate([v, jnp.ones((v.shape[0], LANES), v.dtype)], axis=1)


def _flash_consume(buf, v, m_ref, acc_ref, mask_tq=None, rows=slice(None)):
    s = buf[0][rows]
    m_cur = buf[1][rows]
    if mask_tq is not None:
        s = s + _causal_bias(s.shape[0], s.shape[1], mask_tq)
        m_cur = _row_max(s)
    m_old = m_ref[rows]
    m_new = jnp.maximum(m_old, m_cur)
    alpha = jnp.exp2(m_old - m_new)
    p = jnp.exp2(s - jnp.tile(m_new, (1, s.shape[1] // LANES))).astype(CDT)
    acc_ref[rows] = jnp.tile(alpha, (1, 2)) * acc_ref[rows] + _dot(p, _with_ones(v))
    m_ref[rows] = m_new


def _flash_result(acc):
    return acc[:, :LANES] / acc[:, LANES:]


def _causal_schedule(nq):
    ent = [(qi, ki, int(ki == qi)) for qi in range(nq) for ki in range(qi + 1)]
    n = len(ent)
    a = np.asarray(ent + [ent[-1]] * 2, np.int32)
    return n, tuple(jnp.asarray(a[:, i]) for i in range(3))


def _flash_stream(n, sched, base, produce, consume, finish, buf_a, buf_b, mask_at):
    qt, kt, lt = sched
    assert mask_at in ("produce", "consume", "mixed")
    always, never = mask_at == "produce", mask_at == "consume"

    def step(cur, nxt, t, diag, produce_diag, masked_already):
        if nxt is not None:
            produce(nxt, qt[base + t + 1], kt[base + t + 1], produce_diag)
        consume(cur, kt[base + t], diag and not masked_already)
        if diag:
            finish(qt[base + t])

    produce(buf_a, qt[base], kt[base], always)

    def pair(j, c):
        t = 2 * j
        l0, l1 = lt[base + t], lt[base + t + 1]
        for d0 in (False, True):
            for d1 in (False, True):
                @pl.when(((l0 != 0) == d0) & ((l1 != 0) == d1))
                def _():
                    step(buf_a, buf_b, t, d0, d1 and not never, always)
                    step(buf_b, buf_a, t + 1, d1, lt[base + t + 2] if always else False, not never)
        return c

    lax.fori_loop(0, n // 2, pair, 0)

    def tail():
        step(buf_a, None, n - 1, True, None, always)

    if isinstance(n, int):
        if n % 2 == 1:
            tail()
    else:
        pl.when(n % 2 == 1)(tail)


def _tile(ref, i, t):
    return ref[0, pl.ds(pl.multiple_of(i * t, t), t), :]


def _diff_attn_kernel(qt_ref, kt_ref, lt_ref, slopes_ref, lam_ref, g_ref, q_ref, k_ref, v_ref, o_ref,
                      m_ref, acc_ref, sa_ref, sb_ref, ma_ref, mb_ref, *, tq, n, lam_init):
    slope = slopes_ref[pl.program_id(1)]
    _flash_reset(m_ref, acc_ref)
    col = lax.broadcasted_iota(jnp.int32, (1, tq), 1).astype(F32)
    lane = lax.broadcasted_iota(jnp.int32, (tq, LANES), 1)
    lf = lam_ref[...]
    lam = (jnp.exp(jnp.sum(lf[0:1] * lf[1:2], axis=-1, keepdims=True))
           - jnp.exp(jnp.sum(lf[2:3] * lf[3:4], axis=-1, keepdims=True)) + lam_init)

    def produce(buf, qi, ki, diag):
        q = _tile(q_ref, qi, tq)
        zero = jnp.zeros_like(q)
        qq = jnp.concatenate([jnp.where(lane < DIFF_DH, q, zero), jnp.where(lane >= DIFF_DH, q, zero)], axis=0)
        s = _dot_nt(qq, _tile(k_ref, ki, tq))
        _put_logits(buf, s + slope * (col + ((ki - qi) * tq).astype(F32)), diag=diag, tq=tq)

    def consume(buf, ki, diag):
        _flash_consume(buf, _tile(v_ref, ki, tq), m_ref, acc_ref, tq if diag else None)

    def finish(qi):
        o = _flash_result(acc_ref[...])
        d = o[0:tq] - lam * o[tq:2 * tq]
        o_ref[0, pl.ds(pl.multiple_of(qi * tq, tq), tq), :] = (
            _rms(d, g_ref[...]) * (1.0 - lam_init)).astype(o_ref.dtype)
        _flash_reset(m_ref, acc_ref)

    _flash_stream(n, (qt_ref, kt_ref, lt_ref), 0, produce, consume, finish, (sa_ref, ma_ref), (sb_ref, mb_ref),
                  mask_at="consume")


_SMEM = pl.BlockSpec(memory_space=pltpu.SMEM)


def _diff_attention(proj3, diff_lambda, subln, lam_init):
    b, s, _ = proj3.shape
    tq = min(TQ_DENSE, s)
    dv = 2 * DIFF_DH
    n, sched = _causal_schedule(s // tq)
    kern = functools.partial(_diff_attn_kernel, tq=tq, n=n, lam_init=lam_init)
    return pl.pallas_call(
        kern,
        grid=(b, DIFF_HEADS),
        in_specs=[
            _SMEM, _SMEM, _SMEM, _SMEM,
            pl.BlockSpec((4, DIFF_DH), lambda bi, h: (0, 0)),
            pl.BlockSpec((1, dv), lambda bi, h: (0, 0)),
            pl.BlockSpec((1, s, LANES), lambda bi, h: (bi, 0, PB_AQ + h)),
            pl.BlockSpec((1, s, LANES), lambda bi, h: (bi, 0, PB_AK + h)),
            pl.BlockSpec((1, s, LANES), lambda bi, h: (bi, 0, PB_AV + h)),
        ],
        out_specs=pl.BlockSpec((1, s, dv), lambda bi, h: (bi, 0, h)),
        out_shape=jax.ShapeDtypeStruct((b, s, DIFF_HEADS * dv), CDT),
        scratch_shapes=_flash_scratch(2 * tq, tq),
        compiler_params=_cparams(("parallel", "parallel")),
        name="diff_attention",
    )(*sched, jnp.asarray(_alibi_slopes(DIFF_HEADS)), diff_lambda, subln.reshape(1, dv), proj3, proj3, proj3)


def _mla_prep_kernel(cq_ref, ckv_ref, kr_ref, krs_ref, gq_ref, gkv_ref, wqm_ref, wqs_ref, wk_ref, wv_ref,
                     cosq_ref, sinq_ref, cosk_ref, sink_ref, q_ref, k_ref, v_ref):
    hq = _rms(cq_ref[0].astype(F32), gq_ref[...]).astype(CDT)
    qm = _dot(hq, wqm_ref[...])
    qs = _dot(hq, wqs_ref[...])
    cosq, sinq = cosq_ref[...], sinq_ref[...]
    hw = 2 * LANES
    for h in range(MLA_HEADS):
        sl = slice(h * hw, (h + 1) * hw)
        q_ref[0, :, sl] = (qm[:, sl] * cosq + qs[:, sl] * sinq).astype(q_ref.dtype)
    hkv = _rms(ckv_ref[0].astype(F32), gkv_ref[...]).astype(CDT)
    kn = _dot(hkv, wk_ref[...])
    v_ref[0] = _dot(hkv, wv_ref[...]).astype(v_ref.dtype)
    kpe = (kr_ref[0].astype(F32) * cosk_ref[...] + krs_ref[0].astype(F32) * sink_ref[...]).astype(k_ref.dtype)
    for h in range(MLA_HEADS):
        k_ref[0, :, h * hw:h * hw + LANES] = kn[:, h * LANES:(h + 1) * LANES].astype(k_ref.dtype)
        k_ref[0, :, h * hw + LANES:(h + 1) * hw] = kpe


def _mla_prep(proj3, gq, gkv, wqm, wqs, wk, wv, tabs):
    b, s, _ = proj3.shape
    tm = min(TM_PROJ, s)
    hw = 2 * LANES
    cosq, sinq, cosk, sink = tabs
    const = lambda shape: pl.BlockSpec(shape, lambda bi, i: (0,) * len(shape))
    return pl.pallas_call(
        _mla_prep_kernel,
        grid=(b, s // tm),
        in_specs=[
            pl.BlockSpec((1, tm, MLA_Q_LORA), lambda bi, i: (bi, i, PB_BCQ // 2)),
            pl.BlockSpec((1, tm, MLA_KV_LORA), lambda bi, i: (bi, i, PB_BCKV // 2)),
            pl.BlockSpec((1, tm, LANES), lambda bi, i: (bi, i, PB_BKR)),
            pl.BlockSpec((1, tm, LANES), lambda bi, i: (bi, i, PB_BKRS)),
            const((1, MLA_Q_LORA)), const((1, MLA_KV_LORA)),
            const((MLA_Q_LORA, MLA_HEADS * hw)), const((MLA_Q_LORA, MLA_HEADS * hw)),
            const((MLA_KV_LORA, MLA_HEADS * MLA_NOPE)), const((MLA_KV_LORA, MLA_HEADS * MLA_VDIM)),
            pl.BlockSpec((tm, hw), lambda bi, i: (i, 0)), pl.BlockSpec((tm, hw), lambda bi, i: (i, 0)),
            pl.BlockSpec((tm, LANES), lambda bi, i: (i, 0)), pl.BlockSpec((tm, LANES), lambda bi, i: (i, 0)),
        ],
        out_specs=[
            pl.BlockSpec((1, tm, MLA_HEADS * hw), lambda bi, i: (bi, i, 0)),
            pl.BlockSpec((1, tm, MLA_HEADS * hw), lambda bi, i: (bi, i, 0)),
            pl.BlockSpec((1, tm, MLA_HEADS * MLA_VDIM), lambda bi, i: (bi, i, 0)),
        ],
        out_shape=[
            jax.ShapeDtypeStruct((b, s, MLA_HEADS * hw), CDT),
            jax.ShapeDtypeStruct((b, s, MLA_HEADS * hw), CDT),
            jax.ShapeDtypeStruct((b, s, MLA_HEADS * MLA_VDIM), CDT),
        ],
        compiler_params=_cparams(("parallel", "parallel")),
        name="mla_prep",
    )(proj3, proj3, proj3, proj3, gq.reshape(1, -1), gkv.reshape(1, -1), wqm, wqs, wk, wv,
      cosq, sinq, cosk, sink)


def _plain_attn_kernel(qt_ref, kt_ref, lt_ref, q_ref, k_ref, v_ref, o_ref,
                       m_ref, acc_ref, sa_ref, sb_ref, ma_ref, mb_ref, *, tq, n, hp, dk, dv):
    _flash_reset(m_ref, acc_ref)
    heads = [(slice(h * tq, (h + 1) * tq), slice(h * dk, (h + 1) * dk), slice(h * dv, (h + 1) * dv))
             for h in range(hp)]

    def produce(buf, qi, ki, diag):
        q, k = _tile(q_ref, qi, tq), _tile(k_ref, ki, tq)
        for rows, kcols, _ in heads:
            _put_logits(buf, _dot_nt(q[:, kcols], k[:, kcols]), rows, diag, tq=tq)

    def consume(buf, ki, diag):
        v = _tile(v_ref, ki, tq)
        for rows, _, vcols in heads:
            _flash_consume(buf, v[:, vcols], m_ref, acc_ref, tq if diag else None, rows)

    def finish(qi):
        for rows, _, vcols in heads:
            o_ref[0, pl.ds(pl.multiple_of(qi * tq, tq), tq), vcols] = _flash_result(acc_ref[rows]).astype(o_ref.dtype)
        _flash_reset(m_ref, acc_ref)

    _flash_stream(n, (qt_ref, kt_ref, lt_ref), 0, produce, consume, finish, (sa_ref, ma_ref), (sb_ref, mb_ref),
                  mask_at="mixed")


def _mla_attention(qc, kc, v):
    b, s, _ = qc.shape
    tq = min(TQ_DENSE, s)
    hw = 2 * LANES
    hp = MLA_HP
    n, sched = _causal_schedule(s // tq)
    return pl.pallas_call(
        functools.partial(_plain_attn_kernel, tq=tq, n=n, hp=hp, dk=hw, dv=MLA_VDIM),
        grid=(b, MLA_HEADS // hp),
        in_specs=[
            _SMEM, _SMEM, _SMEM,
            pl.BlockSpec((1, s, hp * hw), lambda bi, h: (bi, 0, h)),
            pl.BlockSpec((1, s, hp * hw), lambda bi, h: (bi, 0, h)),
            pl.BlockSpec((1, s, hp * MLA_VDIM), lambda bi, h: (bi, 0, h)),
        ],
        out_specs=pl.BlockSpec((1, s, hp * MLA_VDIM), lambda bi, h: (bi, 0, h)),
        out_shape=jax.ShapeDtypeStruct((b, s, MLA_HEADS * MLA_VDIM), CDT),
        scratch_shapes=_flash_scratch(hp * tq, tq),
        compiler_params=_cparams(("parallel", "parallel")),
        name="mla_attention",
    )(*sched, qc, kc, v)


def _fox_attn_kernel(qt_ref, kt_ref, lt_ref, c_ref, q_ref, k_ref, v_ref, o_ref,
                     m_ref, acc_ref, sa_ref, sb_ref, ma_ref, mb_ref, *, tq, n, hp):
    _flash_reset(m_ref, acc_ref)
    heads = [(slice(h * tq, (h + 1) * tq), slice(h * FOX_DH, (h + 1) * FOX_DH)) for h in range(hp)]

    def produce(buf, qi, ki, diag):
        q, k = _tile(q_ref, qi, tq), _tile(k_ref, ki, tq)
        for h, (rows, cols) in enumerate(heads):
            cbase = c_ref[0, h, pl.ds(qi, 1), :][:, 0:1]
            s = _dot_nt(q[:, cols], k[:, cols]) + LOG2E * (cbase - c_ref[0, h, pl.ds(ki, 1), :])
            _put_logits(buf, s, rows, diag, tq=tq)

    def consume(buf, ki, diag):
        v = _tile(v_ref, ki, tq)
        for rows, cols in heads:
            _flash_consume(buf, v[:, cols], m_ref, acc_ref, tq if diag else None, rows)

    def finish(qi):
        for rows, cols in heads:
            o_ref[0, pl.ds(pl.multiple_of(qi * tq, tq), tq), cols] = _flash_result(acc_ref[rows]).astype(o_ref.dtype)
        _flash_reset(m_ref, acc_ref)

    _flash_stream(n, (qt_ref, kt_ref, lt_ref), 0, produce, consume, finish, (sa_ref, ma_ref), (sb_ref, mb_ref),
                  mask_at="mixed")


def _fox_attention(proj3, c4):
    b, s, _ = proj3.shape
    tq = min(TQ_DENSE, s)
    nk = s // tq
    hp = FOX_HP
    w = hp * FOX_DH
    n, sched = _causal_schedule(nk)
    return pl.pallas_call(
        functools.partial(_fox_attn_kernel, tq=tq, n=n, hp=hp),
        grid=(b, FOX_HEADS // hp),
        in_specs=[
            _SMEM, _SMEM, _SMEM,
            pl.BlockSpec((1, hp, nk, tq), lambda bi, h: (bi, h, 0, 0)),
            pl.BlockSpec((1, s, w), lambda bi, h: (bi, 0, PB_CQ // hp + h)),
            pl.BlockSpec((1, s, w), lambda bi, h: (bi, 0, PB_CK // hp + h)),
            pl.BlockSpec((1, s, w), lambda bi, h: (bi, 0, PB_CV // hp + h)),
        ],
        out_specs=pl.BlockSpec((1, s, w), lambda bi, h: (bi, 0, h)),
        out_shape=jax.ShapeDtypeStruct((b, s, FOX_HEADS * FOX_DH), CDT),
        scratch_shapes=_flash_scratch(hp * tq, tq),
        compiler_params=_cparams(("parallel", "parallel")),
        name="fox_attention",
    )(*sched, c4.reshape(b, FOX_HEADS, nk, tq), proj3, proj3, proj3)


def _nsa_compress_kernel(x_ref, w1a_ref, w1b_ref, pea_ref, peb_ref, w2_ref, o_ref):
    x = x_ref[0]
    n = x.shape[0]
    pa = _dot(x, w1a_ref[...])
    pb = _dot(x, w1b_ref[...])
    pe = _dot(pea_ref[...], w1a_ref[...]) + _dot(peb_ref[...], w1b_ref[...])
    hid = pa + pltpu.roll(pb, n - 1, axis=0) + pe[0:1]
    act = 0.5 * hid * (1.0 + jnp.tanh(math.sqrt(2.0 / math.pi) * (hid + 0.044715 * hid * hid * hid)))
    o_ref[0] = _dot(act.astype(CDT), w2_ref[...]).astype(o_ref.dtype)


def _nsa_compress(xc, w1a, w1b, pea, peb, w2):
    b, n, kdim = xc.shape
    hdim = w1a.shape[1]
    const = lambda shape: pl.BlockSpec(shape, lambda bi: (0,) * len(shape))
    return pl.pallas_call(
        _nsa_compress_kernel,
        grid=(b,),
        in_specs=[pl.BlockSpec((1, n, kdim), lambda bi: (bi, 0, 0)),
                  const((kdim, hdim)), const((kdim, hdim)), const((8, kdim)), const((8, kdim)),
                  const((hdim, w2.shape[1]))],
        out_specs=pl.BlockSpec((1, n, w2.shape[1]), lambda bi: (bi, 0, 0)),
        out_shape=jax.ShapeDtypeStruct((b, n, w2.shape[1]), CDT),
        compiler_params=_cparams(("parallel",)),
        name="nsa_compress",
    )(xc, w1a, w1b, pea, peb, w2)


def _nsa_cmp_kernel(slopes_ref, q_ref, kv_ref, oc_ref, sb_ref, used_ref, *, tq, n_topk):
    qi = pl.program_id(1)
    nblk = kv_ref.shape[1]
    q0 = qi * tq
    rowpos = q0 + lax.broadcasted_iota(jnp.int32, (tq, 1), 0)
    cmp_end = lax.broadcasted_iota(jnp.int32, (1, nblk), 1) * CMP_STRIDE + (CMP_LEN - 1)
    negmask = jnp.where(rowpos >= cmp_end, 0.0, NEG)
    end_rel = (cmp_end - q0).astype(F32)
    lane = lax.broadcasted_iota(jnp.int32, (tq, LANES), 1)
    low = lane < NSA_DH
    nn = lax.broadcasted_iota(jnp.int32, (NSA_DH, nblk), 1) * CMP_STRIDE
    jj = lax.broadcasted_iota(jnp.int32, (NSA_DH, nblk), 0) * SLC_LEN
    ovt = (jnp.maximum(jnp.minimum(nn + CMP_LEN, jj + SLC_LEN) - jnp.maximum(nn, jj), 0).astype(F32)
           * (1.0 / CMP_LEN)).astype(CDT)
    jt = lax.broadcasted_iota(jnp.int32, (NSA_DH, tq), 0).astype(F32)
    blk = ((q0 + lax.broadcasted_iota(jnp.int32, (1, tq), 1)) >> SLC_SHIFT).astype(F32)
    fixed = (jt == 0.0) | (jt == blk) | (jt == blk - 1.0)
    beyond = jt > blk
    row_ok = rowpos >= CMP_LEN - 1
    outs = []
    bias = []
    for g in range(NSA_GROUPS):
        kc = kv_ref[0, :, g * LANES:(g + 1) * LANES]
        vc = kv_ref[0, :, (NSA_GROUPS + g) * LANES:(NSA_GROUPS + g + 1) * LANES]
        psum = jnp.zeros((tq, nblk), F32)
        mine = low if g == 0 else jnp.logical_not(low)
        zero = jnp.zeros((tq, LANES), q_ref.dtype)
        qs = jnp.concatenate([jnp.where(mine, q_ref[0, :, j * LANES:(j + 1) * LANES], zero)
                              for j in range(NSA_HPG)], axis=0)
        s_all = _dot_nt(qs, kc)
        ps = []
        for j in range(NSA_HPG):
            s = s_all[j * tq:(j + 1) * tq] + slopes_ref[g * NSA_HPG + j] * end_rel + negmask
            e = jnp.exp2(s - jnp.max(s, axis=-1, keepdims=True))
            den = jnp.sum(e, axis=-1, keepdims=True)
            p = e * jnp.where(row_ok, 1.0 / den, 0.0)
            psum = psum + p
            ps.append(p.astype(CDT))
        o_all = _dot(jnp.concatenate(ps, axis=0), vc)
        outs.extend(o_all[j * tq:(j + 1) * tq] for j in range(NSA_HPG))
        hi = psum.astype(CDT)
        lo = (psum - hi.astype(F32)).astype(CDT)
        imp = _dot_nt(ovt, hi) + _dot_nt(ovt, lo)
        imp = jnp.where(fixed, -jnp.inf, jnp.where(beyond, NEG_INF, imp))
        sbt = jnp.where(fixed, 0.0, NEG)
        for _ in range(n_topk - 3):
            mx = jnp.max(imp, axis=0, keepdims=True)
            idx = jnp.min(jnp.where(imp == mx, jt, float(LANES)), axis=0, keepdims=True)
            hit = jt == idx
            sbt = jnp.where(hit, 0.0, sbt)
            imp = jnp.where(hit, -jnp.inf, imp)
        bias.append(sbt)
    sb = jnp.concatenate([bias[1], bias[0]], axis=0).T
    sb_ref[0] = sb.astype(sb_ref.dtype)
    used = jnp.max(jnp.where(sb == 0.0, 1.0, 0.0), axis=0, keepdims=True)
    used_ref[0, 0] = jnp.broadcast_to(used, used_ref.shape[2:])
    for blk_i in range(NSA_HEADS // 2):
        oc_ref[0, :, blk_i * LANES:(blk_i + 1) * LANES] = jnp.where(
            low, outs[2 * blk_i], outs[2 * blk_i + 1]).astype(oc_ref.dtype)


def _nsa_cmp_select(proj3, kvc, n_topk):
    assert n_topk >= 3, "the three always-selected blocks must fit in the top-k budget"
    b, s, _ = proj3.shape
    tq = min(TQ_NSA, s)
    nblk = kvc.shape[1]
    return pl.pallas_call(
        functools.partial(_nsa_cmp_kernel, tq=tq, n_topk=n_topk),
        grid=(b, s // tq),
        in_specs=[
            pl.BlockSpec(memory_space=pltpu.SMEM),
            pl.BlockSpec((1, tq, 4 * LANES), lambda bi, qi: (bi, qi, PB_DQ // 4)),
            pl.BlockSpec((1, nblk, kvc.shape[2]), lambda bi, qi: (bi, 0, 0)),
        ],
        out_specs=[
            pl.BlockSpec((1, tq, NSA_HEADS * NSA_DH), lambda bi, qi: (bi, qi, 0)),
            pl.BlockSpec((1, tq, LANES), lambda bi, qi: (bi, qi, 0)),
            pl.BlockSpec((1, 1, 8, LANES), lambda bi, qi: (bi, qi, 0, 0)),
        ],
        out_shape=[jax.ShapeDtypeStruct((b, s, NSA_HEADS * NSA_DH), CDT),
                   jax.ShapeDtypeStruct((b, s, LANES), CDT),
                   jax.ShapeDtypeStruct((b, s // tq, 8, LANES), F32)],
        compiler_params=_cparams(("parallel", "parallel")),
        name="nsa_cmp_select",
    )(jnp.asarray(_alibi_slopes(NSA_HEADS)), proj3, kvc)


def _compact_heads(heads, mine, low):
    both = [jnp.where(mine, a, pltpu.roll(a, NSA_DH, axis=1)) for a in heads]
    out = [jnp.where(low, both[2 * jj], both[2 * jj + 1]) for jj in range(NSA_HPG // 2)]
    return jnp.concatenate(out, axis=1)


def _nsa_win_kernel(slopes_ref, q_ref, kp_ref, kc_ref, vp_ref, vc_ref, o_ref, *, tq):
    qi = pl.program_id(1)
    lane = lax.broadcasted_iota(jnp.int32, (tq, LANES), 1)
    low = lane < NSA_DH
    r = lax.broadcasted_iota(jnp.int32, (tq, tq), 0)
    c = lax.broadcasted_iota(jnp.int32, (tq, tq), 1)
    own = c <= r
    ndist = jnp.where(own, c - r, c - r - tq).astype(F32)
    own_f = jnp.where(own, 1.0, 0.0).astype(CDT)
    prev_pen = jnp.where(qi > 0, 0.0, NEG)
    q = q_ref[0]
    zero = jnp.zeros((tq, LANES), q.dtype)
    mine = (low, jnp.logical_not(low))
    qs = jnp.concatenate([jnp.where(mine[g], q[:, j * LANES:(j + 1) * LANES], zero)
                          for g in range(NSA_GROUPS) for j in range(NSA_HPG)], axis=0)
    s_own, s_prev = _dot_nt(qs, kc_ref[0]), _dot_nt(qs, kp_ref[0])
    ps = []
    for hd in range(NSA_HEADS):
        rows = slice(hd * tq, (hd + 1) * tq)
        s = jnp.where(own, s_own[rows], s_prev[rows] + prev_pen) + slopes_ref[hd] * ndist
        ps.append(jnp.exp2(s - jnp.max(s, axis=-1, keepdims=True)).astype(CDT))
    p = jnp.concatenate(ps, axis=0)
    p_own = p * jnp.tile(own_f, (NSA_HEADS, 1))
    o = _flash_result(_dot(p_own, _with_ones(vc_ref[0])) + _dot(p - p_own, _with_ones(vp_ref[0])))
    for g in range(NSA_GROUPS):
        heads = [o[(g * NSA_HPG + j) * tq:(g * NSA_HPG + j + 1) * tq] for j in range(NSA_HPG)]
        w = NSA_HPG * NSA_DH
        o_ref[0, :, g * w:(g + 1) * w] = _compact_heads(heads, mine[g], low).astype(o_ref.dtype)


def _nsa_window(proj3):
    b, s, _ = proj3.shape
    tq = WINDOW
    return pl.pallas_call(
        functools.partial(_nsa_win_kernel, tq=tq),
        grid=(b, s // tq),
        in_specs=[
            pl.BlockSpec(memory_space=pltpu.SMEM),
            pl.BlockSpec((1, tq, 4 * LANES), lambda bi, qi: (bi, qi, PB_DQ // 4)),
            pl.BlockSpec((1, tq, LANES), lambda bi, qi: (bi, jnp.maximum(qi - 1, 0), PB_WIN_K)),
            pl.BlockSpec((1, tq, LANES), lambda bi, qi: (bi, qi, PB_WIN_K)),
            pl.BlockSpec((1, tq, LANES), lambda bi, qi: (bi, jnp.maximum(qi - 1, 0), PB_WIN_V)),
            pl.BlockSpec((1, tq, LANES), lambda bi, qi: (bi, qi, PB_WIN_V)),
        ],
        out_specs=pl.BlockSpec((1, tq, NSA_HEADS * NSA_DH), lambda bi, qi: (bi, qi, 0)),
        out_shape=jax.ShapeDtypeStruct((b, s, NSA_HEADS * NSA_DH), CDT),
        compiler_params=_cparams(("parallel", "parallel")),
        name="nsa_window",
    )(jnp.asarray(_alibi_slopes(NSA_HEADS)), proj3, proj3, proj3, proj3, proj3)


def _nsa_sel_kernel(cnt_ref, qt_ref, kt_ref, lt_ref, slopes_ref, q_ref, sb_ref, k_ref, v_ref, oc_ref, ow_ref, gl_ref,
                    e_ref, o_ref, m_ref, acc_ref, sa_ref, sb2_ref, ma_ref, mb_ref, cm_ref, *, tq, rows_per_problem):
    g = pl.program_id(1)
    w = NSA_HPG * NSA_DH
    lane = lax.broadcasted_iota(jnp.int32, (tq, LANES), 1)
    low = lane < NSA_DH
    mine = (lane >> HALF_SHIFT) == g
    _flash_begin(m_ref, acc_ref, cm_ref, tq)
    col = lax.broadcasted_iota(jnp.int32, (1, tq), 1).astype(F32)
    jl = lane & (NSA_DH - 1)
    krow = lax.broadcasted_iota(jnp.int32, (tq, LANES), 0)

    def produce(buf, qi, ki, diag):
        q = _tile(q_ref, qi, tq)
        sb = _tile(sb_ref, qi, tq)
        qa = jnp.concatenate([jnp.where(mine, q[:, j * LANES:(j + 1) * LANES], sb) for j in range(NSA_HPG)], axis=0)
        k = _tile(k_ref, ki, tq)
        onehot = jnp.where(((ki * tq + krow) >> SLC_SHIFT) == jl, 1.0, 0.0).astype(k.dtype)
        s_all = _dot_nt(qa, jnp.where(mine, k, onehot))
        rel = ((ki - qi) * tq).astype(F32)
        for j in range(NSA_HPG):
            rows = slice(j * tq, (j + 1) * tq)
            _put_logits(buf, s_all[rows] + slopes_ref[g * NSA_HPG + j] * (col + rel), rows, diag, cm_ref)

    def consume(buf, ki, diag):
        _flash_consume(buf, _tile(v_ref, ki, tq), m_ref, acc_ref, tq if diag else None)

    def finish(qi):
        o = _flash_result(acc_ref[...])
        o_s = _compact_heads([o[j * tq:(j + 1) * tq] for j in range(NSA_HPG)], mine, low)
        gates = _split_dot(_sigmoid(_tile(gl_ref, qi, tq)), e_ref[0])
        y = (gates[:, 0:w] * _tile(oc_ref, qi, tq).astype(F32) + gates[:, w:2 * w] * o_s
             + gates[:, 2 * w:3 * w] * _tile(ow_ref, qi, tq).astype(F32))
        o_ref[0, pl.ds(pl.multiple_of(qi * tq, tq), tq), :] = y.astype(o_ref.dtype)
        _flash_reset(m_ref, acc_ref)

    prob = pl.program_id(0) * NSA_GROUPS + g
    _flash_stream(cnt_ref[prob], (qt_ref, kt_ref, lt_ref), prob * rows_per_problem, produce, consume, finish,
                  (sa_ref, ma_ref), (sb2_ref, mb_ref), mask_at="mixed")


def _nsa_selected(proj3, sbias, used, o_c, o_w, small3, expand):
    b, s, _ = proj3.shape
    tq = min(TQ_NSA, s)
    nq = s // tq
    w = NSA_HPG * NSA_DH
    u = used[:, :, 0, :].reshape(b, nq, NSA_GROUPS, NSA_DH)[:, :, ::-1, :nq * (tq // SLC_LEN)]
    flags = (u.reshape(b, nq, NSA_GROUPS, nq, tq // SLC_LEN).max(axis=-1) > 0.0).astype(jnp.int32)
    flags = flags.transpose(0, 2, 1, 3)
    qt = jnp.arange(nq, dtype=jnp.int32)
    need = jnp.where(qt[None, :] < qt[:, None], flags, (qt[None, :] == qt[:, None]).astype(jnp.int32))
    need = need.reshape(b, NSA_GROUPS, nq * nq)
    cnt = need.sum(axis=-1).astype(jnp.int32)
    order = jnp.argsort(1 - need, axis=-1, stable=True).astype(jnp.int32)
    order = jnp.pad(order, ((0, 0), (0, 0), (0, 2)))
    rows = nq * nq + 2
    sched = (order // nq, order % nq, (order // nq == order % nq).astype(jnp.int32))
    return pl.pallas_call(
        functools.partial(_nsa_sel_kernel, tq=tq, rows_per_problem=rows),
        grid=(b, NSA_GROUPS),
        in_specs=[
            _SMEM, _SMEM, _SMEM, _SMEM, _SMEM,
            pl.BlockSpec((1, s, 4 * LANES), lambda bi, g: (bi, 0, PB_DQ // 4)),
            pl.BlockSpec((1, s, LANES), lambda bi, g: (bi, 0, 0)),
            pl.BlockSpec((1, s, LANES), lambda bi, g: (bi, 0, PB_SEL_K)),
            pl.BlockSpec((1, s, LANES), lambda bi, g: (bi, 0, PB_SEL_V)),
            pl.BlockSpec((1, s, w), lambda bi, g: (bi, 0, g)),
            pl.BlockSpec((1, s, w), lambda bi, g: (bi, 0, g)),
            pl.BlockSpec((1, s, LANES), lambda bi, g: (bi, 0, 0)),
            pl.BlockSpec((1, LANES, 3 * w), lambda bi, g: (g, 0, 0)),
        ],
        out_specs=pl.BlockSpec((1, s, w), lambda bi, g: (bi, 0, g)),
        out_shape=jax.ShapeDtypeStruct((b, s, NSA_HEADS * NSA_DH), CDT),
        scratch_shapes=_flash_scratch(NSA_HPG * tq, tq, mask_scratch=True),
        compiler_params=_cparams(("parallel", "parallel")),
        name="nsa_selected",
    )(cnt.reshape(-1), *[t.reshape(-1) for t in sched], jnp.asarray(_alibi_slopes(NSA_HEADS)),
      proj3, sbias, proj3, proj3, o_c, o_w, small3, expand)


def _merge_kernel(ya_ref, yb_ref, yc_ref, yd_ref, ga_ref, gb_ref, gc_ref, gd_ref, wb_ref, wo_ref, x_ref, o_ref):
    merged = None
    for n, (y_ref, g_ref) in enumerate(((ya_ref, ga_ref), (yb_ref, gb_ref), (yc_ref, gc_ref), (yd_ref, gd_ref))):
        t = _sigmoid(g_ref[...].astype(F32)) * _dot(y_ref[...], wb_ref[n])
        merged = t if merged is None else merged + t
    o_ref[...] = x_ref[...] + _dot(merged.astype(CDT), wo_ref[...])


def _merge(ys, proj2, wb, wo, x2, layer):
    t, d = x2.shape
    tm = min(TM_ROWS, t)
    gate_blk = PB_GATE * LANES // d
    yspec = pl.BlockSpec((tm, BRANCH_WIDTH), lambda i: (i, 0))
    gspecs = [pl.BlockSpec((tm, d), functools.partial(lambda i, n: (i, gate_blk + n), n=n)) for n in range(N_BRANCH)]
    return pl.pallas_call(
        _merge_kernel,
        grid=(t // tm,),
        in_specs=[yspec] * N_BRANCH + gspecs + [
            pl.BlockSpec((None, N_BRANCH, BRANCH_WIDTH, d), lambda i: (layer, 0, 0, 0)),
            pl.BlockSpec((None, d, d), lambda i: (layer, 0, 0)),
            pl.BlockSpec((tm, d), lambda i: (i, 0)),
        ],
        out_specs=pl.BlockSpec((tm, d), lambda i: (i, 0)),
        out_shape=jax.ShapeDtypeStruct((t, d), F32),
        compiler_params=_cparams(("parallel",)),
        name="merge",
    )(*ys, proj2, proj2, proj2, proj2, wb, wo, x2)


HALO = 16


def _ffn_kernel(x_ref, xh_ref, g_ref, wu_ref, cw_ref, cb_ref, wd_ref, gf_ref, o_ref, he_ref, u_ref, act_ref,
                *, tm, fc, final):
    i = pl.program_id(1)
    x = x_ref[0]
    g = g_ref[...]
    xh = xh_ref[0] * (i > 0).astype(F32)
    he_ref[0:HALO] = _rms(xh, g).astype(CDT)
    he_ref[HALO:HALO + tm] = _rms(x, g).astype(CDT)
    he = he_ref[...]
    for c in range(D_FF // fc):
        outs = []
        for half in range(2):
            ub = u_ref.at[c % 2, half]
            lo = half * D_FF + c * fc
            ub[...] = _dot(he, wu_ref[:, lo:lo + fc])
            conv = cb_ref[:, lo:lo + fc]
            for kk in range(CONV_WIDTH):
                off = HALO - (CONV_WIDTH - 1) + kk
                conv = conv + cw_ref[kk:kk + 1, lo:lo + fc] * ub[off:off + tm, :]
            outs.append(conv)
        a, gg = outs
        act_ref[:, c * fc:(c + 1) * fc] = (a * _sigmoid(a) * gg).astype(CDT)
    y = x + _dot(act_ref[...], wd_ref[...])
    if final:
        y = _rms(y, gf_ref[...])
    o_ref[0] = y


def _ffn(x3, g, wu, cw, cb, wd, gf, layer, final):
    b, s, d = x3.shape
    tm = min(TM_ROWS, s)
    fc = FFN_CHUNK
    assert D_FF % fc == 0
    const = lambda shape: pl.BlockSpec(shape, lambda bi, i: (0,) * len(shape), pipeline_mode=pl.Buffered(1))
    stacked = lambda shape: pl.BlockSpec((None,) + shape, lambda bi, i: (layer,) + (0,) * len(shape),
                                         pipeline_mode=pl.Buffered(1))
    return pl.pallas_call(
        functools.partial(_ffn_kernel, tm=tm, fc=fc, final=final),
        grid=(b, s // tm),
        in_specs=[
            pl.BlockSpec((1, tm, d), lambda bi, i: (bi, i, 0)),
            pl.BlockSpec((1, HALO, d), lambda bi, i: (bi, jnp.maximum(i * (tm // HALO) - 1, 0), 0)),
            const((1, d)), stacked((d, 2 * D_FF)), const((CONV_WIDTH, 2 * D_FF)), const((1, 2 * D_FF)),
            stacked((D_FF, d)), const((1, d)),
        ],
        out_specs=pl.BlockSpec((1, tm, d), lambda bi, i: (bi, i, 0)),
        out_shape=jax.ShapeDtypeStruct((b, s, d), F32),
        scratch_shapes=[pltpu.VMEM((tm + HALO, d), CDT), pltpu.VMEM((2, 2, tm + HALO, fc), F32),
                        pltpu.VMEM((tm, D_FF), CDT)],
        compiler_params=_cparams(("parallel", "arbitrary")),
        name="conv_glu_mlp",
    )(x3, x3, g.reshape(1, d), wu, cw, cb.reshape(1, -1), wd, gf.reshape(1, d))


def _w_in_plan():
    widths = (512, 512, 512, MLA_Q_LORA, MLA_KV_LORA, MLA_ROPE, 512, 512, 512, FOX_HEADS,
              512, 768, 3 * NSA_HEADS, N_BRANCH * D_MODEL)
    (a_q, a_k, a_v, b_cq, b_ckv, b_kr, c_q, c_k, c_v, c_f, d_q, d_kv, d_g, gate, _) = np.cumsum((0,) + widths).tolist()
    half = MLA_ROPE // 2

    def run(src, nblocks, scale=1.0):
        return [[(src + i * LANES, LANES, scale)] for i in range(nblocks)]

    blocks = (run(a_q, 4, LOG2E * DIFF_DH ** -0.5) + run(a_k, 4) + run(a_v, 4)
              + run(c_q, 4, LOG2E * FOX_DH ** -0.5) + run(c_k, 4) + run(c_v, 4))
    sd = LOG2E * NSA_DH ** -0.5
    blocks += [[(d_q + j * NSA_DH, NSA_DH, sd), (d_q + (NSA_HPG + j) * NSA_DH, NSA_DH, sd)] for j in range(NSA_HPG)]
    blocks += run(d_kv, 6) + run(b_cq, 2) + run(b_ckv, 2)
    blocks += [[(b_kr, MLA_ROPE, 1.0), None],
               [(b_kr + half, half, -1.0), (b_kr, half, 1.0), None]]
    blocks += run(gate, N_BRANCH * D_MODEL // LANES)
    assert len(blocks) * LANES == N_PROJ
    small = [(c_f, FOX_HEADS, 1.0), (d_g, 3 * NSA_HEADS, 1.0), None]
    return blocks, small


def _w_in_relayout_kernel(w_ref, big_ref, small_ref):
    cols = w_ref.shape[1]
    blocks, small = _w_in_plan()
    for j, pieces in enumerate(blocks):
        row = j * LANES
        for p in pieces:
            if p is None:
                big_ref[row:(j + 1) * LANES, :] = jnp.zeros(((j + 1) * LANES - row, cols), big_ref.dtype)
            else:
                src, n, scale = p
                v = w_ref[src:src + n, :]
                big_ref[row:row + n, :] = (v if scale == 1.0 else v * scale).astype(big_ref.dtype)
                row += n
    (sf, nf, _), (sg, ng, _), _ = small
    r = lax.broadcasted_iota(jnp.int32, (LANES, cols), 0)
    side = jnp.where(r < nf, w_ref[sf:sf + LANES, :],
                     jnp.where(r < nf + ng, w_ref[sg - nf:sg - nf + LANES, :], 0.0))
    small_ref[...] = side.astype(small_ref.dtype)


def _w_in_relayout(w):
    nl, d, n = w.shape
    wt = jnp.transpose(w, (2, 0, 1)).reshape(n, nl * d)
    tc = min(256, d)
    return pl.pallas_call(
        _w_in_relayout_kernel,
        grid=(nl, d // tc),
        in_specs=[pl.BlockSpec((n, tc), lambda l, i: (0, l * (d // tc) + i))],
        out_specs=[pl.BlockSpec((None, N_PROJ, tc), lambda l, i: (l, 0, i)),
                   pl.BlockSpec((None, LANES, tc), lambda l, i: (l, 0, i))],
        out_shape=[jax.ShapeDtypeStruct((nl, N_PROJ, d), CDT), jax.ShapeDtypeStruct((nl, LANES, d), CDT)],
        compiler_params=_cparams(("parallel", "parallel")),
        name="w_in_relayout",
    )(wt)


def _prep_mla(w_uq, w_ukv):
    r = w_uq.shape[0]
    hw = 2 * LANES
    half = MLA_ROPE // 2
    scale = LOG2E * (MLA_NOPE + MLA_ROPE) ** -0.5
    wq = (w_uq * scale).reshape(r, MLA_HEADS, MLA_NOPE + MLA_ROPE)
    nope, t1, t2 = wq[..., :MLA_NOPE], wq[..., MLA_NOPE:MLA_NOPE + half], wq[..., MLA_NOPE + half:]
    zpad = jnp.zeros((r, MLA_HEADS, hw - MLA_NOPE - MLA_ROPE), w_uq.dtype)
    wqm = jnp.concatenate([nope, t1, t2, zpad], axis=-1).reshape(r, MLA_HEADS * hw)
    wqs = jnp.concatenate([jnp.zeros_like(nope), -t2, t1, zpad], axis=-1).reshape(r, MLA_HEADS * hw)
    wkv = w_ukv.reshape(w_ukv.shape[0], MLA_HEADS, MLA_NOPE + MLA_VDIM)
    wk = wkv[..., :MLA_NOPE].reshape(-1, MLA_HEADS * MLA_NOPE)
    wv = wkv[..., MLA_NOPE:].reshape(-1, MLA_HEADS * MLA_VDIM)
    return wqm.astype(CDT), wqs.astype(CDT), wk.astype(CDT), wv.astype(CDT)


def _rope_tables(s):
    half = MLA_ROPE // 2
    inv_freq = ROPE_THETA ** (-jnp.arange(0, MLA_ROPE, 2, dtype=F32) / MLA_ROPE)
    ang = jnp.arange(s, dtype=F32)[:, None] * inv_freq[None, :]
    cos, sin = jnp.cos(ang), jnp.sin(ang)
    z = jnp.zeros((s, LANES - MLA_ROPE), F32)
    cosk = jnp.concatenate([cos, cos, z], axis=1)
    sink = jnp.concatenate([sin, sin, z], axis=1)
    cosq = jnp.concatenate([jnp.ones((s, MLA_NOPE), F32), cosk], axis=1)
    sinq = jnp.concatenate([jnp.zeros((s, MLA_NOPE), F32), sink], axis=1)
    return cosq, sinq, cosk, sink


def _prep_compress(pe, w1, w2):
    eye2 = jnp.eye(2, dtype=F32)
    w1r = w1.reshape(2, CMP_LEN, NSA_DH, CMP_HIDDEN).astype(CDT)
    same = np.eye(2, dtype=bool)
    diag_kg = jnp.asarray(same[:, None, :, None] & same[None, :, None, :])

    def expand(wpart):
        src = wpart.transpose(1, 0, 2, 3)[:, :, None, :, None, None, :]
        t = jnp.where(diag_kg[None, :, :, None, :, :, None], src, jnp.zeros((), CDT))
        return t.reshape(CMP_STRIDE * 4 * NSA_DH, 4 * CMP_HIDDEN)

    w1a, w1b = expand(w1r[:, :CMP_STRIDE]), expand(w1r[:, CMP_STRIDE:])

    def pe_row(p):
        t = jnp.broadcast_to(p.transpose(1, 0, 2)[:, :, None, :], (CMP_STRIDE, 2, NSA_GROUPS, NSA_DH))
        return jnp.pad(t.reshape(1, -1), ((0, 7), (0, 0)))

    pea, peb = pe_row(pe[:, :CMP_STRIDE]), pe_row(pe[:, CMP_STRIDE:])
    w2b = jnp.einsum('khd,kK,gG,u->kghKGud', w2, eye2, eye2, jnp.ones((2,), F32))
    w2b = w2b.reshape(4 * CMP_HIDDEN, 4 * 2 * NSA_DH)
    return w1a.astype(CDT), w1b.astype(CDT), pea.astype(CDT), peb.astype(CDT), w2b.astype(CDT)


def _gate_expand():
    e = np.zeros((NSA_GROUPS, LANES, 3, NSA_HPG, NSA_DH), np.float32)
    for g in range(NSA_GROUPS):
        for j in range(NSA_HPG):
            for br in range(3):
                e[g, SMALL_G + (g * NSA_HPG + j) * 3 + br, br, j, :] = 1.0
    return jnp.asarray(e.reshape(NSA_GROUPS, LANES, 3 * NSA_HPG * NSA_DH)).astype(CDT)


def _token_mixers(x3, l, norm_mix, w_in, diff_lambda, diff_subln, mla_norm_q, mla_w_uq, mla_norm_kv, mla_w_ukv,
                  fox_b_f, nsa_cmp_pe, nsa_cmp_w1, nsa_cmp_w2, w_branch, w_out, rope_tabs):
    b, s, d = x3.shape
    t = b * s
    x2 = x3.reshape(t, d)
    proj, small = _in_proj(x2, norm_mix, *w_in, l)
    proj3 = proj.reshape(b, s, N_PROJ)
    small3 = small.reshape(b, s, LANES)

    lam_init = 0.8 - 0.6 * math.exp(-0.3 * l)
    y_a = _diff_attention(proj3, diff_lambda, diff_subln, lam_init)

    wqm, wqs, wk, wv = _prep_mla(mla_w_uq, mla_w_ukv)
    qc, kc, vv = _mla_prep(proj3, mla_norm_q, mla_norm_kv, wqm, wqs, wk, wv, rope_tabs)
    y_b = _mla_attention(qc, kc, vv)

    cf_rows = small3[:, :, SMALL_F:SMALL_F + FOX_HEADS].transpose(0, 2, 1).reshape(b * FOX_HEADS, s)
    bias_rows = jnp.tile(fox_b_f.astype(F32), b).reshape(b * FOX_HEADS, 1)
    c4 = _fox_cumsum(cf_rows, bias_rows)
    y_c = _fox_attention(proj3, c4)

    w1a, w1b, pea, peb, w2b = _prep_compress(nsa_cmp_pe, nsa_cmp_w1, nsa_cmp_w2)
    xc = proj3[:, :, PB_CMP_K * LANES:(PB_CMP_V + 1) * LANES].reshape(b, s // CMP_STRIDE, CMP_STRIDE * 2 * LANES)
    kvc = _nsa_compress(xc, w1a, w1b, pea, peb, w2b)
    n_topk = min(SLC_TOPK, s // SLC_LEN)
    o_c, sbias, used = _nsa_cmp_select(proj3, kvc, n_topk)
    o_w = _nsa_window(proj3)
    y_d = _nsa_selected(proj3, sbias, used, o_c, o_w, small3, _gate_expand())

    ys = [y.reshape(t, BRANCH_WIDTH) for y in (y_a, y_b, y_c, y_d)]
    return _merge(ys, proj, w_branch, w_out, x2, l).reshape(b, s, d)


def kernel(x, norm_mix, w_in, diff_lambda, diff_subln, mla_norm_q, mla_w_uq, mla_norm_kv, mla_w_ukv, fox_b_f,
           nsa_cmp_pe, nsa_cmp_w1, nsa_cmp_w2, w_branch, w_out, norm_ffn, w_up, conv_w, conv_b, w_down, norm_final):
    depth = w_in.shape[0]
    s = x.shape[1]
    rope_tabs = _rope_tables(s)
    w_in = _w_in_relayout(w_in)
    w_up16, w_down16 = w_up.astype(CDT), w_down.astype(CDT)
    w_branch16, w_out16 = w_branch.astype(CDT), w_out.astype(CDT)
    for l in range(depth):
        x = _token_mixers(x, l, norm_mix[l], w_in, diff_lambda[l], diff_subln[l], mla_norm_q[l], mla_w_uq[l],
                          mla_norm_kv[l], mla_w_ukv[l], fox_b_f[l], nsa_cmp_pe[l], nsa_cmp_w1[l], nsa_cmp_w2[l],
                          w_branch16, w_out16, rope_tabs)
        x = _ffn(x, norm_ffn[l], w_up16, conv_w[l], conv_b[l], w_down16, norm_final, l, final=(l == depth - 1))
    return x
```

```python
import functools
import math

import numpy as np
import jax
import jax.numpy as jnp
from jax import lax
from jax.experimental import pallas as pl
from jax.experimental.pallas import tpu as pltpu

F32 = jnp.float32
CDT = jnp.bfloat16

NEG = -1e30
NEG_INF = -1e30
NORM_EPS = 1e-6
LOG2E = 1.4426950408889634
LANES = 128

D_MODEL = 1024
DIFF_HEADS, DIFF_DH = 4, 64
MLA_HEADS, MLA_NOPE, MLA_ROPE, MLA_VDIM = 4, 128, 64, 128
MLA_Q_LORA, MLA_KV_LORA = 256, 256
ROPE_THETA = 10000.0
FOX_HEADS, FOX_DH = 4, 128
NSA_HEADS, NSA_GROUPS, NSA_DH = 8, 2, 64
NSA_HPG = NSA_HEADS // NSA_GROUPS
CMP_STRIDE = 16
CMP_LEN = 2 * CMP_STRIDE
CMP_HIDDEN = 128
SLC_LEN = 64
SLC_SHIFT = 6
HALF_SHIFT = 6
SLC_TOPK = 8
WINDOW = 256
N_BRANCH = 4
BRANCH_WIDTH = 512
D_FF = 2816
CONV_WIDTH = 3

PB_AQ, PB_AK, PB_AV = 0, 4, 8
PB_CQ, PB_CK, PB_CV = 12, 16, 20
PB_DQ = 24
PB_CMP_K, PB_CMP_V, PB_SEL_K, PB_SEL_V, PB_WIN_K, PB_WIN_V = 28, 29, 30, 31, 32, 33
PB_BCQ, PB_BCKV, PB_BKR, PB_BKRS = 34, 36, 38, 39
PB_GATE = 40
N_PROJ = 72 * LANES
SMALL_F, SMALL_G = 0, 4

VMEM_LIMIT = 56 * 1024 * 1024
MXU_TILE = 256
TQ_DENSE = 512
TQ_NSA = WINDOW
TM_PROJ, TN_PROJ = 1024, 9 * MXU_TILE
TM_ROWS = 512
FFN_CHUNK = MXU_TILE
FOX_HP = 2
MLA_HP = 2


def _cparams(sem):
    return pltpu.CompilerParams(dimension_semantics=sem, vmem_limit_bytes=VMEM_LIMIT)


def _rms(xf, g):
    return xf * lax.rsqrt(jnp.mean(xf * xf, axis=-1, keepdims=True) + NORM_EPS) * g


def _sigmoid(x):
    return 0.5 * jnp.tanh(0.5 * x) + 0.5


def _dot(a, b):
    return jnp.dot(a, b, preferred_element_type=F32)


def _dot_nt(a, b):
    return lax.dot_general(a, b, (((1,), (1,)), ((), ())), preferred_element_type=F32)


def _split_dot(a, b):
    hi = a.astype(CDT)
    lo = (a - hi.astype(F32)).astype(CDT)
    return _dot(hi, b) + _dot(lo, b)


def _alibi_slopes(n):
    return (LOG2E * np.exp2(-8.0 * np.arange(1, n + 1) / n)).astype(np.float32)


def _inproj_kernel(x_ref, g_ref, w_ref, ws_ref, o_ref, os_ref, ost_ref, h_ref):
    @pl.when(pl.program_id(1) == 0)
    def _():
        h = _rms(x_ref[...], g_ref[...]).astype(CDT)
        h_ref[...] = h
        os_ref[...] = _dot_nt(h, ws_ref[...])
        ost_ref[...] = _dot_nt(ws_ref[...], h)

    o_ref[...] = _dot_nt(h_ref[...], w_ref[...]).astype(o_ref.dtype)


def _in_proj(x2, g, w, ws, layer):
    t, d = x2.shape
    n = w.shape[1]
    tm = min(TM_PROJ, t)
    tn = TN_PROJ
    assert n % tn == 0
    return pl.pallas_call(
        _inproj_kernel,
        grid=(t // tm, n // tn),
        in_specs=[
            pl.BlockSpec((tm, d), lambda i, j: (i, 0)),
            pl.BlockSpec((1, d), lambda i, j: (0, 0)),
            pl.BlockSpec((None, tn, d), lambda i, j: (layer, j, 0)),
            pl.BlockSpec((None, LANES, d), lambda i, j: (layer, 0, 0)),
        ],
        out_specs=[
            pl.BlockSpec((tm, tn), lambda i, j: (i, j)),
            pl.BlockSpec((tm, LANES), lambda i, j: (i, 0)),
            pl.BlockSpec((LANES, tm), lambda i, j: (0, i)),
        ],
        out_shape=[jax.ShapeDtypeStruct((t, n), CDT), jax.ShapeDtypeStruct((t, LANES), F32),
                   jax.ShapeDtypeStruct((LANES, t), F32)],
        scratch_shapes=[pltpu.VMEM((tm, d), CDT)],
        compiler_params=_cparams(("parallel", "arbitrary")),
        name="in_proj",
    )(x2, g.reshape(1, d), w, ws)


def _fox_cumsum_kernel(cf_ref, bf_ref, o_ref):
    rows, s = cf_ref.shape
    lane = lax.broadcasted_iota(jnp.int32, (rows, LANES), 1)
    carry = jnp.zeros((rows, 1), F32)
    for c in range(s // LANES):
        z = cf_ref[:, c * LANES:(c + 1) * LANES] + bf_ref[...]
        xs = jnp.minimum(z, 0.0) - jnp.log1p(jnp.exp(-jnp.abs(z)))
        d = 1
        while d < LANES:
            xs = xs + jnp.where(lane >= d, pltpu.roll(xs, d, axis=1), 0.0)
            d *= 2
        xs = xs + carry
        o_ref[:, c * LANES:(c + 1) * LANES] = xs
        carry = xs[:, LANES - 1:LANES]


def _fox_cumsum(cf_rows, bias_rows):
    return pl.pallas_call(
        _fox_cumsum_kernel,
        out_shape=jax.ShapeDtypeStruct(cf_rows.shape, F32),
        name="fox_cumsum",
    )(cf_rows, bias_rows)


def _flash_scratch(rows, tk, mask_scratch=False):
    return [pltpu.VMEM((rows, LANES), F32), pltpu.VMEM((rows, 2 * LANES), F32),
            pltpu.VMEM((rows, tk), F32), pltpu.VMEM((rows, tk), F32),
            pltpu.VMEM((rows, LANES), F32), pltpu.VMEM((rows, LANES), F32)
            ] + ([pltpu.VMEM((rows, tk), F32)] if mask_scratch else [])


def _flash_reset(m_ref, acc_ref):
    m_ref[...] = jnp.full(m_ref.shape, NEG, F32)
    acc_ref[...] = jnp.zeros(acc_ref.shape, F32)


def _flash_begin(m_ref, acc_ref, cm_ref, tq):
    _flash_reset(m_ref, acc_ref)
    cm_ref[...] = _causal_bias(cm_ref.shape[0], cm_ref.shape[1], tq)


def _row_max(s):
    return jnp.broadcast_to(jnp.max(s, axis=-1, keepdims=True), (s.shape[0], LANES))


def _causal_bias(rows, tk, tq):
    r = lax.broadcasted_iota(jnp.int32, (rows, tk), 0) & (tq - 1)
    c = lax.broadcasted_iota(jnp.int32, (rows, tk), 1)
    return jnp.where(c <= r, 0.0, NEG)


def _put_logits(buf, s, rows=slice(None), diag=False, cm_ref=None, tq=None):
    if diag is True:
        s = s + (cm_ref[rows] if cm_ref is not None else _causal_bias(s.shape[0], s.shape[1], tq))
    elif diag is not False:
        s = s + diag.astype(F32) * cm_ref[rows]
    buf[0][rows] = s
    buf[1][rows] = _row_max(s)


def _with_ones(v):
    return jnp.concatenate([v, jnp.ones((v.shape[0], LANES), v.dtype)], axis=1)


def _flash_consume(buf, v, m_ref, acc_ref, mask_tq=None, rows=slice(None)):
    s = buf[0][rows]
    m_cur = buf[1][rows]
    if mask_tq is not None:
        s = s + _causal_bias(s.shape[0], s.shape[1], mask_tq)
        m_cur = _row_max(s)
    m_old = m_ref[rows]
    m_new = jnp.maximum(m_old, m_cur)
    alpha = jnp.exp2(m_old - m_new)
    p = jnp.exp2(s - jnp.tile(m_new, (1, s.shape[1] // LANES))).astype(CDT)
    acc_ref[rows] = jnp.tile(alpha, (1, 2)) * acc_ref[rows] + _dot(p, _with_ones(v))
    m_ref[rows] = m_new


def _flash_result(acc):
    return acc[:, :LANES] / acc[:, LANES:]


def _causal_schedule(nq):
    ent = [(qi, ki, int(ki == qi)) for qi in range(nq) for ki in range(qi + 1)]
    n = len(ent)
    a = np.asarray(ent + [ent[-1]] * 2, np.int32)
    return n, tuple(jnp.asarray(a[:, i]) for i in range(3))


def _flash_stream(n, sched, base, produce, consume, finish, buf_a, buf_b, mask_at):
    qt, kt, lt = sched
    assert mask_at in ("produce", "consume", "mixed")
    always, never = mask_at == "produce", mask_at == "consume"

    def step(cur, nxt, t, diag, produce_diag, masked_already):
        if nxt is not None:
            produce(nxt, qt[base + t + 1], kt[base + t + 1], produce_diag)
        consume(cur, kt[base + t], diag and not masked_already)
        if diag:
            finish(qt[base + t])

    produce(buf_a, qt[base], kt[base], always)

    def pair(j, c):
        t = 2 * j
        l0, l1 = lt[base + t], lt[base + t + 1]
        for d0 in (False, True):
            for d1 in (False, True):
                @pl.when(((l0 != 0) == d0) & ((l1 != 0) == d1))
                def _():
                    step(buf_a, buf_b, t, d0, d1 and not never, always)
                    step(buf_b, buf_a, t + 1, d1, lt[base + t + 2] if always else False, not never)
        return c

    lax.fori_loop(0, n // 2, pair, 0)

    def tail():
        step(buf_a, None, n - 1, True, None, always)

    if isinstance(n, int):
        if n % 2 == 1:
            tail()
    else:
        pl.when(n % 2 == 1)(tail)


def _tile(ref, i, t):
    return ref[0, pl.ds(pl.multiple_of(i * t, t), t), :]


def _diff_attn_kernel(qt_ref, kt_ref, lt_ref, slopes_ref, lam_ref, g_ref, q_ref, k_ref, v_ref, o_ref,
                      m_ref, acc_ref, sa_ref, sb_ref, ma_ref, mb_ref, *, tq, n, lam_init):
    slope = slopes_ref[pl.program_id(1)]
    _flash_reset(m_ref, acc_ref)
    col = lax.broadcasted_iota(jnp.int32, (1, tq), 1).astype(F32)
    lane = lax.broadcasted_iota(jnp.int32, (tq, LANES), 1)
    lf = lam_ref[...]
    lam = (jnp.exp(jnp.sum(lf[0:1] * lf[1:2], axis=-1, keepdims=True))
           - jnp.exp(jnp.sum(lf[2:3] * lf[3:4], axis=-1, keepdims=True)) + lam_init)

    def produce(buf, qi, ki, diag):
        q = _tile(q_ref, qi, tq)
        zero = jnp.zeros_like(q)
        qq = jnp.concatenate([jnp.where(lane < DIFF_DH, q, zero), jnp.where(lane >= DIFF_DH, q, zero)], axis=0)
        s = _dot_nt(qq, _tile(k_ref, ki, tq))
        _put_logits(buf, s + slope * (col + ((ki - qi) * tq).astype(F32)), diag=diag, tq=tq)

    def consume(buf, ki, diag):
        _flash_consume(buf, _tile(v_ref, ki, tq), m_ref, acc_ref, tq if diag else None)

    def finish(qi):
        o = _flash_result(acc_ref[...])
        d = o[0:tq] - lam * o[tq:2 * tq]
        o_ref[0, pl.ds(pl.multiple_of(qi * tq, tq), tq), :] = (
            _rms(d, g_ref[...]) * (1.0 - lam_init)).astype(o_ref.dtype)
        _flash_reset(m_ref, acc_ref)

    _flash_stream(n, (qt_ref, kt_ref, lt_ref), 0, produce, consume, finish, (sa_ref, ma_ref), (sb_ref, mb_ref),
                  mask_at="consume")


_SMEM = pl.BlockSpec(memory_space=pltpu.SMEM)


def _diff_attention(proj3, diff_lambda, subln, lam_init):
    b, s, _ = proj3.shape
    tq = min(TQ_DENSE, s)
    dv = 2 * DIFF_DH
    n, sched = _causal_schedule(s // tq)
    kern = functools.partial(_diff_attn_kernel, tq=tq, n=n, lam_init=lam_init)
    return pl.pallas_call(
        kern,
        grid=(b, DIFF_HEADS),
        in_specs=[
            _SMEM, _SMEM, _SMEM, _SMEM,
            pl.BlockSpec((4, DIFF_DH), lambda bi, h: (0, 0)),
            pl.BlockSpec((1, dv), lambda bi, h: (0, 0)),
            pl.BlockSpec((1, s, LANES), lambda bi, h: (bi, 0, PB_AQ + h)),
            pl.BlockSpec((1, s, LANES), lambda bi, h: (bi, 0, PB_AK + h)),
            pl.BlockSpec((1, s, LANES), lambda bi, h: (bi, 0, PB_AV + h)),
        ],
        out_specs=pl.BlockSpec((1, s, dv), lambda bi, h: (bi, 0, h)),
        out_shape=jax.ShapeDtypeStruct((b, s, DIFF_HEADS * dv), CDT),
        scratch_shapes=_flash_scratch(2 * tq, tq),
        compiler_params=_cparams(("parallel", "parallel")),
        name="diff_attention",
    )(*sched, jnp.asarray(_alibi_slopes(DIFF_HEADS)), diff_lambda, subln.reshape(1, dv), proj3, proj3, proj3)


def _mla_prep_kernel(cq_ref, ckv_ref, kr_ref, krs_ref, gq_ref, gkv_ref, wqm_ref, wqs_ref, wk_ref, wv_ref,
                     cosq_ref, sinq_ref, cosk_ref, sink_ref, q_ref, k_ref, v_ref):
    hq = _rms(cq_ref[0].astype(F32), gq_ref[...]).astype(CDT)
    qm = _dot(hq, wqm_ref[...])
    qs = _dot(hq, wqs_ref[...])
    cosq, sinq = cosq_ref[...], sinq_ref[...]
    hw = 2 * LANES
    for h in range(MLA_HEADS):
        sl = slice(h * hw, (h + 1) * hw)
        q_ref[0, :, sl] = (qm[:, sl] * cosq + qs[:, sl] * sinq).astype(q_ref.dtype)
    hkv = _rms(ckv_ref[0].astype(F32), gkv_ref[...]).astype(CDT)
    kn = _dot(hkv, wk_ref[...])
    v_ref[0] = _dot(hkv, wv_ref[...]).astype(v_ref.dtype)
    kpe = (kr_ref[0].astype(F32) * cosk_ref[...] + krs_ref[0].astype(F32) * sink_ref[...]).astype(k_ref.dtype)
    for h in range(MLA_HEADS):
        k_ref[0, :, h * hw:h * hw + LANES] = kn[:, h * LANES:(h + 1) * LANES].astype(k_ref.dtype)
        k_ref[0, :, h * hw + LANES:(h + 1) * hw] = kpe


def _mla_prep(proj3, gq, gkv, wqm, wqs, wk, wv, tabs):
    b, s, _ = proj3.shape
    tm = min(TM_PROJ, s)
    hw = 2 * LANES
    cosq, sinq, cosk, sink = tabs
    const = lambda shape: pl.BlockSpec(shape, lambda bi, i: (0,) * len(shape))
    return pl.pallas_call(
        _mla_prep_kernel,
        grid=(b, s // tm),
        in_specs=[
            pl.BlockSpec((1, tm, MLA_Q_LORA), lambda bi, i: (bi, i, PB_BCQ // 2)),
            pl.BlockSpec((1, tm, MLA_KV_LORA), lambda bi, i: (bi, i, PB_BCKV // 2)),
            pl.BlockSpec((1, tm, LANES), lambda bi, i: (bi, i, PB_BKR)),
            pl.BlockSpec((1, tm, LANES), lambda bi, i: (bi, i, PB_BKRS)),
            const((1, MLA_Q_LORA)), const((1, MLA_KV_LORA)),
            const((MLA_Q_LORA, MLA_HEADS * hw)), const((MLA_Q_LORA, MLA_HEADS * hw)),
            const((MLA_KV_LORA, MLA_HEADS * MLA_NOPE)), const((MLA_KV_LORA, MLA_HEADS * MLA_VDIM)),
            pl.BlockSpec((tm, hw), lambda bi, i: (i, 0)), pl.BlockSpec((tm, hw), lambda bi, i: (i, 0)),
            pl.BlockSpec((tm, LANES), lambda bi, i: (i, 0)), pl.BlockSpec((tm, LANES), lambda bi, i: (i, 0)),
        ],
        out_specs=[
            pl.BlockSpec((1, tm, MLA_HEADS * hw), lambda bi, i: (bi, i, 0)),
            pl.BlockSpec((1, tm, MLA_HEADS * hw), lambda bi, i: (bi, i, 0)),
            pl.BlockSpec((1, tm, MLA_HEADS * MLA_VDIM), lambda bi, i: (bi, i, 0)),
        ],
        out_shape=[
            jax.ShapeDtypeStruct((b, s, MLA_HEADS * hw), CDT),
            jax.ShapeDtypeStruct((b, s, MLA_HEADS * hw), CDT),
            jax.ShapeDtypeStruct((b, s, MLA_HEADS * MLA_VDIM), CDT),
        ],
        compiler_params=_cparams(("parallel", "parallel")),
        name="mla_prep",
    )(proj3, proj3, proj3, proj3, gq.reshape(1, -1), gkv.reshape(1, -1), wqm, wqs, wk, wv,
      cosq, sinq, cosk, sink)


def _plain_attn_kernel(qt_ref, kt_ref, lt_ref, q_ref, k_ref, v_ref, o_ref,
                       m_ref, acc_ref, sa_ref, sb_ref, ma_ref, mb_ref, *, tq, n, hp, dk, dv):
    _flash_reset(m_ref, acc_ref)
    heads = [(slice(h * tq, (h + 1) * tq), slice(h * dk, (h + 1) * dk), slice(h * dv, (h + 1) * dv))
             for h in range(hp)]

    def produce(buf, qi, ki, diag):
        q, k = _tile(q_ref, qi, tq), _tile(k_ref, ki, tq)
        for rows, kcols, _ in heads:
            _put_logits(buf, _dot_nt(q[:, kcols], k[:, kcols]), rows, diag, tq=tq)

    def consume(buf, ki, diag):
        v = _tile(v_ref, ki, tq)
        for rows, _, vcols in heads:
            _flash_consume(buf, v[:, vcols], m_ref, acc_ref, tq if diag else None, rows)

    def finish(qi):
        for rows, _, vcols in heads:
            o_ref[0, pl.ds(pl.multiple_of(qi * tq, tq), tq), vcols] = _flash_result(acc_ref[rows]).astype(o_ref.dtype)
        _flash_reset(m_ref, acc_ref)

    _flash_stream(n, (qt_ref, kt_ref, lt_ref), 0, produce, consume, finish, (sa_ref, ma_ref), (sb_ref, mb_ref),
                  mask_at="mixed")


def _mla_attention(qc, kc, v):
    b, s, _ = qc.shape
    tq = min(TQ_DENSE, s)
    hw = 2 * LANES
    hp = MLA_HP
    n, sched = _causal_schedule(s // tq)
    return pl.pallas_call(
        functools.partial(_plain_attn_kernel, tq=tq, n=n, hp=hp, dk=hw, dv=MLA_VDIM),
        grid=(b, MLA_HEADS // hp),
        in_specs=[
            _SMEM, _SMEM, _SMEM,
            pl.BlockSpec((1, s, hp * hw), lambda bi, h: (bi, 0, h)),
            pl.BlockSpec((1, s, hp * hw), lambda bi, h: (bi, 0, h)),
            pl.BlockSpec((1, s, hp * MLA_VDIM), lambda bi, h: (bi, 0, h)),
        ],
        out_specs=pl.BlockSpec((1, s, hp * MLA_VDIM), lambda bi, h: (bi, 0, h)),
        out_shape=jax.ShapeDtypeStruct((b, s, MLA_HEADS * MLA_VDIM), CDT),
        scratch_shapes=_flash_scratch(hp * tq, tq),
        compiler_params=_cparams(("parallel", "parallel")),
        name="mla_attention",
    )(*sched, qc, kc, v)


def _fox_attn_kernel(qt_ref, kt_ref, lt_ref, c_ref, q_ref, k_ref, v_ref, o_ref,
                     m_ref, acc_ref, sa_ref, sb_ref, ma_ref, mb_ref, *, tq, n, hp):
    _flash_reset(m_ref, acc_ref)
    heads = [(slice(h * tq, (h + 1) * tq), slice(h * FOX_DH, (h + 1) * FOX_DH)) for h in range(hp)]

    def produce(buf, qi, ki, diag):
        q, k = _tile(q_ref, qi, tq), _tile(k_ref, ki, tq)
        for h, (rows, cols) in enumerate(heads):
            cbase = c_ref[0, h, pl.ds(qi, 1), :][:, 0:1]
            s = _dot_nt(q[:, cols], k[:, cols]) + LOG2E * (cbase - c_ref[0, h, pl.ds(ki, 1), :])
            _put_logits(buf, s, rows, diag, tq=tq)

    def consume(buf, ki, diag):
        v = _tile(v_ref, ki, tq)
        for rows, cols in heads:
            _flash_consume(buf, v[:, cols], m_ref, acc_ref, tq if diag else None, rows)

    def finish(qi):
        for rows, cols in heads:
            o_ref[0, pl.ds(pl.multiple_of(qi * tq, tq), tq), cols] = _flash_result(acc_ref[rows]).astype(o_ref.dtype)
        _flash_reset(m_ref, acc_ref)

    _flash_stream(n, (qt_ref, kt_ref, lt_ref), 0, produce, consume, finish, (sa_ref, ma_ref), (sb_ref, mb_ref),
                  mask_at="mixed")


def _fox_attention(proj3, c4):
    b, s, _ = proj3.shape
    tq = min(TQ_DENSE, s)
    nk = s // tq
    hp = FOX_HP
    w = hp * FOX_DH
    n, sched = _causal_schedule(nk)
    return pl.pallas_call(
        functools.partial(_fox_attn_kernel, tq=tq, n=n, hp=hp),
        grid=(b, FOX_HEADS // hp),
        in_specs=[
            _SMEM, _SMEM, _SMEM,
            pl.BlockSpec((1, hp, nk, tq), lambda bi, h: (bi, h, 0, 0)),
            pl.BlockSpec((1, s, w), lambda bi, h: (bi, 0, PB_CQ // hp + h)),
            pl.BlockSpec((1, s, w), lambda bi, h: (bi, 0, PB_CK // hp + h)),
            pl.BlockSpec((1, s, w), lambda bi, h: (bi, 0, PB_CV // hp + h)),
        ],
        out_specs=pl.BlockSpec((1, s, w), lambda bi, h: (bi, 0, h)),
        out_shape=jax.ShapeDtypeStruct((b, s, FOX_HEADS * FOX_DH), CDT),
        scratch_shapes=_flash_scratch(hp * tq, tq),
        compiler_params=_cparams(("parallel", "parallel")),
        name="fox_attention",
    )(*sched, c4.reshape(b, FOX_HEADS, nk, tq), proj3, proj3, proj3)


def _nsa_compress_kernel(x_ref, w1a_ref, w1b_ref, pea_ref, peb_ref, w2_ref, o_ref):
    x = x_ref[0]
    n = x.shape[0]
    pa = _dot(x, w1a_ref[...])
    pb = _dot(x, w1b_ref[...])
    pe = _dot(pea_ref[...], w1a_ref[...]) + _dot(peb_ref[...], w1b_ref[...])
    hid = pa + pltpu.roll(pb, n - 1, axis=0) + pe[0:1]
    act = 0.5 * hid * (1.0 + jnp.tanh(math.sqrt(2.0 / math.pi) * (hid + 0.044715 * hid * hid * hid)))
    o_ref[0] = _dot(act.astype(CDT), w2_ref[...]).astype(o_ref.dtype)


def _nsa_compress(xc, w1a, w1b, pea, peb, w2):
    b, n, kdim = xc.shape
    hdim = w1a.shape[1]
    const = lambda shape: pl.BlockSpec(shape, lambda bi: (0,) * len(shape))
    return pl.pallas_call(
        _nsa_compress_kernel,
        grid=(b,),
        in_specs=[pl.BlockSpec((1, n, kdim), lambda bi: (bi, 0, 0)),
                  const((kdim, hdim)), const((kdim, hdim)), const((8, kdim)), const((8, kdim)),
                  const((hdim, w2.shape[1]))],
        out_specs=pl.BlockSpec((1, n, w2.shape[1]), lambda bi: (bi, 0, 0)),
        out_shape=jax.ShapeDtypeStruct((b, n, w2.shape[1]), CDT),
        compiler_params=_cparams(("parallel",)),
        name="nsa_compress",
    )(xc, w1a, w1b, pea, peb, w2)


def _nsa_cmp_kernel(slopes_ref, q_ref, kv_ref, oc_ref, sb_ref, used_ref, *, tq, n_topk):
    qi = pl.program_id(1)
    nblk = kv_ref.shape[1]
    q0 = qi * tq
    rowpos = q0 + lax.broadcasted_iota(jnp.int32, (tq, 1), 0)
    cmp_end = lax.broadcasted_iota(jnp.int32, (1, nblk), 1) * CMP_STRIDE + (CMP_LEN - 1)
    negmask = jnp.where(rowpos >= cmp_end, 0.0, NEG)
    end_rel = (cmp_end - q0).astype(F32)
    lane = lax.broadcasted_iota(jnp.int32, (tq, LANES), 1)
    low = lane < NSA_DH
    nn = lax.broadcasted_iota(jnp.int32, (NSA_DH, nblk), 1) * CMP_STRIDE
    jj = lax.broadcasted_iota(jnp.int32, (NSA_DH, nblk), 0) * SLC_LEN
    ovt = (jnp.maximum(jnp.minimum(nn + CMP_LEN, jj + SLC_LEN) - jnp.maximum(nn, jj), 0).astype(F32)
           * (1.0 / CMP_LEN)).astype(CDT)
    jt = lax.broadcasted_iota(jnp.int32, (NSA_DH, tq), 0).astype(F32)
    blk = ((q0 + lax.broadcasted_iota(jnp.int32, (1, tq), 1)) >> SLC_SHIFT).astype(F32)
    fixed = (jt == 0.0) | (jt == blk) | (jt == blk - 1.0)
    beyond = jt > blk
    row_ok = rowpos >= CMP_LEN - 1
    outs = []
    bias = []
    for g in range(NSA_GROUPS):
        kc = kv_ref[0, :, g * LANES:(g + 1) * LANES]
        vc = kv_ref[0, :, (NSA_GROUPS + g) * LANES:(NSA_GROUPS + g + 1) * LANES]
        psum = jnp.zeros((tq, nblk), F32)
        mine = low if g == 0 else jnp.logical_not(low)
        zero = jnp.zeros((tq, LANES), q_ref.dtype)
        qs = jnp.concatenate([jnp.where(mine, q_ref[0, :, j * LANES:(j + 1) * LANES], zero)
                              for j in range(NSA_HPG)], axis=0)
        s_all = _dot_nt(qs, kc)
        ps = []
        for j in range(NSA_HPG):
            s = s_all[j * tq:(j + 1) * tq] + slopes_ref[g * NSA_HPG + j] * end_rel + negmask
            e = jnp.exp2(s - jnp.max(s, axis=-1, keepdims=True))
            den = jnp.sum(e, axis=-1, keepdims=True)
            p = e * jnp.where(row_ok, 1.0 / den, 0.0)
            psum = psum + p
            ps.append(p.astype(CDT))
        o_all = _dot(jnp.concatenate(ps, axis=0), vc)
        outs.extend(o_all[j * tq:(j + 1) * tq] for j in range(NSA_HPG))
        hi = psum.astype(CDT)
        lo = (psum - hi.astype(F32)).astype(CDT)
        imp = _dot_nt(ovt, hi) + _dot_nt(ovt, lo)
        imp = jnp.where(fixed, -jnp.inf, jnp.where(beyond, NEG_INF, imp))
        sbt = jnp.where(fixed, 0.0, NEG)
        for _ in range(n_topk - 3):
            mx = jnp.max(imp, axis=0, keepdims=True)
            idx = jnp.min(jnp.where(imp == mx, jt, float(LANES)), axis=0, keepdims=True)
            hit = jt == idx
            sbt = jnp.where(hit, 0.0, sbt)
            imp = jnp.where(hit, -jnp.inf, imp)
        bias.append(sbt)
    sb = jnp.concatenate([bias[1], bias[0]], axis=0).T
    sb_ref[0] = sb.astype(sb_ref.dtype)
    used = jnp.max(jnp.where(sb == 0.0, 1.0, 0.0), axis=0, keepdims=True)
    used_ref[0, 0] = jnp.broadcast_to(used, used_ref.shape[2:])
    for blk_i in range(NSA_HEADS // 2):
        oc_ref[0, :, blk_i * LANES:(blk_i + 1) * LANES] = jnp.where(
            low, outs[2 * blk_i], outs[2 * blk_i + 1]).astype(oc_ref.dtype)


def _nsa_cmp_select(proj3, kvc, n_topk):
    assert n_topk >= 3, "the three always-selected blocks must fit in the top-k budget"
    b, s, _ = proj3.shape
    tq = min(TQ_NSA, s)
    nblk = kvc.shape[1]
    return pl.pallas_call(
        functools.partial(_nsa_cmp_kernel, tq=tq, n_topk=n_topk),
        grid=(b, s // tq),
        in_specs=[
            pl.BlockSpec(memory_space=pltpu.SMEM),
            pl.BlockSpec((1, tq, 4 * LANES), lambda bi, qi: (bi, qi, PB_DQ // 4)),
            pl.BlockSpec((1, nblk, kvc.shape[2]), lambda bi, qi: (bi, 0, 0)),
        ],
        out_specs=[
            pl.BlockSpec((1, tq, NSA_HEADS * NSA_DH), lambda bi, qi: (bi, qi, 0)),
            pl.BlockSpec((1, tq, LANES), lambda bi, qi: (bi, qi, 0)),
            pl.BlockSpec((1, 1, 8, LANES), lambda bi, qi: (bi, qi, 0, 0)),
        ],
        out_shape=[jax.ShapeDtypeStruct((b, s, NSA_HEADS * NSA_DH), CDT),
                   jax.ShapeDtypeStruct((b, s, LANES), CDT),
                   jax.ShapeDtypeStruct((b, s // tq, 8, LANES), F32)],
        compiler_params=_cparams(("parallel", "parallel")),
        name="nsa_cmp_select",
    )(jnp.asarray(_alibi_slopes(NSA_HEADS)), proj3, kvc)


def _compact_heads(heads, mine, low):
    both = [jnp.where(mine, a, pltpu.roll(a, NSA_DH, axis=1)) for a in heads]
    out = [jnp.where(low, both[2 * jj], both[2 * jj + 1]) for jj in range(NSA_HPG // 2)]
    return jnp.concatenate(out, axis=1)


def _nsa_win_kernel(slopes_ref, q_ref, kp_ref, kc_ref, vp_ref, vc_ref, o_ref, *, tq):
    qi = pl.program_id(1)
    lane = lax.broadcasted_iota(jnp.int32, (tq, LANES), 1)
    low = lane < NSA_DH
    r = lax.broadcasted_iota(jnp.int32, (tq, tq), 0)
    c = lax.broadcasted_iota(jnp.int32, (tq, tq), 1)
    own = c <= r
    ndist = jnp.where(own, c - r, c - r - tq).astype(F32)
    own_f = jnp.where(own, 1.0, 0.0).astype(CDT)
    prev_pen = jnp.where(qi > 0, 0.0, NEG)
    q = q_ref[0]
    zero = jnp.zeros((tq, LANES), q.dtype)
    mine = (low, jnp.logical_not(low))
    qs = jnp.concatenate([jnp.where(mine[g], q[:, j * LANES:(j + 1) * LANES], zero)
                          for g in range(NSA_GROUPS) for j in range(NSA_HPG)], axis=0)
    s_own, s_prev = _dot_nt(qs, kc_ref[0]), _dot_nt(qs, kp_ref[0])
    ps = []
    for hd in range(NSA_HEADS):
        rows = slice(hd * tq, (hd + 1) * tq)
        s = jnp.where(own, s_own[rows], s_prev[rows] + prev_pen) + slopes_ref[hd] * ndist
        ps.append(jnp.exp2(s - jnp.max(s, axis=-1, keepdims=True)).astype(CDT))
    p = jnp.concatenate(ps, axis=0)
    p_own = p * jnp.tile(own_f, (NSA_HEADS, 1))
    o = _flash_result(_dot(p_own, _with_ones(vc_ref[0])) + _dot(p - p_own, _with_ones(vp_ref[0])))
    for g in range(NSA_GROUPS):
        heads = [o[(g * NSA_HPG + j) * tq:(g * NSA_HPG + j + 1) * tq] for j in range(NSA_HPG)]
        w = NSA_HPG * NSA_DH
        o_ref[0, :, g * w:(g + 1) * w] = _compact_heads(heads, mine[g], low).astype(o_ref.dtype)


def _nsa_window(proj3):
    b, s, _ = proj3.shape
    tq = WINDOW
    return pl.pallas_call(
        functools.partial(_nsa_win_kernel, tq=tq),
        grid=(b, s // tq),
        in_specs=[
            pl.BlockSpec(memory_space=pltpu.SMEM),
            pl.BlockSpec((1, tq, 4 * LANES), lambda bi, qi: (bi, qi, PB_DQ // 4)),
            pl.BlockSpec((1, tq, LANES), lambda bi, qi: (bi, jnp.maximum(qi - 1, 0), PB_WIN_K)),
            pl.BlockSpec((1, tq, LANES), lambda bi, qi: (bi, qi, PB_WIN_K)),
            pl.BlockSpec((1, tq, LANES), lambda bi, qi: (bi, jnp.maximum(qi - 1, 0), PB_WIN_V)),
            pl.BlockSpec((1, tq, LANES), lambda bi, qi: (bi, qi, PB_WIN_V)),
        ],
        out_specs=pl.BlockSpec((1, tq, NSA_HEADS * NSA_DH), lambda bi, qi: (bi, qi, 0)),
        out_shape=jax.ShapeDtypeStruct((b, s, NSA_HEADS * NSA_DH), CDT),
        compiler_params=_cparams(("parallel", "parallel")),
        name="nsa_window",
    )(jnp.asarray(_alibi_slopes(NSA_HEADS)), proj3, proj3, proj3, proj3, proj3)


def _nsa_sel_kernel(cnt_ref, qt_ref, kt_ref, lt_ref, slopes_ref, q_ref, sb_ref, k_ref, v_ref, oc_ref, ow_ref, gl_ref,
                    e_ref, o_ref, m_ref, acc_ref, sa_ref, sb2_ref, ma_ref, mb_ref, cm_ref, *, tq, rows_per_problem):
    g = pl.program_id(1)
    w = NSA_HPG * NSA_DH
    lane = lax.broadcasted_iota(jnp.int32, (tq, LANES), 1)
    low = lane < NSA_DH
    mine = (lane >> HALF_SHIFT) == g
    _flash_begin(m_ref, acc_ref, cm_ref, tq)
    col = lax.broadcasted_iota(jnp.int32, (1, tq), 1).astype(F32)
    jl = lane & (NSA_DH - 1)
    krow = lax.broadcasted_iota(jnp.int32, (tq, LANES), 0)

    def produce(buf, qi, ki, diag):
        q = _tile(q_ref, qi, tq)
        sb = _tile(sb_ref, qi, tq)
        qa = jnp.concatenate([jnp.where(mine, q[:, j * LANES:(j + 1) * LANES], sb) for j in range(NSA_HPG)], axis=0)
        k = _tile(k_ref, ki, tq)
        onehot = jnp.where(((ki * tq + krow) >> SLC_SHIFT) == jl, 1.0, 0.0).astype(k.dtype)
        s_all = _dot_nt(qa, jnp.where(mine, k, onehot))
        rel = ((ki - qi) * tq).astype(F32)
        for j in range(NSA_HPG):
            rows = slice(j * tq, (j + 1) * tq)
            _put_logits(buf, s_all[rows] + slopes_ref[g * NSA_HPG + j] * (col + rel), rows, diag, cm_ref)

    def consume(buf, ki, diag):
        _flash_consume(buf, _tile(v_ref, ki, tq), m_ref, acc_ref, tq if diag else None)

    def finish(qi):
        o = _flash_result(acc_ref[...])
        o_s = _compact_heads([o[j * tq:(j + 1) * tq] for j in range(NSA_HPG)], mine, low)
        gates = _split_dot(_sigmoid(_tile(gl_ref, qi, tq)), e_ref[0])
        y = (gates[:, 0:w] * _tile(oc_ref, qi, tq).astype(F32) + gates[:, w:2 * w] * o_s
             + gates[:, 2 * w:3 * w] * _tile(ow_ref, qi, tq).astype(F32))
        o_ref[0, pl.ds(pl.multiple_of(qi * tq, tq), tq), :] = y.astype(o_ref.dtype)
        _flash_reset(m_ref, acc_ref)

    prob = pl.program_id(0) * NSA_GROUPS + g
    _flash_stream(cnt_ref[prob], (qt_ref, kt_ref, lt_ref), prob * rows_per_problem, produce, consume, finish,
                  (sa_ref, ma_ref), (sb2_ref, mb_ref), mask_at="produce")


def _nsa_selected(proj3, sbias, used, o_c, o_w, small3, expand):
    b, s, _ = proj3.shape
    tq = min(TQ_NSA, s)
    nq = s // tq
    w = NSA_HPG * NSA_DH
    u = used[:, :, 0, :].reshape(b, nq, NSA_GROUPS, NSA_DH)[:, :, ::-1, :nq * (tq // SLC_LEN)]
    flags = (u.reshape(b, nq, NSA_GROUPS, nq, tq // SLC_LEN).max(axis=-1) > 0.0).astype(jnp.int32)
    flags = flags.transpose(0, 2, 1, 3)
    qt = jnp.arange(nq, dtype=jnp.int32)
    need = jnp.where(qt[None, :] < qt[:, None], flags, (qt[None, :] == qt[:, None]).astype(jnp.int32))
    need = need.reshape(b, NSA_GROUPS, nq * nq)
    cnt = need.sum(axis=-1).astype(jnp.int32)
    order = jnp.argsort(1 - need, axis=-1, stable=True).astype(jnp.int32)
    order = jnp.pad(order, ((0, 0), (0, 0), (0, 2)))
    rows = nq * nq + 2
    sched = (order // nq, order % nq, (order // nq == order % nq).astype(jnp.int32))
    return pl.pallas_call(
        functools.partial(_nsa_sel_kernel, tq=tq, rows_per_problem=rows),
        grid=(b, NSA_GROUPS),
        in_specs=[
            _SMEM, _SMEM, _SMEM, _SMEM, _SMEM,
            pl.BlockSpec((1, s, 4 * LANES), lambda bi, g: (bi, 0, PB_DQ // 4)),
            pl.BlockSpec((1, s, LANES), lambda bi, g: (bi, 0, 0)),
            pl.BlockSpec((1, s, LANES), lambda bi, g: (bi, 0, PB_SEL_K)),
            pl.BlockSpec((1, s, LANES), lambda bi, g: (bi, 0, PB_SEL_V)),
            pl.BlockSpec((1, s, w), lambda bi, g: (bi, 0, g)),
            pl.BlockSpec((1, s, w), lambda bi, g: (bi, 0, g)),
            pl.BlockSpec((1, s, LANES), lambda bi, g: (bi, 0, 0)),
            pl.BlockSpec((1, LANES, 3 * w), lambda bi, g: (g, 0, 0)),
        ],
        out_specs=pl.BlockSpec((1, s, w), lambda bi, g: (bi, 0, g)),
        out_shape=jax.ShapeDtypeStruct((b, s, NSA_HEADS * NSA_DH), CDT),
        scratch_shapes=_flash_scratch(NSA_HPG * tq, tq, mask_scratch=True),
        compiler_params=_cparams(("parallel", "parallel")),
        name="nsa_selected",
    )(cnt.reshape(-1), *[t.reshape(-1) for t in sched], jnp.asarray(_alibi_slopes(NSA_HEADS)),
      proj3, sbias, proj3, proj3, o_c, o_w, small3, expand)


def _merge_kernel(ya_ref, yb_ref, yc_ref, yd_ref, ga_ref, gb_ref, gc_ref, gd_ref, wb_ref, wo_ref, x_ref, o_ref):
    merged = None
    for n, (y_ref, g_ref) in enumerate(((ya_ref, ga_ref), (yb_ref, gb_ref), (yc_ref, gc_ref), (yd_ref, gd_ref))):
        t = _sigmoid(g_ref[...].astype(F32)) * _dot(y_ref[...], wb_ref[n])
        merged = t if merged is None else merged + t
    o_ref[...] = x_ref[...] + _dot(merged.astype(CDT), wo_ref[...])


def _merge(ys, proj2, wb, wo, x2, layer):
    t, d = x2.shape
    tm = min(TM_ROWS, t)
    gate_blk = PB_GATE * LANES // d
    yspec = pl.BlockSpec((tm, BRANCH_WIDTH), lambda i: (i, 0))
    gspecs = [pl.BlockSpec((tm, d), functools.partial(lambda i, n: (i, gate_blk + n), n=n)) for n in range(N_BRANCH)]
    return pl.pallas_call(
        _merge_kernel,
        grid=(t // tm,),
        in_specs=[yspec] * N_BRANCH + gspecs + [
            pl.BlockSpec((None, N_BRANCH, BRANCH_WIDTH, d), lambda i: (layer, 0, 0, 0)),
            pl.BlockSpec((None, d, d), lambda i: (layer, 0, 0)),
            pl.BlockSpec((tm, d), lambda i: (i, 0)),
        ],
        out_specs=pl.BlockSpec((tm, d), lambda i: (i, 0)),
        out_shape=jax.ShapeDtypeStruct((t, d), F32),
        compiler_params=_cparams(("parallel",)),
        name="merge",
    )(*ys, proj2, proj2, proj2, proj2, wb, wo, x2)


HALO = 16


def _ffn_kernel(x_ref, xh_ref, g_ref, wu_ref, cw_ref, cb_ref, wd_ref, gf_ref, o_ref, he_ref, u_ref, act_ref,
                *, tm, fc, final):
    i = pl.program_id(1)
    x = x_ref[0]
    g = g_ref[...]
    xh = xh_ref[0] * (i > 0).astype(F32)
    he_ref[0:HALO] = _rms(xh, g).astype(CDT)
    he_ref[HALO:HALO + tm] = _rms(x, g).astype(CDT)
    he = he_ref[...]
    for c in range(D_FF // fc):
        outs = []
        for half in range(2):
            ub = u_ref.at[c % 2, half]
            lo = half * D_FF + c * fc
            ub[...] = _dot(he, wu_ref[:, lo:lo + fc])
            conv = cb_ref[:, lo:lo + fc]
            for kk in range(CONV_WIDTH):
                off = HALO - (CONV_WIDTH - 1) + kk
                conv = conv + cw_ref[kk:kk + 1, lo:lo + fc] * ub[off:off + tm, :]
            outs.append(conv)
        a, gg = outs
        act_ref[:, c * fc:(c + 1) * fc] = (a * _sigmoid(a) * gg).astype(CDT)
    y = x + _dot(act_ref[...], wd_ref[...])
    if final:
        y = _rms(y, gf_ref[...])
    o_ref[0] = y


def _ffn(x3, g, wu, cw, cb, wd, gf, layer, final):
    b, s, d = x3.shape
    tm = min(TM_ROWS, s)
    fc = FFN_CHUNK
    assert D_FF % fc == 0
    const = lambda shape: pl.BlockSpec(shape, lambda bi, i: (0,) * len(shape), pipeline_mode=pl.Buffered(1))
    stacked = lambda shape: pl.BlockSpec((None,) + shape, lambda bi, i: (layer,) + (0,) * len(shape),
                                         pipeline_mode=pl.Buffered(1))
    return pl.pallas_call(
        functools.partial(_ffn_kernel, tm=tm, fc=fc, final=final),
        grid=(b, s // tm),
        in_specs=[
            pl.BlockSpec((1, tm, d), lambda bi, i: (bi, i, 0)),
            pl.BlockSpec((1, HALO, d), lambda bi, i: (bi, jnp.maximum(i * (tm // HALO) - 1, 0), 0)),
            const((1, d)), stacked((d, 2 * D_FF)), const((CONV_WIDTH, 2 * D_FF)), const((1, 2 * D_FF)),
            stacked((D_FF, d)), const((1, d)),
        ],
        out_specs=pl.BlockSpec((1, tm, d), lambda bi, i: (bi, i, 0)),
        out_shape=jax.ShapeDtypeStruct((b, s, d), F32),
        scratch_shapes=[pltpu.VMEM((tm + HALO, d), CDT), pltpu.VMEM((2, 2, tm + HALO, fc), F32),
                        pltpu.VMEM((tm, D_FF), CDT)],
        compiler_params=_cparams(("parallel", "arbitrary")),
        name="conv_glu_mlp",
    )(x3, x3, g.reshape(1, d), wu, cw, cb.reshape(1, -1), wd, gf.reshape(1, d))


def _w_in_plan():
    widths = (512, 512, 512, MLA_Q_LORA, MLA_KV_LORA, MLA_ROPE, 512, 512, 512, FOX_HEADS,
              512, 768, 3 * NSA_HEADS, N_BRANCH * D_MODEL)
    (a_q, a_k, a_v, b_cq, b_ckv, b_kr, c_q, c_k, c_v, c_f, d_q, d_kv, d_g, gate, _) = np.cumsum((0,) + widths).tolist()
    half = MLA_ROPE // 2

    def run(src, nblocks, scale=1.0):
        return [[(src + i * LANES, LANES, scale)] for i in range(nblocks)]

    blocks = (run(a_q, 4, LOG2E * DIFF_DH ** -0.5) + run(a_k, 4) + run(a_v, 4)
              + run(c_q, 4, LOG2E * FOX_DH ** -0.5) + run(c_k, 4) + run(c_v, 4))
    sd = LOG2E * NSA_DH ** -0.5
    blocks += [[(d_q + j * NSA_DH, NSA_DH, sd), (d_q + (NSA_HPG + j) * NSA_DH, NSA_DH, sd)] for j in range(NSA_HPG)]
    blocks += run(d_kv, 6) + run(b_cq, 2) + run(b_ckv, 2)
    blocks += [[(b_kr, MLA_ROPE, 1.0), None],
               [(b_kr + half, half, -1.0), (b_kr, half, 1.0), None]]
    blocks += run(gate, N_BRANCH * D_MODEL // LANES)
    assert len(blocks) * LANES == N_PROJ
    small = [(c_f, FOX_HEADS, 1.0), (d_g, 3 * NSA_HEADS, 1.0), None]
    return blocks, small


def _w_in_relayout_kernel(w_ref, big_ref, small_ref):
    cols = w_ref.shape[1]
    blocks, small = _w_in_plan()
    for j, pieces in enumerate(blocks):
        row = j * LANES
        for p in pieces:
            if p is None:
                big_ref[row:(j + 1) * LANES, :] = jnp.zeros(((j + 1) * LANES - row, cols), big_ref.dtype)
            else:
                src, n, scale = p
                v = w_ref[src:src + n, :]
                big_ref[row:row + n, :] = (v if scale == 1.0 else v * scale).astype(big_ref.dtype)
                row += n
    (sf, nf, _), (sg, ng, _), _ = small
    r = lax.broadcasted_iota(jnp.int32, (LANES, cols), 0)
    side = jnp.where(r < nf, w_ref[sf:sf + LANES, :],
                     jnp.where(r < nf + ng, w_ref[sg - nf:sg - nf + LANES, :], 0.0))
    small_ref[...] = side.astype(small_ref.dtype)


def _w_in_relayout(w):
    nl, d, n = w.shape
    wt = jnp.transpose(w, (2, 0, 1)).reshape(n, nl * d)
    tc = min(256, d)
    return pl.pallas_call(
        _w_in_relayout_kernel,
        grid=(nl, d // tc),
        in_specs=[pl.BlockSpec((n, tc), lambda l, i: (0, l * (d // tc) + i))],
        out_specs=[pl.BlockSpec((None, N_PROJ, tc), lambda l, i: (l, 0, i)),
                   pl.BlockSpec((None, LANES, tc), lambda l, i: (l, 0, i))],
        out_shape=[jax.ShapeDtypeStruct((nl, N_PROJ, d), CDT), jax.ShapeDtypeStruct((nl, LANES, d), CDT)],
        compiler_params=_cparams(("parallel", "parallel")),
        name="w_in_relayout",
    )(wt)


def _prep_mla(w_uq, w_ukv):
    r = w_uq.shape[0]
    hw = 2 * LANES
    half = MLA_ROPE // 2
    scale = LOG2E * (MLA_NOPE + MLA_ROPE) ** -0.5
    wq = (w_uq * scale).reshape(r, MLA_HEADS, MLA_NOPE + MLA_ROPE)
    nope, t1, t2 = wq[..., :MLA_NOPE], wq[..., MLA_NOPE:MLA_NOPE + half], wq[..., MLA_NOPE + half:]
    zpad = jnp.zeros((r, MLA_HEADS, hw - MLA_NOPE - MLA_ROPE), w_uq.dtype)
    wqm = jnp.concatenate([nope, t1, t2, zpad], axis=-1).reshape(r, MLA_HEADS * hw)
    wqs = jnp.concatenate([jnp.zeros_like(nope), -t2, t1, zpad], axis=-1).reshape(r, MLA_HEADS * hw)
    wkv = w_ukv.reshape(w_ukv.shape[0], MLA_HEADS, MLA_NOPE + MLA_VDIM)
    wk = wkv[..., :MLA_NOPE].reshape(-1, MLA_HEADS * MLA_NOPE)
    wv = wkv[..., MLA_NOPE:].reshape(-1, MLA_HEADS * MLA_VDIM)
    return wqm.astype(CDT), wqs.astype(CDT), wk.astype(CDT), wv.astype(CDT)


def _rope_tables(s):
    half = MLA_ROPE // 2
    inv_freq = ROPE_THETA ** (-jnp.arange(0, MLA_ROPE, 2, dtype=F32) / MLA_ROPE)
    ang = jnp.arange(s, dtype=F32)[:, None] * inv_freq[None, :]
    cos, sin = jnp.cos(ang), jnp.sin(ang)
    z = jnp.zeros((s, LANES - MLA_ROPE), F32)
    cosk = jnp.concatenate([cos, cos, z], axis=1)
    sink = jnp.concatenate([sin, sin, z], axis=1)
    cosq = jnp.concatenate([jnp.ones((s, MLA_NOPE), F32), cosk], axis=1)
    sinq = jnp.concatenate([jnp.zeros((s, MLA_NOPE), F32), sink], axis=1)
    return cosq, sinq, cosk, sink


def _prep_compress(pe, w1, w2):
    eye2 = jnp.eye(2, dtype=F32)
    w1r = w1.reshape(2, CMP_LEN, NSA_DH, CMP_HIDDEN).astype(CDT)
    same = np.eye(2, dtype=bool)
    diag_kg = jnp.asarray(same[:, None, :, None] & same[None, :, None, :])

    def expand(wpart):
        src = wpart.transpose(1, 0, 2, 3)[:, :, None, :, None, None, :]
        t = jnp.where(diag_kg[None, :, :, None, :, :, None], src, jnp.zeros((), CDT))
        return t.reshape(CMP_STRIDE * 4 * NSA_DH, 4 * CMP_HIDDEN)

    w1a, w1b = expand(w1r[:, :CMP_STRIDE]), expand(w1r[:, CMP_STRIDE:])

    def pe_row(p):
        t = jnp.broadcast_to(p.transpose(1, 0, 2)[:, :, None, :], (CMP_STRIDE, 2, NSA_GROUPS, NSA_DH))
        return jnp.pad(t.reshape(1, -1), ((0, 7), (0, 0)))

    pea, peb = pe_row(pe[:, :CMP_STRIDE]), pe_row(pe[:, CMP_STRIDE:])
    w2b = jnp.einsum('khd,kK,gG,u->kghKGud', w2, eye2, eye2, jnp.ones((2,), F32))
    w2b = w2b.reshape(4 * CMP_HIDDEN, 4 * 2 * NSA_DH)
    return w1a.astype(CDT), w1b.astype(CDT), pea.astype(CDT), peb.astype(CDT), w2b.astype(CDT)


def _gate_expand():
    e = np.zeros((NSA_GROUPS, LANES, 3, NSA_HPG, NSA_DH), np.float32)
    for g in range(NSA_GROUPS):
        for j in range(NSA_HPG):
            for br in range(3):
                e[g, SMALL_G + (g * NSA_HPG + j) * 3 + br, br, j, :] = 1.0
    return jnp.asarray(e.reshape(NSA_GROUPS, LANES, 3 * NSA_HPG * NSA_DH)).astype(CDT)


def _token_mixers(x3, l, norm_mix, w_in, diff_lambda, diff_subln, mla_norm_q, mla_w_uq, mla_norm_kv, mla_w_ukv,
                  fox_b_f, nsa_cmp_pe, nsa_cmp_w1, nsa_cmp_w2, w_branch, w_out, rope_tabs):
    b, s, d = x3.shape
    t = b * s
    x2 = x3.reshape(t, d)
    proj, small, small_t = _in_proj(x2, norm_mix, *w_in, l)
    proj3 = proj.reshape(b, s, N_PROJ)
    small3 = small.reshape(b, s, LANES)

    lam_init = 0.8 - 0.6 * math.exp(-0.3 * l)
    y_a = _diff_attention(proj3, diff_lambda, diff_subln, lam_init)

    wqm, wqs, wk, wv = _prep_mla(mla_w_uq, mla_w_ukv)
    qc, kc, vv = _mla_prep(proj3, mla_norm_q, mla_norm_kv, wqm, wqs, wk, wv, rope_tabs)
    y_b = _mla_attention(qc, kc, vv)

    cf_rows = (small_t[SMALL_F:SMALL_F + FOX_HEADS].reshape(FOX_HEADS, b, s).transpose(1, 0, 2)
               .reshape(b * FOX_HEADS, s))
    bias_rows = jnp.tile(fox_b_f.astype(F32), b).reshape(b * FOX_HEADS, 1)
    c4 = _fox_cumsum(cf_rows, bias_rows)
    y_c = _fox_attention(proj3, c4)

    w1a, w1b, pea, peb, w2b = _prep_compress(nsa_cmp_pe, nsa_cmp_w1, nsa_cmp_w2)
    xc = proj3[:, :, PB_CMP_K * LANES:(PB_CMP_V + 1) * LANES].reshape(b, s // CMP_STRIDE, CMP_STRIDE * 2 * LANES)
    kvc = _nsa_compress(xc, w1a, w1b, pea, peb, w2b)
    n_topk = min(SLC_TOPK, s // SLC_LEN)
    o_c, sbias, used = _nsa_cmp_select(proj3, kvc, n_topk)
    o_w = _nsa_window(proj3)
    y_d = _nsa_selected(proj3, sbias, used, o_c, o_w, small3, _gate_expand())

    ys = [y.reshape(t, BRANCH_WIDTH) for y in (y_a, y_b, y_c, y_d)]
    return _merge(ys, proj, w_branch, w_out, x2, l).reshape(b, s, d)


def kernel(x, norm_mix, w_in, diff_lambda, diff_subln, mla_norm_q, mla_w_uq, mla_norm_kv, mla_w_ukv, fox_b_f,
           nsa_cmp_pe, nsa_cmp_w1, nsa_cmp_w2, w_branch, w_out, norm_ffn, w_up, conv_w, conv_b, w_down, norm_final):
    depth = w_in.shape[0]
    s = x.shape[1]
    rope_tabs = _rope_tables(s)
    w_in = _w_in_relayout(w_in)
    w_up16, w_down16 = w_up.astype(CDT), w_down.astype(CDT)
    w_branch16, w_out16 = w_branch.astype(CDT), w_out.astype(CDT)
    for l in range(depth):
        x = _token_mixers(x, l, norm_mix[l], w_in, diff_lambda[l], diff_subln[l], mla_norm_q[l], mla_w_uq[l],
                          mla_norm_kv[l], mla_w_ukv[l], fox_b_f[l], nsa_cmp_pe[l], nsa_cmp_w1[l], nsa_cmp_w2[l],
                          w_branch16, w_out16, rope_tabs)
        x = _ffn(x, norm_ffn[l], w_up16, conv_w[l], conv_b[l], w_down16, norm_final, l, final=(l == depth - 1))
    return x
```
